```python
import math
import jax, jax.numpy as jnp
from jax import lax
import numpy as np

D_MODEL = 1024
BATCH = 2
SEQ = 16384
DEPTH = 1

M_HEADS = 4
M_QK_DIM = 64
M_V_DIM = 128
M_CHUNK = 64
GATE_SOFTCAP = 15.0
A_HEADS = 4
A_HEAD_DIM = 64
DILATED_PATTERNS = ((128, 1), (512, 4), (2048, 16))
N_GROUPS = len(DILATED_PATTERNS)
Q_BLOCK = 128
N_EXPERTS = 32
TOP_K = 4
D_FF = 1024
SWIGLU_LIMIT = 7.0
SWIGLU_ALPHA = 1.702
MOE_BLOCK = 512
EPS = 1e-6

M_WIDTH = M_HEADS * M_V_DIM
A_WIDTH = A_HEADS * A_HEAD_DIM
IN_SPLITS = (M_HEADS * M_QK_DIM, M_HEADS * M_QK_DIM, M_WIDTH, M_WIDTH, 2 * M_HEADS,
             N_GROUPS * A_WIDTH, N_GROUPS * A_WIDTH, N_GROUPS * A_WIDTH,
             D_MODEL, D_MODEL)
D_IN = sum(IN_SPLITS)

kernel_name = "hybrid_mlstm_dilated_attn_moe_block"


def _rms(t):
    tf = t.astype(jnp.float32)
    return tf * lax.rsqrt(jnp.mean(tf * tf, axis=-1, keepdims=True) + EPS)


def rms_norm(x, g):
    return (_rms(x) * g.astype(jnp.float32)).astype(x.dtype)


def soft_cap(t, cap):
    return cap * jnp.tanh(t / cap)


def alibi_slopes():
    n = N_GROUPS * A_HEADS
    s = np.exp2(-8.0 * np.arange(1, n + 1) / n).astype(np.float32)
    return jnp.asarray(s, dtype=jnp.float32).reshape(N_GROUPS, A_HEADS)


def mlstm_chunkwise(q, k, v, i_pre, f_pre):
    B, S, H, Dk = q.shape
    Dv = v.shape[-1]
    L = M_CHUNK
    nc = S // L

    def to_chunks(t):
        return t.reshape(B, nc, L, H, t.shape[-1]).transpose(0, 3, 1, 2, 4)

    qc = to_chunks(q)
    kc = to_chunks(k) * (Dk ** -0.5)
    vc = to_chunks(v)
    ic = i_pre.reshape(B, nc, L, H).transpose(0, 3, 1, 2)
    logf = jax.nn.log_sigmoid(f_pre).reshape(B, nc, L, H).transpose(0, 3, 1, 2)
    b = jnp.cumsum(logf, axis=-1)
    g = b[..., -1]

    a = g[..., None] - b + ic
    a_max = jnp.max(a, axis=-1)
    wa = jnp.exp(a - a_max[..., None])
    C_loc = jnp.einsum('bhcs,bhcsk,bhcsv->bhckv', wa, kc, vc)
    n_loc = jnp.einsum('bhcs,bhcsk->bhck', wa, kc)

    def step(carry, xs):
        C, n, m = carry
        g_c, am_c, Cl, nl = xs
        m_new = jnp.maximum(g_c + m, am_c)
        s_old = jnp.exp(g_c + m - m_new)
        s_loc = jnp.exp(am_c - m_new)
        C_new = s_old[..., None, None] * C + s_loc[..., None, None] * Cl
        n_new = s_old[..., None] * n + s_loc[..., None] * nl
        return (C_new, n_new, m_new), (C, n, m)

    init = (jnp.zeros((B, H, Dk, Dv), jnp.float32), jnp.zeros((B, H, Dk), jnp.float32),
            jnp.zeros((B, H), jnp.float32))
    xs = (g.transpose(2, 0, 1), a_max.transpose(2, 0, 1),
          C_loc.transpose(2, 0, 1, 3, 4), n_loc.transpose(2, 0, 1, 3))
    _, (C_prev, n_prev, m_prev) = lax.scan(step, init, xs)
    C_prev = C_prev.transpose(1, 2, 0, 3, 4)
    n_prev = n_prev.transpose(1, 2, 0, 3)
    m_prev = m_prev.transpose(1, 2, 0)

    causal = jnp.tril(jnp.ones((L, L), dtype=bool))
    Dlog = b[..., :, None] - b[..., None, :] + ic[..., None, :]
    Dlog = jnp.where(causal, Dlog, -jnp.inf)
    inter = b + m_prev[..., None]
    m_t = jnp.maximum(inter, jnp.max(Dlog, axis=-1))
    Sqk = jnp.einsum('bhctk,bhcsk->bhcts', qc, kc) * jnp.exp(Dlog - m_t[..., None])
    sc = jnp.exp(inter - m_t)
    num = jnp.einsum('bhcts,bhcsv->bhctv', Sqk, vc) + sc[..., None] * jnp.einsum('bhctk,bhckv->bhctv', qc, C_prev)
    den = jnp.sum(Sqk, axis=-1) + sc * jnp.einsum('bhctk,bhck->bhct', qc, n_prev)
    h = num / jnp.maximum(jnp.abs(den), jnp.exp(-m_t))[..., None]
    return h.transpose(0, 2, 3, 1, 4).reshape(B, S, H, Dv)


def dilated_window_attention(q, k, v, slopes, dilation, n_back):
    B, S, H, E = q.shape
    d = dilation
    L = S // d
    nb = -(-L // Q_BLOCK)
    Lp = nb * Q_BLOCK
    n_prev = -(-n_back // Q_BLOCK)
    KW = (n_prev + 1) * Q_BLOCK

    def to_sub(t):
        t = t.reshape(B, L, d, H, E).transpose(0, 2, 3, 1, 4)
        return jnp.pad(t, ((0, 0), (0, 0), (0, 0), (0, Lp - L), (0, 0)))

    def band(t):
        tp = jnp.pad(t, ((0, 0), (0, 0), (0, 0), (n_prev * Q_BLOCK, 0), (0, 0)))
        tp = tp.reshape(B, d, H, nb + n_prev, Q_BLOCK, E)
        return jnp.concatenate([tp[:, :, :, j:j + nb] for j in range(n_prev + 1)], axis=-2)

    qb = to_sub(q).reshape(B, d, H, nb, Q_BLOCK, E)
    kb = band(to_sub(k))
    vb = band(to_sub(v))
    s = jnp.einsum('bdhnqe,bdhnke->bdhnqk', qb, kb) * (E ** -0.5)

    qi = jnp.arange(Q_BLOCK)[:, None]
    kj = jnp.arange(KW)[None, :]
    dist = qi + n_prev * Q_BLOCK - kj
    key_pos = jnp.arange(nb)[:, None, None] * Q_BLOCK - n_prev * Q_BLOCK + kj[None]
    valid = (dist >= 0) & (dist <= n_back) & (key_pos >= 0)
    bias = -slopes[:, None, None] * (dist * d).astype(jnp.float32)[None]
    s = s + bias[None, None, :, None]
    s = jnp.where(valid[None, None, None], s, -jnp.inf)
    m = jnp.max(s, axis=-1, keepdims=True)
    p = jnp.exp(s - m)
    den = jnp.sum(p, axis=-1)
    o = jnp.einsum('bdhnqk,bdhnke->bdhnqe', p, vb) / den[..., None]
    lse = m[..., 0] + jnp.log(den)

    o = o.reshape(B, d, H, Lp, E)[:, :, :, :L].transpose(0, 3, 1, 2, 4).reshape(B, S, H, E)
    lse = lse.reshape(B, d, H, Lp)[..., :L].transpose(0, 3, 1, 2).reshape(B, S, H)
    return o, lse


def moe_ffn(h, w_router, b_router, w1, b1, w2, b2):
    B, S, D = h.shape
    N = B * S
    A = N * TOP_K
    xt = h.reshape(N, D)
    logits = (xt @ w_router + b_router).astype(jnp.float32)
    top_vals, top_idx = lax.top_k(logits, TOP_K)
    gates = jax.nn.softmax(top_vals, axis=-1)

    flat_e = top_idx.reshape(A).astype(jnp.int32)
    flat_tok = jnp.arange(A, dtype=jnp.int32) // TOP_K
    flat_g = gates.reshape(A)
    order = jnp.argsort(flat_e)
    sorted_e = flat_e[order]
    sorted_tok = flat_tok[order]
    counts = jnp.bincount(flat_e, length=N_EXPERTS)
    starts = jnp.cumsum(counts) - counts
    padded = ((counts + MOE_BLOCK - 1) // MOE_BLOCK) * MOE_BLOCK
    pends = jnp.cumsum(padded)
    pstarts = pends - padded
    dest = (pstarts[sorted_e] + jnp.arange(A) - starts[sorted_e]).astype(jnp.int32)
    NB = -(-A // MOE_BLOCK) + N_EXPERTS
    P = NB * MOE_BLOCK
    row_tok = jnp.full((P,), N, dtype=jnp.int32).at[dest].set(sorted_tok)
    block_e = jnp.minimum(jnp.searchsorted(pends, jnp.arange(NB) * MOE_BLOCK, side='right'),
                          N_EXPERTS - 1).astype(jnp.int32)
    x_pad = jnp.concatenate([xt, jnp.zeros((1, D), xt.dtype)], axis=0)

    def expert_block(args):
        rows, e = args
        xb = x_pad[rows]
        gu = xb @ w1[e] + b1[e]
        gate = jnp.minimum(gu[:, :D_FF], SWIGLU_LIMIT)
        lin = jnp.clip(gu[:, D_FF:], -SWIGLU_LIMIT, SWIGLU_LIMIT)
        glu = gate * jax.nn.sigmoid(SWIGLU_ALPHA * gate)
        return ((lin + 1.0) * glu) @ w2[e] + b2[e]

    ys = lax.map(expert_block, (row_tok.reshape(NB, MOE_BLOCK), block_e)).reshape(P, D)
    y_assign = ys[dest] * flat_g[order][:, None].astype(ys.dtype)
    y = jax.ops.segment_sum(y_assign, sorted_tok, num_segments=N)
    return y.reshape(B, S, D).astype(h.dtype)


def hybrid_block(x, norm1_g, w_in, mlstm_gate_b, mlstm_norm_g, attn_q_norm_g, attn_k_norm_g,
                 w_mlstm_branch, w_attn_branch, w_out, norm2_g, w_router, b_router, w1, b1, w2, b2):
    B, S, D = x.shape
    h = rms_norm(x, norm1_g)
    z = h @ w_in
    cuts = list(np.cumsum(IN_SPLITS)[:-1])
    mq, mk, mv, mo, mif, aq, ak, av, gm, ga = jnp.split(z, cuts, axis=-1)

    f32 = jnp.float32
    mif = mif.astype(f32) + mlstm_gate_b.astype(f32)
    i_pre = soft_cap(mif[..., :M_HEADS], GATE_SOFTCAP)
    f_pre = soft_cap(mif[..., M_HEADS:], GATE_SOFTCAP)
    h_m = mlstm_chunkwise(mq.astype(f32).reshape(B, S, M_HEADS, M_QK_DIM),
                          mk.astype(f32).reshape(B, S, M_HEADS, M_QK_DIM),
                          mv.astype(f32).reshape(B, S, M_HEADS, M_V_DIM), i_pre, f_pre)
    h_m = _rms(h_m) * mlstm_norm_g.astype(f32).reshape(M_HEADS, M_V_DIM)
    h_m = (jax.nn.sigmoid(mo.astype(f32)).reshape(B, S, M_HEADS, M_V_DIM) * h_m)
    h_m = h_m.reshape(B, S, M_WIDTH).astype(x.dtype)

    qa = _rms(aq.reshape(B, S, N_GROUPS, A_HEADS, A_HEAD_DIM)) * attn_q_norm_g.astype(f32)[:, None, :]
    ka = _rms(ak.reshape(B, S, N_GROUPS, A_HEADS, A_HEAD_DIM)) * attn_k_norm_g.astype(f32)[:, None, :]
    va = av.astype(f32).reshape(B, S, N_GROUPS, A_HEADS, A_HEAD_DIM)
    slopes = alibi_slopes()
    outs, lses = [], []
    for gi, (win, dil) in enumerate(DILATED_PATTERNS):
        o, l = dilated_window_attention(qa[:, :, gi], ka[:, :, gi], va[:, :, gi], slopes[gi], dil, win // dil)
        outs.append(o)
        lses.append(l)
    wg = jax.nn.softmax(jnp.stack(lses, axis=0), axis=0)
    h_a = jnp.sum(wg[..., None] * jnp.stack(outs, axis=0), axis=0)
    h_a = h_a.reshape(B, S, A_WIDTH).astype(x.dtype)

    y = jax.nn.sigmoid(gm) * (h_m @ w_mlstm_branch) + jax.nn.sigmoid(ga) * (h_a @ w_attn_branch)
    x = x + y @ w_out

    x = x + moe_ffn(rms_norm(x, norm2_g), w_router, b_router, w1, b1, w2, b2)
    return x


def setup_inputs(seed: int = 0) -> dict:
    key = jax.random.key(seed)
    ks = jax.random.split(key, 20)

    def nrm(k, shape, scale):
        return jax.random.normal(k, shape, jnp.float32) * scale

    x = nrm(ks[0], (BATCH, SEQ, D_MODEL), 1.0)
    norm1_g = 1.0 + nrm(ks[1], (DEPTH, D_MODEL), 0.05)
    w_in = nrm(ks[2], (DEPTH, D_MODEL, D_IN), D_MODEL ** -0.5)
    i_b = nrm(ks[3], (DEPTH, M_HEADS), 0.1) - 1.0
    f_b = jnp.linspace(3.0, 6.0, M_HEADS, dtype=jnp.float32)[None, :] + nrm(ks[4], (DEPTH, M_HEADS), 0.1)
    mlstm_gate_b = jnp.concatenate([i_b, f_b], axis=-1)
    mlstm_norm_g = 1.0 + nrm(ks[5], (DEPTH, M_WIDTH), 0.05)
    attn_q_norm_g = 1.0 + nrm(ks[6], (DEPTH, N_GROUPS, A_HEAD_DIM), 0.05)
    attn_k_norm_g = 1.0 + nrm(ks[7], (DEPTH, N_GROUPS, A_HEAD_DIM), 0.05)
    w_mlstm_branch = nrm(ks[8], (DEPTH, M_WIDTH, D_MODEL), M_WIDTH ** -0.5)
    w_attn_branch = nrm(ks[9], (DEPTH, A_WIDTH, D_MODEL), A_WIDTH ** -0.5)
    w_out = nrm(ks[10], (DEPTH, D_MODEL, D_MODEL), D_MODEL ** -0.5)
    norm2_g = 1.0 + nrm(ks[11], (DEPTH, D_MODEL), 0.05)
    w_router = nrm(ks[12], (DEPTH, D_MODEL, N_EXPERTS), D_MODEL ** -0.5)
    b_router = nrm(ks[13], (DEPTH, N_EXPERTS), 0.01)
    w1 = nrm(ks[14], (DEPTH, N_EXPERTS, D_MODEL, 2 * D_FF), D_MODEL ** -0.5)
    b1 = nrm(ks[15], (DEPTH, N_EXPERTS, 2 * D_FF), 0.01)
    w2 = nrm(ks[16], (DEPTH, N_EXPERTS, D_FF, D_MODEL), D_FF ** -0.5)
    b2 = nrm(ks[17], (DEPTH, N_EXPERTS, D_MODEL), 0.01)
    return {"x": x, "norm1_g": norm1_g, "w_in": w_in, "mlstm_gate_b": mlstm_gate_b,
            "mlstm_norm_g": mlstm_norm_g, "attn_q_norm_g": attn_q_norm_g, "attn_k_norm_g": attn_k_norm_g,
            "w_mlstm_branch": w_mlstm_branch, "w_attn_branch": w_attn_branch, "w_out": w_out,
            "norm2_g": norm2_g, "w_router": w_router, "b_router": b_router,
            "w1": w1, "b1": b1, "w2": w2, "b2": b2}


def reference(x, norm1_g, w_in, mlstm_gate_b, mlstm_norm_g, attn_q_norm_g, attn_k_norm_g,
              w_mlstm_branch, w_attn_branch, w_out, norm2_g, w_router, b_router, w1, b1, w2, b2):
    for l in range(DEPTH):
        x = hybrid_block(x, norm1_g[l], w_in[l], mlstm_gate_b[l], mlstm_norm_g[l], attn_q_norm_g[l],
                         attn_k_norm_g[l], w_mlstm_branch[l], w_attn_branch[l], w_out[l], norm2_g[l],
                         w_router[l], b_router[l], w1[l], b1[l], w2[l], b2[l])
    return x
```

```python
import functools

import numpy as np
import jax
import jax.numpy as jnp
from jax import lax
from jax.experimental import pallas as pl
from jax.experimental.pallas import tpu as pltpu

F32 = jnp.float32
BF16 = jnp.bfloat16
I32 = jnp.int32

D_MODEL = 1024
M_HEADS = 4
M_QK_DIM = 64
M_V_DIM = 128
GATE_SOFTCAP = 15.0
A_HEADS = 4
A_HEAD_DIM = 64
DILATED_PATTERNS = ((128, 1), (512, 4), (2048, 16))
N_GROUPS = len(DILATED_PATTERNS)
N_BACK = 128
N_EXPERTS = 32
TOP_K = 4
D_FF = 1024
SWIGLU_LIMIT = 7.0
SWIGLU_ALPHA = 1.702
MOE_BLOCK = 512
EPS = 1e-6

M_WIDTH = M_HEADS * M_V_DIM
M_QK_WIDTH = M_HEADS * M_QK_DIM
A_WIDTH = A_HEADS * A_HEAD_DIM
IN_SPLITS = (M_QK_WIDTH, M_QK_WIDTH, M_WIDTH, M_WIDTH, 2 * M_HEADS,
             N_GROUPS * A_WIDTH, N_GROUPS * A_WIDTH, N_GROUPS * A_WIDTH, D_MODEL, D_MODEL)

LANES = 128
VMEM_LIMIT = 56 * 1024 * 1024

_NT = (((1,), (1,)), ((), ()))


def _alibi_slopes():
    n = N_GROUPS * A_HEADS
    s = np.exp2(-8.0 * np.arange(1, n + 1) / n).astype(np.float32)
    return s.reshape(N_GROUPS, A_HEADS)


def _params(*sem):
    return pltpu.CompilerParams(dimension_semantics=sem, vmem_limit_bytes=VMEM_LIMIT)


def _log_sigmoid(x):
    return jnp.minimum(x, 0.0) - jnp.log1p(jnp.exp(-jnp.abs(x)))


_C_MQ = (0, 256)
_C_MV = (256, 768)
_C_MO = (768, 1280)
_C_AQ = (1280, 2048)
_C_AK = (2048, 2816)
_C_AV = (2816, 3584)
_C_GM = (3584, 4608)
_C_GA = (4608, 5632)
_W_MAIN = 5632
_WT_ROWS = M_QK_WIDTH + 16


def _inproj_body(x_ref, g1_ref, wm_ref, wt_ref, gb_ref,
                 mq_ref, kT_ref, mv_ref, so_ref, gi_ref, gf_ref,
                 aq_ref, ak_ref, av_ref, sgm_ref, sga_ref):
    x = x_ref[...]
    h = x * lax.rsqrt(jnp.mean(x * x, axis=-1, keepdims=True) + EPS) * g1_ref[...]
    hb = h.astype(BF16)

    def seg(c):
        return jnp.dot(hb, wm_ref[:, c[0]:c[1]], preferred_element_type=F32)

    mq_ref[...] = seg(_C_MQ).astype(BF16)
    mv_ref[...] = seg(_C_MV).astype(BF16)
    so_ref[...] = jax.nn.sigmoid(seg(_C_MO)).astype(BF16)
    aq_ref[...] = seg(_C_AQ).astype(BF16)
    ak_ref[...] = seg(_C_AK).astype(BF16)
    av_ref[...] = seg(_C_AV).astype(BF16)
    sgm_ref[...] = jax.nn.sigmoid(seg(_C_GM)).astype(BF16)
    sga_ref[...] = jax.nn.sigmoid(seg(_C_GA)).astype(BF16)

    t = lax.dot_general(wt_ref[...], hb, _NT, preferred_element_type=F32)
    kT_ref[...] = t[0:M_QK_WIDTH].astype(BF16)
    zi = t[M_QK_WIDTH:M_QK_WIDTH + 8] + gb_ref[0:8]
    zf = t[M_QK_WIDTH + 8:M_QK_WIDTH + 16] + gb_ref[8:16]
    gi_ref[...] = GATE_SOFTCAP * jnp.tanh(zi / GATE_SOFTCAP)
    gf_ref[...] = _log_sigmoid(GATE_SOFTCAP * jnp.tanh(zf / GATE_SOFTCAP))


def _stage_inproj(x2d, norm1_g, w_in, gate_b, tm):
    n = x2d.shape[0]
    cuts = np.concatenate([[0], np.cumsum(IN_SPLITS)])
    col = lambda i: w_in[:, cuts[i]:cuts[i + 1]]
    wm = jnp.concatenate([col(0), col(2), col(3), col(5), col(6), col(7), col(8), col(9)],
                         axis=1).astype(BF16)
    wif = col(4)
    z4 = jnp.zeros((4, D_MODEL), w_in.dtype)
    wt = jnp.concatenate([col(1).T, wif[:, :M_HEADS].T, z4, wif[:, M_HEADS:].T, z4], axis=0).astype(BF16)
    gb = jnp.zeros((16, 1), F32)
    gb = gb.at[0:4, 0].set(gate_b[:M_HEADS].astype(F32)).at[8:12, 0].set(gate_b[M_HEADS:].astype(F32))
    g1 = norm1_g.astype(F32).reshape(1, D_MODEL)

    row = lambda w: pl.BlockSpec((tm, w), lambda i: (i, 0))
    rowT = lambda r: pl.BlockSpec((r, tm), lambda i: (0, i))
    full = lambda a: pl.BlockSpec(a.shape, lambda i: (0,) * a.ndim)
    out_shapes = (
        jax.ShapeDtypeStruct((n, M_QK_WIDTH), BF16),
        jax.ShapeDtypeStruct((M_QK_WIDTH, n), BF16),
        jax.ShapeDtypeStruct((n, M_WIDTH), BF16),
        jax.ShapeDtypeStruct((n, M_WIDTH), BF16),
        jax.ShapeDtypeStruct((8, n), F32),
        jax.ShapeDtypeStruct((8, n), F32),
        jax.ShapeDtypeStruct((n, N_GROUPS * A_WIDTH), BF16),
        jax.ShapeDtypeStruct((n, N_GROUPS * A_WIDTH), BF16),
        jax.ShapeDtypeStruct((n, N_GROUPS * A_WIDTH), BF16),
        jax.ShapeDtypeStruct((n, D_MODEL), BF16),
        jax.ShapeDtypeStruct((n, D_MODEL), BF16),
    )
    out_specs = (row(M_QK_WIDTH), rowT(M_QK_WIDTH), row(M_WIDTH), row(M_WIDTH), rowT(8), rowT(8),
                 row(N_GROUPS * A_WIDTH), row(N_GROUPS * A_WIDTH), row(N_GROUPS * A_WIDTH),
                 row(D_MODEL), row(D_MODEL))
    return pl.pallas_call(
        _inproj_body,
        grid=(n // tm,),
        in_specs=[row(D_MODEL), full(g1), full(wm), full(wt), full(gb)],
        out_specs=out_specs,
        out_shape=out_shapes,
        compiler_params=_params("parallel"),
        name="inproj",
    )(x2d, g1, wm, wt, gb)


M_CHUNK_LEN = 128


def _mlstm_body(q_ref, kT_ref, v_ref, so_ref, gi_ref, gf_ref, ng_ref, o_ref, c_ref, m_ref, *, nchunk):
    L = M_CHUNK_LEN

    @pl.when(pl.program_id(1) == 0)
    def _():
        c_ref[...] = jnp.zeros_like(c_ref)
        m_ref[...] = jnp.zeros_like(m_ref)

    lane8 = lax.broadcasted_iota(I32, (8, L), 1)
    causal = lax.broadcasted_iota(I32, (L, L), 1) <= lax.broadcasted_iota(I32, (L, L), 0)
    lo_half = lax.broadcasted_iota(I32, (L, LANES), 1) < M_QK_DIM
    ones = jnp.ones((L, M_V_DIM), BF16)

    for c in range(nchunk):
        rows = slice(c * L, (c + 1) * L)
        gi = gi_ref[:, rows]
        b = gf_ref[:, rows]
        sh = 1
        while sh < L:
            b = b + jnp.where(lane8 >= sh, pltpu.roll(b, sh, 1), 0.0)
            sh *= 2
        u = gi - b
        g = b[:, L - 1:L]
        a = g + u
        amax = jnp.max(a, axis=1, keepdims=True)
        m_prev = m_ref[:, 0:1]
        m_new = jnp.maximum(g + m_prev, amax)
        w = jnp.exp(a - m_new) * (M_QK_DIM ** -0.5)
        s_old = jnp.exp(g + m_prev - m_new)

        for p in range(M_HEADS // 2):
            lanes_p = slice(p * LANES, (p + 1) * LANES)
            q_pair = q_ref[rows, lanes_p]
            kT_pair = kT_ref[lanes_p, rows]
            c_pair = c_ref[lanes_p, :].astype(BF16)
            for hh in range(2):
                h = 2 * p + hh
                hl = slice(h * M_V_DIM, (h + 1) * M_V_DIM)
                qm = jnp.where(lo_half if hh == 0 else jnp.logical_not(lo_half), q_pair, jnp.zeros_like(q_pair))
                bcol = jnp.transpose(jnp.broadcast_to(b[h:h + 1, :], (L, L)))
                dm = jnp.where(causal, bcol + u[h:h + 1, :], -jnp.inf)
                inter = bcol + m_prev[h:h + 1, :]
                m_t = jnp.maximum(inter, jnp.max(dm, axis=1, keepdims=True))
                s = jnp.dot(qm, kT_pair, preferred_element_type=F32) * (M_QK_DIM ** -0.5)
                pmat = (s * jnp.exp(dm - m_t)).astype(BF16)
                vext = jnp.concatenate([v_ref[rows, hl], ones], axis=1)
                sc = jnp.exp(inter - m_t)
                out = (jnp.dot(pmat, vext, preferred_element_type=F32)
                       + jnp.concatenate([sc, sc], axis=1) * jnp.dot(qm, c_pair, preferred_element_type=F32))
                hv = out[:, :M_V_DIM] / jnp.maximum(jnp.abs(out[:, M_V_DIM:]), jnp.exp(-m_t))
                hn = hv * lax.rsqrt(jnp.mean(hv * hv, axis=1, keepdims=True) + EPS)
                hn = hn * ng_ref[:, hl] * so_ref[rows, hl].astype(F32)
                o_ref[rows, hl] = hn.astype(BF16)
                hr = slice(h * M_QK_DIM, (h + 1) * M_QK_DIM)
                kw = (kT_ref[hr, rows].astype(F32) * w[h:h + 1, :]).astype(BF16)
                c_ref[hr, :] = s_old[h:h + 1, :] * c_ref[hr, :] + jnp.dot(kw, vext, preferred_element_type=F32)
        m_ref[...] = jnp.broadcast_to(m_new, m_ref.shape)


def _stage_mlstm(mq, kT, mv, so, gi, gf, norm_g, batch, seq, rows_per_step):
    n = batch * seq
    R = rows_per_step
    steps = seq // R
    ng = norm_g.astype(F32).reshape(1, M_WIDTH)
    row = lambda w: pl.BlockSpec((R, w), lambda b, i: (b * steps + i, 0))
    rowT = lambda r: pl.BlockSpec((r, R), lambda b, i: (0, b * steps + i))
    return pl.pallas_call(
        functools.partial(_mlstm_body, nchunk=R // M_CHUNK_LEN),
        grid=(batch, steps),
        in_specs=[row(M_QK_WIDTH), rowT(M_QK_WIDTH), row(M_WIDTH), row(M_WIDTH), rowT(8), rowT(8),
                  pl.BlockSpec((1, M_WIDTH), lambda b, i: (0, 0))],
        out_specs=row(M_WIDTH),
        out_shape=jax.ShapeDtypeStruct((n, M_WIDTH), BF16),
        scratch_shapes=[pltpu.VMEM((M_QK_WIDTH, 2 * M_V_DIM), F32), pltpu.VMEM((8, LANES), F32)],
        compiler_params=_params("parallel", "arbitrary"),
        name="mlstm",
    )(mq, kT, mv, so, gi, gf, ng)


def _attn_body(q_ref, kp_ref, kc_ref, vp_ref, vc_ref, gq_ref, gk_ref, o_ref, lse_ref, *, dil, slopes, lq):
    QB = N_BACK
    first = pl.program_id(2) == 0
    hid_r = lax.broadcasted_iota(I32, (A_WIDTH, A_WIDTH), 0) // A_HEAD_DIM
    hid_c = lax.broadcasted_iota(I32, (A_WIDTH, A_WIDTH), 1) // A_HEAD_DIM
    head_ones = (hid_r == hid_c).astype(BF16)

    def head_norm(xb, gain):
        xf = xb.astype(F32)
        ss = jnp.dot((xf * xf).astype(BF16), head_ones, preferred_element_type=F32)
        return xf * lax.rsqrt(ss * (1.0 / A_HEAD_DIM) + EPS) * gain

    qn = (head_norm(q_ref[0], gq_ref[...]) * (A_HEAD_DIM ** -0.5)).astype(BF16)
    kcn = head_norm(kc_ref[0], gk_ref[...]).astype(BF16)
    kpn = head_norm(kp_ref[0], gk_ref[...]).astype(BF16)
    vc = vc_ref[0]
    vp = vp_ref[0]

    qi = lax.broadcasted_iota(I32, (QB, 2 * QB), 0)
    kj = lax.broadcasted_iota(I32, (QB, 2 * QB), 1)
    dist = qi + QB - kj
    band = jnp.logical_and(dist >= 0, dist <= N_BACK)
    band_first = jnp.logical_and(band, jnp.logical_or(kj >= QB, jnp.logical_not(first)))
    distf = (dist * dil).astype(F32)
    lo_half = lax.broadcasted_iota(I32, (QB, LANES), 1) < A_HEAD_DIM
    ones = jnp.ones((2 * QB, LANES), BF16)

    for j in range(lq // QB):
        rows = slice(j * QB, (j + 1) * QB)
        prow = slice((j - 1) * QB, j * QB)
        keys = jnp.concatenate([kpn if j == 0 else kcn[prow], kcn[rows]], axis=0)
        vals = jnp.concatenate([vp if j == 0 else vc[prow], vc[rows]], axis=0)
        mask = band_first if j == 0 else band
        for p in range(A_HEADS // 2):
            lanes_p = slice(p * LANES, (p + 1) * LANES)
            q_pair = qn[rows, lanes_p]
            k_pair = keys[:, lanes_p]
            vext = jnp.concatenate([vals[:, lanes_p], ones], axis=1)
            o_pair = None
            l_pair = None
            for hh in range(2):
                h = 2 * p + hh
                sel = lo_half if hh == 0 else jnp.logical_not(lo_half)
                qm = jnp.where(sel, q_pair, jnp.zeros_like(q_pair))
                s = lax.dot_general(qm, k_pair, _NT, preferred_element_type=F32)
                s = jnp.where(mask, s - float(slopes[h]) * distf, -jnp.inf)
                m = jnp.max(s, axis=1, keepdims=True)
                pv = jnp.dot(jnp.exp(s - m).astype(BF16), vext, preferred_element_type=F32)
                den = pv[:, LANES:]
                o_h = pv[:, :LANES] / den
                l_h = m + jnp.log(den)
                o_pair = o_h if hh == 0 else jnp.where(lo_half, o_pair, o_h)
                l_pair = l_h if hh == 0 else jnp.where(lo_half, l_pair, l_h)
            o_ref[0, rows, lanes_p] = o_pair.astype(BF16)
            lse_ref[0, rows, lanes_p] = l_pair


def _stage_attn(aq, ak, av, gq, gk, batch, seq, group):
    _, dil = DILATED_PATTERNS[group]
    L = seq // dil
    assert L % N_BACK == 0
    lq = min(512, L)
    nq = L // lq
    sub = lq // N_BACK
    wq = N_GROUPS * A_WIDTH
    view = lambda a: a.reshape(batch, L, dil * wq)
    gq_t = jnp.tile(gq.astype(F32), A_HEADS).reshape(1, A_WIDTH)
    gk_t = jnp.tile(gk.astype(F32), A_HEADS).reshape(1, A_WIDTH)
    cur = pl.BlockSpec((1, lq, A_WIDTH), lambda b, r, i: (b, i, r * N_GROUPS + group))
    prev = pl.BlockSpec((1, N_BACK, A_WIDTH),
                        lambda b, r, i: (b, jnp.maximum(i * sub - 1, 0), r * N_GROUPS + group))
    gain = pl.BlockSpec((1, A_WIDTH), lambda b, r, i: (0, 0))
    out = pl.BlockSpec((1, lq, A_WIDTH), lambda b, r, i: (b, i, r))
    o, lse = pl.pallas_call(
        functools.partial(_attn_body, dil=dil, slopes=tuple(_alibi_slopes()[group]), lq=lq),
        grid=(batch, dil, nq),
        in_specs=[cur, prev, cur, prev, cur, gain, gain],
        out_specs=(out, out),
        out_shape=(jax.ShapeDtypeStruct((batch, L, dil * A_WIDTH), BF16),
                   jax.ShapeDtypeStruct((batch, L, dil * A_WIDTH), F32)),
        compiler_params=_params("parallel", "parallel", "parallel"),
        name=f"dilated_attn_d{dil}",
    )(view(aq), view(ak), view(ak), view(av), view(av), gq_t, gk_t)
    n = batch * seq
    return o.reshape(n, A_WIDTH), lse.reshape(n, A_WIDTH)


TOKEN_TILE_ROWS = D_MODEL // LANES


def _store_token_tiles(ref, val):
    t = val.shape[0]
    for s in range(TOKEN_TILE_ROWS):
        ref[pl.ds(s, t, stride=TOKEN_TILE_ROWS), :] = val[:, s * LANES:(s + 1) * LANES]


def _load_token_tiles(ref, t):
    return jnp.concatenate([ref[pl.ds(s, t, stride=TOKEN_TILE_ROWS), :] for s in range(TOKEN_TILE_ROWS)], axis=1)


def _token_tile(ref, tok):
    start = tok * TOKEN_TILE_ROWS
    if not isinstance(tok, int):
        start = pl.multiple_of(start, TOKEN_TILE_ROWS)
    return ref.at[pl.ds(start, TOKEN_TILE_ROWS)]


def _rows8(vals):
    t = vals[0].shape[1]
    rid = lax.broadcasted_iota(I32, (8, t), 0)
    out = jnp.zeros((8, t), vals[0].dtype)
    for k, v in enumerate(vals):
        out = jnp.where(rid == k, jnp.broadcast_to(v, (8, t)), out)
    return out


def _merge_body(hm_ref, o1_ref, o2_ref, o3_ref, l1_ref, l2_ref, l3_ref, sgm_ref, sga_ref, x_ref,
                wm_ref, wa_ref, wo_ref, g2_ref, wrT_ref, br_ref,
                x2_ref, xn_ref, idx_ref, gate_ref, rank_ref, cnt_ref, carry_ref):
    @pl.when(pl.program_id(0) == 0)
    def _():
        carry_ref[...] = jnp.zeros_like(carry_ref)

    l1, l2, l3 = l1_ref[...], l2_ref[...], l3_ref[...]
    lmax = jnp.maximum(jnp.maximum(l1, l2), l3)
    e1, e2, e3 = jnp.exp(l1 - lmax), jnp.exp(l2 - lmax), jnp.exp(l3 - lmax)
    h_a = (e1 * o1_ref[...].astype(F32) + e2 * o2_ref[...].astype(F32) + e3 * o3_ref[...].astype(F32)) / (e1 + e2 + e3)
    y = (sgm_ref[...].astype(F32) * jnp.dot(hm_ref[...], wm_ref[...], preferred_element_type=F32)
         + sga_ref[...].astype(F32) * jnp.dot(h_a.astype(BF16), wa_ref[...], preferred_element_type=F32))
    x2 = x_ref[...] + jnp.dot(y.astype(BF16), wo_ref[...], preferred_element_type=F32)
    x2_ref[...] = x2
    xn = x2 * lax.rsqrt(jnp.mean(x2 * x2, axis=-1, keepdims=True) + EPS) * g2_ref[...]
    _store_token_tiles(xn_ref, xn)

    logits = lax.dot_general(wrT_ref[...], xn, _NT, precision=lax.Precision.HIGHEST,
                             preferred_element_type=F32) + br_ref[...]
    t = logits.shape[1]
    eid = lax.broadcasted_iota(I32, (N_EXPERTS, t), 0).astype(F32)
    vals = logits
    top_v, top_i = [], []
    for _ in range(TOP_K):
        mx = jnp.max(vals, axis=0, keepdims=True)
        ik = jnp.min(jnp.where(vals == mx, eid, float(N_EXPERTS)), axis=0, keepdims=True)
        top_v.append(mx)
        top_i.append(ik)
        vals = jnp.where(eid == ik, -jnp.inf, vals)
    ex = [jnp.exp(v - top_v[0]) for v in top_v]
    den = ex[0] + ex[1] + ex[2] + ex[3]
    gate_ref[...] = _rows8([e / den for e in ex])
    idx_ref[...] = _rows8([i.astype(I32) for i in top_i])

    chosen = jnp.zeros((N_EXPERTS, t), F32)
    for ik in top_i:
        chosen = chosen + (eid == ik).astype(F32)
    before = (lax.broadcasted_iota(I32, (t, t), 0) < lax.broadcasted_iota(I32, (t, t), 1)).astype(BF16)
    prefix = jnp.dot(chosen.astype(BF16), before, preferred_element_type=F32)
    carry = carry_ref[:, 0:1]
    pos = prefix + carry
    rank_ref[...] = _rows8([jnp.sum(jnp.where(eid == ik, pos, 0.0), axis=0, keepdims=True).astype(I32)
                            for ik in top_i])
    total = carry + jnp.sum(chosen, axis=1, keepdims=True)
    carry_ref[...] = jnp.broadcast_to(total, carry_ref.shape)
    cnt_ref[...] = jnp.broadcast_to(total, cnt_ref.shape)


def _stage_merge(h_m, attn, sgm, sga, x2d, w_mb, w_ab, w_out, norm2_g, w_router, b_router, tm):
    n = x2d.shape[0]
    (o1, l1), (o2, l2), (o3, l3) = attn
    wm = w_mb.astype(BF16)
    wa = w_ab.astype(BF16)
    wo = w_out.astype(BF16)
    g2 = norm2_g.astype(F32).reshape(1, D_MODEL)
    wrT = w_router.astype(F32).T
    br = b_router.astype(F32).reshape(N_EXPERTS, 1)
    row = lambda w: pl.BlockSpec((tm, w), lambda i: (i, 0))
    rowT = lambda r: pl.BlockSpec((r, tm), lambda i: (0, i))
    full = lambda a: pl.BlockSpec(a.shape, lambda i: (0,) * a.ndim)
    return pl.pallas_call(
        _merge_body,
        grid=(n // tm,),
        in_specs=[row(M_WIDTH), row(A_WIDTH), row(A_WIDTH), row(A_WIDTH), row(A_WIDTH), row(A_WIDTH), row(A_WIDTH),
                  row(D_MODEL), row(D_MODEL), row(D_MODEL),
                  full(wm), full(wa), full(wo), full(g2), full(wrT), full(br)],
        out_specs=(row(D_MODEL), pl.BlockSpec((tm * TOKEN_TILE_ROWS, LANES), lambda i: (i, 0)),
                   rowT(8), rowT(8), rowT(8),
                   pl.BlockSpec((N_EXPERTS, LANES), lambda i: (0, 0))),
        out_shape=(jax.ShapeDtypeStruct((n, D_MODEL), F32),
                   jax.ShapeDtypeStruct((n * TOKEN_TILE_ROWS, LANES), F32),
                   jax.ShapeDtypeStruct((8, n), I32),
                   jax.ShapeDtypeStruct((8, n), F32),
                   jax.ShapeDtypeStruct((8, n), I32),
                   jax.ShapeDtypeStruct((N_EXPERTS, LANES), F32)),
        scratch_shapes=[pltpu.VMEM((N_EXPERTS, LANES), F32)],
        compiler_params=_params("arbitrary"),
        name="merge_route",
    )(h_m, o1, o2, o3, l1, l2, l3, sgm, sga, x2d, wm, wa, wo, g2, wrT, br)


def _offsets_body(cnt_ref, idx_ref, rank_ref, dest_ref, blk_ref, zlo_ref, zhi_ref, *, nblk_pad):
    cnt = cnt_ref[...]
    padded = jnp.floor((cnt + (MOE_BLOCK - 1)) * (1.0 / MOE_BLOCK)) * MOE_BLOCK
    lower = (lax.broadcasted_iota(I32, (N_EXPERTS, N_EXPERTS), 1)
             <= lax.broadcasted_iota(I32, (N_EXPERTS, N_EXPERTS), 0)).astype(F32)
    pends = jnp.dot(lower, padded, precision=lax.Precision.HIGHEST, preferred_element_type=F32)
    pstart = pends - padded
    zlo_ref[...] = (pstart + cnt).astype(I32)
    zhi_ref[...] = pends.astype(I32)

    t = idx_ref.shape[1]
    eid = lax.broadcasted_iota(I32, (N_EXPERTS, t), 0)
    ps = jnp.broadcast_to(pstart[:, 0:1], (N_EXPERTS, t))
    idx = idx_ref[...]
    rank = rank_ref[...]
    rows = []
    for k in range(TOP_K):
        off = jnp.sum(jnp.where(eid == idx[k:k + 1, :], ps, 0.0), axis=0, keepdims=True)
        rows.append(off.astype(I32) + rank[k:k + 1, :])
    dest_ref[...] = _rows8(rows)

    first_row = (lax.broadcasted_iota(I32, (N_EXPERTS, nblk_pad), 1) * MOE_BLOCK).astype(F32)
    pe = jnp.broadcast_to(pends[:, 0:1], (N_EXPERTS, nblk_pad))
    be = jnp.sum((pe <= first_row).astype(F32), axis=0, keepdims=True)
    be = jnp.minimum(be, float(N_EXPERTS - 1))
    nused = pends[N_EXPERTS - 1:N_EXPERTS, 0:1] * (1.0 / MOE_BLOCK)
    blk_ref[...] = _rows8([be.astype(I32), jnp.broadcast_to(nused, (1, nblk_pad)).astype(I32)])


def _stage_offsets(cnt, idx8, rank8, nblk, tm):
    n = idx8.shape[1]
    nblk_pad = -(-nblk // LANES) * LANES
    rowT = pl.BlockSpec((8, tm), lambda i: (0, i))
    const = lambda r, c: pl.BlockSpec((r, c), lambda i: (0, 0))
    return pl.pallas_call(
        functools.partial(_offsets_body, nblk_pad=nblk_pad),
        grid=(n // tm,),
        in_specs=[const(N_EXPERTS, LANES), rowT, rowT],
        out_specs=(rowT, const(8, nblk_pad), const(N_EXPERTS, LANES), const(N_EXPERTS, LANES)),
        out_shape=(jax.ShapeDtypeStruct((8, n), I32),
                   jax.ShapeDtypeStruct((8, nblk_pad), I32),
                   jax.ShapeDtypeStruct((N_EXPERTS, LANES), I32),
                   jax.ShapeDtypeStruct((N_EXPERTS, LANES), I32)),
        compiler_params=_params("arbitrary"),
        name="route_offsets",
    )(cnt, idx8, rank8)


DISPATCH_BURST = 128


def _dispatch_body(dest_ref, zlo_ref, zhi_ref, xn_hbm, zero_hbm, xs_hbm, sem, *, tm):
    base = pl.program_id(0) * tm

    def row_copy(src_row, dst_row):
        return pltpu.make_async_copy(_token_tile(xn_hbm, src_row), _token_tile(xs_hbm, dst_row), sem)

    def zero_copy(dst_row):
        return pltpu.make_async_copy(_token_tile(zero_hbm, 0), _token_tile(xs_hbm, dst_row), sem)

    block_tiles = MOE_BLOCK * TOKEN_TILE_ROWS

    def zero_block_copy(blk):
        dst = xs_hbm.at[pl.ds(pl.multiple_of(blk * block_tiles, block_tiles), block_tiles)]
        return pltpu.make_async_copy(zero_hbm, dst, sem)

    def burst(c, carry):
        def start(t, carry2):
            tok = c * DISPATCH_BURST + t
            for k in range(TOP_K):
                row_copy(base + tok, dest_ref[k, tok]).start()
            return carry2

        def wait(t, carry2):
            for k in range(TOP_K):
                row_copy(0, 0).wait()
            return carry2

        lax.fori_loop(0, DISPATCH_BURST, start, 0)
        lax.fori_loop(0, DISPATCH_BURST, wait, 0)
        return carry

    lax.fori_loop(0, tm // DISPATCH_BURST, burst, 0)

    @pl.when(pl.program_id(0) == pl.num_programs(0) - 1)
    def _():
        def per_expert(e, carry):
            lo = zlo_ref[e]
            hi = zhi_ref[e]

            def start(r, c2):
                zero_copy(r).start()
                return c2

            def wait(r, c2):
                zero_copy(0).wait()
                return c2

            lax.fori_loop(lo, hi, start, 0)
            lax.fori_loop(lo, hi, wait, 0)
            return carry

        lax.fori_loop(0, N_EXPERTS, per_expert, 0)

        first_unused = zhi_ref[N_EXPERTS - 1] // MOE_BLOCK
        nblk = xs_hbm.shape[0] // block_tiles

        def tail(blk, carry):
            zero_block_copy(blk).start()
            zero_block_copy(blk).wait()
            return carry

        lax.fori_loop(first_unused, nblk, tail, 0)


def _stage_dispatch(dest8, zlo, zhi, xn, nrows, tm):
    n = xn.shape[0] // TOKEN_TILE_ROWS
    zero = jnp.zeros((MOE_BLOCK * TOKEN_TILE_ROWS, LANES), F32)
    grid_spec = pltpu.PrefetchScalarGridSpec(
        num_scalar_prefetch=2,
        grid=(n // tm,),
        in_specs=[pl.BlockSpec((8, tm), lambda i, zl, zh: (0, i), memory_space=pltpu.SMEM),
                  pl.BlockSpec(memory_space=pl.ANY),
                  pl.BlockSpec(memory_space=pl.ANY)],
        out_specs=pl.BlockSpec(memory_space=pl.ANY),
        scratch_shapes=[pltpu.SemaphoreType.DMA(())],
    )

    def body(zlo_ref, zhi_ref, dest_ref, xn_hbm, zero_hbm, xs_hbm, sem):
        _dispatch_body(dest_ref, zlo_ref, zhi_ref, xn_hbm, zero_hbm, xs_hbm, sem, tm=tm)

    return pl.pallas_call(
        body,
        grid_spec=grid_spec,
        out_shape=jax.ShapeDtypeStruct((nrows * TOKEN_TILE_ROWS, LANES), F32),
        compiler_params=_params("arbitrary"),
        name="dispatch",
    )(zlo, zhi, dest8, xn, zero)


def _expert_body(be_ref, nu_ref, xs_ref, w1_ref, b1_ref, w2_ref, b2_ref, ys_ref, w1b_ref, w2b_ref):
    j = pl.program_id(0)
    used = j < nu_ref[0]
    jj = jnp.minimum(j, nu_ref[0] - 1)
    fresh = jnp.logical_or(j == 0, be_ref[jj] != be_ref[jnp.maximum(jj - 1, 0)])

    @pl.when(jnp.logical_and(used, fresh))
    def _():
        w1b_ref[...] = w1_ref[0].astype(BF16)
        w2b_ref[...] = w2_ref[0].astype(BF16)

    @pl.when(used)
    def _():
        xb = _load_token_tiles(xs_ref, MOE_BLOCK).astype(BF16)
        gu = jnp.dot(xb, w1b_ref[...], preferred_element_type=F32) + b1_ref[0]
        gate = jnp.minimum(gu[:, :D_FF], SWIGLU_LIMIT)
        lin = jnp.clip(gu[:, D_FF:], -SWIGLU_LIMIT, SWIGLU_LIMIT)
        act = (lin + 1.0) * (gate * jax.nn.sigmoid(SWIGLU_ALPHA * gate))
        ys = jnp.dot(act.astype(BF16), w2b_ref[...], preferred_element_type=F32) + b2_ref[0]
        _store_token_tiles(ys_ref, ys)

    @pl.when(jnp.logical_not(used))
    def _():
        ys_ref[...] = jnp.zeros_like(ys_ref)


def _stage_experts(block_e, nused, xs, w1, b1, w2, b2):
    nrows = xs.shape[0] // TOKEN_TILE_ROWS
    nblk = nrows // MOE_BLOCK
    blk = lambda j, be, nu: jnp.minimum(j, nu[0] - 1)
    exp = lambda j, be, nu: be[jnp.minimum(j, nu[0] - 1)]
    tiles = (MOE_BLOCK * TOKEN_TILE_ROWS, LANES)
    grid_spec = pltpu.PrefetchScalarGridSpec(
        num_scalar_prefetch=2,
        grid=(nblk,),
        in_specs=[pl.BlockSpec(tiles, lambda j, be, nu: (blk(j, be, nu), 0)),
                  pl.BlockSpec((1, D_MODEL, 2 * D_FF), lambda j, be, nu: (exp(j, be, nu), 0, 0)),
                  pl.BlockSpec((1, 1, 2 * D_FF), lambda j, be, nu: (exp(j, be, nu), 0, 0)),
                  pl.BlockSpec((1, D_FF, D_MODEL), lambda j, be, nu: (exp(j, be, nu), 0, 0)),
                  pl.BlockSpec((1, 1, D_MODEL), lambda j, be, nu: (exp(j, be, nu), 0, 0))],
        out_specs=pl.BlockSpec(tiles, lambda j, be, nu: (j, 0)),
        scratch_shapes=[pltpu.VMEM((D_MODEL, 2 * D_FF), BF16), pltpu.VMEM((D_FF, D_MODEL), BF16)],
    )
    return pl.pallas_call(
        _expert_body,
        grid_spec=grid_spec,
        out_shape=jax.ShapeDtypeStruct((nrows * TOKEN_TILE_ROWS, LANES), F32),
        compiler_params=_params("arbitrary"),
        name="experts",
    )(block_e, nused, xs, w1, b1.reshape(N_EXPERTS, 1, 2 * D_FF), w2, b2.reshape(N_EXPERTS, 1, D_MODEL))


def _combine_body(dest_ref, gate_ref, x2_ref, ys_hbm, out_ref, buf_ref, sem, *, tm):
    def row_copy(src_row, k, t):
        return pltpu.make_async_copy(_token_tile(ys_hbm, src_row), _token_tile(buf_ref.at[k], t), sem)

    def start(t, carry):
        for k in range(TOP_K):
            row_copy(dest_ref[k, t], k, t).start()
        return carry

    def wait(t, carry):
        for k in range(TOP_K):
            row_copy(0, k, 0).wait()
        return carry

    lax.fori_loop(0, tm, start, 0)
    lax.fori_loop(0, tm, wait, 0)

    zpad = jnp.zeros((LANES - 8, LANES), F32)
    for c in range(tm // LANES):
        rows = slice(c * LANES, (c + 1) * LANES)
        tiles = slice(c * LANES * TOKEN_TILE_ROWS, (c + 1) * LANES * TOKEN_TILE_ROWS)
        gcol = jnp.transpose(jnp.concatenate([gate_ref[:, rows], zpad], axis=0))
        acc = x2_ref[rows, :]
        for k in range(TOP_K):
            acc = acc + gcol[:, k:k + 1] * _load_token_tiles(buf_ref.at[k, tiles], LANES)
        out_ref[rows, :] = acc


def _stage_combine(dest8, gate8, x2, ys, tm):
    n = x2.shape[0]
    return pl.pallas_call(
        functools.partial(_combine_body, tm=tm),
        grid=(n // tm,),
        in_specs=[pl.BlockSpec((8, tm), lambda i: (0, i), memory_space=pltpu.SMEM),
                  pl.BlockSpec((8, tm), lambda i: (0, i)),
                  pl.BlockSpec((tm, D_MODEL), lambda i: (i, 0)),
                  pl.BlockSpec(memory_space=pl.ANY)],
        out_specs=pl.BlockSpec((tm, D_MODEL), lambda i: (i, 0)),
        out_shape=jax.ShapeDtypeStruct((n, D_MODEL), F32),
        scratch_shapes=[pltpu.VMEM((TOP_K, tm * TOKEN_TILE_ROWS, LANES), F32), pltpu.SemaphoreType.DMA(())],
        compiler_params=_params("arbitrary"),
        name="combine",
    )(dest8, gate8, x2, ys)


def _moe(x2, xn, idx8, gate8, rank8, cnt, w1, b1, w2, b2):
    n = x2.shape[0]
    nblk = -(-(n * TOP_K) // MOE_BLOCK) + N_EXPERTS
    dest8, blk8, zlo, zhi = _stage_offsets(cnt, idx8, rank8, nblk, min(2048, n))
    xs = _stage_dispatch(dest8, zlo[:, 0], zhi[:, 0], xn, nblk * MOE_BLOCK, min(512, n))
    ys = _stage_experts(blk8[0, :nblk], blk8[1, :1], xs, w1, b1, w2, b2)
    return _stage_combine(dest8, gate8, x2, ys, min(256, n))


def kernel(x, norm1_g, w_in, mlstm_gate_b, mlstm_norm_g, attn_q_norm_g, attn_k_norm_g, w_mlstm_branch,
           w_attn_branch, w_out, norm2_g, w_router, b_router, w1, b1, w2, b2):
    batch, seq, _ = x.shape
    n = batch * seq
    for l in range(norm1_g.shape[0]):
        x2d = x.reshape(n, D_MODEL)
        mq, kT, mv, so, gi, gf, aq, ak, av, sgm, sga = _stage_inproj(
            x2d, norm1_g[l], w_in[l], mlstm_gate_b[l], min(512, n))
        h_m = _stage_mlstm(mq, kT, mv, so, gi, gf, mlstm_norm_g[l], batch, seq, min(512, seq))
        attn = [_stage_attn(aq, ak, av, attn_q_norm_g[l, g], attn_k_norm_g[l, g], batch, seq, g)
                for g in range(N_GROUPS)]
        x2, xn, idx8, gate8, rank8, cnt = _stage_merge(
            h_m, attn, sgm, sga, x2d, w_mlstm_branch[l], w_attn_branch[l], w_out[l], norm2_g[l],
            w_router[l], b_router[l], min(512, n))
        out = _moe(x2, xn, idx8, gate8, rank8, cnt, w1[l], b1[l], w2[l], b2[l])
        x = out.reshape(batch, seq, D_MODEL)
    return x
```

```python
import functools

import numpy as np
import jax
import jax.numpy as jnp
from jax import lax
from jax.experimental import pallas as pl
from jax.experimental.pallas import tpu as pltpu

F32 = jnp.float32
BF16 = jnp.bfloat16
I32 = jnp.int32

D_MODEL = 1024
M_HEADS = 4
M_QK_DIM = 64
M_V_DIM = 128
GATE_SOFTCAP = 15.0
A_HEADS = 4
A_HEAD_DIM = 64
DILATED_PATTERNS = ((128, 1), (512, 4), (2048, 16))
N_GROUPS = len(DILATED_PATTERNS)
N_BACK = 128
N_EXPERTS = 32
TOP_K = 4
D_FF = 1024
SWIGLU_LIMIT = 7.0
SWIGLU_ALPHA = 1.702
MOE_BLOCK = 512
EPS = 1e-6

M_WIDTH = M_HEADS * M_V_DIM
M_QK_WIDTH = M_HEADS * M_QK_DIM
A_WIDTH = A_HEADS * A_HEAD_DIM
IN_SPLITS = (M_QK_WIDTH, M_QK_WIDTH, M_WIDTH, M_WIDTH, 2 * M_HEADS,
             N_GROUPS * A_WIDTH, N_GROUPS * A_WIDTH, N_GROUPS * A_WIDTH, D_MODEL, D_MODEL)

LANES = 128
VMEM_LIMIT = 56 * 1024 * 1024

_NT = (((1,), (1,)), ((), ()))


def _alibi_slopes():
    n = N_GROUPS * A_HEADS
    s = np.exp2(-8.0 * np.arange(1, n + 1) / n).astype(np.float32)
    return s.reshape(N_GROUPS, A_HEADS)


def _params(*sem):
    return pltpu.CompilerParams(dimension_semantics=sem, vmem_limit_bytes=VMEM_LIMIT)


def _log_sigmoid(x):
    return jnp.minimum(x, 0.0) - jnp.log1p(jnp.exp(-jnp.abs(x)))


_C_MQ = (0, 256)
_C_MV = (256, 768)
_C_MO = (768, 1280)
_C_AQ = (1280, 2048)
_C_AK = (2048, 2816)
_C_AV = (2816, 3584)
_C_GM = (3584, 4608)
_C_GA = (4608, 5632)
_W_MAIN = 5632
_WT_ROWS = M_QK_WIDTH + 16


def _split_residues(val, d, out_ref, st_ref):
    t = val.shape[0]
    if d == 1:
        out_ref[0, 0] = val.astype(out_ref.dtype)
        return
    st_ref[0] = val[:, :LANES]
    st_ref[1] = val[:, LANES:]
    for r in range(d):
        piece = jnp.concatenate([st_ref[0, pl.ds(r, t // d, stride=d), :],
                                 st_ref[1, pl.ds(r, t // d, stride=d), :]], axis=1)
        out_ref[0, r] = piece.astype(out_ref.dtype)


def _merge_residues(ref, d, st_ref):
    if d == 1:
        return ref[0, 0].astype(F32)
    m = ref.shape[2]
    for r in range(d):
        blk = ref[0, r].astype(F32)
        st_ref[0, pl.ds(r, m, stride=d), :] = blk[:, :LANES]
        st_ref[1, pl.ds(r, m, stride=d), :] = blk[:, LANES:]
    return jnp.concatenate([st_ref[0], st_ref[1]], axis=1)


def _inproj_body(x_ref, g1_ref, wm_ref, wt_ref, gb_ref,
                 mq_ref, kT_ref, mv_ref, so_ref, gi_ref, gf_ref,
                 q0_ref, q1_ref, q2_ref, k0_ref, k1_ref, k2_ref, v0_ref, v1_ref, v2_ref,
                 sgm_ref, sga_ref, st_ref):
    x = x_ref[...]
    h = x * lax.rsqrt(jnp.mean(x * x, axis=-1, keepdims=True) + EPS) * g1_ref[...]
    hb = h.astype(BF16)

    def seg(c):
        return jnp.dot(hb, wm_ref[:, c[0]:c[1]], preferred_element_type=F32)

    mq_ref[...] = seg(_C_MQ).astype(BF16)
    mv_ref[...] = seg(_C_MV).astype(BF16)
    so_ref[...] = jax.nn.sigmoid(seg(_C_MO)).astype(BF16)
    for c, refs in ((_C_AQ, (q0_ref, q1_ref, q2_ref)), (_C_AK, (k0_ref, k1_ref, k2_ref)),
                    (_C_AV, (v0_ref, v1_ref, v2_ref))):
        val = seg(c)
        for g, ref in enumerate(refs):
            _split_residues(val[:, g * A_WIDTH:(g + 1) * A_WIDTH], DILATED_PATTERNS[g][1], ref, st_ref)
    sgm_ref[...] = jax.nn.sigmoid(seg(_C_GM)).astype(BF16)
    sga_ref[...] = jax.nn.sigmoid(seg(_C_GA)).astype(BF16)

    t = lax.dot_general(wt_ref[...], hb, _NT, preferred_element_type=F32)
    kT_ref[...] = t[0:M_QK_WIDTH].astype(BF16)
    zi = t[M_QK_WIDTH:M_QK_WIDTH + 8] + gb_ref[0:8]
    zf = t[M_QK_WIDTH + 8:M_QK_WIDTH + 16] + gb_ref[8:16]
    gi_ref[...] = GATE_SOFTCAP * jnp.tanh(zi / GATE_SOFTCAP)
    gf_ref[...] = _log_sigmoid(GATE_SOFTCAP * jnp.tanh(zf / GATE_SOFTCAP))


def _stage_inproj(x2d, norm1_g, w_in, gate_b, batch, seq, tm):
    n = x2d.shape[0]
    steps = seq // tm
    cuts = np.concatenate([[0], np.cumsum(IN_SPLITS)])
    col = lambda i: w_in[:, cuts[i]:cuts[i + 1]]
    wm = jnp.concatenate([col(0), col(2), col(3), col(5), col(6), col(7), col(8), col(9)],
                         axis=1).astype(BF16)
    wif = col(4)
    z4 = jnp.zeros((4, D_MODEL), w_in.dtype)
    wt = jnp.concatenate([col(1).T, wif[:, :M_HEADS].T, z4, wif[:, M_HEADS:].T, z4], axis=0).astype(BF16)
    gb = jnp.zeros((16, 1), F32)
    gb = gb.at[0:4, 0].set(gate_b[:M_HEADS].astype(F32)).at[8:12, 0].set(gate_b[M_HEADS:].astype(F32))
    g1 = norm1_g.astype(F32).reshape(1, D_MODEL)

    row = lambda w: pl.BlockSpec((tm, w), lambda i: (i, 0))
    rowT = lambda r: pl.BlockSpec((r, tm), lambda i: (0, i))
    full = lambda a: pl.BlockSpec(a.shape, lambda i: (0,) * a.ndim)
    dils = [d for _, d in DILATED_PATTERNS]
    res_shape = lambda d: jax.ShapeDtypeStruct((batch, d, seq // d, A_WIDTH), BF16)
    res_spec = lambda d: pl.BlockSpec((1, d, tm // d, A_WIDTH), lambda i: (i // steps, 0, i % steps, 0))
    out_shapes = (
        jax.ShapeDtypeStruct((n, M_QK_WIDTH), BF16),
        jax.ShapeDtypeStruct((M_QK_WIDTH, n), BF16),
        jax.ShapeDtypeStruct((n, M_WIDTH), BF16),
        jax.ShapeDtypeStruct((n, M_WIDTH), BF16),
        jax.ShapeDtypeStruct((8, n), F32),
        jax.ShapeDtypeStruct((8, n), F32),
        *[res_shape(d) for d in dils], *[res_shape(d) for d in dils], *[res_shape(d) for d in dils],
        jax.ShapeDtypeStruct((n, D_MODEL), BF16),
        jax.ShapeDtypeStruct((n, D_MODEL), BF16),
    )
    out_specs = (row(M_QK_WIDTH), rowT(M_QK_WIDTH), row(M_WIDTH), row(M_WIDTH), rowT(8), rowT(8),
                 *[res_spec(d) for d in dils], *[res_spec(d) for d in dils], *[res_spec(d) for d in dils],
                 row(D_MODEL), row(D_MODEL))
    outs = pl.pallas_call(
        _inproj_body,
        grid=(n // tm,),
        in_specs=[row(D_MODEL), full(g1), full(wm), full(wt), full(gb)],
        out_specs=out_specs,
        out_shape=out_shapes,
        scratch_shapes=[pltpu.VMEM((2, tm, LANES), F32)],
        compiler_params=_params("parallel"),
        name="inproj",
    )(x2d, g1, wm, wt, gb)
    mq, kT, mv, so, gi, gf = outs[:6]
    aq, ak, av = outs[6:9], outs[9:12], outs[12:15]
    return mq, kT, mv, so, gi, gf, aq, ak, av, outs[15], outs[16]


M_CHUNK_LEN = 128


def _mlstm_body(q_ref, kT_ref, v_ref, so_ref, gi_ref, gf_ref, ng_ref, o_ref, c_ref, m_ref, *, nchunk):
    L = M_CHUNK_LEN

    @pl.when(pl.program_id(1) == 0)
    def _():
        c_ref[...] = jnp.zeros_like(c_ref)
        m_ref[...] = jnp.zeros_like(m_ref)

    lane8 = lax.broadcasted_iota(I32, (8, L), 1)
    causal = lax.broadcasted_iota(I32, (L, L), 1) <= lax.broadcasted_iota(I32, (L, L), 0)
    lo_half = lax.broadcasted_iota(I32, (L, LANES), 1) < M_QK_DIM
    ones = jnp.ones((L, M_V_DIM), BF16)

    for c in range(nchunk):
        rows = slice(c * L, (c + 1) * L)
        gi = gi_ref[:, rows]
        b = gf_ref[:, rows]
        sh = 1
        while sh < L:
            b = b + jnp.where(lane8 >= sh, pltpu.roll(b, sh, 1), 0.0)
            sh *= 2
        u = gi - b
        g = b[:, L - 1:L]
        a = g + u
        amax = jnp.max(a, axis=1, keepdims=True)
        m_prev = m_ref[:, 0:1]
        m_new = jnp.maximum(g + m_prev, amax)
        w = jnp.exp(a - m_new) * (M_QK_DIM ** -0.5)
        s_old = jnp.exp(g + m_prev - m_new)

        for p in range(M_HEADS // 2):
            lanes_p = slice(p * LANES, (p + 1) * LANES)
            q_pair = q_ref[rows, lanes_p]
            kT_pair = kT_ref[lanes_p, rows]
            c_pair = c_ref[lanes_p, :].astype(BF16)
            for hh in range(2):
                h = 2 * p + hh
                hl = slice(h * M_V_DIM, (h + 1) * M_V_DIM)
                qm = jnp.where(lo_half if hh == 0 else jnp.logical_not(lo_half), q_pair, jnp.zeros_like(q_pair))
                bcol = jnp.transpose(jnp.broadcast_to(b[h:h + 1, :], (L, L)))
                dm = jnp.where(causal, bcol + u[h:h + 1, :], -jnp.inf)
                inter = bcol + m_prev[h:h + 1, :]
                m_t = jnp.maximum(inter, jnp.max(dm, axis=1, keepdims=True))
                s = jnp.dot(qm, kT_pair, preferred_element_type=F32) * (M_QK_DIM ** -0.5)
                pmat = (s * jnp.exp(dm - m_t)).astype(BF16)
                vext = jnp.concatenate([v_ref[rows, hl], ones], axis=1)
                sc = jnp.exp(inter - m_t)
                out = (jnp.dot(pmat, vext, preferred_element_type=F32)
                       + jnp.concatenate([sc, sc], axis=1) * jnp.dot(qm, c_pair, preferred_element_type=F32))
                hv = out[:, :M_V_DIM] / jnp.maximum(jnp.abs(out[:, M_V_DIM:]), jnp.exp(-m_t))
                hn = hv * lax.rsqrt(jnp.mean(hv * hv, axis=1, keepdims=True) + EPS)
                hn = hn * ng_ref[:, hl] * so_ref[rows, hl].astype(F32)
                o_ref[rows, hl] = hn.astype(BF16)
                hr = slice(h * M_QK_DIM, (h + 1) * M_QK_DIM)
                kw = (kT_ref[hr, rows].astype(F32) * w[h:h + 1, :]).astype(BF16)
                c_ref[hr, :] = s_old[h:h + 1, :] * c_ref[hr, :] + jnp.dot(kw, vext, preferred_element_type=F32)
        m_ref[...] = jnp.broadcast_to(m_new, m_ref.shape)


def _stage_mlstm(mq, kT, mv, so, gi, gf, norm_g, batch, seq, rows_per_step):
    n = batch * seq
    R = rows_per_step
    steps = seq // R
    ng = norm_g.astype(F32).reshape(1, M_WIDTH)
    row = lambda w: pl.BlockSpec((R, w), lambda b, i: (b * steps + i, 0))
    rowT = lambda r: pl.BlockSpec((r, R), lambda b, i: (0, b * steps + i))
    return pl.pallas_call(
        functools.partial(_mlstm_body, nchunk=R // M_CHUNK_LEN),
        grid=(batch, steps),
        in_specs=[row(M_QK_WIDTH), rowT(M_QK_WIDTH), row(M_WIDTH), row(M_WIDTH), rowT(8), rowT(8),
                  pl.BlockSpec((1, M_WIDTH), lambda b, i: (0, 0))],
        out_specs=row(M_WIDTH),
        out_shape=jax.ShapeDtypeStruct((n, M_WIDTH), BF16),
        scratch_shapes=[pltpu.VMEM((M_QK_WIDTH, 2 * M_V_DIM), F32), pltpu.VMEM((8, LANES), F32)],
        compiler_params=_params("parallel", "arbitrary"),
        name="mlstm",
    )(mq, kT, mv, so, gi, gf, ng)


def _attn_body(q_ref, kp_ref, kc_ref, vp_ref, vc_ref, gq_ref, gk_ref, o_ref, lse_ref, *, dil, slopes, lq):
    QB = N_BACK
    first = pl.program_id(2) == 0
    hid_r = lax.broadcasted_iota(I32, (A_WIDTH, A_WIDTH), 0) // A_HEAD_DIM
    hid_c = lax.broadcasted_iota(I32, (A_WIDTH, A_WIDTH), 1) // A_HEAD_DIM
    head_ones = (hid_r == hid_c).astype(BF16)

    def head_norm(xb, gain):
        xf = xb.astype(F32)
        ss = jnp.dot((xf * xf).astype(BF16), head_ones, preferred_element_type=F32)
        return xf * lax.rsqrt(ss * (1.0 / A_HEAD_DIM) + EPS) * gain

    qn = (head_norm(q_ref[0, 0], gq_ref[...]) * (A_HEAD_DIM ** -0.5)).astype(BF16)
    kcn = head_norm(kc_ref[0, 0], gk_ref[...]).astype(BF16)
    kpn = head_norm(kp_ref[0, 0], gk_ref[...]).astype(BF16)
    vc = vc_ref[0, 0]
    vp = vp_ref[0, 0]

    qi = lax.broadcasted_iota(I32, (QB, 2 * QB), 0)
    kj = lax.broadcasted_iota(I32, (QB, 2 * QB), 1)
    dist = qi + QB - kj
    band = jnp.logical_and(dist >= 0, dist <= N_BACK)
    band_first = jnp.logical_and(band, jnp.logical_or(kj >= QB, jnp.logical_not(first)))
    distf = (dist * dil).astype(F32)
    lo_half = lax.broadcasted_iota(I32, (QB, LANES), 1) < A_HEAD_DIM
    ones = jnp.ones((2 * QB, LANES), BF16)

    for j in range(lq // QB):
        rows = slice(j * QB, (j + 1) * QB)
        prow = slice((j - 1) * QB, j * QB)
        keys = jnp.concatenate([kpn if j == 0 else kcn[prow], kcn[rows]], axis=0)
        vals = jnp.concatenate([vp if j == 0 else vc[prow], vc[rows]], axis=0)
        mask = band_first if j == 0 else band
        for p in range(A_HEADS // 2):
            lanes_p = slice(p * LANES, (p + 1) * LANES)
            q_pair = qn[rows, lanes_p]
            k_pair = keys[:, lanes_p]
            vext = jnp.concatenate([vals[:, lanes_p], ones], axis=1)
            o_pair = None
            l_pair = None
            for hh in range(2):
                h = 2 * p + hh
                sel = lo_half if hh == 0 else jnp.logical_not(lo_half)
                qm = jnp.where(sel, q_pair, jnp.zeros_like(q_pair))
                s = lax.dot_general(qm, k_pair, _NT, preferred_element_type=F32)
                s = jnp.where(mask, s - float(slopes[h]) * distf, -jnp.inf)
                m = jnp.max(s, axis=1, keepdims=True)
                pv = jnp.dot(jnp.exp(s - m).astype(BF16), vext, preferred_element_type=F32)
                den = pv[:, LANES:]
                o_h = pv[:, :LANES] / den
                l_h = m + jnp.log(den)
                o_pair = o_h if hh == 0 else jnp.where(lo_half, o_pair, o_h)
                l_pair = l_h if hh == 0 else jnp.where(lo_half, l_pair, l_h)
            o_ref[0, 0, rows, lanes_p] = o_pair.astype(BF16)
            lse_ref[0, 0, rows, lanes_p] = l_pair


def _stage_attn(aq, ak, av, gq, gk, batch, seq, group):
    _, dil = DILATED_PATTERNS[group]
    L = seq // dil
    assert L % N_BACK == 0
    lq = min(512, L)
    nq = L // lq
    sub = lq // N_BACK
    gq_t = jnp.tile(gq.astype(F32), A_HEADS).reshape(1, A_WIDTH)
    gk_t = jnp.tile(gk.astype(F32), A_HEADS).reshape(1, A_WIDTH)
    cur = pl.BlockSpec((1, 1, lq, A_WIDTH), lambda b, r, i: (b, r, i, 0))
    prev = pl.BlockSpec((1, 1, N_BACK, A_WIDTH), lambda b, r, i: (b, r, jnp.maximum(i * sub - 1, 0), 0))
    gain = pl.BlockSpec((1, A_WIDTH), lambda b, r, i: (0, 0))
    return pl.pallas_call(
        functools.partial(_attn_body, dil=dil, slopes=tuple(_alibi_slopes()[group]), lq=lq),
        grid=(batch, dil, nq),
        in_specs=[cur, prev, cur, prev, cur, gain, gain],
        out_specs=(cur, cur),
        out_shape=(jax.ShapeDtypeStruct((batch, dil, L, A_WIDTH), BF16),
                   jax.ShapeDtypeStruct((batch, dil, L, A_WIDTH), F32)),
        compiler_params=_params("parallel", "parallel", "parallel"),
        name=f"dilated_attn_d{dil}",
    )(aq, ak, ak, av, av, gq_t, gk_t)


TOKEN_TILE_ROWS = D_MODEL // LANES


def _store_token_tiles(ref, val):
    t = val.shape[0]
    for s in range(TOKEN_TILE_ROWS):
        ref[pl.ds(s, t, stride=TOKEN_TILE_ROWS), :] = val[:, s * LANES:(s + 1) * LANES]


def _load_token_tiles(ref, t):
    return jnp.concatenate([ref[pl.ds(s, t, stride=TOKEN_TILE_ROWS), :] for s in range(TOKEN_TILE_ROWS)], axis=1)


def _token_tile(ref, tok):
    start = tok * TOKEN_TILE_ROWS
    if not isinstance(tok, int):
        start = pl.multiple_of(start, TOKEN_TILE_ROWS)
    return ref.at[pl.ds(start, TOKEN_TILE_ROWS)]


def _rows8(vals):
    t = vals[0].shape[1]
    rid = lax.broadcasted_iota(I32, (8, t), 0)
    out = jnp.zeros((8, t), vals[0].dtype)
    for k, v in enumerate(vals):
        out = jnp.where(rid == k, jnp.broadcast_to(v, (8, t)), out)
    return out


def _merge_body(hm_ref, o1_ref, o2_ref, o3_ref, l1_ref, l2_ref, l3_ref, sgm_ref, sga_ref, x_ref,
                wm_ref, wa_ref, wo_ref, g2_ref, wrT_ref, br_ref,
                x2_ref, xn_ref, loc_ref, gate_ref, tcnt_ref, tcar_ref, cnt_ref, carry_ref, st_ref):
    @pl.when(pl.program_id(0) == 0)
    def _():
        carry_ref[...] = jnp.zeros_like(carry_ref)

    dils = [d for _, d in DILATED_PATTERNS]
    l1, l2, l3 = [_merge_residues(r, d, st_ref) for r, d in zip((l1_ref, l2_ref, l3_ref), dils)]
    lmax = jnp.maximum(jnp.maximum(l1, l2), l3)
    e1, e2, e3 = jnp.exp(l1 - lmax), jnp.exp(l2 - lmax), jnp.exp(l3 - lmax)
    num = e1 * _merge_residues(o1_ref, dils[0], st_ref)
    num = num + e2 * _merge_residues(o2_ref, dils[1], st_ref)
    num = num + e3 * _merge_residues(o3_ref, dils[2], st_ref)
    h_a = num / (e1 + e2 + e3)
    y = (sgm_ref[...].astype(F32) * jnp.dot(hm_ref[...], wm_ref[...], preferred_element_type=F32)
         + sga_ref[...].astype(F32) * jnp.dot(h_a.astype(BF16), wa_ref[...], preferred_element_type=F32))
    x2 = x_ref[...] + jnp.dot(y.astype(BF16), wo_ref[...], preferred_element_type=F32)
    x2_ref[...] = x2
    xn = x2 * lax.rsqrt(jnp.mean(x2 * x2, axis=-1, keepdims=True) + EPS) * g2_ref[...]
    xn_ref[...] = xn.astype(BF16)

    logits = lax.dot_general(wrT_ref[...], xn, _NT, precision=lax.Precision.HIGHEST,
                             preferred_element_type=F32) + br_ref[...]
    t = logits.shape[1]
    eid = lax.broadcasted_iota(I32, (N_EXPERTS, t), 0).astype(F32)
    vals = logits
    top_v, top_i = [], []
    for _ in range(TOP_K):
        mx = jnp.max(vals, axis=0, keepdims=True)
        ik = jnp.min(jnp.where(vals == mx, eid, float(N_EXPERTS)), axis=0, keepdims=True)
        top_v.append(mx)
        top_i.append(ik)
        vals = jnp.where(eid == ik, -jnp.inf, vals)
    ex = [jnp.exp(v - top_v[0]) for v in top_v]
    den = ex[0] + ex[1] + ex[2] + ex[3]
    gate_ref[...] = _rows8([e / den for e in ex])

    chosen = jnp.zeros((N_EXPERTS, t), F32)
    for ik in top_i:
        chosen = chosen + (eid == ik).astype(F32)
    before = (lax.broadcasted_iota(I32, (t, t), 0) < lax.broadcasted_iota(I32, (t, t), 1)).astype(BF16)
    prefix = jnp.dot(chosen.astype(BF16), before, preferred_element_type=F32)
    tcount = jnp.broadcast_to(jnp.sum(chosen, axis=1, keepdims=True), (N_EXPERTS, LANES))
    below = (lax.broadcasted_iota(I32, (N_EXPERTS, N_EXPERTS), 1)
             < lax.broadcasted_iota(I32, (N_EXPERTS, N_EXPERTS), 0)).astype(F32)
    tile_off = jnp.dot(below, tcount, precision=lax.Precision.HIGHEST, preferred_element_type=F32)
    pos = prefix + tile_off[:, 0:1]
    loc_ref[...] = _rows8([jnp.sum(jnp.where(eid == ik, pos, 0.0), axis=0, keepdims=True).astype(I32)
                           for ik in top_i])
    carry = carry_ref[...]
    tcnt_ref[...] = tcount.astype(I32)
    tcar_ref[...] = carry.astype(I32)
    total = carry + tcount
    carry_ref[...] = total
    cnt_ref[...] = total


def _stage_merge(h_m, attn, sgm, sga, x2d, w_mb, w_ab, w_out, norm2_g, w_router, b_router, batch, seq, tm):
    n = x2d.shape[0]
    steps = seq // tm
    (o1, l1), (o2, l2), (o3, l3) = attn
    wm = w_mb.astype(BF16)
    wa = w_ab.astype(BF16)
    wo = w_out.astype(BF16)
    g2 = norm2_g.astype(F32).reshape(1, D_MODEL)
    wrT = w_router.astype(F32).T
    br = b_router.astype(F32).reshape(N_EXPERTS, 1)
    row = lambda w: pl.BlockSpec((tm, w), lambda i: (i, 0))
    rowT = lambda r: pl.BlockSpec((r, tm), lambda i: (0, i))
    full = lambda a: pl.BlockSpec(a.shape, lambda i: (0,) * a.ndim)
    res = lambda d: pl.BlockSpec((1, d, tm // d, A_WIDTH), lambda i: (i // steps, 0, i % steps, 0))
    dils = [d for _, d in DILATED_PATTERNS]
    per_tile = pl.BlockSpec((N_EXPERTS, LANES), lambda i: (0, i))
    return pl.pallas_call(
        _merge_body,
        grid=(n // tm,),
        in_specs=[row(M_WIDTH), *[res(d) for d in dils], *[res(d) for d in dils],
                  row(D_MODEL), row(D_MODEL), row(D_MODEL),
                  full(wm), full(wa), full(wo), full(g2), full(wrT), full(br)],
        out_specs=(row(D_MODEL), row(D_MODEL), rowT(8), rowT(8), per_tile, per_tile,
                   pl.BlockSpec((N_EXPERTS, LANES), lambda i: (0, 0))),
        out_shape=(jax.ShapeDtypeStruct((n, D_MODEL), F32),
                   jax.ShapeDtypeStruct((n, D_MODEL), BF16),
                   jax.ShapeDtypeStruct((8, n), I32),
                   jax.ShapeDtypeStruct((8, n), F32),
                   jax.ShapeDtypeStruct((N_EXPERTS, (n // tm) * LANES), I32),
                   jax.ShapeDtypeStruct((N_EXPERTS, (n // tm) * LANES), I32),
                   jax.ShapeDtypeStruct((N_EXPERTS, LANES), F32)),
        scratch_shapes=[pltpu.VMEM((N_EXPERTS, LANES), F32), pltpu.VMEM((2, tm, LANES), F32)],
        compiler_params=_params("arbitrary"),
        name="merge_route",
    )(h_m, o1, o2, o3, l1, l2, l3, sgm, sga, x2d, wm, wa, wo, g2, wrT, br)


def _offsets_body(cnt_ref, blk_ref, pstart_ref, zlo_ref, zhi_ref, *, nblk_pad):
    cnt = cnt_ref[...]
    padded = jnp.floor((cnt + (MOE_BLOCK - 1)) * (1.0 / MOE_BLOCK)) * MOE_BLOCK
    lower = (lax.broadcasted_iota(I32, (N_EXPERTS, N_EXPERTS), 1)
             <= lax.broadcasted_iota(I32, (N_EXPERTS, N_EXPERTS), 0)).astype(F32)
    pends = jnp.dot(lower, padded, precision=lax.Precision.HIGHEST, preferred_element_type=F32)
    pstart = pends - padded
    pstart_ref[...] = pstart.astype(I32)
    zlo_ref[...] = (pstart + cnt).astype(I32)
    zhi_ref[...] = pends.astype(I32)

    first_row = (lax.broadcasted_iota(I32, (N_EXPERTS, nblk_pad), 1) * MOE_BLOCK).astype(F32)
    pe = jnp.broadcast_to(pends[:, 0:1], (N_EXPERTS, nblk_pad))
    be = jnp.sum((pe <= first_row).astype(F32), axis=0, keepdims=True)
    be = jnp.minimum(be, float(N_EXPERTS - 1))
    nused = pends[N_EXPERTS - 1:N_EXPERTS, 0:1] * (1.0 / MOE_BLOCK)
    blk_ref[...] = _rows8([be.astype(I32), jnp.broadcast_to(nused, (1, nblk_pad)).astype(I32)])


def _stage_offsets(cnt, nblk):
    nblk_pad = -(-nblk // LANES) * LANES
    const = lambda r, c: pl.BlockSpec((r, c), lambda i: (0, 0))
    per_expert = jax.ShapeDtypeStruct((N_EXPERTS, LANES), I32)
    return pl.pallas_call(
        functools.partial(_offsets_body, nblk_pad=nblk_pad),
        grid=(1,),
        in_specs=[const(N_EXPERTS, LANES)],
        out_specs=(const(8, nblk_pad), const(N_EXPERTS, LANES), const(N_EXPERTS, LANES), const(N_EXPERTS, LANES)),
        out_shape=(jax.ShapeDtypeStruct((8, nblk_pad), I32), per_expert, per_expert, per_expert),
        compiler_params=_params("arbitrary"),
        name="route_offsets",
    )(cnt)


RUN_BITS = 10


def _tile_rows(ref, first_row, nrows):
    start = first_row * TOKEN_TILE_ROWS
    if not isinstance(first_row, int):
        start = pl.multiple_of(start, TOKEN_TILE_ROWS)
    return ref.at[pl.ds(start, nrows * TOKEN_TILE_ROWS)]


def _for_each_piece(length, fn):
    for b in reversed(range(RUN_BITS)):
        @pl.when(((length >> b) & 1) == 1)
        def _(b=b):
            fn((length >> (b + 1)) << (b + 1), 1 << b)


def _for_each_run(tile, tcnt_ref, tcar_ref, pstart_ref, fn):
    def per_expert(e, local):
        count = tcnt_ref[tile, e]
        first = pstart_ref[e] + tcar_ref[tile, e]
        _for_each_piece(count, lambda off, size: fn(local + off, first + off, size))
        return local + count

    lax.fori_loop(0, N_EXPERTS, per_expert, 0)


PERM_CHUNK = 256


def _dispatch_body(tcnt_ref, tcar_ref, pstart_ref, zlo_ref, zhi_ref, loc_ref, xn_ref, zero_hbm, xs_hbm,
                   buf_ref, sem, *, tm):
    tile = pl.program_id(0)
    loc = loc_ref[...]
    xn = xn_ref[...]
    for c in range(TOP_K * tm // PERM_CHUNK):
        lid = lax.broadcasted_iota(I32, (PERM_CHUNK, tm), 0) + c * PERM_CHUNK
        hit = lid == loc[0:1, :]
        for k in range(1, TOP_K):
            hit = jnp.logical_or(hit, lid == loc[k:k + 1, :])
        rows = jnp.dot(jnp.where(hit, 1.0, 0.0).astype(BF16), xn, preferred_element_type=F32)
        _store_token_tiles(buf_ref.at[pl.ds(c * PERM_CHUNK * TOKEN_TILE_ROWS, PERM_CHUNK * TOKEN_TILE_ROWS)], rows)

    def run_copy(local, first, size):
        return pltpu.make_async_copy(_tile_rows(buf_ref, local, size), _tile_rows(xs_hbm, first, size), sem)

    _for_each_run(tile, tcnt_ref, tcar_ref, pstart_ref, lambda l, f, s: run_copy(l, f, s).start())
    _for_each_run(tile, tcnt_ref, tcar_ref, pstart_ref, lambda l, f, s: run_copy(l, f, s).wait())

    @pl.when(tile == pl.num_programs(0) - 1)
    def _():
        def zero_copy(first, size):
            return pltpu.make_async_copy(_tile_rows(zero_hbm, 0, size), _tile_rows(xs_hbm, first, size), sem)

        def per_expert(e, carry):
            lo = zlo_ref[e]
            npad = zhi_ref[e] - lo
            _for_each_piece(npad, lambda off, size: zero_copy(lo + off, size).start())
            _for_each_piece(npad, lambda off, size: zero_copy(lo + off, size).wait())
            return carry

        lax.fori_loop(0, N_EXPERTS, per_expert, 0)

        first_unused = zhi_ref[N_EXPERTS - 1] // MOE_BLOCK
        nblk = xs_hbm.shape[0] // (MOE_BLOCK * TOKEN_TILE_ROWS)

        def tail(blk, carry):
            zero_copy(blk * MOE_BLOCK, MOE_BLOCK).start()
            zero_copy(blk * MOE_BLOCK, MOE_BLOCK).wait()
            return carry

        lax.fori_loop(first_unused, nblk, tail, 0)


def _stage_dispatch(tables, loc8, xn, nrows, tm):
    n = xn.shape[0]
    zero = jnp.zeros((MOE_BLOCK * TOKEN_TILE_ROWS, LANES), F32)
    grid_spec = pltpu.PrefetchScalarGridSpec(
        num_scalar_prefetch=5,
        grid=(n // tm,),
        in_specs=[pl.BlockSpec((8, tm), lambda i, *_: (0, i)),
                  pl.BlockSpec((tm, D_MODEL), lambda i, *_: (i, 0)),
                  pl.BlockSpec(memory_space=pl.ANY)],
        out_specs=pl.BlockSpec(memory_space=pl.ANY),
        scratch_shapes=[pltpu.VMEM((TOP_K * tm * TOKEN_TILE_ROWS, LANES), F32), pltpu.SemaphoreType.DMA(())],
    )
    return pl.pallas_call(
        functools.partial(_dispatch_body, tm=tm),
        grid_spec=grid_spec,
        out_shape=jax.ShapeDtypeStruct((nrows * TOKEN_TILE_ROWS, LANES), F32),
        compiler_params=_params("arbitrary"),
        name="dispatch",
    )(*tables, loc8, xn, zero)


def _expert_body(be_ref, nu_ref, xs_ref, w1_ref, b1_ref, w2_ref, b2_ref, ys_ref, w1b_ref, w2b_ref):
    j = pl.program_id(0)
    used = j < nu_ref[0]
    jj = jnp.minimum(j, nu_ref[0] - 1)
    fresh = jnp.logical_or(j == 0, be_ref[jj] != be_ref[jnp.maximum(jj - 1, 0)])

    @pl.when(jnp.logical_and(used, fresh))
    def _():
        w1b_ref[...] = w1_ref[0].astype(BF16)
        w2b_ref[...] = w2_ref[0].astype(BF16)

    @pl.when(used)
    def _():
        xb = _load_token_tiles(xs_ref, MOE_BLOCK).astype(BF16)
        gu = jnp.dot(xb, w1b_ref[...], preferred_element_type=F32) + b1_ref[0]
        gate = jnp.minimum(gu[:, :D_FF], SWIGLU_LIMIT)
        lin = jnp.clip(gu[:, D_FF:], -SWIGLU_LIMIT, SWIGLU_LIMIT)
        act = (lin + 1.0) * (gate * jax.nn.sigmoid(SWIGLU_ALPHA * gate))
        ys = jnp.dot(act.astype(BF16), w2b_ref[...], preferred_element_type=F32) + b2_ref[0]
        _store_token_tiles(ys_ref, ys)

    @pl.when(jnp.logical_not(used))
    def _():
        ys_ref[...] = jnp.zeros_like(ys_ref)


def _stage_experts(block_e, nused, xs, w1, b1, w2, b2):
    nrows = xs.shape[0] // TOKEN_TILE_ROWS
    nblk = nrows // MOE_BLOCK
    blk = lambda j, be, nu: jnp.minimum(j, nu[0] - 1)
    exp = lambda j, be, nu: be[jnp.minimum(j, nu[0] - 1)]
    tiles = (MOE_BLOCK * TOKEN_TILE_ROWS, LANES)
    grid_spec = pltpu.PrefetchScalarGridSpec(
        num_scalar_prefetch=2,
        grid=(nblk,),
        in_specs=[pl.BlockSpec(tiles, lambda j, be, nu: (blk(j, be, nu), 0)),
                  pl.BlockSpec((1, D_MODEL, 2 * D_FF), lambda j, be, nu: (exp(j, be, nu), 0, 0)),
                  pl.BlockSpec((1, 1, 2 * D_FF), lambda j, be, nu: (exp(j, be, nu), 0, 0)),
                  pl.BlockSpec((1, D_FF, D_MODEL), lambda j, be, nu: (exp(j, be, nu), 0, 0)),
                  pl.BlockSpec((1, 1, D_MODEL), lambda j, be, nu: (exp(j, be, nu), 0, 0))],
        out_specs=pl.BlockSpec(tiles, lambda j, be, nu: (j, 0)),
        scratch_shapes=[pltpu.VMEM((D_MODEL, 2 * D_FF), BF16), pltpu.VMEM((D_FF, D_MODEL), BF16)],
    )
    return pl.pallas_call(
        _expert_body,
        grid_spec=grid_spec,
        out_shape=jax.ShapeDtypeStruct((nrows * TOKEN_TILE_ROWS, LANES), F32),
        compiler_params=_params("arbitrary"),
        name="experts",
    )(block_e, nused, xs, w1, b1.reshape(N_EXPERTS, 1, 2 * D_FF), w2, b2.reshape(N_EXPERTS, 1, D_MODEL))


def _combine_body(tcnt_ref, tcar_ref, pstart_ref, loc_ref, gate_ref, x2_ref, ys_hbm, out_ref,
                  buf_ref, g_ref, sem, *, tm):
    tile = pl.program_id(0)

    def run_copy(local, first, size):
        return pltpu.make_async_copy(_tile_rows(ys_hbm, first, size), _tile_rows(buf_ref, local, size), sem)

    _for_each_run(tile, tcnt_ref, tcar_ref, pstart_ref, lambda l, f, s: run_copy(l, f, s).start())

    nloc = TOP_K * tm
    zpad = jnp.zeros((LANES - 16, LANES), F32)
    lane = lax.broadcasted_iota(I32, (LANES, nloc), 1).astype(F32)
    for c in range(tm // LANES):
        rows = slice(c * LANES, (c + 1) * LANES)
        cols = jnp.transpose(jnp.concatenate([loc_ref[:, rows].astype(F32), gate_ref[:, rows], zpad], axis=0))
        g = jnp.zeros((LANES, nloc), F32)
        for k in range(TOP_K):
            g = jnp.where(lane == cols[:, k:k + 1], cols[:, 8 + k:9 + k], g)
        g_ref[rows, :] = g.astype(BF16)

    _for_each_run(tile, tcnt_ref, tcar_ref, pstart_ref, lambda l, f, s: run_copy(l, f, s).wait())
    ys = _load_token_tiles(buf_ref, nloc).astype(BF16)
    out_ref[...] = x2_ref[...] + jnp.dot(g_ref[...], ys, preferred_element_type=F32)


def _stage_combine(tables, loc8, gate8, x2, ys, tm):
    n = x2.shape[0]
    grid_spec = pltpu.PrefetchScalarGridSpec(
        num_scalar_prefetch=3,
        grid=(n // tm,),
        in_specs=[pl.BlockSpec((8, tm), lambda i, *_: (0, i)),
                  pl.BlockSpec((8, tm), lambda i, *_: (0, i)),
                  pl.BlockSpec((tm, D_MODEL), lambda i, *_: (i, 0)),
                  pl.BlockSpec(memory_space=pl.ANY)],
        out_specs=pl.BlockSpec((tm, D_MODEL), lambda i, *_: (i, 0)),
        scratch_shapes=[pltpu.VMEM((TOP_K * tm * TOKEN_TILE_ROWS, LANES), F32),
                        pltpu.VMEM((tm, TOP_K * tm), BF16),
                        pltpu.SemaphoreType.DMA(())],
    )
    return pl.pallas_call(
        functools.partial(_combine_body, tm=tm),
        grid_spec=grid_spec,
        out_shape=jax.ShapeDtypeStruct((n, D_MODEL), F32),
        compiler_params=_params("arbitrary"),
        name="combine",
    )(*tables, loc8, gate8, x2, ys)


def _moe(x2, xn, loc8, gate8, tcnt, tcar, cnt, w1, b1, w2, b2, tm):
    n = x2.shape[0]
    ntile = n // tm
    nblk = -(-(n * TOP_K) // MOE_BLOCK) + N_EXPERTS
    blk8, pstart, zlo, zhi = _stage_offsets(cnt, nblk)
    per_tile = lambda a: a.reshape(N_EXPERTS, ntile, LANES)[:, :, 0].T
    tables = (per_tile(tcnt), per_tile(tcar), pstart[:, 0])
    xs = _stage_dispatch(tables + (zlo[:, 0], zhi[:, 0]), loc8, xn, nblk * MOE_BLOCK, tm)
    ys = _stage_experts(blk8[0, :nblk], blk8[1, :1], xs, w1, b1, w2, b2)
    return _stage_combine(tables, loc8, gate8, x2, ys, tm)


def kernel(x, norm1_g, w_in, mlstm_gate_b, mlstm_norm_g, attn_q_norm_g, attn_k_norm_g, w_mlstm_branch,
           w_attn_branch, w_out, norm2_g, w_router, b_router, w1, b1, w2, b2):
    batch, seq, _ = x.shape
    n = batch * seq
    for l in range(norm1_g.shape[0]):
        x2d = x.reshape(n, D_MODEL)
        tm = min(512, seq)
        mq, kT, mv, so, gi, gf, aq, ak, av, sgm, sga = _stage_inproj(
            x2d, norm1_g[l], w_in[l], mlstm_gate_b[l], batch, seq, tm)
        h_m = _stage_mlstm(mq, kT, mv, so, gi, gf, mlstm_norm_g[l], batch, seq, tm)
        attn = [_stage_attn(aq[g], ak[g], av[g], attn_q_norm_g[l, g], attn_k_norm_g[l, g], batch, seq, g)
                for g in range(N_GROUPS)]
        x2, xn, loc8, gate8, tcnt, tcar, cnt = _stage_merge(
            h_m, attn, sgm, sga, x2d, w_mlstm_branch[l], w_attn_branch[l], w_out[l], norm2_g[l],
            w_router[l], b_router[l], batch, seq, tm)
        out = _moe(x2, xn, loc8, gate8, tcnt, tcar, cnt, w1[l], b1[l], w2[l], b2[l], tm)
        x = out.reshape(batch, seq, D_MODEL)
    return x
```

```python
import functools

import numpy as np
import jax
import jax.numpy as jnp
from jax import lax
from jax.experimental import pallas as pl
from jax.experimental.pallas import tpu as pltpu

F32 = jnp.float32
BF16 = jnp.bfloat16
I32 = jnp.int32

D_MODEL = 1024
M_HEADS = 4
M_QK_DIM = 64
M_V_DIM = 128
GATE_SOFTCAP = 15.0
A_HEADS = 4
A_HEAD_DIM = 64
DILATED_PATTERNS = ((128, 1), (512, 4), (2048, 16))
N_GROUPS = len(DILATED_PATTERNS)
N_BACK = 128
N_EXPERTS = 32
TOP_K = 4
D_FF = 1024
SWIGLU_LIMIT = 7.0
SWIGLU_ALPHA = 1.702
MOE_BLOCK = 512
EPS = 1e-6

M_WIDTH = M_HEADS * M_V_DIM
M_QK_WIDTH = M_HEADS * M_QK_DIM
A_WIDTH = A_HEADS * A_HEAD_DIM
IN_SPLITS = (M_QK_WIDTH, M_QK_WIDTH, M_WIDTH, M_WIDTH, 2 * M_HEADS,
             N_GROUPS * A_WIDTH, N_GROUPS * A_WIDTH, N_GROUPS * A_WIDTH, D_MODEL, D_MODEL)

LANES = 128
VMEM_LIMIT = 56 * 1024 * 1024

_NT = (((1,), (1,)), ((), ()))


def _alibi_slopes():
    n = N_GROUPS * A_HEADS
    s = np.exp2(-8.0 * np.arange(1, n + 1) / n).astype(np.float32)
    return s.reshape(N_GROUPS, A_HEADS)


def _params(*sem):
    return pltpu.CompilerParams(dimension_semantics=sem, vmem_limit_bytes=VMEM_LIMIT)


def _log_sigmoid(x):
    return jnp.minimum(x, 0.0) - jnp.log1p(jnp.exp(-jnp.abs(x)))


_C_MQ = (0, 256)
_C_MV = (256, 768)
_C_MO = (768, 1280)
_C_AQ = (1280, 2048)
_C_AK = (2048, 2816)
_C_AV = (2816, 3584)
_C_GM = (3584, 4608)
_C_GA = (4608, 5632)
_W_MAIN = 5632
_WT_ROWS = M_QK_WIDTH + 16


def _split_residues(val, d, out_ref, st_ref):
    t = val.shape[0]
    if d == 1:
        out_ref[0, 0] = val.astype(out_ref.dtype)
        return
    st_ref[0] = val[:, :LANES]
    st_ref[1] = val[:, LANES:]
    for r in range(d):
        piece = jnp.concatenate([st_ref[0, pl.ds(r, t // d, stride=d), :],
                                 st_ref[1, pl.ds(r, t // d, stride=d), :]], axis=1)
        out_ref[0, r] = piece.astype(out_ref.dtype)


def _merge_residues(ref, d, st_ref):
    if d == 1:
        return ref[0, 0].astype(F32)
    m = ref.shape[2]
    for r in range(d):
        blk = ref[0, r].astype(F32)
        st_ref[0, pl.ds(r, m, stride=d), :] = blk[:, :LANES]
        st_ref[1, pl.ds(r, m, stride=d), :] = blk[:, LANES:]
    return jnp.concatenate([st_ref[0], st_ref[1]], axis=1)


def _inproj_body(x_ref, g1_ref, wm_ref, wt_ref, gb_ref,
                 mq_ref, kT_ref, mv_ref, so_ref, gi_ref, gf_ref,
                 q0_ref, q1_ref, q2_ref, k0_ref, k1_ref, k2_ref, v0_ref, v1_ref, v2_ref,
                 sgm_ref, sga_ref, st_ref):
    x = x_ref[...]
    h = x * lax.rsqrt(jnp.mean(x * x, axis=-1, keepdims=True) + EPS) * g1_ref[...]
    hb = h.astype(BF16)

    def seg(c):
        return jnp.dot(hb, wm_ref[:, c[0]:c[1]], preferred_element_type=F32)

    mq_ref[...] = seg(_C_MQ).astype(BF16)
    mv_ref[...] = seg(_C_MV).astype(BF16)
    so_ref[...] = jax.nn.sigmoid(seg(_C_MO)).astype(BF16)
    for c, refs in ((_C_AQ, (q0_ref, q1_ref, q2_ref)), (_C_AK, (k0_ref, k1_ref, k2_ref)),
                    (_C_AV, (v0_ref, v1_ref, v2_ref))):
        val = seg(c)
        for g, ref in enumerate(refs):
            _split_residues(val[:, g * A_WIDTH:(g + 1) * A_WIDTH], DILATED_PATTERNS[g][1], ref, st_ref)
    sgm_ref[...] = jax.nn.sigmoid(seg(_C_GM)).astype(BF16)
    sga_ref[...] = jax.nn.sigmoid(seg(_C_GA)).astype(BF16)

    t = lax.dot_general(wt_ref[...], hb, _NT, preferred_element_type=F32)
    kT_ref[...] = t[0:M_QK_WIDTH].astype(BF16)
    zi = t[M_QK_WIDTH:M_QK_WIDTH + 8] + gb_ref[0:8]
    zf = t[M_QK_WIDTH + 8:M_QK_WIDTH + 16] + gb_ref[8:16]
    gi_ref[...] = GATE_SOFTCAP * jnp.tanh(zi / GATE_SOFTCAP)
    gf_ref[...] = _log_sigmoid(GATE_SOFTCAP * jnp.tanh(zf / GATE_SOFTCAP))


def _stage_inproj(x2d, norm1_g, w_in, gate_b, batch, seq, tm):
    n = x2d.shape[0]
    steps = seq // tm
    cuts = np.concatenate([[0], np.cumsum(IN_SPLITS)])
    col = lambda i: w_in[:, cuts[i]:cuts[i + 1]]
    wm = jnp.concatenate([col(0), col(2), col(3), col(5), col(6), col(7), col(8), col(9)],
                         axis=1).astype(BF16)
    wif = col(4)
    z4 = jnp.zeros((4, D_MODEL), w_in.dtype)
    wt = jnp.concatenate([col(1).T, wif[:, :M_HEADS].T, z4, wif[:, M_HEADS:].T, z4], axis=0).astype(BF16)
    gb = jnp.zeros((16, 1), F32)
    gb = gb.at[0:4, 0].set(gate_b[:M_HEADS].astype(F32)).at[8:12, 0].set(gate_b[M_HEADS:].astype(F32))
    g1 = norm1_g.astype(F32).reshape(1, D_MODEL)

    row = lambda w: pl.BlockSpec((tm, w), lambda i: (i, 0))
    rowT = lambda r: pl.BlockSpec((r, tm), lambda i: (0, i))
    full = lambda a: pl.BlockSpec(a.shape, lambda i: (0,) * a.ndim)
    dils = [d for _, d in DILATED_PATTERNS]
    res_shape = lambda d: jax.ShapeDtypeStruct((batch, d, seq // d, A_WIDTH), BF16)
    res_spec = lambda d: pl.BlockSpec((1, d, tm // d, A_WIDTH), lambda i: (i // steps, 0, i % steps, 0))
    out_shapes = (
        jax.ShapeDtypeStruct((n, M_QK_WIDTH), BF16),
        jax.ShapeDtypeStruct((M_QK_WIDTH, n), BF16),
        jax.ShapeDtypeStruct((n, M_WIDTH), BF16),
        jax.ShapeDtypeStruct((n, M_WIDTH), BF16),
        jax.ShapeDtypeStruct((8, n), F32),
        jax.ShapeDtypeStruct((8, n), F32),
        *[res_shape(d) for d in dils], *[res_shape(d) for d in dils], *[res_shape(d) for d in dils],
        jax.ShapeDtypeStruct((n, D_MODEL), BF16),
        jax.ShapeDtypeStruct((n, D_MODEL), BF16),
    )
    out_specs = (row(M_QK_WIDTH), rowT(M_QK_WIDTH), row(M_WIDTH), row(M_WIDTH), rowT(8), rowT(8),
                 *[res_spec(d) for d in dils], *[res_spec(d) for d in dils], *[res_spec(d) for d in dils],
                 row(D_MODEL), row(D_MODEL))
    outs = pl.pallas_call(
        _inproj_body,
        grid=(n // tm,),
        in_specs=[row(D_MODEL), full(g1), full(wm), full(wt), full(gb)],
        out_specs=out_specs,
        out_shape=out_shapes,
        scratch_shapes=[pltpu.VMEM((2, tm, LANES), F32)],
        compiler_params=_params("parallel"),
        name="inproj",
    )(x2d, g1, wm, wt, gb)
    mq, kT, mv, so, gi, gf = outs[:6]
    aq, ak, av = outs[6:9], outs[9:12], outs[12:15]
    return mq, kT, mv, so, gi, gf, aq, ak, av, outs[15], outs[16]


M_CHUNK_LEN = 128


def _mlstm_body(q_ref, kT_ref, v_ref, so_ref, gi_ref, gf_ref, ng_ref, o_ref, c_ref, m_ref, *, nchunk):
    L = M_CHUNK_LEN

    @pl.when(pl.program_id(1) == 0)
    def _():
        c_ref[...] = jnp.zeros_like(c_ref)
        m_ref[...] = jnp.zeros_like(m_ref)

    lane8 = lax.broadcasted_iota(I32, (8, L), 1)
    causal = lax.broadcasted_iota(I32, (L, L), 1) <= lax.broadcasted_iota(I32, (L, L), 0)
    lo_half = lax.broadcasted_iota(I32, (L, LANES), 1) < M_QK_DIM
    ones = jnp.ones((L, M_V_DIM), BF16)

    for c in range(nchunk):
        rows = slice(c * L, (c + 1) * L)
        gi = gi_ref[:, rows]
        b = gf_ref[:, rows]
        sh = 1
        while sh < L:
            b = b + jnp.where(lane8 >= sh, pltpu.roll(b, sh, 1), 0.0)
            sh *= 2
        u = gi - b
        g = b[:, L - 1:L]
        a = g + u
        amax = jnp.max(a, axis=1, keepdims=True)
        m_prev = m_ref[:, 0:1]
        m_new = jnp.maximum(g + m_prev, amax)
        w = jnp.exp(a - m_new) * (M_QK_DIM ** -0.5)
        s_old = jnp.exp(g + m_prev - m_new)

        for p in range(M_HEADS // 2):
            lanes_p = slice(p * LANES, (p + 1) * LANES)
            q_pair = q_ref[rows, lanes_p]
            kT_pair = kT_ref[lanes_p, rows]
            c_pair = c_ref[lanes_p, :].astype(BF16)
            for hh in range(2):
                h = 2 * p + hh
                hl = slice(h * M_V_DIM, (h + 1) * M_V_DIM)
                qm = jnp.where(lo_half if hh == 0 else jnp.logical_not(lo_half), q_pair, jnp.zeros_like(q_pair))
                bcol = jnp.transpose(jnp.broadcast_to(b[h:h + 1, :], (L, L)))
                dm = jnp.where(causal, bcol + u[h:h + 1, :], -jnp.inf)
                inter = bcol + m_prev[h:h + 1, :]
                m_t = jnp.maximum(inter, jnp.max(dm, axis=1, keepdims=True))
                s = jnp.dot(qm, kT_pair, preferred_element_type=F32) * (M_QK_DIM ** -0.5)
                pmat = (s * jnp.exp(dm - m_t)).astype(BF16)
                vext = jnp.concatenate([v_ref[rows, hl], ones], axis=1)
                sc = jnp.exp(inter - m_t)
                out = (jnp.dot(pmat, vext, preferred_element_type=F32)
                       + jnp.concatenate([sc, sc], axis=1) * jnp.dot(qm, c_pair, preferred_element_type=F32))
                hv = out[:, :M_V_DIM] / jnp.maximum(jnp.abs(out[:, M_V_DIM:]), jnp.exp(-m_t))
                hn = hv * lax.rsqrt(jnp.mean(hv * hv, axis=1, keepdims=True) + EPS)
                hn = hn * ng_ref[:, hl] * so_ref[rows, hl].astype(F32)
                o_ref[rows, hl] = hn.astype(BF16)
                hr = slice(h * M_QK_DIM, (h + 1) * M_QK_DIM)
                kw = (kT_ref[hr, rows].astype(F32) * w[h:h + 1, :]).astype(BF16)
                c_ref[hr, :] = s_old[h:h + 1, :] * c_ref[hr, :] + jnp.dot(kw, vext, preferred_element_type=F32)
        m_ref[...] = jnp.broadcast_to(m_new, m_ref.shape)


def _stage_mlstm(mq, kT, mv, so, gi, gf, norm_g, batch, seq, rows_per_step):
    n = batch * seq
    R = rows_per_step
    steps = seq // R
    ng = norm_g.astype(F32).reshape(1, M_WIDTH)
    row = lambda w: pl.BlockSpec((R, w), lambda b, i: (b * steps + i, 0))
    rowT = lambda r: pl.BlockSpec((r, R), lambda b, i: (0, b * steps + i))
    return pl.pallas_call(
        functools.partial(_mlstm_body, nchunk=R // M_CHUNK_LEN),
        grid=(batch, steps),
        in_specs=[row(M_QK_WIDTH), rowT(M_QK_WIDTH), row(M_WIDTH), row(M_WIDTH), rowT(8), rowT(8),
                  pl.BlockSpec((1, M_WIDTH), lambda b, i: (0, 0))],
        out_specs=row(M_WIDTH),
        out_shape=jax.ShapeDtypeStruct((n, M_WIDTH), BF16),
        scratch_shapes=[pltpu.VMEM((M_QK_WIDTH, 2 * M_V_DIM), F32), pltpu.VMEM((8, LANES), F32)],
        compiler_params=_params("parallel", "arbitrary"),
        name="mlstm",
    )(mq, kT, mv, so, gi, gf, ng)


def _attn_body(q_ref, kp_ref, kc_ref, vp_ref, vc_ref, gq_ref, gk_ref, o_ref, lse_ref, *, dil, slopes, lq):
    QB = N_BACK
    first = pl.program_id(2) == 0
    hid_r = lax.broadcasted_iota(I32, (A_WIDTH, A_WIDTH), 0) // A_HEAD_DIM
    hid_c = lax.broadcasted_iota(I32, (A_WIDTH, A_WIDTH), 1) // A_HEAD_DIM
    head_ones = (hid_r == hid_c).astype(BF16)

    def head_norm(xb, gain):
        xf = xb.astype(F32)
        ss = jnp.dot((xf * xf).astype(BF16), head_ones, preferred_element_type=F32)
        return xf * lax.rsqrt(ss * (1.0 / A_HEAD_DIM) + EPS) * gain

    qn = (head_norm(q_ref[0, 0], gq_ref[...]) * (A_HEAD_DIM ** -0.5)).astype(BF16)
    kcn = head_norm(kc_ref[0, 0], gk_ref[...]).astype(BF16)
    kpn = head_norm(kp_ref[0, 0], gk_ref[...]).astype(BF16)
    vc = vc_ref[0, 0]
    vp = vp_ref[0, 0]

    qi = lax.broadcasted_iota(I32, (QB, 2 * QB), 0)
    kj = lax.broadcasted_iota(I32, (QB, 2 * QB), 1)
    dist = qi + QB - kj
    band = jnp.logical_and(dist >= 0, dist <= N_BACK)
    band_first = jnp.logical_and(band, jnp.logical_or(kj >= QB, jnp.logical_not(first)))
    distf = (dist * dil).astype(F32)
    lo_half = lax.broadcasted_iota(I32, (QB, LANES), 1) < A_HEAD_DIM
    ones = jnp.ones((2 * QB, LANES), BF16)

    for j in range(lq // QB):
        rows = slice(j * QB, (j + 1) * QB)
        prow = slice((j - 1) * QB, j * QB)
        keys = jnp.concatenate([kpn if j == 0 else kcn[prow], kcn[rows]], axis=0)
        vals = jnp.concatenate([vp if j == 0 else vc[prow], vc[rows]], axis=0)
        mask = band_first if j == 0 else band
        for p in range(A_HEADS // 2):
            lanes_p = slice(p * LANES, (p + 1) * LANES)
            q_pair = qn[rows, lanes_p]
            k_pair = keys[:, lanes_p]
            vext = jnp.concatenate([vals[:, lanes_p], ones], axis=1)
            o_pair = None
            l_pair = None
            for hh in range(2):
                h = 2 * p + hh
                sel = lo_half if hh == 0 else jnp.logical_not(lo_half)
                qm = jnp.where(sel, q_pair, jnp.zeros_like(q_pair))
                s = lax.dot_general(qm, k_pair, _NT, preferred_element_type=F32)
                s = jnp.where(mask, s - float(slopes[h]) * distf, -jnp.inf)
                m = jnp.max(s, axis=1, keepdims=True)
                pv = jnp.dot(jnp.exp(s - m).astype(BF16), vext, preferred_element_type=F32)
                den = pv[:, LANES:]
                o_h = pv[:, :LANES] / den
                l_h = m + jnp.log(den)
                o_pair = o_h if hh == 0 else jnp.where(lo_half, o_pair, o_h)
                l_pair = l_h if hh == 0 else jnp.where(lo_half, l_pair, l_h)
            o_ref[0, 0, rows, lanes_p] = o_pair.astype(BF16)
            lse_ref[0, 0, rows, lanes_p] = l_pair


def _stage_attn(aq, ak, av, gq, gk, batch, seq, group):
    _, dil = DILATED_PATTERNS[group]
    L = seq // dil
    assert L % N_BACK == 0
    lq = min(512, L)
    nq = L // lq
    sub = lq // N_BACK
    gq_t = jnp.tile(gq.astype(F32), A_HEADS).reshape(1, A_WIDTH)
    gk_t = jnp.tile(gk.astype(F32), A_HEADS).reshape(1, A_WIDTH)
    cur = pl.BlockSpec((1, 1, lq, A_WIDTH), lambda b, r, i: (b, r, i, 0))
    prev = pl.BlockSpec((1, 1, N_BACK, A_WIDTH), lambda b, r, i: (b, r, jnp.maximum(i * sub - 1, 0), 0))
    gain = pl.BlockSpec((1, A_WIDTH), lambda b, r, i: (0, 0))
    return pl.pallas_call(
        functools.partial(_attn_body, dil=dil, slopes=tuple(_alibi_slopes()[group]), lq=lq),
        grid=(batch, dil, nq),
        in_specs=[cur, prev, cur, prev, cur, gain, gain],
        out_specs=(cur, cur),
        out_shape=(jax.ShapeDtypeStruct((batch, dil, L, A_WIDTH), BF16),
                   jax.ShapeDtypeStruct((batch, dil, L, A_WIDTH), F32)),
        compiler_params=_params("parallel", "parallel", "parallel"),
        name=f"dilated_attn_d{dil}",
    )(aq, ak, ak, av, av, gq_t, gk_t)


TOKEN_TILE_ROWS = D_MODEL // LANES


def _store_token_tiles(ref, val):
    t = val.shape[0]
    for s in range(TOKEN_TILE_ROWS):
        ref[pl.ds(s, t, stride=TOKEN_TILE_ROWS), :] = val[:, s * LANES:(s + 1) * LANES]


def _load_token_tiles(ref, t):
    return jnp.concatenate([ref[pl.ds(s, t, stride=TOKEN_TILE_ROWS), :] for s in range(TOKEN_TILE_ROWS)], axis=1)


def _token_tile(ref, tok):
    start = tok * TOKEN_TILE_ROWS
    if not isinstance(tok, int):
        start = pl.multiple_of(start, TOKEN_TILE_ROWS)
    return ref.at[pl.ds(start, TOKEN_TILE_ROWS)]


def _rows8(vals):
    t = vals[0].shape[1]
    rid = lax.broadcasted_iota(I32, (8, t), 0)
    out = jnp.zeros((8, t), vals[0].dtype)
    for k, v in enumerate(vals):
        out = jnp.where(rid == k, jnp.broadcast_to(v, (8, t)), out)
    return out


def _merge_body(hm_ref, o1_ref, o2_ref, o3_ref, l1_ref, l2_ref, l3_ref, sgm_ref, sga_ref, x_ref,
                wm_ref, wa_ref, wo_ref, g2_ref, wrT_ref, br_ref,
                x2_ref, xn_ref, loc_ref, gate_ref, tcnt_ref, tcar_ref, cnt_ref, carry_ref, st_ref):
    @pl.when(pl.program_id(0) == 0)
    def _():
        carry_ref[...] = jnp.zeros_like(carry_ref)

    dils = [d for _, d in DILATED_PATTERNS]
    l1, l2, l3 = [_merge_residues(r, d, st_ref) for r, d in zip((l1_ref, l2_ref, l3_ref), dils)]
    lmax = jnp.maximum(jnp.maximum(l1, l2), l3)
    e1, e2, e3 = jnp.exp(l1 - lmax), jnp.exp(l2 - lmax), jnp.exp(l3 - lmax)
    num = e1 * _merge_residues(o1_ref, dils[0], st_ref)
    num = num + e2 * _merge_residues(o2_ref, dils[1], st_ref)
    num = num + e3 * _merge_residues(o3_ref, dils[2], st_ref)
    h_a = num / (e1 + e2 + e3)
    y = (sgm_ref[...].astype(F32) * jnp.dot(hm_ref[...], wm_ref[...], preferred_element_type=F32)
         + sga_ref[...].astype(F32) * jnp.dot(h_a.astype(BF16), wa_ref[...], preferred_element_type=F32))
    x2 = x_ref[...] + jnp.dot(y.astype(BF16), wo_ref[...], preferred_element_type=F32)
    x2_ref[...] = x2
    xn = x2 * lax.rsqrt(jnp.mean(x2 * x2, axis=-1, keepdims=True) + EPS) * g2_ref[...]
    xn_ref[...] = xn.astype(BF16)

    logits = lax.dot_general(wrT_ref[...], xn, _NT, precision=lax.Precision.HIGHEST,
                             preferred_element_type=F32) + br_ref[...]
    t = logits.shape[1]
    eid = lax.broadcasted_iota(I32, (N_EXPERTS, t), 0).astype(F32)
    vals = logits
    top_v, top_i = [], []
    for _ in range(TOP_K):
        mx = jnp.max(vals, axis=0, keepdims=True)
        ik = jnp.min(jnp.where(vals == mx, eid, float(N_EXPERTS)), axis=0, keepdims=True)
        top_v.append(mx)
        top_i.append(ik)
        vals = jnp.where(eid == ik, -jnp.inf, vals)
    ex = [jnp.exp(v - top_v[0]) for v in top_v]
    den = ex[0] + ex[1] + ex[2] + ex[3]
    gate_ref[...] = _rows8([e / den for e in ex])

    chosen = jnp.zeros((N_EXPERTS, t), F32)
    for ik in top_i:
        chosen = chosen + (eid == ik).astype(F32)
    before = (lax.broadcasted_iota(I32, (t, t), 0) < lax.broadcasted_iota(I32, (t, t), 1)).astype(BF16)
    prefix = jnp.dot(chosen.astype(BF16), before, preferred_element_type=F32)
    tcount = jnp.broadcast_to(jnp.sum(chosen, axis=1, keepdims=True), (N_EXPERTS, LANES))
    below = (lax.broadcasted_iota(I32, (N_EXPERTS, N_EXPERTS), 1)
             < lax.broadcasted_iota(I32, (N_EXPERTS, N_EXPERTS), 0)).astype(F32)
    tile_off = jnp.dot(below, tcount, precision=lax.Precision.HIGHEST, preferred_element_type=F32)
    pos = prefix + tile_off[:, 0:1]
    loc_ref[...] = _rows8([jnp.sum(jnp.where(eid == ik, pos, 0.0), axis=0, keepdims=True).astype(I32)
                           for ik in top_i])
    carry = carry_ref[...]
    tcnt_ref[...] = tcount.astype(I32)
    tcar_ref[...] = carry.astype(I32)
    total = carry + tcount
    carry_ref[...] = total
    cnt_ref[...] = total


def _stage_merge(h_m, attn, sgm, sga, x2d, w_mb, w_ab, w_out, norm2_g, w_router, b_router, batch, seq, tm):
    n = x2d.shape[0]
    steps = seq // tm
    (o1, l1), (o2, l2), (o3, l3) = attn
    wm = w_mb.astype(BF16)
    wa = w_ab.astype(BF16)
    wo = w_out.astype(BF16)
    g2 = norm2_g.astype(F32).reshape(1, D_MODEL)
    wrT = w_router.astype(F32).T
    br = b_router.astype(F32).reshape(N_EXPERTS, 1)
    row = lambda w: pl.BlockSpec((tm, w), lambda i: (i, 0))
    rowT = lambda r: pl.BlockSpec((r, tm), lambda i: (0, i))
    full = lambda a: pl.BlockSpec(a.shape, lambda i: (0,) * a.ndim)
    res = lambda d: pl.BlockSpec((1, d, tm // d, A_WIDTH), lambda i: (i // steps, 0, i % steps, 0))
    dils = [d for _, d in DILATED_PATTERNS]
    per_tile = pl.BlockSpec((N_EXPERTS, LANES), lambda i: (0, i))
    return pl.pallas_call(
        _merge_body,
        grid=(n // tm,),
        in_specs=[row(M_WIDTH), *[res(d) for d in dils], *[res(d) for d in dils],
                  row(D_MODEL), row(D_MODEL), row(D_MODEL),
                  full(wm), full(wa), full(wo), full(g2), full(wrT), full(br)],
        out_specs=(row(D_MODEL), row(D_MODEL), rowT(8), rowT(8), per_tile, per_tile,
                   pl.BlockSpec((N_EXPERTS, LANES), lambda i: (0, 0))),
        out_shape=(jax.ShapeDtypeStruct((n, D_MODEL), F32),
                   jax.ShapeDtypeStruct((n, D_MODEL), BF16),
                   jax.ShapeDtypeStruct((8, n), I32),
                   jax.ShapeDtypeStruct((8, n), F32),
                   jax.ShapeDtypeStruct((N_EXPERTS, (n // tm) * LANES), I32),
                   jax.ShapeDtypeStruct((N_EXPERTS, (n // tm) * LANES), I32),
                   jax.ShapeDtypeStruct((N_EXPERTS, LANES), F32)),
        scratch_shapes=[pltpu.VMEM((N_EXPERTS, LANES), F32), pltpu.VMEM((2, tm, LANES), F32)],
        compiler_params=_params("arbitrary"),
        name="merge_route",
    )(h_m, o1, o2, o3, l1, l2, l3, sgm, sga, x2d, wm, wa, wo, g2, wrT, br)


def _offsets_body(cnt_ref, blk_ref, pstart_ref, zlo_ref, zhi_ref, *, nblk_pad):
    cnt = cnt_ref[...]
    padded = jnp.floor((cnt + (MOE_BLOCK - 1)) * (1.0 / MOE_BLOCK)) * MOE_BLOCK
    lower = (lax.broadcasted_iota(I32, (N_EXPERTS, N_EXPERTS), 1)
             <= lax.broadcasted_iota(I32, (N_EXPERTS, N_EXPERTS), 0)).astype(F32)
    pends = jnp.dot(lower, padded, precision=lax.Precision.HIGHEST, preferred_element_type=F32)
    pstart = pends - padded
    pstart_ref[...] = pstart.astype(I32)
    zlo_ref[...] = (pstart + cnt).astype(I32)
    zhi_ref[...] = pends.astype(I32)

    first_row = (lax.broadcasted_iota(I32, (N_EXPERTS, nblk_pad), 1) * MOE_BLOCK).astype(F32)
    pe = jnp.broadcast_to(pends[:, 0:1], (N_EXPERTS, nblk_pad))
    be = jnp.sum((pe <= first_row).astype(F32), axis=0, keepdims=True)
    be = jnp.minimum(be, float(N_EXPERTS - 1))
    nused = pends[N_EXPERTS - 1:N_EXPERTS, 0:1] * (1.0 / MOE_BLOCK)
    blk_ref[...] = _rows8([be.astype(I32), jnp.broadcast_to(nused, (1, nblk_pad)).astype(I32)])


def _stage_offsets(cnt, nblk):
    nblk_pad = -(-nblk // LANES) * LANES
    const = lambda r, c: pl.BlockSpec((r, c), lambda i: (0, 0))
    per_expert = jax.ShapeDtypeStruct((N_EXPERTS, LANES), I32)
    return pl.pallas_call(
        functools.partial(_offsets_body, nblk_pad=nblk_pad),
        grid=(1,),
        in_specs=[const(N_EXPERTS, LANES)],
        out_specs=(const(8, nblk_pad), const(N_EXPERTS, LANES), const(N_EXPERTS, LANES), const(N_EXPERTS, LANES)),
        out_shape=(jax.ShapeDtypeStruct((8, nblk_pad), I32), per_expert, per_expert, per_expert),
        compiler_params=_params("arbitrary"),
        name="route_offsets",
    )(cnt)


RUN_BITS = 10


def _tile_rows(ref, first_row, nrows):
    start = first_row * TOKEN_TILE_ROWS
    if not isinstance(first_row, int):
        start = pl.multiple_of(start, TOKEN_TILE_ROWS)
    return ref.at[pl.ds(start, nrows * TOKEN_TILE_ROWS)]


def _for_each_piece(length, fn):
    for b in reversed(range(RUN_BITS)):
        @pl.when(((length >> b) & 1) == 1)
        def _(b=b):
            fn((length >> (b + 1)) << (b + 1), 1 << b)


def _for_each_run(tile, tcnt_ref, tcar_ref, pstart_ref, fn):
    def per_expert(e, local):
        count = tcnt_ref[tile, e]
        first = pstart_ref[e] + tcar_ref[tile, e]
        _for_each_piece(count, lambda off, size: fn(local + off, first + off, size))
        return local + count

    lax.fori_loop(0, N_EXPERTS, per_expert, 0)


PERM_CHUNK = 256


def _dispatch_body(tcnt_ref, tcar_ref, pstart_ref, zlo_ref, zhi_ref, loc_ref, xn_ref, xs_hbm,
                   buf_ref, sem, *, tm):
    tile = pl.program_id(0)
    loc = loc_ref[...]
    xn = xn_ref[...]
    for c in range(TOP_K * tm // PERM_CHUNK):
        lid = lax.broadcasted_iota(I32, (PERM_CHUNK, tm), 0) + c * PERM_CHUNK
        hit = lid == loc[0:1, :]
        for k in range(1, TOP_K):
            hit = jnp.logical_or(hit, lid == loc[k:k + 1, :])
        rows = jnp.dot(jnp.where(hit, 1.0, 0.0).astype(BF16), xn, preferred_element_type=F32)
        _store_token_tiles(buf_ref.at[pl.ds(c * PERM_CHUNK * TOKEN_TILE_ROWS, PERM_CHUNK * TOKEN_TILE_ROWS)], rows)

    def run_copy(local, first, size):
        return pltpu.make_async_copy(_tile_rows(buf_ref, local, size), _tile_rows(xs_hbm, first, size), sem)

    _for_each_run(tile, tcnt_ref, tcar_ref, pstart_ref, lambda l, f, s: run_copy(l, f, s).start())
    _for_each_run(tile, tcnt_ref, tcar_ref, pstart_ref, lambda l, f, s: run_copy(l, f, s).wait())

    @pl.when(tile == pl.num_programs(0) - 1)
    def _():
        buf_ref[pl.ds(0, MOE_BLOCK * TOKEN_TILE_ROWS), :] = jnp.zeros((MOE_BLOCK * TOKEN_TILE_ROWS, LANES), F32)

        def zero_copy(first, size):
            return pltpu.make_async_copy(_tile_rows(buf_ref, 0, size), _tile_rows(xs_hbm, first, size), sem)

        def per_expert(e, carry):
            lo = zlo_ref[e]
            npad = zhi_ref[e] - lo
            _for_each_piece(npad, lambda off, size: zero_copy(lo + off, size).start())
            _for_each_piece(npad, lambda off, size: zero_copy(lo + off, size).wait())
            return carry

        lax.fori_loop(0, N_EXPERTS, per_expert, 0)

        first_unused = zhi_ref[N_EXPERTS - 1] // MOE_BLOCK
        nblk = xs_hbm.shape[0] // (MOE_BLOCK * TOKEN_TILE_ROWS)

        def tail(blk, carry):
            zero_copy(blk * MOE_BLOCK, MOE_BLOCK).start()
            zero_copy(blk * MOE_BLOCK, MOE_BLOCK).wait()
            return carry

        lax.fori_loop(first_unused, nblk, tail, 0)


def _stage_dispatch(tables, loc8, xn, nrows, tm):
    n = xn.shape[0]
    assert TOP_K * tm >= MOE_BLOCK
    grid_spec = pltpu.PrefetchScalarGridSpec(
        num_scalar_prefetch=5,
        grid=(n // tm,),
        in_specs=[pl.BlockSpec((8, tm), lambda i, *_: (0, i)),
                  pl.BlockSpec((tm, D_MODEL), lambda i, *_: (i, 0))],
        out_specs=pl.BlockSpec(memory_space=pl.ANY),
        scratch_shapes=[pltpu.VMEM((TOP_K * tm * TOKEN_TILE_ROWS, LANES), F32), pltpu.SemaphoreType.DMA(())],
    )
    return pl.pallas_call(
        functools.partial(_dispatch_body, tm=tm),
        grid_spec=grid_spec,
        out_shape=jax.ShapeDtypeStruct((nrows * TOKEN_TILE_ROWS, LANES), F32),
        compiler_params=_params("arbitrary"),
        name="dispatch",
    )(*tables, loc8, xn)


def _expert_body(be_ref, nu_ref, xs_ref, w1_ref, b1_ref, w2_ref, b2_ref, ys_ref, w1b_ref, w2b_ref):
    j = pl.program_id(0)
    used = j < nu_ref[0]
    jj = jnp.minimum(j, nu_ref[0] - 1)
    fresh = jnp.logical_or(j == 0, be_ref[jj] != be_ref[jnp.maximum(jj - 1, 0)])

    @pl.when(jnp.logical_and(used, fresh))
    def _():
        w1b_ref[...] = w1_ref[0].astype(BF16)
        w2b_ref[...] = w2_ref[0].astype(BF16)

    @pl.when(used)
    def _():
        xb = _load_token_tiles(xs_ref, MOE_BLOCK).astype(BF16)
        gu = jnp.dot(xb, w1b_ref[...], preferred_element_type=F32) + b1_ref[0]
        gate = jnp.minimum(gu[:, :D_FF], SWIGLU_LIMIT)
        lin = jnp.clip(gu[:, D_FF:], -SWIGLU_LIMIT, SWIGLU_LIMIT)
        act = (lin + 1.0) * (gate * jax.nn.sigmoid(SWIGLU_ALPHA * gate))
        ys = jnp.dot(act.astype(BF16), w2b_ref[...], preferred_element_type=F32) + b2_ref[0]
        _store_token_tiles(ys_ref, ys)

    @pl.when(jnp.logical_not(used))
    def _():
        ys_ref[...] = jnp.zeros_like(ys_ref)


def _stage_experts(block_e, nused, xs, w1, b1, w2, b2):
    nrows = xs.shape[0] // TOKEN_TILE_ROWS
    nblk = nrows // MOE_BLOCK
    blk = lambda j, be, nu: jnp.maximum(jnp.minimum(j, nu[0] - 1), 0)
    exp = lambda j, be, nu: be[blk(j, be, nu)]
    tiles = (MOE_BLOCK * TOKEN_TILE_ROWS, LANES)
    grid_spec = pltpu.PrefetchScalarGridSpec(
        num_scalar_prefetch=2,
        grid=(nblk,),
        in_specs=[pl.BlockSpec(tiles, lambda j, be, nu: (blk(j, be, nu), 0)),
                  pl.BlockSpec((1, D_MODEL, 2 * D_FF), lambda j, be, nu: (exp(j, be, nu), 0, 0)),
                  pl.BlockSpec((1, 1, 2 * D_FF), lambda j, be, nu: (exp(j, be, nu), 0, 0)),
                  pl.BlockSpec((1, D_FF, D_MODEL), lambda j, be, nu: (exp(j, be, nu), 0, 0)),
                  pl.BlockSpec((1, 1, D_MODEL), lambda j, be, nu: (exp(j, be, nu), 0, 0))],
        out_specs=pl.BlockSpec(tiles, lambda j, be, nu: (j, 0)),
        scratch_shapes=[pltpu.VMEM((D_MODEL, 2 * D_FF), BF16), pltpu.VMEM((D_FF, D_MODEL), BF16)],
    )
    return pl.pallas_call(
        _expert_body,
        grid_spec=grid_spec,
        out_shape=jax.ShapeDtypeStruct((nrows * TOKEN_TILE_ROWS, LANES), F32),
        compiler_params=_params("arbitrary"),
        name="experts",
    )(block_e, nused, xs, w1, b1.reshape(N_EXPERTS, 1, 2 * D_FF), w2, b2.reshape(N_EXPERTS, 1, D_MODEL))


def _combine_body(tcnt_ref, tcar_ref, pstart_ref, loc_ref, gate_ref, x2_ref, ys_hbm, out_ref,
                  buf_ref, g_ref, sem, *, tm):
    tile = pl.program_id(0)

    def run_copy(local, first, size):
        return pltpu.make_async_copy(_tile_rows(ys_hbm, first, size), _tile_rows(buf_ref, local, size), sem)

    _for_each_run(tile, tcnt_ref, tcar_ref, pstart_ref, lambda l, f, s: run_copy(l, f, s).start())

    nloc = TOP_K * tm
    zpad = jnp.zeros((LANES - 16, LANES), F32)
    lane = lax.broadcasted_iota(I32, (LANES, nloc), 1).astype(F32)
    for c in range(tm // LANES):
        rows = slice(c * LANES, (c + 1) * LANES)
        cols = jnp.transpose(jnp.concatenate([loc_ref[:, rows].astype(F32), gate_ref[:, rows], zpad], axis=0))
        g = jnp.zeros((LANES, nloc), F32)
        for k in range(TOP_K):
            g = jnp.where(lane == cols[:, k:k + 1], cols[:, 8 + k:9 + k], g)
        g_ref[rows, :] = g.astype(BF16)

    _for_each_run(tile, tcnt_ref, tcar_ref, pstart_ref, lambda l, f, s: run_copy(l, f, s).wait())
    ys = _load_token_tiles(buf_ref, nloc).astype(BF16)
    out_ref[...] = x2_ref[...] + jnp.dot(g_ref[...], ys, preferred_element_type=F32)


def _stage_combine(tables, loc8, gate8, x2, ys, tm):
    n = x2.shape[0]
    grid_spec = pltpu.PrefetchScalarGridSpec(
        num_scalar_prefetch=3,
        grid=(n // tm,),
        in_specs=[pl.BlockSpec((8, tm), lambda i, *_: (0, i)),
                  pl.BlockSpec((8, tm), lambda i, *_: (0, i)),
                  pl.BlockSpec((tm, D_MODEL), lambda i, *_: (i, 0)),
                  pl.BlockSpec(memory_space=pl.ANY)],
        out_specs=pl.BlockSpec((tm, D_MODEL), lambda i, *_: (i, 0)),
        scratch_shapes=[pltpu.VMEM((TOP_K * tm * TOKEN_TILE_ROWS, LANES), F32),
                        pltpu.VMEM((tm, TOP_K * tm), BF16),
                        pltpu.SemaphoreType.DMA(())],
    )
    return pl.pallas_call(
        functools.partial(_combine_body, tm=tm),
        grid_spec=grid_spec,
        out_shape=jax.ShapeDtypeStruct((n, D_MODEL), F32),
        compiler_params=_params("arbitrary"),
        name="combine",
    )(*tables, loc8, gate8, x2, ys)


def _moe(x2, xn, loc8, gate8, tcnt, tcar, cnt, w1, b1, w2, b2, tm):
    n = x2.shape[0]
    ntile = n // tm
    nblk = -(-(n * TOP_K) // MOE_BLOCK) + N_EXPERTS
    blk8, pstart, zlo, zhi = _stage_offsets(cnt, nblk)
    per_tile = lambda a: a.reshape(N_EXPERTS, ntile, LANES)[:, :, 0].T
    tables = (per_tile(tcnt), per_tile(tcar), pstart[:, 0])
    xs = _stage_dispatch(tables + (zlo[:, 0], zhi[:, 0]), loc8, xn, nblk * MOE_BLOCK, tm)
    ys = _stage_experts(blk8[0, :nblk], blk8[1, :1], xs, w1, b1, w2, b2)
    return _stage_combine(tables, loc8, gate8, x2, ys, tm)


def kernel(x, norm1_g, w_in, mlstm_gate_b, mlstm_norm_g, attn_q_norm_g, attn_k_norm_g, w_mlstm_branch,
           w_attn_branch, w_out, norm2_g, w_router, b_router, w1, b1, w2, b2):
    batch, seq, _ = x.shape
    n = batch * seq
    for l in range(norm1_g.shape[0]):
        x2d = x.reshape(n, D_MODEL)
        tm = min(512, seq)
        mq, kT, mv, so, gi, gf, aq, ak, av, sgm, sga = _stage_inproj(
            x2d, norm1_g[l], w_in[l], mlstm_gate_b[l], batch, seq, tm)
        h_m = _stage_mlstm(mq, kT, mv, so, gi, gf, mlstm_norm_g[l], batch, seq, tm)
        attn = [_stage_attn(aq[g], ak[g], av[g], attn_q_norm_g[l, g], attn_k_norm_g[l, g], batch, seq, g)
                for g in range(N_GROUPS)]
        x2, xn, loc8, gate8, tcnt, tcar, cnt = _stage_merge(
            h_m, attn, sgm, sga, x2d, w_mlstm_branch[l], w_attn_branch[l], w_out[l], norm2_g[l],
            w_router[l], b_router[l], batch, seq, tm)
        out = _moe(x2, xn, loc8, gate8, tcnt, tcar, cnt, w1[l], b1[l], w2[l], b2[l], tm)
        x = out.reshape(batch, seq, D_MODEL)
    return x
```

```python
import functools

import numpy as np
import jax
import jax.numpy as jnp
from jax import lax
from jax.experimental import pallas as pl
from jax.experimental.pallas import tpu as pltpu

F32 = jnp.float32
BF16 = jnp.bfloat16
I32 = jnp.int32

D_MODEL = 1024
M_HEADS = 4
M_QK_DIM = 64
M_V_DIM = 128
GATE_SOFTCAP = 15.0
A_HEADS = 4
A_HEAD_DIM = 64
DILATED_PATTERNS = ((128, 1), (512, 4), (2048, 16))
N_GROUPS = len(DILATED_PATTERNS)
N_BACK = 128
N_EXPERTS = 32
TOP_K = 4
D_FF = 1024
SWIGLU_LIMIT = 7.0
SWIGLU_ALPHA = 1.702
MOE_BLOCK = 512
EPS = 1e-6

M_WIDTH = M_HEADS * M_V_DIM
M_QK_WIDTH = M_HEADS * M_QK_DIM
A_WIDTH = A_HEADS * A_HEAD_DIM
IN_SPLITS = (M_QK_WIDTH, M_QK_WIDTH, M_WIDTH, M_WIDTH, 2 * M_HEADS,
             N_GROUPS * A_WIDTH, N_GROUPS * A_WIDTH, N_GROUPS * A_WIDTH, D_MODEL, D_MODEL)

LANES = 128
VMEM_LIMIT = 56 * 1024 * 1024

_NT = (((1,), (1,)), ((), ()))


def _alibi_slopes():
    n = N_GROUPS * A_HEADS
    s = np.exp2(-8.0 * np.arange(1, n + 1) / n).astype(np.float32)
    return s.reshape(N_GROUPS, A_HEADS)


def _params(*sem):
    return pltpu.CompilerParams(dimension_semantics=sem, vmem_limit_bytes=VMEM_LIMIT)


def _log_sigmoid(x):
    return jnp.minimum(x, 0.0) - jnp.log1p(jnp.exp(-jnp.abs(x)))


_C_MQ = (0, 256)
_C_MV = (256, 768)
_C_MO = (768, 1280)
_C_AQ = (1280, 2048)
_C_AK = (2048, 2816)
_C_AV = (2816, 3584)
_C_GM = (3584, 4608)
_C_GA = (4608, 5632)
_W_MAIN = 5632
_WT_ROWS = M_QK_WIDTH + 16


def _split_residues(val, d, out_ref, st_ref):
    t = val.shape[0]
    if d == 1:
        out_ref[0, 0] = val.astype(out_ref.dtype)
        return
    st_ref[0] = val[:, :LANES]
    st_ref[1] = val[:, LANES:]
    for r in range(d):
        piece = jnp.concatenate([st_ref[0, pl.ds(r, t // d, stride=d), :],
                                 st_ref[1, pl.ds(r, t // d, stride=d), :]], axis=1)
        out_ref[0, r] = piece.astype(out_ref.dtype)


def _merge_residues(ref, d, st_ref):
    if d == 1:
        return ref[0, 0].astype(F32)
    m = ref.shape[2]
    for r in range(d):
        blk = ref[0, r].astype(F32)
        st_ref[0, pl.ds(r, m, stride=d), :] = blk[:, :LANES]
        st_ref[1, pl.ds(r, m, stride=d), :] = blk[:, LANES:]
    return jnp.concatenate([st_ref[0], st_ref[1]], axis=1)


def _inproj_body(x_ref, g1_ref, wm_ref, wt_ref, gb_ref,
                 mq_ref, kT_ref, mv_ref, so_ref, gi_ref, gf_ref,
                 q0_ref, q1_ref, q2_ref, k0_ref, k1_ref, k2_ref, v0_ref, v1_ref, v2_ref,
                 sgm_ref, sga_ref, st_ref):
    x = x_ref[...]
    h = x * lax.rsqrt(jnp.mean(x * x, axis=-1, keepdims=True) + EPS) * g1_ref[...]
    hb = h.astype(BF16)

    def seg(c):
        return jnp.dot(hb, wm_ref[:, c[0]:c[1]], preferred_element_type=F32)

    mq_ref[...] = seg(_C_MQ).astype(BF16)
    mv_ref[...] = seg(_C_MV).astype(BF16)
    so_ref[...] = jax.nn.sigmoid(seg(_C_MO)).astype(BF16)
    for c, refs in ((_C_AQ, (q0_ref, q1_ref, q2_ref)), (_C_AK, (k0_ref, k1_ref, k2_ref)),
                    (_C_AV, (v0_ref, v1_ref, v2_ref))):
        val = seg(c)
        for g, ref in enumerate(refs):
            _split_residues(val[:, g * A_WIDTH:(g + 1) * A_WIDTH], DILATED_PATTERNS[g][1], ref, st_ref)
    sgm_ref[...] = jax.nn.sigmoid(seg(_C_GM)).astype(BF16)
    sga_ref[...] = jax.nn.sigmoid(seg(_C_GA)).astype(BF16)

    t = lax.dot_general(wt_ref[...], hb, _NT, preferred_element_type=F32)
    kT_ref[...] = t[0:M_QK_WIDTH].astype(BF16)
    zi = t[M_QK_WIDTH:M_QK_WIDTH + 8] + gb_ref[0:8]
    zf = t[M_QK_WIDTH + 8:M_QK_WIDTH + 16] + gb_ref[8:16]
    gi_ref[...] = GATE_SOFTCAP * jnp.tanh(zi / GATE_SOFTCAP)
    gf_ref[...] = _log_sigmoid(GATE_SOFTCAP * jnp.tanh(zf / GATE_SOFTCAP))


def _stage_inproj(x2d, norm1_g, w_in, gate_b, batch, seq, tm):
    n = x2d.shape[0]
    steps = seq // tm
    cuts = np.concatenate([[0], np.cumsum(IN_SPLITS)])
    col = lambda i: w_in[:, cuts[i]:cuts[i + 1]]
    wm = jnp.concatenate([col(0), col(2), col(3), col(5), col(6), col(7), col(8), col(9)],
                         axis=1).astype(BF16)
    wif = col(4)
    z4 = jnp.zeros((4, D_MODEL), w_in.dtype)
    wt = jnp.concatenate([col(1).T, wif[:, :M_HEADS].T, z4, wif[:, M_HEADS:].T, z4], axis=0).astype(BF16)
    gb = jnp.zeros((16, 1), F32)
    gb = gb.at[0:4, 0].set(gate_b[:M_HEADS].astype(F32)).at[8:12, 0].set(gate_b[M_HEADS:].astype(F32))
    g1 = norm1_g.astype(F32).reshape(1, D_MODEL)

    row = lambda w: pl.BlockSpec((tm, w), lambda i: (i, 0))
    rowT = lambda r: pl.BlockSpec((r, tm), lambda i: (0, i))
    full = lambda a: pl.BlockSpec(a.shape, lambda i: (0,) * a.ndim)
    dils = [d for _, d in DILATED_PATTERNS]
    res_shape = lambda d: jax.ShapeDtypeStruct((batch, d, seq // d, A_WIDTH), BF16)
    res_spec = lambda d: pl.BlockSpec((1, d, tm // d, A_WIDTH), lambda i: (i // steps, 0, i % steps, 0))
    out_shapes = (
        jax.ShapeDtypeStruct((n, M_QK_WIDTH), BF16),
        jax.ShapeDtypeStruct((M_QK_WIDTH, n), BF16),
        jax.ShapeDtypeStruct((n, M_WIDTH), BF16),
        jax.ShapeDtypeStruct((n, M_WIDTH), BF16),
        jax.ShapeDtypeStruct((8, n), F32),
        jax.ShapeDtypeStruct((8, n), F32),
        *[res_shape(d) for d in dils], *[res_shape(d) for d in dils], *[res_shape(d) for d in dils],
        jax.ShapeDtypeStruct((n, D_MODEL), BF16),
        jax.ShapeDtypeStruct((n, D_MODEL), BF16),
    )
    out_specs = (row(M_QK_WIDTH), rowT(M_QK_WIDTH), row(M_WIDTH), row(M_WIDTH), rowT(8), rowT(8),
                 *[res_spec(d) for d in dils], *[res_spec(d) for d in dils], *[res_spec(d) for d in dils],
                 row(D_MODEL), row(D_MODEL))
    outs = pl.pallas_call(
        _inproj_body,
        grid=(n // tm,),
        in_specs=[row(D_MODEL), full(g1), full(wm), full(wt), full(gb)],
        out_specs=out_specs,
        out_shape=out_shapes,
        scratch_shapes=[pltpu.VMEM((2, tm, LANES), F32)],
        compiler_params=_params("parallel"),
        name="inproj",
    )(x2d, g1, wm, wt, gb)
    mq, kT, mv, so, gi, gf = outs[:6]
    aq, ak, av = outs[6:9], outs[9:12], outs[12:15]
    return mq, kT, mv, so, gi, gf, aq, ak, av, outs[15], outs[16]


M_CHUNK_LEN = 128


def _mlstm_body(q_ref, kT_ref, v_ref, so_ref, gi_ref, gf_ref, ng_ref, o_ref, c_ref, m_ref, *, nchunk):
    L = M_CHUNK_LEN

    @pl.when(pl.program_id(1) == 0)
    def _():
        c_ref[...] = jnp.zeros_like(c_ref)
        m_ref[...] = jnp.zeros_like(m_ref)

    lane8 = lax.broadcasted_iota(I32, (8, L), 1)
    causal = lax.broadcasted_iota(I32, (L, L), 1) <= lax.broadcasted_iota(I32, (L, L), 0)
    lo_half = lax.broadcasted_iota(I32, (L, LANES), 1) < M_QK_DIM
    ones = jnp.ones((L, M_V_DIM), BF16)

    for c in range(nchunk):
        rows = slice(c * L, (c + 1) * L)
        gi = gi_ref[:, rows]
        b = gf_ref[:, rows]
        sh = 1
        while sh < L:
            b = b + jnp.where(lane8 >= sh, pltpu.roll(b, sh, 1), 0.0)
            sh *= 2
        u = gi - b
        g = b[:, L - 1:L]
        a = g + u
        amax = jnp.max(a, axis=1, keepdims=True)
        m_prev = m_ref[:, 0:1]
        m_new = jnp.maximum(g + m_prev, amax)
        w = jnp.exp(a - m_new) * (M_QK_DIM ** -0.5)
        s_old = jnp.exp(g + m_prev - m_new)

        for p in range(M_HEADS // 2):
            lanes_p = slice(p * LANES, (p + 1) * LANES)
            q_pair = q_ref[rows, lanes_p]
            kT_pair = kT_ref[lanes_p, rows]
            c_pair = c_ref[lanes_p, :].astype(BF16)
            for hh in range(2):
                h = 2 * p + hh
                hl = slice(h * M_V_DIM, (h + 1) * M_V_DIM)
                qm = jnp.where(lo_half if hh == 0 else jnp.logical_not(lo_half), q_pair, jnp.zeros_like(q_pair))
                bcol = jnp.transpose(jnp.broadcast_to(b[h:h + 1, :], (L, L)))
                dm = jnp.where(causal, bcol + u[h:h + 1, :], -jnp.inf)
                inter = bcol + m_prev[h:h + 1, :]
                m_t = jnp.maximum(inter, jnp.max(dm, axis=1, keepdims=True))
                s = jnp.dot(qm, kT_pair, preferred_element_type=F32) * (M_QK_DIM ** -0.5)
                pmat = (s * jnp.exp(dm - m_t)).astype(BF16)
                vext = jnp.concatenate([v_ref[rows, hl], ones], axis=1)
                sc = jnp.exp(inter - m_t)
                out = (jnp.dot(pmat, vext, preferred_element_type=F32)
                       + jnp.concatenate([sc, sc], axis=1) * jnp.dot(qm, c_pair, preferred_element_type=F32))
                hv = out[:, :M_V_DIM] / jnp.maximum(jnp.abs(out[:, M_V_DIM:]), jnp.exp(-m_t))
                hn = hv * lax.rsqrt(jnp.mean(hv * hv, axis=1, keepdims=True) + EPS)
                hn = hn * ng_ref[:, hl] * so_ref[rows, hl].astype(F32)
                o_ref[rows, hl] = hn.astype(BF16)
                hr = slice(h * M_QK_DIM, (h + 1) * M_QK_DIM)
                kw = (kT_ref[hr, rows].astype(F32) * w[h:h + 1, :]).astype(BF16)
                c_ref[hr, :] = s_old[h:h + 1, :] * c_ref[hr, :] + jnp.dot(kw, vext, preferred_element_type=F32)
        m_ref[...] = jnp.broadcast_to(m_new, m_ref.shape)


def _stage_mlstm(mq, kT, mv, so, gi, gf, norm_g, batch, seq, rows_per_step):
    n = batch * seq
    R = rows_per_step
    steps = seq // R
    ng = norm_g.astype(F32).reshape(1, M_WIDTH)
    row = lambda w: pl.BlockSpec((R, w), lambda b, i: (b * steps + i, 0))
    rowT = lambda r: pl.BlockSpec((r, R), lambda b, i: (0, b * steps + i))
    return pl.pallas_call(
        functools.partial(_mlstm_body, nchunk=R // M_CHUNK_LEN),
        grid=(batch, steps),
        in_specs=[row(M_QK_WIDTH), rowT(M_QK_WIDTH), row(M_WIDTH), row(M_WIDTH), rowT(8), rowT(8),
                  pl.BlockSpec((1, M_WIDTH), lambda b, i: (0, 0))],
        out_specs=row(M_WIDTH),
        out_shape=jax.ShapeDtypeStruct((n, M_WIDTH), BF16),
        scratch_shapes=[pltpu.VMEM((M_QK_WIDTH, 2 * M_V_DIM), F32), pltpu.VMEM((8, LANES), F32)],
        compiler_params=_params("parallel", "arbitrary"),
        name="mlstm",
    )(mq, kT, mv, so, gi, gf, ng)


def _attn_body(q_ref, kp_ref, kc_ref, vp_ref, vc_ref, gq_ref, gk_ref, o_ref, lse_ref, *, dil, slopes, lq):
    QB = N_BACK
    first = pl.program_id(2) == 0
    hid_r = lax.broadcasted_iota(I32, (A_WIDTH, A_WIDTH), 0) // A_HEAD_DIM
    hid_c = lax.broadcasted_iota(I32, (A_WIDTH, A_WIDTH), 1) // A_HEAD_DIM
    head_ones = (hid_r == hid_c).astype(BF16)

    def head_norm(xb, gain):
        xf = xb.astype(F32)
        ss = jnp.dot((xf * xf).astype(BF16), head_ones, preferred_element_type=F32)
        return xf * lax.rsqrt(ss * (1.0 / A_HEAD_DIM) + EPS) * gain

    qn = (head_norm(q_ref[0, 0], gq_ref[...]) * (A_HEAD_DIM ** -0.5)).astype(BF16)
    kcn = head_norm(kc_ref[0, 0], gk_ref[...]).astype(BF16)
    kpn = head_norm(kp_ref[0, 0], gk_ref[...]).astype(BF16)
    vc = vc_ref[0, 0]
    vp = vp_ref[0, 0]

    qi = lax.broadcasted_iota(I32, (QB, 2 * QB), 0)
    kj = lax.broadcasted_iota(I32, (QB, 2 * QB), 1)
    dist = qi + QB - kj
    band = jnp.logical_and(dist >= 0, dist <= N_BACK)
    band_first = jnp.logical_and(band, jnp.logical_or(kj >= QB, jnp.logical_not(first)))
    distf = (dist * dil).astype(F32)
    lo_half = lax.broadcasted_iota(I32, (QB, LANES), 1) < A_HEAD_DIM
    ones = jnp.ones((2 * QB, LANES), BF16)

    for j in range(lq // QB):
        rows = slice(j * QB, (j + 1) * QB)
        prow = slice((j - 1) * QB, j * QB)
        keys = jnp.concatenate([kpn if j == 0 else kcn[prow], kcn[rows]], axis=0)
        vals = jnp.concatenate([vp if j == 0 else vc[prow], vc[rows]], axis=0)
        mask = band_first if j == 0 else band
        for p in range(A_HEADS // 2):
            lanes_p = slice(p * LANES, (p + 1) * LANES)
            q_pair = qn[rows, lanes_p]
            k_pair = keys[:, lanes_p]
            vext = jnp.concatenate([vals[:, lanes_p], ones], axis=1)
            o_pair = None
            l_pair = None
            for hh in range(2):
                h = 2 * p + hh
                sel = lo_half if hh == 0 else jnp.logical_not(lo_half)
                qm = jnp.where(sel, q_pair, jnp.zeros_like(q_pair))
                s = lax.dot_general(qm, k_pair, _NT, preferred_element_type=F32)
                s = jnp.where(mask, s - float(slopes[h]) * distf, -jnp.inf)
                m = jnp.max(s, axis=1, keepdims=True)
                pv = jnp.dot(jnp.exp(s - m).astype(BF16), vext, preferred_element_type=F32)
                den = pv[:, LANES:]
                o_h = pv[:, :LANES] / den
                l_h = m + jnp.log(den)
                o_pair = o_h if hh == 0 else jnp.where(lo_half, o_pair, o_h)
                l_pair = l_h if hh == 0 else jnp.where(lo_half, l_pair, l_h)
            o_ref[0, 0, rows, lanes_p] = o_pair.astype(BF16)
            lse_ref[0, 0, rows, lanes_p] = l_pair


def _stage_attn(aq, ak, av, gq, gk, batch, seq, group):
    _, dil = DILATED_PATTERNS[group]
    L = seq // dil
    assert L % N_BACK == 0
    lq = min(512, L)
    nq = L // lq
    sub = lq // N_BACK
    gq_t = jnp.tile(gq.astype(F32), A_HEADS).reshape(1, A_WIDTH)
    gk_t = jnp.tile(gk.astype(F32), A_HEADS).reshape(1, A_WIDTH)
    cur = pl.BlockSpec((1, 1, lq, A_WIDTH), lambda b, r, i: (b, r, i, 0))
    prev = pl.BlockSpec((1, 1, N_BACK, A_WIDTH), lambda b, r, i: (b, r, jnp.maximum(i * sub - 1, 0), 0))
    gain = pl.BlockSpec((1, A_WIDTH), lambda b, r, i: (0, 0))
    return pl.pallas_call(
        functools.partial(_attn_body, dil=dil, slopes=tuple(_alibi_slopes()[group]), lq=lq),
        grid=(batch, dil, nq),
        in_specs=[cur, prev, cur, prev, cur, gain, gain],
        out_specs=(cur, cur),
        out_shape=(jax.ShapeDtypeStruct((batch, dil, L, A_WIDTH), BF16),
                   jax.ShapeDtypeStruct((batch, dil, L, A_WIDTH), F32)),
        compiler_params=_params("parallel", "parallel", "parallel"),
        name=f"dilated_attn_d{dil}",
    )(aq, ak, ak, av, av, gq_t, gk_t)


TOKEN_TILE_ROWS = D_MODEL // LANES


def _store_token_tiles(ref, val):
    t = val.shape[0]
    for s in range(TOKEN_TILE_ROWS):
        ref[pl.ds(s, t, stride=TOKEN_TILE_ROWS), :] = val[:, s * LANES:(s + 1) * LANES]


def _load_token_tiles(ref, t):
    return jnp.concatenate([ref[pl.ds(s, t, stride=TOKEN_TILE_ROWS), :] for s in range(TOKEN_TILE_ROWS)], axis=1)


def _token_tile(ref, tok):
    start = tok * TOKEN_TILE_ROWS
    if not isinstance(tok, int):
        start = pl.multiple_of(start, TOKEN_TILE_ROWS)
    return ref.at[pl.ds(start, TOKEN_TILE_ROWS)]


def _rows8(vals):
    t = vals[0].shape[1]
    rid = lax.broadcasted_iota(I32, (8, t), 0)
    out = jnp.zeros((8, t), vals[0].dtype)
    for k, v in enumerate(vals):
        out = jnp.where(rid == k, jnp.broadcast_to(v, (8, t)), out)
    return out


def _merge_body(hm_ref, o1_ref, o2_ref, o3_ref, l1_ref, l2_ref, l3_ref, sgm_ref, sga_ref, x_ref,
                wm_ref, wa_ref, wo_ref, g2_ref, wrT_ref, br_ref,
                x2_ref, xn_ref, loc_ref, gate_ref, tcnt_ref, tcar_ref, cnt_ref, carry_ref, st_ref):
    @pl.when(pl.program_id(0) == 0)
    def _():
        carry_ref[...] = jnp.zeros_like(carry_ref)

    dils = [d for _, d in DILATED_PATTERNS]
    l1, l2, l3 = [_merge_residues(r, d, st_ref) for r, d in zip((l1_ref, l2_ref, l3_ref), dils)]
    lmax = jnp.maximum(jnp.maximum(l1, l2), l3)
    e1, e2, e3 = jnp.exp(l1 - lmax), jnp.exp(l2 - lmax), jnp.exp(l3 - lmax)
    num = e1 * _merge_residues(o1_ref, dils[0], st_ref)
    num = num + e2 * _merge_residues(o2_ref, dils[1], st_ref)
    num = num + e3 * _merge_residues(o3_ref, dils[2], st_ref)
    h_a = num / (e1 + e2 + e3)
    y = (sgm_ref[...].astype(F32) * jnp.dot(hm_ref[...], wm_ref[...], preferred_element_type=F32)
         + sga_ref[...].astype(F32) * jnp.dot(h_a.astype(BF16), wa_ref[...], preferred_element_type=F32))
    x2 = x_ref[...] + jnp.dot(y.astype(BF16), wo_ref[...], preferred_element_type=F32)
    x2_ref[...] = x2
    xn = x2 * lax.rsqrt(jnp.mean(x2 * x2, axis=-1, keepdims=True) + EPS) * g2_ref[...]
    xn_ref[...] = xn.astype(BF16)

    logits = lax.dot_general(wrT_ref[...], xn, _NT, precision=lax.Precision.HIGHEST,
                             preferred_element_type=F32) + br_ref[...]
    t = logits.shape[1]
    eid = lax.broadcasted_iota(I32, (N_EXPERTS, t), 0).astype(F32)
    vals = logits
    top_v, top_i = [], []
    for _ in range(TOP_K):
        mx = jnp.max(vals, axis=0, keepdims=True)
        ik = jnp.min(jnp.where(vals == mx, eid, float(N_EXPERTS)), axis=0, keepdims=True)
        top_v.append(mx)
        top_i.append(ik)
        vals = jnp.where(eid == ik, -jnp.inf, vals)
    ex = [jnp.exp(v - top_v[0]) for v in top_v]
    den = ex[0] + ex[1] + ex[2] + ex[3]
    gate_ref[...] = _rows8([e / den for e in ex])

    chosen = jnp.zeros((N_EXPERTS, t), F32)
    for ik in top_i:
        chosen = chosen + (eid == ik).astype(F32)
    before = (lax.broadcasted_iota(I32, (t, t), 0) < lax.broadcasted_iota(I32, (t, t), 1)).astype(BF16)
    prefix = jnp.dot(chosen.astype(BF16), before, preferred_element_type=F32)
    tcount = jnp.broadcast_to(jnp.sum(chosen, axis=1, keepdims=True), (N_EXPERTS, LANES))
    below = (lax.broadcasted_iota(I32, (N_EXPERTS, N_EXPERTS), 1)
             < lax.broadcasted_iota(I32, (N_EXPERTS, N_EXPERTS), 0)).astype(F32)
    tile_off = jnp.dot(below, tcount, precision=lax.Precision.HIGHEST, preferred_element_type=F32)
    pos = prefix + tile_off[:, 0:1]
    loc_ref[...] = _rows8([jnp.sum(jnp.where(eid == ik, pos, 0.0), axis=0, keepdims=True).astype(I32)
                           for ik in top_i])
    carry = carry_ref[...]
    tcnt_ref[...] = tcount.astype(I32)
    tcar_ref[...] = carry.astype(I32)
    total = carry + tcount
    carry_ref[...] = total
    cnt_ref[...] = total


def _stage_merge(h_m, attn, sgm, sga, x2d, w_mb, w_ab, w_out, norm2_g, w_router, b_router, batch, seq, tm):
    n = x2d.shape[0]
    steps = seq // tm
    (o1, l1), (o2, l2), (o3, l3) = attn
    wm = w_mb.astype(BF16)
    wa = w_ab.astype(BF16)
    wo = w_out.astype(BF16)
    g2 = norm2_g.astype(F32).reshape(1, D_MODEL)
    wrT = w_router.astype(F32).T
    br = b_router.astype(F32).reshape(N_EXPERTS, 1)
    row = lambda w: pl.BlockSpec((tm, w), lambda i: (i, 0))
    rowT = lambda r: pl.BlockSpec((r, tm), lambda i: (0, i))
    full = lambda a: pl.BlockSpec(a.shape, lambda i: (0,) * a.ndim)
    res = lambda d: pl.BlockSpec((1, d, tm // d, A_WIDTH), lambda i: (i // steps, 0, i % steps, 0))
    dils = [d for _, d in DILATED_PATTERNS]
    per_tile = pl.BlockSpec((N_EXPERTS, LANES), lambda i: (0, i))
    return pl.pallas_call(
        _merge_body,
        grid=(n // tm,),
        in_specs=[row(M_WIDTH), *[res(d) for d in dils], *[res(d) for d in dils],
                  row(D_MODEL), row(D_MODEL), row(D_MODEL),
                  full(wm), full(wa), full(wo), full(g2), full(wrT), full(br)],
        out_specs=(row(D_MODEL), row(D_MODEL), rowT(8), rowT(8), per_tile, per_tile,
                   pl.BlockSpec((N_EXPERTS, LANES), lambda i: (0, 0))),
        out_shape=(jax.ShapeDtypeStruct((n, D_MODEL), F32),
                   jax.ShapeDtypeStruct((n, D_MODEL), BF16),
                   jax.ShapeDtypeStruct((8, n), I32),
                   jax.ShapeDtypeStruct((8, n), F32),
                   jax.ShapeDtypeStruct((N_EXPERTS, (n // tm) * LANES), I32),
                   jax.ShapeDtypeStruct((N_EXPERTS, (n // tm) * LANES), I32),
                   jax.ShapeDtypeStruct((N_EXPERTS, LANES), F32)),
        scratch_shapes=[pltpu.VMEM((N_EXPERTS, LANES), F32), pltpu.VMEM((2, tm, LANES), F32)],
        compiler_params=_params("arbitrary"),
        name="merge_route",
    )(h_m, o1, o2, o3, l1, l2, l3, sgm, sga, x2d, wm, wa, wo, g2, wrT, br)


def _offsets_body(cnt_ref, blk_ref, pstart_ref, zlo_ref, zhi_ref, *, nblk_pad):
    cnt = cnt_ref[...]
    padded = jnp.floor((cnt + (MOE_BLOCK - 1)) * (1.0 / MOE_BLOCK)) * MOE_BLOCK
    lower = (lax.broadcasted_iota(I32, (N_EXPERTS, N_EXPERTS), 1)
             <= lax.broadcasted_iota(I32, (N_EXPERTS, N_EXPERTS), 0)).astype(F32)
    pends = jnp.dot(lower, padded, precision=lax.Precision.HIGHEST, preferred_element_type=F32)
    pstart = pends - padded
    pstart_ref[...] = pstart.astype(I32)
    zlo_ref[...] = (pstart + cnt).astype(I32)
    zhi_ref[...] = pends.astype(I32)

    first_row = (lax.broadcasted_iota(I32, (N_EXPERTS, nblk_pad), 1) * MOE_BLOCK).astype(F32)
    pe = jnp.broadcast_to(pends[:, 0:1], (N_EXPERTS, nblk_pad))
    be = jnp.sum((pe <= first_row).astype(F32), axis=0, keepdims=True)
    be = jnp.minimum(be, float(N_EXPERTS - 1))
    nused = pends[N_EXPERTS - 1:N_EXPERTS, 0:1] * (1.0 / MOE_BLOCK)
    blk_ref[...] = _rows8([be.astype(I32), jnp.broadcast_to(nused, (1, nblk_pad)).astype(I32)])


def _stage_offsets(cnt, nblk):
    nblk_pad = -(-nblk // LANES) * LANES
    const = lambda r, c: pl.BlockSpec((r, c), lambda i: (0, 0))
    per_expert = jax.ShapeDtypeStruct((N_EXPERTS, LANES), I32)
    return pl.pallas_call(
        functools.partial(_offsets_body, nblk_pad=nblk_pad),
        grid=(1,),
        in_specs=[const(N_EXPERTS, LANES)],
        out_specs=(const(8, nblk_pad), const(N_EXPERTS, LANES), const(N_EXPERTS, LANES), const(N_EXPERTS, LANES)),
        out_shape=(jax.ShapeDtypeStruct((8, nblk_pad), I32), per_expert, per_expert, per_expert),
        compiler_params=_params("arbitrary"),
        name="route_offsets",
    )(cnt)


RUN_BITS = 10


def _tile_rows(ref, first_row, nrows):
    start = first_row * TOKEN_TILE_ROWS
    if not isinstance(first_row, int):
        start = pl.multiple_of(start, TOKEN_TILE_ROWS)
    return ref.at[pl.ds(start, nrows * TOKEN_TILE_ROWS)]


def _for_each_piece(length, fn):
    for b in reversed(range(RUN_BITS)):
        @pl.when(((length >> b) & 1) == 1)
        def _(b=b):
            fn((length >> (b + 1)) << (b + 1), 1 << b)


def _for_each_run(tile, tcnt_ref, tcar_ref, pstart_ref, fn):
    def per_expert(e, local):
        count = tcnt_ref[tile, e]
        first = pstart_ref[e] + tcar_ref[tile, e]
        _for_each_piece(count, lambda off, size: fn(local + off, first + off, size))
        return local + count

    lax.fori_loop(0, N_EXPERTS, per_expert, 0)


PERM_CHUNK = 256


def _dispatch_body(tcnt_ref, tcar_ref, pstart_ref, zlo_ref, zhi_ref, loc_ref, xn_ref, xs_hbm,
                   buf_ref, sems, *, tm):
    step = pl.program_id(0)
    nloc_tiles = TOP_K * tm * TOKEN_TILE_ROWS

    def wait_buffer(slot):
        pltpu.make_async_copy(buf_ref.at[slot], xs_hbm.at[pl.ds(0, nloc_tiles)], sems.at[slot]).wait()

    for slot in range(2):
        tile = 2 * step + slot

        @pl.when(step > 0)
        def _(slot=slot):
            wait_buffer(slot)

        loc = loc_ref[:, slot * tm:(slot + 1) * tm]
        xn = xn_ref[slot * tm:(slot + 1) * tm, :]
        for c in range(TOP_K * tm // PERM_CHUNK):
            lid = lax.broadcasted_iota(I32, (PERM_CHUNK, tm), 0) + c * PERM_CHUNK
            hit = lid == loc[0:1, :]
            for k in range(1, TOP_K):
                hit = jnp.logical_or(hit, lid == loc[k:k + 1, :])
            rows = jnp.dot(jnp.where(hit, 1.0, 0.0).astype(BF16), xn, preferred_element_type=F32)
            _store_token_tiles(
                buf_ref.at[slot, pl.ds(c * PERM_CHUNK * TOKEN_TILE_ROWS, PERM_CHUNK * TOKEN_TILE_ROWS)], rows)

        def run_copy(local, first, size, slot=slot):
            return pltpu.make_async_copy(_tile_rows(buf_ref.at[slot], local, size),
                                         _tile_rows(xs_hbm, first, size), sems.at[slot])

        _for_each_run(tile, tcnt_ref, tcar_ref, pstart_ref, lambda l, f, s: run_copy(l, f, s).start())

    @pl.when(step == pl.num_programs(0) - 1)
    def _():
        wait_buffer(0)
        wait_buffer(1)
        zsrc = buf_ref.at[0]
        zsrc[pl.ds(0, MOE_BLOCK * TOKEN_TILE_ROWS), :] = jnp.zeros((MOE_BLOCK * TOKEN_TILE_ROWS, LANES), F32)

        def zero_copy(first, size):
            return pltpu.make_async_copy(_tile_rows(zsrc, 0, size), _tile_rows(xs_hbm, first, size), sems.at[0])

        def per_expert(e, carry):
            lo = zlo_ref[e]
            npad = zhi_ref[e] - lo
            _for_each_piece(npad, lambda off, size: zero_copy(lo + off, size).start())
            _for_each_piece(npad, lambda off, size: zero_copy(lo + off, size).wait())
            return carry

        lax.fori_loop(0, N_EXPERTS, per_expert, 0)

        first_unused = zhi_ref[N_EXPERTS - 1] // MOE_BLOCK
        nblk = xs_hbm.shape[0] // (MOE_BLOCK * TOKEN_TILE_ROWS)

        def tail(blk, carry):
            zero_copy(blk * MOE_BLOCK, MOE_BLOCK).start()
            zero_copy(blk * MOE_BLOCK, MOE_BLOCK).wait()
            return carry

        lax.fori_loop(first_unused, nblk, tail, 0)


def _stage_dispatch(tables, loc8, xn, nrows, tm):
    n = xn.shape[0]
    assert TOP_K * tm >= MOE_BLOCK and (n // tm) % 2 == 0
    grid_spec = pltpu.PrefetchScalarGridSpec(
        num_scalar_prefetch=5,
        grid=(n // (2 * tm),),
        in_specs=[pl.BlockSpec((8, 2 * tm), lambda i, *_: (0, i)),
                  pl.BlockSpec((2 * tm, D_MODEL), lambda i, *_: (i, 0))],
        out_specs=pl.BlockSpec(memory_space=pl.ANY),
        scratch_shapes=[pltpu.VMEM((2, TOP_K * tm * TOKEN_TILE_ROWS, LANES), F32), pltpu.SemaphoreType.DMA((2,))],
    )
    return pl.pallas_call(
        functools.partial(_dispatch_body, tm=tm),
        grid_spec=grid_spec,
        out_shape=jax.ShapeDtypeStruct((nrows * TOKEN_TILE_ROWS, LANES), F32),
        compiler_params=_params("arbitrary"),
        name="dispatch",
    )(*tables, loc8, xn)


def _expert_body(be_ref, nu_ref, xs_ref, w1_ref, b1_ref, w2_ref, b2_ref, ys_ref, w1b_ref, w2b_ref):
    j = pl.program_id(0)
    used = j < nu_ref[0]
    jj = jnp.minimum(j, nu_ref[0] - 1)
    fresh = jnp.logical_or(j == 0, be_ref[jj] != be_ref[jnp.maximum(jj - 1, 0)])

    @pl.when(jnp.logical_and(used, fresh))
    def _():
        w1b_ref[...] = w1_ref[0].astype(BF16)
        w2b_ref[...] = w2_ref[0].astype(BF16)

    @pl.when(used)
    def _():
        xb = _load_token_tiles(xs_ref, MOE_BLOCK).astype(BF16)
        gu = jnp.dot(xb, w1b_ref[...], preferred_element_type=F32) + b1_ref[0]
        gate = jnp.minimum(gu[:, :D_FF], SWIGLU_LIMIT)
        lin = jnp.clip(gu[:, D_FF:], -SWIGLU_LIMIT, SWIGLU_LIMIT)
        act = (lin + 1.0) * (gate * jax.nn.sigmoid(SWIGLU_ALPHA * gate))
        ys = jnp.dot(act.astype(BF16), w2b_ref[...], preferred_element_type=F32) + b2_ref[0]
        _store_token_tiles(ys_ref, ys)

    @pl.when(jnp.logical_not(used))
    def _():
        ys_ref[...] = jnp.zeros_like(ys_ref)


def _stage_experts(block_e, nused, xs, w1, b1, w2, b2):
    nrows = xs.shape[0] // TOKEN_TILE_ROWS
    nblk = nrows // MOE_BLOCK
    blk = lambda j, be, nu: jnp.maximum(jnp.minimum(j, nu[0] - 1), 0)
    exp = lambda j, be, nu: be[blk(j, be, nu)]
    tiles = (MOE_BLOCK * TOKEN_TILE_ROWS, LANES)
    grid_spec = pltpu.PrefetchScalarGridSpec(
        num_scalar_prefetch=2,
        grid=(nblk,),
        in_specs=[pl.BlockSpec(tiles, lambda j, be, nu: (blk(j, be, nu), 0)),
                  pl.BlockSpec((1, D_MODEL, 2 * D_FF), lambda j, be, nu: (exp(j, be, nu), 0, 0)),
                  pl.BlockSpec((1, 1, 2 * D_FF), lambda j, be, nu: (exp(j, be, nu), 0, 0)),
                  pl.BlockSpec((1, D_FF, D_MODEL), lambda j, be, nu: (exp(j, be, nu), 0, 0)),
                  pl.BlockSpec((1, 1, D_MODEL), lambda j, be, nu: (exp(j, be, nu), 0, 0))],
        out_specs=pl.BlockSpec(tiles, lambda j, be, nu: (j, 0)),
        scratch_shapes=[pltpu.VMEM((D_MODEL, 2 * D_FF), BF16), pltpu.VMEM((D_FF, D_MODEL), BF16)],
    )
    return pl.pallas_call(
        _expert_body,
        grid_spec=grid_spec,
        out_shape=jax.ShapeDtypeStruct((nrows * TOKEN_TILE_ROWS, LANES), F32),
        compiler_params=_params("arbitrary"),
        name="experts",
    )(block_e, nused, xs, w1, b1.reshape(N_EXPERTS, 1, 2 * D_FF), w2, b2.reshape(N_EXPERTS, 1, D_MODEL))


def _combine_body(tcnt_ref, tcar_ref, pstart_ref, loc_ref, gate_ref, x2_ref, ys_hbm, out_ref,
                  buf_ref, g_ref, sems, *, tm):
    step = pl.program_id(0)
    nloc = TOP_K * tm

    def start_runs(tile, slot):
        def run_copy(local, first, size):
            return pltpu.make_async_copy(_tile_rows(ys_hbm, first, size),
                                         _tile_rows(buf_ref.at[slot], local, size), sems.at[slot])
        _for_each_run(tile, tcnt_ref, tcar_ref, pstart_ref, lambda l, f, s: run_copy(l, f, s).start())

    def wait_buffer(slot):
        pltpu.make_async_copy(ys_hbm.at[pl.ds(0, nloc * TOKEN_TILE_ROWS)], buf_ref.at[slot], sems.at[slot]).wait()

    def combine(slot):
        zpad = jnp.zeros((LANES - 16, LANES), F32)
        lane = lax.broadcasted_iota(I32, (LANES, nloc), 1).astype(F32)
        for c in range(tm // LANES):
            cols_in = slice(slot * tm + c * LANES, slot * tm + (c + 1) * LANES)
            cols = jnp.transpose(jnp.concatenate([loc_ref[:, cols_in].astype(F32), gate_ref[:, cols_in], zpad], axis=0))
            g = jnp.zeros((LANES, nloc), F32)
            for k in range(TOP_K):
                g = jnp.where(lane == cols[:, k:k + 1], cols[:, 8 + k:9 + k], g)
            g_ref[c * LANES:(c + 1) * LANES, :] = g.astype(BF16)
        wait_buffer(slot)
        ys = _load_token_tiles(buf_ref.at[slot], nloc).astype(BF16)
        rows = slice(slot * tm, (slot + 1) * tm)
        out_ref[rows, :] = x2_ref[rows, :] + jnp.dot(g_ref[...], ys, preferred_element_type=F32)

    @pl.when(step == 0)
    def _():
        start_runs(0, 0)

    start_runs(2 * step + 1, 1)
    combine(0)

    @pl.when(step + 1 < pl.num_programs(0))
    def _():
        start_runs(2 * step + 2, 0)

    combine(1)


def _stage_combine(tables, loc8, gate8, x2, ys, tm):
    n = x2.shape[0]
    assert (n // tm) % 2 == 0
    grid_spec = pltpu.PrefetchScalarGridSpec(
        num_scalar_prefetch=3,
        grid=(n // (2 * tm),),
        in_specs=[pl.BlockSpec((8, 2 * tm), lambda i, *_: (0, i)),
                  pl.BlockSpec((8, 2 * tm), lambda i, *_: (0, i)),
                  pl.BlockSpec((2 * tm, D_MODEL), lambda i, *_: (i, 0)),
                  pl.BlockSpec(memory_space=pl.ANY)],
        out_specs=pl.BlockSpec((2 * tm, D_MODEL), lambda i, *_: (i, 0)),
        scratch_shapes=[pltpu.VMEM((2, TOP_K * tm * TOKEN_TILE_ROWS, LANES), F32),
                        pltpu.VMEM((tm, TOP_K * tm), BF16),
                        pltpu.SemaphoreType.DMA((2,))],
    )
    return pl.pallas_call(
        functools.partial(_combine_body, tm=tm),
        grid_spec=grid_spec,
        out_shape=jax.ShapeDtypeStruct((n, D_MODEL), F32),
        compiler_params=_params("arbitrary"),
        name="combine",
    )(*tables, loc8, gate8, x2, ys)


def _moe(x2, xn, loc8, gate8, tcnt, tcar, cnt, w1, b1, w2, b2, tm):
    n = x2.shape[0]
    ntile = n // tm
    nblk = -(-(n * TOP_K) // MOE_BLOCK) + N_EXPERTS
    blk8, pstart, zlo, zhi = _stage_offsets(cnt, nblk)
    per_tile = lambda a: a.reshape(N_EXPERTS, ntile, LANES)[:, :, 0].T
    tables = (per_tile(tcnt), per_tile(tcar), pstart[:, 0])
    xs = _stage_dispatch(tables + (zlo[:, 0], zhi[:, 0]), loc8, xn, nblk * MOE_BLOCK, tm)
    ys = _stage_experts(blk8[0, :nblk], blk8[1, :1], xs, w1, b1, w2, b2)
    return _stage_combine(tables, loc8, gate8, x2, ys, tm)


def kernel(x, norm1_g, w_in, mlstm_gate_b, mlstm_norm_g, attn_q_norm_g, attn_k_norm_g, w_mlstm_branch,
           w_attn_branch, w_out, norm2_g, w_router, b_router, w1, b1, w2, b2):
    batch, seq, _ = x.shape
    n = batch * seq
    for l in range(norm1_g.shape[0]):
        x2d = x.reshape(n, D_MODEL)
        tm = min(512, seq)
        mq, kT, mv, so, gi, gf, aq, ak, av, sgm, sga = _stage_inproj(
            x2d, norm1_g[l], w_in[l], mlstm_gate_b[l], batch, seq, tm)
        h_m = _stage_mlstm(mq, kT, mv, so, gi, gf, mlstm_norm_g[l], batch, seq, tm)
        attn = [_stage_attn(aq[g], ak[g], av[g], attn_q_norm_g[l, g], attn_k_norm_g[l, g], batch, seq, g)
                for g in range(N_GROUPS)]
        x2, xn, loc8, gate8, tcnt, tcar, cnt = _stage_merge(
            h_m, attn, sgm, sga, x2d, w_mlstm_branch[l], w_attn_branch[l], w_out[l], norm2_g[l],
            w_router[l], b_router[l], batch, seq, tm)
        out = _moe(x2, xn, loc8, gate8, tcnt, tcar, cnt, w1[l], b1[l], w2[l], b2[l], tm)
        x = out.reshape(batch, seq, D_MODEL)
    return x
```

```python
import functools

import numpy as np
import jax
import jax.numpy as jnp
from jax import lax
from jax.experimental import pallas as pl
from jax.experimental.pallas import tpu as pltpu

F32 = jnp.float32
BF16 = jnp.bfloat16
I32 = jnp.int32

D_MODEL = 1024
M_HEADS = 4
M_QK_DIM = 64
M_V_DIM = 128
GATE_SOFTCAP = 15.0
A_HEADS = 4
A_HEAD_DIM = 64
DILATED_PATTERNS = ((128, 1), (512, 4), (2048, 16))
N_GROUPS = len(DILATED_PATTERNS)
N_BACK = 128
N_EXPERTS = 32
TOP_K = 4
D_FF = 1024
SWIGLU_LIMIT = 7.0
SWIGLU_ALPHA = 1.702
MOE_BLOCK = 512
EPS = 1e-6

M_WIDTH = M_HEADS * M_V_DIM
M_QK_WIDTH = M_HEADS * M_QK_DIM
A_WIDTH = A_HEADS * A_HEAD_DIM
IN_SPLITS = (M_QK_WIDTH, M_QK_WIDTH, M_WIDTH, M_WIDTH, 2 * M_HEADS,
             N_GROUPS * A_WIDTH, N_GROUPS * A_WIDTH, N_GROUPS * A_WIDTH, D_MODEL, D_MODEL)

LANES = 128
VMEM_LIMIT = 56 * 1024 * 1024

_NT = (((1,), (1,)), ((), ()))


def _alibi_slopes():
    n = N_GROUPS * A_HEADS
    s = np.exp2(-8.0 * np.arange(1, n + 1) / n).astype(np.float32)
    return s.reshape(N_GROUPS, A_HEADS)


def _params(*sem):
    return pltpu.CompilerParams(dimension_semantics=sem, vmem_limit_bytes=VMEM_LIMIT)


def _log_sigmoid(x):
    return jnp.minimum(x, 0.0) - jnp.log1p(jnp.exp(-jnp.abs(x)))


_C_MQ = (0, 256)
_C_MV = (256, 768)
_C_MO = (768, 1280)
_C_AQ = (1280, 2048)
_C_AK = (2048, 2816)
_C_AV = (2816, 3584)
_C_GM = (3584, 4608)
_C_GA = (4608, 5632)
_W_MAIN = 5632
_WT_ROWS = M_QK_WIDTH + 16


def _split_residues(val, d, out_ref, st_ref):
    t = val.shape[0]
    if d == 1:
        out_ref[0, 0] = val.astype(out_ref.dtype)
        return
    st_ref[0] = val[:, :LANES]
    st_ref[1] = val[:, LANES:]
    for r in range(d):
        piece = jnp.concatenate([st_ref[0, pl.ds(r, t // d, stride=d), :],
                                 st_ref[1, pl.ds(r, t // d, stride=d), :]], axis=1)
        out_ref[0, r] = piece.astype(out_ref.dtype)


def _merge_residues(ref, d, st_ref):
    if d == 1:
        return ref[0, 0].astype(F32)
    m = ref.shape[2]
    for r in range(d):
        blk = ref[0, r].astype(F32)
        st_ref[0, pl.ds(r, m, stride=d), :] = blk[:, :LANES]
        st_ref[1, pl.ds(r, m, stride=d), :] = blk[:, LANES:]
    return jnp.concatenate([st_ref[0], st_ref[1]], axis=1)


def _inproj_body(x_ref, g1_ref, wm_ref, wt_ref, gb_ref, gq_ref, gk_ref,
                 mq_ref, kT_ref, mv_ref, so_ref, gi_ref, gf_ref,
                 q0_ref, q1_ref, q2_ref, k0_ref, k1_ref, k2_ref, v0_ref, v1_ref, v2_ref,
                 sgm_ref, sga_ref, st_ref):
    x = x_ref[...]
    h = x * lax.rsqrt(jnp.mean(x * x, axis=-1, keepdims=True) + EPS) * g1_ref[...]
    hb = h.astype(BF16)

    def seg(c):
        return jnp.dot(hb, wm_ref[:, c[0]:c[1]], preferred_element_type=F32)

    mq_ref[...] = seg(_C_MQ).astype(BF16)
    mv_ref[...] = seg(_C_MV).astype(BF16)
    so_ref[...] = jax.nn.sigmoid(seg(_C_MO)).astype(BF16)
    hid_r = lax.broadcasted_iota(I32, (A_WIDTH, A_WIDTH), 0) // A_HEAD_DIM
    hid_c = lax.broadcasted_iota(I32, (A_WIDTH, A_WIDTH), 1) // A_HEAD_DIM
    head_ones = (hid_r == hid_c).astype(BF16)
    for c, refs, gain_ref in ((_C_AQ, (q0_ref, q1_ref, q2_ref), gq_ref), (_C_AK, (k0_ref, k1_ref, k2_ref), gk_ref),
                              (_C_AV, (v0_ref, v1_ref, v2_ref), None)):
        val = seg(c)
        for g, ref in enumerate(refs):
            piece = val[:, g * A_WIDTH:(g + 1) * A_WIDTH]
            if gain_ref is not None:
                ss = jnp.dot((piece * piece).astype(BF16), head_ones, preferred_element_type=F32)
                piece = piece * lax.rsqrt(ss * (1.0 / A_HEAD_DIM) + EPS) * gain_ref[:, g * A_WIDTH:(g + 1) * A_WIDTH]
            _split_residues(piece, DILATED_PATTERNS[g][1], ref, st_ref)
    sgm_ref[...] = jax.nn.sigmoid(seg(_C_GM)).astype(BF16)
    sga_ref[...] = jax.nn.sigmoid(seg(_C_GA)).astype(BF16)

    t = lax.dot_general(wt_ref[...], hb, _NT, preferred_element_type=F32)
    kT_ref[...] = t[0:M_QK_WIDTH].astype(BF16)
    zi = t[M_QK_WIDTH:M_QK_WIDTH + 8] + gb_ref[0:8]
    zf = t[M_QK_WIDTH + 8:M_QK_WIDTH + 16] + gb_ref[8:16]
    gi_ref[...] = GATE_SOFTCAP * jnp.tanh(zi / GATE_SOFTCAP)
    gf_ref[...] = _log_sigmoid(GATE_SOFTCAP * jnp.tanh(zf / GATE_SOFTCAP))


def _stage_inproj(x2d, norm1_g, w_in, gate_b, gq, gk, batch, seq, tm):
    n = x2d.shape[0]
    steps = seq // tm
    cuts = np.concatenate([[0], np.cumsum(IN_SPLITS)])
    col = lambda i: w_in[:, cuts[i]:cuts[i + 1]]
    wm = jnp.concatenate([col(0), col(2), col(3), col(5), col(6), col(7), col(8), col(9)],
                         axis=1).astype(BF16)
    wif = col(4)
    z4 = jnp.zeros((4, D_MODEL), w_in.dtype)
    wt = jnp.concatenate([col(1).T, wif[:, :M_HEADS].T, z4, wif[:, M_HEADS:].T, z4], axis=0).astype(BF16)
    gb = jnp.zeros((16, 1), F32)
    gb = gb.at[0:4, 0].set(gate_b[:M_HEADS].astype(F32)).at[8:12, 0].set(gate_b[M_HEADS:].astype(F32))
    g1 = norm1_g.astype(F32).reshape(1, D_MODEL)
    gq_t = (jnp.tile(gq.astype(F32), (1, A_HEADS)) * (A_HEAD_DIM ** -0.5)).reshape(1, N_GROUPS * A_WIDTH)
    gk_t = jnp.tile(gk.astype(F32), (1, A_HEADS)).reshape(1, N_GROUPS * A_WIDTH)

    row = lambda w: pl.BlockSpec((tm, w), lambda i: (i, 0))
    rowT = lambda r: pl.BlockSpec((r, tm), lambda i: (0, i))
    full = lambda a: pl.BlockSpec(a.shape, lambda i: (0,) * a.ndim)
    dils = [d for _, d in DILATED_PATTERNS]
    res_shape = lambda d: jax.ShapeDtypeStruct((batch, d, seq // d, A_WIDTH), BF16)
    res_spec = lambda d: pl.BlockSpec((1, d, tm // d, A_WIDTH), lambda i: (i // steps, 0, i % steps, 0))
    out_shapes = (
        jax.ShapeDtypeStruct((n, M_QK_WIDTH), BF16),
        jax.ShapeDtypeStruct((M_QK_WIDTH, n), BF16),
        jax.ShapeDtypeStruct((n, M_WIDTH), BF16),
        jax.ShapeDtypeStruct((n, M_WIDTH), BF16),
        jax.ShapeDtypeStruct((8, n), F32),
        jax.ShapeDtypeStruct((8, n), F32),
        *[res_shape(d) for d in dils], *[res_shape(d) for d in dils], *[res_shape(d) for d in dils],
        jax.ShapeDtypeStruct((n, D_MODEL), BF16),
        jax.ShapeDtypeStruct((n, D_MODEL), BF16),
    )
    out_specs = (row(M_QK_WIDTH), rowT(M_QK_WIDTH), row(M_WIDTH), row(M_WIDTH), rowT(8), rowT(8),
                 *[res_spec(d) for d in dils], *[res_spec(d) for d in dils], *[res_spec(d) for d in dils],
                 row(D_MODEL), row(D_MODEL))
    outs = pl.pallas_call(
        _inproj_body,
        grid=(n // tm,),
        in_specs=[row(D_MODEL), full(g1), full(wm), full(wt), full(gb), full(gq_t), full(gk_t)],
        out_specs=out_specs,
        out_shape=out_shapes,
        scratch_shapes=[pltpu.VMEM((2, tm, LANES), F32)],
        compiler_params=_params("parallel"),
        name="inproj",
    )(x2d, g1, wm, wt, gb, gq_t, gk_t)
    mq, kT, mv, so, gi, gf = outs[:6]
    aq, ak, av = outs[6:9], outs[9:12], outs[12:15]
    return mq, kT, mv, so, gi, gf, aq, ak, av, outs[15], outs[16]


M_CHUNK_LEN = 128


def _mlstm_body(q_ref, kT_ref, v_ref, so_ref, gi_ref, gf_ref, ng_ref, o_ref, c_ref, m_ref, *, nchunk):
    L = M_CHUNK_LEN

    @pl.when(pl.program_id(1) == 0)
    def _():
        c_ref[...] = jnp.zeros_like(c_ref)
        m_ref[...] = jnp.zeros_like(m_ref)

    lane8 = lax.broadcasted_iota(I32, (8, L), 1)
    causal = lax.broadcasted_iota(I32, (L, L), 1) <= lax.broadcasted_iota(I32, (L, L), 0)
    lo_half = lax.broadcasted_iota(I32, (L, LANES), 1) < M_QK_DIM
    ones = jnp.ones((L, M_V_DIM), BF16)

    heads = range(M_HEADS)
    cstate = [c_ref[h * M_QK_DIM:(h + 1) * M_QK_DIM, :] for h in heads]
    m_prev = m_ref[:, 0:1]
    chunks = []
    for c in range(nchunk):
        rows = slice(c * L, (c + 1) * L)
        gi = gi_ref[:, rows]
        b = gf_ref[:, rows]
        sh = 1
        while sh < L:
            b = b + jnp.where(lane8 >= sh, pltpu.roll(b, sh, 1), 0.0)
            sh *= 2
        u = gi - b
        g = b[:, L - 1:L]
        a = g + u
        amax = jnp.max(a, axis=1, keepdims=True)
        m_new = jnp.maximum(g + m_prev, amax)
        w = jnp.exp(a - m_new) * (M_QK_DIM ** -0.5)
        s_old = jnp.exp(g + m_prev - m_new)
        vext = [jnp.concatenate([v_ref[rows, h * M_V_DIM:(h + 1) * M_V_DIM], ones], axis=1) for h in heads]
        cloc = []
        for h in heads:
            hr = slice(h * M_QK_DIM, (h + 1) * M_QK_DIM)
            kw = (kT_ref[hr, rows].astype(F32) * w[h:h + 1, :]).astype(BF16)
            cloc.append(jnp.dot(kw, vext[h], preferred_element_type=F32))
        chunks.append(dict(rows=rows, b=b, u=u, m_prev=m_prev, state=cstate, vext=vext))
        cstate = [s_old[h:h + 1, :] * cstate[h] + cloc[h] for h in heads]
        m_prev = m_new
    for h in heads:
        c_ref[h * M_QK_DIM:(h + 1) * M_QK_DIM, :] = cstate[h]
    m_ref[...] = jnp.broadcast_to(m_prev, m_ref.shape)

    for ch in chunks:
        rows = ch["rows"]
        ch["s"], ch["qc"] = [], []
        for p in range(M_HEADS // 2):
            lanes_p = slice(p * LANES, (p + 1) * LANES)
            q_pair = q_ref[rows, lanes_p]
            kT_pair = kT_ref[lanes_p, rows]
            c_pair = jnp.concatenate([ch["state"][2 * p], ch["state"][2 * p + 1]], axis=0).astype(BF16)
            for hh in range(2):
                qm = jnp.where(lo_half if hh == 0 else jnp.logical_not(lo_half), q_pair, jnp.zeros_like(q_pair))
                ch["s"].append(jnp.dot(qm, kT_pair, preferred_element_type=F32) * (M_QK_DIM ** -0.5))
                ch["qc"].append(jnp.dot(qm, c_pair, preferred_element_type=F32))

    for ch in chunks:
        rows, b, u, m_prev = ch["rows"], ch["b"], ch["u"], ch["m_prev"]
        for h in heads:
            hl = slice(h * M_V_DIM, (h + 1) * M_V_DIM)
            bcol = jnp.transpose(jnp.broadcast_to(b[h:h + 1, :], (L, L)))
            dm = jnp.where(causal, bcol + u[h:h + 1, :], -jnp.inf)
            inter = bcol + m_prev[h:h + 1, :]
            m_t = jnp.maximum(inter, jnp.max(dm, axis=1, keepdims=True))
            pmat = (ch["s"][h] * jnp.exp(dm - m_t)).astype(BF16)
            sc = jnp.exp(inter - m_t)
            out = (jnp.dot(pmat, ch["vext"][h], preferred_element_type=F32)
                   + jnp.concatenate([sc, sc], axis=1) * ch["qc"][h])
            hv = out[:, :M_V_DIM] / jnp.maximum(jnp.abs(out[:, M_V_DIM:]), jnp.exp(-m_t))
            hn = hv * lax.rsqrt(jnp.mean(hv * hv, axis=1, keepdims=True) + EPS)
            hn = hn * ng_ref[:, hl] * so_ref[rows, hl].astype(F32)
            o_ref[rows, hl] = hn.astype(BF16)


def _stage_mlstm(mq, kT, mv, so, gi, gf, norm_g, batch, seq, rows_per_step):
    n = batch * seq
    R = rows_per_step
    steps = seq // R
    ng = norm_g.astype(F32).reshape(1, M_WIDTH)
    row = lambda w: pl.BlockSpec((R, w), lambda b, i: (b * steps + i, 0))
    rowT = lambda r: pl.BlockSpec((r, R), lambda b, i: (0, b * steps + i))
    return pl.pallas_call(
        functools.partial(_mlstm_body, nchunk=R // M_CHUNK_LEN),
        grid=(batch, steps),
        in_specs=[row(M_QK_WIDTH), rowT(M_QK_WIDTH), row(M_WIDTH), row(M_WIDTH), rowT(8), rowT(8),
                  pl.BlockSpec((1, M_WIDTH), lambda b, i: (0, 0))],
        out_specs=row(M_WIDTH),
        out_shape=jax.ShapeDtypeStruct((n, M_WIDTH), BF16),
        scratch_shapes=[pltpu.VMEM((M_QK_WIDTH, 2 * M_V_DIM), F32), pltpu.VMEM((8, LANES), F32)],
        compiler_params=_params("parallel", "arbitrary"),
        name="mlstm",
    )(mq, kT, mv, so, gi, gf, ng)


def _attn_body(q_ref, kp_ref, kc_ref, vp_ref, vc_ref, o_ref, lse_ref, *, dil, slopes, lq):
    QB = N_BACK
    first = pl.program_id(2) == 0
    qn = q_ref[0, 0]
    kcn = kc_ref[0, 0]
    kpn = kp_ref[0, 0]
    vc = vc_ref[0, 0]
    vp = vp_ref[0, 0]

    qi = lax.broadcasted_iota(I32, (QB, 2 * QB), 0)
    kj = lax.broadcasted_iota(I32, (QB, 2 * QB), 1)
    dist = qi + QB - kj
    band = jnp.logical_and(dist >= 0, dist <= N_BACK)
    distf = (dist * dil).astype(F32)
    bias = [jnp.where(band, -float(slopes[h]) * distf, -jnp.inf) for h in range(A_HEADS)]
    no_prev = jnp.logical_and(first, kj < QB)
    lo_half = lax.broadcasted_iota(I32, (QB, LANES), 1) < A_HEAD_DIM
    ones = jnp.ones((2 * QB, LANES), BF16)

    units = []
    for j in range(lq // QB):
        rows = slice(j * QB, (j + 1) * QB)
        prow = slice((j - 1) * QB, j * QB)
        keys = jnp.concatenate([kpn if j == 0 else kcn[prow], kcn[rows]], axis=0)
        vals = jnp.concatenate([vp if j == 0 else vc[prow], vc[rows]], axis=0)
        for p in range(A_HEADS // 2):
            lanes_p = slice(p * LANES, (p + 1) * LANES)
            q_pair = qn[rows, lanes_p]
            k_pair = keys[:, lanes_p]
            vext = jnp.concatenate([vals[:, lanes_p], ones], axis=1)
            scores = []
            for hh in range(2):
                sel = lo_half if hh == 0 else jnp.logical_not(lo_half)
                qm = jnp.where(sel, q_pair, jnp.zeros_like(q_pair))
                scores.append(lax.dot_general(qm, k_pair, _NT, preferred_element_type=F32))
            units.append((j, rows, lanes_p, p, vext, scores))

    for j, rows, lanes_p, p, vext, scores in units:
        o_pair = None
        l_pair = None
        for hh in range(2):
            s = scores[hh] + bias[2 * p + hh]
            if j == 0:
                s = jnp.where(no_prev, -jnp.inf, s)
            m = jnp.max(s, axis=1, keepdims=True)
            pv = jnp.dot(jnp.exp(s - m).astype(BF16), vext, preferred_element_type=F32)
            den = pv[:, LANES:]
            o_h = pv[:, :LANES] / den
            l_h = m + jnp.log(den)
            o_pair = o_h if hh == 0 else jnp.where(lo_half, o_pair, o_h)
            l_pair = l_h if hh == 0 else jnp.where(lo_half, l_pair, l_h)
        o_ref[0, 0, rows, lanes_p] = o_pair.astype(BF16)
        lse_ref[0, 0, rows, lanes_p] = l_pair


def _stage_attn(aq, ak, av, batch, seq, group):
    _, dil = DILATED_PATTERNS[group]
    L = seq // dil
    assert L % N_BACK == 0
    lq = min(512, L)
    nq = L // lq
    sub = lq // N_BACK
    cur = pl.BlockSpec((1, 1, lq, A_WIDTH), lambda b, r, i: (b, r, i, 0))
    prev = pl.BlockSpec((1, 1, N_BACK, A_WIDTH), lambda b, r, i: (b, r, jnp.maximum(i * sub - 1, 0), 0))
    return pl.pallas_call(
        functools.partial(_attn_body, dil=dil, slopes=tuple(_alibi_slopes()[group]), lq=lq),
        grid=(batch, dil, nq),
        in_specs=[cur, prev, cur, prev, cur],
        out_specs=(cur, cur),
        out_shape=(jax.ShapeDtypeStruct((batch, dil, L, A_WIDTH), BF16),
                   jax.ShapeDtypeStruct((batch, dil, L, A_WIDTH), F32)),
        compiler_params=_params("parallel", "parallel", "parallel"),
        name=f"dilated_attn_d{dil}",
    )(aq, ak, ak, av, av)


TOKEN_TILE_ROWS = D_MODEL // LANES


def _store_token_tiles(ref, val):
    t = val.shape[0]
    for s in range(TOKEN_TILE_ROWS):
        ref[pl.ds(s, t, stride=TOKEN_TILE_ROWS), :] = val[:, s * LANES:(s + 1) * LANES]


def _load_token_tiles(ref, t):
    return jnp.concatenate([ref[pl.ds(s, t, stride=TOKEN_TILE_ROWS), :] for s in range(TOKEN_TILE_ROWS)], axis=1)


def _token_tile(ref, tok):
    start = tok * TOKEN_TILE_ROWS
    if not isinstance(tok, int):
        start = pl.multiple_of(start, TOKEN_TILE_ROWS)
    return ref.at[pl.ds(start, TOKEN_TILE_ROWS)]


def _rows8(vals):
    t = vals[0].shape[1]
    rid = lax.broadcasted_iota(I32, (8, t), 0)
    out = jnp.zeros((8, t), vals[0].dtype)
    for k, v in enumerate(vals):
        out = jnp.where(rid == k, jnp.broadcast_to(v, (8, t)), out)
    return out


def _merge_body(hm_ref, o1_ref, o2_ref, o3_ref, l1_ref, l2_ref, l3_ref, sgm_ref, sga_ref, x_ref,
                wm_ref, wa_ref, wo_ref, g2_ref, wrT_ref, br_ref,
                x2_ref, xn_ref, loc_ref, gate_ref, tcnt_ref, tcar_ref, cnt_ref, carry_ref, st_ref):
    @pl.when(pl.program_id(0) == 0)
    def _():
        carry_ref[...] = jnp.zeros_like(carry_ref)

    dils = [d for _, d in DILATED_PATTERNS]
    l1, l2, l3 = [_merge_residues(r, d, st_ref) for r, d in zip((l1_ref, l2_ref, l3_ref), dils)]
    lmax = jnp.maximum(jnp.maximum(l1, l2), l3)
    e1, e2, e3 = jnp.exp(l1 - lmax), jnp.exp(l2 - lmax), jnp.exp(l3 - lmax)
    num = e1 * _merge_residues(o1_ref, dils[0], st_ref)
    num = num + e2 * _merge_residues(o2_ref, dils[1], st_ref)
    num = num + e3 * _merge_residues(o3_ref, dils[2], st_ref)
    h_a = num / (e1 + e2 + e3)
    y = (sgm_ref[...].astype(F32) * jnp.dot(hm_ref[...], wm_ref[...], preferred_element_type=F32)
         + sga_ref[...].astype(F32) * jnp.dot(h_a.astype(BF16), wa_ref[...], preferred_element_type=F32))
    x2 = x_ref[...] + jnp.dot(y.astype(BF16), wo_ref[...], preferred_element_type=F32)
    x2_ref[...] = x2
    xn = x2 * lax.rsqrt(jnp.mean(x2 * x2, axis=-1, keepdims=True) + EPS) * g2_ref[...]
    xn_ref[...] = xn.astype(BF16)

    logits = lax.dot_general(wrT_ref[...], xn, _NT, precision=lax.Precision.HIGHEST,
                             preferred_element_type=F32) + br_ref[...]
    t = logits.shape[1]
    eid = lax.broadcasted_iota(I32, (N_EXPERTS, t), 0).astype(F32)
    vals = logits
    top_v, top_i = [], []
    for _ in range(TOP_K):
        mx = jnp.max(vals, axis=0, keepdims=True)
        ik = jnp.min(jnp.where(vals == mx, eid, float(N_EXPERTS)), axis=0, keepdims=True)
        top_v.append(mx)
        top_i.append(ik)
        vals = jnp.where(eid == ik, -jnp.inf, vals)
    ex = [jnp.exp(v - top_v[0]) for v in top_v]
    den = ex[0] + ex[1] + ex[2] + ex[3]
    gate_ref[...] = _rows8([e / den for e in ex])

    chosen = jnp.zeros((N_EXPERTS, t), F32)
    for ik in top_i:
        chosen = chosen + (eid == ik).astype(F32)
    before = (lax.broadcasted_iota(I32, (t, t), 0) < lax.broadcasted_iota(I32, (t, t), 1)).astype(BF16)
    prefix = jnp.dot(chosen.astype(BF16), before, preferred_element_type=F32)
    tcount = jnp.broadcast_to(jnp.sum(chosen, axis=1, keepdims=True), (N_EXPERTS, LANES))
    below = (lax.broadcasted_iota(I32, (N_EXPERTS, N_EXPERTS), 1)
             < lax.broadcasted_iota(I32, (N_EXPERTS, N_EXPERTS), 0)).astype(F32)
    tile_off = jnp.dot(below, tcount, precision=lax.Precision.HIGHEST, preferred_element_type=F32)
    pos = prefix + tile_off[:, 0:1]
    loc_ref[...] = _rows8([jnp.sum(jnp.where(eid == ik, pos, 0.0), axis=0, keepdims=True).astype(I32)
                           for ik in top_i])
    carry = carry_ref[...]
    tcnt_ref[...] = tcount.astype(I32)
    tcar_ref[...] = carry.astype(I32)
    total = carry + tcount
    carry_ref[...] = total
    cnt_ref[...] = total


def _stage_merge(h_m, attn, sgm, sga, x2d, w_mb, w_ab, w_out, norm2_g, w_router, b_router, batch, seq, tm):
    n = x2d.shape[0]
    steps = seq // tm
    (o1, l1), (o2, l2), (o3, l3) = attn
    wm = w_mb.astype(BF16)
    wa = w_ab.astype(BF16)
    wo = w_out.astype(BF16)
    g2 = norm2_g.astype(F32).reshape(1, D_MODEL)
    wrT = w_router.astype(F32).T
    br = b_router.astype(F32).reshape(N_EXPERTS, 1)
    row = lambda w: pl.BlockSpec((tm, w), lambda i: (i, 0))
    rowT = lambda r: pl.BlockSpec((r, tm), lambda i: (0, i))
    full = lambda a: pl.BlockSpec(a.shape, lambda i: (0,) * a.ndim)
    res = lambda d: pl.BlockSpec((1, d, tm // d, A_WIDTH), lambda i: (i // steps, 0, i % steps, 0))
    dils = [d for _, d in DILATED_PATTERNS]
    per_tile = pl.BlockSpec((N_EXPERTS, LANES), lambda i: (0, i))
    return pl.pallas_call(
        _merge_body,
        grid=(n // tm,),
        in_specs=[row(M_WIDTH), *[res(d) for d in dils], *[res(d) for d in dils],
                  row(D_MODEL), row(D_MODEL), row(D_MODEL),
                  full(wm), full(wa), full(wo), full(g2), full(wrT), full(br)],
        out_specs=(row(D_MODEL), row(D_MODEL), rowT(8), rowT(8), per_tile, per_tile,
                   pl.BlockSpec((N_EXPERTS, LANES), lambda i: (0, 0))),
        out_shape=(jax.ShapeDtypeStruct((n, D_MODEL), F32),
                   jax.ShapeDtypeStruct((n, D_MODEL), BF16),
                   jax.ShapeDtypeStruct((8, n), I32),
                   jax.ShapeDtypeStruct((8, n), F32),
                   jax.ShapeDtypeStruct((N_EXPERTS, (n // tm) * LANES), I32),
                   jax.ShapeDtypeStruct((N_EXPERTS, (n // tm) * LANES), I32),
                   jax.ShapeDtypeStruct((N_EXPERTS, LANES), F32)),
        scratch_shapes=[pltpu.VMEM((N_EXPERTS, LANES), F32), pltpu.VMEM((2, tm, LANES), F32)],
        compiler_params=_params("arbitrary"),
        name="merge_route",
    )(h_m, o1, o2, o3, l1, l2, l3, sgm, sga, x2d, wm, wa, wo, g2, wrT, br)


def _offsets_body(cnt_ref, blk_ref, pstart_ref, zlo_ref, zhi_ref, *, nblk_pad):
    cnt = cnt_ref[...]
    padded = jnp.floor((cnt + (MOE_BLOCK - 1)) * (1.0 / MOE_BLOCK)) * MOE_BLOCK
    lower = (lax.broadcasted_iota(I32, (N_EXPERTS, N_EXPERTS), 1)
             <= lax.broadcasted_iota(I32, (N_EXPERTS, N_EXPERTS), 0)).astype(F32)
    pends = jnp.dot(lower, padded, precision=lax.Precision.HIGHEST, preferred_element_type=F32)
    pstart = pends - padded
    pstart_ref[...] = pstart.astype(I32)
    zlo_ref[...] = (pstart + cnt).astype(I32)
    zhi_ref[...] = pends.astype(I32)

    first_row = (lax.broadcasted_iota(I32, (N_EXPERTS, nblk_pad), 1) * MOE_BLOCK).astype(F32)
    pe = jnp.broadcast_to(pends[:, 0:1], (N_EXPERTS, nblk_pad))
    be = jnp.sum((pe <= first_row).astype(F32), axis=0, keepdims=True)
    be = jnp.minimum(be, float(N_EXPERTS - 1))
    nused = pends[N_EXPERTS - 1:N_EXPERTS, 0:1] * (1.0 / MOE_BLOCK)
    blk_ref[...] = _rows8([be.astype(I32), jnp.broadcast_to(nused, (1, nblk_pad)).astype(I32)])


def _stage_offsets(cnt, nblk):
    nblk_pad = -(-nblk // LANES) * LANES
    const = lambda r, c: pl.BlockSpec((r, c), lambda i: (0, 0))
    per_expert = jax.ShapeDtypeStruct((N_EXPERTS, LANES), I32)
    return pl.pallas_call(
        functools.partial(_offsets_body, nblk_pad=nblk_pad),
        grid=(1,),
        in_specs=[const(N_EXPERTS, LANES)],
        out_specs=(const(8, nblk_pad), const(N_EXPERTS, LANES), const(N_EXPERTS, LANES), const(N_EXPERTS, LANES)),
        out_shape=(jax.ShapeDtypeStruct((8, nblk_pad), I32), per_expert, per_expert, per_expert),
        compiler_params=_params("arbitrary"),
        name="route_offsets",
    )(cnt)


RUN_BITS = 10


def _tile_rows(ref, first_row, nrows):
    start = first_row * TOKEN_TILE_ROWS
    if not isinstance(first_row, int):
        start = pl.multiple_of(start, TOKEN_TILE_ROWS)
    return ref.at[pl.ds(start, nrows * TOKEN_TILE_ROWS)]


def _for_each_piece(length, fn):
    for b in reversed(range(RUN_BITS)):
        @pl.when(((length >> b) & 1) == 1)
        def _(b=b):
            fn((length >> (b + 1)) << (b + 1), 1 << b)


def _for_each_run(tile, tcnt_ref, tcar_ref, pstart_ref, fn):
    def per_expert(e, local):
        count = tcnt_ref[tile, e]
        first = pstart_ref[e] + tcar_ref[tile, e]
        _for_each_piece(count, lambda off, size: fn(local + off, first + off, size))
        return local + count

    lax.fori_loop(0, N_EXPERTS, per_expert, 0)


PERM_CHUNK = 256


def _dispatch_body(tcnt_ref, tcar_ref, pstart_ref, zlo_ref, zhi_ref, loc_ref, xn_ref, xs_hbm,
                   buf_ref, sems, *, tm):
    step = pl.program_id(0)
    nloc_tiles = TOP_K * tm * TOKEN_TILE_ROWS

    def wait_buffer(slot):
        pltpu.make_async_copy(buf_ref.at[slot], xs_hbm.at[pl.ds(0, nloc_tiles)], sems.at[slot]).wait()

    for slot in range(2):
        tile = 2 * step + slot

        @pl.when(step > 0)
        def _(slot=slot):
            wait_buffer(slot)

        loc = loc_ref[:, slot * tm:(slot + 1) * tm]
        xn = xn_ref[slot * tm:(slot + 1) * tm, :]
        for c in range(TOP_K * tm // PERM_CHUNK):
            lid = lax.broadcasted_iota(I32, (PERM_CHUNK, tm), 0) + c * PERM_CHUNK
            hit = lid == loc[0:1, :]
            for k in range(1, TOP_K):
                hit = jnp.logical_or(hit, lid == loc[k:k + 1, :])
            rows = jnp.dot(jnp.where(hit, 1.0, 0.0).astype(BF16), xn, preferred_element_type=F32)
            _store_token_tiles(
                buf_ref.at[slot, pl.ds(c * PERM_CHUNK * TOKEN_TILE_ROWS, PERM_CHUNK * TOKEN_TILE_ROWS)], rows)

        def run_copy(local, first, size, slot=slot):
            return pltpu.make_async_copy(_tile_rows(buf_ref.at[slot], local, size),
                                         _tile_rows(xs_hbm, first, size), sems.at[slot])

        _for_each_run(tile, tcnt_ref, tcar_ref, pstart_ref, lambda l, f, s: run_copy(l, f, s).start())

    @pl.when(step == pl.num_programs(0) - 1)
    def _():
        wait_buffer(0)
        wait_buffer(1)
        zsrc = buf_ref.at[0]
        zsrc[pl.ds(0, MOE_BLOCK * TOKEN_TILE_ROWS), :] = jnp.zeros((MOE_BLOCK * TOKEN_TILE_ROWS, LANES), F32)

        def zero_copy(first, size):
            return pltpu.make_async_copy(_tile_rows(zsrc, 0, size), _tile_rows(xs_hbm, first, size), sems.at[0])

        def per_expert(e, carry):
            lo = zlo_ref[e]
            npad = zhi_ref[e] - lo
            _for_each_piece(npad, lambda off, size: zero_copy(lo + off, size).start())
            _for_each_piece(npad, lambda off, size: zero_copy(lo + off, size).wait())
            return carry

        lax.fori_loop(0, N_EXPERTS, per_expert, 0)

        first_unused = zhi_ref[N_EXPERTS - 1] // MOE_BLOCK
        nblk = xs_hbm.shape[0] // (MOE_BLOCK * TOKEN_TILE_ROWS)

        def tail(blk, carry):
            zero_copy(blk * MOE_BLOCK, MOE_BLOCK).start()
            zero_copy(blk * MOE_BLOCK, MOE_BLOCK).wait()
            return carry

        lax.fori_loop(first_unused, nblk, tail, 0)


def _stage_dispatch(tables, loc8, xn, nrows, tm):
    n = xn.shape[0]
    assert TOP_K * tm >= MOE_BLOCK and (n // tm) % 2 == 0
    grid_spec = pltpu.PrefetchScalarGridSpec(
        num_scalar_prefetch=5,
        grid=(n // (2 * tm),),
        in_specs=[pl.BlockSpec((8, 2 * tm), lambda i, *_: (0, i)),
                  pl.BlockSpec((2 * tm, D_MODEL), lambda i, *_: (i, 0))],
        out_specs=pl.BlockSpec(memory_space=pl.ANY),
        scratch_shapes=[pltpu.VMEM((2, TOP_K * tm * TOKEN_TILE_ROWS, LANES), F32), pltpu.SemaphoreType.DMA((2,))],
    )
    return pl.pallas_call(
        functools.partial(_dispatch_body, tm=tm),
        grid_spec=grid_spec,
        out_shape=jax.ShapeDtypeStruct((nrows * TOKEN_TILE_ROWS, LANES), F32),
        compiler_params=_params("arbitrary"),
        name="dispatch",
    )(*tables, loc8, xn)


def _expert_body(be_ref, nu_ref, xs_ref, w1_ref, b1_ref, w2_ref, b2_ref, ys_ref, w1b_ref, w2b_ref):
    j = pl.program_id(0)
    used = j < nu_ref[0]
    jj = jnp.minimum(j, nu_ref[0] - 1)
    fresh = jnp.logical_or(j == 0, be_ref[jj] != be_ref[jnp.maximum(jj - 1, 0)])

    @pl.when(jnp.logical_and(used, fresh))
    def _():
        w1b_ref[...] = w1_ref[0].astype(BF16)
        w2b_ref[...] = w2_ref[0].astype(BF16)

    @pl.when(used)
    def _():
        xb = _load_token_tiles(xs_ref, MOE_BLOCK).astype(BF16)
        gu = jnp.dot(xb, w1b_ref[...], preferred_element_type=F32) + b1_ref[0]
        gate = jnp.minimum(gu[:, :D_FF], SWIGLU_LIMIT)
        lin = jnp.clip(gu[:, D_FF:], -SWIGLU_LIMIT, SWIGLU_LIMIT)
        act = (lin + 1.0) * (gate * jax.nn.sigmoid(SWIGLU_ALPHA * gate))
        ys = jnp.dot(act.astype(BF16), w2b_ref[...], preferred_element_type=F32) + b2_ref[0]
        _store_token_tiles(ys_ref, ys)

    @pl.when(jnp.logical_not(used))
    def _():
        ys_ref[...] = jnp.zeros_like(ys_ref)


def _stage_experts(block_e, nused, xs, w1, b1, w2, b2):
    nrows = xs.shape[0] // TOKEN_TILE_ROWS
    nblk = nrows // MOE_BLOCK
    blk = lambda j, be, nu: jnp.maximum(jnp.minimum(j, nu[0] - 1), 0)
    exp = lambda j, be, nu: be[blk(j, be, nu)]
    tiles = (MOE_BLOCK * TOKEN_TILE_ROWS, LANES)
    grid_spec = pltpu.PrefetchScalarGridSpec(
        num_scalar_prefetch=2,
        grid=(nblk,),
        in_specs=[pl.BlockSpec(tiles, lambda j, be, nu: (blk(j, be, nu), 0)),
                  pl.BlockSpec((1, D_MODEL, 2 * D_FF), lambda j, be, nu: (exp(j, be, nu), 0, 0)),
                  pl.BlockSpec((1, 1, 2 * D_FF), lambda j, be, nu: (exp(j, be, nu), 0, 0)),
                  pl.BlockSpec((1, D_FF, D_MODEL), lambda j, be, nu: (exp(j, be, nu), 0, 0)),
                  pl.BlockSpec((1, 1, D_MODEL), lambda j, be, nu: (exp(j, be, nu), 0, 0))],
        out_specs=pl.BlockSpec(tiles, lambda j, be, nu: (j, 0)),
        scratch_shapes=[pltpu.VMEM((D_MODEL, 2 * D_FF), BF16), pltpu.VMEM((D_FF, D_MODEL), BF16)],
    )
    return pl.pallas_call(
        _expert_body,
        grid_spec=grid_spec,
        out_shape=jax.ShapeDtypeStruct((nrows * TOKEN_TILE_ROWS, LANES), F32),
        compiler_params=_params("arbitrary"),
        name="experts",
    )(block_e, nused, xs, w1, b1.reshape(N_EXPERTS, 1, 2 * D_FF), w2, b2.reshape(N_EXPERTS, 1, D_MODEL))


def _combine_body(tcnt_ref, tcar_ref, pstart_ref, loc_ref, gate_ref, x2_ref, ys_hbm, out_ref,
                  buf_ref, g_ref, sems, *, tm):
    step = pl.program_id(0)
    nloc = TOP_K * tm

    def start_runs(tile, slot):
        def run_copy(local, first, size):
            return pltpu.make_async_copy(_tile_rows(ys_hbm, first, size),
                                         _tile_rows(buf_ref.at[slot], local, size), sems.at[slot])
        _for_each_run(tile, tcnt_ref, tcar_ref, pstart_ref, lambda l, f, s: run_copy(l, f, s).start())

    def wait_buffer(slot):
        pltpu.make_async_copy(ys_hbm.at[pl.ds(0, nloc * TOKEN_TILE_ROWS)], buf_ref.at[slot], sems.at[slot]).wait()

    def combine(slot):
        zpad = jnp.zeros((LANES - 16, LANES), F32)
        lane = lax.broadcasted_iota(I32, (LANES, nloc), 1).astype(F32)
        for c in range(tm // LANES):
            cols_in = slice(slot * tm + c * LANES, slot * tm + (c + 1) * LANES)
            cols = jnp.transpose(jnp.concatenate([loc_ref[:, cols_in].astype(F32), gate_ref[:, cols_in], zpad], axis=0))
            g = jnp.zeros((LANES, nloc), F32)
            for k in range(TOP_K):
                g = jnp.where(lane == cols[:, k:k + 1], cols[:, 8 + k:9 + k], g)
            g_ref[c * LANES:(c + 1) * LANES, :] = g.astype(BF16)
        wait_buffer(slot)
        ys = _load_token_tiles(buf_ref.at[slot], nloc).astype(BF16)
        rows = slice(slot * tm, (slot + 1) * tm)
        out_ref[rows, :] = x2_ref[rows, :] + jnp.dot(g_ref[...], ys, preferred_element_type=F32)

    @pl.when(step == 0)
    def _():
        start_runs(0, 0)

    start_runs(2 * step + 1, 1)
    combine(0)

    @pl.when(step + 1 < pl.num_programs(0))
    def _():
        start_runs(2 * step + 2, 0)

    combine(1)


def _stage_combine(tables, loc8, gate8, x2, ys, tm):
    n = x2.shape[0]
    assert (n // tm) % 2 == 0
    grid_spec = pltpu.PrefetchScalarGridSpec(
        num_scalar_prefetch=3,
        grid=(n // (2 * tm),),
        in_specs=[pl.BlockSpec((8, 2 * tm), lambda i, *_: (0, i)),
                  pl.BlockSpec((8, 2 * tm), lambda i, *_: (0, i)),
                  pl.BlockSpec((2 * tm, D_MODEL), lambda i, *_: (i, 0)),
                  pl.BlockSpec(memory_space=pl.ANY)],
        out_specs=pl.BlockSpec((2 * tm, D_MODEL), lambda i, *_: (i, 0)),
        scratch_shapes=[pltpu.VMEM((2, TOP_K * tm * TOKEN_TILE_ROWS, LANES), F32),
                        pltpu.VMEM((tm, TOP_K * tm), BF16),
                        pltpu.SemaphoreType.DMA((2,))],
    )
    return pl.pallas_call(
        functools.partial(_combine_body, tm=tm),
        grid_spec=grid_spec,
        out_shape=jax.ShapeDtypeStruct((n, D_MODEL), F32),
        compiler_params=_params("arbitrary"),
        name="combine",
    )(*tables, loc8, gate8, x2, ys)


def _moe(x2, xn, loc8, gate8, tcnt, tcar, cnt, w1, b1, w2, b2, tm):
    n = x2.shape[0]
    ntile = n // tm
    nblk = -(-(n * TOP_K) // MOE_BLOCK) + N_EXPERTS
    blk8, pstart, zlo, zhi = _stage_offsets(cnt, nblk)
    per_tile = lambda a: a.reshape(N_EXPERTS, ntile, LANES)[:, :, 0].T
    tables = (per_tile(tcnt), per_tile(tcar), pstart[:, 0])
    xs = _stage_dispatch(tables + (zlo[:, 0], zhi[:, 0]), loc8, xn, nblk * MOE_BLOCK, tm)
    ys = _stage_experts(blk8[0, :nblk], blk8[1, :1], xs, w1, b1, w2, b2)
    return _stage_combine(tables, loc8, gate8, x2, ys, tm)


def kernel(x, norm1_g, w_in, mlstm_gate_b, mlstm_norm_g, attn_q_norm_g, attn_k_norm_g, w_mlstm_branch,
           w_attn_branch, w_out, norm2_g, w_router, b_router, w1, b1, w2, b2):
    batch, seq, _ = x.shape
    n = batch * seq
    for l in range(norm1_g.shape[0]):
        x2d = x.reshape(n, D_MODEL)
        tm = min(512, seq)
        mq, kT, mv, so, gi, gf, aq, ak, av, sgm, sga = _stage_inproj(
            x2d, norm1_g[l], w_in[l], mlstm_gate_b[l], attn_q_norm_g[l], attn_k_norm_g[l], batch, seq, tm)
        h_m = _stage_mlstm(mq, kT, mv, so, gi, gf, mlstm_norm_g[l], batch, seq, tm)
        attn = [_stage_attn(aq[g], ak[g], av[g], batch, seq, g)
                for g in range(N_GROUPS)]
        x2, xn, loc8, gate8, tcnt, tcar, cnt = _stage_merge(
            h_m, attn, sgm, sga, x2d, w_mlstm_branch[l], w_attn_branch[l], w_out[l], norm2_g[l],
            w_router[l], b_router[l], batch, seq, tm)
        out = _moe(x2, xn, loc8, gate8, tcnt, tcar, cnt, w1[l], b1[l], w2[l], b2[l], tm)
        x = out.reshape(batch, seq, D_MODEL)
    return x
```

```python
import functools

import numpy as np
import jax
import jax.numpy as jnp
from jax import lax
from jax.experimental import pallas as pl
from jax.experimental.pallas import tpu as pltpu

F32 = jnp.float32
BF16 = jnp.bfloat16
I32 = jnp.int32

D_MODEL = 1024
M_HEADS = 4
M_QK_DIM = 64
M_V_DIM = 128
GATE_SOFTCAP = 15.0
A_HEADS = 4
A_HEAD_DIM = 64
DILATED_PATTERNS = ((128, 1), (512, 4), (2048, 16))
N_GROUPS = len(DILATED_PATTERNS)
N_BACK = 128
N_EXPERTS = 32
TOP_K = 4
D_FF = 1024
SWIGLU_LIMIT = 7.0
SWIGLU_ALPHA = 1.702
MOE_BLOCK = 512
EPS = 1e-6

M_WIDTH = M_HEADS * M_V_DIM
M_QK_WIDTH = M_HEADS * M_QK_DIM
A_WIDTH = A_HEADS * A_HEAD_DIM
IN_SPLITS = (M_QK_WIDTH, M_QK_WIDTH, M_WIDTH, M_WIDTH, 2 * M_HEADS,
             N_GROUPS * A_WIDTH, N_GROUPS * A_WIDTH, N_GROUPS * A_WIDTH, D_MODEL, D_MODEL)

LANES = 128
VMEM_LIMIT = 56 * 1024 * 1024

_NT = (((1,), (1,)), ((), ()))


def _alibi_slopes():
    n = N_GROUPS * A_HEADS
    s = np.exp2(-8.0 * np.arange(1, n + 1) / n).astype(np.float32)
    return s.reshape(N_GROUPS, A_HEADS)


def _params(*sem):
    return pltpu.CompilerParams(dimension_semantics=sem, vmem_limit_bytes=VMEM_LIMIT)


def _log_sigmoid(x):
    return jnp.minimum(x, 0.0) - jnp.log1p(jnp.exp(-jnp.abs(x)))


_C_MQ = (0, 256)
_C_MV = (256, 768)
_C_MO = (768, 1280)
_C_AQ = (1280, 2048)
_C_AK = (2048, 2816)
_C_AV = (2816, 3584)
_C_GM = (3584, 4608)
_C_GA = (4608, 5632)
_W_MAIN = 5632
_WT_ROWS = M_QK_WIDTH + 16


def _split_residues(val, d, out_ref, st_ref):
    t = val.shape[0]
    if d == 1:
        out_ref[0, 0] = val.astype(out_ref.dtype)
        return
    st_ref[0] = val[:, :LANES]
    st_ref[1] = val[:, LANES:]
    for r in range(d):
        piece = jnp.concatenate([st_ref[0, pl.ds(r, t // d, stride=d), :],
                                 st_ref[1, pl.ds(r, t // d, stride=d), :]], axis=1)
        out_ref[0, r] = piece.astype(out_ref.dtype)


def _merge_residues(ref, d, st_ref):
    if d == 1:
        return ref[0, 0].astype(F32)
    m = ref.shape[2]
    for r in range(d):
        blk = ref[0, r].astype(F32)
        st_ref[0, pl.ds(r, m, stride=d), :] = blk[:, :LANES]
        st_ref[1, pl.ds(r, m, stride=d), :] = blk[:, LANES:]
    return jnp.concatenate([st_ref[0], st_ref[1]], axis=1)


def _inproj_body(x_ref, g1_ref, wm_ref, wt_ref, gb_ref, gq_ref, gk_ref,
                 mq_ref, kT_ref, mv_ref, so_ref, gi_ref, gf_ref,
                 q0_ref, q1_ref, q2_ref, k0_ref, k1_ref, k2_ref, v0_ref, v1_ref, v2_ref,
                 sgm_ref, sga_ref, st_ref):
    x = x_ref[...]
    h = x * lax.rsqrt(jnp.mean(x * x, axis=-1, keepdims=True) + EPS) * g1_ref[...]
    hb = h.astype(BF16)

    def seg(c):
        return jnp.dot(hb, wm_ref[:, c[0]:c[1]], preferred_element_type=F32)

    mq_ref[...] = seg(_C_MQ).astype(BF16)
    mv_ref[...] = seg(_C_MV).astype(BF16)
    so_ref[...] = jax.nn.sigmoid(seg(_C_MO)).astype(BF16)
    hid_r = lax.broadcasted_iota(I32, (A_WIDTH, A_WIDTH), 0) // A_HEAD_DIM
    hid_c = lax.broadcasted_iota(I32, (A_WIDTH, A_WIDTH), 1) // A_HEAD_DIM
    head_ones = (hid_r == hid_c).astype(BF16)
    for c, refs, gain_ref in ((_C_AQ, (q0_ref, q1_ref, q2_ref), gq_ref), (_C_AK, (k0_ref, k1_ref, k2_ref), gk_ref),
                              (_C_AV, (v0_ref, v1_ref, v2_ref), None)):
        val = seg(c)
        for g, ref in enumerate(refs):
            piece = val[:, g * A_WIDTH:(g + 1) * A_WIDTH]
            if gain_ref is not None:
                ss = jnp.dot((piece * piece).astype(BF16), head_ones, preferred_element_type=F32)
                piece = piece * lax.rsqrt(ss * (1.0 / A_HEAD_DIM) + EPS) * gain_ref[:, g * A_WIDTH:(g + 1) * A_WIDTH]
            _split_residues(piece, DILATED_PATTERNS[g][1], ref, st_ref)
    sgm_ref[...] = jax.nn.sigmoid(seg(_C_GM)).astype(BF16)
    sga_ref[...] = jax.nn.sigmoid(seg(_C_GA)).astype(BF16)

    t = lax.dot_general(wt_ref[...], hb, _NT, preferred_element_type=F32)
    kT_ref[...] = t[0:M_QK_WIDTH].astype(BF16)
    zi = t[M_QK_WIDTH:M_QK_WIDTH + 8] + gb_ref[0:8]
    zf = t[M_QK_WIDTH + 8:M_QK_WIDTH + 16] + gb_ref[8:16]
    gi_ref[...] = GATE_SOFTCAP * jnp.tanh(zi / GATE_SOFTCAP)
    gf_ref[...] = _log_sigmoid(GATE_SOFTCAP * jnp.tanh(zf / GATE_SOFTCAP))


def _stage_inproj(x2d, norm1_g, w_in, gate_b, gq, gk, batch, seq, tm):
    n = x2d.shape[0]
    steps = seq // tm
    cuts = np.concatenate([[0], np.cumsum(IN_SPLITS)])
    col = lambda i: w_in[:, cuts[i]:cuts[i + 1]]
    wm = jnp.concatenate([col(0), col(2), col(3), col(5), col(6), col(7), col(8), col(9)],
                         axis=1).astype(BF16)
    wif = col(4)
    z4 = jnp.zeros((4, D_MODEL), w_in.dtype)
    wt = jnp.concatenate([col(1).T, wif[:, :M_HEADS].T, z4, wif[:, M_HEADS:].T, z4], axis=0).astype(BF16)
    gb = jnp.zeros((16, 1), F32)
    gb = gb.at[0:4, 0].set(gate_b[:M_HEADS].astype(F32)).at[8:12, 0].set(gate_b[M_HEADS:].astype(F32))
    g1 = norm1_g.astype(F32).reshape(1, D_MODEL)
    gq_t = (jnp.tile(gq.astype(F32), (1, A_HEADS)) * (A_HEAD_DIM ** -0.5)).reshape(1, N_GROUPS * A_WIDTH)
    gk_t = jnp.tile(gk.astype(F32), (1, A_HEADS)).reshape(1, N_GROUPS * A_WIDTH)

    row = lambda w: pl.BlockSpec((tm, w), lambda i: (i, 0))
    rowT = lambda r: pl.BlockSpec((r, tm), lambda i: (0, i))
    full = lambda a: pl.BlockSpec(a.shape, lambda i: (0,) * a.ndim)
    dils = [d for _, d in DILATED_PATTERNS]
    res_shape = lambda d: jax.ShapeDtypeStruct((batch, d, seq // d, A_WIDTH), BF16)
    res_spec = lambda d: pl.BlockSpec((1, d, tm // d, A_WIDTH), lambda i: (i // steps, 0, i % steps, 0))
    out_shapes = (
        jax.ShapeDtypeStruct((n, M_QK_WIDTH), BF16),
        jax.ShapeDtypeStruct((M_QK_WIDTH, n), BF16),
        jax.ShapeDtypeStruct((n, M_WIDTH), BF16),
        jax.ShapeDtypeStruct((n, M_WIDTH), BF16),
        jax.ShapeDtypeStruct((8, n), F32),
        jax.ShapeDtypeStruct((8, n), F32),
        *[res_shape(d) for d in dils], *[res_shape(d) for d in dils], *[res_shape(d) for d in dils],
        jax.ShapeDtypeStruct((n, D_MODEL), BF16),
        jax.ShapeDtypeStruct((n, D_MODEL), BF16),
    )
    out_specs = (row(M_QK_WIDTH), rowT(M_QK_WIDTH), row(M_WIDTH), row(M_WIDTH), rowT(8), rowT(8),
                 *[res_spec(d) for d in dils], *[res_spec(d) for d in dils], *[res_spec(d) for d in dils],
                 row(D_MODEL), row(D_MODEL))
    outs = pl.pallas_call(
        _inproj_body,
        grid=(n // tm,),
        in_specs=[row(D_MODEL), full(g1), full(wm), full(wt), full(gb), full(gq_t), full(gk_t)],
        out_specs=out_specs,
        out_shape=out_shapes,
        scratch_shapes=[pltpu.VMEM((2, tm, LANES), F32)],
        compiler_params=_params("parallel"),
        name="inproj",
    )(x2d, g1, wm, wt, gb, gq_t, gk_t)
    mq, kT, mv, so, gi, gf = outs[:6]
    aq, ak, av = outs[6:9], outs[9:12], outs[12:15]
    return mq, kT, mv, so, gi, gf, aq, ak, av, outs[15], outs[16]


M_CHUNK_LEN = 128


def _mlstm_body(q_ref, kT_ref, v_ref, so_ref, gi_ref, gf_ref, ng_ref, o_ref, c_ref, m_ref, *, nchunk):
    L = M_CHUNK_LEN

    @pl.when(pl.program_id(1) == 0)
    def _():
        c_ref[...] = jnp.zeros_like(c_ref)
        m_ref[...] = jnp.zeros_like(m_ref)

    lane8 = lax.broadcasted_iota(I32, (8, L), 1)
    causal = lax.broadcasted_iota(I32, (L, L), 1) <= lax.broadcasted_iota(I32, (L, L), 0)
    lo_half = lax.broadcasted_iota(I32, (L, LANES), 1) < M_QK_DIM
    ones = jnp.ones((L, M_V_DIM), BF16)

    heads = range(M_HEADS)
    cstate = [c_ref[h * M_QK_DIM:(h + 1) * M_QK_DIM, :] for h in heads]
    m_prev = m_ref[:, 0:1]
    chunks = []
    for c in range(nchunk):
        rows = slice(c * L, (c + 1) * L)
        gi = gi_ref[:, rows]
        b = gf_ref[:, rows]
        sh = 1
        while sh < L:
            b = b + jnp.where(lane8 >= sh, pltpu.roll(b, sh, 1), 0.0)
            sh *= 2
        u = gi - b
        g = b[:, L - 1:L]
        a = g + u
        amax = jnp.max(a, axis=1, keepdims=True)
        m_new = jnp.maximum(g + m_prev, amax)
        w = jnp.exp(a - m_new) * (M_QK_DIM ** -0.5)
        s_old = jnp.exp(g + m_prev - m_new)
        vext = [jnp.concatenate([v_ref[rows, h * M_V_DIM:(h + 1) * M_V_DIM], ones], axis=1) for h in heads]
        cloc = []
        for h in heads:
            hr = slice(h * M_QK_DIM, (h + 1) * M_QK_DIM)
            kw = (kT_ref[hr, rows].astype(F32) * w[h:h + 1, :]).astype(BF16)
            cloc.append(jnp.dot(kw, vext[h], preferred_element_type=F32))
        chunks.append(dict(rows=rows, b=b, u=u, m_prev=m_prev, state=cstate, vext=vext))
        cstate = [s_old[h:h + 1, :] * cstate[h] + cloc[h] for h in heads]
        m_prev = m_new
    for h in heads:
        c_ref[h * M_QK_DIM:(h + 1) * M_QK_DIM, :] = cstate[h]
    m_ref[...] = jnp.broadcast_to(m_prev, m_ref.shape)

    for ch in chunks:
        rows = ch["rows"]
        ch["s"], ch["qc"] = [], []
        for p in range(M_HEADS // 2):
            lanes_p = slice(p * LANES, (p + 1) * LANES)
            q_pair = q_ref[rows, lanes_p]
            kT_pair = kT_ref[lanes_p, rows]
            c_pair = jnp.concatenate([ch["state"][2 * p], ch["state"][2 * p + 1]], axis=0).astype(BF16)
            for hh in range(2):
                qm = jnp.where(lo_half if hh == 0 else jnp.logical_not(lo_half), q_pair, jnp.zeros_like(q_pair))
                ch["s"].append(jnp.dot(qm, kT_pair, preferred_element_type=F32) * (M_QK_DIM ** -0.5))
                ch["qc"].append(jnp.dot(qm, c_pair, preferred_element_type=F32))

    for ch in chunks:
        rows, b, u, m_prev = ch["rows"], ch["b"], ch["u"], ch["m_prev"]
        for h in heads:
            hl = slice(h * M_V_DIM, (h + 1) * M_V_DIM)
            bcol = jnp.transpose(jnp.broadcast_to(b[h:h + 1, :], (L, L)))
            dm = jnp.where(causal, bcol + u[h:h + 1, :], -jnp.inf)
            inter = bcol + m_prev[h:h + 1, :]
            m_t = jnp.maximum(inter, jnp.max(dm, axis=1, keepdims=True))
            pmat = (ch["s"][h] * jnp.exp(dm - m_t)).astype(BF16)
            sc = jnp.exp(inter - m_t)
            out = (jnp.dot(pmat, ch["vext"][h], preferred_element_type=F32)
                   + jnp.concatenate([sc, sc], axis=1) * ch["qc"][h])
            hv = out[:, :M_V_DIM] / jnp.maximum(jnp.abs(out[:, M_V_DIM:]), jnp.exp(-m_t))
            hn = hv * lax.rsqrt(jnp.mean(hv * hv, axis=1, keepdims=True) + EPS)
            hn = hn * ng_ref[:, hl] * so_ref[rows, hl].astype(F32)
            o_ref[rows, hl] = hn.astype(BF16)


def _stage_mlstm(mq, kT, mv, so, gi, gf, norm_g, batch, seq, rows_per_step):
    n = batch * seq
    R = rows_per_step
    steps = seq // R
    ng = norm_g.astype(F32).reshape(1, M_WIDTH)
    row = lambda w: pl.BlockSpec((R, w), lambda b, i: (b * steps + i, 0))
    rowT = lambda r: pl.BlockSpec((r, R), lambda b, i: (0, b * steps + i))
    return pl.pallas_call(
        functools.partial(_mlstm_body, nchunk=R // M_CHUNK_LEN),
        grid=(batch, steps),
        in_specs=[row(M_QK_WIDTH), rowT(M_QK_WIDTH), row(M_WIDTH), row(M_WIDTH), rowT(8), rowT(8),
                  pl.BlockSpec((1, M_WIDTH), lambda b, i: (0, 0))],
        out_specs=row(M_WIDTH),
        out_shape=jax.ShapeDtypeStruct((n, M_WIDTH), BF16),
        scratch_shapes=[pltpu.VMEM((M_QK_WIDTH, 2 * M_V_DIM), F32), pltpu.VMEM((8, LANES), F32)],
        compiler_params=_params("parallel", "arbitrary"),
        name="mlstm",
    )(mq, kT, mv, so, gi, gf, ng)


def _attn_body(q_ref, kp_ref, kc_ref, vp_ref, vc_ref, o_ref, lse_ref, *, dil, slopes, lq):
    QB = N_BACK
    first = pl.program_id(2) == 0
    qn = q_ref[0, 0]
    kcn = kc_ref[0, 0]
    kpn = kp_ref[0, 0]
    vc = vc_ref[0, 0]
    vp = vp_ref[0, 0]

    qi = lax.broadcasted_iota(I32, (QB, 2 * QB), 0)
    kj = lax.broadcasted_iota(I32, (QB, 2 * QB), 1)
    dist = qi + QB - kj
    band = jnp.logical_and(dist >= 0, dist <= N_BACK)
    distf = (dist * dil).astype(F32)
    bias = [jnp.where(band, -float(slopes[h]) * distf, -jnp.inf) for h in range(A_HEADS)]
    no_prev = jnp.logical_and(first, kj < QB)
    lo_half = lax.broadcasted_iota(I32, (QB, LANES), 1) < A_HEAD_DIM
    ones = jnp.ones((2 * QB, LANES), BF16)

    units = []
    for j in range(lq // QB):
        rows = slice(j * QB, (j + 1) * QB)
        prow = slice((j - 1) * QB, j * QB)
        keys = jnp.concatenate([kpn if j == 0 else kcn[prow], kcn[rows]], axis=0)
        vals = jnp.concatenate([vp if j == 0 else vc[prow], vc[rows]], axis=0)
        for p in range(A_HEADS // 2):
            lanes_p = slice(p * LANES, (p + 1) * LANES)
            q_pair = qn[rows, lanes_p]
            k_pair = keys[:, lanes_p]
            vext = jnp.concatenate([vals[:, lanes_p], ones], axis=1)
            scores = []
            for hh in range(2):
                sel = lo_half if hh == 0 else jnp.logical_not(lo_half)
                qm = jnp.where(sel, q_pair, jnp.zeros_like(q_pair))
                scores.append(lax.dot_general(qm, k_pair, _NT, preferred_element_type=F32))
            units.append((j, rows, lanes_p, p, vext, scores))

    for j, rows, lanes_p, p, vext, scores in units:
        o_pair = None
        l_pair = None
        for hh in range(2):
            s = scores[hh] + bias[2 * p + hh]
            if j == 0:
                s = jnp.where(no_prev, -jnp.inf, s)
            m = jnp.max(s, axis=1, keepdims=True)
            pv = jnp.dot(jnp.exp(s - m).astype(BF16), vext, preferred_element_type=F32)
            den = pv[:, LANES:]
            o_h = pv[:, :LANES] / den
            l_h = m + jnp.log(den)
            o_pair = o_h if hh == 0 else jnp.where(lo_half, o_pair, o_h)
            l_pair = l_h if hh == 0 else jnp.where(lo_half, l_pair, l_h)
        o_ref[0, 0, rows, lanes_p] = o_pair.astype(BF16)
        lse_ref[0, 0, rows, lanes_p] = l_pair


def _stage_attn(aq, ak, av, batch, seq, group):
    _, dil = DILATED_PATTERNS[group]
    L = seq // dil
    assert L % N_BACK == 0
    lq = min(1024, L)
    nq = L // lq
    sub = lq // N_BACK
    cur = pl.BlockSpec((1, 1, lq, A_WIDTH), lambda b, r, i: (b, r, i, 0))
    prev = pl.BlockSpec((1, 1, N_BACK, A_WIDTH), lambda b, r, i: (b, r, jnp.maximum(i * sub - 1, 0), 0))
    return pl.pallas_call(
        functools.partial(_attn_body, dil=dil, slopes=tuple(_alibi_slopes()[group]), lq=lq),
        grid=(batch, dil, nq),
        in_specs=[cur, prev, cur, prev, cur],
        out_specs=(cur, cur),
        out_shape=(jax.ShapeDtypeStruct((batch, dil, L, A_WIDTH), BF16),
                   jax.ShapeDtypeStruct((batch, dil, L, A_WIDTH), F32)),
        compiler_params=_params("parallel", "parallel", "parallel"),
        name=f"dilated_attn_d{dil}",
    )(aq, ak, ak, av, av)


TOKEN_TILE_ROWS = D_MODEL // LANES


def _store_token_tiles(ref, val):
    t = val.shape[0]
    for s in range(TOKEN_TILE_ROWS):
        ref[pl.ds(s, t, stride=TOKEN_TILE_ROWS), :] = val[:, s * LANES:(s + 1) * LANES]


def _load_token_tiles(ref, t):
    return jnp.concatenate([ref[pl.ds(s, t, stride=TOKEN_TILE_ROWS), :] for s in range(TOKEN_TILE_ROWS)], axis=1)


def _token_tile(ref, tok):
    start = tok * TOKEN_TILE_ROWS
    if not isinstance(tok, int):
        start = pl.multiple_of(start, TOKEN_TILE_ROWS)
    return ref.at[pl.ds(start, TOKEN_TILE_ROWS)]


def _rows8(vals):
    t = vals[0].shape[1]
    rid = lax.broadcasted_iota(I32, (8, t), 0)
    out = jnp.zeros((8, t), vals[0].dtype)
    for k, v in enumerate(vals):
        out = jnp.where(rid == k, jnp.broadcast_to(v, (8, t)), out)
    return out


def _merge_body(hm_ref, o1_ref, o2_ref, o3_ref, l1_ref, l2_ref, l3_ref, sgm_ref, sga_ref, x_ref,
                wm_ref, wa_ref, wo_ref, g2_ref, wrh_ref, br_ref,
                x2_ref, xn_ref, loc_ref, gate_ref, tcnt_ref, tcar_ref, cnt_ref, carry_ref, st_ref):
    @pl.when(pl.program_id(0) == 0)
    def _():
        carry_ref[...] = jnp.zeros_like(carry_ref)

    m_branch = jnp.dot(hm_ref[...], wm_ref[...], preferred_element_type=F32)
    dils = [d for _, d in DILATED_PATTERNS]
    l1, l2, l3 = [_merge_residues(r, d, st_ref) for r, d in zip((l1_ref, l2_ref, l3_ref), dils)]
    lmax = jnp.maximum(jnp.maximum(l1, l2), l3)
    e1, e2, e3 = jnp.exp(l1 - lmax), jnp.exp(l2 - lmax), jnp.exp(l3 - lmax)
    num = e1 * _merge_residues(o1_ref, dils[0], st_ref)
    num = num + e2 * _merge_residues(o2_ref, dils[1], st_ref)
    num = num + e3 * _merge_residues(o3_ref, dils[2], st_ref)
    h_a = num / (e1 + e2 + e3)
    y = (sgm_ref[...].astype(F32) * m_branch
         + sga_ref[...].astype(F32) * jnp.dot(h_a.astype(BF16), wa_ref[...], preferred_element_type=F32))
    x2 = x_ref[...] + jnp.dot(y.astype(BF16), wo_ref[...], preferred_element_type=F32)
    x2_ref[...] = x2
    xn = x2 * lax.rsqrt(jnp.mean(x2 * x2, axis=-1, keepdims=True) + EPS) * g2_ref[...]
    xh = xn.astype(BF16)
    xn_ref[...] = xh

    logits = lax.dot_general(wrh_ref[...], xh, _NT, preferred_element_type=F32) + br_ref[...]
    t = logits.shape[1]
    eid = lax.broadcasted_iota(I32, (N_EXPERTS, t), 0).astype(F32)
    vals = logits
    top_v, top_i = [], []
    for _ in range(TOP_K):
        mx = jnp.max(vals, axis=0, keepdims=True)
        ik = jnp.min(jnp.where(vals == mx, eid, float(N_EXPERTS)), axis=0, keepdims=True)
        top_v.append(mx)
        top_i.append(ik)
        vals = jnp.where(eid == ik, -jnp.inf, vals)
    ex = [jnp.exp(v - top_v[0]) for v in top_v]
    den = ex[0] + ex[1] + ex[2] + ex[3]
    gate_ref[...] = _rows8([e / den for e in ex])

    chosen = jnp.zeros((N_EXPERTS, t), F32)
    for ik in top_i:
        chosen = chosen + (eid == ik).astype(F32)
    before = (lax.broadcasted_iota(I32, (t, t), 0) < lax.broadcasted_iota(I32, (t, t), 1)).astype(BF16)
    prefix = jnp.dot(chosen.astype(BF16), before, preferred_element_type=F32)
    tcount = jnp.broadcast_to(jnp.sum(chosen, axis=1, keepdims=True), (N_EXPERTS, LANES))
    below = (lax.broadcasted_iota(I32, (N_EXPERTS, N_EXPERTS), 1)
             < lax.broadcasted_iota(I32, (N_EXPERTS, N_EXPERTS), 0)).astype(BF16)
    t_hi = jnp.floor(tcount * (1.0 / 256.0)) * 256.0
    tile_off = (jnp.dot(below, t_hi.astype(BF16), preferred_element_type=F32)
                + jnp.dot(below, (tcount - t_hi).astype(BF16), preferred_element_type=F32))
    pos = prefix + tile_off[:, 0:1]
    loc_ref[...] = _rows8([jnp.sum(jnp.where(eid == ik, pos, 0.0), axis=0, keepdims=True).astype(I32)
                           for ik in top_i])
    carry = carry_ref[...]
    tcnt_ref[...] = tcount.astype(I32)
    tcar_ref[...] = carry.astype(I32)
    total = carry + tcount
    carry_ref[...] = total
    cnt_ref[...] = total


def _stage_merge(h_m, attn, sgm, sga, x2d, w_mb, w_ab, w_out, norm2_g, w_router, b_router, batch, seq, tm):
    n = x2d.shape[0]
    steps = seq // tm
    (o1, l1), (o2, l2), (o3, l3) = attn
    wm = w_mb.astype(BF16)
    wa = w_ab.astype(BF16)
    wo = w_out.astype(BF16)
    g2 = norm2_g.astype(F32).reshape(1, D_MODEL)
    wrh = w_router.astype(BF16).T
    br = b_router.astype(F32).reshape(N_EXPERTS, 1)
    row = lambda w: pl.BlockSpec((tm, w), lambda i: (i, 0))
    rowT = lambda r: pl.BlockSpec((r, tm), lambda i: (0, i))
    full = lambda a: pl.BlockSpec(a.shape, lambda i: (0,) * a.ndim)
    res = lambda d: pl.BlockSpec((1, d, tm // d, A_WIDTH), lambda i: (i // steps, 0, i % steps, 0))
    dils = [d for _, d in DILATED_PATTERNS]
    per_tile = pl.BlockSpec((N_EXPERTS, LANES), lambda i: (0, i))
    return pl.pallas_call(
        _merge_body,
        grid=(n // tm,),
        in_specs=[row(M_WIDTH), *[res(d) for d in dils], *[res(d) for d in dils],
                  row(D_MODEL), row(D_MODEL), row(D_MODEL),
                  full(wm), full(wa), full(wo), full(g2), full(wrh), full(br)],
        out_specs=(row(D_MODEL), row(D_MODEL), rowT(8), rowT(8), per_tile, per_tile,
                   pl.BlockSpec((N_EXPERTS, LANES), lambda i: (0, 0))),
        out_shape=(jax.ShapeDtypeStruct((n, D_MODEL), F32),
                   jax.ShapeDtypeStruct((n, D_MODEL), BF16),
                   jax.ShapeDtypeStruct((8, n), I32),
                   jax.ShapeDtypeStruct((8, n), F32),
                   jax.ShapeDtypeStruct((N_EXPERTS, (n // tm) * LANES), I32),
                   jax.ShapeDtypeStruct((N_EXPERTS, (n // tm) * LANES), I32),
                   jax.ShapeDtypeStruct((N_EXPERTS, LANES), F32)),
        scratch_shapes=[pltpu.VMEM((N_EXPERTS, LANES), F32), pltpu.VMEM((2, tm, LANES), F32)],
        compiler_params=_params("arbitrary"),
        name="merge_route",
    )(h_m, o1, o2, o3, l1, l2, l3, sgm, sga, x2d, wm, wa, wo, g2, wrh, br)


def _offsets_body(cnt_ref, blk_ref, pstart_ref, zlo_ref, zhi_ref, *, nblk_pad):
    cnt = cnt_ref[...]
    padded = jnp.floor((cnt + (MOE_BLOCK - 1)) * (1.0 / MOE_BLOCK)) * MOE_BLOCK
    lower = (lax.broadcasted_iota(I32, (N_EXPERTS, N_EXPERTS), 1)
             <= lax.broadcasted_iota(I32, (N_EXPERTS, N_EXPERTS), 0)).astype(F32)
    pends = jnp.dot(lower, padded, precision=lax.Precision.HIGHEST, preferred_element_type=F32)
    pstart = pends - padded
    pstart_ref[...] = pstart.astype(I32)
    zlo_ref[...] = (pstart + cnt).astype(I32)
    zhi_ref[...] = pends.astype(I32)

    first_row = (lax.broadcasted_iota(I32, (N_EXPERTS, nblk_pad), 1) * MOE_BLOCK).astype(F32)
    pe = jnp.broadcast_to(pends[:, 0:1], (N_EXPERTS, nblk_pad))
    be = jnp.sum((pe <= first_row).astype(F32), axis=0, keepdims=True)
    be = jnp.minimum(be, float(N_EXPERTS - 1))
    nused = pends[N_EXPERTS - 1:N_EXPERTS, 0:1] * (1.0 / MOE_BLOCK)
    nonempty = jnp.broadcast_to(padded[:, 0:1], (N_EXPERTS, nblk_pad)) > 0.0
    runidx = jnp.sum(jnp.logical_and(pe <= first_row, nonempty).astype(F32), axis=0, keepdims=True)
    parity = runidx - 2.0 * jnp.floor(runidx * 0.5)
    eid = lax.broadcasted_iota(I32, (N_EXPERTS, nblk_pad), 0).astype(F32)
    later = jnp.logical_and(eid > be, nonempty)
    nxt = jnp.min(jnp.where(later, eid, float(N_EXPERTS)), axis=0, keepdims=True)
    blk_ref[...] = _rows8([be.astype(I32), jnp.broadcast_to(nused, (1, nblk_pad)).astype(I32),
                           parity.astype(I32), nxt.astype(I32)])


def _stage_offsets(cnt, nblk):
    nblk_pad = -(-nblk // LANES) * LANES
    const = lambda r, c: pl.BlockSpec((r, c), lambda i: (0, 0))
    per_expert = jax.ShapeDtypeStruct((N_EXPERTS, LANES), I32)
    return pl.pallas_call(
        functools.partial(_offsets_body, nblk_pad=nblk_pad),
        grid=(1,),
        in_specs=[const(N_EXPERTS, LANES)],
        out_specs=(const(8, nblk_pad), const(N_EXPERTS, LANES), const(N_EXPERTS, LANES), const(N_EXPERTS, LANES)),
        out_shape=(jax.ShapeDtypeStruct((8, nblk_pad), I32), per_expert, per_expert, per_expert),
        compiler_params=_params("arbitrary"),
        name="route_offsets",
    )(cnt)


RUN_BITS = 10


def _tile_rows(ref, first_row, nrows):
    start = first_row * TOKEN_TILE_ROWS
    if not isinstance(first_row, int):
        start = pl.multiple_of(start, TOKEN_TILE_ROWS)
    return ref.at[pl.ds(start, nrows * TOKEN_TILE_ROWS)]


def _for_each_piece(length, fn):
    for b in reversed(range(RUN_BITS)):
        @pl.when(((length >> b) & 1) == 1)
        def _(b=b):
            fn((length >> (b + 1)) << (b + 1), 1 << b)


def _for_each_run(tile, tcnt_ref, tcar_ref, pstart_ref, fn):
    def per_expert(e, local):
        count = tcnt_ref[tile, e]
        first = pstart_ref[e] + tcar_ref[tile, e]
        _for_each_piece(count, lambda off, size: fn(local + off, first + off, size))
        return local + count

    lax.fori_loop(0, N_EXPERTS, per_expert, 0)


PERM_CHUNK = 256


def _dispatch_body(tcnt_ref, tcar_ref, pstart_ref, zlo_ref, zhi_ref, loc_ref, xn_ref, xs_hbm,
                   buf_ref, sems, *, tm):
    step = pl.program_id(0)
    nloc_tiles = TOP_K * tm * TOKEN_TILE_ROWS

    def wait_buffer(slot):
        pltpu.make_async_copy(buf_ref.at[slot], xs_hbm.at[pl.ds(0, nloc_tiles)], sems.at[slot]).wait()

    for slot in range(2):
        tile = 2 * step + slot

        @pl.when(step > 0)
        def _(slot=slot):
            wait_buffer(slot)

        loc = loc_ref[:, slot * tm:(slot + 1) * tm]
        xn = xn_ref[slot * tm:(slot + 1) * tm, :]
        for c in range(TOP_K * tm // PERM_CHUNK):
            lid = lax.broadcasted_iota(I32, (PERM_CHUNK, tm), 0) + c * PERM_CHUNK
            hit = lid == loc[0:1, :]
            for k in range(1, TOP_K):
                hit = jnp.logical_or(hit, lid == loc[k:k + 1, :])
            rows = jnp.dot(jnp.where(hit, 1.0, 0.0).astype(BF16), xn, preferred_element_type=F32)
            _store_token_tiles(
                buf_ref.at[slot, pl.ds(c * PERM_CHUNK * TOKEN_TILE_ROWS, PERM_CHUNK * TOKEN_TILE_ROWS)], rows)

        def run_copy(local, first, size, slot=slot):
            return pltpu.make_async_copy(_tile_rows(buf_ref.at[slot], local, size),
                                         _tile_rows(xs_hbm, first, size), sems.at[slot])

        _for_each_run(tile, tcnt_ref, tcar_ref, pstart_ref, lambda l, f, s: run_copy(l, f, s).start())

    @pl.when(step == pl.num_programs(0) - 1)
    def _():
        wait_buffer(0)
        wait_buffer(1)
        zsrc = buf_ref.at[0]
        zsrc[pl.ds(0, MOE_BLOCK * TOKEN_TILE_ROWS), :] = jnp.zeros((MOE_BLOCK * TOKEN_TILE_ROWS, LANES), F32)

        def zero_copy(first, size):
            return pltpu.make_async_copy(_tile_rows(zsrc, 0, size), _tile_rows(xs_hbm, first, size), sems.at[0])

        def per_expert(e, carry):
            lo = zlo_ref[e]
            npad = zhi_ref[e] - lo
            _for_each_piece(npad, lambda off, size: zero_copy(lo + off, size).start())
            _for_each_piece(npad, lambda off, size: zero_copy(lo + off, size).wait())
            return carry

        lax.fori_loop(0, N_EXPERTS, per_expert, 0)

        first_unused = zhi_ref[N_EXPERTS - 1] // MOE_BLOCK
        nblk = xs_hbm.shape[0] // (MOE_BLOCK * TOKEN_TILE_ROWS)

        def tail(blk, carry):
            zero_copy(blk * MOE_BLOCK, MOE_BLOCK).start()
            zero_copy(blk * MOE_BLOCK, MOE_BLOCK).wait()
            return carry

        lax.fori_loop(first_unused, nblk, tail, 0)


def _stage_dispatch(tables, loc8, xn, nrows, tm):
    n = xn.shape[0]
    assert TOP_K * tm >= MOE_BLOCK and (n // tm) % 2 == 0
    grid_spec = pltpu.PrefetchScalarGridSpec(
        num_scalar_prefetch=5,
        grid=(n // (2 * tm),),
        in_specs=[pl.BlockSpec((8, 2 * tm), lambda i, *_: (0, i)),
                  pl.BlockSpec((2 * tm, D_MODEL), lambda i, *_: (i, 0))],
        out_specs=pl.BlockSpec(memory_space=pl.ANY),
        scratch_shapes=[pltpu.VMEM((2, TOP_K * tm * TOKEN_TILE_ROWS, LANES), F32), pltpu.SemaphoreType.DMA((2,))],
    )
    return pl.pallas_call(
        functools.partial(_dispatch_body, tm=tm),
        grid_spec=grid_spec,
        out_shape=jax.ShapeDtypeStruct((nrows * TOKEN_TILE_ROWS, LANES), F32),
        compiler_params=_params("arbitrary"),
        name="dispatch",
    )(*tables, loc8, xn)


def _expert_body(be_ref, nu_ref, par_ref, nxt_ref, xs_ref, w1_hbm, b1_ref, w2_hbm, b2_ref, ys_ref,
                 w1f_ref, w2f_ref, w1b_ref, w2b_ref, sems):
    j = pl.program_id(0)
    used = j < nu_ref[0]
    jj = jnp.maximum(jnp.minimum(j, nu_ref[0] - 1), 0)
    e = be_ref[jj]
    fresh = jnp.logical_or(j == 0, e != be_ref[jnp.maximum(jj - 1, 0)])

    def fetch(expert, slot):
        return (pltpu.make_async_copy(w1_hbm.at[expert], w1f_ref.at[slot], sems.at[slot, 0]),
                pltpu.make_async_copy(w2_hbm.at[expert], w2f_ref.at[slot], sems.at[slot, 1]))

    @pl.when(jnp.logical_and(used, fresh))
    def _():
        slot = par_ref[jj]

        @pl.when(j == 0)
        def _():
            for c in fetch(e, slot):
                c.start()

        for c in fetch(e, slot):
            c.wait()
        nxt = nxt_ref[jj]

        @pl.when(nxt < N_EXPERTS)
        def _():
            for c in fetch(nxt, 1 - slot):
                c.start()

        w1b_ref[...] = w1f_ref[slot].astype(BF16)
        w2b_ref[...] = w2f_ref[slot].astype(BF16)

    @pl.when(used)
    def _():
        xb = _load_token_tiles(xs_ref, MOE_BLOCK).astype(BF16)
        gu = jnp.dot(xb, w1b_ref[...], preferred_element_type=F32) + b1_ref[0]
        gate = jnp.minimum(gu[:, :D_FF], SWIGLU_LIMIT)
        lin = jnp.clip(gu[:, D_FF:], -SWIGLU_LIMIT, SWIGLU_LIMIT)
        act = (lin + 1.0) * (gate * jax.nn.sigmoid(SWIGLU_ALPHA * gate))
        ys = jnp.dot(act.astype(BF16), w2b_ref[...], preferred_element_type=F32) + b2_ref[0]
        _store_token_tiles(ys_ref, ys)

    @pl.when(jnp.logical_not(used))
    def _():
        ys_ref[...] = jnp.zeros_like(ys_ref)


def _stage_experts(blk8, xs, w1, b1, w2, b2):
    nrows = xs.shape[0] // TOKEN_TILE_ROWS
    nblk = nrows // MOE_BLOCK
    block_e, nused, parity, nxt = blk8[0, :nblk], blk8[1, :1], blk8[2, :nblk], blk8[3, :nblk]
    blk = lambda j, be, nu, *_: jnp.maximum(jnp.minimum(j, nu[0] - 1), 0)
    exp = lambda j, be, nu, *_: be[blk(j, be, nu)]
    tiles = (MOE_BLOCK * TOKEN_TILE_ROWS, LANES)
    grid_spec = pltpu.PrefetchScalarGridSpec(
        num_scalar_prefetch=4,
        grid=(nblk,),
        in_specs=[pl.BlockSpec(tiles, lambda j, *t: (blk(j, *t), 0)),
                  pl.BlockSpec(memory_space=pl.ANY),
                  pl.BlockSpec((1, 1, 2 * D_FF), lambda j, *t: (exp(j, *t), 0, 0)),
                  pl.BlockSpec(memory_space=pl.ANY),
                  pl.BlockSpec((1, 1, D_MODEL), lambda j, *t: (exp(j, *t), 0, 0))],
        out_specs=pl.BlockSpec(tiles, lambda j, *t: (j, 0)),
        scratch_shapes=[pltpu.VMEM((2, D_MODEL, 2 * D_FF), F32), pltpu.VMEM((2, D_FF, D_MODEL), F32),
                        pltpu.VMEM((D_MODEL, 2 * D_FF), BF16), pltpu.VMEM((D_FF, D_MODEL), BF16),
                        pltpu.SemaphoreType.DMA((2, 2))],
    )
    return pl.pallas_call(
        _expert_body,
        grid_spec=grid_spec,
        out_shape=jax.ShapeDtypeStruct((nrows * TOKEN_TILE_ROWS, LANES), F32),
        compiler_params=_params("arbitrary"),
        name="experts",
    )(block_e, nused, parity, nxt, xs, w1, b1.reshape(N_EXPERTS, 1, 2 * D_FF), w2, b2.reshape(N_EXPERTS, 1, D_MODEL))


def _combine_body(tcnt_ref, tcar_ref, pstart_ref, loc_ref, gate_ref, x2_ref, ys_hbm, out_ref,
                  buf_ref, g_ref, sems, *, tm):
    step = pl.program_id(0)
    nloc = TOP_K * tm

    def start_runs(tile, slot):
        def run_copy(local, first, size):
            return pltpu.make_async_copy(_tile_rows(ys_hbm, first, size),
                                         _tile_rows(buf_ref.at[slot], local, size), sems.at[slot])
        _for_each_run(tile, tcnt_ref, tcar_ref, pstart_ref, lambda l, f, s: run_copy(l, f, s).start())

    def wait_buffer(slot):
        pltpu.make_async_copy(ys_hbm.at[pl.ds(0, nloc * TOKEN_TILE_ROWS)], buf_ref.at[slot], sems.at[slot]).wait()

    def combine(slot):
        zpad = jnp.zeros((LANES - 16, LANES), F32)
        lane = lax.broadcasted_iota(I32, (LANES, nloc), 1).astype(F32)
        for c in range(tm // LANES):
            cols_in = slice(slot * tm + c * LANES, slot * tm + (c + 1) * LANES)
            cols = jnp.transpose(jnp.concatenate([loc_ref[:, cols_in].astype(F32), gate_ref[:, cols_in], zpad], axis=0))
            g = jnp.zeros((LANES, nloc), F32)
            for k in range(TOP_K):
                g = jnp.where(lane == cols[:, k:k + 1], cols[:, 8 + k:9 + k], g)
            g_ref[c * LANES:(c + 1) * LANES, :] = g.astype(BF16)
        wait_buffer(slot)
        ys = _load_token_tiles(buf_ref.at[slot], nloc).astype(BF16)
        rows = slice(slot * tm, (slot + 1) * tm)
        out_ref[rows, :] = x2_ref[rows, :] + jnp.dot(g_ref[...], ys, preferred_element_type=F32)

    @pl.when(step == 0)
    def _():
        start_runs(0, 0)

    start_runs(2 * step + 1, 1)
    combine(0)

    @pl.when(step + 1 < pl.num_programs(0))
    def _():
        start_runs(2 * step + 2, 0)

    combine(1)


def _stage_combine(tables, loc8, gate8, x2, ys, tm):
    n = x2.shape[0]
    assert (n // tm) % 2 == 0
    grid_spec = pltpu.PrefetchScalarGridSpec(
        num_scalar_prefetch=3,
        grid=(n // (2 * tm),),
        in_specs=[pl.BlockSpec((8, 2 * tm), lambda i, *_: (0, i)),
                  pl.BlockSpec((8, 2 * tm), lambda i, *_: (0, i)),
                  pl.BlockSpec((2 * tm, D_MODEL), lambda i, *_: (i, 0)),
                  pl.BlockSpec(memory_space=pl.ANY)],
        out_specs=pl.BlockSpec((2 * tm, D_MODEL), lambda i, *_: (i, 0)),
        scratch_shapes=[pltpu.VMEM((2, TOP_K * tm * TOKEN_TILE_ROWS, LANES), F32),
                        pltpu.VMEM((tm, TOP_K * tm), BF16),
                        pltpu.SemaphoreType.DMA((2,))],
    )
    return pl.pallas_call(
        functools.partial(_combine_body, tm=tm),
        grid_spec=grid_spec,
        out_shape=jax.ShapeDtypeStruct((n, D_MODEL), F32),
        compiler_params=_params("arbitrary"),
        name="combine",
    )(*tables, loc8, gate8, x2, ys)


def _moe(x2, xn, loc8, gate8, tcnt, tcar, cnt, w1, b1, w2, b2, tm):
    n = x2.shape[0]
    ntile = n // tm
    nblk = -(-(n * TOP_K) // MOE_BLOCK) + N_EXPERTS
    blk8, pstart, zlo, zhi = _stage_offsets(cnt, nblk)
    per_tile = lambda a: a.reshape(N_EXPERTS, ntile, LANES)[:, :, 0].T
    tables = (per_tile(tcnt), per_tile(tcar), pstart[:, 0])
    xs = _stage_dispatch(tables + (zlo[:, 0], zhi[:, 0]), loc8, xn, nblk * MOE_BLOCK, tm)
    ys = _stage_experts(blk8, xs, w1, b1, w2, b2)
    return _stage_combine(tables, loc8, gate8, x2, ys, tm)


def kernel(x, norm1_g, w_in, mlstm_gate_b, mlstm_norm_g, attn_q_norm_g, attn_k_norm_g, w_mlstm_branch,
           w_attn_branch, w_out, norm2_g, w_router, b_router, w1, b1, w2, b2):
    batch, seq, _ = x.shape
    n = batch * seq
    for l in range(norm1_g.shape[0]):
        x2d = x.reshape(n, D_MODEL)
        tm = min(512, seq)
        mq, kT, mv, so, gi, gf, aq, ak, av, sgm, sga = _stage_inproj(
            x2d, norm1_g[l], w_in[l], mlstm_gate_b[l], attn_q_norm_g[l], attn_k_norm_g[l], batch, seq, tm)
        h_m = _stage_mlstm(mq, kT, mv, so, gi, gf, mlstm_norm_g[l], batch, seq, tm)
        attn = [_stage_attn(aq[g], ak[g], av[g], batch, seq, g)
                for g in range(N_GROUPS)]
        x2, xn, loc8, gate8, tcnt, tcar, cnt = _stage_merge(
            h_m, attn, sgm, sga, x2d, w_mlstm_branch[l], w_attn_branch[l], w_out[l], norm2_g[l],
            w_router[l], b_router[l], batch, seq, tm)
        out = _moe(x2, xn, loc8, gate8, tcnt, tcar, cnt, w1[l], b1[l], w2[l], b2[l], tm)
        x = out.reshape(batch, seq, D_MODEL)
    return x
```

```python
import functools

import numpy as np
import jax
import jax.numpy as jnp
from jax import lax
from jax.experimental import pallas as pl
from jax.experimental.pallas import tpu as pltpu

F32 = jnp.float32
BF16 = jnp.bfloat16
I32 = jnp.int32

D_MODEL = 1024
M_HEADS = 4
M_QK_DIM = 64
M_V_DIM = 128
GATE_SOFTCAP = 15.0
A_HEADS = 4
A_HEAD_DIM = 64
DILATED_PATTERNS = ((128, 1), (512, 4), (2048, 16))
N_GROUPS = len(DILATED_PATTERNS)
N_BACK = 128
N_EXPERTS = 32
TOP_K = 4
D_FF = 1024
SWIGLU_LIMIT = 7.0
SWIGLU_ALPHA = 1.702
MOE_BLOCK = 512
EPS = 1e-6

M_WIDTH = M_HEADS * M_V_DIM
M_QK_WIDTH = M_HEADS * M_QK_DIM
A_WIDTH = A_HEADS * A_HEAD_DIM
IN_SPLITS = (M_QK_WIDTH, M_QK_WIDTH, M_WIDTH, M_WIDTH, 2 * M_HEADS,
             N_GROUPS * A_WIDTH, N_GROUPS * A_WIDTH, N_GROUPS * A_WIDTH, D_MODEL, D_MODEL)

LANES = 128
VMEM_LIMIT = 56 * 1024 * 1024

_NT = (((1,), (1,)), ((), ()))


def _alibi_slopes():
    n = N_GROUPS * A_HEADS
    s = np.exp2(-8.0 * np.arange(1, n + 1) / n).astype(np.float32)
    return s.reshape(N_GROUPS, A_HEADS)


def _params(*sem):
    return pltpu.CompilerParams(dimension_semantics=sem, vmem_limit_bytes=VMEM_LIMIT)


def _log_sigmoid(x):
    return jnp.minimum(x, 0.0) - jnp.log1p(jnp.exp(-jnp.abs(x)))


_GATE_PAD = LANES - 2 * M_HEADS


def _segments():
    bounds, start = [], 0
    for i, width in enumerate(IN_SPLITS):
        bounds.append((start, start + width))
        start += width + (_GATE_PAD if i == 4 else 0)
    return bounds, start


_SEG, _W_MAIN = _segments()
_C_MQ, _C_MV, _C_MO, _C_AQ, _C_AK, _C_AV, _C_GM, _C_GA = (_SEG[i] for i in (0, 2, 3, 5, 6, 7, 8, 9))
_WT_ROWS = M_QK_WIDTH + 16


def _split_residues(val, d, out_ref, st_ref):
    t = val.shape[0]
    if d == 1:
        out_ref[0, 0] = val.astype(out_ref.dtype)
        return
    st_ref[0] = val[:, :LANES]
    st_ref[1] = val[:, LANES:]
    for r in range(d):
        piece = jnp.concatenate([st_ref[0, pl.ds(r, t // d, stride=d), :],
                                 st_ref[1, pl.ds(r, t // d, stride=d), :]], axis=1)
        out_ref[0, r] = piece.astype(out_ref.dtype)


def _merge_residues(ref, d, st_ref):
    if d == 1:
        return ref[0, 0].astype(F32)
    m = ref.shape[2]
    for r in range(d):
        blk = ref[0, r].astype(F32)
        st_ref[0, pl.ds(r, m, stride=d), :] = blk[:, :LANES]
        st_ref[1, pl.ds(r, m, stride=d), :] = blk[:, LANES:]
    return jnp.concatenate([st_ref[0], st_ref[1]], axis=1)


def _inproj_body(x_ref, g1_ref, wm_ref, wt_ref, gb_ref, gq_ref, gk_ref,
                 mq_ref, kT_ref, mv_ref, so_ref, gi_ref, gf_ref,
                 q0_ref, q1_ref, q2_ref, k0_ref, k1_ref, k2_ref, v0_ref, v1_ref, v2_ref,
                 sgm_ref, sga_ref, st_ref):
    x = x_ref[...]
    h = x * lax.rsqrt(jnp.mean(x * x, axis=-1, keepdims=True) + EPS) * g1_ref[...]
    hb = h.astype(BF16)

    def seg(c):
        return jnp.dot(hb, wm_ref[:, c[0]:c[1]], preferred_element_type=F32)

    mq_ref[...] = seg(_C_MQ).astype(BF16)
    mv_ref[...] = seg(_C_MV).astype(BF16)
    so_ref[...] = jax.nn.sigmoid(seg(_C_MO)).astype(BF16)
    hid_r = lax.broadcasted_iota(I32, (A_WIDTH, A_WIDTH), 0) // A_HEAD_DIM
    hid_c = lax.broadcasted_iota(I32, (A_WIDTH, A_WIDTH), 1) // A_HEAD_DIM
    head_ones = (hid_r == hid_c).astype(BF16)
    for c, refs, gain_ref in ((_C_AQ, (q0_ref, q1_ref, q2_ref), gq_ref), (_C_AK, (k0_ref, k1_ref, k2_ref), gk_ref),
                              (_C_AV, (v0_ref, v1_ref, v2_ref), None)):
        val = seg(c)
        for g, ref in enumerate(refs):
            piece = val[:, g * A_WIDTH:(g + 1) * A_WIDTH]
            if gain_ref is not None:
                ss = jnp.dot((piece * piece).astype(BF16), head_ones, preferred_element_type=F32)
                piece = piece * lax.rsqrt(ss * (1.0 / A_HEAD_DIM) + EPS) * gain_ref[:, g * A_WIDTH:(g + 1) * A_WIDTH]
            _split_residues(piece, DILATED_PATTERNS[g][1], ref, st_ref)
    sgm_ref[...] = jax.nn.sigmoid(seg(_C_GM)).astype(BF16)
    sga_ref[...] = jax.nn.sigmoid(seg(_C_GA)).astype(BF16)

    t = lax.dot_general(wt_ref[...], hb, _NT, preferred_element_type=F32)
    kT_ref[...] = t[0:M_QK_WIDTH].astype(BF16)
    zi = t[M_QK_WIDTH:M_QK_WIDTH + 8] + gb_ref[0:8]
    zf = t[M_QK_WIDTH + 8:M_QK_WIDTH + 16] + gb_ref[8:16]
    gi_ref[...] = GATE_SOFTCAP * jnp.tanh(zi / GATE_SOFTCAP)
    gf_ref[...] = _log_sigmoid(GATE_SOFTCAP * jnp.tanh(zf / GATE_SOFTCAP))


def _stage_inproj(x2d, norm1_g, w_in, gate_b, gq, gk, batch, seq, tm):
    n = x2d.shape[0]
    steps = seq // tm
    cuts = np.concatenate([[0], np.cumsum(IN_SPLITS)])
    col = lambda i: w_in[:, cuts[i]:cuts[i + 1]]
    gate_end = int(cuts[5])
    wm = jnp.concatenate([w_in[:, :gate_end], jnp.zeros((D_MODEL, _GATE_PAD), w_in.dtype), w_in[:, gate_end:]],
                         axis=1).astype(BF16)
    wif = col(4)
    z4 = jnp.zeros((4, D_MODEL), w_in.dtype)
    wt = jnp.concatenate([col(1).T, wif[:, :M_HEADS].T, z4, wif[:, M_HEADS:].T, z4], axis=0).astype(BF16)
    gb = jnp.zeros((16, 1), F32)
    gb = gb.at[0:4, 0].set(gate_b[:M_HEADS].astype(F32)).at[8:12, 0].set(gate_b[M_HEADS:].astype(F32))
    g1 = norm1_g.astype(F32).reshape(1, D_MODEL)
    gq_t = (jnp.tile(gq.astype(F32), (1, A_HEADS)) * (A_HEAD_DIM ** -0.5)).reshape(1, N_GROUPS * A_WIDTH)
    gk_t = jnp.tile(gk.astype(F32), (1, A_HEADS)).reshape(1, N_GROUPS * A_WIDTH)

    row = lambda w: pl.BlockSpec((tm, w), lambda i: (i, 0))
    rowT = lambda r: pl.BlockSpec((r, tm), lambda i: (0, i))
    full = lambda a: pl.BlockSpec(a.shape, lambda i: (0,) * a.ndim)
    dils = [d for _, d in DILATED_PATTERNS]
    res_shape = lambda d: jax.ShapeDtypeStruct((batch, d, seq // d, A_WIDTH), BF16)
    res_spec = lambda d: pl.BlockSpec((1, d, tm // d, A_WIDTH), lambda i: (i // steps, 0, i % steps, 0))
    out_shapes = (
        jax.ShapeDtypeStruct((n, M_QK_WIDTH), BF16),
        jax.ShapeDtypeStruct((M_QK_WIDTH, n), BF16),
        jax.ShapeDtypeStruct((n, M_WIDTH), BF16),
        jax.ShapeDtypeStruct((n, M_WIDTH), BF16),
        jax.ShapeDtypeStruct((8, n), F32),
        jax.ShapeDtypeStruct((8, n), F32),
        *[res_shape(d) for d in dils], *[res_shape(d) for d in dils], *[res_shape(d) for d in dils],
        jax.ShapeDtypeStruct((n, D_MODEL), BF16),
        jax.ShapeDtypeStruct((n, D_MODEL), BF16),
    )
    out_specs = (row(M_QK_WIDTH), rowT(M_QK_WIDTH), row(M_WIDTH), row(M_WIDTH), rowT(8), rowT(8),
                 *[res_spec(d) for d in dils], *[res_spec(d) for d in dils], *[res_spec(d) for d in dils],
                 row(D_MODEL), row(D_MODEL))
    outs = pl.pallas_call(
        _inproj_body,
        grid=(n // tm,),
        in_specs=[row(D_MODEL), full(g1), full(wm), full(wt), full(gb), full(gq_t), full(gk_t)],
        out_specs=out_specs,
        out_shape=out_shapes,
        scratch_shapes=[pltpu.VMEM((2, tm, LANES), F32)],
        compiler_params=_params("parallel"),
        name="inproj",
    )(x2d, g1, wm, wt, gb, gq_t, gk_t)
    mq, kT, mv, so, gi, gf = outs[:6]
    aq, ak, av = outs[6:9], outs[9:12], outs[12:15]
    return mq, kT, mv, so, gi, gf, aq, ak, av, outs[15], outs[16]


M_CHUNK_LEN = 128


def _mlstm_body(q_ref, kT_ref, v_ref, so_ref, gi_ref, gf_ref, ng_ref, o_ref, c_ref, m_ref, *, nchunk):
    L = M_CHUNK_LEN

    @pl.when(pl.program_id(1) == 0)
    def _():
        c_ref[...] = jnp.zeros_like(c_ref)
        m_ref[...] = jnp.zeros_like(m_ref)

    lane8 = lax.broadcasted_iota(I32, (8, L), 1)
    causal = lax.broadcasted_iota(I32, (L, L), 1) <= lax.broadcasted_iota(I32, (L, L), 0)
    lo_half = lax.broadcasted_iota(I32, (L, LANES), 1) < M_QK_DIM
    ones = jnp.ones((L, M_V_DIM), BF16)

    heads = range(M_HEADS)
    cstate = [c_ref[h * M_QK_DIM:(h + 1) * M_QK_DIM, :] for h in heads]
    m_prev = m_ref[:, 0:1]
    chunks = []
    for c in range(nchunk):
        rows = slice(c * L, (c + 1) * L)
        gi = gi_ref[:, rows]
        b = gf_ref[:, rows]
        sh = 1
        while sh < L:
            b = b + jnp.where(lane8 >= sh, pltpu.roll(b, sh, 1), 0.0)
            sh *= 2
        u = gi - b
        g = b[:, L - 1:L]
        a = g + u
        amax = jnp.max(a, axis=1, keepdims=True)
        m_new = jnp.maximum(g + m_prev, amax)
        w = jnp.exp(a - m_new) * (M_QK_DIM ** -0.5)
        s_old = jnp.exp(g + m_prev - m_new)
        vext = [jnp.concatenate([v_ref[rows, h * M_V_DIM:(h + 1) * M_V_DIM], ones], axis=1) for h in heads]
        cloc = []
        for h in heads:
            hr = slice(h * M_QK_DIM, (h + 1) * M_QK_DIM)
            kw = (kT_ref[hr, rows].astype(F32) * w[h:h + 1, :]).astype(BF16)
            cloc.append(jnp.dot(kw, vext[h], preferred_element_type=F32))
        chunks.append(dict(rows=rows, b=b, u=u, m_prev=m_prev, state=cstate, vext=vext))
        cstate = [s_old[h:h + 1, :] * cstate[h] + cloc[h] for h in heads]
        m_prev = m_new
    for h in heads:
        c_ref[h * M_QK_DIM:(h + 1) * M_QK_DIM, :] = cstate[h]
    m_ref[...] = jnp.broadcast_to(m_prev, m_ref.shape)

    for ch in chunks:
        rows = ch["rows"]
        ch["s"], ch["qc"] = [], []
        for p in range(M_HEADS // 2):
            lanes_p = slice(p * LANES, (p + 1) * LANES)
            q_pair = q_ref[rows, lanes_p]
            kT_pair = kT_ref[lanes_p, rows]
            c_pair = jnp.concatenate([ch["state"][2 * p], ch["state"][2 * p + 1]], axis=0).astype(BF16)
            for hh in range(2):
                qm = jnp.where(lo_half if hh == 0 else jnp.logical_not(lo_half), q_pair, jnp.zeros_like(q_pair))
                ch["s"].append(jnp.dot(qm, kT_pair, preferred_element_type=F32) * (M_QK_DIM ** -0.5))
                ch["qc"].append(jnp.dot(qm, c_pair, preferred_element_type=F32))

    for ch in chunks:
        rows, b, u, m_prev = ch["rows"], ch["b"], ch["u"], ch["m_prev"]
        for h in heads:
            hl = slice(h * M_V_DIM, (h + 1) * M_V_DIM)
            bcol = jnp.transpose(jnp.broadcast_to(b[h:h + 1, :], (L, L)))
            dm = jnp.where(causal, bcol + u[h:h + 1, :], -jnp.inf)
            inter = bcol + m_prev[h:h + 1, :]
            m_t = jnp.maximum(inter, jnp.max(dm, axis=1, keepdims=True))
            pmat = (ch["s"][h] * jnp.exp(dm - m_t)).astype(BF16)
            sc = jnp.exp(inter - m_t)
            out = (jnp.dot(pmat, ch["vext"][h], preferred_element_type=F32)
                   + jnp.concatenate([sc, sc], axis=1) * ch["qc"][h])
            hv = out[:, :M_V_DIM] / jnp.maximum(jnp.abs(out[:, M_V_DIM:]), jnp.exp(-m_t))
            hn = hv * lax.rsqrt(jnp.mean(hv * hv, axis=1, keepdims=True) + EPS)
            hn = hn * ng_ref[:, hl] * so_ref[rows, hl].astype(F32)
            o_ref[rows, hl] = hn.astype(BF16)


def _stage_mlstm(mq, kT, mv, so, gi, gf, norm_g, batch, seq, rows_per_step):
    n = batch * seq
    R = rows_per_step
    steps = seq // R
    ng = norm_g.astype(F32).reshape(1, M_WIDTH)
    row = lambda w: pl.BlockSpec((R, w), lambda b, i: (b * steps + i, 0))
    rowT = lambda r: pl.BlockSpec((r, R), lambda b, i: (0, b * steps + i))
    return pl.pallas_call(
        functools.partial(_mlstm_body, nchunk=R // M_CHUNK_LEN),
        grid=(batch, steps),
        in_specs=[row(M_QK_WIDTH), rowT(M_QK_WIDTH), row(M_WIDTH), row(M_WIDTH), rowT(8), rowT(8),
                  pl.BlockSpec((1, M_WIDTH), lambda b, i: (0, 0))],
        out_specs=row(M_WIDTH),
        out_shape=jax.ShapeDtypeStruct((n, M_WIDTH), BF16),
        scratch_shapes=[pltpu.VMEM((M_QK_WIDTH, 2 * M_V_DIM), F32), pltpu.VMEM((8, LANES), F32)],
        compiler_params=_params("parallel", "arbitrary"),
        name="mlstm",
    )(mq, kT, mv, so, gi, gf, ng)


def _attn_body(q_ref, kp_ref, kc_ref, vp_ref, vc_ref, o_ref, lse_ref, *, dil, slopes, lq):
    QB = N_BACK
    first = pl.program_id(2) == 0
    qn = q_ref[0, 0]
    kcn = kc_ref[0, 0]
    kpn = kp_ref[0, 0]
    vc = vc_ref[0, 0]
    vp = vp_ref[0, 0]

    qi = lax.broadcasted_iota(I32, (QB, 2 * QB), 0)
    kj = lax.broadcasted_iota(I32, (QB, 2 * QB), 1)
    dist = qi + QB - kj
    band = jnp.logical_and(dist >= 0, dist <= N_BACK)
    distf = (dist * dil).astype(F32)
    bias = [jnp.where(band, -float(slopes[h]) * distf, -jnp.inf) for h in range(A_HEADS)]
    no_prev = jnp.logical_and(first, kj < QB)
    lo_half = lax.broadcasted_iota(I32, (QB, LANES), 1) < A_HEAD_DIM
    ones = jnp.ones((2 * QB, LANES), BF16)

    units = []
    for j in range(lq // QB):
        rows = slice(j * QB, (j + 1) * QB)
        prow = slice((j - 1) * QB, j * QB)
        keys = jnp.concatenate([kpn if j == 0 else kcn[prow], kcn[rows]], axis=0)
        vals = jnp.concatenate([vp if j == 0 else vc[prow], vc[rows]], axis=0)
        for p in range(A_HEADS // 2):
            lanes_p = slice(p * LANES, (p + 1) * LANES)
            q_pair = qn[rows, lanes_p]
            k_pair = keys[:, lanes_p]
            vext = jnp.concatenate([vals[:, lanes_p], ones], axis=1)
            scores = []
            for hh in range(2):
                sel = lo_half if hh == 0 else jnp.logical_not(lo_half)
                qm = jnp.where(sel, q_pair, jnp.zeros_like(q_pair))
                scores.append(lax.dot_general(qm, k_pair, _NT, preferred_element_type=F32))
            units.append((j, rows, lanes_p, p, vext, scores))

    for j, rows, lanes_p, p, vext, scores in units:
        o_pair = None
        l_pair = None
        for hh in range(2):
            s = scores[hh] + bias[2 * p + hh]
            if j == 0:
                s = jnp.where(no_prev, -jnp.inf, s)
            m = jnp.max(s, axis=1, keepdims=True)
            pv = jnp.dot(jnp.exp(s - m).astype(BF16), vext, preferred_element_type=F32)
            den = pv[:, LANES:]
            o_h = pv[:, :LANES] / den
            l_h = m + jnp.log(den)
            o_pair = o_h if hh == 0 else jnp.where(lo_half, o_pair, o_h)
            l_pair = l_h if hh == 0 else jnp.where(lo_half, l_pair, l_h)
        o_ref[0, 0, rows, lanes_p] = o_pair.astype(BF16)
        lse_ref[0, 0, rows, lanes_p] = l_pair


def _stage_attn(aq, ak, av, batch, seq, group):
    _, dil = DILATED_PATTERNS[group]
    L = seq // dil
    assert L % N_BACK == 0
    lq = min(1024, L)
    nq = L // lq
    sub = lq // N_BACK
    cur = pl.BlockSpec((1, 1, lq, A_WIDTH), lambda b, r, i: (b, r, i, 0))
    prev = pl.BlockSpec((1, 1, N_BACK, A_WIDTH), lambda b, r, i: (b, r, jnp.maximum(i * sub - 1, 0), 0))
    return pl.pallas_call(
        functools.partial(_attn_body, dil=dil, slopes=tuple(_alibi_slopes()[group]), lq=lq),
        grid=(batch, dil, nq),
        in_specs=[cur, prev, cur, prev, cur],
        out_specs=(cur, cur),
        out_shape=(jax.ShapeDtypeStruct((batch, dil, L, A_WIDTH), BF16),
                   jax.ShapeDtypeStruct((batch, dil, L, A_WIDTH), F32)),
        compiler_params=_params("parallel", "parallel", "parallel"),
        name=f"dilated_attn_d{dil}",
    )(aq, ak, ak, av, av)


TOKEN_TILE_ROWS = D_MODEL // LANES


def _store_token_tiles(ref, val):
    t = val.shape[0]
    for s in range(TOKEN_TILE_ROWS):
        ref[pl.ds(s, t, stride=TOKEN_TILE_ROWS), :] = val[:, s * LANES:(s + 1) * LANES]


def _load_token_tiles(ref, t):
    return jnp.concatenate([ref[pl.ds(s, t, stride=TOKEN_TILE_ROWS), :] for s in range(TOKEN_TILE_ROWS)], axis=1)


def _token_tile(ref, tok):
    start = tok * TOKEN_TILE_ROWS
    if not isinstance(tok, int):
        start = pl.multiple_of(start, TOKEN_TILE_ROWS)
    return ref.at[pl.ds(start, TOKEN_TILE_ROWS)]


def _rows8(vals):
    t = vals[0].shape[1]
    rid = lax.broadcasted_iota(I32, (8, t), 0)
    out = jnp.zeros((8, t), vals[0].dtype)
    for k, v in enumerate(vals):
        out = jnp.where(rid == k, jnp.broadcast_to(v, (8, t)), out)
    return out


def _merge_body(hm_ref, o1_ref, o2_ref, o3_ref, l1_ref, l2_ref, l3_ref, sgm_ref, sga_ref, x_ref,
                wm_ref, wa_ref, wo_ref, g2_ref, wrh_ref, br_ref,
                x2_ref, xn_ref, loc_ref, gate_ref, tcnt_ref, tcar_ref, cnt_ref, carry_ref, st_ref):
    @pl.when(pl.program_id(0) == 0)
    def _():
        carry_ref[...] = jnp.zeros_like(carry_ref)

    m_branch = jnp.dot(hm_ref[...], wm_ref[...], preferred_element_type=F32)
    dils = [d for _, d in DILATED_PATTERNS]
    l1, l2, l3 = [_merge_residues(r, d, st_ref) for r, d in zip((l1_ref, l2_ref, l3_ref), dils)]
    lmax = jnp.maximum(jnp.maximum(l1, l2), l3)
    e1, e2, e3 = jnp.exp(l1 - lmax), jnp.exp(l2 - lmax), jnp.exp(l3 - lmax)
    num = e1 * _merge_residues(o1_ref, dils[0], st_ref)
    num = num + e2 * _merge_residues(o2_ref, dils[1], st_ref)
    num = num + e3 * _merge_residues(o3_ref, dils[2], st_ref)
    h_a = num / (e1 + e2 + e3)
    y = (sgm_ref[...].astype(F32) * m_branch
         + sga_ref[...].astype(F32) * jnp.dot(h_a.astype(BF16), wa_ref[...], preferred_element_type=F32))
    x2 = x_ref[...] + jnp.dot(y.astype(BF16), wo_ref[...], preferred_element_type=F32)
    x2_ref[...] = x2
    xn = x2 * lax.rsqrt(jnp.mean(x2 * x2, axis=-1, keepdims=True) + EPS) * g2_ref[...]
    xh = xn.astype(BF16)
    xn_ref[...] = xh

    logits = lax.dot_general(wrh_ref[...], xh, _NT, preferred_element_type=F32) + br_ref[...]
    t = logits.shape[1]
    eid = lax.broadcasted_iota(I32, (N_EXPERTS, t), 0).astype(F32)
    vals = logits
    top_v, top_i = [], []
    for _ in range(TOP_K):
        mx = jnp.max(vals, axis=0, keepdims=True)
        ik = jnp.min(jnp.where(vals == mx, eid, float(N_EXPERTS)), axis=0, keepdims=True)
        top_v.append(mx)
        top_i.append(ik)
        vals = jnp.where(eid == ik, -jnp.inf, vals)
    ex = [jnp.exp(v - top_v[0]) for v in top_v]
    den = ex[0] + ex[1] + ex[2] + ex[3]
    gate_ref[...] = _rows8([e / den for e in ex])

    chosen = jnp.zeros((N_EXPERTS, t), F32)
    for ik in top_i:
        chosen = chosen + (eid == ik).astype(F32)
    before = (lax.broadcasted_iota(I32, (t, t), 0) < lax.broadcasted_iota(I32, (t, t), 1)).astype(BF16)
    prefix = jnp.dot(chosen.astype(BF16), before, preferred_element_type=F32)
    tcount = jnp.broadcast_to(jnp.sum(chosen, axis=1, keepdims=True), (N_EXPERTS, LANES))
    below = (lax.broadcasted_iota(I32, (N_EXPERTS, N_EXPERTS), 1)
             < lax.broadcasted_iota(I32, (N_EXPERTS, N_EXPERTS), 0)).astype(BF16)
    t_hi = jnp.floor(tcount * (1.0 / 256.0)) * 256.0
    tile_off = (jnp.dot(below, t_hi.astype(BF16), preferred_element_type=F32)
                + jnp.dot(below, (tcount - t_hi).astype(BF16), preferred_element_type=F32))
    pos = prefix + tile_off[:, 0:1]
    loc_ref[...] = _rows8([jnp.sum(jnp.where(eid == ik, pos, 0.0), axis=0, keepdims=True).astype(I32)
                           for ik in top_i])
    carry = carry_ref[...]
    tcnt_ref[...] = tcount.astype(I32)
    tcar_ref[...] = carry.astype(I32)
    total = carry + tcount
    carry_ref[...] = total
    cnt_ref[...] = total


def _stage_merge(h_m, attn, sgm, sga, x2d, w_mb, w_ab, w_out, norm2_g, w_router, b_router, batch, seq, tm):
    n = x2d.shape[0]
    steps = seq // tm
    (o1, l1), (o2, l2), (o3, l3) = attn
    wm = w_mb.astype(BF16)
    wa = w_ab.astype(BF16)
    wo = w_out.astype(BF16)
    g2 = norm2_g.astype(F32).reshape(1, D_MODEL)
    wrh = w_router.astype(BF16).T
    br = b_router.astype(F32).reshape(N_EXPERTS, 1)
    row = lambda w: pl.BlockSpec((tm, w), lambda i: (i, 0))
    rowT = lambda r: pl.BlockSpec((r, tm), lambda i: (0, i))
    full = lambda a: pl.BlockSpec(a.shape, lambda i: (0,) * a.ndim)
    res = lambda d: pl.BlockSpec((1, d, tm // d, A_WIDTH), lambda i: (i // steps, 0, i % steps, 0))
    dils = [d for _, d in DILATED_PATTERNS]
    per_tile = pl.BlockSpec((N_EXPERTS, LANES), lambda i: (0, i))
    return pl.pallas_call(
        _merge_body,
        grid=(n // tm,),
        in_specs=[row(M_WIDTH), *[res(d) for d in dils], *[res(d) for d in dils],
                  row(D_MODEL), row(D_MODEL), row(D_MODEL),
                  full(wm), full(wa), full(wo), full(g2), full(wrh), full(br)],
        out_specs=(row(D_MODEL), row(D_MODEL), rowT(8), rowT(8), per_tile, per_tile,
                   pl.BlockSpec((N_EXPERTS, LANES), lambda i: (0, 0))),
        out_shape=(jax.ShapeDtypeStruct((n, D_MODEL), F32),
                   jax.ShapeDtypeStruct((n, D_MODEL), BF16),
                   jax.ShapeDtypeStruct((8, n), I32),
                   jax.ShapeDtypeStruct((8, n), F32),
                   jax.ShapeDtypeStruct((N_EXPERTS, (n // tm) * LANES), I32),
                   jax.ShapeDtypeStruct((N_EXPERTS, (n // tm) * LANES), I32),
                   jax.ShapeDtypeStruct((N_EXPERTS, LANES), F32)),
        scratch_shapes=[pltpu.VMEM((N_EXPERTS, LANES), F32), pltpu.VMEM((2, tm, LANES), F32)],
        compiler_params=_params("arbitrary"),
        name="merge_route",
    )(h_m, o1, o2, o3, l1, l2, l3, sgm, sga, x2d, wm, wa, wo, g2, wrh, br)


def _offsets_body(cnt_ref, blk_ref, pstart_ref, zlo_ref, zhi_ref, *, nblk_pad):
    cnt = cnt_ref[...]
    padded = jnp.floor((cnt + (MOE_BLOCK - 1)) * (1.0 / MOE_BLOCK)) * MOE_BLOCK
    lower = (lax.broadcasted_iota(I32, (N_EXPERTS, N_EXPERTS), 1)
             <= lax.broadcasted_iota(I32, (N_EXPERTS, N_EXPERTS), 0)).astype(F32)
    pends = jnp.dot(lower, padded, precision=lax.Precision.HIGHEST, preferred_element_type=F32)
    pstart = pends - padded
    pstart_ref[...] = pstart.astype(I32)
    zlo_ref[...] = (pstart + cnt).astype(I32)
    zhi_ref[...] = pends.astype(I32)

    first_row = (lax.broadcasted_iota(I32, (N_EXPERTS, nblk_pad), 1) * MOE_BLOCK).astype(F32)
    pe = jnp.broadcast_to(pends[:, 0:1], (N_EXPERTS, nblk_pad))
    be = jnp.sum((pe <= first_row).astype(F32), axis=0, keepdims=True)
    be = jnp.minimum(be, float(N_EXPERTS - 1))
    nused = pends[N_EXPERTS - 1:N_EXPERTS, 0:1] * (1.0 / MOE_BLOCK)
    nonempty = jnp.broadcast_to(padded[:, 0:1], (N_EXPERTS, nblk_pad)) > 0.0
    runidx = jnp.sum(jnp.logical_and(pe <= first_row, nonempty).astype(F32), axis=0, keepdims=True)
    parity = runidx - 2.0 * jnp.floor(runidx * 0.5)
    eid = lax.broadcasted_iota(I32, (N_EXPERTS, nblk_pad), 0).astype(F32)
    later = jnp.logical_and(eid > be, nonempty)
    nxt = jnp.min(jnp.where(later, eid, float(N_EXPERTS)), axis=0, keepdims=True)
    blk_ref[...] = _rows8([be.astype(I32), jnp.broadcast_to(nused, (1, nblk_pad)).astype(I32),
                           parity.astype(I32), nxt.astype(I32)])


def _stage_offsets(cnt, nblk):
    nblk_pad = -(-nblk // LANES) * LANES
    const = lambda r, c: pl.BlockSpec((r, c), lambda i: (0, 0))
    per_expert = jax.ShapeDtypeStruct((N_EXPERTS, LANES), I32)
    return pl.pallas_call(
        functools.partial(_offsets_body, nblk_pad=nblk_pad),
        grid=(1,),
        in_specs=[const(N_EXPERTS, LANES)],
        out_specs=(const(8, nblk_pad), const(N_EXPERTS, LANES), const(N_EXPERTS, LANES), const(N_EXPERTS, LANES)),
        out_shape=(jax.ShapeDtypeStruct((8, nblk_pad), I32), per_expert, per_expert, per_expert),
        compiler_params=_params("arbitrary"),
        name="route_offsets",
    )(cnt)


RUN_BITS = 10


def _tile_rows(ref, first_row, nrows):
    start = first_row * TOKEN_TILE_ROWS
    if not isinstance(first_row, int):
        start = pl.multiple_of(start, TOKEN_TILE_ROWS)
    return ref.at[pl.ds(start, nrows * TOKEN_TILE_ROWS)]


def _for_each_piece(length, fn):
    for b in reversed(range(RUN_BITS)):
        @pl.when(((length >> b) & 1) == 1)
        def _(b=b):
            fn((length >> (b + 1)) << (b + 1), 1 << b)


def _for_each_run(tile, tcnt_ref, tcar_ref, pstart_ref, fn):
    def per_expert(e, local):
        count = tcnt_ref[tile, e]
        first = pstart_ref[e] + tcar_ref[tile, e]
        _for_each_piece(count, lambda off, size: fn(local + off, first + off, size))
        return local + count

    lax.fori_loop(0, N_EXPERTS, per_expert, 0)


PERM_CHUNK = 256


def _dispatch_body(tcnt_ref, tcar_ref, pstart_ref, zlo_ref, zhi_ref, loc_ref, xn_ref, xs_hbm,
                   buf_ref, sems, *, tm):
    step = pl.program_id(0)
    nloc_tiles = TOP_K * tm * TOKEN_TILE_ROWS

    def wait_buffer(slot):
        pltpu.make_async_copy(buf_ref.at[slot], xs_hbm.at[pl.ds(0, nloc_tiles)], sems.at[slot]).wait()

    for slot in range(2):
        tile = 2 * step + slot

        @pl.when(step > 0)
        def _(slot=slot):
            wait_buffer(slot)

        loc = loc_ref[:, slot * tm:(slot + 1) * tm]
        xn = xn_ref[slot * tm:(slot + 1) * tm, :]
        for c in range(TOP_K * tm // PERM_CHUNK):
            lid = lax.broadcasted_iota(I32, (PERM_CHUNK, tm), 0) + c * PERM_CHUNK
            hit = lid == loc[0:1, :]
            for k in range(1, TOP_K):
                hit = jnp.logical_or(hit, lid == loc[k:k + 1, :])
            rows = jnp.dot(jnp.where(hit, 1.0, 0.0).astype(BF16), xn, preferred_element_type=F32)
            _store_token_tiles(
                buf_ref.at[slot, pl.ds(c * PERM_CHUNK * TOKEN_TILE_ROWS, PERM_CHUNK * TOKEN_TILE_ROWS)], rows)

        def run_copy(local, first, size, slot=slot):
            return pltpu.make_async_copy(_tile_rows(buf_ref.at[slot], local, size),
                                         _tile_rows(xs_hbm, first, size), sems.at[slot])

        _for_each_run(tile, tcnt_ref, tcar_ref, pstart_ref, lambda l, f, s: run_copy(l, f, s).start())

    @pl.when(step == pl.num_programs(0) - 1)
    def _():
        wait_buffer(0)
        wait_buffer(1)
        zsrc = buf_ref.at[0]
        zsrc[pl.ds(0, MOE_BLOCK * TOKEN_TILE_ROWS), :] = jnp.zeros((MOE_BLOCK * TOKEN_TILE_ROWS, LANES), F32)

        def zero_copy(first, size):
            return pltpu.make_async_copy(_tile_rows(zsrc, 0, size), _tile_rows(xs_hbm, first, size), sems.at[0])

        def per_expert(e, carry):
            lo = zlo_ref[e]
            npad = zhi_ref[e] - lo
            _for_each_piece(npad, lambda off, size: zero_copy(lo + off, size).start())
            _for_each_piece(npad, lambda off, size: zero_copy(lo + off, size).wait())
            return carry

        lax.fori_loop(0, N_EXPERTS, per_expert, 0)

        first_unused = zhi_ref[N_EXPERTS - 1] // MOE_BLOCK
        nblk = xs_hbm.shape[0] // (MOE_BLOCK * TOKEN_TILE_ROWS)

        def tail(blk, carry):
            zero_copy(blk * MOE_BLOCK, MOE_BLOCK).start()
            zero_copy(blk * MOE_BLOCK, MOE_BLOCK).wait()
            return carry

        lax.fori_loop(first_unused, nblk, tail, 0)


def _stage_dispatch(tables, loc8, xn, nrows, tm):
    n = xn.shape[0]
    assert TOP_K * tm >= MOE_BLOCK and (n // tm) % 2 == 0
    grid_spec = pltpu.PrefetchScalarGridSpec(
        num_scalar_prefetch=5,
        grid=(n // (2 * tm),),
        in_specs=[pl.BlockSpec((8, 2 * tm), lambda i, *_: (0, i)),
                  pl.BlockSpec((2 * tm, D_MODEL), lambda i, *_: (i, 0))],
        out_specs=pl.BlockSpec(memory_space=pl.ANY),
        scratch_shapes=[pltpu.VMEM((2, TOP_K * tm * TOKEN_TILE_ROWS, LANES), F32), pltpu.SemaphoreType.DMA((2,))],
    )
    return pl.pallas_call(
        functools.partial(_dispatch_body, tm=tm),
        grid_spec=grid_spec,
        out_shape=jax.ShapeDtypeStruct((nrows * TOKEN_TILE_ROWS, LANES), F32),
        compiler_params=_params("arbitrary"),
        name="dispatch",
    )(*tables, loc8, xn)


EXPERT_BLOCKS_PER_STEP = 2


def _expert_body(be_ref, nu_ref, par_ref, nxt_ref, xs_ref, w1_hbm, b1_ref, w2_hbm, b2_ref, ys_ref,
                 w1f_ref, w2f_ref, w1b_ref, w2b_ref, sems):
    block_tiles = MOE_BLOCK * TOKEN_TILE_ROWS

    def fetch(expert, slot):
        return (pltpu.make_async_copy(w1_hbm.at[expert], w1f_ref.at[slot], sems.at[slot, 0]),
                pltpu.make_async_copy(w2_hbm.at[expert], w2f_ref.at[slot], sems.at[slot, 1]))

    for sub in range(EXPERT_BLOCKS_PER_STEP):
        j = pl.program_id(0) * EXPERT_BLOCKS_PER_STEP + sub
        used = j < nu_ref[0]
        jj = jnp.maximum(jnp.minimum(j, nu_ref[0] - 1), 0)
        e = be_ref[jj]
        fresh = jnp.logical_or(j == 0, e != be_ref[jnp.maximum(jj - 1, 0)])
        xs_blk = xs_ref.at[pl.ds(sub * block_tiles, block_tiles)]
        ys_blk = ys_ref.at[pl.ds(sub * block_tiles, block_tiles)]

        @pl.when(jnp.logical_and(used, fresh))
        def _(j=j, jj=jj, e=e):
            slot = par_ref[jj]

            @pl.when(j == 0)
            def _():
                for c in fetch(e, slot):
                    c.start()

            for c in fetch(e, slot):
                c.wait()
            nxt = nxt_ref[jj]

            @pl.when(nxt < N_EXPERTS)
            def _():
                for c in fetch(nxt, 1 - slot):
                    c.start()

            w1b_ref[...] = w1f_ref[slot].astype(BF16)
            w2b_ref[...] = w2f_ref[slot].astype(BF16)

        @pl.when(used)
        def _(e=e, xs_blk=xs_blk, ys_blk=ys_blk):
            xb = _load_token_tiles(xs_blk, MOE_BLOCK).astype(BF16)
            gu = jnp.dot(xb, w1b_ref[...], preferred_element_type=F32) + b1_ref[pl.ds(e, 1), :]
            gate = jnp.minimum(gu[:, :D_FF], SWIGLU_LIMIT)
            lin = jnp.clip(gu[:, D_FF:], -SWIGLU_LIMIT, SWIGLU_LIMIT)
            act = (lin + 1.0) * (gate * jax.nn.sigmoid(SWIGLU_ALPHA * gate))
            ys = jnp.dot(act.astype(BF16), w2b_ref[...], preferred_element_type=F32) + b2_ref[pl.ds(e, 1), :]
            _store_token_tiles(ys_blk, ys)

        @pl.when(jnp.logical_not(used))
        def _(ys_blk=ys_blk):
            ys_blk[...] = jnp.zeros(ys_blk.shape, F32)


def _stage_experts(blk8, xs, w1, b1, w2, b2):
    nrows = xs.shape[0] // TOKEN_TILE_ROWS
    nblk = nrows // MOE_BLOCK
    assert nblk % EXPERT_BLOCKS_PER_STEP == 0
    block_e, nused, parity, nxt = blk8[0, :nblk], blk8[1, :1], blk8[2, :nblk], blk8[3, :nblk]
    tiles = (EXPERT_BLOCKS_PER_STEP * MOE_BLOCK * TOKEN_TILE_ROWS, LANES)
    full = lambda a: pl.BlockSpec(a.shape, lambda j, *_: (0,) * a.ndim)
    grid_spec = pltpu.PrefetchScalarGridSpec(
        num_scalar_prefetch=4,
        grid=(nblk // EXPERT_BLOCKS_PER_STEP,),
        in_specs=[pl.BlockSpec(tiles, lambda j, *_: (j, 0)),
                  pl.BlockSpec(memory_space=pl.ANY), full(b1),
                  pl.BlockSpec(memory_space=pl.ANY), full(b2)],
        out_specs=pl.BlockSpec(tiles, lambda j, *_: (j, 0)),
        scratch_shapes=[pltpu.VMEM((2, D_MODEL, 2 * D_FF), F32), pltpu.VMEM((2, D_FF, D_MODEL), F32),
                        pltpu.VMEM((D_MODEL, 2 * D_FF), BF16), pltpu.VMEM((D_FF, D_MODEL), BF16),
                        pltpu.SemaphoreType.DMA((2, 2))],
    )
    return pl.pallas_call(
        _expert_body,
        grid_spec=grid_spec,
        out_shape=jax.ShapeDtypeStruct((nrows * TOKEN_TILE_ROWS, LANES), F32),
        compiler_params=_params("arbitrary"),
        name="experts",
    )(block_e, nused, parity, nxt, xs, w1, b1, w2, b2)


def _combine_body(tcnt_ref, tcar_ref, pstart_ref, loc_ref, gate_ref, x2_ref, ys_hbm, out_ref,
                  buf_ref, g_ref, sems, *, tm):
    step = pl.program_id(0)
    nloc = TOP_K * tm

    def start_runs(tile, slot):
        def run_copy(local, first, size):
            return pltpu.make_async_copy(_tile_rows(ys_hbm, first, size),
                                         _tile_rows(buf_ref.at[slot], local, size), sems.at[slot])
        _for_each_run(tile, tcnt_ref, tcar_ref, pstart_ref, lambda l, f, s: run_copy(l, f, s).start())

    def wait_buffer(slot):
        pltpu.make_async_copy(ys_hbm.at[pl.ds(0, nloc * TOKEN_TILE_ROWS)], buf_ref.at[slot], sems.at[slot]).wait()

    def combine(slot):
        zpad = jnp.zeros((LANES - 16, LANES), F32)
        lane = lax.broadcasted_iota(I32, (LANES, nloc), 1).astype(F32)
        for c in range(tm // LANES):
            cols_in = slice(slot * tm + c * LANES, slot * tm + (c + 1) * LANES)
            cols = jnp.transpose(jnp.concatenate([loc_ref[:, cols_in].astype(F32), gate_ref[:, cols_in], zpad], axis=0))
            g = jnp.zeros((LANES, nloc), F32)
            for k in range(TOP_K):
                g = jnp.where(lane == cols[:, k:k + 1], cols[:, 8 + k:9 + k], g)
            g_ref[c * LANES:(c + 1) * LANES, :] = g.astype(BF16)
        wait_buffer(slot)
        ys = _load_token_tiles(buf_ref.at[slot], nloc).astype(BF16)
        rows = slice(slot * tm, (slot + 1) * tm)
        out_ref[rows, :] = x2_ref[rows, :] + jnp.dot(g_ref[...], ys, preferred_element_type=F32)

    @pl.when(step == 0)
    def _():
        start_runs(0, 0)

    start_runs(2 * step + 1, 1)
    combine(0)

    @pl.when(step + 1 < pl.num_programs(0))
    def _():
        start_runs(2 * step + 2, 0)

    combine(1)


def _stage_combine(tables, loc8, gate8, x2, ys, tm):
    n = x2.shape[0]
    assert (n // tm) % 2 == 0
    grid_spec = pltpu.PrefetchScalarGridSpec(
        num_scalar_prefetch=3,
        grid=(n // (2 * tm),),
        in_specs=[pl.BlockSpec((8, 2 * tm), lambda i, *_: (0, i)),
                  pl.BlockSpec((8, 2 * tm), lambda i, *_: (0, i)),
                  pl.BlockSpec((2 * tm, D_MODEL), lambda i, *_: (i, 0)),
                  pl.BlockSpec(memory_space=pl.ANY)],
        out_specs=pl.BlockSpec((2 * tm, D_MODEL), lambda i, *_: (i, 0)),
        scratch_shapes=[pltpu.VMEM((2, TOP_K * tm * TOKEN_TILE_ROWS, LANES), F32),
                        pltpu.VMEM((tm, TOP_K * tm), BF16),
                        pltpu.SemaphoreType.DMA((2,))],
    )
    return pl.pallas_call(
        functools.partial(_combine_body, tm=tm),
        grid_spec=grid_spec,
        out_shape=jax.ShapeDtypeStruct((n, D_MODEL), F32),
        compiler_params=_params("arbitrary"),
        name="combine",
    )(*tables, loc8, gate8, x2, ys)


def _moe(x2, xn, loc8, gate8, tcnt, tcar, cnt, w1, b1, w2, b2, tm):
    n = x2.shape[0]
    ntile = n // tm
    nblk = -(-(n * TOP_K) // MOE_BLOCK) + N_EXPERTS
    blk8, pstart, zlo, zhi = _stage_offsets(cnt, nblk)
    per_tile = lambda a: a.reshape(N_EXPERTS, ntile, LANES)[:, :, 0].T
    tables = (per_tile(tcnt), per_tile(tcar), pstart[:, 0])
    xs = _stage_dispatch(tables + (zlo[:, 0], zhi[:, 0]), loc8, xn, nblk * MOE_BLOCK, tm)
    ys = _stage_experts(blk8, xs, w1, b1, w2, b2)
    return _stage_combine(tables, loc8, gate8, x2, ys, tm)


def kernel(x, norm1_g, w_in, mlstm_gate_b, mlstm_norm_g, attn_q_norm_g, attn_k_norm_g, w_mlstm_branch,
           w_attn_branch, w_out, norm2_g, w_router, b_router, w1, b1, w2, b2):
    batch, seq, _ = x.shape
    n = batch * seq
    for l in range(norm1_g.shape[0]):
        x2d = x.reshape(n, D_MODEL)
        tm = min(512, seq)
        mq, kT, mv, so, gi, gf, aq, ak, av, sgm, sga = _stage_inproj(
            x2d, norm1_g[l], w_in[l], mlstm_gate_b[l], attn_q_norm_g[l], attn_k_norm_g[l], batch, seq, tm)
        h_m = _stage_mlstm(mq, kT, mv, so, gi, gf, mlstm_norm_g[l], batch, seq, min(1024, seq))
        attn = [_stage_attn(aq[g], ak[g], av[g], batch, seq, g)
                for g in range(N_GROUPS)]
        x2, xn, loc8, gate8, tcnt, tcar, cnt = _stage_merge(
            h_m, attn, sgm, sga, x2d, w_mlstm_branch[l], w_attn_branch[l], w_out[l], norm2_g[l],
            w_router[l], b_router[l], batch, seq, tm)
        out = _moe(x2, xn, loc8, gate8, tcnt, tcar, cnt, w1[l], b1[l], w2[l], b2[l], tm)
        x = out.reshape(batch, seq, D_MODEL)
    return x
```

```python
import functools

import numpy as np
import jax
import jax.numpy as jnp
from jax import lax
from jax.experimental import pallas as pl
from jax.experimental.pallas import tpu as pltpu

F32 = jnp.float32
BF16 = jnp.bfloat16
I32 = jnp.int32

D_MODEL = 1024
M_HEADS = 4
M_QK_DIM = 64
M_V_DIM = 128
GATE_SOFTCAP = 15.0
A_HEADS = 4
A_HEAD_DIM = 64
DILATED_PATTERNS = ((128, 1), (512, 4), (2048, 16))
N_GROUPS = len(DILATED_PATTERNS)
N_BACK = 128
N_EXPERTS = 32
TOP_K = 4
D_FF = 1024
SWIGLU_LIMIT = 7.0
SWIGLU_ALPHA = 1.702
MOE_BLOCK = 512
EPS = 1e-6

M_WIDTH = M_HEADS * M_V_DIM
M_QK_WIDTH = M_HEADS * M_QK_DIM
A_WIDTH = A_HEADS * A_HEAD_DIM
IN_SPLITS = (M_QK_WIDTH, M_QK_WIDTH, M_WIDTH, M_WIDTH, 2 * M_HEADS,
             N_GROUPS * A_WIDTH, N_GROUPS * A_WIDTH, N_GROUPS * A_WIDTH, D_MODEL, D_MODEL)

LANES = 128
VMEM_LIMIT = 56 * 1024 * 1024

BF16_EXACT_INT = 256.0

_NT = (((1,), (1,)), ((), ()))
_TN = (((0,), (1,)), ((), ()))


def _alibi_slopes():
    n = N_GROUPS * A_HEADS
    s = np.exp2(-8.0 * np.arange(1, n + 1) / n).astype(np.float32)
    return s.reshape(N_GROUPS, A_HEADS)


def _params(*sem):
    return pltpu.CompilerParams(dimension_semantics=sem, vmem_limit_bytes=VMEM_LIMIT)


def _log_sigmoid(x):
    return jnp.minimum(x, 0.0) - jnp.log1p(jnp.exp(-jnp.abs(x)))


_GATE_PAD = LANES - 2 * M_HEADS


def _segments():
    bounds, start = [], 0
    for i, width in enumerate(IN_SPLITS):
        bounds.append((start, start + width))
        start += width + (_GATE_PAD if i == 4 else 0)
    return bounds, start


_SEG, _W_MAIN = _segments()
_C_MQ, _C_MV, _C_MO, _C_AQ, _C_AK, _C_AV, _C_GM, _C_GA = (_SEG[i] for i in (0, 2, 3, 5, 6, 7, 8, 9))


def _split_residues(val, d, out_ref, st_ref):
    t = val.shape[0]
    if d == 1:
        out_ref[0, 0] = val.astype(out_ref.dtype)
        return
    st_ref[0] = val[:, :LANES]
    st_ref[1] = val[:, LANES:]
    for r in range(d):
        piece = jnp.concatenate([st_ref[0, pl.ds(r, t // d, stride=d), :],
                                 st_ref[1, pl.ds(r, t // d, stride=d), :]], axis=1)
        out_ref[0, r] = piece.astype(out_ref.dtype)


def _merge_residues(ref, d, st_ref):
    if d == 1:
        return ref[0, 0].astype(F32)
    m = ref.shape[2]
    if d == 16:
        a_ref, b_ref = st_ref.at[0], st_ref.at[1]
        for r0 in range(4):
            for r1 in range(4):
                blk = ref[0, 4 * r1 + r0].astype(F32)
                a_ref[0, pl.ds(r0 * 4 * m + r1, m, stride=4), :] = blk[:, :LANES]
                a_ref[1, pl.ds(r0 * 4 * m + r1, m, stride=4), :] = blk[:, LANES:]
        for r0 in range(4):
            b_ref[0, pl.ds(r0, 4 * m, stride=4), :] = a_ref[0, r0 * 4 * m:(r0 + 1) * 4 * m, :]
            b_ref[1, pl.ds(r0, 4 * m, stride=4), :] = a_ref[1, r0 * 4 * m:(r0 + 1) * 4 * m, :]
        return jnp.concatenate([b_ref[0], b_ref[1]], axis=1)
    a_ref = st_ref.at[0]
    for r in range(d):
        blk = ref[0, r].astype(F32)
        a_ref[0, pl.ds(r, m, stride=d), :] = blk[:, :LANES]
        a_ref[1, pl.ds(r, m, stride=d), :] = blk[:, LANES:]
    return jnp.concatenate([a_ref[0], a_ref[1]], axis=1)


def _inproj_body(x_ref, g1_ref, wm_ref, gb_ref, gq_ref, gk_ref,
                 mq_ref, kT_ref, mv_ref, so_ref, gi_ref, gf_ref,
                 q0_ref, q1_ref, q2_ref, k0_ref, k1_ref, k2_ref, v0_ref, v1_ref, v2_ref,
                 sgm_ref, sga_ref, st_ref):
    x = x_ref[...]
    h = x * lax.rsqrt(jnp.mean(x * x, axis=-1, keepdims=True) + EPS) * g1_ref[...]
    hb = h.astype(BF16)

    def seg(c):
        return jnp.dot(hb, wm_ref[:, c[0]:c[1]], preferred_element_type=F32)

    mq_ref[...] = seg(_C_MQ).astype(BF16)
    mv_ref[...] = seg(_C_MV).astype(BF16)
    so_ref[...] = jax.nn.sigmoid(seg(_C_MO)).astype(BF16)
    hid_r = lax.broadcasted_iota(I32, (A_WIDTH, A_WIDTH), 0) // A_HEAD_DIM
    hid_c = lax.broadcasted_iota(I32, (A_WIDTH, A_WIDTH), 1) // A_HEAD_DIM
    head_ones = (hid_r == hid_c).astype(BF16)
    for c, refs, gain_ref in ((_C_AQ, (q0_ref, q1_ref, q2_ref), gq_ref), (_C_AK, (k0_ref, k1_ref, k2_ref), gk_ref),
                              (_C_AV, (v0_ref, v1_ref, v2_ref), None)):
        val = seg(c)
        for g, ref in enumerate(refs):
            piece = val[:, g * A_WIDTH:(g + 1) * A_WIDTH]
            if gain_ref is not None:
                ss = jnp.dot((piece * piece).astype(BF16), head_ones, preferred_element_type=F32)
                piece = piece * lax.rsqrt(ss * (1.0 / A_HEAD_DIM) + EPS) * gain_ref[:, g * A_WIDTH:(g + 1) * A_WIDTH]
            _split_residues(piece, DILATED_PATTERNS[g][1], ref, st_ref)
    sgm_ref[...] = jax.nn.sigmoid(seg(_C_GM)).astype(BF16)
    sga_ref[...] = jax.nn.sigmoid(seg(_C_GA)).astype(BF16)

    kT_ref[...] = lax.dot_general(wm_ref[:, _SEG[1][0]:_SEG[1][1]], hb, _TN, preferred_element_type=F32).astype(BF16)
    zg = lax.dot_general(wm_ref[:, _SEG[4][0]:_SEG[4][0] + LANES], hb, _TN, preferred_element_type=F32)
    zi = zg[0:8] + gb_ref[0:8]
    zf = zg[M_HEADS:M_HEADS + 8] + gb_ref[8:16]
    gi_ref[...] = GATE_SOFTCAP * jnp.tanh(zi / GATE_SOFTCAP)
    gf_ref[...] = _log_sigmoid(GATE_SOFTCAP * jnp.tanh(zf / GATE_SOFTCAP))


def _stage_inproj(x2d, norm1_g, w_in, gate_b, gq, gk, batch, seq, tm):
    n = x2d.shape[0]
    steps = seq // tm
    cuts = np.concatenate([[0], np.cumsum(IN_SPLITS)])
    gate_end = int(cuts[5])
    wm = jnp.concatenate([w_in[:, :gate_end], jnp.zeros((D_MODEL, _GATE_PAD), w_in.dtype), w_in[:, gate_end:]],
                         axis=1).astype(BF16)
    gb = jnp.zeros((16, 1), F32)
    gb = gb.at[0:4, 0].set(gate_b[:M_HEADS].astype(F32)).at[8:12, 0].set(gate_b[M_HEADS:].astype(F32))
    g1 = norm1_g.astype(F32).reshape(1, D_MODEL)
    gq_t = (jnp.tile(gq.astype(F32), (1, A_HEADS)) * (A_HEAD_DIM ** -0.5)).reshape(1, N_GROUPS * A_WIDTH)
    gk_t = jnp.tile(gk.astype(F32), (1, A_HEADS)).reshape(1, N_GROUPS * A_WIDTH)

    row = lambda w: pl.BlockSpec((tm, w), lambda i: (i, 0))
    rowT = lambda r: pl.BlockSpec((r, tm), lambda i: (0, i))
    full = lambda a: pl.BlockSpec(a.shape, lambda i: (0,) * a.ndim)
    dils = [d for _, d in DILATED_PATTERNS]
    res_shape = lambda d: jax.ShapeDtypeStruct((batch, d, seq // d, A_WIDTH), BF16)
    res_spec = lambda d: pl.BlockSpec((1, d, tm // d, A_WIDTH), lambda i: (i // steps, 0, i % steps, 0))
    out_shapes = (
        jax.ShapeDtypeStruct((n, M_QK_WIDTH), BF16),
        jax.ShapeDtypeStruct((M_QK_WIDTH, n), BF16),
        jax.ShapeDtypeStruct((n, M_WIDTH), BF16),
        jax.ShapeDtypeStruct((n, M_WIDTH), BF16),
        jax.ShapeDtypeStruct((8, n), F32),
        jax.ShapeDtypeStruct((8, n), F32),
        *[res_shape(d) for d in dils], *[res_shape(d) for d in dils], *[res_shape(d) for d in dils],
        jax.ShapeDtypeStruct((n, D_MODEL), BF16),
        jax.ShapeDtypeStruct((n, D_MODEL), BF16),
    )
    out_specs = (row(M_QK_WIDTH), rowT(M_QK_WIDTH), row(M_WIDTH), row(M_WIDTH), rowT(8), rowT(8),
                 *[res_spec(d) for d in dils], *[res_spec(d) for d in dils], *[res_spec(d) for d in dils],
                 row(D_MODEL), row(D_MODEL))
    outs = pl.pallas_call(
        _inproj_body,
        grid=(n // tm,),
        in_specs=[row(D_MODEL), full(g1), full(wm), full(gb), full(gq_t), full(gk_t)],
        out_specs=out_specs,
        out_shape=out_shapes,
        scratch_shapes=[pltpu.VMEM((2, tm, LANES), F32)],
        compiler_params=_params("parallel"),
        name="inproj",
    )(x2d, g1, wm, gb, gq_t, gk_t)
    mq, kT, mv, so, gi, gf = outs[:6]
    aq, ak, av = outs[6:9], outs[9:12], outs[12:15]
    return mq, kT, mv, so, gi, gf, aq, ak, av, outs[15], outs[16]


M_CHUNK_LEN = 128


def _mlstm_body(q_ref, v_ref, so_ref, ng_ref, *rest, nchunk, nseq):
    kT_refs, gi_refs, gf_refs = rest[0:nseq], rest[nseq:2 * nseq], rest[2 * nseq:3 * nseq]
    o_ref, c_ref, m_ref = rest[3 * nseq:]
    L = M_CHUNK_LEN

    @pl.when(pl.program_id(0) == 0)
    def _():
        c_ref[...] = jnp.zeros_like(c_ref)
        m_ref[...] = jnp.zeros_like(m_ref)

    lane8 = lax.broadcasted_iota(I32, (8, L), 1)
    causal = lax.broadcasted_iota(I32, (L, L), 1) <= lax.broadcasted_iota(I32, (L, L), 0)
    lo_half = lax.broadcasted_iota(I32, (L, LANES), 1) < M_QK_DIM
    ones = jnp.ones((L, M_V_DIM), BF16)

    heads = range(M_HEADS)
    cstate = [[c_ref[s, h * M_QK_DIM:(h + 1) * M_QK_DIM, :] for h in heads] for s in range(nseq)]
    m_prev = [m_ref[s, :, 0:1] for s in range(nseq)]
    units = []
    for c in range(nchunk):
        rows = slice(c * L, (c + 1) * L)
        for s in range(nseq):
            gi = gi_refs[s][:, rows]
            b = gf_refs[s][:, rows]
            sh = 1
            while sh < L:
                b = b + jnp.where(lane8 >= sh, pltpu.roll(b, sh, 1), 0.0)
                sh *= 2
            u = gi - b
            g = b[:, L - 1:L]
            a = g + u
            amax = jnp.max(a, axis=1, keepdims=True)
            m_new = jnp.maximum(g + m_prev[s], amax)
            w = jnp.exp(a - m_new) * (M_QK_DIM ** -0.5)
            s_old = jnp.exp(g + m_prev[s] - m_new)
            vext = [jnp.concatenate([v_ref[s, rows, h * M_V_DIM:(h + 1) * M_V_DIM], ones], axis=1) for h in heads]
            cloc = []
            for h in heads:
                hr = slice(h * M_QK_DIM, (h + 1) * M_QK_DIM)
                kw = (kT_refs[s][hr, rows].astype(F32) * w[h:h + 1, :]).astype(BF16)
                cloc.append(jnp.dot(kw, vext[h], preferred_element_type=F32))
            units.append(dict(seq=s, rows=rows, b=b, u=u, m_prev=m_prev[s], state=cstate[s], vext=vext))
            cstate[s] = [s_old[h:h + 1, :] * cstate[s][h] + cloc[h] for h in heads]
            m_prev[s] = m_new
    for s in range(nseq):
        for h in heads:
            c_ref[s, h * M_QK_DIM:(h + 1) * M_QK_DIM, :] = cstate[s][h]
        m_ref[s] = jnp.broadcast_to(m_prev[s], (8, LANES))

    for un in units:
        s, rows = un["seq"], un["rows"]
        un["s"], un["qc"] = [], []
        for p in range(M_HEADS // 2):
            lanes_p = slice(p * LANES, (p + 1) * LANES)
            q_pair = q_ref[s, rows, lanes_p]
            kT_pair = kT_refs[s][lanes_p, rows]
            c_pair = jnp.concatenate([un["state"][2 * p], un["state"][2 * p + 1]], axis=0).astype(BF16)
            for hh in range(2):
                qm = jnp.where(lo_half if hh == 0 else jnp.logical_not(lo_half), q_pair, jnp.zeros_like(q_pair))
                un["s"].append(jnp.dot(qm, kT_pair, preferred_element_type=F32) * (M_QK_DIM ** -0.5))
                un["qc"].append(jnp.dot(qm, c_pair, preferred_element_type=F32))

    for un in units:
        s, rows, b, u, mp = un["seq"], un["rows"], un["b"], un["u"], un["m_prev"]
        for h in heads:
            hl = slice(h * M_V_DIM, (h + 1) * M_V_DIM)
            bcol = jnp.transpose(jnp.broadcast_to(b[h:h + 1, :], (L, L)))
            dm = jnp.where(causal, bcol + u[h:h + 1, :], -jnp.inf)
            inter = bcol + mp[h:h + 1, :]
            m_t = jnp.maximum(inter, jnp.max(dm, axis=1, keepdims=True))
            pmat = (un["s"][h] * jnp.exp(dm - m_t)).astype(BF16)
            sc = jnp.exp(inter - m_t)
            out = (jnp.dot(pmat, un["vext"][h], preferred_element_type=F32)
                   + jnp.concatenate([sc, sc], axis=1) * un["qc"][h])
            hv = out[:, :M_V_DIM] / jnp.maximum(jnp.abs(out[:, M_V_DIM:]), jnp.exp(-m_t))
            hn = hv * lax.rsqrt(jnp.mean(hv * hv, axis=1, keepdims=True) + EPS)
            hn = hn * ng_ref[:, hl] * so_ref[s, rows, hl].astype(F32)
            o_ref[s, rows, hl] = hn.astype(BF16)


def _stage_mlstm(mq, kT, mv, so, gi, gf, norm_g, batch, seq, rows_per_step):
    n = batch * seq
    R = rows_per_step
    steps = seq // R
    ng = norm_g.astype(F32).reshape(1, M_WIDTH)
    per_seq = lambda a: a.reshape(batch, seq, a.shape[1])
    row = lambda w: pl.BlockSpec((batch, R, w), lambda i: (0, i, 0))
    colT = lambda r, s: pl.BlockSpec((r, R), lambda i, s=s: (0, s * steps + i))
    seqs = range(batch)
    out = pl.pallas_call(
        functools.partial(_mlstm_body, nchunk=R // M_CHUNK_LEN, nseq=batch),
        grid=(steps,),
        in_specs=[row(M_QK_WIDTH), row(M_WIDTH), row(M_WIDTH), pl.BlockSpec((1, M_WIDTH), lambda i: (0, 0)),
                  *[colT(M_QK_WIDTH, s) for s in seqs], *[colT(8, s) for s in seqs], *[colT(8, s) for s in seqs]],
        out_specs=row(M_WIDTH),
        out_shape=jax.ShapeDtypeStruct((batch, seq, M_WIDTH), BF16),
        scratch_shapes=[pltpu.VMEM((batch, M_QK_WIDTH, 2 * M_V_DIM), F32), pltpu.VMEM((batch, 8, LANES), F32)],
        compiler_params=_params("arbitrary"),
        name="mlstm",
    )(per_seq(mq), per_seq(mv), per_seq(so), ng, *[kT] * batch, *[gi] * batch, *[gf] * batch)
    return out.reshape(n, M_WIDTH)


def _attn_body(q_ref, kp_ref, kc_ref, vp_ref, vc_ref, o_ref, lse_ref, *, dil, slopes, lq):
    QB = N_BACK
    first = pl.program_id(2) == 0
    qn = q_ref[0, 0]
    kcn = kc_ref[0, 0]
    kpn = kp_ref[0, 0]
    vc = vc_ref[0, 0]
    vp = vp_ref[0, 0]

    qi = lax.broadcasted_iota(I32, (QB, 2 * QB), 0)
    kj = lax.broadcasted_iota(I32, (QB, 2 * QB), 1)
    dist = qi + QB - kj
    band = jnp.logical_and(dist >= 0, dist <= N_BACK)
    distf = (dist * dil).astype(F32)
    bias = [jnp.where(band, -float(slopes[h]) * distf, -jnp.inf) for h in range(A_HEADS)]
    no_prev = jnp.logical_and(first, kj < QB)
    lo_half = lax.broadcasted_iota(I32, (QB, LANES), 1) < A_HEAD_DIM
    ones = jnp.ones((2 * QB, LANES), BF16)

    units = []
    for j in range(lq // QB):
        rows = slice(j * QB, (j + 1) * QB)
        prow = slice((j - 1) * QB, j * QB)
        keys = jnp.concatenate([kpn if j == 0 else kcn[prow], kcn[rows]], axis=0)
        vals = jnp.concatenate([vp if j == 0 else vc[prow], vc[rows]], axis=0)
        for p in range(A_HEADS // 2):
            lanes_p = slice(p * LANES, (p + 1) * LANES)
            q_pair = qn[rows, lanes_p]
            k_pair = keys[:, lanes_p]
            vext = jnp.concatenate([vals[:, lanes_p], ones], axis=1)
            scores = []
            for hh in range(2):
                sel = lo_half if hh == 0 else jnp.logical_not(lo_half)
                qm = jnp.where(sel, q_pair, jnp.zeros_like(q_pair))
                scores.append(lax.dot_general(qm, k_pair, _NT, preferred_element_type=F32))
            units.append((j, rows, lanes_p, p, vext, scores))

    for j, rows, lanes_p, p, vext, scores in units:
        o_pair = None
        l_pair = None
        for hh in range(2):
            s = scores[hh] + bias[2 * p + hh]
            if j == 0:
                s = jnp.where(no_prev, -jnp.inf, s)
            m = jnp.max(s, axis=1, keepdims=True)
            pv = jnp.dot(jnp.exp(s - m).astype(BF16), vext, preferred_element_type=F32)
            den = pv[:, LANES:]
            o_h = pv[:, :LANES] / den
            l_h = m + jnp.log(den)
            o_pair = o_h if hh == 0 else jnp.where(lo_half, o_pair, o_h)
            l_pair = l_h if hh == 0 else jnp.where(lo_half, l_pair, l_h)
        o_ref[0, 0, rows, lanes_p] = o_pair.astype(BF16)
        lse_ref[0, 0, rows, lanes_p] = l_pair


def _stage_attn(aq, ak, av, batch, seq, group):
    _, dil = DILATED_PATTERNS[group]
    L = seq // dil
    assert L % N_BACK == 0
    lq = min(1024, L)
    nq = L // lq
    sub = lq // N_BACK
    cur = pl.BlockSpec((1, 1, lq, A_WIDTH), lambda b, r, i: (b, r, i, 0))
    prev = pl.BlockSpec((1, 1, N_BACK, A_WIDTH), lambda b, r, i: (b, r, jnp.maximum(i * sub - 1, 0), 0))
    return pl.pallas_call(
        functools.partial(_attn_body, dil=dil, slopes=tuple(_alibi_slopes()[group]), lq=lq),
        grid=(batch, dil, nq),
        in_specs=[cur, prev, cur, prev, cur],
        out_specs=(cur, cur),
        out_shape=(jax.ShapeDtypeStruct((batch, dil, L, A_WIDTH), BF16),
                   jax.ShapeDtypeStruct((batch, dil, L, A_WIDTH), F32)),
        compiler_params=_params("parallel", "parallel", "parallel"),
        name=f"dilated_attn_d{dil}",
    )(aq, ak, ak, av, av)


TOKEN_TILE_ROWS = D_MODEL // LANES


def _store_token_tiles(ref, val):
    t = val.shape[0]
    for s in range(TOKEN_TILE_ROWS):
        ref[pl.ds(s, t, stride=TOKEN_TILE_ROWS), :] = val[:, s * LANES:(s + 1) * LANES]


def _load_token_tiles(ref, t):
    return jnp.concatenate([ref[pl.ds(s, t, stride=TOKEN_TILE_ROWS), :] for s in range(TOKEN_TILE_ROWS)], axis=1)


def _rows8(vals):
    t = vals[0].shape[1]
    rid = lax.broadcasted_iota(I32, (8, t), 0)
    out = jnp.zeros((8, t), vals[0].dtype)
    for k, v in enumerate(vals):
        out = jnp.where(rid == k, jnp.broadcast_to(v, (8, t)), out)
    return out


def _merge_body(hm_ref, o1_ref, o2_ref, o3_ref, l1_ref, l2_ref, l3_ref, sgm_ref, sga_ref, x_ref,
                wm_ref, wa_ref, wo_ref, g2_ref, wrh_ref, br_ref,
                x2_ref, xn_ref, loc_ref, gate_ref, tcnt_ref, tcar_ref, cnt_ref, carry_ref, st_ref):
    @pl.when(pl.program_id(0) == 0)
    def _():
        carry_ref[...] = jnp.zeros_like(carry_ref)

    m_branch = jnp.dot(hm_ref[...], wm_ref[...], preferred_element_type=F32)
    dils = [d for _, d in DILATED_PATTERNS]
    l1, l2, l3 = [_merge_residues(r, d, st_ref) for r, d in zip((l1_ref, l2_ref, l3_ref), dils)]
    lmax = jnp.maximum(jnp.maximum(l1, l2), l3)
    e1, e2, e3 = jnp.exp(l1 - lmax), jnp.exp(l2 - lmax), jnp.exp(l3 - lmax)
    num = e1 * _merge_residues(o1_ref, dils[0], st_ref)
    num = num + e2 * _merge_residues(o2_ref, dils[1], st_ref)
    num = num + e3 * _merge_residues(o3_ref, dils[2], st_ref)
    h_a = num / (e1 + e2 + e3)
    y = (sgm_ref[...].astype(F32) * m_branch
         + sga_ref[...].astype(F32) * jnp.dot(h_a.astype(BF16), wa_ref[...], preferred_element_type=F32))
    x2 = x_ref[...] + jnp.dot(y.astype(BF16), wo_ref[...], preferred_element_type=F32)
    x2_ref[...] = x2
    xn = x2 * lax.rsqrt(jnp.mean(x2 * x2, axis=-1, keepdims=True) + EPS) * g2_ref[...]
    xh = xn.astype(BF16)
    xn_ref[...] = xh

    logits = lax.dot_general(wrh_ref[...], xh, _NT, preferred_element_type=F32) + br_ref[...]
    t = logits.shape[1]
    eid = lax.broadcasted_iota(I32, (N_EXPERTS, t), 0).astype(F32)
    vals = logits
    top_v, top_i = [], []
    for _ in range(TOP_K):
        mx = jnp.max(vals, axis=0, keepdims=True)
        ik = jnp.min(jnp.where(vals == mx, eid, float(N_EXPERTS)), axis=0, keepdims=True)
        top_v.append(mx)
        top_i.append(ik)
        vals = jnp.where(eid == ik, -jnp.inf, vals)
    ex = [jnp.exp(v - top_v[0]) for v in top_v]
    den = ex[0] + ex[1] + ex[2] + ex[3]
    gate_ref[...] = _rows8([e / den for e in ex])

    chosen = jnp.zeros((N_EXPERTS, t), F32)
    for ik in top_i:
        chosen = chosen + (eid == ik).astype(F32)
    before = (lax.broadcasted_iota(I32, (t, t), 0) < lax.broadcasted_iota(I32, (t, t), 1)).astype(BF16)
    prefix = jnp.dot(chosen.astype(BF16), before, preferred_element_type=F32)
    tcount = jnp.broadcast_to(jnp.sum(chosen, axis=1, keepdims=True), (N_EXPERTS, LANES))
    below = (lax.broadcasted_iota(I32, (N_EXPERTS, N_EXPERTS), 1)
             < lax.broadcasted_iota(I32, (N_EXPERTS, N_EXPERTS), 0)).astype(BF16)
    t_hi = jnp.floor(tcount * (1.0 / BF16_EXACT_INT)) * BF16_EXACT_INT
    tile_off = (jnp.dot(below, t_hi.astype(BF16), preferred_element_type=F32)
                + jnp.dot(below, (tcount - t_hi).astype(BF16), preferred_element_type=F32))
    pos = prefix + tile_off[:, 0:1]
    loc_ref[...] = _rows8([jnp.sum(jnp.where(eid == ik, pos, 0.0), axis=0, keepdims=True).astype(I32)
                           for ik in top_i])
    carry = carry_ref[...]
    tcnt_ref[...] = tcount.astype(I32)
    tcar_ref[...] = carry.astype(I32)
    total = carry + tcount
    carry_ref[...] = total
    cnt_ref[...] = total


def _stage_merge(h_m, attn, sgm, sga, x2d, w_mb, w_ab, w_out, norm2_g, w_router, b_router, batch, seq, tm):
    n = x2d.shape[0]
    steps = seq // tm
    (o1, l1), (o2, l2), (o3, l3) = attn
    wm = w_mb.astype(BF16)
    wa = w_ab.astype(BF16)
    wo = w_out.astype(BF16)
    g2 = norm2_g.astype(F32).reshape(1, D_MODEL)
    wrh = w_router.astype(BF16).T
    br = b_router.astype(F32).reshape(N_EXPERTS, 1)
    row = lambda w: pl.BlockSpec((tm, w), lambda i: (i, 0))
    rowT = lambda r: pl.BlockSpec((r, tm), lambda i: (0, i))
    full = lambda a: pl.BlockSpec(a.shape, lambda i: (0,) * a.ndim)
    res = lambda d: pl.BlockSpec((1, d, tm // d, A_WIDTH), lambda i: (i // steps, 0, i % steps, 0))
    dils = [d for _, d in DILATED_PATTERNS]
    per_tile = pl.BlockSpec((N_EXPERTS, LANES), lambda i: (0, i))
    return pl.pallas_call(
        _merge_body,
        grid=(n // tm,),
        in_specs=[row(M_WIDTH), *[res(d) for d in dils], *[res(d) for d in dils],
                  row(D_MODEL), row(D_MODEL), row(D_MODEL),
                  full(wm), full(wa), full(wo), full(g2), full(wrh), full(br)],
        out_specs=(row(D_MODEL), row(D_MODEL), rowT(8), rowT(8), per_tile, per_tile,
                   pl.BlockSpec((N_EXPERTS, LANES), lambda i: (0, 0))),
        out_shape=(jax.ShapeDtypeStruct((n, D_MODEL), F32),
                   jax.ShapeDtypeStruct((n, D_MODEL), BF16),
                   jax.ShapeDtypeStruct((8, n), I32),
                   jax.ShapeDtypeStruct((8, n), F32),
                   jax.ShapeDtypeStruct((N_EXPERTS, (n // tm) * LANES), I32),
                   jax.ShapeDtypeStruct((N_EXPERTS, (n // tm) * LANES), I32),
                   jax.ShapeDtypeStruct((N_EXPERTS, LANES), F32)),
        scratch_shapes=[pltpu.VMEM((N_EXPERTS, LANES), F32), pltpu.VMEM((2, 2, tm, LANES), F32)],
        compiler_params=_params("arbitrary"),
        name="merge_route",
    )(h_m, o1, o2, o3, l1, l2, l3, sgm, sga, x2d, wm, wa, wo, g2, wrh, br)


def _offsets_body(cnt_ref, blk_ref, pstart_ref, zlo_ref, zhi_ref, *, nblk_pad):
    cnt = cnt_ref[...]
    padded = jnp.floor((cnt + (MOE_BLOCK - 1)) * (1.0 / MOE_BLOCK)) * MOE_BLOCK
    lower = (lax.broadcasted_iota(I32, (N_EXPERTS, N_EXPERTS), 1)
             <= lax.broadcasted_iota(I32, (N_EXPERTS, N_EXPERTS), 0)).astype(F32)
    pends = jnp.dot(lower, padded, precision=lax.Precision.HIGHEST, preferred_element_type=F32)
    pstart = pends - padded
    pstart_ref[...] = pstart.astype(I32)
    zlo_ref[...] = (pstart + cnt).astype(I32)
    zhi_ref[...] = pends.astype(I32)

    first_row = (lax.broadcasted_iota(I32, (N_EXPERTS, nblk_pad), 1) * MOE_BLOCK).astype(F32)
    pe = jnp.broadcast_to(pends[:, 0:1], (N_EXPERTS, nblk_pad))
    be = jnp.sum((pe <= first_row).astype(F32), axis=0, keepdims=True)
    be = jnp.minimum(be, float(N_EXPERTS - 1))
    nused = pends[N_EXPERTS - 1:N_EXPERTS, 0:1] * (1.0 / MOE_BLOCK)
    nonempty = jnp.broadcast_to(padded[:, 0:1], (N_EXPERTS, nblk_pad)) > 0.0
    runidx = jnp.sum(jnp.logical_and(pe <= first_row, nonempty).astype(F32), axis=0, keepdims=True)
    parity = runidx - 2.0 * jnp.floor(runidx * 0.5)
    eid = lax.broadcasted_iota(I32, (N_EXPERTS, nblk_pad), 0).astype(F32)
    later = jnp.logical_and(eid > be, nonempty)
    nxt = jnp.min(jnp.where(later, eid, float(N_EXPERTS)), axis=0, keepdims=True)
    blk_ref[...] = _rows8([be.astype(I32), jnp.broadcast_to(nused, (1, nblk_pad)).astype(I32),
                           parity.astype(I32), nxt.astype(I32)])


def _stage_offsets(cnt, nblk):
    nblk_pad = -(-nblk // LANES) * LANES
    const = lambda r, c: pl.BlockSpec((r, c), lambda i: (0, 0))
    per_expert = jax.ShapeDtypeStruct((N_EXPERTS, LANES), I32)
    return pl.pallas_call(
        functools.partial(_offsets_body, nblk_pad=nblk_pad),
        grid=(1,),
        in_specs=[const(N_EXPERTS, LANES)],
        out_specs=(const(8, nblk_pad), const(N_EXPERTS, LANES), const(N_EXPERTS, LANES), const(N_EXPERTS, LANES)),
        out_shape=(jax.ShapeDtypeStruct((8, nblk_pad), I32), per_expert, per_expert, per_expert),
        compiler_params=_params("arbitrary"),
        name="route_offsets",
    )(cnt)


RUN_BITS = 10


def _tile_rows(ref, first_row, nrows):
    start = first_row * TOKEN_TILE_ROWS
    if not isinstance(first_row, int):
        start = pl.multiple_of(start, TOKEN_TILE_ROWS)
    return ref.at[pl.ds(start, nrows * TOKEN_TILE_ROWS)]


def _for_each_piece(length, fn):
    for b in reversed(range(RUN_BITS)):
        @pl.when(((length >> b) & 1) == 1)
        def _(b=b):
            fn((length >> (b + 1)) << (b + 1), 1 << b)


def _for_each_run(tile, tcnt_ref, tcar_ref, pstart_ref, fn):
    def per_expert(e, local):
        count = tcnt_ref[tile, e]
        first = pstart_ref[e] + tcar_ref[tile, e]
        _for_each_piece(count, lambda off, size: fn(local + off, first + off, size))
        return local + count

    lax.fori_loop(0, N_EXPERTS, per_expert, 0)


PERM_CHUNK = 256


def _dispatch_body(tcnt_ref, tcar_ref, pstart_ref, zlo_ref, zhi_ref, loc_ref, xn_ref, xs_hbm,
                   buf_ref, sems, *, tm):
    step = pl.program_id(0)
    nloc_tiles = TOP_K * tm * TOKEN_TILE_ROWS

    def wait_buffer(slot):
        pltpu.make_async_copy(buf_ref.at[slot], xs_hbm.at[pl.ds(0, nloc_tiles)], sems.at[slot]).wait()

    for slot in range(2):
        tile = 2 * step + slot

        @pl.when(step > 0)
        def _(slot=slot):
            wait_buffer(slot)

        loc = loc_ref[:, slot * tm:(slot + 1) * tm]
        xn = xn_ref[slot * tm:(slot + 1) * tm, :]
        for c in range(TOP_K * tm // PERM_CHUNK):
            lid = lax.broadcasted_iota(I32, (PERM_CHUNK, tm), 0) + c * PERM_CHUNK
            hit = lid == loc[0:1, :]
            for k in range(1, TOP_K):
                hit = jnp.logical_or(hit, lid == loc[k:k + 1, :])
            rows = jnp.dot(jnp.where(hit, 1.0, 0.0).astype(BF16), xn, preferred_element_type=F32)
            _store_token_tiles(
                buf_ref.at[slot, pl.ds(c * PERM_CHUNK * TOKEN_TILE_ROWS, PERM_CHUNK * TOKEN_TILE_ROWS)], rows)

        def run_copy(local, first, size, slot=slot):
            return pltpu.make_async_copy(_tile_rows(buf_ref.at[slot], local, size),
                                         _tile_rows(xs_hbm, first, size), sems.at[slot])

        _for_each_run(tile, tcnt_ref, tcar_ref, pstart_ref, lambda l, f, s: run_copy(l, f, s).start())

    @pl.when(step == pl.num_programs(0) - 1)
    def _():
        wait_buffer(0)
        wait_buffer(1)
        zsrc = buf_ref.at[0]
        zsrc[pl.ds(0, MOE_BLOCK * TOKEN_TILE_ROWS), :] = jnp.zeros((MOE_BLOCK * TOKEN_TILE_ROWS, LANES), F32)

        def zero_copy(first, size):
            return pltpu.make_async_copy(_tile_rows(zsrc, 0, size), _tile_rows(xs_hbm, first, size), sems.at[0])

        def per_expert(e, carry):
            lo = zlo_ref[e]
            npad = zhi_ref[e] - lo
            _for_each_piece(npad, lambda off, size: zero_copy(lo + off, size).start())
            _for_each_piece(npad, lambda off, size: zero_copy(lo + off, size).wait())
            return carry

        lax.fori_loop(0, N_EXPERTS, per_expert, 0)

        first_unused = zhi_ref[N_EXPERTS - 1] // MOE_BLOCK
        nblk = xs_hbm.shape[0] // (MOE_BLOCK * TOKEN_TILE_ROWS)

        def tail(blk, carry):
            zero_copy(blk * MOE_BLOCK, MOE_BLOCK).start()
            zero_copy(blk * MOE_BLOCK, MOE_BLOCK).wait()
            return carry

        lax.fori_loop(first_unused, nblk, tail, 0)


def _stage_dispatch(tables, loc8, xn, nrows, tm):
    n = xn.shape[0]
    assert TOP_K * tm >= MOE_BLOCK and (n // tm) % 2 == 0
    grid_spec = pltpu.PrefetchScalarGridSpec(
        num_scalar_prefetch=5,
        grid=(n // (2 * tm),),
        in_specs=[pl.BlockSpec((8, 2 * tm), lambda i, *_: (0, i)),
                  pl.BlockSpec((2 * tm, D_MODEL), lambda i, *_: (i, 0))],
        out_specs=pl.BlockSpec(memory_space=pl.ANY),
        scratch_shapes=[pltpu.VMEM((2, TOP_K * tm * TOKEN_TILE_ROWS, LANES), F32), pltpu.SemaphoreType.DMA((2,))],
    )
    return pl.pallas_call(
        functools.partial(_dispatch_body, tm=tm),
        grid_spec=grid_spec,
        out_shape=jax.ShapeDtypeStruct((nrows * TOKEN_TILE_ROWS, LANES), F32),
        compiler_params=_params("arbitrary"),
        name="dispatch",
    )(*tables, loc8, xn)


EXPERT_BLOCKS_PER_STEP = 2


def _expert_body(be_ref, nu_ref, par_ref, nxt_ref, xs_ref, w1_hbm, b1_ref, w2_hbm, b2_ref, ys_ref,
                 w1f_ref, w2f_ref, w1b_ref, w2b_ref, sems):
    block_tiles = MOE_BLOCK * TOKEN_TILE_ROWS

    def fetch(expert, slot):
        return (pltpu.make_async_copy(w1_hbm.at[expert], w1f_ref.at[slot], sems.at[slot, 0]),
                pltpu.make_async_copy(w2_hbm.at[expert], w2f_ref.at[slot], sems.at[slot, 1]))

    for sub in range(EXPERT_BLOCKS_PER_STEP):
        j = pl.program_id(0) * EXPERT_BLOCKS_PER_STEP + sub
        used = j < nu_ref[0]
        jj = jnp.maximum(jnp.minimum(j, nu_ref[0] - 1), 0)
        e = be_ref[jj]
        fresh = jnp.logical_or(j == 0, e != be_ref[jnp.maximum(jj - 1, 0)])
        xs_blk = xs_ref.at[pl.ds(sub * block_tiles, block_tiles)]
        ys_blk = ys_ref.at[pl.ds(sub * block_tiles, block_tiles)]

        @pl.when(jnp.logical_and(used, fresh))
        def _(j=j, jj=jj, e=e):
            slot = par_ref[jj]

            @pl.when(j == 0)
            def _():
                for c in fetch(e, slot):
                    c.start()

            for c in fetch(e, slot):
                c.wait()
            nxt = nxt_ref[jj]

            @pl.when(nxt < N_EXPERTS)
            def _():
                for c in fetch(nxt, 1 - slot):
                    c.start()

            w1b_ref[...] = w1f_ref[slot].astype(BF16)
            w2b_ref[...] = w2f_ref[slot].astype(BF16)

        @pl.when(used)
        def _(e=e, xs_blk=xs_blk, ys_blk=ys_blk):
            xb = _load_token_tiles(xs_blk, MOE_BLOCK).astype(BF16)
            gu = jnp.dot(xb, w1b_ref[...], preferred_element_type=F32) + b1_ref[pl.ds(e, 1), :]
            gate = jnp.minimum(gu[:, :D_FF], SWIGLU_LIMIT)
            lin = jnp.clip(gu[:, D_FF:], -SWIGLU_LIMIT, SWIGLU_LIMIT)
            act = (lin + 1.0) * (gate * jax.nn.sigmoid(SWIGLU_ALPHA * gate))
            ys = jnp.dot(act.astype(BF16), w2b_ref[...], preferred_element_type=F32) + b2_ref[pl.ds(e, 1), :]
            _store_token_tiles(ys_blk, ys)

        @pl.when(jnp.logical_not(used))
        def _(ys_blk=ys_blk):
            ys_blk[...] = jnp.zeros(ys_blk.shape, F32)


def _stage_experts(blk8, xs, w1, b1, w2, b2):
    nrows = xs.shape[0] // TOKEN_TILE_ROWS
    nblk = nrows // MOE_BLOCK
    assert nblk % EXPERT_BLOCKS_PER_STEP == 0
    block_e, nused, parity, nxt = blk8[0, :nblk], blk8[1, :1], blk8[2, :nblk], blk8[3, :nblk]
    tiles = (EXPERT_BLOCKS_PER_STEP * MOE_BLOCK * TOKEN_TILE_ROWS, LANES)
    full = lambda a: pl.BlockSpec(a.shape, lambda j, *_: (0,) * a.ndim)
    grid_spec = pltpu.PrefetchScalarGridSpec(
        num_scalar_prefetch=4,
        grid=(nblk // EXPERT_BLOCKS_PER_STEP,),
        in_specs=[pl.BlockSpec(tiles, lambda j, *_: (j, 0)),
                  pl.BlockSpec(memory_space=pl.ANY), full(b1),
                  pl.BlockSpec(memory_space=pl.ANY), full(b2)],
        out_specs=pl.BlockSpec(tiles, lambda j, *_: (j, 0)),
        scratch_shapes=[pltpu.VMEM((2, D_MODEL, 2 * D_FF), F32), pltpu.VMEM((2, D_FF, D_MODEL), F32),
                        pltpu.VMEM((D_MODEL, 2 * D_FF), BF16), pltpu.VMEM((D_FF, D_MODEL), BF16),
                        pltpu.SemaphoreType.DMA((2, 2))],
    )
    return pl.pallas_call(
        _expert_body,
        grid_spec=grid_spec,
        out_shape=jax.ShapeDtypeStruct((nrows * TOKEN_TILE_ROWS, LANES), F32),
        compiler_params=_params("arbitrary"),
        name="experts",
    )(block_e, nused, parity, nxt, xs, w1, b1, w2, b2)


def _combine_body(tcnt_ref, tcar_ref, pstart_ref, loc_ref, gate_ref, x2_ref, ys_hbm, out_ref,
                  buf_ref, g_ref, sems, *, tm):
    step = pl.program_id(0)
    nloc = TOP_K * tm

    def start_runs(tile, slot):
        def run_copy(local, first, size):
            return pltpu.make_async_copy(_tile_rows(ys_hbm, first, size),
                                         _tile_rows(buf_ref.at[slot], local, size), sems.at[slot])
        _for_each_run(tile, tcnt_ref, tcar_ref, pstart_ref, lambda l, f, s: run_copy(l, f, s).start())

    def wait_buffer(slot):
        pltpu.make_async_copy(ys_hbm.at[pl.ds(0, nloc * TOKEN_TILE_ROWS)], buf_ref.at[slot], sems.at[slot]).wait()

    def combine(slot):
        zpad = jnp.zeros((LANES - 16, LANES), F32)
        lane = lax.broadcasted_iota(I32, (LANES, nloc), 1).astype(F32)
        for c in range(tm // LANES):
            cols_in = slice(slot * tm + c * LANES, slot * tm + (c + 1) * LANES)
            cols = jnp.transpose(jnp.concatenate([loc_ref[:, cols_in].astype(F32), gate_ref[:, cols_in], zpad], axis=0))
            g = jnp.zeros((LANES, nloc), F32)
            for k in range(TOP_K):
                g = jnp.where(lane == cols[:, k:k + 1], cols[:, 8 + k:9 + k], g)
            g_ref[c * LANES:(c + 1) * LANES, :] = g.astype(BF16)
        wait_buffer(slot)
        ys = _load_token_tiles(buf_ref.at[slot], nloc).astype(BF16)
        rows = slice(slot * tm, (slot + 1) * tm)
        out_ref[rows, :] = x2_ref[rows, :] + jnp.dot(g_ref[...], ys, preferred_element_type=F32)

    @pl.when(step == 0)
    def _():
        start_runs(0, 0)

    start_runs(2 * step + 1, 1)
    combine(0)

    @pl.when(step + 1 < pl.num_programs(0))
    def _():
        start_runs(2 * step + 2, 0)

    combine(1)


def _stage_combine(tables, loc8, gate8, x2, ys, tm):
    n = x2.shape[0]
    assert (n // tm) % 2 == 0
    grid_spec = pltpu.PrefetchScalarGridSpec(
        num_scalar_prefetch=3,
        grid=(n // (2 * tm),),
        in_specs=[pl.BlockSpec((8, 2 * tm), lambda i, *_: (0, i)),
                  pl.BlockSpec((8, 2 * tm), lambda i, *_: (0, i)),
                  pl.BlockSpec((2 * tm, D_MODEL), lambda i, *_: (i, 0)),
                  pl.BlockSpec(memory_space=pl.ANY)],
        out_specs=pl.BlockSpec((2 * tm, D_MODEL), lambda i, *_: (i, 0)),
        scratch_shapes=[pltpu.VMEM((2, TOP_K * tm * TOKEN_TILE_ROWS, LANES), F32),
                        pltpu.VMEM((tm, TOP_K * tm), BF16),
                        pltpu.SemaphoreType.DMA((2,))],
    )
    return pl.pallas_call(
        functools.partial(_combine_body, tm=tm),
        grid_spec=grid_spec,
        out_shape=jax.ShapeDtypeStruct((n, D_MODEL), F32),
        compiler_params=_params("arbitrary"),
        name="combine",
    )(*tables, loc8, gate8, x2, ys)


def _moe(x2, xn, loc8, gate8, tcnt, tcar, cnt, w1, b1, w2, b2, tm):
    n = x2.shape[0]
    ntile = n // tm
    nblk = -(-(n * TOP_K) // MOE_BLOCK) + N_EXPERTS
    blk8, pstart, zlo, zhi = _stage_offsets(cnt, nblk)
    per_tile = lambda a: a.reshape(N_EXPERTS, ntile, LANES)[:, :, 0].T
    tables = (per_tile(tcnt), per_tile(tcar), pstart[:, 0])
    xs = _stage_dispatch(tables + (zlo[:, 0], zhi[:, 0]), loc8, xn, nblk * MOE_BLOCK, tm)
    ys = _stage_experts(blk8, xs, w1, b1, w2, b2)
    return _stage_combine(tables, loc8, gate8, x2, ys, tm)


def kernel(x, norm1_g, w_in, mlstm_gate_b, mlstm_norm_g, attn_q_norm_g, attn_k_norm_g, w_mlstm_branch,
           w_attn_branch, w_out, norm2_g, w_router, b_router, w1, b1, w2, b2):
    batch, seq, _ = x.shape
    n = batch * seq
    for l in range(norm1_g.shape[0]):
        x2d = x.reshape(n, D_MODEL)
        tm = min(512, seq)
        mq, kT, mv, so, gi, gf, aq, ak, av, sgm, sga = _stage_inproj(
            x2d, norm1_g[l], w_in[l], mlstm_gate_b[l], attn_q_norm_g[l], attn_k_norm_g[l], batch, seq, tm)
        h_m = _stage_mlstm(mq, kT, mv, so, gi, gf, mlstm_norm_g[l], batch, seq, tm)
        attn = [_stage_attn(aq[g], ak[g], av[g], batch, seq, g)
                for g in range(N_GROUPS)]
        x2, xn, loc8, gate8, tcnt, tcar, cnt = _stage_merge(
            h_m, attn, sgm, sga, x2d, w_mlstm_branch[l], w_attn_branch[l], w_out[l], norm2_g[l],
            w_router[l], b_router[l], batch, seq, tm)
        out = _moe(x2, xn, loc8, gate8, tcnt, tcar, cnt, w1[l], b1[l], w2[l], b2[l], tm)
        x = out.reshape(batch, seq, D_MODEL)
    return x
```

```python
import functools

import numpy as np
import jax
import jax.numpy as jnp
from jax import lax
from jax.experimental import pallas as pl
from jax.experimental.pallas import tpu as pltpu

F32 = jnp.float32
BF16 = jnp.bfloat16
I32 = jnp.int32

D_MODEL = 1024
M_HEADS = 4
M_QK_DIM = 64
M_V_DIM = 128
GATE_SOFTCAP = 15.0
A_HEADS = 4
A_HEAD_DIM = 64
DILATED_PATTERNS = ((128, 1), (512, 4), (2048, 16))
N_GROUPS = len(DILATED_PATTERNS)
N_BACK = 128
N_EXPERTS = 32
TOP_K = 4
D_FF = 1024
SWIGLU_LIMIT = 7.0
SWIGLU_ALPHA = 1.702
MOE_BLOCK = 512
EPS = 1e-6

M_WIDTH = M_HEADS * M_V_DIM
M_QK_WIDTH = M_HEADS * M_QK_DIM
A_WIDTH = A_HEADS * A_HEAD_DIM
IN_SPLITS = (M_QK_WIDTH, M_QK_WIDTH, M_WIDTH, M_WIDTH, 2 * M_HEADS,
             N_GROUPS * A_WIDTH, N_GROUPS * A_WIDTH, N_GROUPS * A_WIDTH, D_MODEL, D_MODEL)

LANES = 128
VMEM_LIMIT = 56 * 1024 * 1024

BF16_EXACT_INT = 256.0

_NT = (((1,), (1,)), ((), ()))
_TN = (((0,), (1,)), ((), ()))


def _alibi_slopes():
    n = N_GROUPS * A_HEADS
    s = np.exp2(-8.0 * np.arange(1, n + 1) / n).astype(np.float32)
    return s.reshape(N_GROUPS, A_HEADS)


def _params(*sem):
    return pltpu.CompilerParams(dimension_semantics=sem, vmem_limit_bytes=VMEM_LIMIT)


def _log_sigmoid(x):
    return jnp.minimum(x, 0.0) - jnp.log1p(jnp.exp(-jnp.abs(x)))


_GATE_PAD = LANES - 2 * M_HEADS


def _segments():
    bounds, start = [], 0
    for i, width in enumerate(IN_SPLITS):
        bounds.append((start, start + width))
        start += width + (_GATE_PAD if i == 4 else 0)
    return bounds, start


_SEG, _W_MAIN = _segments()
_C_MQ, _C_MV, _C_MO, _C_AQ, _C_AK, _C_AV, _C_GM, _C_GA = (_SEG[i] for i in (0, 2, 3, 5, 6, 7, 8, 9))


def _split_residues(val, d, out_ref, st_ref):
    t = val.shape[0]
    if d == 1:
        out_ref[0, 0] = val.astype(out_ref.dtype)
        return
    st_ref[0] = val[:, :LANES]
    st_ref[1] = val[:, LANES:]
    for r in range(d):
        piece = jnp.concatenate([st_ref[0, pl.ds(r, t // d, stride=d), :],
                                 st_ref[1, pl.ds(r, t // d, stride=d), :]], axis=1)
        out_ref[0, r] = piece.astype(out_ref.dtype)


def _merge_residues(ref, d, st_ref):
    if d == 1:
        return ref[0, 0].astype(F32)
    m = ref.shape[2]
    if d == 16:
        a_ref, b_ref = st_ref.at[0], st_ref.at[1]
        for r0 in range(4):
            for r1 in range(4):
                blk = ref[0, 4 * r1 + r0].astype(F32)
                a_ref[0, pl.ds(r0 * 4 * m + r1, m, stride=4), :] = blk[:, :LANES]
                a_ref[1, pl.ds(r0 * 4 * m + r1, m, stride=4), :] = blk[:, LANES:]
        for r0 in range(4):
            b_ref[0, pl.ds(r0, 4 * m, stride=4), :] = a_ref[0, r0 * 4 * m:(r0 + 1) * 4 * m, :]
            b_ref[1, pl.ds(r0, 4 * m, stride=4), :] = a_ref[1, r0 * 4 * m:(r0 + 1) * 4 * m, :]
        return jnp.concatenate([b_ref[0], b_ref[1]], axis=1)
    a_ref = st_ref.at[0]
    for r in range(d):
        blk = ref[0, r].astype(F32)
        a_ref[0, pl.ds(r, m, stride=d), :] = blk[:, :LANES]
        a_ref[1, pl.ds(r, m, stride=d), :] = blk[:, LANES:]
    return jnp.concatenate([a_ref[0], a_ref[1]], axis=1)


def _inproj_body(x_ref, g1_ref, wm_ref, gb_ref, gq_ref, gk_ref,
                 mq_ref, kT_ref, mv_ref, so_ref, gi_ref, gf_ref,
                 q0_ref, q1_ref, q2_ref, k0_ref, k1_ref, k2_ref, v0_ref, v1_ref, v2_ref,
                 sgm_ref, sga_ref, st_ref):
    x = x_ref[...]
    h = x * lax.rsqrt(jnp.mean(x * x, axis=-1, keepdims=True) + EPS) * g1_ref[...]
    hb = h.astype(BF16)

    def seg(c):
        return jnp.dot(hb, wm_ref[:, c[0]:c[1]], preferred_element_type=F32)

    mq_ref[...] = seg(_C_MQ).astype(BF16)
    mv_ref[...] = seg(_C_MV).astype(BF16)
    so_ref[...] = jax.nn.sigmoid(seg(_C_MO)).astype(BF16)
    hid_r = lax.broadcasted_iota(I32, (A_WIDTH, A_WIDTH), 0) // A_HEAD_DIM
    hid_c = lax.broadcasted_iota(I32, (A_WIDTH, A_WIDTH), 1) // A_HEAD_DIM
    head_ones = (hid_r == hid_c).astype(BF16)
    for c, refs, gain_ref in ((_C_AQ, (q0_ref, q1_ref, q2_ref), gq_ref), (_C_AK, (k0_ref, k1_ref, k2_ref), gk_ref),
                              (_C_AV, (v0_ref, v1_ref, v2_ref), None)):
        val = seg(c)
        for g, ref in enumerate(refs):
            piece = val[:, g * A_WIDTH:(g + 1) * A_WIDTH]
            if gain_ref is not None:
                ss = jnp.dot((piece * piece).astype(BF16), head_ones, preferred_element_type=F32)
                piece = piece * lax.rsqrt(ss * (1.0 / A_HEAD_DIM) + EPS) * gain_ref[:, g * A_WIDTH:(g + 1) * A_WIDTH]
            _split_residues(piece, DILATED_PATTERNS[g][1], ref, st_ref)
    sgm_ref[...] = jax.nn.sigmoid(seg(_C_GM)).astype(BF16)
    sga_ref[...] = jax.nn.sigmoid(seg(_C_GA)).astype(BF16)

    kT_ref[...] = lax.dot_general(wm_ref[:, _SEG[1][0]:_SEG[1][1]], hb, _TN, preferred_element_type=F32).astype(BF16)
    zg = lax.dot_general(wm_ref[:, _SEG[4][0]:_SEG[4][0] + LANES], hb, _TN, preferred_element_type=F32)
    zi = zg[0:8] + gb_ref[0:8]
    zf = zg[M_HEADS:M_HEADS + 8] + gb_ref[8:16]
    gi_ref[...] = GATE_SOFTCAP * jnp.tanh(zi / GATE_SOFTCAP)
    gf_ref[...] = _log_sigmoid(GATE_SOFTCAP * jnp.tanh(zf / GATE_SOFTCAP))


def _stage_inproj(x2d, norm1_g, w_in, gate_b, gq, gk, batch, seq, tm):
    n = x2d.shape[0]
    steps = seq // tm
    cuts = np.concatenate([[0], np.cumsum(IN_SPLITS)])
    gate_end = int(cuts[5])
    wb = w_in.astype(BF16)
    wm = jnp.concatenate([wb[:, :gate_end], jnp.zeros((D_MODEL, _GATE_PAD), BF16), wb[:, gate_end:]], axis=1)
    gb = jnp.zeros((16, 1), F32)
    gb = gb.at[0:4, 0].set(gate_b[:M_HEADS].astype(F32)).at[8:12, 0].set(gate_b[M_HEADS:].astype(F32))
    g1 = norm1_g.astype(F32).reshape(1, D_MODEL)
    gq_t = (jnp.tile(gq.astype(F32), (1, A_HEADS)) * (A_HEAD_DIM ** -0.5)).reshape(1, N_GROUPS * A_WIDTH)
    gk_t = jnp.tile(gk.astype(F32), (1, A_HEADS)).reshape(1, N_GROUPS * A_WIDTH)

    row = lambda w: pl.BlockSpec((tm, w), lambda i: (i, 0))
    rowT = lambda r: pl.BlockSpec((r, tm), lambda i: (0, i))
    full = lambda a: pl.BlockSpec(a.shape, lambda i: (0,) * a.ndim)
    dils = [d for _, d in DILATED_PATTERNS]
    res_shape = lambda d: jax.ShapeDtypeStruct((batch, d, seq // d, A_WIDTH), BF16)
    res_spec = lambda d: pl.BlockSpec((1, d, tm // d, A_WIDTH), lambda i: (i // steps, 0, i % steps, 0))
    out_shapes = (
        jax.ShapeDtypeStruct((n, M_QK_WIDTH), BF16),
        jax.ShapeDtypeStruct((M_QK_WIDTH, n), BF16),
        jax.ShapeDtypeStruct((n, M_WIDTH), BF16),
        jax.ShapeDtypeStruct((n, M_WIDTH), BF16),
        jax.ShapeDtypeStruct((8, n), F32),
        jax.ShapeDtypeStruct((8, n), F32),
        *[res_shape(d) for d in dils], *[res_shape(d) for d in dils], *[res_shape(d) for d in dils],
        jax.ShapeDtypeStruct((n, D_MODEL), BF16),
        jax.ShapeDtypeStruct((n, D_MODEL), BF16),
    )
    out_specs = (row(M_QK_WIDTH), rowT(M_QK_WIDTH), row(M_WIDTH), row(M_WIDTH), rowT(8), rowT(8),
                 *[res_spec(d) for d in dils], *[res_spec(d) for d in dils], *[res_spec(d) for d in dils],
                 row(D_MODEL), row(D_MODEL))
    outs = pl.pallas_call(
        _inproj_body,
        grid=(n // tm,),
        in_specs=[row(D_MODEL), full(g1), full(wm), full(gb), full(gq_t), full(gk_t)],
        out_specs=out_specs,
        out_shape=out_shapes,
        scratch_shapes=[pltpu.VMEM((2, tm, LANES), F32)],
        compiler_params=_params("parallel"),
        name="inproj",
    )(x2d, g1, wm, gb, gq_t, gk_t)
    mq, kT, mv, so, gi, gf = outs[:6]
    aq, ak, av = outs[6:9], outs[9:12], outs[12:15]
    return mq, kT, mv, so, gi, gf, aq, ak, av, outs[15], outs[16]


M_CHUNK_LEN = 128


def _mlstm_body(q_ref, v_ref, so_ref, ng_ref, *rest, nchunk, nseq):
    kT_refs, gi_refs, gf_refs = rest[0:nseq], rest[nseq:2 * nseq], rest[2 * nseq:3 * nseq]
    o_ref, c_ref, m_ref = rest[3 * nseq:]
    L = M_CHUNK_LEN

    @pl.when(pl.program_id(0) == 0)
    def _():
        c_ref[...] = jnp.zeros_like(c_ref)
        m_ref[...] = jnp.zeros_like(m_ref)

    lane8 = lax.broadcasted_iota(I32, (8, L), 1)
    causal = lax.broadcasted_iota(I32, (L, L), 1) <= lax.broadcasted_iota(I32, (L, L), 0)
    lo_half = lax.broadcasted_iota(I32, (L, LANES), 1) < M_QK_DIM
    ones = jnp.ones((L, M_V_DIM), BF16)

    heads = range(M_HEADS)
    cstate = [[c_ref[s, h * M_QK_DIM:(h + 1) * M_QK_DIM, :] for h in heads] for s in range(nseq)]
    m_prev = [m_ref[s, :, 0:1] for s in range(nseq)]
    units = []
    for c in range(nchunk):
        rows = slice(c * L, (c + 1) * L)
        for s in range(nseq):
            gi = gi_refs[s][:, rows]
            b = gf_refs[s][:, rows]
            sh = 1
            while sh < L:
                b = b + jnp.where(lane8 >= sh, pltpu.roll(b, sh, 1), 0.0)
                sh *= 2
            u = gi - b
            g = b[:, L - 1:L]
            a = g + u
            amax = jnp.max(a, axis=1, keepdims=True)
            m_new = jnp.maximum(g + m_prev[s], amax)
            w = jnp.exp(a - m_new) * (M_QK_DIM ** -0.5)
            s_old = jnp.exp(g + m_prev[s] - m_new)
            vext = [jnp.concatenate([v_ref[s, rows, h * M_V_DIM:(h + 1) * M_V_DIM], ones], axis=1) for h in heads]
            cloc = []
            for h in heads:
                hr = slice(h * M_QK_DIM, (h + 1) * M_QK_DIM)
                kw = (kT_refs[s][hr, rows].astype(F32) * w[h:h + 1, :]).astype(BF16)
                cloc.append(jnp.dot(kw, vext[h], preferred_element_type=F32))
            units.append(dict(seq=s, rows=rows, b=b, u=u, m_prev=m_prev[s], state=cstate[s], vext=vext))
            cstate[s] = [s_old[h:h + 1, :] * cstate[s][h] + cloc[h] for h in heads]
            m_prev[s] = m_new
    for s in range(nseq):
        for h in heads:
            c_ref[s, h * M_QK_DIM:(h + 1) * M_QK_DIM, :] = cstate[s][h]
        m_ref[s] = jnp.broadcast_to(m_prev[s], (8, LANES))

    for un in units:
        s, rows = un["seq"], un["rows"]
        un["s"], un["qc"] = [], []
        for p in range(M_HEADS // 2):
            lanes_p = slice(p * LANES, (p + 1) * LANES)
            q_pair = q_ref[s, rows, lanes_p]
            kT_pair = kT_refs[s][lanes_p, rows]
            c_pair = jnp.concatenate([un["state"][2 * p], un["state"][2 * p + 1]], axis=0).astype(BF16)
            for hh in range(2):
                qm = jnp.where(lo_half if hh == 0 else jnp.logical_not(lo_half), q_pair, jnp.zeros_like(q_pair))
                un["s"].append(jnp.dot(qm, kT_pair, preferred_element_type=F32) * (M_QK_DIM ** -0.5))
                un["qc"].append(jnp.dot(qm, c_pair, preferred_element_type=F32))

    for un in units:
        s, rows, b, u, mp = un["seq"], un["rows"], un["b"], un["u"], un["m_prev"]
        for h in heads:
            hl = slice(h * M_V_DIM, (h + 1) * M_V_DIM)
            bcol = jnp.transpose(jnp.broadcast_to(b[h:h + 1, :], (L, L)))
            dm = jnp.where(causal, bcol + u[h:h + 1, :], -jnp.inf)
            inter = bcol + mp[h:h + 1, :]
            m_t = jnp.maximum(inter, jnp.max(dm, axis=1, keepdims=True))
            pmat = (un["s"][h] * jnp.exp(dm - m_t)).astype(BF16)
            sc = jnp.exp(inter - m_t)
            out = (jnp.dot(pmat, un["vext"][h], preferred_element_type=F32)
                   + jnp.concatenate([sc, sc], axis=1) * un["qc"][h])
            hv = out[:, :M_V_DIM] / jnp.maximum(jnp.abs(out[:, M_V_DIM:]), jnp.exp(-m_t))
            hn = hv * lax.rsqrt(jnp.mean(hv * hv, axis=1, keepdims=True) + EPS)
            hn = hn * ng_ref[:, hl] * so_ref[s, rows, hl].astype(F32)
            o_ref[s, rows, hl] = hn.astype(BF16)


def _stage_mlstm(mq, kT, mv, so, gi, gf, norm_g, batch, seq, rows_per_step):
    n = batch * seq
    R = rows_per_step
    steps = seq // R
    ng = norm_g.astype(F32).reshape(1, M_WIDTH)
    per_seq = lambda a: a.reshape(batch, seq, a.shape[1])
    row = lambda w: pl.BlockSpec((batch, R, w), lambda i: (0, i, 0))
    colT = lambda r, s: pl.BlockSpec((r, R), lambda i, s=s: (0, s * steps + i))
    seqs = range(batch)
    out = pl.pallas_call(
        functools.partial(_mlstm_body, nchunk=R // M_CHUNK_LEN, nseq=batch),
        grid=(steps,),
        in_specs=[row(M_QK_WIDTH), row(M_WIDTH), row(M_WIDTH), pl.BlockSpec((1, M_WIDTH), lambda i: (0, 0)),
                  *[colT(M_QK_WIDTH, s) for s in seqs], *[colT(8, s) for s in seqs], *[colT(8, s) for s in seqs]],
        out_specs=row(M_WIDTH),
        out_shape=jax.ShapeDtypeStruct((batch, seq, M_WIDTH), BF16),
        scratch_shapes=[pltpu.VMEM((batch, M_QK_WIDTH, 2 * M_V_DIM), F32), pltpu.VMEM((batch, 8, LANES), F32)],
        compiler_params=_params("arbitrary"),
        name="mlstm",
    )(per_seq(mq), per_seq(mv), per_seq(so), ng, *[kT] * batch, *[gi] * batch, *[gf] * batch)
    return out.reshape(n, M_WIDTH)


def _attn_body(q_ref, kp_ref, kc_ref, vp_ref, vc_ref, o_ref, lse_ref, *, dil, slopes, lq):
    QB = N_BACK
    first = pl.program_id(2) == 0
    qn = q_ref[0, 0]
    kcn = kc_ref[0, 0]
    kpn = kp_ref[0, 0]
    vc = vc_ref[0, 0]
    vp = vp_ref[0, 0]

    qi = lax.broadcasted_iota(I32, (QB, 2 * QB), 0)
    kj = lax.broadcasted_iota(I32, (QB, 2 * QB), 1)
    dist = qi + QB - kj
    band = jnp.logical_and(dist >= 0, dist <= N_BACK)
    distf = (dist * dil).astype(F32)
    bias = [jnp.where(band, -float(slopes[h]) * distf, -jnp.inf) for h in range(A_HEADS)]
    no_prev = jnp.logical_and(first, kj < QB)
    lo_half = lax.broadcasted_iota(I32, (QB, LANES), 1) < A_HEAD_DIM
    ones = jnp.ones((2 * QB, LANES), BF16)

    units = []
    for j in range(lq // QB):
        rows = slice(j * QB, (j + 1) * QB)
        prow = slice((j - 1) * QB, j * QB)
        keys = jnp.concatenate([kpn if j == 0 else kcn[prow], kcn[rows]], axis=0)
        vals = jnp.concatenate([vp if j == 0 else vc[prow], vc[rows]], axis=0)
        for p in range(A_HEADS // 2):
            lanes_p = slice(p * LANES, (p + 1) * LANES)
            q_pair = qn[rows, lanes_p]
            k_pair = keys[:, lanes_p]
            vext = jnp.concatenate([vals[:, lanes_p], ones], axis=1)
            scores = []
            for hh in range(2):
                sel = lo_half if hh == 0 else jnp.logical_not(lo_half)
                qm = jnp.where(sel, q_pair, jnp.zeros_like(q_pair))
                scores.append(lax.dot_general(qm, k_pair, _NT, preferred_element_type=F32))
            units.append((j, rows, lanes_p, p, vext, scores))

    for j, rows, lanes_p, p, vext, scores in units:
        o_pair = None
        l_pair = None
        for hh in range(2):
            s = scores[hh] + bias[2 * p + hh]
            if j == 0:
                s = jnp.where(no_prev, -jnp.inf, s)
            m = jnp.max(s, axis=1, keepdims=True)
            pv = jnp.dot(jnp.exp(s - m).astype(BF16), vext, preferred_element_type=F32)
            den = pv[:, LANES:]
            o_h = pv[:, :LANES] / den
            l_h = m + jnp.log(den)
            o_pair = o_h if hh == 0 else jnp.where(lo_half, o_pair, o_h)
            l_pair = l_h if hh == 0 else jnp.where(lo_half, l_pair, l_h)
        o_ref[0, 0, rows, lanes_p] = o_pair.astype(BF16)
        lse_ref[0, 0, rows, lanes_p] = l_pair


def _stage_attn(aq, ak, av, batch, seq, group):
    _, dil = DILATED_PATTERNS[group]
    L = seq // dil
    assert L % N_BACK == 0
    lq = min(1024, L)
    nq = L // lq
    sub = lq // N_BACK
    cur = pl.BlockSpec((1, 1, lq, A_WIDTH), lambda b, r, i: (b, r, i, 0))
    prev = pl.BlockSpec((1, 1, N_BACK, A_WIDTH), lambda b, r, i: (b, r, jnp.maximum(i * sub - 1, 0), 0))
    return pl.pallas_call(
        functools.partial(_attn_body, dil=dil, slopes=tuple(_alibi_slopes()[group]), lq=lq),
        grid=(batch, dil, nq),
        in_specs=[cur, prev, cur, prev, cur],
        out_specs=(cur, cur),
        out_shape=(jax.ShapeDtypeStruct((batch, dil, L, A_WIDTH), BF16),
                   jax.ShapeDtypeStruct((batch, dil, L, A_WIDTH), F32)),
        compiler_params=_params("parallel", "parallel", "parallel"),
        name=f"dilated_attn_d{dil}",
    )(aq, ak, ak, av, av)


TOKEN_TILE_ROWS = D_MODEL // LANES


def _store_token_tiles(ref, val):
    t = val.shape[0]
    for s in range(TOKEN_TILE_ROWS):
        ref[pl.ds(s, t, stride=TOKEN_TILE_ROWS), :] = val[:, s * LANES:(s + 1) * LANES]


def _load_token_tiles(ref, t):
    return jnp.concatenate([ref[pl.ds(s, t, stride=TOKEN_TILE_ROWS), :] for s in range(TOKEN_TILE_ROWS)], axis=1)


def _rows8(vals):
    t = vals[0].shape[1]
    rid = lax.broadcasted_iota(I32, (8, t), 0)
    out = jnp.zeros((8, t), vals[0].dtype)
    for k, v in enumerate(vals):
        out = jnp.where(rid == k, jnp.broadcast_to(v, (8, t)), out)
    return out


def _merge_body(hm_ref, o1_ref, o2_ref, o3_ref, l1_ref, l2_ref, l3_ref, sgm_ref, sga_ref, x_ref,
                wm_ref, wa_ref, wo_ref, g2_ref, wrh_ref, br_ref,
                x2_ref, xn_ref, loc_ref, gate_ref, tcnt_ref, tcar_ref, cnt_ref, carry_ref, st_ref):
    @pl.when(pl.program_id(0) == 0)
    def _():
        carry_ref[...] = jnp.zeros_like(carry_ref)

    m_branch = jnp.dot(hm_ref[...], wm_ref[...], preferred_element_type=F32)
    dils = [d for _, d in DILATED_PATTERNS]
    l1, l2, l3 = [_merge_residues(r, d, st_ref) for r, d in zip((l1_ref, l2_ref, l3_ref), dils)]
    lmax = jnp.maximum(jnp.maximum(l1, l2), l3)
    e1, e2, e3 = jnp.exp(l1 - lmax), jnp.exp(l2 - lmax), jnp.exp(l3 - lmax)
    num = e1 * _merge_residues(o1_ref, dils[0], st_ref)
    num = num + e2 * _merge_residues(o2_ref, dils[1], st_ref)
    num = num + e3 * _merge_residues(o3_ref, dils[2], st_ref)
    h_a = num / (e1 + e2 + e3)
    y = (sgm_ref[...].astype(F32) * m_branch
         + sga_ref[...].astype(F32) * jnp.dot(h_a.astype(BF16), wa_ref[...], preferred_element_type=F32))
    x2 = x_ref[...] + jnp.dot(y.astype(BF16), wo_ref[...], preferred_element_type=F32)
    x2_ref[...] = x2
    xn = x2 * lax.rsqrt(jnp.mean(x2 * x2, axis=-1, keepdims=True) + EPS) * g2_ref[...]
    xh = xn.astype(BF16)
    xn_ref[...] = xh

    logits = lax.dot_general(wrh_ref[...], xh, _NT, preferred_element_type=F32) + br_ref[...]
    t = logits.shape[1]
    eid = lax.broadcasted_iota(I32, (N_EXPERTS, t), 0).astype(F32)
    vals = logits
    top_v, top_i = [], []
    for _ in range(TOP_K):
        mx = jnp.max(vals, axis=0, keepdims=True)
        ik = jnp.min(jnp.where(vals == mx, eid, float(N_EXPERTS)), axis=0, keepdims=True)
        top_v.append(mx)
        top_i.append(ik)
        vals = jnp.where(eid == ik, -jnp.inf, vals)
    ex = [jnp.exp(v - top_v[0]) for v in top_v]
    den = ex[0] + ex[1] + ex[2] + ex[3]
    gate_ref[...] = _rows8([e / den for e in ex])

    chosen = jnp.zeros((N_EXPERTS, t), F32)
    for ik in top_i:
        chosen = chosen + (eid == ik).astype(F32)
    before = (lax.broadcasted_iota(I32, (t, t), 0) < lax.broadcasted_iota(I32, (t, t), 1)).astype(BF16)
    prefix = jnp.dot(chosen.astype(BF16), before, preferred_element_type=F32)
    tcount = jnp.broadcast_to(jnp.sum(chosen, axis=1, keepdims=True), (N_EXPERTS, LANES))
    below = (lax.broadcasted_iota(I32, (N_EXPERTS, N_EXPERTS), 1)
             < lax.broadcasted_iota(I32, (N_EXPERTS, N_EXPERTS), 0)).astype(BF16)
    t_hi = jnp.floor(tcount * (1.0 / BF16_EXACT_INT)) * BF16_EXACT_INT
    tile_off = (jnp.dot(below, t_hi.astype(BF16), preferred_element_type=F32)
                + jnp.dot(below, (tcount - t_hi).astype(BF16), preferred_element_type=F32))
    pos = prefix + tile_off[:, 0:1]
    loc_ref[...] = _rows8([jnp.sum(jnp.where(eid == ik, pos, 0.0), axis=0, keepdims=True).astype(I32)
                           for ik in top_i])
    carry = carry_ref[...]
    tcnt_ref[...] = tcount.astype(I32)
    tcar_ref[...] = carry.astype(I32)
    total = carry + tcount
    carry_ref[...] = total
    cnt_ref[...] = total


def _stage_merge(h_m, attn, sgm, sga, x2d, w_mb, w_ab, w_out, norm2_g, w_router, b_router, batch, seq, tm):
    n = x2d.shape[0]
    steps = seq // tm
    (o1, l1), (o2, l2), (o3, l3) = attn
    wm = w_mb.astype(BF16)
    wa = w_ab.astype(BF16)
    wo = w_out.astype(BF16)
    g2 = norm2_g.astype(F32).reshape(1, D_MODEL)
    wrh = w_router.astype(BF16).T
    br = b_router.astype(F32).reshape(N_EXPERTS, 1)
    row = lambda w: pl.BlockSpec((tm, w), lambda i: (i, 0))
    rowT = lambda r: pl.BlockSpec((r, tm), lambda i: (0, i))
    full = lambda a: pl.BlockSpec(a.shape, lambda i: (0,) * a.ndim)
    res = lambda d: pl.BlockSpec((1, d, tm // d, A_WIDTH), lambda i: (i // steps, 0, i % steps, 0))
    dils = [d for _, d in DILATED_PATTERNS]
    per_tile = pl.BlockSpec((N_EXPERTS, LANES), lambda i: (0, i))
    return pl.pallas_call(
        _merge_body,
        grid=(n // tm,),
        in_specs=[row(M_WIDTH), *[res(d) for d in dils], *[res(d) for d in dils],
                  row(D_MODEL), row(D_MODEL), row(D_MODEL),
                  full(wm), full(wa), full(wo), full(g2), full(wrh), full(br)],
        out_specs=(row(D_MODEL), row(D_MODEL), rowT(8), rowT(8), per_tile, per_tile,
                   pl.BlockSpec((N_EXPERTS, LANES), lambda i: (0, 0))),
        out_shape=(jax.ShapeDtypeStruct((n, D_MODEL), F32),
                   jax.ShapeDtypeStruct((n, D_MODEL), BF16),
                   jax.ShapeDtypeStruct((8, n), I32),
                   jax.ShapeDtypeStruct((8, n), F32),
                   jax.ShapeDtypeStruct((N_EXPERTS, (n // tm) * LANES), I32),
                   jax.ShapeDtypeStruct((N_EXPERTS, (n // tm) * LANES), I32),
                   jax.ShapeDtypeStruct((N_EXPERTS, LANES), F32)),
        scratch_shapes=[pltpu.VMEM((N_EXPERTS, LANES), F32), pltpu.VMEM((2, 2, tm, LANES), F32)],
        compiler_params=_params("arbitrary"),
        name="merge_route",
    )(h_m, o1, o2, o3, l1, l2, l3, sgm, sga, x2d, wm, wa, wo, g2, wrh, br)


def _offsets_body(cnt_ref, blk_ref, pstart_ref, zlo_ref, zhi_ref, *, nblk_pad):
    cnt = cnt_ref[...]
    padded = jnp.floor((cnt + (MOE_BLOCK - 1)) * (1.0 / MOE_BLOCK)) * MOE_BLOCK
    lower = (lax.broadcasted_iota(I32, (N_EXPERTS, N_EXPERTS), 1)
             <= lax.broadcasted_iota(I32, (N_EXPERTS, N_EXPERTS), 0)).astype(BF16)
    nb = padded * (1.0 / MOE_BLOCK)
    nb_hi = jnp.floor(nb * (1.0 / BF16_EXACT_INT)) * BF16_EXACT_INT
    pends = (jnp.dot(lower, nb_hi.astype(BF16), preferred_element_type=F32)
             + jnp.dot(lower, (nb - nb_hi).astype(BF16), preferred_element_type=F32)) * MOE_BLOCK
    pstart = pends - padded
    pstart_ref[...] = pstart.astype(I32)
    zlo_ref[...] = (pstart + cnt).astype(I32)
    zhi_ref[...] = pends.astype(I32)

    first_row = (lax.broadcasted_iota(I32, (N_EXPERTS, nblk_pad), 1) * MOE_BLOCK).astype(F32)
    pe = jnp.broadcast_to(pends[:, 0:1], (N_EXPERTS, nblk_pad))
    be = jnp.sum((pe <= first_row).astype(F32), axis=0, keepdims=True)
    be = jnp.minimum(be, float(N_EXPERTS - 1))
    nused = pends[N_EXPERTS - 1:N_EXPERTS, 0:1] * (1.0 / MOE_BLOCK)
    nonempty = jnp.broadcast_to(padded[:, 0:1], (N_EXPERTS, nblk_pad)) > 0.0
    runidx = jnp.sum(jnp.logical_and(pe <= first_row, nonempty).astype(F32), axis=0, keepdims=True)
    parity = runidx - 2.0 * jnp.floor(runidx * 0.5)
    eid = lax.broadcasted_iota(I32, (N_EXPERTS, nblk_pad), 0).astype(F32)
    later = jnp.logical_and(eid > be, nonempty)
    nxt = jnp.min(jnp.where(later, eid, float(N_EXPERTS)), axis=0, keepdims=True)
    blk_ref[...] = _rows8([be.astype(I32), jnp.broadcast_to(nused, (1, nblk_pad)).astype(I32),
                           parity.astype(I32), nxt.astype(I32)])


def _stage_offsets(cnt, nblk):
    nblk_pad = -(-nblk // LANES) * LANES
    const = lambda r, c: pl.BlockSpec((r, c), lambda i: (0, 0))
    per_expert = jax.ShapeDtypeStruct((N_EXPERTS, LANES), I32)
    return pl.pallas_call(
        functools.partial(_offsets_body, nblk_pad=nblk_pad),
        grid=(1,),
        in_specs=[const(N_EXPERTS, LANES)],
        out_specs=(const(8, nblk_pad), const(N_EXPERTS, LANES), const(N_EXPERTS, LANES), const(N_EXPERTS, LANES)),
        out_shape=(jax.ShapeDtypeStruct((8, nblk_pad), I32), per_expert, per_expert, per_expert),
        compiler_params=_params("arbitrary"),
        name="route_offsets",
    )(cnt)


RUN_BITS = 10


def _tile_rows(ref, first_row, nrows):
    start = first_row * TOKEN_TILE_ROWS
    if not isinstance(first_row, int):
        start = pl.multiple_of(start, TOKEN_TILE_ROWS)
    return ref.at[pl.ds(start, nrows * TOKEN_TILE_ROWS)]


def _for_each_piece(length, fn):
    for b in reversed(range(RUN_BITS)):
        @pl.when(((length >> b) & 1) == 1)
        def _(b=b):
            fn((length >> (b + 1)) << (b + 1), 1 << b)


def _for_each_run(tile, tcnt_ref, tcar_ref, pstart_ref, fn):
    def per_expert(e, local):
        count = tcnt_ref[tile, e]
        first = pstart_ref[e] + tcar_ref[tile, e]
        _for_each_piece(count, lambda off, size: fn(local + off, first + off, size))
        return local + count

    lax.fori_loop(0, N_EXPERTS, per_expert, 0)


PERM_CHUNK = 256


def _dispatch_body(tcnt_ref, tcar_ref, pstart_ref, zlo_ref, zhi_ref, loc_ref, xn_ref, xs_hbm,
                   buf_ref, sems, *, tm):
    step = pl.program_id(0)
    nloc_tiles = TOP_K * tm * TOKEN_TILE_ROWS

    def wait_buffer(slot):
        pltpu.make_async_copy(buf_ref.at[slot], xs_hbm.at[pl.ds(0, nloc_tiles)], sems.at[slot]).wait()

    for slot in range(2):
        tile = 2 * step + slot

        @pl.when(step > 0)
        def _(slot=slot):
            wait_buffer(slot)

        loc = loc_ref[:, slot * tm:(slot + 1) * tm]
        xn = xn_ref[slot * tm:(slot + 1) * tm, :]
        for c in range(TOP_K * tm // PERM_CHUNK):
            lid = lax.broadcasted_iota(I32, (PERM_CHUNK, tm), 0) + c * PERM_CHUNK
            hit = lid == loc[0:1, :]
            for k in range(1, TOP_K):
                hit = jnp.logical_or(hit, lid == loc[k:k + 1, :])
            rows = jnp.dot(jnp.where(hit, 1.0, 0.0).astype(BF16), xn, preferred_element_type=F32)
            _store_token_tiles(
                buf_ref.at[slot, pl.ds(c * PERM_CHUNK * TOKEN_TILE_ROWS, PERM_CHUNK * TOKEN_TILE_ROWS)], rows)

        def run_copy(local, first, size, slot=slot):
            return pltpu.make_async_copy(_tile_rows(buf_ref.at[slot], local, size),
                                         _tile_rows(xs_hbm, first, size), sems.at[slot])

        _for_each_run(tile, tcnt_ref, tcar_ref, pstart_ref, lambda l, f, s: run_copy(l, f, s).start())

    @pl.when(step == pl.num_programs(0) - 1)
    def _():
        wait_buffer(0)
        wait_buffer(1)
        zsrc = buf_ref.at[0]
        zsrc[pl.ds(0, MOE_BLOCK * TOKEN_TILE_ROWS), :] = jnp.zeros((MOE_BLOCK * TOKEN_TILE_ROWS, LANES), F32)

        def zero_copy(first, size):
            return pltpu.make_async_copy(_tile_rows(zsrc, 0, size), _tile_rows(xs_hbm, first, size), sems.at[0])

        def per_expert(e, carry):
            lo = zlo_ref[e]
            npad = zhi_ref[e] - lo
            _for_each_piece(npad, lambda off, size: zero_copy(lo + off, size).start())
            _for_each_piece(npad, lambda off, size: zero_copy(lo + off, size).wait())
            return carry

        lax.fori_loop(0, N_EXPERTS, per_expert, 0)

        first_unused = zhi_ref[N_EXPERTS - 1] // MOE_BLOCK
        nblk = xs_hbm.shape[0] // (MOE_BLOCK * TOKEN_TILE_ROWS)

        def tail(blk, carry):
            zero_copy(blk * MOE_BLOCK, MOE_BLOCK).start()
            zero_copy(blk * MOE_BLOCK, MOE_BLOCK).wait()
            return carry

        lax.fori_loop(first_unused, nblk, tail, 0)


def _stage_dispatch(tables, loc8, xn, nrows, tm):
    n = xn.shape[0]
    assert TOP_K * tm >= MOE_BLOCK and (n // tm) % 2 == 0
    grid_spec = pltpu.PrefetchScalarGridSpec(
        num_scalar_prefetch=5,
        grid=(n // (2 * tm),),
        in_specs=[pl.BlockSpec((8, 2 * tm), lambda i, *_: (0, i)),
                  pl.BlockSpec((2 * tm, D_MODEL), lambda i, *_: (i, 0))],
        out_specs=pl.BlockSpec(memory_space=pl.ANY),
        scratch_shapes=[pltpu.VMEM((2, TOP_K * tm * TOKEN_TILE_ROWS, LANES), F32), pltpu.SemaphoreType.DMA((2,))],
    )
    return pl.pallas_call(
        functools.partial(_dispatch_body, tm=tm),
        grid_spec=grid_spec,
        out_shape=jax.ShapeDtypeStruct((nrows * TOKEN_TILE_ROWS, LANES), F32),
        compiler_params=_params("arbitrary"),
        name="dispatch",
    )(*tables, loc8, xn)


EXPERT_BLOCKS_PER_STEP = 2


def _expert_body(be_ref, nu_ref, par_ref, nxt_ref, xs_ref, w1_hbm, b1_ref, w2_hbm, b2_ref, ys_ref,
                 w1f_ref, w2f_ref, w1b_ref, w2b_ref, sems):
    block_tiles = MOE_BLOCK * TOKEN_TILE_ROWS

    def fetch(expert, slot):
        return (pltpu.make_async_copy(w1_hbm.at[expert], w1f_ref.at[slot], sems.at[slot, 0]),
                pltpu.make_async_copy(w2_hbm.at[expert], w2f_ref.at[slot], sems.at[slot, 1]))

    for sub in range(EXPERT_BLOCKS_PER_STEP):
        j = pl.program_id(0) * EXPERT_BLOCKS_PER_STEP + sub
        used = j < nu_ref[0]
        jj = jnp.maximum(jnp.minimum(j, nu_ref[0] - 1), 0)
        e = be_ref[jj]
        fresh = jnp.logical_or(j == 0, e != be_ref[jnp.maximum(jj - 1, 0)])
        xs_blk = xs_ref.at[pl.ds(sub * block_tiles, block_tiles)]
        ys_blk = ys_ref.at[pl.ds(sub * block_tiles, block_tiles)]

        @pl.when(jnp.logical_and(used, fresh))
        def _(j=j, jj=jj, e=e):
            slot = par_ref[jj]

            @pl.when(j == 0)
            def _():
                for c in fetch(e, slot):
                    c.start()

            for c in fetch(e, slot):
                c.wait()
            nxt = nxt_ref[jj]

            @pl.when(nxt < N_EXPERTS)
            def _():
                for c in fetch(nxt, 1 - slot):
                    c.start()

            w1b_ref[...] = w1f_ref[slot].astype(BF16)
            w2b_ref[...] = w2f_ref[slot].astype(BF16)

        @pl.when(used)
        def _(e=e, xs_blk=xs_blk, ys_blk=ys_blk):
            xb = _load_token_tiles(xs_blk, MOE_BLOCK).astype(BF16)
            gu = jnp.dot(xb, w1b_ref[...], preferred_element_type=F32) + b1_ref[pl.ds(e, 1), :]
            gate = jnp.minimum(gu[:, :D_FF], SWIGLU_LIMIT)
            lin = jnp.clip(gu[:, D_FF:], -SWIGLU_LIMIT, SWIGLU_LIMIT)
            act = (lin + 1.0) * (gate * jax.nn.sigmoid(SWIGLU_ALPHA * gate))
            ys = jnp.dot(act.astype(BF16), w2b_ref[...], preferred_element_type=F32) + b2_ref[pl.ds(e, 1), :]
            _store_token_tiles(ys_blk, ys)

        @pl.when(jnp.logical_not(used))
        def _(ys_blk=ys_blk):
            ys_blk[...] = jnp.zeros(ys_blk.shape, F32)


def _stage_experts(blk8, xs, w1, b1, w2, b2):
    nrows = xs.shape[0] // TOKEN_TILE_ROWS
    nblk = nrows // MOE_BLOCK
    assert nblk % EXPERT_BLOCKS_PER_STEP == 0
    block_e, nused, parity, nxt = blk8[0, :nblk], blk8[1, :1], blk8[2, :nblk], blk8[3, :nblk]
    tiles = (EXPERT_BLOCKS_PER_STEP * MOE_BLOCK * TOKEN_TILE_ROWS, LANES)
    full = lambda a: pl.BlockSpec(a.shape, lambda j, *_: (0,) * a.ndim)
    grid_spec = pltpu.PrefetchScalarGridSpec(
        num_scalar_prefetch=4,
        grid=(nblk // EXPERT_BLOCKS_PER_STEP,),
        in_specs=[pl.BlockSpec(tiles, lambda j, *_: (j, 0)),
                  pl.BlockSpec(memory_space=pl.ANY), full(b1),
                  pl.BlockSpec(memory_space=pl.ANY), full(b2)],
        out_specs=pl.BlockSpec(tiles, lambda j, *_: (j, 0)),
        scratch_shapes=[pltpu.VMEM((2, D_MODEL, 2 * D_FF), F32), pltpu.VMEM((2, D_FF, D_MODEL), F32),
                        pltpu.VMEM((D_MODEL, 2 * D_FF), BF16), pltpu.VMEM((D_FF, D_MODEL), BF16),
                        pltpu.SemaphoreType.DMA((2, 2))],
    )
    return pl.pallas_call(
        _expert_body,
        grid_spec=grid_spec,
        out_shape=jax.ShapeDtypeStruct((nrows * TOKEN_TILE_ROWS, LANES), F32),
        compiler_params=_params("arbitrary"),
        name="experts",
    )(block_e, nused, parity, nxt, xs, w1, b1, w2, b2)


def _combine_body(tcnt_ref, tcar_ref, pstart_ref, loc_ref, gate_ref, x2_ref, ys_hbm, out_ref,
                  buf_ref, g_ref, sems, *, tm):
    step = pl.program_id(0)
    nloc = TOP_K * tm

    def start_runs(tile, slot):
        def run_copy(local, first, size):
            return pltpu.make_async_copy(_tile_rows(ys_hbm, first, size),
                                         _tile_rows(buf_ref.at[slot], local, size), sems.at[slot])
        _for_each_run(tile, tcnt_ref, tcar_ref, pstart_ref, lambda l, f, s: run_copy(l, f, s).start())

    def wait_buffer(slot):
        pltpu.make_async_copy(ys_hbm.at[pl.ds(0, nloc * TOKEN_TILE_ROWS)], buf_ref.at[slot], sems.at[slot]).wait()

    def combine(slot):
        zpad = jnp.zeros((LANES - 16, LANES), F32)
        lane = lax.broadcasted_iota(I32, (LANES, nloc), 1).astype(F32)
        for c in range(tm // LANES):
            cols_in = slice(slot * tm + c * LANES, slot * tm + (c + 1) * LANES)
            cols = jnp.transpose(jnp.concatenate([loc_ref[:, cols_in].astype(F32), gate_ref[:, cols_in], zpad], axis=0))
            g = jnp.zeros((LANES, nloc), F32)
            for k in range(TOP_K):
                g = jnp.where(lane == cols[:, k:k + 1], cols[:, 8 + k:9 + k], g)
            g_ref[c * LANES:(c + 1) * LANES, :] = g.astype(BF16)
        wait_buffer(slot)
        ys = _load_token_tiles(buf_ref.at[slot], nloc).astype(BF16)
        rows = slice(slot * tm, (slot + 1) * tm)
        out_ref[rows, :] = x2_ref[rows, :] + jnp.dot(g_ref[...], ys, preferred_element_type=F32)

    @pl.when(step == 0)
    def _():
        start_runs(0, 0)

    start_runs(2 * step + 1, 1)
    combine(0)

    @pl.when(step + 1 < pl.num_programs(0))
    def _():
        start_runs(2 * step + 2, 0)

    combine(1)


def _stage_combine(tables, loc8, gate8, x2, ys, tm):
    n = x2.shape[0]
    assert (n // tm) % 2 == 0
    grid_spec = pltpu.PrefetchScalarGridSpec(
        num_scalar_prefetch=3,
        grid=(n // (2 * tm),),
        in_specs=[pl.BlockSpec((8, 2 * tm), lambda i, *_: (0, i)),
                  pl.BlockSpec((8, 2 * tm), lambda i, *_: (0, i)),
                  pl.BlockSpec((2 * tm, D_MODEL), lambda i, *_: (i, 0)),
                  pl.BlockSpec(memory_space=pl.ANY)],
        out_specs=pl.BlockSpec((2 * tm, D_MODEL), lambda i, *_: (i, 0)),
        scratch_shapes=[pltpu.VMEM((2, TOP_K * tm * TOKEN_TILE_ROWS, LANES), F32),
                        pltpu.VMEM((tm, TOP_K * tm), BF16),
                        pltpu.SemaphoreType.DMA((2,))],
    )
    return pl.pallas_call(
        functools.partial(_combine_body, tm=tm),
        grid_spec=grid_spec,
        out_shape=jax.ShapeDtypeStruct((n, D_MODEL), F32),
        compiler_params=_params("arbitrary"),
        name="combine",
    )(*tables, loc8, gate8, x2, ys)


def _moe(x2, xn, loc8, gate8, tcnt, tcar, cnt, w1, b1, w2, b2, tm):
    n = x2.shape[0]
    ntile = n // tm
    nblk = -(-(n * TOP_K) // MOE_BLOCK) + N_EXPERTS
    blk8, pstart, zlo, zhi = _stage_offsets(cnt, nblk)
    per_tile = lambda a: a.reshape(N_EXPERTS, ntile, LANES)[:, :, 0].T
    tables = (per_tile(tcnt), per_tile(tcar), pstart[:, 0])
    xs = _stage_dispatch(tables + (zlo[:, 0], zhi[:, 0]), loc8, xn, nblk * MOE_BLOCK, tm)
    ys = _stage_experts(blk8, xs, w1, b1, w2, b2)
    return _stage_combine(tables, loc8, gate8, x2, ys, tm)


def kernel(x, norm1_g, w_in, mlstm_gate_b, mlstm_norm_g, attn_q_norm_g, attn_k_norm_g, w_mlstm_branch,
           w_attn_branch, w_out, norm2_g, w_router, b_router, w1, b1, w2, b2):
    batch, seq, _ = x.shape
    n = batch * seq
    for l in range(norm1_g.shape[0]):
        x2d = x.reshape(n, D_MODEL)
        tm = min(512, seq)
        mq, kT, mv, so, gi, gf, aq, ak, av, sgm, sga = _stage_inproj(
            x2d, norm1_g[l], w_in[l], mlstm_gate_b[l], attn_q_norm_g[l], attn_k_norm_g[l], batch, seq, tm)
        h_m = _stage_mlstm(mq, kT, mv, so, gi, gf, mlstm_norm_g[l], batch, seq, tm)
        attn = [_stage_attn(aq[g], ak[g], av[g], batch, seq, g)
                for g in range(N_GROUPS)]
        x2, xn, loc8, gate8, tcnt, tcar, cnt = _stage_merge(
            h_m, attn, sgm, sga, x2d, w_mlstm_branch[l], w_attn_branch[l], w_out[l], norm2_g[l],
            w_router[l], b_router[l], batch, seq, tm)
        out = _moe(x2, xn, loc8, gate8, tcnt, tcar, cnt, w1[l], b1[l], w2[l], b2[l], tm)
        x = out.reshape(batch, seq, D_MODEL)
    return x
```

```python
import functools

import numpy as np
import jax
import jax.numpy as jnp
from jax import lax
from jax.experimental import pallas as pl
from jax.experimental.pallas import tpu as pltpu

F32 = jnp.float32
BF16 = jnp.bfloat16
I32 = jnp.int32

D_MODEL = 1024
M_HEADS = 4
M_QK_DIM = 64
M_V_DIM = 128
GATE_SOFTCAP = 15.0
A_HEADS = 4
A_HEAD_DIM = 64
DILATED_PATTERNS = ((128, 1), (512, 4), (2048, 16))
N_GROUPS = len(DILATED_PATTERNS)
N_BACK = 128
N_EXPERTS = 32
TOP_K = 4
D_FF = 1024
SWIGLU_LIMIT = 7.0
SWIGLU_ALPHA = 1.702
MOE_BLOCK = 512
EPS = 1e-6

M_WIDTH = M_HEADS * M_V_DIM
M_QK_WIDTH = M_HEADS * M_QK_DIM
A_WIDTH = A_HEADS * A_HEAD_DIM
IN_SPLITS = (M_QK_WIDTH, M_QK_WIDTH, M_WIDTH, M_WIDTH, 2 * M_HEADS,
             N_GROUPS * A_WIDTH, N_GROUPS * A_WIDTH, N_GROUPS * A_WIDTH, D_MODEL, D_MODEL)

LANES = 128
VMEM_LIMIT = 56 * 1024 * 1024

BF16_EXACT_INT = 256.0

_NT = (((1,), (1,)), ((), ()))
_TN = (((0,), (1,)), ((), ()))


def _alibi_slopes():
    n = N_GROUPS * A_HEADS
    s = np.exp2(-8.0 * np.arange(1, n + 1) / n).astype(np.float32)
    return s.reshape(N_GROUPS, A_HEADS)


def _params(*sem):
    return pltpu.CompilerParams(dimension_semantics=sem, vmem_limit_bytes=VMEM_LIMIT)


def _log_sigmoid(x):
    return jnp.minimum(x, 0.0) - jnp.log1p(jnp.exp(-jnp.abs(x)))


_A_WIDTH = sum(IN_SPLITS[:4])
_B_START = _A_WIDTH + IN_SPLITS[4]


def _piece_segments(widths):
    bounds, start = [], 0
    for width in widths:
        bounds.append((start, start + width))
        start += width
    return bounds


_C_MQ, _C_MK, _C_MV, _C_MO = _piece_segments(IN_SPLITS[:4])
_C_AQ, _C_AK, _C_AV, _C_GM, _C_GA = _piece_segments(IN_SPLITS[5:])


def _split_residues(val, d, out_ref, st_ref):
    t = val.shape[0]
    if d == 1:
        out_ref[0, 0] = val.astype(out_ref.dtype)
        return
    st_ref[0] = val[:, :LANES]
    st_ref[1] = val[:, LANES:]
    for r in range(d):
        piece = jnp.concatenate([st_ref[0, pl.ds(r, t // d, stride=d), :],
                                 st_ref[1, pl.ds(r, t // d, stride=d), :]], axis=1)
        out_ref[0, r] = piece.astype(out_ref.dtype)


def _merge_residues(ref, d, st_ref):
    if d == 1:
        return ref[0, 0].astype(F32)
    m = ref.shape[2]
    if d == 16:
        a_ref, b_ref = st_ref.at[0], st_ref.at[1]
        for r0 in range(4):
            for r1 in range(4):
                blk = ref[0, 4 * r1 + r0].astype(F32)
                a_ref[0, pl.ds(r0 * 4 * m + r1, m, stride=4), :] = blk[:, :LANES]
                a_ref[1, pl.ds(r0 * 4 * m + r1, m, stride=4), :] = blk[:, LANES:]
        for r0 in range(4):
            b_ref[0, pl.ds(r0, 4 * m, stride=4), :] = a_ref[0, r0 * 4 * m:(r0 + 1) * 4 * m, :]
            b_ref[1, pl.ds(r0, 4 * m, stride=4), :] = a_ref[1, r0 * 4 * m:(r0 + 1) * 4 * m, :]
        return jnp.concatenate([b_ref[0], b_ref[1]], axis=1)
    a_ref = st_ref.at[0]
    for r in range(d):
        blk = ref[0, r].astype(F32)
        a_ref[0, pl.ds(r, m, stride=d), :] = blk[:, :LANES]
        a_ref[1, pl.ds(r, m, stride=d), :] = blk[:, LANES:]
    return jnp.concatenate([a_ref[0], a_ref[1]], axis=1)


def _inproj_body(x_ref, g1_ref, wa_ref, wg_ref, wb_ref, gb_ref, gq_ref, gk_ref,
                 mq_ref, kT_ref, mv_ref, so_ref, gi_ref, gf_ref,
                 q0_ref, q1_ref, q2_ref, k0_ref, k1_ref, k2_ref, v0_ref, v1_ref, v2_ref,
                 sgm_ref, sga_ref, st_ref):
    x = x_ref[...]
    h = x * lax.rsqrt(jnp.mean(x * x, axis=-1, keepdims=True) + EPS) * g1_ref[...]
    hb = h.astype(BF16)

    def seg(w_ref, c):
        return jnp.dot(hb, w_ref[:, c[0]:c[1]], preferred_element_type=F32)

    mq_ref[...] = seg(wa_ref, _C_MQ).astype(BF16)
    mv_ref[...] = seg(wa_ref, _C_MV).astype(BF16)
    so_ref[...] = jax.nn.sigmoid(seg(wa_ref, _C_MO)).astype(BF16)
    hid_r = lax.broadcasted_iota(I32, (A_WIDTH, A_WIDTH), 0) // A_HEAD_DIM
    hid_c = lax.broadcasted_iota(I32, (A_WIDTH, A_WIDTH), 1) // A_HEAD_DIM
    head_ones = (hid_r == hid_c).astype(BF16)
    for c, refs, gain_ref in ((_C_AQ, (q0_ref, q1_ref, q2_ref), gq_ref), (_C_AK, (k0_ref, k1_ref, k2_ref), gk_ref),
                              (_C_AV, (v0_ref, v1_ref, v2_ref), None)):
        val = seg(wb_ref, c)
        for g, ref in enumerate(refs):
            piece = val[:, g * A_WIDTH:(g + 1) * A_WIDTH]
            if gain_ref is not None:
                ss = jnp.dot((piece * piece).astype(BF16), head_ones, preferred_element_type=F32)
                piece = piece * lax.rsqrt(ss * (1.0 / A_HEAD_DIM) + EPS) * gain_ref[:, g * A_WIDTH:(g + 1) * A_WIDTH]
            _split_residues(piece, DILATED_PATTERNS[g][1], ref, st_ref)
    sgm_ref[...] = jax.nn.sigmoid(seg(wb_ref, _C_GM)).astype(BF16)
    sga_ref[...] = jax.nn.sigmoid(seg(wb_ref, _C_GA)).astype(BF16)

    kT_ref[...] = lax.dot_general(wa_ref[:, _C_MK[0]:_C_MK[1]], hb, _TN, preferred_element_type=F32).astype(BF16)
    zg = lax.dot_general(wg_ref[...], hb, _TN, preferred_element_type=F32)
    zi = zg[0:8] + gb_ref[0:8]
    zf = zg[M_HEADS:M_HEADS + 8] + gb_ref[8:16]
    gi_ref[...] = GATE_SOFTCAP * jnp.tanh(zi / GATE_SOFTCAP)
    gf_ref[...] = _log_sigmoid(GATE_SOFTCAP * jnp.tanh(zf / GATE_SOFTCAP))


def _stage_inproj(x2d, norm1_g, w_in, gate_b, gq, gk, batch, seq, tm):
    n = x2d.shape[0]
    steps = seq // tm
    wa = w_in[:, :_A_WIDTH].astype(BF16)
    wg = jnp.pad(w_in[:, _A_WIDTH:_B_START], ((0, 0), (0, LANES - IN_SPLITS[4]))).astype(BF16)
    wb = w_in[:, _B_START:].astype(BF16)
    gb = jnp.zeros((16, 1), F32)
    gb = gb.at[0:4, 0].set(gate_b[:M_HEADS].astype(F32)).at[8:12, 0].set(gate_b[M_HEADS:].astype(F32))
    g1 = norm1_g.astype(F32).reshape(1, D_MODEL)
    gq_t = (jnp.tile(gq.astype(F32), (1, A_HEADS)) * (A_HEAD_DIM ** -0.5)).reshape(1, N_GROUPS * A_WIDTH)
    gk_t = jnp.tile(gk.astype(F32), (1, A_HEADS)).reshape(1, N_GROUPS * A_WIDTH)

    row = lambda w: pl.BlockSpec((tm, w), lambda i: (i, 0))
    rowT = lambda r: pl.BlockSpec((r, tm), lambda i: (0, i))
    full = lambda a: pl.BlockSpec(a.shape, lambda i: (0,) * a.ndim)
    dils = [d for _, d in DILATED_PATTERNS]
    res_shape = lambda d: jax.ShapeDtypeStruct((batch, d, seq // d, A_WIDTH), BF16)
    res_spec = lambda d: pl.BlockSpec((1, d, tm // d, A_WIDTH), lambda i: (i // steps, 0, i % steps, 0))
    out_shapes = (
        jax.ShapeDtypeStruct((n, M_QK_WIDTH), BF16),
        jax.ShapeDtypeStruct((M_QK_WIDTH, n), BF16),
        jax.ShapeDtypeStruct((n, M_WIDTH), BF16),
        jax.ShapeDtypeStruct((n, M_WIDTH), BF16),
        jax.ShapeDtypeStruct((8, n), F32),
        jax.ShapeDtypeStruct((8, n), F32),
        *[res_shape(d) for d in dils], *[res_shape(d) for d in dils], *[res_shape(d) for d in dils],
        jax.ShapeDtypeStruct((n, D_MODEL), BF16),
        jax.ShapeDtypeStruct((n, D_MODEL), BF16),
    )
    out_specs = (row(M_QK_WIDTH), rowT(M_QK_WIDTH), row(M_WIDTH), row(M_WIDTH), rowT(8), rowT(8),
                 *[res_spec(d) for d in dils], *[res_spec(d) for d in dils], *[res_spec(d) for d in dils],
                 row(D_MODEL), row(D_MODEL))
    outs = pl.pallas_call(
        _inproj_body,
        grid=(n // tm,),
        in_specs=[row(D_MODEL), full(g1), full(wa), full(wg), full(wb), full(gb), full(gq_t), full(gk_t)],
        out_specs=out_specs,
        out_shape=out_shapes,
        scratch_shapes=[pltpu.VMEM((2, tm, LANES), F32)],
        compiler_params=_params("parallel"),
        name="inproj",
    )(x2d, g1, wa, wg, wb, gb, gq_t, gk_t)
    mq, kT, mv, so, gi, gf = outs[:6]
    aq, ak, av = outs[6:9], outs[9:12], outs[12:15]
    return mq, kT, mv, so, gi, gf, aq, ak, av, outs[15], outs[16]


M_CHUNK_LEN = 128


def _mlstm_body(q_ref, v_ref, so_ref, ng_ref, *rest, nchunk, nseq):
    kT_refs, gi_refs, gf_refs = rest[0:nseq], rest[nseq:2 * nseq], rest[2 * nseq:3 * nseq]
    o_ref, c_ref, m_ref = rest[3 * nseq:]
    L = M_CHUNK_LEN

    @pl.when(pl.program_id(0) == 0)
    def _():
        c_ref[...] = jnp.zeros_like(c_ref)
        m_ref[...] = jnp.zeros_like(m_ref)

    lane8 = lax.broadcasted_iota(I32, (8, L), 1)
    causal = lax.broadcasted_iota(I32, (L, L), 1) <= lax.broadcasted_iota(I32, (L, L), 0)
    lo_half = lax.broadcasted_iota(I32, (L, LANES), 1) < M_QK_DIM
    ones = jnp.ones((L, M_V_DIM), BF16)

    heads = range(M_HEADS)
    cstate = [[c_ref[s, h * M_QK_DIM:(h + 1) * M_QK_DIM, :] for h in heads] for s in range(nseq)]
    m_prev = [m_ref[s, :, 0:1] for s in range(nseq)]
    units = []
    for c in range(nchunk):
        rows = slice(c * L, (c + 1) * L)
        for s in range(nseq):
            gi = gi_refs[s][:, rows]
            b = gf_refs[s][:, rows]
            sh = 1
            while sh < L:
                b = b + jnp.where(lane8 >= sh, pltpu.roll(b, sh, 1), 0.0)
                sh *= 2
            u = gi - b
            g = b[:, L - 1:L]
            a = g + u
            amax = jnp.max(a, axis=1, keepdims=True)
            m_new = jnp.maximum(g + m_prev[s], amax)
            w = jnp.exp(a - m_new) * (M_QK_DIM ** -0.5)
            s_old = jnp.exp(g + m_prev[s] - m_new)
            vext = [jnp.concatenate([v_ref[s, rows, h * M_V_DIM:(h + 1) * M_V_DIM], ones], axis=1) for h in heads]
            cloc = []
            for h in heads:
                hr = slice(h * M_QK_DIM, (h + 1) * M_QK_DIM)
                kw = (kT_refs[s][hr, rows].astype(F32) * w[h:h + 1, :]).astype(BF16)
                cloc.append(jnp.dot(kw, vext[h], preferred_element_type=F32))
            units.append(dict(seq=s, rows=rows, b=b, u=u, m_prev=m_prev[s], state=cstate[s], vext=vext))
            cstate[s] = [s_old[h:h + 1, :] * cstate[s][h] + cloc[h] for h in heads]
            m_prev[s] = m_new
    for s in range(nseq):
        for h in heads:
            c_ref[s, h * M_QK_DIM:(h + 1) * M_QK_DIM, :] = cstate[s][h]
        m_ref[s] = jnp.broadcast_to(m_prev[s], (8, LANES))

    for un in units:
        s, rows = un["seq"], un["rows"]
        un["s"], un["qc"] = [], []
        for p in range(M_HEADS // 2):
            lanes_p = slice(p * LANES, (p + 1) * LANES)
            q_pair = q_ref[s, rows, lanes_p]
            kT_pair = kT_refs[s][lanes_p, rows]
            c_pair = jnp.concatenate([un["state"][2 * p], un["state"][2 * p + 1]], axis=0).astype(BF16)
            for hh in range(2):
                qm = jnp.where(lo_half if hh == 0 else jnp.logical_not(lo_half), q_pair, jnp.zeros_like(q_pair))
                un["s"].append(jnp.dot(qm, kT_pair, preferred_element_type=F32) * (M_QK_DIM ** -0.5))
                un["qc"].append(jnp.dot(qm, c_pair, preferred_element_type=F32))

    for un in units:
        s, rows, b, u, mp = un["seq"], un["rows"], un["b"], un["u"], un["m_prev"]
        for h in heads:
            hl = slice(h * M_V_DIM, (h + 1) * M_V_DIM)
            bcol = jnp.transpose(jnp.broadcast_to(b[h:h + 1, :], (L, L)))
            dm = jnp.where(causal, bcol + u[h:h + 1, :], -jnp.inf)
            inter = bcol + mp[h:h + 1, :]
            m_t = jnp.maximum(inter, jnp.max(dm, axis=1, keepdims=True))
            pmat = (un["s"][h] * jnp.exp(dm - m_t)).astype(BF16)
            sc = jnp.exp(inter - m_t)
            out = (jnp.dot(pmat, un["vext"][h], preferred_element_type=F32)
                   + jnp.concatenate([sc, sc], axis=1) * un["qc"][h])
            hv = out[:, :M_V_DIM] / jnp.maximum(jnp.abs(out[:, M_V_DIM:]), jnp.exp(-m_t))
            hn = hv * lax.rsqrt(jnp.mean(hv * hv, axis=1, keepdims=True) + EPS)
            hn = hn * ng_ref[:, hl] * so_ref[s, rows, hl].astype(F32)
            o_ref[s, rows, hl] = hn.astype(BF16)


def _stage_mlstm(mq, kT, mv, so, gi, gf, norm_g, batch, seq, rows_per_step):
    n = batch * seq
    R = rows_per_step
    steps = seq // R
    ng = norm_g.astype(F32).reshape(1, M_WIDTH)
    per_seq = lambda a: a.reshape(batch, seq, a.shape[1])
    row = lambda w: pl.BlockSpec((batch, R, w), lambda i: (0, i, 0))
    colT = lambda r, s: pl.BlockSpec((r, R), lambda i, s=s: (0, s * steps + i))
    seqs = range(batch)
    out = pl.pallas_call(
        functools.partial(_mlstm_body, nchunk=R // M_CHUNK_LEN, nseq=batch),
        grid=(steps,),
        in_specs=[row(M_QK_WIDTH), row(M_WIDTH), row(M_WIDTH), pl.BlockSpec((1, M_WIDTH), lambda i: (0, 0)),
                  *[colT(M_QK_WIDTH, s) for s in seqs], *[colT(8, s) for s in seqs], *[colT(8, s) for s in seqs]],
        out_specs=row(M_WIDTH),
        out_shape=jax.ShapeDtypeStruct((batch, seq, M_WIDTH), BF16),
        scratch_shapes=[pltpu.VMEM((batch, M_QK_WIDTH, 2 * M_V_DIM), F32), pltpu.VMEM((batch, 8, LANES), F32)],
        compiler_params=_params("arbitrary"),
        name="mlstm",
    )(per_seq(mq), per_seq(mv), per_seq(so), ng, *[kT] * batch, *[gi] * batch, *[gf] * batch)
    return out.reshape(n, M_WIDTH)


def _attn_body(q_ref, kp_ref, kc_ref, vp_ref, vc_ref, o_ref, lse_ref, *, dil, slopes, lq):
    QB = N_BACK
    first = pl.program_id(2) == 0
    qn = q_ref[0, 0]
    kcn = kc_ref[0, 0]
    kpn = kp_ref[0, 0]
    vc = vc_ref[0, 0]
    vp = vp_ref[0, 0]

    qi = lax.broadcasted_iota(I32, (QB, 2 * QB), 0)
    kj = lax.broadcasted_iota(I32, (QB, 2 * QB), 1)
    dist = qi + QB - kj
    band = jnp.logical_and(dist >= 0, dist <= N_BACK)
    distf = (dist * dil).astype(F32)
    bias = [jnp.where(band, -float(slopes[h]) * distf, -jnp.inf) for h in range(A_HEADS)]
    no_prev = jnp.logical_and(first, kj < QB)
    lo_half = lax.broadcasted_iota(I32, (QB, LANES), 1) < A_HEAD_DIM
    ones = jnp.ones((2 * QB, LANES), BF16)

    units = []
    for j in range(lq // QB):
        rows = slice(j * QB, (j + 1) * QB)
        prow = slice((j - 1) * QB, j * QB)
        keys = jnp.concatenate([kpn if j == 0 else kcn[prow], kcn[rows]], axis=0)
        vals = jnp.concatenate([vp if j == 0 else vc[prow], vc[rows]], axis=0)
        for p in range(A_HEADS // 2):
            lanes_p = slice(p * LANES, (p + 1) * LANES)
            q_pair = qn[rows, lanes_p]
            k_pair = keys[:, lanes_p]
            vext = jnp.concatenate([vals[:, lanes_p], ones], axis=1)
            scores = []
            for hh in range(2):
                sel = lo_half if hh == 0 else jnp.logical_not(lo_half)
                qm = jnp.where(sel, q_pair, jnp.zeros_like(q_pair))
                scores.append(lax.dot_general(qm, k_pair, _NT, preferred_element_type=F32))
            units.append((j, rows, lanes_p, p, vext, scores))

    for j, rows, lanes_p, p, vext, scores in units:
        o_pair = None
        l_pair = None
        for hh in range(2):
            s = scores[hh] + bias[2 * p + hh]
            if j == 0:
                s = jnp.where(no_prev, -jnp.inf, s)
            m = jnp.max(s, axis=1, keepdims=True)
            pv = jnp.dot(jnp.exp(s - m).astype(BF16), vext, preferred_element_type=F32)
            den = pv[:, LANES:]
            o_h = pv[:, :LANES] / den
            l_h = m + jnp.log(den)
            o_pair = o_h if hh == 0 else jnp.where(lo_half, o_pair, o_h)
            l_pair = l_h if hh == 0 else jnp.where(lo_half, l_pair, l_h)
        o_ref[0, 0, rows, lanes_p] = o_pair.astype(BF16)
        lse_ref[0, 0, rows, lanes_p] = l_pair


def _stage_attn(aq, ak, av, batch, seq, group):
    _, dil = DILATED_PATTERNS[group]
    L = seq // dil
    assert L % N_BACK == 0
    lq = min(1024, L)
    nq = L // lq
    sub = lq // N_BACK
    cur = pl.BlockSpec((1, 1, lq, A_WIDTH), lambda b, r, i: (b, r, i, 0))
    prev = pl.BlockSpec((1, 1, N_BACK, A_WIDTH), lambda b, r, i: (b, r, jnp.maximum(i * sub - 1, 0), 0))
    return pl.pallas_call(
        functools.partial(_attn_body, dil=dil, slopes=tuple(_alibi_slopes()[group]), lq=lq),
        grid=(batch, dil, nq),
        in_specs=[cur, prev, cur, prev, cur],
        out_specs=(cur, cur),
        out_shape=(jax.ShapeDtypeStruct((batch, dil, L, A_WIDTH), BF16),
                   jax.ShapeDtypeStruct((batch, dil, L, A_WIDTH), F32)),
        compiler_params=_params("parallel", "parallel", "parallel"),
        name=f"dilated_attn_d{dil}",
    )(aq, ak, ak, av, av)


TOKEN_TILE_ROWS = D_MODEL // LANES


def _store_token_tiles(ref, val):
    t = val.shape[0]
    for s in range(TOKEN_TILE_ROWS):
        ref[pl.ds(s, t, stride=TOKEN_TILE_ROWS), :] = val[:, s * LANES:(s + 1) * LANES]


def _load_token_tiles(ref, t):
    return jnp.concatenate([ref[pl.ds(s, t, stride=TOKEN_TILE_ROWS), :] for s in range(TOKEN_TILE_ROWS)], axis=1)


def _rows8(vals):
    t = vals[0].shape[1]
    rid = lax.broadcasted_iota(I32, (8, t), 0)
    out = jnp.zeros((8, t), vals[0].dtype)
    for k, v in enumerate(vals):
        out = jnp.where(rid == k, jnp.broadcast_to(v, (8, t)), out)
    return out


def _merge_body(hm_ref, o1_ref, o2_ref, o3_ref, l1_ref, l2_ref, l3_ref, sgm_ref, sga_ref, x_ref,
                wm_ref, wa_ref, wo_ref, g2_ref, wrh_ref, br_ref,
                x2_ref, xn_ref, loc_ref, gate_ref, tcnt_ref, tcar_ref, cnt_ref, carry_ref, st_ref):
    @pl.when(pl.program_id(0) == 0)
    def _():
        carry_ref[...] = jnp.zeros_like(carry_ref)

    m_branch = jnp.dot(hm_ref[...], wm_ref[...], preferred_element_type=F32)
    dils = [d for _, d in DILATED_PATTERNS]
    l1, l2, l3 = [_merge_residues(r, d, st_ref) for r, d in zip((l1_ref, l2_ref, l3_ref), dils)]
    lmax = jnp.maximum(jnp.maximum(l1, l2), l3)
    e1, e2, e3 = jnp.exp(l1 - lmax), jnp.exp(l2 - lmax), jnp.exp(l3 - lmax)
    num = e1 * _merge_residues(o1_ref, dils[0], st_ref)
    num = num + e2 * _merge_residues(o2_ref, dils[1], st_ref)
    num = num + e3 * _merge_residues(o3_ref, dils[2], st_ref)
    h_a = num / (e1 + e2 + e3)
    y = (sgm_ref[...].astype(F32) * m_branch
         + sga_ref[...].astype(F32) * jnp.dot(h_a.astype(BF16), wa_ref[...], preferred_element_type=F32))
    x2 = x_ref[...] + jnp.dot(y.astype(BF16), wo_ref[...], preferred_element_type=F32)
    x2_ref[...] = x2
    xn = x2 * lax.rsqrt(jnp.mean(x2 * x2, axis=-1, keepdims=True) + EPS) * g2_ref[...]
    xh = xn.astype(BF16)
    xn_ref[...] = xh

    logits = lax.dot_general(wrh_ref[...], xh, _NT, preferred_element_type=F32) + br_ref[...]
    t = logits.shape[1]
    eid = lax.broadcasted_iota(I32, (N_EXPERTS, t), 0).astype(F32)
    vals = logits
    top_v, top_i = [], []
    for _ in range(TOP_K):
        mx = jnp.max(vals, axis=0, keepdims=True)
        ik = jnp.min(jnp.where(vals == mx, eid, float(N_EXPERTS)), axis=0, keepdims=True)
        top_v.append(mx)
        top_i.append(ik)
        vals = jnp.where(eid == ik, -jnp.inf, vals)
    ex = [jnp.exp(v - top_v[0]) for v in top_v]
    den = ex[0] + ex[1] + ex[2] + ex[3]
    gate_ref[...] = _rows8([e / den for e in ex])

    chosen = jnp.zeros((N_EXPERTS, t), F32)
    for ik in top_i:
        chosen = chosen + (eid == ik).astype(F32)
    before = (lax.broadcasted_iota(I32, (t, t), 0) < lax.broadcasted_iota(I32, (t, t), 1)).astype(BF16)
    prefix = jnp.dot(chosen.astype(BF16), before, preferred_element_type=F32)
    tcount = jnp.broadcast_to(jnp.sum(chosen, axis=1, keepdims=True), (N_EXPERTS, LANES))
    below = (lax.broadcasted_iota(I32, (N_EXPERTS, N_EXPERTS), 1)
             < lax.broadcasted_iota(I32, (N_EXPERTS, N_EXPERTS), 0)).astype(BF16)
    t_hi = jnp.floor(tcount * (1.0 / BF16_EXACT_INT)) * BF16_EXACT_INT
    tile_off = (jnp.dot(below, t_hi.astype(BF16), preferred_element_type=F32)
                + jnp.dot(below, (tcount - t_hi).astype(BF16), preferred_element_type=F32))
    pos = prefix + tile_off[:, 0:1]
    loc_ref[...] = _rows8([jnp.sum(jnp.where(eid == ik, pos, 0.0), axis=0, keepdims=True).astype(I32)
                           for ik in top_i])
    carry = carry_ref[...]
    tcnt_ref[...] = tcount.astype(I32)
    tcar_ref[...] = carry.astype(I32)
    total = carry + tcount
    carry_ref[...] = total
    cnt_ref[...] = total


def _stage_merge(h_m, attn, sgm, sga, x2d, w_mb, w_ab, w_out, norm2_g, w_router, b_router, batch, seq, tm):
    n = x2d.shape[0]
    steps = seq // tm
    (o1, l1), (o2, l2), (o3, l3) = attn
    wm = w_mb.astype(BF16)
    wa = w_ab.astype(BF16)
    wo = w_out.astype(BF16)
    g2 = norm2_g.astype(F32).reshape(1, D_MODEL)
    wrh = w_router.astype(BF16).T
    br = b_router.astype(F32).reshape(N_EXPERTS, 1)
    row = lambda w: pl.BlockSpec((tm, w), lambda i: (i, 0))
    rowT = lambda r: pl.BlockSpec((r, tm), lambda i: (0, i))
    full = lambda a: pl.BlockSpec(a.shape, lambda i: (0,) * a.ndim)
    res = lambda d: pl.BlockSpec((1, d, tm // d, A_WIDTH), lambda i: (i // steps, 0, i % steps, 0))
    dils = [d for _, d in DILATED_PATTERNS]
    per_tile = pl.BlockSpec((N_EXPERTS, LANES), lambda i: (0, i))
    return pl.pallas_call(
        _merge_body,
        grid=(n // tm,),
        in_specs=[row(M_WIDTH), *[res(d) for d in dils], *[res(d) for d in dils],
                  row(D_MODEL), row(D_MODEL), row(D_MODEL),
                  full(wm), full(wa), full(wo), full(g2), full(wrh), full(br)],
        out_specs=(row(D_MODEL), row(D_MODEL), rowT(8), rowT(8), per_tile, per_tile,
                   pl.BlockSpec((N_EXPERTS, LANES), lambda i: (0, 0))),
        out_shape=(jax.ShapeDtypeStruct((n, D_MODEL), F32),
                   jax.ShapeDtypeStruct((n, D_MODEL), BF16),
                   jax.ShapeDtypeStruct((8, n), I32),
                   jax.ShapeDtypeStruct((8, n), F32),
                   jax.ShapeDtypeStruct((N_EXPERTS, (n // tm) * LANES), I32),
                   jax.ShapeDtypeStruct((N_EXPERTS, (n // tm) * LANES), I32),
                   jax.ShapeDtypeStruct((N_EXPERTS, LANES), F32)),
        scratch_shapes=[pltpu.VMEM((N_EXPERTS, LANES), F32), pltpu.VMEM((2, 2, tm, LANES), F32)],
        compiler_params=_params("arbitrary"),
        name="merge_route",
    )(h_m, o1, o2, o3, l1, l2, l3, sgm, sga, x2d, wm, wa, wo, g2, wrh, br)


def _offsets_body(cnt_ref, blk_ref, pstart_ref, zlo_ref, zhi_ref, *, nblk_pad):
    cnt = cnt_ref[...]
    padded = jnp.floor((cnt + (MOE_BLOCK - 1)) * (1.0 / MOE_BLOCK)) * MOE_BLOCK
    lower = (lax.broadcasted_iota(I32, (N_EXPERTS, N_EXPERTS), 1)
             <= lax.broadcasted_iota(I32, (N_EXPERTS, N_EXPERTS), 0)).astype(BF16)
    nb = padded * (1.0 / MOE_BLOCK)
    nb_hi = jnp.floor(nb * (1.0 / BF16_EXACT_INT)) * BF16_EXACT_INT
    pends = (jnp.dot(lower, nb_hi.astype(BF16), preferred_element_type=F32)
             + jnp.dot(lower, (nb - nb_hi).astype(BF16), preferred_element_type=F32)) * MOE_BLOCK
    pstart = pends - padded
    pstart_ref[...] = pstart.astype(I32)
    zlo_ref[...] = (pstart + cnt).astype(I32)
    zhi_ref[...] = pends.astype(I32)

    first_row = (lax.broadcasted_iota(I32, (N_EXPERTS, nblk_pad), 1) * MOE_BLOCK).astype(F32)
    pe = jnp.broadcast_to(pends[:, 0:1], (N_EXPERTS, nblk_pad))
    be = jnp.sum((pe <= first_row).astype(F32), axis=0, keepdims=True)
    be = jnp.minimum(be, float(N_EXPERTS - 1))
    nused = pends[N_EXPERTS - 1:N_EXPERTS, 0:1] * (1.0 / MOE_BLOCK)
    nonempty = jnp.broadcast_to(padded[:, 0:1], (N_EXPERTS, nblk_pad)) > 0.0
    runidx = jnp.sum(jnp.logical_and(pe <= first_row, nonempty).astype(F32), axis=0, keepdims=True)
    parity = runidx - 2.0 * jnp.floor(runidx * 0.5)
    eid = lax.broadcasted_iota(I32, (N_EXPERTS, nblk_pad), 0).astype(F32)
    later = jnp.logical_and(eid > be, nonempty)
    nxt = jnp.min(jnp.where(later, eid, float(N_EXPERTS)), axis=0, keepdims=True)
    blk_ref[...] = _rows8([be.astype(I32), jnp.broadcast_to(nused, (1, nblk_pad)).astype(I32),
                           parity.astype(I32), nxt.astype(I32)])


def _stage_offsets(cnt, nblk):
    nblk_pad = -(-nblk // LANES) * LANES
    const = lambda r, c: pl.BlockSpec((r, c), lambda i: (0, 0))
    per_expert = jax.ShapeDtypeStruct((N_EXPERTS, LANES), I32)
    return pl.pallas_call(
        functools.partial(_offsets_body, nblk_pad=nblk_pad),
        grid=(1,),
        in_specs=[const(N_EXPERTS, LANES)],
        out_specs=(const(8, nblk_pad), const(N_EXPERTS, LANES), const(N_EXPERTS, LANES), const(N_EXPERTS, LANES)),
        out_shape=(jax.ShapeDtypeStruct((8, nblk_pad), I32), per_expert, per_expert, per_expert),
        compiler_params=_params("arbitrary"),
        name="route_offsets",
    )(cnt)


RUN_BITS = 10


def _tile_rows(ref, first_row, nrows):
    start = first_row * TOKEN_TILE_ROWS
    if not isinstance(first_row, int):
        start = pl.multiple_of(start, TOKEN_TILE_ROWS)
    return ref.at[pl.ds(start, nrows * TOKEN_TILE_ROWS)]


def _for_each_piece(length, fn):
    for b in reversed(range(RUN_BITS)):
        @pl.when(((length >> b) & 1) == 1)
        def _(b=b):
            fn((length >> (b + 1)) << (b + 1), 1 << b)


def _for_each_run(tile, tcnt_ref, tcar_ref, pstart_ref, fn):
    def per_expert(e, local):
        count = tcnt_ref[tile, e]
        first = pstart_ref[e] + tcar_ref[tile, e]
        _for_each_piece(count, lambda off, size: fn(local + off, first + off, size))
        return local + count

    lax.fori_loop(0, N_EXPERTS, per_expert, 0)


PERM_CHUNK = 256


def _dispatch_body(tcnt_ref, tcar_ref, pstart_ref, zlo_ref, zhi_ref, loc_ref, xn_ref, xs_hbm,
                   buf_ref, sems, *, tm):
    step = pl.program_id(0)
    nloc_tiles = TOP_K * tm * TOKEN_TILE_ROWS

    def wait_buffer(slot):
        pltpu.make_async_copy(buf_ref.at[slot], xs_hbm.at[pl.ds(0, nloc_tiles)], sems.at[slot]).wait()

    for slot in range(2):
        tile = 2 * step + slot

        @pl.when(step > 0)
        def _(slot=slot):
            wait_buffer(slot)

        loc = loc_ref[:, slot * tm:(slot + 1) * tm]
        xn = xn_ref[slot * tm:(slot + 1) * tm, :]
        for c in range(TOP_K * tm // PERM_CHUNK):
            lid = lax.broadcasted_iota(I32, (PERM_CHUNK, tm), 0) + c * PERM_CHUNK
            hit = lid == loc[0:1, :]
            for k in range(1, TOP_K):
                hit = jnp.logical_or(hit, lid == loc[k:k + 1, :])
            rows = jnp.dot(jnp.where(hit, 1.0, 0.0).astype(BF16), xn, preferred_element_type=F32)
            _store_token_tiles(
                buf_ref.at[slot, pl.ds(c * PERM_CHUNK * TOKEN_TILE_ROWS, PERM_CHUNK * TOKEN_TILE_ROWS)], rows)

        def run_copy(local, first, size, slot=slot):
            return pltpu.make_async_copy(_tile_rows(buf_ref.at[slot], local, size),
                                         _tile_rows(xs_hbm, first, size), sems.at[slot])

        _for_each_run(tile, tcnt_ref, tcar_ref, pstart_ref, lambda l, f, s: run_copy(l, f, s).start())

    @pl.when(step == pl.num_programs(0) - 1)
    def _():
        wait_buffer(0)
        wait_buffer(1)
        zsrc = buf_ref.at[0]
        zsrc[pl.ds(0, MOE_BLOCK * TOKEN_TILE_ROWS), :] = jnp.zeros((MOE_BLOCK * TOKEN_TILE_ROWS, LANES), F32)

        def zero_copy(first, size):
            return pltpu.make_async_copy(_tile_rows(zsrc, 0, size), _tile_rows(xs_hbm, first, size), sems.at[0])

        def per_expert(e, carry):
            lo = zlo_ref[e]
            npad = zhi_ref[e] - lo
            _for_each_piece(npad, lambda off, size: zero_copy(lo + off, size).start())
            _for_each_piece(npad, lambda off, size: zero_copy(lo + off, size).wait())
            return carry

        lax.fori_loop(0, N_EXPERTS, per_expert, 0)

        first_unused = zhi_ref[N_EXPERTS - 1] // MOE_BLOCK
        nblk = xs_hbm.shape[0] // (MOE_BLOCK * TOKEN_TILE_ROWS)

        def tail(blk, carry):
            zero_copy(blk * MOE_BLOCK, MOE_BLOCK).start()
            zero_copy(blk * MOE_BLOCK, MOE_BLOCK).wait()
            return carry

        lax.fori_loop(first_unused, nblk, tail, 0)


def _stage_dispatch(tables, loc8, xn, nrows, tm):
    n = xn.shape[0]
    assert TOP_K * tm >= MOE_BLOCK and (n // tm) % 2 == 0
    grid_spec = pltpu.PrefetchScalarGridSpec(
        num_scalar_prefetch=5,
        grid=(n // (2 * tm),),
        in_specs=[pl.BlockSpec((8, 2 * tm), lambda i, *_: (0, i)),
                  pl.BlockSpec((2 * tm, D_MODEL), lambda i, *_: (i, 0))],
        out_specs=pl.BlockSpec(memory_space=pl.ANY),
        scratch_shapes=[pltpu.VMEM((2, TOP_K * tm * TOKEN_TILE_ROWS, LANES), F32), pltpu.SemaphoreType.DMA((2,))],
    )
    return pl.pallas_call(
        functools.partial(_dispatch_body, tm=tm),
        grid_spec=grid_spec,
        out_shape=jax.ShapeDtypeStruct((nrows * TOKEN_TILE_ROWS, LANES), F32),
        compiler_params=_params("arbitrary"),
        name="dispatch",
    )(*tables, loc8, xn)


EXPERT_BLOCKS_PER_STEP = 2


def _expert_body(be_ref, nu_ref, par_ref, nxt_ref, xs_ref, w1_hbm, b1_ref, w2_hbm, b2_ref, ys_ref,
                 w1f_ref, w2f_ref, w1b_ref, w2b_ref, sems):
    block_tiles = MOE_BLOCK * TOKEN_TILE_ROWS

    def fetch(expert, slot):
        return (pltpu.make_async_copy(w1_hbm.at[expert], w1f_ref.at[slot], sems.at[slot, 0]),
                pltpu.make_async_copy(w2_hbm.at[expert], w2f_ref.at[slot], sems.at[slot, 1]))

    for sub in range(EXPERT_BLOCKS_PER_STEP):
        j = pl.program_id(0) * EXPERT_BLOCKS_PER_STEP + sub
        used = j < nu_ref[0]
        jj = jnp.maximum(jnp.minimum(j, nu_ref[0] - 1), 0)
        e = be_ref[jj]
        fresh = jnp.logical_or(j == 0, e != be_ref[jnp.maximum(jj - 1, 0)])
        xs_blk = xs_ref.at[pl.ds(sub * block_tiles, block_tiles)]
        ys_blk = ys_ref.at[pl.ds(sub * block_tiles, block_tiles)]

        @pl.when(jnp.logical_and(used, fresh))
        def _(j=j, jj=jj, e=e):
            slot = par_ref[jj]

            @pl.when(j == 0)
            def _():
                for c in fetch(e, slot):
                    c.start()

            for c in fetch(e, slot):
                c.wait()
            nxt = nxt_ref[jj]

            @pl.when(nxt < N_EXPERTS)
            def _():
                for c in fetch(nxt, 1 - slot):
                    c.start()

            w1b_ref[...] = w1f_ref[slot].astype(BF16)
            w2b_ref[...] = w2f_ref[slot].astype(BF16)

        @pl.when(used)
        def _(e=e, xs_blk=xs_blk, ys_blk=ys_blk):
            xb = _load_token_tiles(xs_blk, MOE_BLOCK).astype(BF16)
            gu = jnp.dot(xb, w1b_ref[...], preferred_element_type=F32) + b1_ref[pl.ds(e, 1), :]
            gate = jnp.minimum(gu[:, :D_FF], SWIGLU_LIMIT)
            lin = jnp.clip(gu[:, D_FF:], -SWIGLU_LIMIT, SWIGLU_LIMIT)
            act = (lin + 1.0) * (gate * jax.nn.sigmoid(SWIGLU_ALPHA * gate))
            ys = jnp.dot(act.astype(BF16), w2b_ref[...], preferred_element_type=F32) + b2_ref[pl.ds(e, 1), :]
            _store_token_tiles(ys_blk, ys)

        @pl.when(jnp.logical_not(used))
        def _(ys_blk=ys_blk):
            ys_blk[...] = jnp.zeros(ys_blk.shape, F32)


def _stage_experts(blk8, xs, w1, b1, w2, b2):
    nrows = xs.shape[0] // TOKEN_TILE_ROWS
    nblk = nrows // MOE_BLOCK
    assert nblk % EXPERT_BLOCKS_PER_STEP == 0
    block_e, nused, parity, nxt = blk8[0, :nblk], blk8[1, :1], blk8[2, :nblk], blk8[3, :nblk]
    tiles = (EXPERT_BLOCKS_PER_STEP * MOE_BLOCK * TOKEN_TILE_ROWS, LANES)
    full = lambda a: pl.BlockSpec(a.shape, lambda j, *_: (0,) * a.ndim)
    grid_spec = pltpu.PrefetchScalarGridSpec(
        num_scalar_prefetch=4,
        grid=(nblk // EXPERT_BLOCKS_PER_STEP,),
        in_specs=[pl.BlockSpec(tiles, lambda j, *_: (j, 0)),
                  pl.BlockSpec(memory_space=pl.ANY), full(b1),
                  pl.BlockSpec(memory_space=pl.ANY), full(b2)],
        out_specs=pl.BlockSpec(tiles, lambda j, *_: (j, 0)),
        scratch_shapes=[pltpu.VMEM((2, D_MODEL, 2 * D_FF), F32), pltpu.VMEM((2, D_FF, D_MODEL), F32),
                        pltpu.VMEM((D_MODEL, 2 * D_FF), BF16), pltpu.VMEM((D_FF, D_MODEL), BF16),
                        pltpu.SemaphoreType.DMA((2, 2))],
    )
    return pl.pallas_call(
        _expert_body,
        grid_spec=grid_spec,
        out_shape=jax.ShapeDtypeStruct((nrows * TOKEN_TILE_ROWS, LANES), F32),
        compiler_params=_params("arbitrary"),
        name="experts",
    )(block_e, nused, parity, nxt, xs, w1, b1, w2, b2)


def _combine_body(tcnt_ref, tcar_ref, pstart_ref, loc_ref, gate_ref, x2_ref, ys_hbm, out_ref,
                  buf_ref, g_ref, sems, *, tm):
    step = pl.program_id(0)
    nloc = TOP_K * tm

    def start_runs(tile, slot):
        def run_copy(local, first, size):
            return pltpu.make_async_copy(_tile_rows(ys_hbm, first, size),
                                         _tile_rows(buf_ref.at[slot], local, size), sems.at[slot])
        _for_each_run(tile, tcnt_ref, tcar_ref, pstart_ref, lambda l, f, s: run_copy(l, f, s).start())

    def wait_buffer(slot):
        pltpu.make_async_copy(ys_hbm.at[pl.ds(0, nloc * TOKEN_TILE_ROWS)], buf_ref.at[slot], sems.at[slot]).wait()

    def combine(slot):
        zpad = jnp.zeros((LANES - 16, LANES), F32)
        lane = lax.broadcasted_iota(I32, (LANES, nloc), 1).astype(F32)
        for c in range(tm // LANES):
            cols_in = slice(slot * tm + c * LANES, slot * tm + (c + 1) * LANES)
            cols = jnp.transpose(jnp.concatenate([loc_ref[:, cols_in].astype(F32), gate_ref[:, cols_in], zpad], axis=0))
            g = jnp.zeros((LANES, nloc), F32)
            for k in range(TOP_K):
                g = jnp.where(lane == cols[:, k:k + 1], cols[:, 8 + k:9 + k], g)
            g_ref[c * LANES:(c + 1) * LANES, :] = g.astype(BF16)
        wait_buffer(slot)
        ys = _load_token_tiles(buf_ref.at[slot], nloc).astype(BF16)
        rows = slice(slot * tm, (slot + 1) * tm)
        out_ref[rows, :] = x2_ref[rows, :] + jnp.dot(g_ref[...], ys, preferred_element_type=F32)

    @pl.when(step == 0)
    def _():
        start_runs(0, 0)

    start_runs(2 * step + 1, 1)
    combine(0)

    @pl.when(step + 1 < pl.num_programs(0))
    def _():
        start_runs(2 * step + 2, 0)

    combine(1)


def _stage_combine(tables, loc8, gate8, x2, ys, tm):
    n = x2.shape[0]
    assert (n // tm) % 2 == 0
    grid_spec = pltpu.PrefetchScalarGridSpec(
        num_scalar_prefetch=3,
        grid=(n // (2 * tm),),
        in_specs=[pl.BlockSpec((8, 2 * tm), lambda i, *_: (0, i)),
                  pl.BlockSpec((8, 2 * tm), lambda i, *_: (0, i)),
                  pl.BlockSpec((2 * tm, D_MODEL), lambda i, *_: (i, 0)),
                  pl.BlockSpec(memory_space=pl.ANY)],
        out_specs=pl.BlockSpec((2 * tm, D_MODEL), lambda i, *_: (i, 0)),
        scratch_shapes=[pltpu.VMEM((2, TOP_K * tm * TOKEN_TILE_ROWS, LANES), F32),
                        pltpu.VMEM((tm, TOP_K * tm), BF16),
                        pltpu.SemaphoreType.DMA((2,))],
    )
    return pl.pallas_call(
        functools.partial(_combine_body, tm=tm),
        grid_spec=grid_spec,
        out_shape=jax.ShapeDtypeStruct((n, D_MODEL), F32),
        compiler_params=_params("arbitrary"),
        name="combine",
    )(*tables, loc8, gate8, x2, ys)


def _moe(x2, xn, loc8, gate8, tcnt, tcar, cnt, w1, b1, w2, b2, tm):
    n = x2.shape[0]
    ntile = n // tm
    nblk = -(-(n * TOP_K) // MOE_BLOCK) + N_EXPERTS
    blk8, pstart, zlo, zhi = _stage_offsets(cnt, nblk)
    per_tile = lambda a: a.reshape(N_EXPERTS, ntile, LANES)[:, :, 0].T
    tables = (per_tile(tcnt), per_tile(tcar), pstart[:, 0])
    xs = _stage_dispatch(tables + (zlo[:, 0], zhi[:, 0]), loc8, xn, nblk * MOE_BLOCK, tm)
    ys = _stage_experts(blk8, xs, w1, b1, w2, b2)
    return _stage_combine(tables, loc8, gate8, x2, ys, tm)


def kernel(x, norm1_g, w_in, mlstm_gate_b, mlstm_norm_g, attn_q_norm_g, attn_k_norm_g, w_mlstm_branch,
           w_attn_branch, w_out, norm2_g, w_router, b_router, w1, b1, w2, b2):
    batch, seq, _ = x.shape
    n = batch * seq
    for l in range(norm1_g.shape[0]):
        x2d = x.reshape(n, D_MODEL)
        tm = min(512, seq)
        mq, kT, mv, so, gi, gf, aq, ak, av, sgm, sga = _stage_inproj(
            x2d, norm1_g[l], w_in[l], mlstm_gate_b[l], attn_q_norm_g[l], attn_k_norm_g[l], batch, seq, tm)
        h_m = _stage_mlstm(mq, kT, mv, so, gi, gf, mlstm_norm_g[l], batch, seq, tm)
        attn = [_stage_attn(aq[g], ak[g], av[g], batch, seq, g)
                for g in range(N_GROUPS)]
        x2, xn, loc8, gate8, tcnt, tcar, cnt = _stage_merge(
            h_m, attn, sgm, sga, x2d, w_mlstm_branch[l], w_attn_branch[l], w_out[l], norm2_g[l],
            w_router[l], b_router[l], batch, seq, tm)
        out = _moe(x2, xn, loc8, gate8, tcnt, tcar, cnt, w1[l], b1[l], w2[l], b2[l], tm)
        x = out.reshape(batch, seq, D_MODEL)
    return x
```

```python
import functools

import numpy as np
import jax
import jax.numpy as jnp
from jax import lax
from jax.experimental import pallas as pl
from jax.experimental.pallas import tpu as pltpu

F32 = jnp.float32
BF16 = jnp.bfloat16
I32 = jnp.int32

D_MODEL = 1024
M_HEADS = 4
M_QK_DIM = 64
M_V_DIM = 128
GATE_SOFTCAP = 15.0
A_HEADS = 4
A_HEAD_DIM = 64
DILATED_PATTERNS = ((128, 1), (512, 4), (2048, 16))
N_GROUPS = len(DILATED_PATTERNS)
N_BACK = 128
N_EXPERTS = 32
TOP_K = 4
D_FF = 1024
SWIGLU_LIMIT = 7.0
SWIGLU_ALPHA = 1.702
MOE_BLOCK = 512
EPS = 1e-6

M_WIDTH = M_HEADS * M_V_DIM
M_QK_WIDTH = M_HEADS * M_QK_DIM
A_WIDTH = A_HEADS * A_HEAD_DIM
IN_SPLITS = (M_QK_WIDTH, M_QK_WIDTH, M_WIDTH, M_WIDTH, 2 * M_HEADS,
             N_GROUPS * A_WIDTH, N_GROUPS * A_WIDTH, N_GROUPS * A_WIDTH, D_MODEL, D_MODEL)

LANES = 128
VMEM_LIMIT = 56 * 1024 * 1024

BF16_EXACT_INT = 256.0

_NT = (((1,), (1,)), ((), ()))
_TN = (((0,), (1,)), ((), ()))


def _alibi_slopes():
    n = N_GROUPS * A_HEADS
    s = np.exp2(-8.0 * np.arange(1, n + 1) / n).astype(np.float32)
    return s.reshape(N_GROUPS, A_HEADS)


def _params(*sem):
    return pltpu.CompilerParams(dimension_semantics=sem, vmem_limit_bytes=VMEM_LIMIT)


def _log_sigmoid(x):
    return jnp.minimum(x, 0.0) - jnp.log1p(jnp.exp(-jnp.abs(x)))


_A_WIDTH = sum(IN_SPLITS[:4])
_B_START = _A_WIDTH + IN_SPLITS[4]


def _piece_segments(widths):
    bounds, start = [], 0
    for width in widths:
        bounds.append((start, start + width))
        start += width
    return bounds


_C_MQ, _C_MK, _C_MV, _C_MO = _piece_segments(IN_SPLITS[:4])
_C_AQ, _C_AK, _C_AV, _C_GM, _C_GA = _piece_segments(IN_SPLITS[5:])


def _split_residues(val, d, out_ref, st_ref):
    t = val.shape[0]
    if d == 1:
        out_ref[0, 0] = val.astype(out_ref.dtype)
        return
    st_ref[0] = val[:, :LANES]
    st_ref[1] = val[:, LANES:]
    for r in range(d):
        piece = jnp.concatenate([st_ref[0, pl.ds(r, t // d, stride=d), :],
                                 st_ref[1, pl.ds(r, t // d, stride=d), :]], axis=1)
        out_ref[0, r] = piece.astype(out_ref.dtype)


def _merge_residues(ref, d, st_ref):
    if d == 1:
        return ref[0, 0].astype(F32)
    m = ref.shape[2]
    if d == 16:
        a_ref, b_ref = st_ref.at[0], st_ref.at[1]
        for r0 in range(4):
            for r1 in range(4):
                blk = ref[0, 4 * r1 + r0].astype(F32)
                a_ref[0, pl.ds(r0 * 4 * m + r1, m, stride=4), :] = blk[:, :LANES]
                a_ref[1, pl.ds(r0 * 4 * m + r1, m, stride=4), :] = blk[:, LANES:]
        for r0 in range(4):
            b_ref[0, pl.ds(r0, 4 * m, stride=4), :] = a_ref[0, r0 * 4 * m:(r0 + 1) * 4 * m, :]
            b_ref[1, pl.ds(r0, 4 * m, stride=4), :] = a_ref[1, r0 * 4 * m:(r0 + 1) * 4 * m, :]
        return jnp.concatenate([b_ref[0], b_ref[1]], axis=1)
    a_ref = st_ref.at[0]
    for r in range(d):
        blk = ref[0, r].astype(F32)
        a_ref[0, pl.ds(r, m, stride=d), :] = blk[:, :LANES]
        a_ref[1, pl.ds(r, m, stride=d), :] = blk[:, LANES:]
    return jnp.concatenate([a_ref[0], a_ref[1]], axis=1)


def _inproj_body(x_ref, g1_ref, wa_ref, wg_ref, wb_ref, gb_ref, gq_ref, gk_ref,
                 mq_ref, kT_ref, mv_ref, so_ref, gi_ref, gf_ref,
                 q0_ref, q1_ref, q2_ref, k0_ref, k1_ref, k2_ref, v0_ref, v1_ref, v2_ref,
                 sgm_ref, sga_ref, st_ref):
    x = x_ref[...]
    h = x * lax.rsqrt(jnp.mean(x * x, axis=-1, keepdims=True) + EPS) * g1_ref[...]
    hb = h.astype(BF16)

    def seg(w_ref, c):
        return jnp.dot(hb, w_ref[:, c[0]:c[1]], preferred_element_type=F32)

    mq_ref[...] = seg(wa_ref, _C_MQ).astype(BF16)
    mv_ref[...] = seg(wa_ref, _C_MV).astype(BF16)
    so_ref[...] = jax.nn.sigmoid(seg(wa_ref, _C_MO)).astype(BF16)
    hid_r = lax.broadcasted_iota(I32, (A_WIDTH, A_WIDTH), 0) // A_HEAD_DIM
    hid_c = lax.broadcasted_iota(I32, (A_WIDTH, A_WIDTH), 1) // A_HEAD_DIM
    head_ones = (hid_r == hid_c).astype(BF16)
    for c, refs, gain_ref in ((_C_AQ, (q0_ref, q1_ref, q2_ref), gq_ref), (_C_AK, (k0_ref, k1_ref, k2_ref), gk_ref),
                              (_C_AV, (v0_ref, v1_ref, v2_ref), None)):
        val = seg(wb_ref, c)
        for g, ref in enumerate(refs):
            piece = val[:, g * A_WIDTH:(g + 1) * A_WIDTH]
            if gain_ref is not None:
                ss = jnp.dot((piece * piece).astype(BF16), head_ones, preferred_element_type=F32)
                piece = piece * lax.rsqrt(ss * (1.0 / A_HEAD_DIM) + EPS) * gain_ref[:, g * A_WIDTH:(g + 1) * A_WIDTH]
            _split_residues(piece, DILATED_PATTERNS[g][1], ref, st_ref)
    sgm_ref[...] = jax.nn.sigmoid(seg(wb_ref, _C_GM)).astype(BF16)
    sga_ref[...] = jax.nn.sigmoid(seg(wb_ref, _C_GA)).astype(BF16)

    kT_ref[...] = lax.dot_general(wa_ref[:, _C_MK[0]:_C_MK[1]], hb, _TN, preferred_element_type=F32).astype(BF16)
    zg = lax.dot_general(wg_ref[...], hb, _TN, preferred_element_type=F32)
    zi = zg[0:8] + gb_ref[0:8]
    zf = zg[M_HEADS:M_HEADS + 8] + gb_ref[8:16]
    gi_ref[...] = GATE_SOFTCAP * jnp.tanh(zi / GATE_SOFTCAP)
    gf_ref[...] = _log_sigmoid(GATE_SOFTCAP * jnp.tanh(zf / GATE_SOFTCAP))


def _stage_inproj(x2d, norm1_g, w_in, gate_b, gq, gk, batch, seq, tm):
    n = x2d.shape[0]
    steps = seq // tm
    wa = w_in[:, :_A_WIDTH].astype(BF16)
    wg = jnp.pad(w_in[:, _A_WIDTH:_B_START], ((0, 0), (0, LANES - IN_SPLITS[4]))).astype(BF16)
    wb = w_in[:, _B_START:].astype(BF16)
    gb = jnp.zeros((16, 1), F32)
    gb = gb.at[0:4, 0].set(gate_b[:M_HEADS].astype(F32)).at[8:12, 0].set(gate_b[M_HEADS:].astype(F32))
    g1 = norm1_g.astype(F32).reshape(1, D_MODEL)
    gq_t = (jnp.tile(gq.astype(F32), (1, A_HEADS)) * (A_HEAD_DIM ** -0.5)).reshape(1, N_GROUPS * A_WIDTH)
    gk_t = jnp.tile(gk.astype(F32), (1, A_HEADS)).reshape(1, N_GROUPS * A_WIDTH)

    row = lambda w: pl.BlockSpec((tm, w), lambda i: (i, 0))
    rowT = lambda r: pl.BlockSpec((r, tm), lambda i: (0, i))
    full = lambda a: pl.BlockSpec(a.shape, lambda i: (0,) * a.ndim)
    dils = [d for _, d in DILATED_PATTERNS]
    res_shape = lambda d: jax.ShapeDtypeStruct((batch, d, seq // d, A_WIDTH), BF16)
    res_spec = lambda d: pl.BlockSpec((1, d, tm // d, A_WIDTH), lambda i: (i // steps, 0, i % steps, 0))
    out_shapes = (
        jax.ShapeDtypeStruct((n, M_QK_WIDTH), BF16),
        jax.ShapeDtypeStruct((M_QK_WIDTH, n), BF16),
        jax.ShapeDtypeStruct((n, M_WIDTH), BF16),
        jax.ShapeDtypeStruct((n, M_WIDTH), BF16),
        jax.ShapeDtypeStruct((8, n), F32),
        jax.ShapeDtypeStruct((8, n), F32),
        *[res_shape(d) for d in dils], *[res_shape(d) for d in dils], *[res_shape(d) for d in dils],
        jax.ShapeDtypeStruct((n, D_MODEL), BF16),
        jax.ShapeDtypeStruct((n, D_MODEL), BF16),
    )
    out_specs = (row(M_QK_WIDTH), rowT(M_QK_WIDTH), row(M_WIDTH), row(M_WIDTH), rowT(8), rowT(8),
                 *[res_spec(d) for d in dils], *[res_spec(d) for d in dils], *[res_spec(d) for d in dils],
                 row(D_MODEL), row(D_MODEL))
    outs = pl.pallas_call(
        _inproj_body,
        grid=(n // tm,),
        in_specs=[row(D_MODEL), full(g1), full(wa), full(wg), full(wb), full(gb), full(gq_t), full(gk_t)],
        out_specs=out_specs,
        out_shape=out_shapes,
        scratch_shapes=[pltpu.VMEM((2, tm, LANES), F32)],
        compiler_params=_params("parallel"),
        name="inproj",
    )(x2d, g1, wa, wg, wb, gb, gq_t, gk_t)
    mq, kT, mv, so, gi, gf = outs[:6]
    aq, ak, av = outs[6:9], outs[9:12], outs[12:15]
    return mq, kT, mv, so, gi, gf, aq, ak, av, outs[15], outs[16]


M_CHUNK_LEN = 128


def _mlstm_body(q_ref, v_ref, so_ref, ng_ref, *rest, nchunk, nseq):
    kT_refs, gi_refs, gf_refs = rest[0:nseq], rest[nseq:2 * nseq], rest[2 * nseq:3 * nseq]
    o_ref, c_ref, m_ref = rest[3 * nseq:]
    L = M_CHUNK_LEN

    @pl.when(pl.program_id(0) == 0)
    def _():
        c_ref[...] = jnp.zeros_like(c_ref)
        m_ref[...] = jnp.zeros_like(m_ref)

    lane8 = lax.broadcasted_iota(I32, (8, L), 1)
    causal = lax.broadcasted_iota(I32, (L, L), 1) <= lax.broadcasted_iota(I32, (L, L), 0)
    lo_half = lax.broadcasted_iota(I32, (L, LANES), 1) < M_QK_DIM
    ones = jnp.ones((L, M_V_DIM), BF16)

    heads = range(M_HEADS)
    cstate = [[c_ref[s, h * M_QK_DIM:(h + 1) * M_QK_DIM, :] for h in heads] for s in range(nseq)]
    m_prev = [m_ref[s, :, 0:1] for s in range(nseq)]
    units = []
    for c in range(nchunk):
        rows = slice(c * L, (c + 1) * L)
        for s in range(nseq):
            gi = gi_refs[s][:, rows]
            b = gf_refs[s][:, rows]
            sh = 1
            while sh < L:
                b = b + jnp.where(lane8 >= sh, pltpu.roll(b, sh, 1), 0.0)
                sh *= 2
            u = gi - b
            g = b[:, L - 1:L]
            a = g + u
            amax = jnp.max(a, axis=1, keepdims=True)
            m_new = jnp.maximum(g + m_prev[s], amax)
            w = jnp.exp(a - m_new) * (M_QK_DIM ** -0.5)
            s_old = jnp.exp(g + m_prev[s] - m_new)
            vext = [jnp.concatenate([v_ref[s, rows, h * M_V_DIM:(h + 1) * M_V_DIM], ones], axis=1) for h in heads]
            cloc = []
            for h in heads:
                hr = slice(h * M_QK_DIM, (h + 1) * M_QK_DIM)
                kw = (kT_refs[s][hr, rows].astype(F32) * w[h:h + 1, :]).astype(BF16)
                cloc.append(jnp.dot(kw, vext[h], preferred_element_type=F32))
            units.append(dict(seq=s, rows=rows, b=b, u=u, m_prev=m_prev[s], state=cstate[s], vext=vext))
            cstate[s] = [s_old[h:h + 1, :] * cstate[s][h] + cloc[h] for h in heads]
            m_prev[s] = m_new
    for s in range(nseq):
        for h in heads:
            c_ref[s, h * M_QK_DIM:(h + 1) * M_QK_DIM, :] = cstate[s][h]
        m_ref[s] = jnp.broadcast_to(m_prev[s], (8, LANES))

    for un in units:
        s, rows = un["seq"], un["rows"]
        un["s"], un["qc"] = [], []
        for p in range(M_HEADS // 2):
            lanes_p = slice(p * LANES, (p + 1) * LANES)
            q_pair = q_ref[s, rows, lanes_p]
            kT_pair = kT_refs[s][lanes_p, rows]
            c_pair = jnp.concatenate([un["state"][2 * p], un["state"][2 * p + 1]], axis=0).astype(BF16)
            for hh in range(2):
                qm = jnp.where(lo_half if hh == 0 else jnp.logical_not(lo_half), q_pair, jnp.zeros_like(q_pair))
                un["s"].append(jnp.dot(qm, kT_pair, preferred_element_type=F32) * (M_QK_DIM ** -0.5))
                un["qc"].append(jnp.dot(qm, c_pair, preferred_element_type=F32))

    for un in units:
        s, rows, b, u, mp = un["seq"], un["rows"], un["b"], un["u"], un["m_prev"]
        for h in heads:
            hl = slice(h * M_V_DIM, (h + 1) * M_V_DIM)
            bcol = jnp.transpose(jnp.broadcast_to(b[h:h + 1, :], (L, L)))
            dm = jnp.where(causal, bcol + u[h:h + 1, :], -jnp.inf)
            inter = bcol + mp[h:h + 1, :]
            m_t = jnp.maximum(inter, jnp.max(dm, axis=1, keepdims=True))
            pmat = (un["s"][h] * jnp.exp(dm - m_t)).astype(BF16)
            sc = jnp.exp(inter - m_t)
            out = (jnp.dot(pmat, un["vext"][h], preferred_element_type=F32)
                   + jnp.concatenate([sc, sc], axis=1) * un["qc"][h])
            hv = out[:, :M_V_DIM] / jnp.maximum(jnp.abs(out[:, M_V_DIM:]), jnp.exp(-m_t))
            hn = hv * lax.rsqrt(jnp.mean(hv * hv, axis=1, keepdims=True) + EPS)
            hn = hn * ng_ref[:, hl] * so_ref[s, rows, hl].astype(F32)
            o_ref[s, rows, hl] = hn.astype(BF16)


def _stage_mlstm(mq, kT, mv, so, gi, gf, norm_g, batch, seq, rows_per_step):
    n = batch * seq
    R = rows_per_step
    steps = seq // R
    ng = norm_g.astype(F32).reshape(1, M_WIDTH)
    per_seq = lambda a: a.reshape(batch, seq, a.shape[1])
    row = lambda w: pl.BlockSpec((batch, R, w), lambda i: (0, i, 0))
    colT = lambda r, s: pl.BlockSpec((r, R), lambda i, s=s: (0, s * steps + i))
    seqs = range(batch)
    out = pl.pallas_call(
        functools.partial(_mlstm_body, nchunk=R // M_CHUNK_LEN, nseq=batch),
        grid=(steps,),
        in_specs=[row(M_QK_WIDTH), row(M_WIDTH), row(M_WIDTH), pl.BlockSpec((1, M_WIDTH), lambda i: (0, 0)),
                  *[colT(M_QK_WIDTH, s) for s in seqs], *[colT(8, s) for s in seqs], *[colT(8, s) for s in seqs]],
        out_specs=row(M_WIDTH),
        out_shape=jax.ShapeDtypeStruct((batch, seq, M_WIDTH), BF16),
        scratch_shapes=[pltpu.VMEM((batch, M_QK_WIDTH, 2 * M_V_DIM), F32), pltpu.VMEM((batch, 8, LANES), F32)],
        compiler_params=_params("arbitrary"),
        name="mlstm",
    )(per_seq(mq), per_seq(mv), per_seq(so), ng, *[kT] * batch, *[gi] * batch, *[gf] * batch)
    return out.reshape(n, M_WIDTH)


def _attn_body(q_ref, kp_ref, kc_ref, vp_ref, vc_ref, o_ref, lse_ref, *, dil, slopes, lq):
    QB = N_BACK
    first = pl.program_id(2) == 0
    qn = q_ref[0, 0]
    kcn = kc_ref[0, 0]
    kpn = kp_ref[0, 0]
    vc = vc_ref[0, 0]
    vp = vp_ref[0, 0]

    qi = lax.broadcasted_iota(I32, (QB, 2 * QB), 0)
    kj = lax.broadcasted_iota(I32, (QB, 2 * QB), 1)
    dist = qi + QB - kj
    band = jnp.logical_and(dist >= 0, dist <= N_BACK)
    distf = (dist * dil).astype(F32)
    bias = [jnp.where(band, -float(slopes[h]) * distf, -jnp.inf) for h in range(A_HEADS)]
    no_prev = jnp.logical_and(first, kj < QB)
    lo_half = lax.broadcasted_iota(I32, (QB, LANES), 1) < A_HEAD_DIM
    ones = jnp.ones((2 * QB, LANES), BF16)

    units = []
    for j in range(lq // QB):
        rows = slice(j * QB, (j + 1) * QB)
        prow = slice((j - 1) * QB, j * QB)
        keys = jnp.concatenate([kpn if j == 0 else kcn[prow], kcn[rows]], axis=0)
        vals = jnp.concatenate([vp if j == 0 else vc[prow], vc[rows]], axis=0)
        for p in range(A_HEADS // 2):
            lanes_p = slice(p * LANES, (p + 1) * LANES)
            q_pair = qn[rows, lanes_p]
            k_pair = keys[:, lanes_p]
            vext = jnp.concatenate([vals[:, lanes_p], ones], axis=1)
            scores = []
            for hh in range(2):
                sel = lo_half if hh == 0 else jnp.logical_not(lo_half)
                qm = jnp.where(sel, q_pair, jnp.zeros_like(q_pair))
                scores.append(lax.dot_general(qm, k_pair, _NT, preferred_element_type=F32))
            units.append((j, rows, lanes_p, p, vext, scores))

    for j, rows, lanes_p, p, vext, scores in units:
        o_pair = None
        l_pair = None
        for hh in range(2):
            s = scores[hh] + bias[2 * p + hh]
            if j == 0:
                s = jnp.where(no_prev, -jnp.inf, s)
            m = jnp.max(s, axis=1, keepdims=True)
            pv = jnp.dot(jnp.exp(s - m).astype(BF16), vext, preferred_element_type=F32)
            den = pv[:, LANES:]
            o_h = pv[:, :LANES] / den
            l_h = m + jnp.log(den)
            o_pair = o_h if hh == 0 else jnp.where(lo_half, o_pair, o_h)
            l_pair = l_h if hh == 0 else jnp.where(lo_half, l_pair, l_h)
        o_ref[0, 0, rows, lanes_p] = o_pair.astype(BF16)
        lse_ref[0, 0, rows, lanes_p] = l_pair


def _stage_attn(aq, ak, av, batch, seq, group):
    _, dil = DILATED_PATTERNS[group]
    L = seq // dil
    assert L % N_BACK == 0
    lq = min(1024, L)
    nq = L // lq
    sub = lq // N_BACK
    cur = pl.BlockSpec((1, 1, lq, A_WIDTH), lambda b, r, i: (b, r, i, 0))
    prev = pl.BlockSpec((1, 1, N_BACK, A_WIDTH), lambda b, r, i: (b, r, jnp.maximum(i * sub - 1, 0), 0))
    return pl.pallas_call(
        functools.partial(_attn_body, dil=dil, slopes=tuple(_alibi_slopes()[group]), lq=lq),
        grid=(batch, dil, nq),
        in_specs=[cur, prev, cur, prev, cur],
        out_specs=(cur, cur),
        out_shape=(jax.ShapeDtypeStruct((batch, dil, L, A_WIDTH), BF16),
                   jax.ShapeDtypeStruct((batch, dil, L, A_WIDTH), F32)),
        compiler_params=_params("parallel", "parallel", "parallel"),
        name=f"dilated_attn_d{dil}",
    )(aq, ak, ak, av, av)


TOKEN_TILE_ROWS = D_MODEL // LANES


def _store_token_tiles(ref, val):
    t = val.shape[0]
    for s in range(TOKEN_TILE_ROWS):
        ref[pl.ds(s, t, stride=TOKEN_TILE_ROWS), :] = val[:, s * LANES:(s + 1) * LANES]


def _load_token_tiles(ref, t):
    return jnp.concatenate([ref[pl.ds(s, t, stride=TOKEN_TILE_ROWS), :] for s in range(TOKEN_TILE_ROWS)], axis=1)


def _rows8(vals):
    t = vals[0].shape[1]
    rid = lax.broadcasted_iota(I32, (8, t), 0)
    out = jnp.zeros((8, t), vals[0].dtype)
    for k, v in enumerate(vals):
        out = jnp.where(rid == k, jnp.broadcast_to(v, (8, t)), out)
    return out


def _merge_body(hm_ref, o1_ref, o2_ref, o3_ref, l1_ref, l2_ref, l3_ref, sgm_ref, sga_ref, x_ref,
                wm_ref, wa_ref, wo_ref, g2_ref, wrh_ref, br_ref,
                x2_ref, xn_ref, loc_ref, gate_ref, tcnt_ref, tcar_ref, cnt_ref, carry_ref, st_ref):
    @pl.when(pl.program_id(0) == 0)
    def _():
        carry_ref[...] = jnp.zeros_like(carry_ref)

    m_branch = jnp.dot(hm_ref[...], wm_ref[...], preferred_element_type=F32)
    dils = [d for _, d in DILATED_PATTERNS]
    l1, l2, l3 = [_merge_residues(r, d, st_ref) for r, d in zip((l1_ref, l2_ref, l3_ref), dils)]
    lmax = jnp.maximum(jnp.maximum(l1, l2), l3)
    e1, e2, e3 = jnp.exp(l1 - lmax), jnp.exp(l2 - lmax), jnp.exp(l3 - lmax)
    num = e1 * _merge_residues(o1_ref, dils[0], st_ref)
    num = num + e2 * _merge_residues(o2_ref, dils[1], st_ref)
    num = num + e3 * _merge_residues(o3_ref, dils[2], st_ref)
    h_a = num / (e1 + e2 + e3)
    y = (sgm_ref[...].astype(F32) * m_branch
         + sga_ref[...].astype(F32) * jnp.dot(h_a.astype(BF16), wa_ref[...], preferred_element_type=F32))
    x2 = x_ref[...] + jnp.dot(y.astype(BF16), wo_ref[...], preferred_element_type=F32)
    x2_ref[...] = x2
    xn = x2 * lax.rsqrt(jnp.mean(x2 * x2, axis=-1, keepdims=True) + EPS) * g2_ref[...]
    xh = xn.astype(BF16)
    xn_ref[...] = xh

    logits = lax.dot_general(wrh_ref[...], xh, _NT, preferred_element_type=F32) + br_ref[...]
    t = logits.shape[1]
    eid = lax.broadcasted_iota(I32, (N_EXPERTS, t), 0).astype(F32)
    vals = logits
    top_v, top_i = [], []
    for _ in range(TOP_K):
        mx = jnp.max(vals, axis=0, keepdims=True)
        ik = jnp.min(jnp.where(vals == mx, eid, float(N_EXPERTS)), axis=0, keepdims=True)
        top_v.append(mx)
        top_i.append(ik)
        vals = jnp.where(eid == ik, -jnp.inf, vals)
    ex = [jnp.exp(v - top_v[0]) for v in top_v]
    den = ex[0] + ex[1] + ex[2] + ex[3]
    gate_ref[...] = _rows8([e / den for e in ex])

    chosen = jnp.zeros((N_EXPERTS, t), F32)
    for ik in top_i:
        chosen = chosen + (eid == ik).astype(F32)
    before = (lax.broadcasted_iota(I32, (t, t), 0) < lax.broadcasted_iota(I32, (t, t), 1)).astype(BF16)
    prefix = jnp.dot(chosen.astype(BF16), before, preferred_element_type=F32)
    tcount = jnp.broadcast_to(jnp.sum(chosen, axis=1, keepdims=True), (N_EXPERTS, LANES))
    below = (lax.broadcasted_iota(I32, (N_EXPERTS, N_EXPERTS), 1)
             < lax.broadcasted_iota(I32, (N_EXPERTS, N_EXPERTS), 0)).astype(BF16)
    t_hi = jnp.floor(tcount * (1.0 / BF16_EXACT_INT)) * BF16_EXACT_INT
    tile_off = (jnp.dot(below, t_hi.astype(BF16), preferred_element_type=F32)
                + jnp.dot(below, (tcount - t_hi).astype(BF16), preferred_element_type=F32))
    pos = prefix + tile_off[:, 0:1]
    loc_ref[...] = _rows8([jnp.sum(jnp.where(eid == ik, pos, 0.0), axis=0, keepdims=True).astype(I32)
                           for ik in top_i])
    carry = carry_ref[...]
    tcnt_ref[...] = tcount.astype(I32)
    tcar_ref[...] = carry.astype(I32)
    total = carry + tcount
    carry_ref[...] = total
    cnt_ref[...] = total


def _stage_merge(h_m, attn, sgm, sga, x2d, w_mb, w_ab, w_out, norm2_g, w_router, b_router, batch, seq, tm):
    n = x2d.shape[0]
    steps = seq // tm
    (o1, l1), (o2, l2), (o3, l3) = attn
    wm = w_mb.astype(BF16)
    wa = w_ab.astype(BF16)
    wo = w_out.astype(BF16)
    g2 = norm2_g.astype(F32).reshape(1, D_MODEL)
    wrh = w_router.astype(BF16).T
    br = b_router.astype(F32).reshape(N_EXPERTS, 1)
    row = lambda w: pl.BlockSpec((tm, w), lambda i: (i, 0))
    rowT = lambda r: pl.BlockSpec((r, tm), lambda i: (0, i))
    full = lambda a: pl.BlockSpec(a.shape, lambda i: (0,) * a.ndim)
    res = lambda d: pl.BlockSpec((1, d, tm // d, A_WIDTH), lambda i: (i // steps, 0, i % steps, 0))
    dils = [d for _, d in DILATED_PATTERNS]
    per_tile = pl.BlockSpec((N_EXPERTS, LANES), lambda i: (0, i))
    return pl.pallas_call(
        _merge_body,
        grid=(n // tm,),
        in_specs=[row(M_WIDTH), *[res(d) for d in dils], *[res(d) for d in dils],
                  row(D_MODEL), row(D_MODEL), row(D_MODEL),
                  full(wm), full(wa), full(wo), full(g2), full(wrh), full(br)],
        out_specs=(row(D_MODEL), row(D_MODEL), rowT(8), rowT(8), per_tile, per_tile,
                   pl.BlockSpec((N_EXPERTS, LANES), lambda i: (0, 0))),
        out_shape=(jax.ShapeDtypeStruct((n, D_MODEL), F32),
                   jax.ShapeDtypeStruct((n, D_MODEL), BF16),
                   jax.ShapeDtypeStruct((8, n), I32),
                   jax.ShapeDtypeStruct((8, n), F32),
                   jax.ShapeDtypeStruct((N_EXPERTS, (n // tm) * LANES), I32),
                   jax.ShapeDtypeStruct((N_EXPERTS, (n // tm) * LANES), I32),
                   jax.ShapeDtypeStruct((N_EXPERTS, LANES), F32)),
        scratch_shapes=[pltpu.VMEM((N_EXPERTS, LANES), F32), pltpu.VMEM((2, 2, tm, LANES), F32)],
        compiler_params=_params("arbitrary"),
        name="merge_route",
    )(h_m, o1, o2, o3, l1, l2, l3, sgm, sga, x2d, wm, wa, wo, g2, wrh, br)


def _offsets_body(cnt_ref, blk_ref, pstart_ref, zlo_ref, zhi_ref, *, nblk_pad):
    cnt = cnt_ref[...]
    padded = jnp.floor((cnt + (MOE_BLOCK - 1)) * (1.0 / MOE_BLOCK)) * MOE_BLOCK
    lower = (lax.broadcasted_iota(I32, (N_EXPERTS, N_EXPERTS), 1)
             <= lax.broadcasted_iota(I32, (N_EXPERTS, N_EXPERTS), 0)).astype(BF16)
    nb = padded * (1.0 / MOE_BLOCK)
    nb_hi = jnp.floor(nb * (1.0 / BF16_EXACT_INT)) * BF16_EXACT_INT
    pends = (jnp.dot(lower, nb_hi.astype(BF16), preferred_element_type=F32)
             + jnp.dot(lower, (nb - nb_hi).astype(BF16), preferred_element_type=F32)) * MOE_BLOCK
    pstart = pends - padded
    pstart_ref[...] = pstart.astype(I32)
    zlo_ref[...] = (pstart + cnt).astype(I32)
    zhi_ref[...] = pends.astype(I32)

    first_row = (lax.broadcasted_iota(I32, (N_EXPERTS, nblk_pad), 1) * MOE_BLOCK).astype(F32)
    pe = jnp.broadcast_to(pends[:, 0:1], (N_EXPERTS, nblk_pad))
    be = jnp.sum((pe <= first_row).astype(F32), axis=0, keepdims=True)
    be = jnp.minimum(be, float(N_EXPERTS - 1))
    nused = pends[N_EXPERTS - 1:N_EXPERTS, 0:1] * (1.0 / MOE_BLOCK)
    nonempty = jnp.broadcast_to(padded[:, 0:1], (N_EXPERTS, nblk_pad)) > 0.0
    runidx = jnp.sum(jnp.logical_and(pe <= first_row, nonempty).astype(F32), axis=0, keepdims=True)
    parity = runidx - 2.0 * jnp.floor(runidx * 0.5)
    eid = lax.broadcasted_iota(I32, (N_EXPERTS, nblk_pad), 0).astype(F32)
    later = jnp.logical_and(eid > be, nonempty)
    nxt = jnp.min(jnp.where(later, eid, float(N_EXPERTS)), axis=0, keepdims=True)
    blk_ref[...] = _rows8([be.astype(I32), jnp.broadcast_to(nused, (1, nblk_pad)).astype(I32),
                           parity.astype(I32), nxt.astype(I32)])


def _stage_offsets(cnt, nblk):
    nblk_pad = -(-nblk // LANES) * LANES
    const = lambda r, c: pl.BlockSpec((r, c), lambda i: (0, 0))
    per_expert = jax.ShapeDtypeStruct((N_EXPERTS, LANES), I32)
    return pl.pallas_call(
        functools.partial(_offsets_body, nblk_pad=nblk_pad),
        grid=(1,),
        in_specs=[const(N_EXPERTS, LANES)],
        out_specs=(const(8, nblk_pad), const(N_EXPERTS, LANES), const(N_EXPERTS, LANES), const(N_EXPERTS, LANES)),
        out_shape=(jax.ShapeDtypeStruct((8, nblk_pad), I32), per_expert, per_expert, per_expert),
        compiler_params=_params("arbitrary"),
        name="route_offsets",
    )(cnt)


RUN_BITS = 10
COMMON_RUN_BITS = 7


def _tile_rows(ref, first_row, nrows):
    start = first_row * TOKEN_TILE_ROWS
    if not isinstance(first_row, int):
        start = pl.multiple_of(start, TOKEN_TILE_ROWS)
    return ref.at[pl.ds(start, nrows * TOKEN_TILE_ROWS)]


def _for_each_piece(length, fn):
    def piece(b):
        @pl.when(((length >> b) & 1) == 1)
        def _():
            fn((length >> (b + 1)) << (b + 1), 1 << b)

    @pl.when(length >= (1 << COMMON_RUN_BITS))
    def _():
        for b in reversed(range(COMMON_RUN_BITS, RUN_BITS)):
            piece(b)

    for b in reversed(range(COMMON_RUN_BITS)):
        piece(b)


def _for_each_run(tile, tcnt_ref, tcar_ref, pstart_ref, fn):
    def per_expert(e, local):
        count = tcnt_ref[tile, e]
        first = pstart_ref[e] + tcar_ref[tile, e]
        _for_each_piece(count, lambda off, size: fn(local + off, first + off, size))
        return local + count

    lax.fori_loop(0, N_EXPERTS, per_expert, 0)


PERM_CHUNK = 256


def _dispatch_body(tcnt_ref, tcar_ref, pstart_ref, zlo_ref, zhi_ref, loc_ref, xn_ref, xs_hbm,
                   buf_ref, sems, *, tm):
    step = pl.program_id(0)
    nloc_tiles = TOP_K * tm * TOKEN_TILE_ROWS

    def wait_buffer(slot):
        pltpu.make_async_copy(buf_ref.at[slot], xs_hbm.at[pl.ds(0, nloc_tiles)], sems.at[slot]).wait()

    for slot in range(2):
        tile = 2 * step + slot

        @pl.when(step > 0)
        def _(slot=slot):
            wait_buffer(slot)

        loc = loc_ref[:, slot * tm:(slot + 1) * tm]
        xn = xn_ref[slot * tm:(slot + 1) * tm, :]
        for c in range(TOP_K * tm // PERM_CHUNK):
            lid = lax.broadcasted_iota(I32, (PERM_CHUNK, tm), 0) + c * PERM_CHUNK
            hit = lid == loc[0:1, :]
            for k in range(1, TOP_K):
                hit = jnp.logical_or(hit, lid == loc[k:k + 1, :])
            rows = jnp.dot(jnp.where(hit, 1.0, 0.0).astype(BF16), xn, preferred_element_type=F32)
            _store_token_tiles(
                buf_ref.at[slot, pl.ds(c * PERM_CHUNK * TOKEN_TILE_ROWS, PERM_CHUNK * TOKEN_TILE_ROWS)], rows)

        def run_copy(local, first, size, slot=slot):
            return pltpu.make_async_copy(_tile_rows(buf_ref.at[slot], local, size),
                                         _tile_rows(xs_hbm, first, size), sems.at[slot])

        _for_each_run(tile, tcnt_ref, tcar_ref, pstart_ref, lambda l, f, s: run_copy(l, f, s).start())

    @pl.when(step == pl.num_programs(0) - 1)
    def _():
        wait_buffer(0)
        wait_buffer(1)
        zsrc = buf_ref.at[0]
        zsrc[pl.ds(0, MOE_BLOCK * TOKEN_TILE_ROWS), :] = jnp.zeros((MOE_BLOCK * TOKEN_TILE_ROWS, LANES), F32)

        def zero_copy(first, size):
            return pltpu.make_async_copy(_tile_rows(zsrc, 0, size), _tile_rows(xs_hbm, first, size), sems.at[0])

        def per_expert(e, carry):
            lo = zlo_ref[e]
            npad = zhi_ref[e] - lo
            _for_each_piece(npad, lambda off, size: zero_copy(lo + off, size).start())
            _for_each_piece(npad, lambda off, size: zero_copy(lo + off, size).wait())
            return carry

        lax.fori_loop(0, N_EXPERTS, per_expert, 0)

        first_unused = zhi_ref[N_EXPERTS - 1] // MOE_BLOCK
        nblk = xs_hbm.shape[0] // (MOE_BLOCK * TOKEN_TILE_ROWS)

        def tail(blk, carry):
            zero_copy(blk * MOE_BLOCK, MOE_BLOCK).start()
            zero_copy(blk * MOE_BLOCK, MOE_BLOCK).wait()
            return carry

        lax.fori_loop(first_unused, nblk, tail, 0)


def _stage_dispatch(tables, loc8, xn, nrows, tm):
    n = xn.shape[0]
    assert TOP_K * tm >= MOE_BLOCK and (n // tm) % 2 == 0
    grid_spec = pltpu.PrefetchScalarGridSpec(
        num_scalar_prefetch=5,
        grid=(n // (2 * tm),),
        in_specs=[pl.BlockSpec((8, 2 * tm), lambda i, *_: (0, i)),
                  pl.BlockSpec((2 * tm, D_MODEL), lambda i, *_: (i, 0))],
        out_specs=pl.BlockSpec(memory_space=pl.ANY),
        scratch_shapes=[pltpu.VMEM((2, TOP_K * tm * TOKEN_TILE_ROWS, LANES), F32), pltpu.SemaphoreType.DMA((2,))],
    )
    return pl.pallas_call(
        functools.partial(_dispatch_body, tm=tm),
        grid_spec=grid_spec,
        out_shape=jax.ShapeDtypeStruct((nrows * TOKEN_TILE_ROWS, LANES), F32),
        compiler_params=_params("arbitrary"),
        name="dispatch",
    )(*tables, loc8, xn)


EXPERT_BLOCKS_PER_STEP = 2


def _expert_body(be_ref, nu_ref, par_ref, nxt_ref, xs_ref, w1_hbm, b1_ref, w2_hbm, b2_ref, ys_ref,
                 w1f_ref, w2f_ref, w1b_ref, w2b_ref, sems):
    block_tiles = MOE_BLOCK * TOKEN_TILE_ROWS

    def fetch(expert, slot):
        return (pltpu.make_async_copy(w1_hbm.at[expert], w1f_ref.at[slot], sems.at[slot, 0]),
                pltpu.make_async_copy(w2_hbm.at[expert], w2f_ref.at[slot], sems.at[slot, 1]))

    for sub in range(EXPERT_BLOCKS_PER_STEP):
        j = pl.program_id(0) * EXPERT_BLOCKS_PER_STEP + sub
        used = j < nu_ref[0]
        jj = jnp.maximum(jnp.minimum(j, nu_ref[0] - 1), 0)
        e = be_ref[jj]
        fresh = jnp.logical_or(j == 0, e != be_ref[jnp.maximum(jj - 1, 0)])
        xs_blk = xs_ref.at[pl.ds(sub * block_tiles, block_tiles)]
        ys_blk = ys_ref.at[pl.ds(sub * block_tiles, block_tiles)]

        @pl.when(jnp.logical_and(used, fresh))
        def _(j=j, jj=jj, e=e):
            slot = par_ref[jj]

            @pl.when(j == 0)
            def _():
                for c in fetch(e, slot):
                    c.start()

            for c in fetch(e, slot):
                c.wait()
            nxt = nxt_ref[jj]

            @pl.when(nxt < N_EXPERTS)
            def _():
                for c in fetch(nxt, 1 - slot):
                    c.start()

            w1b_ref[...] = w1f_ref[slot].astype(BF16)
            w2b_ref[...] = w2f_ref[slot].astype(BF16)

        @pl.when(used)
        def _(e=e, xs_blk=xs_blk, ys_blk=ys_blk):
            xb = _load_token_tiles(xs_blk, MOE_BLOCK).astype(BF16)
            gu = jnp.dot(xb, w1b_ref[...], preferred_element_type=F32) + b1_ref[pl.ds(e, 1), :]
            gate = jnp.minimum(gu[:, :D_FF], SWIGLU_LIMIT)
            lin = jnp.clip(gu[:, D_FF:], -SWIGLU_LIMIT, SWIGLU_LIMIT)
            act = (lin + 1.0) * (gate * jax.nn.sigmoid(SWIGLU_ALPHA * gate))
            ys = jnp.dot(act.astype(BF16), w2b_ref[...], preferred_element_type=F32) + b2_ref[pl.ds(e, 1), :]
            _store_token_tiles(ys_blk, ys)

        @pl.when(jnp.logical_not(used))
        def _(ys_blk=ys_blk):
            ys_blk[...] = jnp.zeros(ys_blk.shape, F32)


def _stage_experts(blk8, xs, w1, b1, w2, b2):
    nrows = xs.shape[0] // TOKEN_TILE_ROWS
    nblk = nrows // MOE_BLOCK
    assert nblk % EXPERT_BLOCKS_PER_STEP == 0
    block_e, nused, parity, nxt = blk8[0, :nblk], blk8[1, :1], blk8[2, :nblk], blk8[3, :nblk]
    tiles = (EXPERT_BLOCKS_PER_STEP * MOE_BLOCK * TOKEN_TILE_ROWS, LANES)
    full = lambda a: pl.BlockSpec(a.shape, lambda j, *_: (0,) * a.ndim)
    grid_spec = pltpu.PrefetchScalarGridSpec(
        num_scalar_prefetch=4,
        grid=(nblk // EXPERT_BLOCKS_PER_STEP,),
        in_specs=[pl.BlockSpec(tiles, lambda j, *_: (j, 0)),
                  pl.BlockSpec(memory_space=pl.ANY), full(b1),
                  pl.BlockSpec(memory_space=pl.ANY), full(b2)],
        out_specs=pl.BlockSpec(tiles, lambda j, *_: (j, 0)),
        scratch_shapes=[pltpu.VMEM((2, D_MODEL, 2 * D_FF), F32), pltpu.VMEM((2, D_FF, D_MODEL), F32),
                        pltpu.VMEM((D_MODEL, 2 * D_FF), BF16), pltpu.VMEM((D_FF, D_MODEL), BF16),
                        pltpu.SemaphoreType.DMA((2, 2))],
    )
    return pl.pallas_call(
        _expert_body,
        grid_spec=grid_spec,
        out_shape=jax.ShapeDtypeStruct((nrows * TOKEN_TILE_ROWS, LANES), F32),
        compiler_params=_params("arbitrary"),
        name="experts",
    )(block_e, nused, parity, nxt, xs, w1, b1, w2, b2)


def _combine_body(tcnt_ref, tcar_ref, pstart_ref, loc_ref, gate_ref, x2_ref, ys_hbm, out_ref,
                  buf_ref, g_ref, sems, *, tm):
    step = pl.program_id(0)
    nloc = TOP_K * tm

    def start_runs(tile, slot):
        def run_copy(local, first, size):
            return pltpu.make_async_copy(_tile_rows(ys_hbm, first, size),
                                         _tile_rows(buf_ref.at[slot], local, size), sems.at[slot])
        _for_each_run(tile, tcnt_ref, tcar_ref, pstart_ref, lambda l, f, s: run_copy(l, f, s).start())

    def wait_buffer(slot):
        pltpu.make_async_copy(ys_hbm.at[pl.ds(0, nloc * TOKEN_TILE_ROWS)], buf_ref.at[slot], sems.at[slot]).wait()

    def combine(slot):
        zpad = jnp.zeros((LANES - 16, LANES), F32)
        lane = lax.broadcasted_iota(I32, (LANES, nloc), 1).astype(F32)
        for c in range(tm // LANES):
            cols_in = slice(slot * tm + c * LANES, slot * tm + (c + 1) * LANES)
            cols = jnp.transpose(jnp.concatenate([loc_ref[:, cols_in].astype(F32), gate_ref[:, cols_in], zpad], axis=0))
            g = jnp.zeros((LANES, nloc), F32)
            for k in range(TOP_K):
                g = jnp.where(lane == cols[:, k:k + 1], cols[:, 8 + k:9 + k], g)
            g_ref[c * LANES:(c + 1) * LANES, :] = g.astype(BF16)
        wait_buffer(slot)
        ys = _load_token_tiles(buf_ref.at[slot], nloc).astype(BF16)
        rows = slice(slot * tm, (slot + 1) * tm)
        out_ref[rows, :] = x2_ref[rows, :] + jnp.dot(g_ref[...], ys, preferred_element_type=F32)

    @pl.when(step == 0)
    def _():
        start_runs(0, 0)

    start_runs(2 * step + 1, 1)
    combine(0)

    @pl.when(step + 1 < pl.num_programs(0))
    def _():
        start_runs(2 * step + 2, 0)

    combine(1)


def _stage_combine(tables, loc8, gate8, x2, ys, tm):
    n = x2.shape[0]
    assert (n // tm) % 2 == 0
    grid_spec = pltpu.PrefetchScalarGridSpec(
        num_scalar_prefetch=3,
        grid=(n // (2 * tm),),
        in_specs=[pl.BlockSpec((8, 2 * tm), lambda i, *_: (0, i)),
                  pl.BlockSpec((8, 2 * tm), lambda i, *_: (0, i)),
                  pl.BlockSpec((2 * tm, D_MODEL), lambda i, *_: (i, 0)),
                  pl.BlockSpec(memory_space=pl.ANY)],
        out_specs=pl.BlockSpec((2 * tm, D_MODEL), lambda i, *_: (i, 0)),
        scratch_shapes=[pltpu.VMEM((2, TOP_K * tm * TOKEN_TILE_ROWS, LANES), F32),
                        pltpu.VMEM((tm, TOP_K * tm), BF16),
                        pltpu.SemaphoreType.DMA((2,))],
    )
    return pl.pallas_call(
        functools.partial(_combine_body, tm=tm),
        grid_spec=grid_spec,
        out_shape=jax.ShapeDtypeStruct((n, D_MODEL), F32),
        compiler_params=_params("arbitrary"),
        name="combine",
    )(*tables, loc8, gate8, x2, ys)


def _moe(x2, xn, loc8, gate8, tcnt, tcar, cnt, w1, b1, w2, b2, tm):
    n = x2.shape[0]
    ntile = n // tm
    nblk = -(-(n * TOP_K) // MOE_BLOCK) + N_EXPERTS
    blk8, pstart, zlo, zhi = _stage_offsets(cnt, nblk)
    per_tile = lambda a: a.reshape(N_EXPERTS, ntile, LANES)[:, :, 0].T
    tables = (per_tile(tcnt), per_tile(tcar), pstart[:, 0])
    xs = _stage_dispatch(tables + (zlo[:, 0], zhi[:, 0]), loc8, xn, nblk * MOE_BLOCK, tm)
    ys = _stage_experts(blk8, xs, w1, b1, w2, b2)
    return _stage_combine(tables, loc8, gate8, x2, ys, tm)


def kernel(x, norm1_g, w_in, mlstm_gate_b, mlstm_norm_g, attn_q_norm_g, attn_k_norm_g, w_mlstm_branch,
           w_attn_branch, w_out, norm2_g, w_router, b_router, w1, b1, w2, b2):
    batch, seq, _ = x.shape
    n = batch * seq
    for l in range(norm1_g.shape[0]):
        x2d = x.reshape(n, D_MODEL)
        tm = min(512, seq)
        mq, kT, mv, so, gi, gf, aq, ak, av, sgm, sga = _stage_inproj(
            x2d, norm1_g[l], w_in[l], mlstm_gate_b[l], attn_q_norm_g[l], attn_k_norm_g[l], batch, seq, tm)
        h_m = _stage_mlstm(mq, kT, mv, so, gi, gf, mlstm_norm_g[l], batch, seq, tm)
        attn = [_stage_attn(aq[g], ak[g], av[g], batch, seq, g)
                for g in range(N_GROUPS)]
        x2, xn, loc8, gate8, tcnt, tcar, cnt = _stage_merge(
            h_m, attn, sgm, sga, x2d, w_mlstm_branch[l], w_attn_branch[l], w_out[l], norm2_g[l],
            w_router[l], b_router[l], batch, seq, tm)
        out = _moe(x2, xn, loc8, gate8, tcnt, tcar, cnt, w1[l], b1[l], w2[l], b2[l], tm)
        x = out.reshape(batch, seq, D_MODEL)
    return x
```

```python
import functools

import numpy as np
import jax
import jax.numpy as jnp
from jax import lax
from jax.experimental import pallas as pl
from jax.experimental.pallas import tpu as pltpu

F32 = jnp.float32
BF16 = jnp.bfloat16
I32 = jnp.int32

D_MODEL = 1024
M_HEADS = 4
M_QK_DIM = 64
M_V_DIM = 128
GATE_SOFTCAP = 15.0
A_HEADS = 4
A_HEAD_DIM = 64
DILATED_PATTERNS = ((128, 1), (512, 4), (2048, 16))
N_GROUPS = len(DILATED_PATTERNS)
N_BACK = 128
N_EXPERTS = 32
TOP_K = 4
D_FF = 1024
SWIGLU_LIMIT = 7.0
SWIGLU_ALPHA = 1.702
MOE_BLOCK = 512
EPS = 1e-6

M_WIDTH = M_HEADS * M_V_DIM
M_QK_WIDTH = M_HEADS * M_QK_DIM
A_WIDTH = A_HEADS * A_HEAD_DIM
IN_SPLITS = (M_QK_WIDTH, M_QK_WIDTH, M_WIDTH, M_WIDTH, 2 * M_HEADS,
             N_GROUPS * A_WIDTH, N_GROUPS * A_WIDTH, N_GROUPS * A_WIDTH, D_MODEL, D_MODEL)

LANES = 128
VMEM_LIMIT = 56 * 1024 * 1024

BF16_EXACT_INT = 256.0

_NT = (((1,), (1,)), ((), ()))
_TN = (((0,), (1,)), ((), ()))


def _alibi_slopes():
    n = N_GROUPS * A_HEADS
    s = np.exp2(-8.0 * np.arange(1, n + 1) / n).astype(np.float32)
    return s.reshape(N_GROUPS, A_HEADS)


def _params(*sem):
    return pltpu.CompilerParams(dimension_semantics=sem, vmem_limit_bytes=VMEM_LIMIT)


def _log_sigmoid(x):
    return jnp.minimum(x, 0.0) - jnp.log1p(jnp.exp(-jnp.abs(x)))


_A_WIDTH = sum(IN_SPLITS[:4])
_B_START = _A_WIDTH + IN_SPLITS[4]


def _piece_segments(widths):
    bounds, start = [], 0
    for width in widths:
        bounds.append((start, start + width))
        start += width
    return bounds


_C_MQ, _C_MK, _C_MV, _C_MO = _piece_segments(IN_SPLITS[:4])
_C_AQ, _C_AK, _C_AV, _C_GM, _C_GA = _piece_segments(IN_SPLITS[5:])


def _split_residues(val, d, out_ref, st_ref):
    t = val.shape[0]
    if d == 1:
        out_ref[0, 0] = val.astype(out_ref.dtype)
        return
    a_ref = st_ref.at[0]
    a_ref[0] = val[:, :LANES]
    a_ref[1] = val[:, LANES:]
    m = t // d
    if d == 16:
        b_ref = st_ref.at[1]
        for r0 in range(4):
            b_ref[0, r0 * 4 * m:(r0 + 1) * 4 * m, :] = a_ref[0, pl.ds(r0, 4 * m, stride=4), :]
            b_ref[1, r0 * 4 * m:(r0 + 1) * 4 * m, :] = a_ref[1, pl.ds(r0, 4 * m, stride=4), :]
        for r0 in range(4):
            for r1 in range(4):
                piece = jnp.concatenate([b_ref[0, pl.ds(r0 * 4 * m + r1, m, stride=4), :],
                                         b_ref[1, pl.ds(r0 * 4 * m + r1, m, stride=4), :]], axis=1)
                out_ref[0, 4 * r1 + r0] = piece.astype(out_ref.dtype)
        return
    for r in range(d):
        piece = jnp.concatenate([a_ref[0, pl.ds(r, m, stride=d), :], a_ref[1, pl.ds(r, m, stride=d), :]], axis=1)
        out_ref[0, r] = piece.astype(out_ref.dtype)


def _merge_residues(ref, d, st_ref):
    if d == 1:
        return ref[0, 0].astype(F32)
    m = ref.shape[2]
    if d == 16:
        a_ref, b_ref = st_ref.at[0], st_ref.at[1]
        for r0 in range(4):
            for r1 in range(4):
                blk = ref[0, 4 * r1 + r0].astype(F32)
                a_ref[0, pl.ds(r0 * 4 * m + r1, m, stride=4), :] = blk[:, :LANES]
                a_ref[1, pl.ds(r0 * 4 * m + r1, m, stride=4), :] = blk[:, LANES:]
        for r0 in range(4):
            b_ref[0, pl.ds(r0, 4 * m, stride=4), :] = a_ref[0, r0 * 4 * m:(r0 + 1) * 4 * m, :]
            b_ref[1, pl.ds(r0, 4 * m, stride=4), :] = a_ref[1, r0 * 4 * m:(r0 + 1) * 4 * m, :]
        return jnp.concatenate([b_ref[0], b_ref[1]], axis=1)
    a_ref = st_ref.at[0]
    for r in range(d):
        blk = ref[0, r].astype(F32)
        a_ref[0, pl.ds(r, m, stride=d), :] = blk[:, :LANES]
        a_ref[1, pl.ds(r, m, stride=d), :] = blk[:, LANES:]
    return jnp.concatenate([a_ref[0], a_ref[1]], axis=1)


def _inproj_body(x_ref, g1_ref, wa_ref, wg_ref, wb_ref, gb_ref, gq_ref, gk_ref,
                 mq_ref, kT_ref, mv_ref, so_ref, gi_ref, gf_ref,
                 q0_ref, q1_ref, q2_ref, k0_ref, k1_ref, k2_ref, v0_ref, v1_ref, v2_ref,
                 sgm_ref, sga_ref, st_ref):
    x = x_ref[...]
    h = x * lax.rsqrt(jnp.mean(x * x, axis=-1, keepdims=True) + EPS) * g1_ref[...]
    hb = h.astype(BF16)

    def seg(w_ref, c):
        return jnp.dot(hb, w_ref[:, c[0]:c[1]], preferred_element_type=F32)

    mq_ref[...] = seg(wa_ref, _C_MQ).astype(BF16)
    mv_ref[...] = seg(wa_ref, _C_MV).astype(BF16)
    so_ref[...] = jax.nn.sigmoid(seg(wa_ref, _C_MO)).astype(BF16)
    hid_r = lax.broadcasted_iota(I32, (A_WIDTH, A_WIDTH), 0) // A_HEAD_DIM
    hid_c = lax.broadcasted_iota(I32, (A_WIDTH, A_WIDTH), 1) // A_HEAD_DIM
    head_ones = (hid_r == hid_c).astype(BF16)
    for c, refs, gain_ref in ((_C_AQ, (q0_ref, q1_ref, q2_ref), gq_ref), (_C_AK, (k0_ref, k1_ref, k2_ref), gk_ref),
                              (_C_AV, (v0_ref, v1_ref, v2_ref), None)):
        val = seg(wb_ref, c)
        for g, ref in enumerate(refs):
            piece = val[:, g * A_WIDTH:(g + 1) * A_WIDTH]
            if gain_ref is not None:
                ss = jnp.dot((piece * piece).astype(BF16), head_ones, preferred_element_type=F32)
                piece = piece * lax.rsqrt(ss * (1.0 / A_HEAD_DIM) + EPS) * gain_ref[:, g * A_WIDTH:(g + 1) * A_WIDTH]
            _split_residues(piece, DILATED_PATTERNS[g][1], ref, st_ref)
    sgm_ref[...] = jax.nn.sigmoid(seg(wb_ref, _C_GM)).astype(BF16)
    sga_ref[...] = jax.nn.sigmoid(seg(wb_ref, _C_GA)).astype(BF16)

    kT_ref[...] = lax.dot_general(wa_ref[:, _C_MK[0]:_C_MK[1]], hb, _TN, preferred_element_type=F32).astype(BF16)
    zg = lax.dot_general(wg_ref[...], hb, _TN, preferred_element_type=F32)
    zi = zg[0:8] + gb_ref[0:8]
    zf = zg[M_HEADS:M_HEADS + 8] + gb_ref[8:16]
    gi_ref[...] = GATE_SOFTCAP * jnp.tanh(zi / GATE_SOFTCAP)
    gf_ref[...] = _log_sigmoid(GATE_SOFTCAP * jnp.tanh(zf / GATE_SOFTCAP))


def _stage_inproj(x2d, norm1_g, w_in, gate_b, gq, gk, batch, seq, tm):
    n = x2d.shape[0]
    steps = seq // tm
    wa = w_in[:, :_A_WIDTH].astype(BF16)
    wg = jnp.pad(w_in[:, _A_WIDTH:_B_START], ((0, 0), (0, LANES - IN_SPLITS[4]))).astype(BF16)
    wb = w_in[:, _B_START:].astype(BF16)
    gb = jnp.zeros((16, 1), F32)
    gb = gb.at[0:4, 0].set(gate_b[:M_HEADS].astype(F32)).at[8:12, 0].set(gate_b[M_HEADS:].astype(F32))
    g1 = norm1_g.astype(F32).reshape(1, D_MODEL)
    gq_t = (jnp.tile(gq.astype(F32), (1, A_HEADS)) * (A_HEAD_DIM ** -0.5)).reshape(1, N_GROUPS * A_WIDTH)
    gk_t = jnp.tile(gk.astype(F32), (1, A_HEADS)).reshape(1, N_GROUPS * A_WIDTH)

    row = lambda w: pl.BlockSpec((tm, w), lambda i: (i, 0))
    rowT = lambda r: pl.BlockSpec((r, tm), lambda i: (0, i))
    full = lambda a: pl.BlockSpec(a.shape, lambda i: (0,) * a.ndim)
    dils = [d for _, d in DILATED_PATTERNS]
    res_shape = lambda d: jax.ShapeDtypeStruct((batch, d, seq // d, A_WIDTH), BF16)
    res_spec = lambda d: pl.BlockSpec((1, d, tm // d, A_WIDTH), lambda i: (i // steps, 0, i % steps, 0))
    out_shapes = (
        jax.ShapeDtypeStruct((n, M_QK_WIDTH), BF16),
        jax.ShapeDtypeStruct((M_QK_WIDTH, n), BF16),
        jax.ShapeDtypeStruct((n, M_WIDTH), BF16),
        jax.ShapeDtypeStruct((n, M_WIDTH), BF16),
        jax.ShapeDtypeStruct((8, n), F32),
        jax.ShapeDtypeStruct((8, n), F32),
        *[res_shape(d) for d in dils], *[res_shape(d) for d in dils], *[res_shape(d) for d in dils],
        jax.ShapeDtypeStruct((n, D_MODEL), BF16),
        jax.ShapeDtypeStruct((n, D_MODEL), BF16),
    )
    out_specs = (row(M_QK_WIDTH), rowT(M_QK_WIDTH), row(M_WIDTH), row(M_WIDTH), rowT(8), rowT(8),
                 *[res_spec(d) for d in dils], *[res_spec(d) for d in dils], *[res_spec(d) for d in dils],
                 row(D_MODEL), row(D_MODEL))
    outs = pl.pallas_call(
        _inproj_body,
        grid=(n // tm,),
        in_specs=[row(D_MODEL), full(g1), full(wa), full(wg), full(wb), full(gb), full(gq_t), full(gk_t)],
        out_specs=out_specs,
        out_shape=out_shapes,
        scratch_shapes=[pltpu.VMEM((2, 2, tm, LANES), F32)],
        compiler_params=_params("parallel"),
        name="inproj",
    )(x2d, g1, wa, wg, wb, gb, gq_t, gk_t)
    mq, kT, mv, so, gi, gf = outs[:6]
    aq, ak, av = outs[6:9], outs[9:12], outs[12:15]
    return mq, kT, mv, so, gi, gf, aq, ak, av, outs[15], outs[16]


M_CHUNK_LEN = 128


def _mlstm_body(q_ref, v_ref, so_ref, ng_ref, *rest, nchunk, nseq):
    kT_refs, gi_refs, gf_refs = rest[0:nseq], rest[nseq:2 * nseq], rest[2 * nseq:3 * nseq]
    o_ref, c_ref, m_ref = rest[3 * nseq:]
    L = M_CHUNK_LEN

    @pl.when(pl.program_id(0) == 0)
    def _():
        c_ref[...] = jnp.zeros_like(c_ref)
        m_ref[...] = jnp.zeros_like(m_ref)

    lane8 = lax.broadcasted_iota(I32, (8, L), 1)
    causal = lax.broadcasted_iota(I32, (L, L), 1) <= lax.broadcasted_iota(I32, (L, L), 0)
    lo_half = lax.broadcasted_iota(I32, (L, LANES), 1) < M_QK_DIM
    ones = jnp.ones((L, M_V_DIM), BF16)

    heads = range(M_HEADS)
    cstate = [[c_ref[s, h * M_QK_DIM:(h + 1) * M_QK_DIM, :] for h in heads] for s in range(nseq)]
    m_prev = [m_ref[s, :, 0:1] for s in range(nseq)]
    units = []
    for c in range(nchunk):
        rows = slice(c * L, (c + 1) * L)
        for s in range(nseq):
            gi = gi_refs[s][:, rows]
            b = gf_refs[s][:, rows]
            sh = 1
            while sh < L:
                b = b + jnp.where(lane8 >= sh, pltpu.roll(b, sh, 1), 0.0)
                sh *= 2
            u = gi - b
            g = b[:, L - 1:L]
            a = g + u
            amax = jnp.max(a, axis=1, keepdims=True)
            m_new = jnp.maximum(g + m_prev[s], amax)
            w = jnp.exp(a - m_new) * (M_QK_DIM ** -0.5)
            s_old = jnp.exp(g + m_prev[s] - m_new)
            vext = [jnp.concatenate([v_ref[s, rows, h * M_V_DIM:(h + 1) * M_V_DIM], ones], axis=1) for h in heads]
            cloc = []
            for h in heads:
                hr = slice(h * M_QK_DIM, (h + 1) * M_QK_DIM)
                kw = (kT_refs[s][hr, rows].astype(F32) * w[h:h + 1, :]).astype(BF16)
                cloc.append(jnp.dot(kw, vext[h], preferred_element_type=F32))
            units.append(dict(seq=s, rows=rows, b=b, u=u, m_prev=m_prev[s], state=cstate[s], vext=vext))
            cstate[s] = [s_old[h:h + 1, :] * cstate[s][h] + cloc[h] for h in heads]
            m_prev[s] = m_new
    for s in range(nseq):
        for h in heads:
            c_ref[s, h * M_QK_DIM:(h + 1) * M_QK_DIM, :] = cstate[s][h]
        m_ref[s] = jnp.broadcast_to(m_prev[s], (8, LANES))

    for un in units:
        s, rows = un["seq"], un["rows"]
        un["s"], un["qc"] = [], []
        for p in range(M_HEADS // 2):
            lanes_p = slice(p * LANES, (p + 1) * LANES)
            q_pair = q_ref[s, rows, lanes_p]
            kT_pair = kT_refs[s][lanes_p, rows]
            c_pair = jnp.concatenate([un["state"][2 * p], un["state"][2 * p + 1]], axis=0).astype(BF16)
            for hh in range(2):
                qm = jnp.where(lo_half if hh == 0 else jnp.logical_not(lo_half), q_pair, jnp.zeros_like(q_pair))
                un["s"].append(jnp.dot(qm, kT_pair, preferred_element_type=F32) * (M_QK_DIM ** -0.5))
                un["qc"].append(jnp.dot(qm, c_pair, preferred_element_type=F32))

    for un in units:
        s, rows, b, u, mp = un["seq"], un["rows"], un["b"], un["u"], un["m_prev"]
        for h in heads:
            hl = slice(h * M_V_DIM, (h + 1) * M_V_DIM)
            bcol = jnp.transpose(jnp.broadcast_to(b[h:h + 1, :], (L, L)))
            dm = jnp.where(causal, bcol + u[h:h + 1, :], -jnp.inf)
            inter = bcol + mp[h:h + 1, :]
            m_t = jnp.maximum(inter, jnp.max(dm, axis=1, keepdims=True))
            pmat = (un["s"][h] * jnp.exp(dm - m_t)).astype(BF16)
            sc = jnp.exp(inter - m_t)
            out = (jnp.dot(pmat, un["vext"][h], preferred_element_type=F32)
                   + jnp.concatenate([sc, sc], axis=1) * un["qc"][h])
            hv = out[:, :M_V_DIM] / jnp.maximum(jnp.abs(out[:, M_V_DIM:]), jnp.exp(-m_t))
            hn = hv * lax.rsqrt(jnp.mean(hv * hv, axis=1, keepdims=True) + EPS)
            hn = hn * ng_ref[:, hl] * so_ref[s, rows, hl].astype(F32)
            o_ref[s, rows, hl] = hn.astype(BF16)


def _stage_mlstm(mq, kT, mv, so, gi, gf, norm_g, batch, seq, rows_per_step):
    n = batch * seq
    R = rows_per_step
    steps = seq // R
    ng = norm_g.astype(F32).reshape(1, M_WIDTH)
    per_seq = lambda a: a.reshape(batch, seq, a.shape[1])
    row = lambda w: pl.BlockSpec((batch, R, w), lambda i: (0, i, 0))
    colT = lambda r, s: pl.BlockSpec((r, R), lambda i, s=s: (0, s * steps + i))
    seqs = range(batch)
    out = pl.pallas_call(
        functools.partial(_mlstm_body, nchunk=R // M_CHUNK_LEN, nseq=batch),
        grid=(steps,),
        in_specs=[row(M_QK_WIDTH), row(M_WIDTH), row(M_WIDTH), pl.BlockSpec((1, M_WIDTH), lambda i: (0, 0)),
                  *[colT(M_QK_WIDTH, s) for s in seqs], *[colT(8, s) for s in seqs], *[colT(8, s) for s in seqs]],
        out_specs=row(M_WIDTH),
        out_shape=jax.ShapeDtypeStruct((batch, seq, M_WIDTH), BF16),
        scratch_shapes=[pltpu.VMEM((batch, M_QK_WIDTH, 2 * M_V_DIM), F32), pltpu.VMEM((batch, 8, LANES), F32)],
        compiler_params=_params("arbitrary"),
        name="mlstm",
    )(per_seq(mq), per_seq(mv), per_seq(so), ng, *[kT] * batch, *[gi] * batch, *[gf] * batch)
    return out.reshape(n, M_WIDTH)


def _attn_body(q_ref, kp_ref, kc_ref, vp_ref, vc_ref, o_ref, lse_ref, *, dil, slopes, lq):
    QB = N_BACK
    first = pl.program_id(2) == 0
    qn = q_ref[0, 0]
    kcn = kc_ref[0, 0]
    kpn = kp_ref[0, 0]
    vc = vc_ref[0, 0]
    vp = vp_ref[0, 0]

    qi = lax.broadcasted_iota(I32, (QB, 2 * QB), 0)
    kj = lax.broadcasted_iota(I32, (QB, 2 * QB), 1)
    dist = qi + QB - kj
    band = jnp.logical_and(dist >= 0, dist <= N_BACK)
    distf = (dist * dil).astype(F32)
    bias = [jnp.where(band, -float(slopes[h]) * distf, -jnp.inf) for h in range(A_HEADS)]
    no_prev = jnp.logical_and(first, kj < QB)
    lo_half = lax.broadcasted_iota(I32, (QB, LANES), 1) < A_HEAD_DIM
    ones = jnp.ones((2 * QB, LANES), BF16)

    units = []
    for j in range(lq // QB):
        rows = slice(j * QB, (j + 1) * QB)
        prow = slice((j - 1) * QB, j * QB)
        keys = jnp.concatenate([kpn if j == 0 else kcn[prow], kcn[rows]], axis=0)
        vals = jnp.concatenate([vp if j == 0 else vc[prow], vc[rows]], axis=0)
        for p in range(A_HEADS // 2):
            lanes_p = slice(p * LANES, (p + 1) * LANES)
            q_pair = qn[rows, lanes_p]
            k_pair = keys[:, lanes_p]
            vext = jnp.concatenate([vals[:, lanes_p], ones], axis=1)
            scores = []
            for hh in range(2):
                sel = lo_half if hh == 0 else jnp.logical_not(lo_half)
                qm = jnp.where(sel, q_pair, jnp.zeros_like(q_pair))
                scores.append(lax.dot_general(qm, k_pair, _NT, preferred_element_type=F32))
            units.append((j, rows, lanes_p, p, vext, scores))

    for j, rows, lanes_p, p, vext, scores in units:
        o_pair = None
        l_pair = None
        for hh in range(2):
            s = scores[hh] + bias[2 * p + hh]
            if j == 0:
                s = jnp.where(no_prev, -jnp.inf, s)
            m = jnp.max(s, axis=1, keepdims=True)
            pv = jnp.dot(jnp.exp(s - m).astype(BF16), vext, preferred_element_type=F32)
            den = pv[:, LANES:]
            o_h = pv[:, :LANES] / den
            l_h = m + jnp.log(den)
            o_pair = o_h if hh == 0 else jnp.where(lo_half, o_pair, o_h)
            l_pair = l_h if hh == 0 else jnp.where(lo_half, l_pair, l_h)
        o_ref[0, 0, rows, lanes_p] = o_pair.astype(BF16)
        lse_ref[0, 0, rows, lanes_p] = l_pair


def _stage_attn(aq, ak, av, batch, seq, group):
    _, dil = DILATED_PATTERNS[group]
    L = seq // dil
    assert L % N_BACK == 0
    lq = min(2048, L)
    nq = L // lq
    sub = lq // N_BACK
    cur = pl.BlockSpec((1, 1, lq, A_WIDTH), lambda b, r, i: (b, r, i, 0))
    prev = pl.BlockSpec((1, 1, N_BACK, A_WIDTH), lambda b, r, i: (b, r, jnp.maximum(i * sub - 1, 0), 0))
    return pl.pallas_call(
        functools.partial(_attn_body, dil=dil, slopes=tuple(_alibi_slopes()[group]), lq=lq),
        grid=(batch, dil, nq),
        in_specs=[cur, prev, cur, prev, cur],
        out_specs=(cur, cur),
        out_shape=(jax.ShapeDtypeStruct((batch, dil, L, A_WIDTH), BF16),
                   jax.ShapeDtypeStruct((batch, dil, L, A_WIDTH), F32)),
        compiler_params=_params("parallel", "parallel", "parallel"),
        name=f"dilated_attn_d{dil}",
    )(aq, ak, ak, av, av)


TOKEN_TILE_ROWS = D_MODEL // LANES


def _store_token_tiles(ref, val):
    t = val.shape[0]
    for s in range(TOKEN_TILE_ROWS):
        ref[pl.ds(s, t, stride=TOKEN_TILE_ROWS), :] = val[:, s * LANES:(s + 1) * LANES]


def _load_token_tiles(ref, t):
    return jnp.concatenate([ref[pl.ds(s, t, stride=TOKEN_TILE_ROWS), :] for s in range(TOKEN_TILE_ROWS)], axis=1)


def _rows8(vals):
    t = vals[0].shape[1]
    rid = lax.broadcasted_iota(I32, (8, t), 0)
    out = jnp.zeros((8, t), vals[0].dtype)
    for k, v in enumerate(vals):
        out = jnp.where(rid == k, jnp.broadcast_to(v, (8, t)), out)
    return out


def _merge_body(hm_ref, o1_ref, o2_ref, o3_ref, l1_ref, l2_ref, l3_ref, sgm_ref, sga_ref, x_ref,
                wm_ref, wa_ref, wo_ref, g2_ref, wrh_ref, br_ref,
                x2_ref, xn_ref, loc_ref, gate_ref, tcnt_ref, tcar_ref, cnt_ref, carry_ref, st_ref):
    @pl.when(pl.program_id(0) == 0)
    def _():
        carry_ref[...] = jnp.zeros_like(carry_ref)

    m_branch = jnp.dot(hm_ref[...], wm_ref[...], preferred_element_type=F32)
    dils = [d for _, d in DILATED_PATTERNS]
    l1, l2, l3 = [_merge_residues(r, d, st_ref) for r, d in zip((l1_ref, l2_ref, l3_ref), dils)]
    lmax = jnp.maximum(jnp.maximum(l1, l2), l3)
    e1, e2, e3 = jnp.exp(l1 - lmax), jnp.exp(l2 - lmax), jnp.exp(l3 - lmax)
    num = e1 * _merge_residues(o1_ref, dils[0], st_ref)
    num = num + e2 * _merge_residues(o2_ref, dils[1], st_ref)
    num = num + e3 * _merge_residues(o3_ref, dils[2], st_ref)
    h_a = num / (e1 + e2 + e3)
    y = (sgm_ref[...].astype(F32) * m_branch
         + sga_ref[...].astype(F32) * jnp.dot(h_a.astype(BF16), wa_ref[...], preferred_element_type=F32))
    x2 = x_ref[...] + jnp.dot(y.astype(BF16), wo_ref[...], preferred_element_type=F32)
    x2_ref[...] = x2
    xn = x2 * lax.rsqrt(jnp.mean(x2 * x2, axis=-1, keepdims=True) + EPS) * g2_ref[...]
    xh = xn.astype(BF16)
    xn_ref[...] = xh

    logits = lax.dot_general(wrh_ref[...], xh, _NT, preferred_element_type=F32) + br_ref[...]
    t = logits.shape[1]
    eid = lax.broadcasted_iota(I32, (N_EXPERTS, t), 0).astype(F32)
    vals = logits
    top_v, top_i = [], []
    for _ in range(TOP_K):
        mx = jnp.max(vals, axis=0, keepdims=True)
        ik = jnp.min(jnp.where(vals == mx, eid, float(N_EXPERTS)), axis=0, keepdims=True)
        top_v.append(mx)
        top_i.append(ik)
        vals = jnp.where(eid == ik, -jnp.inf, vals)
    ex = [jnp.exp(v - top_v[0]) for v in top_v]
    den = ex[0] + ex[1] + ex[2] + ex[3]
    gate_ref[...] = _rows8([e / den for e in ex])

    chosen = jnp.zeros((N_EXPERTS, t), F32)
    for ik in top_i:
        chosen = chosen + (eid == ik).astype(F32)
    before = (lax.broadcasted_iota(I32, (t, t), 0) < lax.broadcasted_iota(I32, (t, t), 1)).astype(BF16)
    prefix = jnp.dot(chosen.astype(BF16), before, preferred_element_type=F32)
    tcount = jnp.broadcast_to(jnp.sum(chosen, axis=1, keepdims=True), (N_EXPERTS, LANES))
    below = (lax.broadcasted_iota(I32, (N_EXPERTS, N_EXPERTS), 1)
             < lax.broadcasted_iota(I32, (N_EXPERTS, N_EXPERTS), 0)).astype(BF16)
    t_hi = jnp.floor(tcount * (1.0 / BF16_EXACT_INT)) * BF16_EXACT_INT
    tile_off = (jnp.dot(below, t_hi.astype(BF16), preferred_element_type=F32)
                + jnp.dot(below, (tcount - t_hi).astype(BF16), preferred_element_type=F32))
    pos = prefix + tile_off[:, 0:1]
    loc_ref[...] = _rows8([jnp.sum(jnp.where(eid == ik, pos, 0.0), axis=0, keepdims=True).astype(I32)
                           for ik in top_i])
    carry = carry_ref[...]
    tcnt_ref[...] = tcount.astype(I32)
    tcar_ref[...] = carry.astype(I32)
    total = carry + tcount
    carry_ref[...] = total
    cnt_ref[...] = total


def _stage_merge(h_m, attn, sgm, sga, x2d, w_mb, w_ab, w_out, norm2_g, w_router, b_router, batch, seq, tm):
    n = x2d.shape[0]
    steps = seq // tm
    (o1, l1), (o2, l2), (o3, l3) = attn
    wm = w_mb.astype(BF16)
    wa = w_ab.astype(BF16)
    wo = w_out.astype(BF16)
    g2 = norm2_g.astype(F32).reshape(1, D_MODEL)
    wrh = w_router.astype(BF16).T
    br = b_router.astype(F32).reshape(N_EXPERTS, 1)
    row = lambda w: pl.BlockSpec((tm, w), lambda i: (i, 0))
    rowT = lambda r: pl.BlockSpec((r, tm), lambda i: (0, i))
    full = lambda a: pl.BlockSpec(a.shape, lambda i: (0,) * a.ndim)
    res = lambda d: pl.BlockSpec((1, d, tm // d, A_WIDTH), lambda i: (i // steps, 0, i % steps, 0))
    dils = [d for _, d in DILATED_PATTERNS]
    per_tile = pl.BlockSpec((N_EXPERTS, LANES), lambda i: (0, i))
    return pl.pallas_call(
        _merge_body,
        grid=(n // tm,),
        in_specs=[row(M_WIDTH), *[res(d) for d in dils], *[res(d) for d in dils],
                  row(D_MODEL), row(D_MODEL), row(D_MODEL),
                  full(wm), full(wa), full(wo), full(g2), full(wrh), full(br)],
        out_specs=(row(D_MODEL), row(D_MODEL), rowT(8), rowT(8), per_tile, per_tile,
                   pl.BlockSpec((N_EXPERTS, LANES), lambda i: (0, 0))),
        out_shape=(jax.ShapeDtypeStruct((n, D_MODEL), F32),
                   jax.ShapeDtypeStruct((n, D_MODEL), BF16),
                   jax.ShapeDtypeStruct((8, n), I32),
                   jax.ShapeDtypeStruct((8, n), F32),
                   jax.ShapeDtypeStruct((N_EXPERTS, (n // tm) * LANES), I32),
                   jax.ShapeDtypeStruct((N_EXPERTS, (n // tm) * LANES), I32),
                   jax.ShapeDtypeStruct((N_EXPERTS, LANES), F32)),
        scratch_shapes=[pltpu.VMEM((N_EXPERTS, LANES), F32), pltpu.VMEM((2, 2, tm, LANES), F32)],
        compiler_params=_params("arbitrary"),
        name="merge_route",
    )(h_m, o1, o2, o3, l1, l2, l3, sgm, sga, x2d, wm, wa, wo, g2, wrh, br)


def _offsets_body(cnt_ref, blk_ref, pstart_ref, zlo_ref, zhi_ref, *, nblk_pad):
    cnt = cnt_ref[...]
    padded = jnp.floor((cnt + (MOE_BLOCK - 1)) * (1.0 / MOE_BLOCK)) * MOE_BLOCK
    lower = (lax.broadcasted_iota(I32, (N_EXPERTS, N_EXPERTS), 1)
             <= lax.broadcasted_iota(I32, (N_EXPERTS, N_EXPERTS), 0)).astype(BF16)
    nb = padded * (1.0 / MOE_BLOCK)
    nb_hi = jnp.floor(nb * (1.0 / BF16_EXACT_INT)) * BF16_EXACT_INT
    pends = (jnp.dot(lower, nb_hi.astype(BF16), preferred_element_type=F32)
             + jnp.dot(lower, (nb - nb_hi).astype(BF16), preferred_element_type=F32)) * MOE_BLOCK
    pstart = pends - padded
    pstart_ref[...] = pstart.astype(I32)
    zlo_ref[...] = (pstart + cnt).astype(I32)
    zhi_ref[...] = pends.astype(I32)

    first_row = (lax.broadcasted_iota(I32, (N_EXPERTS, nblk_pad), 1) * MOE_BLOCK).astype(F32)
    pe = jnp.broadcast_to(pends[:, 0:1], (N_EXPERTS, nblk_pad))
    be = jnp.sum((pe <= first_row).astype(F32), axis=0, keepdims=True)
    be = jnp.minimum(be, float(N_EXPERTS - 1))
    nused = pends[N_EXPERTS - 1:N_EXPERTS, 0:1] * (1.0 / MOE_BLOCK)
    nonempty = jnp.broadcast_to(padded[:, 0:1], (N_EXPERTS, nblk_pad)) > 0.0
    runidx = jnp.sum(jnp.logical_and(pe <= first_row, nonempty).astype(F32), axis=0, keepdims=True)
    parity = runidx - 2.0 * jnp.floor(runidx * 0.5)
    eid = lax.broadcasted_iota(I32, (N_EXPERTS, nblk_pad), 0).astype(F32)
    later = jnp.logical_and(eid > be, nonempty)
    nxt = jnp.min(jnp.where(later, eid, float(N_EXPERTS)), axis=0, keepdims=True)
    blk_ref[...] = _rows8([be.astype(I32), jnp.broadcast_to(nused, (1, nblk_pad)).astype(I32),
                           parity.astype(I32), nxt.astype(I32)])


def _stage_offsets(cnt, nblk):
    nblk_pad = -(-nblk // LANES) * LANES
    const = lambda r, c: pl.BlockSpec((r, c), lambda i: (0, 0))
    per_expert = jax.ShapeDtypeStruct((N_EXPERTS, LANES), I32)
    return pl.pallas_call(
        functools.partial(_offsets_body, nblk_pad=nblk_pad),
        grid=(1,),
        in_specs=[const(N_EXPERTS, LANES)],
        out_specs=(const(8, nblk_pad), const(N_EXPERTS, LANES), const(N_EXPERTS, LANES), const(N_EXPERTS, LANES)),
        out_shape=(jax.ShapeDtypeStruct((8, nblk_pad), I32), per_expert, per_expert, per_expert),
        compiler_params=_params("arbitrary"),
        name="route_offsets",
    )(cnt)


RUN_BITS = 10


def _tile_rows(ref, first_row, nrows):
    start = first_row * TOKEN_TILE_ROWS
    if not isinstance(first_row, int):
        start = pl.multiple_of(start, TOKEN_TILE_ROWS)
    return ref.at[pl.ds(start, nrows * TOKEN_TILE_ROWS)]


def _for_each_piece(length, fn):
    for b in reversed(range(RUN_BITS)):
        @pl.when(((length >> b) & 1) == 1)
        def _(b=b):
            fn((length >> (b + 1)) << (b + 1), 1 << b)


def _for_each_run(tile, tcnt_ref, tcar_ref, pstart_ref, fn):
    def per_expert(e, local):
        count = tcnt_ref[tile, e]
        first = pstart_ref[e] + tcar_ref[tile, e]
        _for_each_piece(count, lambda off, size: fn(local + off, first + off, size))
        return local + count

    lax.fori_loop(0, N_EXPERTS, per_expert, 0)


PERM_CHUNK = 256


def _dispatch_body(tcnt_ref, tcar_ref, pstart_ref, zlo_ref, zhi_ref, loc_ref, xn_ref, xs_hbm,
                   buf_ref, sems, *, tm):
    step = pl.program_id(0)
    nloc_tiles = TOP_K * tm * TOKEN_TILE_ROWS

    def wait_buffer(slot):
        pltpu.make_async_copy(buf_ref.at[slot], xs_hbm.at[pl.ds(0, nloc_tiles)], sems.at[slot]).wait()

    for slot in range(2):
        tile = 2 * step + slot

        @pl.when(step > 0)
        def _(slot=slot):
            wait_buffer(slot)

        loc = loc_ref[:, slot * tm:(slot + 1) * tm]
        xn = xn_ref[slot * tm:(slot + 1) * tm, :]
        for c in range(TOP_K * tm // PERM_CHUNK):
            lid = lax.broadcasted_iota(I32, (PERM_CHUNK, tm), 0) + c * PERM_CHUNK
            hit = lid == loc[0:1, :]
            for k in range(1, TOP_K):
                hit = jnp.logical_or(hit, lid == loc[k:k + 1, :])
            rows = jnp.dot(jnp.where(hit, 1.0, 0.0).astype(BF16), xn, preferred_element_type=F32)
            _store_token_tiles(
                buf_ref.at[slot, pl.ds(c * PERM_CHUNK * TOKEN_TILE_ROWS, PERM_CHUNK * TOKEN_TILE_ROWS)], rows)

        def run_copy(local, first, size, slot=slot):
            return pltpu.make_async_copy(_tile_rows(buf_ref.at[slot], local, size),
                                         _tile_rows(xs_hbm, first, size), sems.at[slot])

        _for_each_run(tile, tcnt_ref, tcar_ref, pstart_ref, lambda l, f, s: run_copy(l, f, s).start())

    @pl.when(step == pl.num_programs(0) - 1)
    def _():
        wait_buffer(0)
        wait_buffer(1)
        zsrc = buf_ref.at[0]
        zsrc[pl.ds(0, MOE_BLOCK * TOKEN_TILE_ROWS), :] = jnp.zeros((MOE_BLOCK * TOKEN_TILE_ROWS, LANES), F32)

        def zero_copy(first, size):
            return pltpu.make_async_copy(_tile_rows(zsrc, 0, size), _tile_rows(xs_hbm, first, size), sems.at[0])

        def per_expert(e, carry):
            lo = zlo_ref[e]
            npad = zhi_ref[e] - lo
            _for_each_piece(npad, lambda off, size: zero_copy(lo + off, size).start())
            _for_each_piece(npad, lambda off, size: zero_copy(lo + off, size).wait())
            return carry

        lax.fori_loop(0, N_EXPERTS, per_expert, 0)

        first_unused = zhi_ref[N_EXPERTS - 1] // MOE_BLOCK
        nblk = xs_hbm.shape[0] // (MOE_BLOCK * TOKEN_TILE_ROWS)

        def tail(blk, carry):
            zero_copy(blk * MOE_BLOCK, MOE_BLOCK).start()
            zero_copy(blk * MOE_BLOCK, MOE_BLOCK).wait()
            return carry

        lax.fori_loop(first_unused, nblk, tail, 0)


def _stage_dispatch(tables, loc8, xn, nrows, tm):
    n = xn.shape[0]
    assert TOP_K * tm >= MOE_BLOCK and (n // tm) % 2 == 0
    grid_spec = pltpu.PrefetchScalarGridSpec(
        num_scalar_prefetch=5,
        grid=(n // (2 * tm),),
        in_specs=[pl.BlockSpec((8, 2 * tm), lambda i, *_: (0, i)),
                  pl.BlockSpec((2 * tm, D_MODEL), lambda i, *_: (i, 0))],
        out_specs=pl.BlockSpec(memory_space=pl.ANY),
        scratch_shapes=[pltpu.VMEM((2, TOP_K * tm * TOKEN_TILE_ROWS, LANES), F32), pltpu.SemaphoreType.DMA((2,))],
    )
    return pl.pallas_call(
        functools.partial(_dispatch_body, tm=tm),
        grid_spec=grid_spec,
        out_shape=jax.ShapeDtypeStruct((nrows * TOKEN_TILE_ROWS, LANES), F32),
        compiler_params=_params("arbitrary"),
        name="dispatch",
    )(*tables, loc8, xn)


EXPERT_BLOCKS_PER_STEP = 2


def _expert_body(be_ref, nu_ref, par_ref, nxt_ref, xs_ref, w1_hbm, b1_ref, w2_hbm, b2_ref, ys_ref,
                 w1f_ref, w2f_ref, w1b_ref, w2b_ref, sems):
    block_tiles = MOE_BLOCK * TOKEN_TILE_ROWS

    def fetch(expert, slot):
        return (pltpu.make_async_copy(w1_hbm.at[expert], w1f_ref.at[slot], sems.at[slot, 0]),
                pltpu.make_async_copy(w2_hbm.at[expert], w2f_ref.at[slot], sems.at[slot, 1]))

    for sub in range(EXPERT_BLOCKS_PER_STEP):
        j = pl.program_id(0) * EXPERT_BLOCKS_PER_STEP + sub
        used = j < nu_ref[0]
        jj = jnp.maximum(jnp.minimum(j, nu_ref[0] - 1), 0)
        e = be_ref[jj]
        fresh = jnp.logical_or(j == 0, e != be_ref[jnp.maximum(jj - 1, 0)])
        xs_blk = xs_ref.at[pl.ds(sub * block_tiles, block_tiles)]
        ys_blk = ys_ref.at[pl.ds(sub * block_tiles, block_tiles)]

        @pl.when(jnp.logical_and(used, fresh))
        def _(j=j, jj=jj, e=e):
            slot = par_ref[jj]

            @pl.when(j == 0)
            def _():
                for c in fetch(e, slot):
                    c.start()

            for c in fetch(e, slot):
                c.wait()
            nxt = nxt_ref[jj]

            @pl.when(nxt < N_EXPERTS)
            def _():
                for c in fetch(nxt, 1 - slot):
                    c.start()

            w1b_ref[...] = w1f_ref[slot].astype(BF16)
            w2b_ref[...] = w2f_ref[slot].astype(BF16)

        @pl.when(used)
        def _(e=e, xs_blk=xs_blk, ys_blk=ys_blk):
            xb = _load_token_tiles(xs_blk, MOE_BLOCK).astype(BF16)
            gu = jnp.dot(xb, w1b_ref[...], preferred_element_type=F32) + b1_ref[pl.ds(e, 1), :]
            gate = jnp.minimum(gu[:, :D_FF], SWIGLU_LIMIT)
            lin = jnp.clip(gu[:, D_FF:], -SWIGLU_LIMIT, SWIGLU_LIMIT)
            act = (lin + 1.0) * (gate * jax.nn.sigmoid(SWIGLU_ALPHA * gate))
            ys = jnp.dot(act.astype(BF16), w2b_ref[...], preferred_element_type=F32) + b2_ref[pl.ds(e, 1), :]
            _store_token_tiles(ys_blk, ys)

        @pl.when(jnp.logical_not(used))
        def _(ys_blk=ys_blk):
            ys_blk[...] = jnp.zeros(ys_blk.shape, F32)


def _stage_experts(blk8, xs, w1, b1, w2, b2):
    nrows = xs.shape[0] // TOKEN_TILE_ROWS
    nblk = nrows // MOE_BLOCK
    assert nblk % EXPERT_BLOCKS_PER_STEP == 0
    block_e, nused, parity, nxt = blk8[0, :nblk], blk8[1, :1], blk8[2, :nblk], blk8[3, :nblk]
    tiles = (EXPERT_BLOCKS_PER_STEP * MOE_BLOCK * TOKEN_TILE_ROWS, LANES)
    full = lambda a: pl.BlockSpec(a.shape, lambda j, *_: (0,) * a.ndim)
    grid_spec = pltpu.PrefetchScalarGridSpec(
        num_scalar_prefetch=4,
        grid=(nblk // EXPERT_BLOCKS_PER_STEP,),
        in_specs=[pl.BlockSpec(tiles, lambda j, *_: (j, 0)),
                  pl.BlockSpec(memory_space=pl.ANY), full(b1),
                  pl.BlockSpec(memory_space=pl.ANY), full(b2)],
        out_specs=pl.BlockSpec(tiles, lambda j, *_: (j, 0)),
        scratch_shapes=[pltpu.VMEM((2, D_MODEL, 2 * D_FF), F32), pltpu.VMEM((2, D_FF, D_MODEL), F32),
                        pltpu.VMEM((D_MODEL, 2 * D_FF), BF16), pltpu.VMEM((D_FF, D_MODEL), BF16),
                        pltpu.SemaphoreType.DMA((2, 2))],
    )
    return pl.pallas_call(
        _expert_body,
        grid_spec=grid_spec,
        out_shape=jax.ShapeDtypeStruct((nrows * TOKEN_TILE_ROWS, LANES), F32),
        compiler_params=_params("arbitrary"),
        name="experts",
    )(block_e, nused, parity, nxt, xs, w1, b1, w2, b2)


def _combine_body(tcnt_ref, tcar_ref, pstart_ref, loc_ref, gate_ref, x2_ref, ys_hbm, out_ref,
                  buf_ref, g_ref, sems, *, tm):
    step = pl.program_id(0)
    nloc = TOP_K * tm

    def start_runs(tile, slot):
        def run_copy(local, first, size):
            return pltpu.make_async_copy(_tile_rows(ys_hbm, first, size),
                                         _tile_rows(buf_ref.at[slot], local, size), sems.at[slot])
        _for_each_run(tile, tcnt_ref, tcar_ref, pstart_ref, lambda l, f, s: run_copy(l, f, s).start())

    def wait_buffer(slot):
        pltpu.make_async_copy(ys_hbm.at[pl.ds(0, nloc * TOKEN_TILE_ROWS)], buf_ref.at[slot], sems.at[slot]).wait()

    def combine(slot):
        zpad = jnp.zeros((LANES - 16, LANES), F32)
        lane = lax.broadcasted_iota(I32, (LANES, nloc), 1).astype(F32)
        for c in range(tm // LANES):
            cols_in = slice(slot * tm + c * LANES, slot * tm + (c + 1) * LANES)
            cols = jnp.transpose(jnp.concatenate([loc_ref[:, cols_in].astype(F32), gate_ref[:, cols_in], zpad], axis=0))
            g = jnp.zeros((LANES, nloc), F32)
            for k in range(TOP_K):
                g = jnp.where(lane == cols[:, k:k + 1], cols[:, 8 + k:9 + k], g)
            g_ref[c * LANES:(c + 1) * LANES, :] = g.astype(BF16)
        wait_buffer(slot)
        ys = _load_token_tiles(buf_ref.at[slot], nloc).astype(BF16)
        rows = slice(slot * tm, (slot + 1) * tm)
        out_ref[rows, :] = x2_ref[rows, :] + jnp.dot(g_ref[...], ys, preferred_element_type=F32)

    @pl.when(step == 0)
    def _():
        start_runs(0, 0)

    start_runs(2 * step + 1, 1)
    combine(0)

    @pl.when(step + 1 < pl.num_programs(0))
    def _():
        start_runs(2 * step + 2, 0)

    combine(1)


def _stage_combine(tables, loc8, gate8, x2, ys, tm):
    n = x2.shape[0]
    assert (n // tm) % 2 == 0
    grid_spec = pltpu.PrefetchScalarGridSpec(
        num_scalar_prefetch=3,
        grid=(n // (2 * tm),),
        in_specs=[pl.BlockSpec((8, 2 * tm), lambda i, *_: (0, i)),
                  pl.BlockSpec((8, 2 * tm), lambda i, *_: (0, i)),
                  pl.BlockSpec((2 * tm, D_MODEL), lambda i, *_: (i, 0)),
                  pl.BlockSpec(memory_space=pl.ANY)],
        out_specs=pl.BlockSpec((2 * tm, D_MODEL), lambda i, *_: (i, 0)),
        scratch_shapes=[pltpu.VMEM((2, TOP_K * tm * TOKEN_TILE_ROWS, LANES), F32),
                        pltpu.VMEM((tm, TOP_K * tm), BF16),
                        pltpu.SemaphoreType.DMA((2,))],
    )
    return pl.pallas_call(
        functools.partial(_combine_body, tm=tm),
        grid_spec=grid_spec,
        out_shape=jax.ShapeDtypeStruct((n, D_MODEL), F32),
        compiler_params=_params("arbitrary"),
        name="combine",
    )(*tables, loc8, gate8, x2, ys)


def _moe(x2, xn, loc8, gate8, tcnt, tcar, cnt, w1, b1, w2, b2, tm):
    n = x2.shape[0]
    ntile = n // tm
    nblk = -(-(n * TOP_K) // MOE_BLOCK) + N_EXPERTS
    blk8, pstart, zlo, zhi = _stage_offsets(cnt, nblk)
    per_tile = lambda a: a.reshape(N_EXPERTS, ntile, LANES)[:, :, 0].T
    tables = (per_tile(tcnt), per_tile(tcar), pstart[:, 0])
    xs = _stage_dispatch(tables + (zlo[:, 0], zhi[:, 0]), loc8, xn, nblk * MOE_BLOCK, tm)
    ys = _stage_experts(blk8, xs, w1, b1, w2, b2)
    return _stage_combine(tables, loc8, gate8, x2, ys, tm)


def kernel(x, norm1_g, w_in, mlstm_gate_b, mlstm_norm_g, attn_q_norm_g, attn_k_norm_g, w_mlstm_branch,
           w_attn_branch, w_out, norm2_g, w_router, b_router, w1, b1, w2, b2):
    batch, seq, _ = x.shape
    n = batch * seq
    for l in range(norm1_g.shape[0]):
        x2d = x.reshape(n, D_MODEL)
        tm = min(512, seq)
        mq, kT, mv, so, gi, gf, aq, ak, av, sgm, sga = _stage_inproj(
            x2d, norm1_g[l], w_in[l], mlstm_gate_b[l], attn_q_norm_g[l], attn_k_norm_g[l], batch, seq, tm)
        h_m = _stage_mlstm(mq, kT, mv, so, gi, gf, mlstm_norm_g[l], batch, seq, tm)
        attn = [_stage_attn(aq[g], ak[g], av[g], batch, seq, g)
                for g in range(N_GROUPS)]
        x2, xn, loc8, gate8, tcnt, tcar, cnt = _stage_merge(
            h_m, attn, sgm, sga, x2d, w_mlstm_branch[l], w_attn_branch[l], w_out[l], norm2_g[l],
            w_router[l], b_router[l], batch, seq, tm)
        out = _moe(x2, xn, loc8, gate8, tcnt, tcar, cnt, w1[l], b1[l], w2[l], b2[l], tm)
        x = out.reshape(batch, seq, D_MODEL)
    return x
```

```python
import functools

import numpy as np
import jax
import jax.numpy as jnp
from jax import lax
from jax.experimental import pallas as pl
from jax.experimental.pallas import tpu as pltpu

F32 = jnp.float32
BF16 = jnp.bfloat16
I32 = jnp.int32

D_MODEL = 1024
M_HEADS = 4
M_QK_DIM = 64
M_V_DIM = 128
GATE_SOFTCAP = 15.0
A_HEADS = 4
A_HEAD_DIM = 64
DILATED_PATTERNS = ((128, 1), (512, 4), (2048, 16))
N_GROUPS = len(DILATED_PATTERNS)
N_BACK = 128
N_EXPERTS = 32
TOP_K = 4
D_FF = 1024
SWIGLU_LIMIT = 7.0
SWIGLU_ALPHA = 1.702
MOE_BLOCK = 512
EPS = 1e-6

M_WIDTH = M_HEADS * M_V_DIM
M_QK_WIDTH = M_HEADS * M_QK_DIM
A_WIDTH = A_HEADS * A_HEAD_DIM
IN_SPLITS = (M_QK_WIDTH, M_QK_WIDTH, M_WIDTH, M_WIDTH, 2 * M_HEADS,
             N_GROUPS * A_WIDTH, N_GROUPS * A_WIDTH, N_GROUPS * A_WIDTH, D_MODEL, D_MODEL)

LANES = 128
VMEM_LIMIT = 56 * 1024 * 1024

BF16_EXACT_INT = 256.0

_NT = (((1,), (1,)), ((), ()))
_TN = (((0,), (1,)), ((), ()))


def _alibi_slopes():
    n = N_GROUPS * A_HEADS
    s = np.exp2(-8.0 * np.arange(1, n + 1) / n).astype(np.float32)
    return s.reshape(N_GROUPS, A_HEADS)


def _params(*sem):
    return pltpu.CompilerParams(dimension_semantics=sem, vmem_limit_bytes=VMEM_LIMIT)


def _log_sigmoid(x):
    return jnp.minimum(x, 0.0) - jnp.log1p(jnp.exp(-jnp.abs(x)))


_A_WIDTH = sum(IN_SPLITS[:4])
_B_START = _A_WIDTH + IN_SPLITS[4]


def _piece_segments(widths):
    bounds, start = [], 0
    for width in widths:
        bounds.append((start, start + width))
        start += width
    return bounds


_C_MQ, _C_MK, _C_MV, _C_MO = _piece_segments(IN_SPLITS[:4])
_C_AQ, _C_AK, _C_AV, _C_GM, _C_GA = _piece_segments(IN_SPLITS[5:])


def _split_residues(val, d, out_ref, st_ref):
    t = val.shape[0]
    if d == 1:
        out_ref[0, 0] = val.astype(out_ref.dtype)
        return
    a_ref = st_ref.at[0]
    a_ref[0] = val[:, :LANES]
    a_ref[1] = val[:, LANES:]
    m = t // d
    if d == 16:
        b_ref = st_ref.at[1]
        for r0 in range(4):
            b_ref[0, r0 * 4 * m:(r0 + 1) * 4 * m, :] = a_ref[0, pl.ds(r0, 4 * m, stride=4), :]
            b_ref[1, r0 * 4 * m:(r0 + 1) * 4 * m, :] = a_ref[1, pl.ds(r0, 4 * m, stride=4), :]
        for r0 in range(4):
            for r1 in range(4):
                piece = jnp.concatenate([b_ref[0, pl.ds(r0 * 4 * m + r1, m, stride=4), :],
                                         b_ref[1, pl.ds(r0 * 4 * m + r1, m, stride=4), :]], axis=1)
                out_ref[0, 4 * r1 + r0] = piece.astype(out_ref.dtype)
        return
    for r in range(d):
        piece = jnp.concatenate([a_ref[0, pl.ds(r, m, stride=d), :], a_ref[1, pl.ds(r, m, stride=d), :]], axis=1)
        out_ref[0, r] = piece.astype(out_ref.dtype)


def _merge_residues(ref, d, st_ref):
    if d == 1:
        return ref[0, 0].astype(F32)
    m = ref.shape[2]
    if d == 16:
        a_ref, b_ref = st_ref.at[0], st_ref.at[1]
        for r0 in range(4):
            for r1 in range(4):
                blk = ref[0, 4 * r1 + r0].astype(F32)
                a_ref[0, pl.ds(r0 * 4 * m + r1, m, stride=4), :] = blk[:, :LANES]
                a_ref[1, pl.ds(r0 * 4 * m + r1, m, stride=4), :] = blk[:, LANES:]
        for r0 in range(4):
            b_ref[0, pl.ds(r0, 4 * m, stride=4), :] = a_ref[0, r0 * 4 * m:(r0 + 1) * 4 * m, :]
            b_ref[1, pl.ds(r0, 4 * m, stride=4), :] = a_ref[1, r0 * 4 * m:(r0 + 1) * 4 * m, :]
        return jnp.concatenate([b_ref[0], b_ref[1]], axis=1)
    a_ref = st_ref.at[0]
    for r in range(d):
        blk = ref[0, r].astype(F32)
        a_ref[0, pl.ds(r, m, stride=d), :] = blk[:, :LANES]
        a_ref[1, pl.ds(r, m, stride=d), :] = blk[:, LANES:]
    return jnp.concatenate([a_ref[0], a_ref[1]], axis=1)


def _inproj_body(x_ref, g1_ref, wa_ref, wg_ref, wb_ref, gb_ref, gq_ref, gk_ref,
                 mq_ref, kT_ref, mv_ref, so_ref, gi_ref, gf_ref,
                 q0_ref, q1_ref, q2_ref, k0_ref, k1_ref, k2_ref, v0_ref, v1_ref, v2_ref,
                 sgm_ref, sga_ref, st_ref):
    x = x_ref[...]
    h = x * lax.rsqrt(jnp.mean(x * x, axis=-1, keepdims=True) + EPS) * g1_ref[...]
    hb = h.astype(BF16)

    def seg(w_ref, c):
        return jnp.dot(hb, w_ref[:, c[0]:c[1]], preferred_element_type=F32)

    mq_ref[...] = seg(wa_ref, _C_MQ).astype(BF16)
    mv_ref[...] = seg(wa_ref, _C_MV).astype(BF16)
    so_ref[...] = jax.nn.sigmoid(seg(wa_ref, _C_MO)).astype(BF16)
    hid_r = lax.broadcasted_iota(I32, (A_WIDTH, A_WIDTH), 0) // A_HEAD_DIM
    hid_c = lax.broadcasted_iota(I32, (A_WIDTH, A_WIDTH), 1) // A_HEAD_DIM
    head_ones = (hid_r == hid_c).astype(BF16)
    for c, refs, gain_ref in ((_C_AQ, (q0_ref, q1_ref, q2_ref), gq_ref), (_C_AK, (k0_ref, k1_ref, k2_ref), gk_ref),
                              (_C_AV, (v0_ref, v1_ref, v2_ref), None)):
        val = seg(wb_ref, c)
        for g, ref in enumerate(refs):
            piece = val[:, g * A_WIDTH:(g + 1) * A_WIDTH]
            if gain_ref is not None:
                ss = jnp.dot((piece * piece).astype(BF16), head_ones, preferred_element_type=F32)
                piece = piece * lax.rsqrt(ss * (1.0 / A_HEAD_DIM) + EPS) * gain_ref[:, g * A_WIDTH:(g + 1) * A_WIDTH]
            _split_residues(piece, DILATED_PATTERNS[g][1], ref, st_ref)
    sgm_ref[...] = jax.nn.sigmoid(seg(wb_ref, _C_GM)).astype(BF16)
    sga_ref[...] = jax.nn.sigmoid(seg(wb_ref, _C_GA)).astype(BF16)

    kT_ref[...] = lax.dot_general(wa_ref[:, _C_MK[0]:_C_MK[1]], hb, _TN, preferred_element_type=F32).astype(BF16)
    zg = lax.dot_general(wg_ref[...], hb, _TN, preferred_element_type=F32)
    zi = zg[0:8] + gb_ref[0:8]
    zf = zg[M_HEADS:M_HEADS + 8] + gb_ref[8:16]
    gi_ref[...] = GATE_SOFTCAP * jnp.tanh(zi / GATE_SOFTCAP)
    gf_ref[...] = _log_sigmoid(GATE_SOFTCAP * jnp.tanh(zf / GATE_SOFTCAP))


def _stage_inproj(x2d, norm1_g, w_in, gate_b, gq, gk, batch, seq, tm):
    n = x2d.shape[0]
    steps = seq // tm
    wa = w_in[:, :_A_WIDTH].astype(BF16)
    wg = jnp.pad(w_in[:, _A_WIDTH:_B_START], ((0, 0), (0, LANES - IN_SPLITS[4]))).astype(BF16)
    wb = w_in[:, _B_START:].astype(BF16)
    gb = jnp.zeros((16, 1), F32)
    gb = gb.at[0:4, 0].set(gate_b[:M_HEADS].astype(F32)).at[8:12, 0].set(gate_b[M_HEADS:].astype(F32))
    g1 = norm1_g.astype(F32).reshape(1, D_MODEL)
    gq_t = (jnp.tile(gq.astype(F32), (1, A_HEADS)) * (A_HEAD_DIM ** -0.5)).reshape(1, N_GROUPS * A_WIDTH)
    gk_t = jnp.tile(gk.astype(F32), (1, A_HEADS)).reshape(1, N_GROUPS * A_WIDTH)

    row = lambda w: pl.BlockSpec((tm, w), lambda i: (i, 0))
    rowT = lambda r: pl.BlockSpec((r, tm), lambda i: (0, i))
    full = lambda a: pl.BlockSpec(a.shape, lambda i: (0,) * a.ndim)
    dils = [d for _, d in DILATED_PATTERNS]
    res_shape = lambda d: jax.ShapeDtypeStruct((batch, d, seq // d, A_WIDTH), BF16)
    res_spec = lambda d: pl.BlockSpec((1, d, tm // d, A_WIDTH), lambda i: (i // steps, 0, i % steps, 0))
    out_shapes = (
        jax.ShapeDtypeStruct((n, M_QK_WIDTH), BF16),
        jax.ShapeDtypeStruct((M_QK_WIDTH, n), BF16),
        jax.ShapeDtypeStruct((n, M_WIDTH), BF16),
        jax.ShapeDtypeStruct((n, M_WIDTH), BF16),
        jax.ShapeDtypeStruct((8, n), F32),
        jax.ShapeDtypeStruct((8, n), F32),
        *[res_shape(d) for d in dils], *[res_shape(d) for d in dils], *[res_shape(d) for d in dils],
        jax.ShapeDtypeStruct((n, D_MODEL), BF16),
        jax.ShapeDtypeStruct((n, D_MODEL), BF16),
    )
    out_specs = (row(M_QK_WIDTH), rowT(M_QK_WIDTH), row(M_WIDTH), row(M_WIDTH), rowT(8), rowT(8),
                 *[res_spec(d) for d in dils], *[res_spec(d) for d in dils], *[res_spec(d) for d in dils],
                 row(D_MODEL), row(D_MODEL))
    outs = pl.pallas_call(
        _inproj_body,
        grid=(n // tm,),
        in_specs=[row(D_MODEL), full(g1), full(wa), full(wg), full(wb), full(gb), full(gq_t), full(gk_t)],
        out_specs=out_specs,
        out_shape=out_shapes,
        scratch_shapes=[pltpu.VMEM((2, 2, tm, LANES), F32)],
        compiler_params=_params("parallel"),
        name="inproj",
    )(x2d, g1, wa, wg, wb, gb, gq_t, gk_t)
    mq, kT, mv, so, gi, gf = outs[:6]
    aq, ak, av = outs[6:9], outs[9:12], outs[12:15]
    return mq, kT, mv, so, gi, gf, aq, ak, av, outs[15], outs[16]


M_CHUNK_LEN = 128


def _mlstm_body(q_ref, v_ref, so_ref, ng_ref, *rest, nchunk, nseq):
    kT_refs, gi_refs, gf_refs = rest[0:nseq], rest[nseq:2 * nseq], rest[2 * nseq:3 * nseq]
    o_ref, c_ref, m_ref = rest[3 * nseq:]
    L = M_CHUNK_LEN

    @pl.when(pl.program_id(0) == 0)
    def _():
        c_ref[...] = jnp.zeros_like(c_ref)
        m_ref[...] = jnp.zeros_like(m_ref)

    lane8 = lax.broadcasted_iota(I32, (8, L), 1)
    causal = lax.broadcasted_iota(I32, (L, L), 1) <= lax.broadcasted_iota(I32, (L, L), 0)
    lo_half = lax.broadcasted_iota(I32, (L, LANES), 1) < M_QK_DIM
    ones = jnp.ones((L, M_V_DIM), BF16)

    heads = range(M_HEADS)
    cstate = [[c_ref[s, h * M_QK_DIM:(h + 1) * M_QK_DIM, :] for h in heads] for s in range(nseq)]
    m_prev = [m_ref[s, :, 0:1] for s in range(nseq)]
    units = []
    for c in range(nchunk):
        rows = slice(c * L, (c + 1) * L)
        for s in range(nseq):
            gi = gi_refs[s][:, rows]
            b = gf_refs[s][:, rows]
            sh = 1
            while sh < L:
                b = b + jnp.where(lane8 >= sh, pltpu.roll(b, sh, 1), 0.0)
                sh *= 2
            u = gi - b
            g = b[:, L - 1:L]
            a = g + u
            amax = jnp.max(a, axis=1, keepdims=True)
            m_new = jnp.maximum(g + m_prev[s], amax)
            w = jnp.exp(a - m_new) * (M_QK_DIM ** -0.5)
            s_old = jnp.exp(g + m_prev[s] - m_new)
            vext = [jnp.concatenate([v_ref[s, rows, h * M_V_DIM:(h + 1) * M_V_DIM], ones], axis=1) for h in heads]
            cloc = []
            for h in heads:
                hr = slice(h * M_QK_DIM, (h + 1) * M_QK_DIM)
                kw = (kT_refs[s][hr, rows].astype(F32) * w[h:h + 1, :]).astype(BF16)
                cloc.append(jnp.dot(kw, vext[h], preferred_element_type=F32))
            units.append(dict(seq=s, rows=rows, b=b, u=u, m_prev=m_prev[s], state=cstate[s], vext=vext))
            cstate[s] = [s_old[h:h + 1, :] * cstate[s][h] + cloc[h] for h in heads]
            m_prev[s] = m_new
    for s in range(nseq):
        for h in heads:
            c_ref[s, h * M_QK_DIM:(h + 1) * M_QK_DIM, :] = cstate[s][h]
        m_ref[s] = jnp.broadcast_to(m_prev[s], (8, LANES))

    for un in units:
        s, rows = un["seq"], un["rows"]
        un["s"], un["qc"] = [], []
        for p in range(M_HEADS // 2):
            lanes_p = slice(p * LANES, (p + 1) * LANES)
            q_pair = q_ref[s, rows, lanes_p]
            kT_pair = kT_refs[s][lanes_p, rows]
            c_pair = jnp.concatenate([un["state"][2 * p], un["state"][2 * p + 1]], axis=0).astype(BF16)
            for hh in range(2):
                qm = jnp.where(lo_half if hh == 0 else jnp.logical_not(lo_half), q_pair, jnp.zeros_like(q_pair))
                un["s"].append(jnp.dot(qm, kT_pair, preferred_element_type=F32) * (M_QK_DIM ** -0.5))
                un["qc"].append(jnp.dot(qm, c_pair, preferred_element_type=F32))

    for un in units:
        s, rows, b, u, mp = un["seq"], un["rows"], un["b"], un["u"], un["m_prev"]
        for h in heads:
            hl = slice(h * M_V_DIM, (h + 1) * M_V_DIM)
            bcol = jnp.transpose(jnp.broadcast_to(b[h:h + 1, :], (L, L)))
            dm = jnp.where(causal, bcol + u[h:h + 1, :], -jnp.inf)
            inter = bcol + mp[h:h + 1, :]
            m_t = jnp.maximum(inter, jnp.max(dm, axis=1, keepdims=True))
            pmat = (un["s"][h] * jnp.exp(dm - m_t)).astype(BF16)
            sc = jnp.exp(inter - m_t)
            out = (jnp.dot(pmat, un["vext"][h], preferred_element_type=F32)
                   + jnp.concatenate([sc, sc], axis=1) * un["qc"][h])
            hv = out[:, :M_V_DIM] / jnp.maximum(jnp.abs(out[:, M_V_DIM:]), jnp.exp(-m_t))
            hn = hv * lax.rsqrt(jnp.mean(hv * hv, axis=1, keepdims=True) + EPS)
            hn = hn * ng_ref[:, hl] * so_ref[s, rows, hl].astype(F32)
            o_ref[s, rows, hl] = hn.astype(BF16)


def _stage_mlstm(mq, kT, mv, so, gi, gf, norm_g, batch, seq, rows_per_step):
    n = batch * seq
    R = rows_per_step
    steps = seq // R
    ng = norm_g.astype(F32).reshape(1, M_WIDTH)
    per_seq = lambda a: a.reshape(batch, seq, a.shape[1])
    row = lambda w: pl.BlockSpec((batch, R, w), lambda i: (0, i, 0))
    colT = lambda r, s: pl.BlockSpec((r, R), lambda i, s=s: (0, s * steps + i))
    seqs = range(batch)
    out = pl.pallas_call(
        functools.partial(_mlstm_body, nchunk=R // M_CHUNK_LEN, nseq=batch),
        grid=(steps,),
        in_specs=[row(M_QK_WIDTH), row(M_WIDTH), row(M_WIDTH), pl.BlockSpec((1, M_WIDTH), lambda i: (0, 0)),
                  *[colT(M_QK_WIDTH, s) for s in seqs], *[colT(8, s) for s in seqs], *[colT(8, s) for s in seqs]],
        out_specs=row(M_WIDTH),
        out_shape=jax.ShapeDtypeStruct((batch, seq, M_WIDTH), BF16),
        scratch_shapes=[pltpu.VMEM((batch, M_QK_WIDTH, 2 * M_V_DIM), F32), pltpu.VMEM((batch, 8, LANES), F32)],
        compiler_params=_params("arbitrary"),
        name="mlstm",
    )(per_seq(mq), per_seq(mv), per_seq(so), ng, *[kT] * batch, *[gi] * batch, *[gf] * batch)
    return out.reshape(n, M_WIDTH)


def _attn_body(q_ref, kp_ref, kc_ref, vp_ref, vc_ref, o_ref, lse_ref, *, dil, slopes, lq):
    QB = N_BACK
    first = pl.program_id(2) == 0
    qn = q_ref[0, 0]
    kcn = kc_ref[0, 0]
    kpn = kp_ref[0, 0]
    vc = vc_ref[0, 0]
    vp = vp_ref[0, 0]

    qi = lax.broadcasted_iota(I32, (QB, 2 * QB), 0)
    kj = lax.broadcasted_iota(I32, (QB, 2 * QB), 1)
    dist = qi + QB - kj
    band = jnp.logical_and(dist >= 0, dist <= N_BACK)
    distf = (dist * dil).astype(F32)
    bias = [jnp.where(band, -float(slopes[h]) * distf, -jnp.inf) for h in range(A_HEADS)]
    no_prev = jnp.logical_and(first, kj < QB)
    lo_half = lax.broadcasted_iota(I32, (QB, LANES), 1) < A_HEAD_DIM
    ones = jnp.ones((2 * QB, LANES), BF16)

    units = []
    for j in range(lq // QB):
        rows = slice(j * QB, (j + 1) * QB)
        prow = slice((j - 1) * QB, j * QB)
        keys = jnp.concatenate([kpn if j == 0 else kcn[prow], kcn[rows]], axis=0)
        vals = jnp.concatenate([vp if j == 0 else vc[prow], vc[rows]], axis=0)
        for p in range(A_HEADS // 2):
            lanes_p = slice(p * LANES, (p + 1) * LANES)
            q_pair = qn[rows, lanes_p]
            k_pair = keys[:, lanes_p]
            vext = jnp.concatenate([vals[:, lanes_p], ones], axis=1)
            scores = []
            for hh in range(2):
                sel = lo_half if hh == 0 else jnp.logical_not(lo_half)
                qm = jnp.where(sel, q_pair, jnp.zeros_like(q_pair))
                scores.append(lax.dot_general(qm, k_pair, _NT, preferred_element_type=F32))
            units.append((j, rows, lanes_p, p, vext, scores))

    for j, rows, lanes_p, p, vext, scores in units:
        o_pair = None
        l_pair = None
        for hh in range(2):
            s = scores[hh] + bias[2 * p + hh]
            if j == 0:
                s = jnp.where(no_prev, -jnp.inf, s)
            m = jnp.max(s, axis=1, keepdims=True)
            pv = jnp.dot(jnp.exp(s - m).astype(BF16), vext, preferred_element_type=F32)
            den = pv[:, LANES:]
            o_h = pv[:, :LANES] / den
            l_h = m + jnp.log(den)
            o_pair = o_h if hh == 0 else jnp.where(lo_half, o_pair, o_h)
            l_pair = l_h if hh == 0 else jnp.where(lo_half, l_pair, l_h)
        o_ref[0, 0, rows, lanes_p] = o_pair.astype(BF16)
        lse_ref[0, 0, rows, lanes_p] = l_pair


def _stage_attn(aq, ak, av, batch, seq, group):
    _, dil = DILATED_PATTERNS[group]
    L = seq // dil
    assert L % N_BACK == 0
    lq = min(2048, L)
    nq = L // lq
    sub = lq // N_BACK
    cur = pl.BlockSpec((1, 1, lq, A_WIDTH), lambda b, r, i: (b, r, i, 0))
    prev = pl.BlockSpec((1, 1, N_BACK, A_WIDTH), lambda b, r, i: (b, r, jnp.maximum(i * sub - 1, 0), 0))
    return pl.pallas_call(
        functools.partial(_attn_body, dil=dil, slopes=tuple(_alibi_slopes()[group]), lq=lq),
        grid=(batch, dil, nq),
        in_specs=[cur, prev, cur, prev, cur],
        out_specs=(cur, cur),
        out_shape=(jax.ShapeDtypeStruct((batch, dil, L, A_WIDTH), BF16),
                   jax.ShapeDtypeStruct((batch, dil, L, A_WIDTH), F32)),
        compiler_params=_params("parallel", "parallel", "parallel"),
        name=f"dilated_attn_d{dil}",
    )(aq, ak, ak, av, av)


PACK_ROWS = D_MODEL // (2 * LANES)
U32 = jnp.uint32
_HIGH_HALF = 0xFFFF0000


def _pack_rows(val):
    half = D_MODEL // 2

    def bits(v):
        return lax.bitcast_convert_type(v.astype(BF16).astype(F32), U32)

    return (bits(val[:, :half]) >> 16) | (bits(val[:, half:]) & U32(_HIGH_HALF))


def _unpack_rows(words):
    lo = lax.bitcast_convert_type(words << 16, F32).astype(BF16)
    hi = lax.bitcast_convert_type(words & U32(_HIGH_HALF), F32).astype(BF16)
    return jnp.concatenate([lo, hi], axis=1)


def _flat(ref):
    rows = 1
    for d in ref.shape[:-2]:
        rows *= d
    return ref.reshape(rows * PACK_ROWS, LANES)


def _store_packed(flat_ref, row0, words):
    t = words.shape[0]
    for s in range(PACK_ROWS):
        flat_ref[pl.ds(row0 * PACK_ROWS + s, t, stride=PACK_ROWS), :] = words[:, s * LANES:(s + 1) * LANES]


def _load_packed(flat_ref, row0, t):
    return jnp.concatenate([flat_ref[pl.ds(row0 * PACK_ROWS + s, t, stride=PACK_ROWS), :] for s in range(PACK_ROWS)],
                           axis=1)


def _rows8(vals):
    t = vals[0].shape[1]
    rid = lax.broadcasted_iota(I32, (8, t), 0)
    out = jnp.zeros((8, t), vals[0].dtype)
    for k, v in enumerate(vals):
        out = jnp.where(rid == k, jnp.broadcast_to(v, (8, t)), out)
    return out


def _merge_body(hm_ref, o1_ref, o2_ref, o3_ref, l1_ref, l2_ref, l3_ref, sgm_ref, sga_ref, x_ref,
                wm_ref, wa_ref, wo_ref, g2_ref, wrh_ref, br_ref,
                x2_ref, xn_ref, loc_ref, gate_ref, tcnt_ref, tcar_ref, cnt_ref, carry_ref, st_ref):
    @pl.when(pl.program_id(0) == 0)
    def _():
        carry_ref[...] = jnp.zeros_like(carry_ref)

    m_branch = jnp.dot(hm_ref[...], wm_ref[...], preferred_element_type=F32)
    dils = [d for _, d in DILATED_PATTERNS]
    l1, l2, l3 = [_merge_residues(r, d, st_ref) for r, d in zip((l1_ref, l2_ref, l3_ref), dils)]
    lmax = jnp.maximum(jnp.maximum(l1, l2), l3)
    e1, e2, e3 = jnp.exp(l1 - lmax), jnp.exp(l2 - lmax), jnp.exp(l3 - lmax)
    num = e1 * _merge_residues(o1_ref, dils[0], st_ref)
    num = num + e2 * _merge_residues(o2_ref, dils[1], st_ref)
    num = num + e3 * _merge_residues(o3_ref, dils[2], st_ref)
    h_a = num / (e1 + e2 + e3)
    y = (sgm_ref[...].astype(F32) * m_branch
         + sga_ref[...].astype(F32) * jnp.dot(h_a.astype(BF16), wa_ref[...], preferred_element_type=F32))
    x2 = x_ref[...] + jnp.dot(y.astype(BF16), wo_ref[...], preferred_element_type=F32)
    x2_ref[...] = x2
    xn = x2 * lax.rsqrt(jnp.mean(x2 * x2, axis=-1, keepdims=True) + EPS) * g2_ref[...]
    xh = xn.astype(BF16)
    xn_ref[...] = xh

    logits = lax.dot_general(wrh_ref[...], xh, _NT, preferred_element_type=F32) + br_ref[...]
    t = logits.shape[1]
    eid = lax.broadcasted_iota(I32, (N_EXPERTS, t), 0).astype(F32)
    vals = logits
    top_v, top_i = [], []
    for _ in range(TOP_K):
        mx = jnp.max(vals, axis=0, keepdims=True)
        ik = jnp.min(jnp.where(vals == mx, eid, float(N_EXPERTS)), axis=0, keepdims=True)
        top_v.append(mx)
        top_i.append(ik)
        vals = jnp.where(eid == ik, -jnp.inf, vals)
    ex = [jnp.exp(v - top_v[0]) for v in top_v]
    den = ex[0] + ex[1] + ex[2] + ex[3]
    gate_ref[...] = _rows8([e / den for e in ex])

    chosen = jnp.zeros((N_EXPERTS, t), F32)
    for ik in top_i:
        chosen = chosen + (eid == ik).astype(F32)
    before = (lax.broadcasted_iota(I32, (t, t), 0) < lax.broadcasted_iota(I32, (t, t), 1)).astype(BF16)
    prefix = jnp.dot(chosen.astype(BF16), before, preferred_element_type=F32)
    tcount = jnp.broadcast_to(jnp.sum(chosen, axis=1, keepdims=True), (N_EXPERTS, LANES))
    below = (lax.broadcasted_iota(I32, (N_EXPERTS, N_EXPERTS), 1)
             < lax.broadcasted_iota(I32, (N_EXPERTS, N_EXPERTS), 0)).astype(BF16)
    t_hi = jnp.floor(tcount * (1.0 / BF16_EXACT_INT)) * BF16_EXACT_INT
    tile_off = (jnp.dot(below, t_hi.astype(BF16), preferred_element_type=F32)
                + jnp.dot(below, (tcount - t_hi).astype(BF16), preferred_element_type=F32))
    pos = prefix + tile_off[:, 0:1]
    loc_ref[...] = _rows8([jnp.sum(jnp.where(eid == ik, pos, 0.0), axis=0, keepdims=True).astype(I32)
                           for ik in top_i])
    carry = carry_ref[...]
    tcnt_ref[...] = tcount.astype(I32)
    tcar_ref[...] = carry.astype(I32)
    total = carry + tcount
    carry_ref[...] = total
    cnt_ref[...] = total


def _stage_merge(h_m, attn, sgm, sga, x2d, w_mb, w_ab, w_out, norm2_g, w_router, b_router, batch, seq, tm):
    n = x2d.shape[0]
    steps = seq // tm
    (o1, l1), (o2, l2), (o3, l3) = attn
    wm = w_mb.astype(BF16)
    wa = w_ab.astype(BF16)
    wo = w_out.astype(BF16)
    g2 = norm2_g.astype(F32).reshape(1, D_MODEL)
    wrh = w_router.astype(BF16).T
    br = b_router.astype(F32).reshape(N_EXPERTS, 1)
    row = lambda w: pl.BlockSpec((tm, w), lambda i: (i, 0))
    rowT = lambda r: pl.BlockSpec((r, tm), lambda i: (0, i))
    full = lambda a: pl.BlockSpec(a.shape, lambda i: (0,) * a.ndim)
    res = lambda d: pl.BlockSpec((1, d, tm // d, A_WIDTH), lambda i: (i // steps, 0, i % steps, 0))
    dils = [d for _, d in DILATED_PATTERNS]
    per_tile = pl.BlockSpec((N_EXPERTS, LANES), lambda i: (0, i))
    return pl.pallas_call(
        _merge_body,
        grid=(n // tm,),
        in_specs=[row(M_WIDTH), *[res(d) for d in dils], *[res(d) for d in dils],
                  row(D_MODEL), row(D_MODEL), row(D_MODEL),
                  full(wm), full(wa), full(wo), full(g2), full(wrh), full(br)],
        out_specs=(row(D_MODEL), row(D_MODEL), rowT(8), rowT(8), per_tile, per_tile,
                   pl.BlockSpec((N_EXPERTS, LANES), lambda i: (0, 0))),
        out_shape=(jax.ShapeDtypeStruct((n, D_MODEL), F32),
                   jax.ShapeDtypeStruct((n, D_MODEL), BF16),
                   jax.ShapeDtypeStruct((8, n), I32),
                   jax.ShapeDtypeStruct((8, n), F32),
                   jax.ShapeDtypeStruct((N_EXPERTS, (n // tm) * LANES), I32),
                   jax.ShapeDtypeStruct((N_EXPERTS, (n // tm) * LANES), I32),
                   jax.ShapeDtypeStruct((N_EXPERTS, LANES), F32)),
        scratch_shapes=[pltpu.VMEM((N_EXPERTS, LANES), F32), pltpu.VMEM((2, 2, tm, LANES), F32)],
        compiler_params=_params("arbitrary"),
        name="merge_route",
    )(h_m, o1, o2, o3, l1, l2, l3, sgm, sga, x2d, wm, wa, wo, g2, wrh, br)


def _offsets_body(cnt_ref, blk_ref, pstart_ref, zlo_ref, zhi_ref, *, nblk_pad):
    cnt = cnt_ref[...]
    padded = jnp.floor((cnt + (MOE_BLOCK - 1)) * (1.0 / MOE_BLOCK)) * MOE_BLOCK
    lower = (lax.broadcasted_iota(I32, (N_EXPERTS, N_EXPERTS), 1)
             <= lax.broadcasted_iota(I32, (N_EXPERTS, N_EXPERTS), 0)).astype(BF16)
    nb = padded * (1.0 / MOE_BLOCK)
    nb_hi = jnp.floor(nb * (1.0 / BF16_EXACT_INT)) * BF16_EXACT_INT
    pends = (jnp.dot(lower, nb_hi.astype(BF16), preferred_element_type=F32)
             + jnp.dot(lower, (nb - nb_hi).astype(BF16), preferred_element_type=F32)) * MOE_BLOCK
    pstart = pends - padded
    pstart_ref[...] = pstart.astype(I32)
    zlo_ref[...] = (pstart + cnt).astype(I32)
    zhi_ref[...] = pends.astype(I32)

    first_row = (lax.broadcasted_iota(I32, (N_EXPERTS, nblk_pad), 1) * MOE_BLOCK).astype(F32)
    pe = jnp.broadcast_to(pends[:, 0:1], (N_EXPERTS, nblk_pad))
    be = jnp.sum((pe <= first_row).astype(F32), axis=0, keepdims=True)
    be = jnp.minimum(be, float(N_EXPERTS - 1))
    nused = pends[N_EXPERTS - 1:N_EXPERTS, 0:1] * (1.0 / MOE_BLOCK)
    nonempty = jnp.broadcast_to(padded[:, 0:1], (N_EXPERTS, nblk_pad)) > 0.0
    runidx = jnp.sum(jnp.logical_and(pe <= first_row, nonempty).astype(F32), axis=0, keepdims=True)
    parity = runidx - 2.0 * jnp.floor(runidx * 0.5)
    eid = lax.broadcasted_iota(I32, (N_EXPERTS, nblk_pad), 0).astype(F32)
    later = jnp.logical_and(eid > be, nonempty)
    nxt = jnp.min(jnp.where(later, eid, float(N_EXPERTS)), axis=0, keepdims=True)
    blk_ref[...] = _rows8([be.astype(I32), jnp.broadcast_to(nused, (1, nblk_pad)).astype(I32),
                           parity.astype(I32), nxt.astype(I32)])


def _stage_offsets(cnt, nblk):
    nblk_pad = -(-nblk // LANES) * LANES
    const = lambda r, c: pl.BlockSpec((r, c), lambda i: (0, 0))
    per_expert = jax.ShapeDtypeStruct((N_EXPERTS, LANES), I32)
    return pl.pallas_call(
        functools.partial(_offsets_body, nblk_pad=nblk_pad),
        grid=(1,),
        in_specs=[const(N_EXPERTS, LANES)],
        out_specs=(const(8, nblk_pad), const(N_EXPERTS, LANES), const(N_EXPERTS, LANES), const(N_EXPERTS, LANES)),
        out_shape=(jax.ShapeDtypeStruct((8, nblk_pad), I32), per_expert, per_expert, per_expert),
        compiler_params=_params("arbitrary"),
        name="route_offsets",
    )(cnt)


RUN_BITS = 10


def _tile_rows(ref, first_row, nrows):
    return ref.at[pl.ds(first_row, nrows)]


def _for_each_piece(length, fn):
    for b in reversed(range(RUN_BITS)):
        @pl.when(((length >> b) & 1) == 1)
        def _(b=b):
            fn((length >> (b + 1)) << (b + 1), 1 << b)


def _for_each_run(tile, tcnt_ref, tcar_ref, pstart_ref, fn):
    def per_expert(e, local):
        count = tcnt_ref[tile, e]
        first = pstart_ref[e] + tcar_ref[tile, e]
        _for_each_piece(count, lambda off, size: fn(local + off, first + off, size))
        return local + count

    lax.fori_loop(0, N_EXPERTS, per_expert, 0)


PERM_CHUNK = 256


def _dispatch_body(tcnt_ref, tcar_ref, pstart_ref, zlo_ref, zhi_ref, loc_ref, xn_ref, xs_hbm,
                   buf_ref, sems, *, tm):
    step = pl.program_id(0)
    nloc = TOP_K * tm

    def wait_buffer(slot):
        pltpu.make_async_copy(buf_ref.at[slot], xs_hbm.at[pl.ds(0, nloc)], sems.at[slot]).wait()

    for slot in range(2):
        tile = 2 * step + slot

        @pl.when(step > 0)
        def _(slot=slot):
            wait_buffer(slot)

        loc = loc_ref[:, slot * tm:(slot + 1) * tm]
        xn = xn_ref[slot * tm:(slot + 1) * tm, :]
        for c in range(TOP_K * tm // PERM_CHUNK):
            lid = lax.broadcasted_iota(I32, (PERM_CHUNK, tm), 0) + c * PERM_CHUNK
            hit = lid == loc[0:1, :]
            for k in range(1, TOP_K):
                hit = jnp.logical_or(hit, lid == loc[k:k + 1, :])
            rows = jnp.dot(jnp.where(hit, 1.0, 0.0).astype(BF16), xn, preferred_element_type=F32)
            _store_packed(_flat(buf_ref), slot * nloc + c * PERM_CHUNK, _pack_rows(rows))

        def run_copy(local, first, size, slot=slot):
            return pltpu.make_async_copy(_tile_rows(buf_ref.at[slot], local, size),
                                         _tile_rows(xs_hbm, first, size), sems.at[slot])

        _for_each_run(tile, tcnt_ref, tcar_ref, pstart_ref, lambda l, f, s: run_copy(l, f, s).start())

    @pl.when(step == pl.num_programs(0) - 1)
    def _():
        wait_buffer(0)
        wait_buffer(1)
        zsrc = buf_ref.at[0]
        zsrc[pl.ds(0, MOE_BLOCK)] = jnp.zeros((MOE_BLOCK, PACK_ROWS, LANES), U32)

        def zero_copy(first, size):
            return pltpu.make_async_copy(_tile_rows(zsrc, 0, size), _tile_rows(xs_hbm, first, size), sems.at[0])

        def per_expert(e, carry):
            lo = zlo_ref[e]
            npad = zhi_ref[e] - lo
            _for_each_piece(npad, lambda off, size: zero_copy(lo + off, size).start())
            _for_each_piece(npad, lambda off, size: zero_copy(lo + off, size).wait())
            return carry

        lax.fori_loop(0, N_EXPERTS, per_expert, 0)

        first_unused = zhi_ref[N_EXPERTS - 1] // MOE_BLOCK
        nblk = xs_hbm.shape[0] // MOE_BLOCK

        def tail(blk, carry):
            zero_copy(blk * MOE_BLOCK, MOE_BLOCK).start()
            zero_copy(blk * MOE_BLOCK, MOE_BLOCK).wait()
            return carry

        lax.fori_loop(first_unused, nblk, tail, 0)


def _stage_dispatch(tables, loc8, xn, nrows, tm):
    n = xn.shape[0]
    assert TOP_K * tm >= MOE_BLOCK and (n // tm) % 2 == 0
    grid_spec = pltpu.PrefetchScalarGridSpec(
        num_scalar_prefetch=5,
        grid=(n // (2 * tm),),
        in_specs=[pl.BlockSpec((8, 2 * tm), lambda i, *_: (0, i)),
                  pl.BlockSpec((2 * tm, D_MODEL), lambda i, *_: (i, 0))],
        out_specs=pl.BlockSpec(memory_space=pl.ANY),
        scratch_shapes=[pltpu.VMEM((2, TOP_K * tm, PACK_ROWS, LANES), U32), pltpu.SemaphoreType.DMA((2,))],
    )
    return pl.pallas_call(
        functools.partial(_dispatch_body, tm=tm),
        grid_spec=grid_spec,
        out_shape=jax.ShapeDtypeStruct((nrows, PACK_ROWS, LANES), U32),
        compiler_params=_params("arbitrary"),
        name="dispatch",
    )(*tables, loc8, xn)


EXPERT_BLOCKS_PER_STEP = 2


def _expert_body(be_ref, nu_ref, par_ref, nxt_ref, xs_ref, w1_hbm, b1_ref, w2_hbm, b2_ref, ys_ref,
                 w1f_ref, w2f_ref, w1b_ref, w2b_ref, sems):
    def fetch(expert, slot):
        return (pltpu.make_async_copy(w1_hbm.at[expert], w1f_ref.at[slot], sems.at[slot, 0]),
                pltpu.make_async_copy(w2_hbm.at[expert], w2f_ref.at[slot], sems.at[slot, 1]))

    for sub in range(EXPERT_BLOCKS_PER_STEP):
        j = pl.program_id(0) * EXPERT_BLOCKS_PER_STEP + sub
        used = j < nu_ref[0]
        jj = jnp.maximum(jnp.minimum(j, nu_ref[0] - 1), 0)
        e = be_ref[jj]
        fresh = jnp.logical_or(j == 0, e != be_ref[jnp.maximum(jj - 1, 0)])

        @pl.when(jnp.logical_and(used, fresh))
        def _(j=j, jj=jj, e=e):
            slot = par_ref[jj]

            @pl.when(j == 0)
            def _():
                for c in fetch(e, slot):
                    c.start()

            for c in fetch(e, slot):
                c.wait()
            nxt = nxt_ref[jj]

            @pl.when(nxt < N_EXPERTS)
            def _():
                for c in fetch(nxt, 1 - slot):
                    c.start()

            w1b_ref[...] = w1f_ref[slot].astype(BF16)
            w2b_ref[...] = w2f_ref[slot].astype(BF16)

        @pl.when(used)
        def _(e=e, sub=sub):
            xb = _unpack_rows(_load_packed(_flat(xs_ref), sub * MOE_BLOCK, MOE_BLOCK))
            gu = jnp.dot(xb, w1b_ref[...], preferred_element_type=F32) + b1_ref[pl.ds(e, 1), :]
            gate = jnp.minimum(gu[:, :D_FF], SWIGLU_LIMIT)
            lin = jnp.clip(gu[:, D_FF:], -SWIGLU_LIMIT, SWIGLU_LIMIT)
            act = (lin + 1.0) * (gate * jax.nn.sigmoid(SWIGLU_ALPHA * gate))
            ys = jnp.dot(act.astype(BF16), w2b_ref[...], preferred_element_type=F32) + b2_ref[pl.ds(e, 1), :]
            _store_packed(_flat(ys_ref), sub * MOE_BLOCK, _pack_rows(ys))

        @pl.when(jnp.logical_not(used))
        def _(sub=sub):
            ys_ref[pl.ds(sub * MOE_BLOCK, MOE_BLOCK)] = jnp.zeros((MOE_BLOCK, PACK_ROWS, LANES), U32)


def _stage_experts(blk8, xs, w1, b1, w2, b2):
    nrows = xs.shape[0]
    nblk = nrows // MOE_BLOCK
    assert nblk % EXPERT_BLOCKS_PER_STEP == 0
    block_e, nused, parity, nxt = blk8[0, :nblk], blk8[1, :1], blk8[2, :nblk], blk8[3, :nblk]
    tiles = (EXPERT_BLOCKS_PER_STEP * MOE_BLOCK, PACK_ROWS, LANES)
    full = lambda a: pl.BlockSpec(a.shape, lambda j, *_: (0,) * a.ndim)
    grid_spec = pltpu.PrefetchScalarGridSpec(
        num_scalar_prefetch=4,
        grid=(nblk // EXPERT_BLOCKS_PER_STEP,),
        in_specs=[pl.BlockSpec(tiles, lambda j, *_: (j, 0, 0)),
                  pl.BlockSpec(memory_space=pl.ANY), full(b1),
                  pl.BlockSpec(memory_space=pl.ANY), full(b2)],
        out_specs=pl.BlockSpec(tiles, lambda j, *_: (j, 0, 0)),
        scratch_shapes=[pltpu.VMEM((2, D_MODEL, 2 * D_FF), F32), pltpu.VMEM((2, D_FF, D_MODEL), F32),
                        pltpu.VMEM((D_MODEL, 2 * D_FF), BF16), pltpu.VMEM((D_FF, D_MODEL), BF16),
                        pltpu.SemaphoreType.DMA((2, 2))],
    )
    return pl.pallas_call(
        _expert_body,
        grid_spec=grid_spec,
        out_shape=jax.ShapeDtypeStruct((nrows, PACK_ROWS, LANES), U32),
        compiler_params=_params("arbitrary"),
        name="experts",
    )(block_e, nused, parity, nxt, xs, w1, b1, w2, b2)


def _combine_body(tcnt_ref, tcar_ref, pstart_ref, loc_ref, gate_ref, x2_ref, ys_hbm, out_ref,
                  buf_ref, g_ref, sems, *, tm):
    step = pl.program_id(0)
    nloc = TOP_K * tm

    def start_runs(tile, slot):
        def run_copy(local, first, size):
            return pltpu.make_async_copy(_tile_rows(ys_hbm, first, size),
                                         _tile_rows(buf_ref.at[slot], local, size), sems.at[slot])
        _for_each_run(tile, tcnt_ref, tcar_ref, pstart_ref, lambda l, f, s: run_copy(l, f, s).start())

    def wait_buffer(slot):
        pltpu.make_async_copy(ys_hbm.at[pl.ds(0, nloc)], buf_ref.at[slot], sems.at[slot]).wait()

    def combine(slot):
        zpad = jnp.zeros((LANES - 16, LANES), F32)
        lane = lax.broadcasted_iota(I32, (LANES, nloc), 1).astype(F32)
        for c in range(tm // LANES):
            cols_in = slice(slot * tm + c * LANES, slot * tm + (c + 1) * LANES)
            cols = jnp.transpose(jnp.concatenate([loc_ref[:, cols_in].astype(F32), gate_ref[:, cols_in], zpad], axis=0))
            g = jnp.zeros((LANES, nloc), F32)
            for k in range(TOP_K):
                g = jnp.where(lane == cols[:, k:k + 1], cols[:, 8 + k:9 + k], g)
            g_ref[c * LANES:(c + 1) * LANES, :] = g.astype(BF16)
        wait_buffer(slot)
        ys = _unpack_rows(_load_packed(_flat(buf_ref), slot * nloc, nloc))
        rows = slice(slot * tm, (slot + 1) * tm)
        out_ref[rows, :] = x2_ref[rows, :] + jnp.dot(g_ref[...], ys, preferred_element_type=F32)

    @pl.when(step == 0)
    def _():
        start_runs(0, 0)

    start_runs(2 * step + 1, 1)
    combine(0)

    @pl.when(step + 1 < pl.num_programs(0))
    def _():
        start_runs(2 * step + 2, 0)

    combine(1)


def _stage_combine(tables, loc8, gate8, x2, ys, tm):
    n = x2.shape[0]
    assert (n // tm) % 2 == 0
    grid_spec = pltpu.PrefetchScalarGridSpec(
        num_scalar_prefetch=3,
        grid=(n // (2 * tm),),
        in_specs=[pl.BlockSpec((8, 2 * tm), lambda i, *_: (0, i)),
                  pl.BlockSpec((8, 2 * tm), lambda i, *_: (0, i)),
                  pl.BlockSpec((2 * tm, D_MODEL), lambda i, *_: (i, 0)),
                  pl.BlockSpec(memory_space=pl.ANY)],
        out_specs=pl.BlockSpec((2 * tm, D_MODEL), lambda i, *_: (i, 0)),
        scratch_shapes=[pltpu.VMEM((2, TOP_K * tm, PACK_ROWS, LANES), U32),
                        pltpu.VMEM((tm, TOP_K * tm), BF16),
                        pltpu.SemaphoreType.DMA((2,))],
    )
    return pl.pallas_call(
        functools.partial(_combine_body, tm=tm),
        grid_spec=grid_spec,
        out_shape=jax.ShapeDtypeStruct((n, D_MODEL), F32),
        compiler_params=_params("arbitrary"),
        name="combine",
    )(*tables, loc8, gate8, x2, ys)


def _moe(x2, xn, loc8, gate8, tcnt, tcar, cnt, w1, b1, w2, b2, tm):
    n = x2.shape[0]
    ntile = n // tm
    nblk = -(-(n * TOP_K) // MOE_BLOCK) + N_EXPERTS
    blk8, pstart, zlo, zhi = _stage_offsets(cnt, nblk)
    per_tile = lambda a: a.reshape(N_EXPERTS, ntile, LANES)[:, :, 0].T
    tables = (per_tile(tcnt), per_tile(tcar), pstart[:, 0])
    xs = _stage_dispatch(tables + (zlo[:, 0], zhi[:, 0]), loc8, xn, nblk * MOE_BLOCK, tm)
    ys = _stage_experts(blk8, xs, w1, b1, w2, b2)
    return _stage_combine(tables, loc8, gate8, x2, ys, tm)


def kernel(x, norm1_g, w_in, mlstm_gate_b, mlstm_norm_g, attn_q_norm_g, attn_k_norm_g, w_mlstm_branch,
           w_attn_branch, w_out, norm2_g, w_router, b_router, w1, b1, w2, b2):
    batch, seq, _ = x.shape
    n = batch * seq
    for l in range(norm1_g.shape[0]):
        x2d = x.reshape(n, D_MODEL)
        tm = min(512, seq)
        mq, kT, mv, so, gi, gf, aq, ak, av, sgm, sga = _stage_inproj(
            x2d, norm1_g[l], w_in[l], mlstm_gate_b[l], attn_q_norm_g[l], attn_k_norm_g[l], batch, seq, tm)
        h_m = _stage_mlstm(mq, kT, mv, so, gi, gf, mlstm_norm_g[l], batch, seq, tm)
        attn = [_stage_attn(aq[g], ak[g], av[g], batch, seq, g)
                for g in range(N_GROUPS)]
        x2, xn, loc8, gate8, tcnt, tcar, cnt = _stage_merge(
            h_m, attn, sgm, sga, x2d, w_mlstm_branch[l], w_attn_branch[l], w_out[l], norm2_g[l],
            w_router[l], b_router[l], batch, seq, tm)
        out = _moe(x2, xn, loc8, gate8, tcnt, tcar, cnt, w1[l], b1[l], w2[l], b2[l], tm)
        x = out.reshape(batch, seq, D_MODEL)
    return x
```

```python
import functools

import numpy as np
import jax
import jax.numpy as jnp
from jax import lax
from jax.experimental import pallas as pl
from jax.experimental.pallas import tpu as pltpu

F32 = jnp.float32
BF16 = jnp.bfloat16
I32 = jnp.int32

D_MODEL = 1024
M_HEADS = 4
M_QK_DIM = 64
M_V_DIM = 128
GATE_SOFTCAP = 15.0
A_HEADS = 4
A_HEAD_DIM = 64
DILATED_PATTERNS = ((128, 1), (512, 4), (2048, 16))
N_GROUPS = len(DILATED_PATTERNS)
N_BACK = 128
N_EXPERTS = 32
TOP_K = 4
D_FF = 1024
SWIGLU_LIMIT = 7.0
SWIGLU_ALPHA = 1.702
MOE_BLOCK = 512
EPS = 1e-6

M_WIDTH = M_HEADS * M_V_DIM
M_QK_WIDTH = M_HEADS * M_QK_DIM
A_WIDTH = A_HEADS * A_HEAD_DIM
IN_SPLITS = (M_QK_WIDTH, M_QK_WIDTH, M_WIDTH, M_WIDTH, 2 * M_HEADS,
             N_GROUPS * A_WIDTH, N_GROUPS * A_WIDTH, N_GROUPS * A_WIDTH, D_MODEL, D_MODEL)

LANES = 128
VMEM_LIMIT = 56 * 1024 * 1024

BF16_EXACT_INT = 256.0

_NT = (((1,), (1,)), ((), ()))
_TN = (((0,), (1,)), ((), ()))


def _alibi_slopes():
    n = N_GROUPS * A_HEADS
    s = np.exp2(-8.0 * np.arange(1, n + 1) / n).astype(np.float32)
    return s.reshape(N_GROUPS, A_HEADS)


def _params(*sem):
    return pltpu.CompilerParams(dimension_semantics=sem, vmem_limit_bytes=VMEM_LIMIT)


def _log_sigmoid(x):
    return jnp.minimum(x, 0.0) - jnp.log1p(jnp.exp(-jnp.abs(x)))


_A_WIDTH = sum(IN_SPLITS[:4])
_B_START = _A_WIDTH + IN_SPLITS[4]


def _piece_segments(widths):
    bounds, start = [], 0
    for width in widths:
        bounds.append((start, start + width))
        start += width
    return bounds


_C_MQ, _C_MK, _C_MV, _C_MO = _piece_segments(IN_SPLITS[:4])
_C_AQ, _C_AK, _C_AV, _C_GM, _C_GA = _piece_segments(IN_SPLITS[5:])


def _split_residues(val, d, out_ref, st_ref):
    t = val.shape[0]
    if d == 1:
        out_ref[0, 0] = val.astype(out_ref.dtype)
        return
    a_ref = st_ref.at[0]
    a_ref[0] = val[:, :LANES]
    a_ref[1] = val[:, LANES:]
    m = t // d
    if d == 16:
        b_ref = st_ref.at[1]
        for r0 in range(4):
            b_ref[0, r0 * 4 * m:(r0 + 1) * 4 * m, :] = a_ref[0, pl.ds(r0, 4 * m, stride=4), :]
            b_ref[1, r0 * 4 * m:(r0 + 1) * 4 * m, :] = a_ref[1, pl.ds(r0, 4 * m, stride=4), :]
        for r0 in range(4):
            for r1 in range(4):
                piece = jnp.concatenate([b_ref[0, pl.ds(r0 * 4 * m + r1, m, stride=4), :],
                                         b_ref[1, pl.ds(r0 * 4 * m + r1, m, stride=4), :]], axis=1)
                out_ref[0, 4 * r1 + r0] = piece.astype(out_ref.dtype)
        return
    for r in range(d):
        piece = jnp.concatenate([a_ref[0, pl.ds(r, m, stride=d), :], a_ref[1, pl.ds(r, m, stride=d), :]], axis=1)
        out_ref[0, r] = piece.astype(out_ref.dtype)


def _merge_residues(ref, d, st_ref):
    if d == 1:
        return ref[0, 0].astype(F32)
    m = ref.shape[2]
    if d == 16:
        a_ref, b_ref = st_ref.at[0], st_ref.at[1]
        for r0 in range(4):
            for r1 in range(4):
                blk = ref[0, 4 * r1 + r0].astype(F32)
                a_ref[0, pl.ds(r0 * 4 * m + r1, m, stride=4), :] = blk[:, :LANES]
                a_ref[1, pl.ds(r0 * 4 * m + r1, m, stride=4), :] = blk[:, LANES:]
        for r0 in range(4):
            b_ref[0, pl.ds(r0, 4 * m, stride=4), :] = a_ref[0, r0 * 4 * m:(r0 + 1) * 4 * m, :]
            b_ref[1, pl.ds(r0, 4 * m, stride=4), :] = a_ref[1, r0 * 4 * m:(r0 + 1) * 4 * m, :]
        return jnp.concatenate([b_ref[0], b_ref[1]], axis=1)
    a_ref = st_ref.at[0]
    for r in range(d):
        blk = ref[0, r].astype(F32)
        a_ref[0, pl.ds(r, m, stride=d), :] = blk[:, :LANES]
        a_ref[1, pl.ds(r, m, stride=d), :] = blk[:, LANES:]
    return jnp.concatenate([a_ref[0], a_ref[1]], axis=1)


def _inproj_body(x_ref, g1_ref, wa_ref, wg_ref, wb_ref, gb_ref, gq_ref, gk_ref,
                 mq_ref, kT_ref, mv_ref, so_ref, gi_ref, gf_ref,
                 q0_ref, q1_ref, q2_ref, k0_ref, k1_ref, k2_ref, v0_ref, v1_ref, v2_ref,
                 sgm_ref, sga_ref, st_ref):
    x = x_ref[...]
    h = x * lax.rsqrt(jnp.mean(x * x, axis=-1, keepdims=True) + EPS) * g1_ref[...]
    hb = h.astype(BF16)

    def seg(w_ref, c):
        return jnp.dot(hb, w_ref[:, c[0]:c[1]], preferred_element_type=F32)

    mq_ref[...] = seg(wa_ref, _C_MQ).astype(BF16)
    mv_ref[...] = seg(wa_ref, _C_MV).astype(BF16)
    so_ref[...] = jax.nn.sigmoid(seg(wa_ref, _C_MO)).astype(BF16)
    hid_r = lax.broadcasted_iota(I32, (A_WIDTH, A_WIDTH), 0) // A_HEAD_DIM
    hid_c = lax.broadcasted_iota(I32, (A_WIDTH, A_WIDTH), 1) // A_HEAD_DIM
    head_ones = (hid_r == hid_c).astype(BF16)
    for c, refs, gain_ref in ((_C_AQ, (q0_ref, q1_ref, q2_ref), gq_ref), (_C_AK, (k0_ref, k1_ref, k2_ref), gk_ref),
                              (_C_AV, (v0_ref, v1_ref, v2_ref), None)):
        val = seg(wb_ref, c)
        for g, ref in enumerate(refs):
            piece = val[:, g * A_WIDTH:(g + 1) * A_WIDTH]
            if gain_ref is not None:
                ss = jnp.dot((piece * piece).astype(BF16), head_ones, preferred_element_type=F32)
                piece = piece * lax.rsqrt(ss * (1.0 / A_HEAD_DIM) + EPS) * gain_ref[:, g * A_WIDTH:(g + 1) * A_WIDTH]
            _split_residues(piece, DILATED_PATTERNS[g][1], ref, st_ref)
    sgm_ref[...] = jax.nn.sigmoid(seg(wb_ref, _C_GM)).astype(BF16)
    sga_ref[...] = jax.nn.sigmoid(seg(wb_ref, _C_GA)).astype(BF16)

    kT_ref[...] = lax.dot_general(wa_ref[:, _C_MK[0]:_C_MK[1]], hb, _TN, preferred_element_type=F32).astype(BF16)
    zg = lax.dot_general(wg_ref[...], hb, _TN, preferred_element_type=F32)
    zi = zg[0:8] + gb_ref[0:8]
    zf = zg[M_HEADS:M_HEADS + 8] + gb_ref[8:16]
    gi_ref[...] = GATE_SOFTCAP * jnp.tanh(zi / GATE_SOFTCAP)
    gf_ref[...] = _log_sigmoid(GATE_SOFTCAP * jnp.tanh(zf / GATE_SOFTCAP))


def _stage_inproj(x2d, norm1_g, w_in, gate_b, gq, gk, batch, seq, tm):
    n = x2d.shape[0]
    steps = seq // tm
    wa = w_in[:, :_A_WIDTH].astype(BF16)
    wg = jnp.pad(w_in[:, _A_WIDTH:_B_START], ((0, 0), (0, LANES - IN_SPLITS[4]))).astype(BF16)
    wb = w_in[:, _B_START:].astype(BF16)
    gb = jnp.zeros((16, 1), F32)
    gb = gb.at[0:4, 0].set(gate_b[:M_HEADS].astype(F32)).at[8:12, 0].set(gate_b[M_HEADS:].astype(F32))
    g1 = norm1_g.astype(F32).reshape(1, D_MODEL)
    gq_t = (jnp.tile(gq.astype(F32), (1, A_HEADS)) * (A_HEAD_DIM ** -0.5)).reshape(1, N_GROUPS * A_WIDTH)
    gk_t = jnp.tile(gk.astype(F32), (1, A_HEADS)).reshape(1, N_GROUPS * A_WIDTH)

    row = lambda w: pl.BlockSpec((tm, w), lambda i: (i, 0))
    rowT = lambda r: pl.BlockSpec((r, tm), lambda i: (0, i))
    full = lambda a: pl.BlockSpec(a.shape, lambda i: (0,) * a.ndim)
    dils = [d for _, d in DILATED_PATTERNS]
    res_shape = lambda d: jax.ShapeDtypeStruct((batch, d, seq // d, A_WIDTH), BF16)
    res_spec = lambda d: pl.BlockSpec((1, d, tm // d, A_WIDTH), lambda i: (i // steps, 0, i % steps, 0))
    out_shapes = (
        jax.ShapeDtypeStruct((n, M_QK_WIDTH), BF16),
        jax.ShapeDtypeStruct((M_QK_WIDTH, n), BF16),
        jax.ShapeDtypeStruct((n, M_WIDTH), BF16),
        jax.ShapeDtypeStruct((n, M_WIDTH), BF16),
        jax.ShapeDtypeStruct((8, n), F32),
        jax.ShapeDtypeStruct((8, n), F32),
        *[res_shape(d) for d in dils], *[res_shape(d) for d in dils], *[res_shape(d) for d in dils],
        jax.ShapeDtypeStruct((n, D_MODEL), BF16),
        jax.ShapeDtypeStruct((n, D_MODEL), BF16),
    )
    out_specs = (row(M_QK_WIDTH), rowT(M_QK_WIDTH), row(M_WIDTH), row(M_WIDTH), rowT(8), rowT(8),
                 *[res_spec(d) for d in dils], *[res_spec(d) for d in dils], *[res_spec(d) for d in dils],
                 row(D_MODEL), row(D_MODEL))
    outs = pl.pallas_call(
        _inproj_body,
        grid=(n // tm,),
        in_specs=[row(D_MODEL), full(g1), full(wa), full(wg), full(wb), full(gb), full(gq_t), full(gk_t)],
        out_specs=out_specs,
        out_shape=out_shapes,
        scratch_shapes=[pltpu.VMEM((2, 2, tm, LANES), F32)],
        compiler_params=_params("parallel"),
        name="inproj",
    )(x2d, g1, wa, wg, wb, gb, gq_t, gk_t)
    mq, kT, mv, so, gi, gf = outs[:6]
    aq, ak, av = outs[6:9], outs[9:12], outs[12:15]
    return mq, kT, mv, so, gi, gf, aq, ak, av, outs[15], outs[16]


M_CHUNK_LEN = 128


def _mlstm_body(q_ref, v_ref, so_ref, ng_ref, *rest, nchunk, nseq):
    kT_refs, gi_refs, gf_refs = rest[0:nseq], rest[nseq:2 * nseq], rest[2 * nseq:3 * nseq]
    o_ref, c_ref, m_ref = rest[3 * nseq:]
    L = M_CHUNK_LEN

    @pl.when(pl.program_id(0) == 0)
    def _():
        c_ref[...] = jnp.zeros_like(c_ref)
        m_ref[...] = jnp.zeros_like(m_ref)

    lane8 = lax.broadcasted_iota(I32, (8, L), 1)
    causal = lax.broadcasted_iota(I32, (L, L), 1) <= lax.broadcasted_iota(I32, (L, L), 0)
    lo_half = lax.broadcasted_iota(I32, (L, LANES), 1) < M_QK_DIM
    ones = jnp.ones((L, M_V_DIM), BF16)

    heads = range(M_HEADS)
    cstate = [[c_ref[s, h * M_QK_DIM:(h + 1) * M_QK_DIM, :] for h in heads] for s in range(nseq)]
    m_prev = [m_ref[s, :, 0:1] for s in range(nseq)]
    units = []
    for c in range(nchunk):
        rows = slice(c * L, (c + 1) * L)
        for s in range(nseq):
            gi = gi_refs[s][:, rows]
            b = gf_refs[s][:, rows]
            sh = 1
            while sh < L:
                b = b + jnp.where(lane8 >= sh, pltpu.roll(b, sh, 1), 0.0)
                sh *= 2
            u = gi - b
            g = b[:, L - 1:L]
            a = g + u
            amax = jnp.max(a, axis=1, keepdims=True)
            m_new = jnp.maximum(g + m_prev[s], amax)
            w = jnp.exp(a - m_new) * (M_QK_DIM ** -0.5)
            s_old = jnp.exp(g + m_prev[s] - m_new)
            vext = [jnp.concatenate([v_ref[s, rows, h * M_V_DIM:(h + 1) * M_V_DIM], ones], axis=1) for h in heads]
            cloc = []
            for h in heads:
                hr = slice(h * M_QK_DIM, (h + 1) * M_QK_DIM)
                kw = (kT_refs[s][hr, rows].astype(F32) * w[h:h + 1, :]).astype(BF16)
                cloc.append(jnp.dot(kw, vext[h], preferred_element_type=F32))
            units.append(dict(seq=s, rows=rows, b=b, u=u, m_prev=m_prev[s], state=cstate[s], vext=vext))
            cstate[s] = [s_old[h:h + 1, :] * cstate[s][h] + cloc[h] for h in heads]
            m_prev[s] = m_new
    for s in range(nseq):
        for h in heads:
            c_ref[s, h * M_QK_DIM:(h + 1) * M_QK_DIM, :] = cstate[s][h]
        m_ref[s] = jnp.broadcast_to(m_prev[s], (8, LANES))

    for un in units:
        s, rows = un["seq"], un["rows"]
        un["s"], un["qc"] = [], []
        for p in range(M_HEADS // 2):
            lanes_p = slice(p * LANES, (p + 1) * LANES)
            q_pair = q_ref[s, rows, lanes_p]
            kT_pair = kT_refs[s][lanes_p, rows]
            c_pair = jnp.concatenate([un["state"][2 * p], un["state"][2 * p + 1]], axis=0).astype(BF16)
            for hh in range(2):
                qm = jnp.where(lo_half if hh == 0 else jnp.logical_not(lo_half), q_pair, jnp.zeros_like(q_pair))
                un["s"].append(jnp.dot(qm, kT_pair, preferred_element_type=F32) * (M_QK_DIM ** -0.5))
                un["qc"].append(jnp.dot(qm, c_pair, preferred_element_type=F32))

    for un in units:
        s, rows, b, u, mp = un["seq"], un["rows"], un["b"], un["u"], un["m_prev"]
        for h in heads:
            hl = slice(h * M_V_DIM, (h + 1) * M_V_DIM)
            bcol = jnp.transpose(jnp.broadcast_to(b[h:h + 1, :], (L, L)))
            dm = jnp.where(causal, bcol + u[h:h + 1, :], -jnp.inf)
            inter = bcol + mp[h:h + 1, :]
            m_t = jnp.maximum(inter, jnp.max(dm, axis=1, keepdims=True))
            pmat = (un["s"][h] * jnp.exp(dm - m_t)).astype(BF16)
            sc = jnp.exp(inter - m_t)
            out = (jnp.dot(pmat, un["vext"][h], preferred_element_type=F32)
                   + jnp.concatenate([sc, sc], axis=1) * un["qc"][h])
            hv = out[:, :M_V_DIM] / jnp.maximum(jnp.abs(out[:, M_V_DIM:]), jnp.exp(-m_t))
            hn = hv * lax.rsqrt(jnp.mean(hv * hv, axis=1, keepdims=True) + EPS)
            hn = hn * ng_ref[:, hl] * so_ref[s, rows, hl].astype(F32)
            o_ref[s, rows, hl] = hn.astype(BF16)


def _stage_mlstm(mq, kT, mv, so, gi, gf, norm_g, batch, seq, rows_per_step):
    n = batch * seq
    R = rows_per_step
    steps = seq // R
    ng = norm_g.astype(F32).reshape(1, M_WIDTH)
    per_seq = lambda a: a.reshape(batch, seq, a.shape[1])
    row = lambda w: pl.BlockSpec((batch, R, w), lambda i: (0, i, 0))
    colT = lambda r, s: pl.BlockSpec((r, R), lambda i, s=s: (0, s * steps + i))
    seqs = range(batch)
    out = pl.pallas_call(
        functools.partial(_mlstm_body, nchunk=R // M_CHUNK_LEN, nseq=batch),
        grid=(steps,),
        in_specs=[row(M_QK_WIDTH), row(M_WIDTH), row(M_WIDTH), pl.BlockSpec((1, M_WIDTH), lambda i: (0, 0)),
                  *[colT(M_QK_WIDTH, s) for s in seqs], *[colT(8, s) for s in seqs], *[colT(8, s) for s in seqs]],
        out_specs=row(M_WIDTH),
        out_shape=jax.ShapeDtypeStruct((batch, seq, M_WIDTH), BF16),
        scratch_shapes=[pltpu.VMEM((batch, M_QK_WIDTH, 2 * M_V_DIM), F32), pltpu.VMEM((batch, 8, LANES), F32)],
        compiler_params=_params("arbitrary"),
        name="mlstm",
    )(per_seq(mq), per_seq(mv), per_seq(so), ng, *[kT] * batch, *[gi] * batch, *[gf] * batch)
    return out.reshape(n, M_WIDTH)


def _attn_body(q_ref, kp_ref, kc_ref, vp_ref, vc_ref, o_ref, lse_ref, *, dil, slopes, lq):
    QB = N_BACK
    first = pl.program_id(2) == 0
    qn = q_ref[0, 0]
    kcn = kc_ref[0, 0]
    kpn = kp_ref[0, 0]
    vc = vc_ref[0, 0]
    vp = vp_ref[0, 0]

    qi = lax.broadcasted_iota(I32, (QB, 2 * QB), 0)
    kj = lax.broadcasted_iota(I32, (QB, 2 * QB), 1)
    dist = qi + QB - kj
    band = jnp.logical_and(dist >= 0, dist <= N_BACK)
    distf = (dist * dil).astype(F32)
    bias = [jnp.where(band, -float(slopes[h]) * distf, -jnp.inf) for h in range(A_HEADS)]
    no_prev = jnp.logical_and(first, kj < QB)
    lo_half = lax.broadcasted_iota(I32, (QB, LANES), 1) < A_HEAD_DIM
    ones = jnp.ones((2 * QB, LANES), BF16)

    units = []
    for j in range(lq // QB):
        rows = slice(j * QB, (j + 1) * QB)
        prow = slice((j - 1) * QB, j * QB)
        keys = jnp.concatenate([kpn if j == 0 else kcn[prow], kcn[rows]], axis=0)
        vals = jnp.concatenate([vp if j == 0 else vc[prow], vc[rows]], axis=0)
        for p in range(A_HEADS // 2):
            lanes_p = slice(p * LANES, (p + 1) * LANES)
            q_pair = qn[rows, lanes_p]
            k_pair = keys[:, lanes_p]
            vext = jnp.concatenate([vals[:, lanes_p], ones], axis=1)
            scores = []
            for hh in range(2):
                sel = lo_half if hh == 0 else jnp.logical_not(lo_half)
                qm = jnp.where(sel, q_pair, jnp.zeros_like(q_pair))
                scores.append(lax.dot_general(qm, k_pair, _NT, preferred_element_type=F32))
            units.append((j, rows, lanes_p, p, vext, scores))

    for j, rows, lanes_p, p, vext, scores in units:
        o_pair = None
        l_pair = None
        for hh in range(2):
            s = scores[hh] + bias[2 * p + hh]
            if j == 0:
                s = jnp.where(no_prev, -jnp.inf, s)
            m = jnp.max(s, axis=1, keepdims=True)
            pv = jnp.dot(jnp.exp(s - m).astype(BF16), vext, preferred_element_type=F32)
            den = pv[:, LANES:]
            o_h = pv[:, :LANES] / den
            l_h = m + jnp.log(den)
            o_pair = o_h if hh == 0 else jnp.where(lo_half, o_pair, o_h)
            l_pair = l_h if hh == 0 else jnp.where(lo_half, l_pair, l_h)
        o_ref[0, 0, rows, lanes_p] = o_pair.astype(BF16)
        lse_ref[0, 0, rows, lanes_p] = l_pair


def _stage_attn(aq, ak, av, batch, seq, group):
    _, dil = DILATED_PATTERNS[group]
    L = seq // dil
    assert L % N_BACK == 0
    lq = min(2048, L)
    nq = L // lq
    sub = lq // N_BACK
    cur = pl.BlockSpec((1, 1, lq, A_WIDTH), lambda b, r, i: (b, r, i, 0))
    prev = pl.BlockSpec((1, 1, N_BACK, A_WIDTH), lambda b, r, i: (b, r, jnp.maximum(i * sub - 1, 0), 0))
    return pl.pallas_call(
        functools.partial(_attn_body, dil=dil, slopes=tuple(_alibi_slopes()[group]), lq=lq),
        grid=(batch, dil, nq),
        in_specs=[cur, prev, cur, prev, cur],
        out_specs=(cur, cur),
        out_shape=(jax.ShapeDtypeStruct((batch, dil, L, A_WIDTH), BF16),
                   jax.ShapeDtypeStruct((batch, dil, L, A_WIDTH), F32)),
        compiler_params=_params("parallel", "parallel", "parallel"),
        name=f"dilated_attn_d{dil}",
    )(aq, ak, ak, av, av)


PACK_ROWS = D_MODEL // (2 * LANES)
U32 = jnp.uint32
_HIGH_HALF = 0xFFFF0000


def _pack_rows(val):
    half = D_MODEL // 2

    def bits(v):
        return lax.bitcast_convert_type(v.astype(BF16).astype(F32), U32)

    return (bits(val[:, :half]) >> 16) | (bits(val[:, half:]) & U32(_HIGH_HALF))


def _unpack_rows(words):
    lo = lax.bitcast_convert_type(words << 16, F32).astype(BF16)
    hi = lax.bitcast_convert_type(words & U32(_HIGH_HALF), F32).astype(BF16)
    return jnp.concatenate([lo, hi], axis=1)


def _flat(ref):
    rows = 1
    for d in ref.shape[:-2]:
        rows *= d
    return ref.reshape(rows * PACK_ROWS, LANES)


def _store_packed(flat_ref, row0, words):
    t = words.shape[0]
    for s in range(PACK_ROWS):
        flat_ref[pl.ds(row0 * PACK_ROWS + s, t, stride=PACK_ROWS), :] = words[:, s * LANES:(s + 1) * LANES]


def _load_packed(flat_ref, row0, t):
    return jnp.concatenate([flat_ref[pl.ds(row0 * PACK_ROWS + s, t, stride=PACK_ROWS), :] for s in range(PACK_ROWS)],
                           axis=1)


def _rows8(vals):
    t = vals[0].shape[1]
    rid = lax.broadcasted_iota(I32, (8, t), 0)
    out = jnp.zeros((8, t), vals[0].dtype)
    for k, v in enumerate(vals):
        out = jnp.where(rid == k, jnp.broadcast_to(v, (8, t)), out)
    return out


def _merge_body(hm_ref, o1_ref, o2_ref, o3_ref, l1_ref, l2_ref, l3_ref, sgm_ref, sga_ref, x_ref,
                wm_ref, wa_ref, wo_ref, g2_ref, wrh_ref, br_ref,
                x2_ref, xn_ref, loc_ref, gate_ref, tcnt_ref, tcar_ref, cnt_ref, carry_ref, st_ref):
    @pl.when(pl.program_id(0) == 0)
    def _():
        carry_ref[...] = jnp.zeros_like(carry_ref)

    m_branch = jnp.dot(hm_ref[...], wm_ref[...], preferred_element_type=F32)
    dils = [d for _, d in DILATED_PATTERNS]
    l1, l2, l3 = [_merge_residues(r, d, st_ref) for r, d in zip((l1_ref, l2_ref, l3_ref), dils)]
    lmax = jnp.maximum(jnp.maximum(l1, l2), l3)
    e1, e2, e3 = jnp.exp(l1 - lmax), jnp.exp(l2 - lmax), jnp.exp(l3 - lmax)
    num = e1 * _merge_residues(o1_ref, dils[0], st_ref)
    num = num + e2 * _merge_residues(o2_ref, dils[1], st_ref)
    num = num + e3 * _merge_residues(o3_ref, dils[2], st_ref)
    h_a = num / (e1 + e2 + e3)
    y = (sgm_ref[...].astype(F32) * m_branch
         + sga_ref[...].astype(F32) * jnp.dot(h_a.astype(BF16), wa_ref[...], preferred_element_type=F32))
    x2 = x_ref[...] + jnp.dot(y.astype(BF16), wo_ref[...], preferred_element_type=F32)
    x2_ref[...] = x2
    xn = x2 * lax.rsqrt(jnp.mean(x2 * x2, axis=-1, keepdims=True) + EPS) * g2_ref[...]
    xh = xn.astype(BF16)
    xn_ref[...] = xh

    logits = lax.dot_general(wrh_ref[...], xh, _NT, preferred_element_type=F32) + br_ref[...]
    t = logits.shape[1]
    eid = lax.broadcasted_iota(I32, (N_EXPERTS, t), 0).astype(F32)
    vals = logits
    top_v, top_i = [], []
    for _ in range(TOP_K):
        mx = jnp.max(vals, axis=0, keepdims=True)
        ik = jnp.min(jnp.where(vals == mx, eid, float(N_EXPERTS)), axis=0, keepdims=True)
        top_v.append(mx)
        top_i.append(ik)
        vals = jnp.where(eid == ik, -jnp.inf, vals)
    ex = [jnp.exp(v - top_v[0]) for v in top_v]
    den = ex[0] + ex[1] + ex[2] + ex[3]
    gate_ref[...] = _rows8([e / den for e in ex])

    chosen = jnp.zeros((N_EXPERTS, t), F32)
    for ik in top_i:
        chosen = chosen + (eid == ik).astype(F32)
    before = (lax.broadcasted_iota(I32, (t, t), 0) < lax.broadcasted_iota(I32, (t, t), 1)).astype(BF16)
    prefix = jnp.dot(chosen.astype(BF16), before, preferred_element_type=F32)
    tcount = jnp.broadcast_to(jnp.sum(chosen, axis=1, keepdims=True), (N_EXPERTS, LANES))
    below = (lax.broadcasted_iota(I32, (N_EXPERTS, N_EXPERTS), 1)
             < lax.broadcasted_iota(I32, (N_EXPERTS, N_EXPERTS), 0)).astype(BF16)
    t_hi = jnp.floor(tcount * (1.0 / BF16_EXACT_INT)) * BF16_EXACT_INT
    tile_off = (jnp.dot(below, t_hi.astype(BF16), preferred_element_type=F32)
                + jnp.dot(below, (tcount - t_hi).astype(BF16), preferred_element_type=F32))
    pos = prefix + tile_off[:, 0:1]
    loc_ref[...] = _rows8([jnp.sum(jnp.where(eid == ik, pos, 0.0), axis=0, keepdims=True).astype(I32)
                           for ik in top_i])
    carry = carry_ref[...]
    tcnt_ref[...] = tcount.astype(I32)
    tcar_ref[...] = carry.astype(I32)
    total = carry + tcount
    carry_ref[...] = total
    cnt_ref[...] = total


def _stage_merge(h_m, attn, sgm, sga, x2d, w_mb, w_ab, w_out, norm2_g, w_router, b_router, batch, seq, tm):
    n = x2d.shape[0]
    steps = seq // tm
    (o1, l1), (o2, l2), (o3, l3) = attn
    wm = w_mb.astype(BF16)
    wa = w_ab.astype(BF16)
    wo = w_out.astype(BF16)
    g2 = norm2_g.astype(F32).reshape(1, D_MODEL)
    wrh = w_router.astype(BF16).T
    br = b_router.astype(F32).reshape(N_EXPERTS, 1)
    row = lambda w: pl.BlockSpec((tm, w), lambda i: (i, 0))
    rowT = lambda r: pl.BlockSpec((r, tm), lambda i: (0, i))
    full = lambda a: pl.BlockSpec(a.shape, lambda i: (0,) * a.ndim)
    res = lambda d: pl.BlockSpec((1, d, tm // d, A_WIDTH), lambda i: (i // steps, 0, i % steps, 0))
    dils = [d for _, d in DILATED_PATTERNS]
    per_tile = pl.BlockSpec((N_EXPERTS, LANES), lambda i: (0, i))
    return pl.pallas_call(
        _merge_body,
        grid=(n // tm,),
        in_specs=[row(M_WIDTH), *[res(d) for d in dils], *[res(d) for d in dils],
                  row(D_MODEL), row(D_MODEL), row(D_MODEL),
                  full(wm), full(wa), full(wo), full(g2), full(wrh), full(br)],
        out_specs=(row(D_MODEL), row(D_MODEL), rowT(8), rowT(8), per_tile, per_tile,
                   pl.BlockSpec((N_EXPERTS, LANES), lambda i: (0, 0))),
        out_shape=(jax.ShapeDtypeStruct((n, D_MODEL), F32),
                   jax.ShapeDtypeStruct((n, D_MODEL), BF16),
                   jax.ShapeDtypeStruct((8, n), I32),
                   jax.ShapeDtypeStruct((8, n), F32),
                   jax.ShapeDtypeStruct((N_EXPERTS, (n // tm) * LANES), I32),
                   jax.ShapeDtypeStruct((N_EXPERTS, (n // tm) * LANES), I32),
                   jax.ShapeDtypeStruct((N_EXPERTS, LANES), F32)),
        scratch_shapes=[pltpu.VMEM((N_EXPERTS, LANES), F32), pltpu.VMEM((2, 2, tm, LANES), F32)],
        compiler_params=_params("arbitrary"),
        name="merge_route",
    )(h_m, o1, o2, o3, l1, l2, l3, sgm, sga, x2d, wm, wa, wo, g2, wrh, br)


def _offsets_body(cnt_ref, blk_ref, pstart_ref, zlo_ref, zhi_ref, *, nblk_pad):
    cnt = cnt_ref[...]
    padded = jnp.floor((cnt + (MOE_BLOCK - 1)) * (1.0 / MOE_BLOCK)) * MOE_BLOCK
    lower = (lax.broadcasted_iota(I32, (N_EXPERTS, N_EXPERTS), 1)
             <= lax.broadcasted_iota(I32, (N_EXPERTS, N_EXPERTS), 0)).astype(BF16)
    nb = padded * (1.0 / MOE_BLOCK)
    nb_hi = jnp.floor(nb * (1.0 / BF16_EXACT_INT)) * BF16_EXACT_INT
    pends = (jnp.dot(lower, nb_hi.astype(BF16), preferred_element_type=F32)
             + jnp.dot(lower, (nb - nb_hi).astype(BF16), preferred_element_type=F32)) * MOE_BLOCK
    pstart = pends - padded
    pstart_ref[...] = pstart.astype(I32)
    zlo_ref[...] = (pstart + cnt).astype(I32)
    zhi_ref[...] = pends.astype(I32)

    first_row = (lax.broadcasted_iota(I32, (N_EXPERTS, nblk_pad), 1) * MOE_BLOCK).astype(F32)
    pe = jnp.broadcast_to(pends[:, 0:1], (N_EXPERTS, nblk_pad))
    be = jnp.sum((pe <= first_row).astype(F32), axis=0, keepdims=True)
    be = jnp.minimum(be, float(N_EXPERTS - 1))
    nused = pends[N_EXPERTS - 1:N_EXPERTS, 0:1] * (1.0 / MOE_BLOCK)
    nonempty = jnp.broadcast_to(padded[:, 0:1], (N_EXPERTS, nblk_pad)) > 0.0
    runidx = jnp.sum(jnp.logical_and(pe <= first_row, nonempty).astype(F32), axis=0, keepdims=True)
    parity = runidx - 2.0 * jnp.floor(runidx * 0.5)
    eid = lax.broadcasted_iota(I32, (N_EXPERTS, nblk_pad), 0).astype(F32)
    later = jnp.logical_and(eid > be, nonempty)
    nxt = jnp.min(jnp.where(later, eid, float(N_EXPERTS)), axis=0, keepdims=True)
    blk_ref[...] = _rows8([be.astype(I32), jnp.broadcast_to(nused, (1, nblk_pad)).astype(I32),
                           parity.astype(I32), nxt.astype(I32)])


def _stage_offsets(cnt, nblk):
    nblk_pad = -(-nblk // LANES) * LANES
    const = lambda r, c: pl.BlockSpec((r, c), lambda i: (0, 0))
    per_expert = jax.ShapeDtypeStruct((N_EXPERTS, LANES), I32)
    return pl.pallas_call(
        functools.partial(_offsets_body, nblk_pad=nblk_pad),
        grid=(1,),
        in_specs=[const(N_EXPERTS, LANES)],
        out_specs=(const(8, nblk_pad), const(N_EXPERTS, LANES), const(N_EXPERTS, LANES), const(N_EXPERTS, LANES)),
        out_shape=(jax.ShapeDtypeStruct((8, nblk_pad), I32), per_expert, per_expert, per_expert),
        compiler_params=_params("arbitrary"),
        name="route_offsets",
    )(cnt)


RUN_BITS = 10


def _tile_rows(ref, first_row, nrows):
    return ref.at[pl.ds(first_row, nrows)]


def _for_each_piece(length, fn):
    for b in reversed(range(RUN_BITS)):
        @pl.when(((length >> b) & 1) == 1)
        def _(b=b):
            fn((length >> (b + 1)) << (b + 1), 1 << b)


def _for_each_run(tile, tcnt_ref, tcar_ref, pstart_ref, fn):
    def per_expert(e, local):
        count = tcnt_ref[tile, e]
        first = pstart_ref[e] + tcar_ref[tile, e]
        _for_each_piece(count, lambda off, size: fn(local + off, first + off, size))
        return local + count

    lax.fori_loop(0, N_EXPERTS, per_expert, 0)


PERM_CHUNK = 256


def _dispatch_body(tcnt_ref, tcar_ref, pstart_ref, zlo_ref, zhi_ref, loc_ref, xn_ref, xs_hbm,
                   buf_ref, sems, *, tm):
    step = pl.program_id(0)
    nloc = TOP_K * tm

    def wait_buffer(slot):
        pltpu.make_async_copy(buf_ref.at[slot], xs_hbm.at[pl.ds(0, nloc)], sems.at[slot]).wait()

    for slot in range(2):
        tile = 2 * step + slot

        @pl.when(step > 0)
        def _(slot=slot):
            wait_buffer(slot)

        loc = loc_ref[:, slot * tm:(slot + 1) * tm]
        xn = xn_ref[slot * tm:(slot + 1) * tm, :]
        for c in range(TOP_K * tm // PERM_CHUNK):
            lid = lax.broadcasted_iota(I32, (PERM_CHUNK, tm), 0) + c * PERM_CHUNK
            hit = lid == loc[0:1, :]
            for k in range(1, TOP_K):
                hit = jnp.logical_or(hit, lid == loc[k:k + 1, :])
            rows = jnp.dot(jnp.where(hit, 1.0, 0.0).astype(BF16), xn, preferred_element_type=F32)
            _store_packed(_flat(buf_ref), slot * nloc + c * PERM_CHUNK, _pack_rows(rows))

        def run_copy(local, first, size, slot=slot):
            return pltpu.make_async_copy(_tile_rows(buf_ref.at[slot], local, size),
                                         _tile_rows(xs_hbm, first, size), sems.at[slot])

        _for_each_run(tile, tcnt_ref, tcar_ref, pstart_ref, lambda l, f, s: run_copy(l, f, s).start())

    @pl.when(step == pl.num_programs(0) - 1)
    def _():
        wait_buffer(0)
        wait_buffer(1)
        zsrc = buf_ref.at[0]
        zsrc[pl.ds(0, MOE_BLOCK)] = jnp.zeros((MOE_BLOCK, PACK_ROWS, LANES), U32)

        def zero_copy(first, size):
            return pltpu.make_async_copy(_tile_rows(zsrc, 0, size), _tile_rows(xs_hbm, first, size), sems.at[0])

        def per_expert(e, carry):
            lo = zlo_ref[e]
            npad = zhi_ref[e] - lo
            _for_each_piece(npad, lambda off, size: zero_copy(lo + off, size).start())
            _for_each_piece(npad, lambda off, size: zero_copy(lo + off, size).wait())
            return carry

        lax.fori_loop(0, N_EXPERTS, per_expert, 0)

        first_unused = zhi_ref[N_EXPERTS - 1] // MOE_BLOCK
        nblk = xs_hbm.shape[0] // MOE_BLOCK

        def tail(blk, carry):
            zero_copy(blk * MOE_BLOCK, MOE_BLOCK).start()
            zero_copy(blk * MOE_BLOCK, MOE_BLOCK).wait()
            return carry

        lax.fori_loop(first_unused, nblk, tail, 0)


def _stage_dispatch(tables, loc8, xn, nrows, tm):
    n = xn.shape[0]
    assert TOP_K * tm >= MOE_BLOCK and (n // tm) % 2 == 0
    grid_spec = pltpu.PrefetchScalarGridSpec(
        num_scalar_prefetch=5,
        grid=(n // (2 * tm),),
        in_specs=[pl.BlockSpec((8, 2 * tm), lambda i, *_: (0, i)),
                  pl.BlockSpec((2 * tm, D_MODEL), lambda i, *_: (i, 0))],
        out_specs=pl.BlockSpec(memory_space=pl.ANY),
        scratch_shapes=[pltpu.VMEM((2, TOP_K * tm, PACK_ROWS, LANES), U32), pltpu.SemaphoreType.DMA((2,))],
    )
    return pl.pallas_call(
        functools.partial(_dispatch_body, tm=tm),
        grid_spec=grid_spec,
        out_shape=jax.ShapeDtypeStruct((nrows, PACK_ROWS, LANES), U32),
        compiler_params=_params("arbitrary"),
        name="dispatch",
    )(*tables, loc8, xn)


EXPERT_BLOCKS_PER_STEP = 2


def _expert_body(be_ref, nu_ref, par_ref, nxt_ref, xs_ref, w1_hbm, b1_ref, w2_hbm, b2_ref, ys_ref,
                 w1f_ref, w2f_ref, w1b_ref, w2b_ref, sems):
    def fetch(expert, slot):
        return (pltpu.make_async_copy(w1_hbm.at[expert], w1f_ref.at[slot], sems.at[slot, 0]),
                pltpu.make_async_copy(w2_hbm.at[expert], w2f_ref.at[slot], sems.at[slot, 1]))

    for sub in range(EXPERT_BLOCKS_PER_STEP):
        j = pl.program_id(0) * EXPERT_BLOCKS_PER_STEP + sub
        used = j < nu_ref[0]
        jj = jnp.maximum(jnp.minimum(j, nu_ref[0] - 1), 0)
        e = be_ref[jj]
        fresh = jnp.logical_or(j == 0, e != be_ref[jnp.maximum(jj - 1, 0)])

        @pl.when(jnp.logical_and(used, fresh))
        def _(j=j, jj=jj, e=e):
            slot = par_ref[jj]

            @pl.when(j == 0)
            def _():
                for c in fetch(e, slot):
                    c.start()

            for c in fetch(e, slot):
                c.wait()
            nxt = nxt_ref[jj]

            @pl.when(nxt < N_EXPERTS)
            def _():
                for c in fetch(nxt, 1 - slot):
                    c.start()

            w1b_ref[...] = w1f_ref[slot].astype(BF16)
            w2b_ref[...] = w2f_ref[slot].astype(BF16)

        @pl.when(used)
        def _(e=e, sub=sub):
            xb = _unpack_rows(_load_packed(_flat(xs_ref), sub * MOE_BLOCK, MOE_BLOCK))
            gu = jnp.dot(xb, w1b_ref[...], preferred_element_type=F32) + b1_ref[pl.ds(e, 1), :]
            gate = jnp.minimum(gu[:, :D_FF], SWIGLU_LIMIT)
            lin = jnp.clip(gu[:, D_FF:], -SWIGLU_LIMIT, SWIGLU_LIMIT)
            act = (lin + 1.0) * (gate * jax.nn.sigmoid(SWIGLU_ALPHA * gate))
            ys = jnp.dot(act.astype(BF16), w2b_ref[...], preferred_element_type=F32) + b2_ref[pl.ds(e, 1), :]
            _store_packed(_flat(ys_ref), sub * MOE_BLOCK, _pack_rows(ys))

        @pl.when(jnp.logical_not(used))
        def _(sub=sub):
            ys_ref[pl.ds(sub * MOE_BLOCK, MOE_BLOCK)] = jnp.zeros((MOE_BLOCK, PACK_ROWS, LANES), U32)


def _stage_experts(blk8, xs, w1, b1, w2, b2):
    nrows = xs.shape[0]
    nblk = nrows // MOE_BLOCK
    assert nblk % EXPERT_BLOCKS_PER_STEP == 0
    block_e, nused, parity, nxt = blk8[0, :nblk], blk8[1, :1], blk8[2, :nblk], blk8[3, :nblk]
    tiles = (EXPERT_BLOCKS_PER_STEP * MOE_BLOCK, PACK_ROWS, LANES)
    full = lambda a: pl.BlockSpec(a.shape, lambda j, *_: (0,) * a.ndim)
    grid_spec = pltpu.PrefetchScalarGridSpec(
        num_scalar_prefetch=4,
        grid=(nblk // EXPERT_BLOCKS_PER_STEP,),
        in_specs=[pl.BlockSpec(tiles, lambda j, *_: (j, 0, 0)),
                  pl.BlockSpec(memory_space=pl.ANY), full(b1),
                  pl.BlockSpec(memory_space=pl.ANY), full(b2)],
        out_specs=pl.BlockSpec(tiles, lambda j, *_: (j, 0, 0)),
        scratch_shapes=[pltpu.VMEM((2, D_MODEL, 2 * D_FF), F32), pltpu.VMEM((2, D_FF, D_MODEL), F32),
                        pltpu.VMEM((D_MODEL, 2 * D_FF), BF16), pltpu.VMEM((D_FF, D_MODEL), BF16),
                        pltpu.SemaphoreType.DMA((2, 2))],
    )
    return pl.pallas_call(
        _expert_body,
        grid_spec=grid_spec,
        out_shape=jax.ShapeDtypeStruct((nrows, PACK_ROWS, LANES), U32),
        compiler_params=_params("arbitrary"),
        name="experts",
    )(block_e, nused, parity, nxt, xs, w1, b1, w2, b2)


COMBINE_CHUNK = 512


def _combine_body(tcnt_ref, tcar_ref, pstart_ref, loc_ref, gate_ref, x2_ref, ys_hbm, out_ref,
                  buf_ref, g_ref, sems, *, tm):
    step = pl.program_id(0)
    nloc = TOP_K * tm

    def start_runs(tile, slot):
        def run_copy(local, first, size):
            return pltpu.make_async_copy(_tile_rows(ys_hbm, first, size),
                                         _tile_rows(buf_ref.at[slot], local, size), sems.at[slot])
        _for_each_run(tile, tcnt_ref, tcar_ref, pstart_ref, lambda l, f, s: run_copy(l, f, s).start())

    def wait_buffer(slot):
        pltpu.make_async_copy(ys_hbm.at[pl.ds(0, nloc)], buf_ref.at[slot], sems.at[slot]).wait()

    def combine(slot):
        zpad = jnp.zeros((LANES - 16, LANES), F32)
        cols = []
        for c in range(tm // LANES):
            cols_in = slice(slot * tm + c * LANES, slot * tm + (c + 1) * LANES)
            cols.append(jnp.transpose(jnp.concatenate([loc_ref[:, cols_in].astype(F32), gate_ref[:, cols_in], zpad],
                                                      axis=0)))
        rows = slice(slot * tm, (slot + 1) * tm)
        acc = x2_ref[rows, :]
        for lc in range(nloc // COMBINE_CHUNK):
            lane = (lax.broadcasted_iota(I32, (LANES, COMBINE_CHUNK), 1) + lc * COMBINE_CHUNK).astype(F32)
            for c in range(tm // LANES):
                g = jnp.zeros((LANES, COMBINE_CHUNK), F32)
                for k in range(TOP_K):
                    g = jnp.where(lane == cols[c][:, k:k + 1], cols[c][:, 8 + k:9 + k], g)
                g_ref[c * LANES:(c + 1) * LANES, lc * COMBINE_CHUNK:(lc + 1) * COMBINE_CHUNK] = g.astype(BF16)
            if lc == 0:
                wait_buffer(slot)
            ys = _unpack_rows(_load_packed(_flat(buf_ref), slot * nloc + lc * COMBINE_CHUNK, COMBINE_CHUNK))
            acc = acc + jnp.dot(g_ref[:, lc * COMBINE_CHUNK:(lc + 1) * COMBINE_CHUNK], ys, preferred_element_type=F32)
        out_ref[rows, :] = acc

    @pl.when(step == 0)
    def _():
        start_runs(0, 0)

    start_runs(2 * step + 1, 1)
    combine(0)

    @pl.when(step + 1 < pl.num_programs(0))
    def _():
        start_runs(2 * step + 2, 0)

    combine(1)


def _stage_combine(tables, loc8, gate8, x2, ys, tm):
    n = x2.shape[0]
    assert (n // tm) % 2 == 0
    grid_spec = pltpu.PrefetchScalarGridSpec(
        num_scalar_prefetch=3,
        grid=(n // (2 * tm),),
        in_specs=[pl.BlockSpec((8, 2 * tm), lambda i, *_: (0, i)),
                  pl.BlockSpec((8, 2 * tm), lambda i, *_: (0, i)),
                  pl.BlockSpec((2 * tm, D_MODEL), lambda i, *_: (i, 0)),
                  pl.BlockSpec(memory_space=pl.ANY)],
        out_specs=pl.BlockSpec((2 * tm, D_MODEL), lambda i, *_: (i, 0)),
        scratch_shapes=[pltpu.VMEM((2, TOP_K * tm, PACK_ROWS, LANES), U32),
                        pltpu.VMEM((tm, TOP_K * tm), BF16),
                        pltpu.SemaphoreType.DMA((2,))],
    )
    return pl.pallas_call(
        functools.partial(_combine_body, tm=tm),
        grid_spec=grid_spec,
        out_shape=jax.ShapeDtypeStruct((n, D_MODEL), F32),
        compiler_params=_params("arbitrary"),
        name="combine",
    )(*tables, loc8, gate8, x2, ys)


def _moe(x2, xn, loc8, gate8, tcnt, tcar, cnt, w1, b1, w2, b2, tm):
    n = x2.shape[0]
    ntile = n // tm
    nblk = -(-(n * TOP_K) // MOE_BLOCK) + N_EXPERTS
    blk8, pstart, zlo, zhi = _stage_offsets(cnt, nblk)
    per_tile = lambda a: a.reshape(N_EXPERTS, ntile, LANES)[:, :, 0].T
    tables = (per_tile(tcnt), per_tile(tcar), pstart[:, 0])
    xs = _stage_dispatch(tables + (zlo[:, 0], zhi[:, 0]), loc8, xn, nblk * MOE_BLOCK, tm)
    ys = _stage_experts(blk8, xs, w1, b1, w2, b2)
    return _stage_combine(tables, loc8, gate8, x2, ys, tm)


def kernel(x, norm1_g, w_in, mlstm_gate_b, mlstm_norm_g, attn_q_norm_g, attn_k_norm_g, w_mlstm_branch,
           w_attn_branch, w_out, norm2_g, w_router, b_router, w1, b1, w2, b2):
    batch, seq, _ = x.shape
    n = batch * seq
    for l in range(norm1_g.shape[0]):
        x2d = x.reshape(n, D_MODEL)
        tm = min(512, seq)
        mq, kT, mv, so, gi, gf, aq, ak, av, sgm, sga = _stage_inproj(
            x2d, norm1_g[l], w_in[l], mlstm_gate_b[l], attn_q_norm_g[l], attn_k_norm_g[l], batch, seq, tm)
        h_m = _stage_mlstm(mq, kT, mv, so, gi, gf, mlstm_norm_g[l], batch, seq, tm)
        attn = [_stage_attn(aq[g], ak[g], av[g], batch, seq, g)
                for g in range(N_GROUPS)]
        x2, xn, loc8, gate8, tcnt, tcar, cnt = _stage_merge(
            h_m, attn, sgm, sga, x2d, w_mlstm_branch[l], w_attn_branch[l], w_out[l], norm2_g[l],
            w_router[l], b_router[l], batch, seq, tm)
        out = _moe(x2, xn, loc8, gate8, tcnt, tcar, cnt, w1[l], b1[l], w2[l], b2[l], tm)
        x = out.reshape(batch, seq, D_MODEL)
    return x
```

```python
import functools

import numpy as np
import jax
import jax.numpy as jnp
from jax import lax
from jax.experimental import pallas as pl
from jax.experimental.pallas import tpu as pltpu

F32 = jnp.float32
BF16 = jnp.bfloat16
I32 = jnp.int32

D_MODEL = 1024
M_HEADS = 4
M_QK_DIM = 64
M_V_DIM = 128
GATE_SOFTCAP = 15.0
A_HEADS = 4
A_HEAD_DIM = 64
DILATED_PATTERNS = ((128, 1), (512, 4), (2048, 16))
N_GROUPS = len(DILATED_PATTERNS)
N_BACK = 128
N_EXPERTS = 32
TOP_K = 4
D_FF = 1024
SWIGLU_LIMIT = 7.0
SWIGLU_ALPHA = 1.702
MOE_BLOCK = 512
EPS = 1e-6

M_WIDTH = M_HEADS * M_V_DIM
M_QK_WIDTH = M_HEADS * M_QK_DIM
A_WIDTH = A_HEADS * A_HEAD_DIM
IN_SPLITS = (M_QK_WIDTH, M_QK_WIDTH, M_WIDTH, M_WIDTH, 2 * M_HEADS,
             N_GROUPS * A_WIDTH, N_GROUPS * A_WIDTH, N_GROUPS * A_WIDTH, D_MODEL, D_MODEL)

LANES = 128
VMEM_LIMIT = 56 * 1024 * 1024

BF16_EXACT_INT = 256.0

_NT = (((1,), (1,)), ((), ()))
_TN = (((0,), (1,)), ((), ()))


def _alibi_slopes():
    n = N_GROUPS * A_HEADS
    s = np.exp2(-8.0 * np.arange(1, n + 1) / n).astype(np.float32)
    return s.reshape(N_GROUPS, A_HEADS)


def _params(*sem):
    return pltpu.CompilerParams(dimension_semantics=sem, vmem_limit_bytes=VMEM_LIMIT)


def _log_sigmoid(x):
    return jnp.minimum(x, 0.0) - jnp.log1p(jnp.exp(-jnp.abs(x)))


_A_WIDTH = sum(IN_SPLITS[:4])
_B_START = _A_WIDTH + IN_SPLITS[4]


def _piece_segments(widths):
    bounds, start = [], 0
    for width in widths:
        bounds.append((start, start + width))
        start += width
    return bounds


_C_MQ, _C_MK, _C_MV, _C_MO = _piece_segments(IN_SPLITS[:4])
_C_AQ, _C_AK, _C_AV, _C_GM, _C_GA = _piece_segments(IN_SPLITS[5:])


def _split_residues(val, d, out_ref, st_ref):
    t = val.shape[0]
    if d == 1:
        out_ref[0, 0] = val.astype(out_ref.dtype)
        return
    a_ref = st_ref.at[0]
    a_ref[0] = val[:, :LANES]
    a_ref[1] = val[:, LANES:]
    m = t // d
    if d == 16:
        b_ref = st_ref.at[1]
        for r0 in range(4):
            b_ref[0, r0 * 4 * m:(r0 + 1) * 4 * m, :] = a_ref[0, pl.ds(r0, 4 * m, stride=4), :]
            b_ref[1, r0 * 4 * m:(r0 + 1) * 4 * m, :] = a_ref[1, pl.ds(r0, 4 * m, stride=4), :]
        for r0 in range(4):
            for r1 in range(4):
                piece = jnp.concatenate([b_ref[0, pl.ds(r0 * 4 * m + r1, m, stride=4), :],
                                         b_ref[1, pl.ds(r0 * 4 * m + r1, m, stride=4), :]], axis=1)
                out_ref[0, 4 * r1 + r0] = piece.astype(out_ref.dtype)
        return
    for r in range(d):
        piece = jnp.concatenate([a_ref[0, pl.ds(r, m, stride=d), :], a_ref[1, pl.ds(r, m, stride=d), :]], axis=1)
        out_ref[0, r] = piece.astype(out_ref.dtype)


def _merge_residues(ref, d, st_ref):
    if d == 1:
        return ref[0, 0].astype(F32)
    m = ref.shape[2]
    if d == 16:
        a_ref, b_ref = st_ref.at[0], st_ref.at[1]
        for r0 in range(4):
            for r1 in range(4):
                blk = ref[0, 4 * r1 + r0].astype(F32)
                a_ref[0, pl.ds(r0 * 4 * m + r1, m, stride=4), :] = blk[:, :LANES]
                a_ref[1, pl.ds(r0 * 4 * m + r1, m, stride=4), :] = blk[:, LANES:]
        for r0 in range(4):
            b_ref[0, pl.ds(r0, 4 * m, stride=4), :] = a_ref[0, r0 * 4 * m:(r0 + 1) * 4 * m, :]
            b_ref[1, pl.ds(r0, 4 * m, stride=4), :] = a_ref[1, r0 * 4 * m:(r0 + 1) * 4 * m, :]
        return jnp.concatenate([b_ref[0], b_ref[1]], axis=1)
    a_ref = st_ref.at[0]
    for r in range(d):
        blk = ref[0, r].astype(F32)
        a_ref[0, pl.ds(r, m, stride=d), :] = blk[:, :LANES]
        a_ref[1, pl.ds(r, m, stride=d), :] = blk[:, LANES:]
    return jnp.concatenate([a_ref[0], a_ref[1]], axis=1)


def _inproj_body(x_ref, g1_ref, wa_ref, wg_ref, wb_ref, gb_ref, gq_ref, gk_ref,
                 mq_ref, kT_ref, mv_ref, so_ref, gi_ref, gf_ref,
                 q0_ref, q1_ref, q2_ref, k0_ref, k1_ref, k2_ref, v0_ref, v1_ref, v2_ref,
                 sgm_ref, sga_ref, st_ref):
    x = x_ref[...]
    h = x * lax.rsqrt(jnp.mean(x * x, axis=-1, keepdims=True) + EPS) * g1_ref[...]
    hb = h.astype(BF16)

    def seg(w_ref, c):
        return jnp.dot(hb, w_ref[:, c[0]:c[1]], preferred_element_type=F32)

    mq_ref[...] = seg(wa_ref, _C_MQ).astype(BF16)
    mv_ref[...] = seg(wa_ref, _C_MV).astype(BF16)
    so_ref[...] = jax.nn.sigmoid(seg(wa_ref, _C_MO)).astype(BF16)
    hid_r = lax.broadcasted_iota(I32, (A_WIDTH, A_WIDTH), 0) // A_HEAD_DIM
    hid_c = lax.broadcasted_iota(I32, (A_WIDTH, A_WIDTH), 1) // A_HEAD_DIM
    head_ones = (hid_r == hid_c).astype(BF16)
    for c, refs, gain_ref in ((_C_AQ, (q0_ref, q1_ref, q2_ref), gq_ref), (_C_AK, (k0_ref, k1_ref, k2_ref), gk_ref),
                              (_C_AV, (v0_ref, v1_ref, v2_ref), None)):
        val = seg(wb_ref, c)
        for g, ref in enumerate(refs):
            piece = val[:, g * A_WIDTH:(g + 1) * A_WIDTH]
            if gain_ref is not None:
                ss = jnp.dot((piece * piece).astype(BF16), head_ones, preferred_element_type=F32)
                piece = piece * lax.rsqrt(ss * (1.0 / A_HEAD_DIM) + EPS) * gain_ref[:, g * A_WIDTH:(g + 1) * A_WIDTH]
            _split_residues(piece, DILATED_PATTERNS[g][1], ref, st_ref)
    sgm_ref[...] = jax.nn.sigmoid(seg(wb_ref, _C_GM)).astype(BF16)
    sga_ref[...] = jax.nn.sigmoid(seg(wb_ref, _C_GA)).astype(BF16)

    kT_ref[...] = lax.dot_general(wa_ref[:, _C_MK[0]:_C_MK[1]], hb, _TN, preferred_element_type=F32).astype(BF16)
    zg = lax.dot_general(wg_ref[...], hb, _TN, preferred_element_type=F32)
    zi = zg[0:8] + gb_ref[0:8]
    zf = zg[M_HEADS:M_HEADS + 8] + gb_ref[8:16]
    gi_ref[...] = GATE_SOFTCAP * jnp.tanh(zi / GATE_SOFTCAP)
    gf_ref[...] = _log_sigmoid(GATE_SOFTCAP * jnp.tanh(zf / GATE_SOFTCAP))


def _stage_inproj(x2d, norm1_g, w_in, gate_b, gq, gk, batch, seq, tm):
    n = x2d.shape[0]
    steps = seq // tm
    wa = w_in[:, :_A_WIDTH].astype(BF16)
    wg = jnp.pad(w_in[:, _A_WIDTH:_B_START], ((0, 0), (0, LANES - IN_SPLITS[4]))).astype(BF16)
    wb = w_in[:, _B_START:].astype(BF16)
    gb = jnp.zeros((16, 1), F32)
    gb = gb.at[0:4, 0].set(gate_b[:M_HEADS].astype(F32)).at[8:12, 0].set(gate_b[M_HEADS:].astype(F32))
    g1 = norm1_g.astype(F32).reshape(1, D_MODEL)
    gq_t = (jnp.tile(gq.astype(F32), (1, A_HEADS)) * (A_HEAD_DIM ** -0.5)).reshape(1, N_GROUPS * A_WIDTH)
    gk_t = jnp.tile(gk.astype(F32), (1, A_HEADS)).reshape(1, N_GROUPS * A_WIDTH)

    row = lambda w: pl.BlockSpec((tm, w), lambda i: (i, 0))
    rowT = lambda r: pl.BlockSpec((r, tm), lambda i: (0, i))
    full = lambda a: pl.BlockSpec(a.shape, lambda i: (0,) * a.ndim)
    dils = [d for _, d in DILATED_PATTERNS]
    res_shape = lambda d: jax.ShapeDtypeStruct((batch, d, seq // d, A_WIDTH), BF16)
    res_spec = lambda d: pl.BlockSpec((1, d, tm // d, A_WIDTH), lambda i: (i // steps, 0, i % steps, 0))
    out_shapes = (
        jax.ShapeDtypeStruct((n, M_QK_WIDTH), BF16),
        jax.ShapeDtypeStruct((M_QK_WIDTH, n), BF16),
        jax.ShapeDtypeStruct((n, M_WIDTH), BF16),
        jax.ShapeDtypeStruct((n, M_WIDTH), BF16),
        jax.ShapeDtypeStruct((8, n), F32),
        jax.ShapeDtypeStruct((8, n), F32),
        *[res_shape(d) for d in dils], *[res_shape(d) for d in dils], *[res_shape(d) for d in dils],
        jax.ShapeDtypeStruct((n, D_MODEL), BF16),
        jax.ShapeDtypeStruct((n, D_MODEL), BF16),
    )
    out_specs = (row(M_QK_WIDTH), rowT(M_QK_WIDTH), row(M_WIDTH), row(M_WIDTH), rowT(8), rowT(8),
                 *[res_spec(d) for d in dils], *[res_spec(d) for d in dils], *[res_spec(d) for d in dils],
                 row(D_MODEL), row(D_MODEL))
    outs = pl.pallas_call(
        _inproj_body,
        grid=(n // tm,),
        in_specs=[row(D_MODEL), full(g1), full(wa), full(wg), full(wb), full(gb), full(gq_t), full(gk_t)],
        out_specs=out_specs,
        out_shape=out_shapes,
        scratch_shapes=[pltpu.VMEM((2, 2, tm, LANES), F32)],
        compiler_params=_params("parallel"),
        name="inproj",
    )(x2d, g1, wa, wg, wb, gb, gq_t, gk_t)
    mq, kT, mv, so, gi, gf = outs[:6]
    aq, ak, av = outs[6:9], outs[9:12], outs[12:15]
    return mq, kT, mv, so, gi, gf, aq, ak, av, outs[15], outs[16]


M_CHUNK_LEN = 128


def _mlstm_body(q_ref, v_ref, so_ref, ng_ref, *rest, nchunk, nseq):
    kT_refs, gi_refs, gf_refs = rest[0:nseq], rest[nseq:2 * nseq], rest[2 * nseq:3 * nseq]
    o_ref, c_ref, m_ref = rest[3 * nseq:]
    L = M_CHUNK_LEN

    @pl.when(pl.program_id(0) == 0)
    def _():
        c_ref[...] = jnp.zeros_like(c_ref)
        m_ref[...] = jnp.zeros_like(m_ref)

    lane8 = lax.broadcasted_iota(I32, (8, L), 1)
    causal = lax.broadcasted_iota(I32, (L, L), 1) <= lax.broadcasted_iota(I32, (L, L), 0)
    lo_half = lax.broadcasted_iota(I32, (L, LANES), 1) < M_QK_DIM
    ones = jnp.ones((L, M_V_DIM), BF16)

    heads = range(M_HEADS)
    cstate = [[c_ref[s, h * M_QK_DIM:(h + 1) * M_QK_DIM, :] for h in heads] for s in range(nseq)]
    m_prev = [m_ref[s, :, 0:1] for s in range(nseq)]
    units = []
    for c in range(nchunk):
        rows = slice(c * L, (c + 1) * L)
        for s in range(nseq):
            gi = gi_refs[s][:, rows]
            b = gf_refs[s][:, rows]
            sh = 1
            while sh < L:
                b = b + jnp.where(lane8 >= sh, pltpu.roll(b, sh, 1), 0.0)
                sh *= 2
            u = gi - b
            g = b[:, L - 1:L]
            a = g + u
            amax = jnp.max(a, axis=1, keepdims=True)
            m_new = jnp.maximum(g + m_prev[s], amax)
            w = jnp.exp(a - m_new) * (M_QK_DIM ** -0.5)
            s_old = jnp.exp(g + m_prev[s] - m_new)
            vext = [jnp.concatenate([v_ref[s, rows, h * M_V_DIM:(h + 1) * M_V_DIM], ones], axis=1) for h in heads]
            cloc = []
            for h in heads:
                hr = slice(h * M_QK_DIM, (h + 1) * M_QK_DIM)
                kw = (kT_refs[s][hr, rows].astype(F32) * w[h:h + 1, :]).astype(BF16)
                cloc.append(jnp.dot(kw, vext[h], preferred_element_type=F32))
            units.append(dict(seq=s, rows=rows, b=b, u=u, m_prev=m_prev[s], state=cstate[s], vext=vext))
            cstate[s] = [s_old[h:h + 1, :] * cstate[s][h] + cloc[h] for h in heads]
            m_prev[s] = m_new
    for s in range(nseq):
        for h in heads:
            c_ref[s, h * M_QK_DIM:(h + 1) * M_QK_DIM, :] = cstate[s][h]
        m_ref[s] = jnp.broadcast_to(m_prev[s], (8, LANES))

    for un in units:
        s, rows = un["seq"], un["rows"]
        un["s"], un["qc"] = [], []
        for p in range(M_HEADS // 2):
            lanes_p = slice(p * LANES, (p + 1) * LANES)
            q_pair = q_ref[s, rows, lanes_p]
            kT_pair = kT_refs[s][lanes_p, rows]
            c_pair = jnp.concatenate([un["state"][2 * p], un["state"][2 * p + 1]], axis=0).astype(BF16)
            for hh in range(2):
                qm = jnp.where(lo_half if hh == 0 else jnp.logical_not(lo_half), q_pair, jnp.zeros_like(q_pair))
                un["s"].append(jnp.dot(qm, kT_pair, preferred_element_type=F32) * (M_QK_DIM ** -0.5))
                un["qc"].append(jnp.dot(qm, c_pair, preferred_element_type=F32))

    for un in units:
        s, rows, b, u, mp = un["seq"], un["rows"], un["b"], un["u"], un["m_prev"]
        for h in heads:
            hl = slice(h * M_V_DIM, (h + 1) * M_V_DIM)
            bcol = jnp.transpose(jnp.broadcast_to(b[h:h + 1, :], (L, L)))
            dm = jnp.where(causal, bcol + u[h:h + 1, :], -jnp.inf)
            inter = bcol + mp[h:h + 1, :]
            m_t = jnp.maximum(inter, jnp.max(dm, axis=1, keepdims=True))
            pmat = (un["s"][h] * jnp.exp(dm - m_t)).astype(BF16)
            sc = jnp.exp(inter - m_t)
            out = (jnp.dot(pmat, un["vext"][h], preferred_element_type=F32)
                   + jnp.concatenate([sc, sc], axis=1) * un["qc"][h])
            hv = out[:, :M_V_DIM] / jnp.maximum(jnp.abs(out[:, M_V_DIM:]), jnp.exp(-m_t))
            hn = hv * lax.rsqrt(jnp.mean(hv * hv, axis=1, keepdims=True) + EPS)
            hn = hn * ng_ref[:, hl] * so_ref[s, rows, hl].astype(F32)
            o_ref[s, rows, hl] = hn.astype(BF16)


def _stage_mlstm(mq, kT, mv, so, gi, gf, norm_g, batch, seq, rows_per_step):
    n = batch * seq
    R = rows_per_step
    steps = seq // R
    ng = norm_g.astype(F32).reshape(1, M_WIDTH)
    per_seq = lambda a: a.reshape(batch, seq, a.shape[1])
    row = lambda w: pl.BlockSpec((batch, R, w), lambda i: (0, i, 0))
    colT = lambda r, s: pl.BlockSpec((r, R), lambda i, s=s: (0, s * steps + i))
    seqs = range(batch)
    out = pl.pallas_call(
        functools.partial(_mlstm_body, nchunk=R // M_CHUNK_LEN, nseq=batch),
        grid=(steps,),
        in_specs=[row(M_QK_WIDTH), row(M_WIDTH), row(M_WIDTH), pl.BlockSpec((1, M_WIDTH), lambda i: (0, 0)),
                  *[colT(M_QK_WIDTH, s) for s in seqs], *[colT(8, s) for s in seqs], *[colT(8, s) for s in seqs]],
        out_specs=row(M_WIDTH),
        out_shape=jax.ShapeDtypeStruct((batch, seq, M_WIDTH), BF16),
        scratch_shapes=[pltpu.VMEM((batch, M_QK_WIDTH, 2 * M_V_DIM), F32), pltpu.VMEM((batch, 8, LANES), F32)],
        compiler_params=_params("arbitrary"),
        name="mlstm",
    )(per_seq(mq), per_seq(mv), per_seq(so), ng, *[kT] * batch, *[gi] * batch, *[gf] * batch)
    return out.reshape(n, M_WIDTH)


def _attn_body(q_ref, kp_ref, kc_ref, vp_ref, vc_ref, o_ref, lse_ref, *, dil, slopes, lq):
    QB = N_BACK
    first = pl.program_id(2) == 0
    qn = q_ref[0, 0]
    kcn = kc_ref[0, 0]
    kpn = kp_ref[0, 0]
    vc = vc_ref[0, 0]
    vp = vp_ref[0, 0]

    qi = lax.broadcasted_iota(I32, (QB, 2 * QB), 0)
    kj = lax.broadcasted_iota(I32, (QB, 2 * QB), 1)
    dist = qi + QB - kj
    band = jnp.logical_and(dist >= 0, dist <= N_BACK)
    distf = (dist * dil).astype(F32)
    bias = [jnp.where(band, -float(slopes[h]) * distf, -jnp.inf) for h in range(A_HEADS)]
    no_prev = jnp.logical_and(first, kj < QB)
    lo_half = lax.broadcasted_iota(I32, (QB, LANES), 1) < A_HEAD_DIM
    ones = jnp.ones((2 * QB, LANES), BF16)

    units = []
    for j in range(lq // QB):
        rows = slice(j * QB, (j + 1) * QB)
        prow = slice((j - 1) * QB, j * QB)
        keys = jnp.concatenate([kpn if j == 0 else kcn[prow], kcn[rows]], axis=0)
        vals = jnp.concatenate([vp if j == 0 else vc[prow], vc[rows]], axis=0)
        for p in range(A_HEADS // 2):
            lanes_p = slice(p * LANES, (p + 1) * LANES)
            q_pair = qn[rows, lanes_p]
            k_pair = keys[:, lanes_p]
            vext = jnp.concatenate([vals[:, lanes_p], ones], axis=1)
            scores = []
            for hh in range(2):
                sel = lo_half if hh == 0 else jnp.logical_not(lo_half)
                qm = jnp.where(sel, q_pair, jnp.zeros_like(q_pair))
                scores.append(lax.dot_general(qm, k_pair, _NT, preferred_element_type=F32))
            units.append((j, rows, lanes_p, p, vext, scores))

    for j, rows, lanes_p, p, vext, scores in units:
        o_pair = None
        l_pair = None
        for hh in range(2):
            s = scores[hh] + bias[2 * p + hh]
            if j == 0:
                s = jnp.where(no_prev, -jnp.inf, s)
            m = jnp.max(s, axis=1, keepdims=True)
            pv = jnp.dot(jnp.exp(s - m).astype(BF16), vext, preferred_element_type=F32)
            den = pv[:, LANES:]
            o_h = pv[:, :LANES] / den
            l_h = m + jnp.log(den)
            o_pair = o_h if hh == 0 else jnp.where(lo_half, o_pair, o_h)
            l_pair = l_h if hh == 0 else jnp.where(lo_half, l_pair, l_h)
        o_ref[0, 0, rows, lanes_p] = o_pair.astype(BF16)
        lse_ref[0, 0, rows, lanes_p] = l_pair


def _stage_attn(aq, ak, av, batch, seq, group):
    _, dil = DILATED_PATTERNS[group]
    L = seq // dil
    assert L % N_BACK == 0
    lq = min(2048, L)
    nq = L // lq
    sub = lq // N_BACK
    cur = pl.BlockSpec((1, 1, lq, A_WIDTH), lambda b, r, i: (b, r, i, 0))
    prev = pl.BlockSpec((1, 1, N_BACK, A_WIDTH), lambda b, r, i: (b, r, jnp.maximum(i * sub - 1, 0), 0))
    return pl.pallas_call(
        functools.partial(_attn_body, dil=dil, slopes=tuple(_alibi_slopes()[group]), lq=lq),
        grid=(batch, dil, nq),
        in_specs=[cur, prev, cur, prev, cur],
        out_specs=(cur, cur),
        out_shape=(jax.ShapeDtypeStruct((batch, dil, L, A_WIDTH), BF16),
                   jax.ShapeDtypeStruct((batch, dil, L, A_WIDTH), F32)),
        compiler_params=_params("parallel", "parallel", "parallel"),
        name=f"dilated_attn_d{dil}",
    )(aq, ak, ak, av, av)


PACK_ROWS = D_MODEL // (2 * LANES)
U32 = jnp.uint32
_HIGH_HALF = 0xFFFF0000


def _pack_rows(val, exact=False):
    half = D_MODEL // 2
    lo, hi = val[:, :half], val[:, half:]
    if exact:
        return (lax.bitcast_convert_type(lo, U32) >> 16) | lax.bitcast_convert_type(hi, U32)

    def bits(v):
        return lax.bitcast_convert_type(v.astype(BF16).astype(F32), U32)

    return (bits(lo) >> 16) | (bits(hi) & U32(_HIGH_HALF))


def _unpack_rows(words):
    lo = lax.bitcast_convert_type(words << 16, F32).astype(BF16)
    hi = lax.bitcast_convert_type(words & U32(_HIGH_HALF), F32).astype(BF16)
    return jnp.concatenate([lo, hi], axis=1)


def _flat(ref):
    rows = 1
    for d in ref.shape[:-2]:
        rows *= d
    return ref.reshape(rows * PACK_ROWS, LANES)


def _store_packed(flat_ref, row0, words):
    t = words.shape[0]
    for s in range(PACK_ROWS):
        flat_ref[pl.ds(row0 * PACK_ROWS + s, t, stride=PACK_ROWS), :] = words[:, s * LANES:(s + 1) * LANES]


def _load_packed(flat_ref, row0, t):
    return jnp.concatenate([flat_ref[pl.ds(row0 * PACK_ROWS + s, t, stride=PACK_ROWS), :] for s in range(PACK_ROWS)],
                           axis=1)


def _rows8(vals):
    t = vals[0].shape[1]
    rid = lax.broadcasted_iota(I32, (8, t), 0)
    out = jnp.zeros((8, t), vals[0].dtype)
    for k, v in enumerate(vals):
        out = jnp.where(rid == k, jnp.broadcast_to(v, (8, t)), out)
    return out


def _merge_body(hm_ref, o1_ref, o2_ref, o3_ref, l1_ref, l2_ref, l3_ref, sgm_ref, sga_ref, x_ref,
                wm_ref, wa_ref, wo_ref, g2_ref, wrh_ref, br_ref,
                x2_ref, xn_ref, loc_ref, gate_ref, tcnt_ref, tcar_ref, cnt_ref, carry_ref, st_ref):
    @pl.when(pl.program_id(0) == 0)
    def _():
        carry_ref[...] = jnp.zeros_like(carry_ref)

    m_branch = jnp.dot(hm_ref[...], wm_ref[...], preferred_element_type=F32)
    dils = [d for _, d in DILATED_PATTERNS]
    l1, l2, l3 = [_merge_residues(r, d, st_ref) for r, d in zip((l1_ref, l2_ref, l3_ref), dils)]
    lmax = jnp.maximum(jnp.maximum(l1, l2), l3)
    e1, e2, e3 = jnp.exp(l1 - lmax), jnp.exp(l2 - lmax), jnp.exp(l3 - lmax)
    num = e1 * _merge_residues(o1_ref, dils[0], st_ref)
    num = num + e2 * _merge_residues(o2_ref, dils[1], st_ref)
    num = num + e3 * _merge_residues(o3_ref, dils[2], st_ref)
    h_a = num / (e1 + e2 + e3)
    y = (sgm_ref[...].astype(F32) * m_branch
         + sga_ref[...].astype(F32) * jnp.dot(h_a.astype(BF16), wa_ref[...], preferred_element_type=F32))
    x2 = x_ref[...] + jnp.dot(y.astype(BF16), wo_ref[...], preferred_element_type=F32)
    x2_ref[...] = x2
    xn = x2 * lax.rsqrt(jnp.mean(x2 * x2, axis=-1, keepdims=True) + EPS) * g2_ref[...]
    xh = xn.astype(BF16)
    xn_ref[...] = xh

    logits = lax.dot_general(wrh_ref[...], xh, _NT, preferred_element_type=F32) + br_ref[...]
    t = logits.shape[1]
    eid = lax.broadcasted_iota(I32, (N_EXPERTS, t), 0).astype(F32)
    vals = logits
    top_v, top_i = [], []
    for _ in range(TOP_K):
        mx = jnp.max(vals, axis=0, keepdims=True)
        ik = jnp.min(jnp.where(vals == mx, eid, float(N_EXPERTS)), axis=0, keepdims=True)
        top_v.append(mx)
        top_i.append(ik)
        vals = jnp.where(eid == ik, -jnp.inf, vals)
    ex = [jnp.exp(v - top_v[0]) for v in top_v]
    den = ex[0] + ex[1] + ex[2] + ex[3]
    gate_ref[...] = _rows8([e / den for e in ex])

    chosen = jnp.zeros((N_EXPERTS, t), F32)
    for ik in top_i:
        chosen = chosen + (eid == ik).astype(F32)
    before = jnp.where(lax.broadcasted_iota(jnp.int16, (t, t), 0) < lax.broadcasted_iota(jnp.int16, (t, t), 1),
                       jnp.ones((), BF16), jnp.zeros((), BF16))
    prefix = jnp.dot(chosen.astype(BF16), before, preferred_element_type=F32)
    tcount = jnp.broadcast_to(jnp.sum(chosen, axis=1, keepdims=True), (N_EXPERTS, LANES))
    below = (lax.broadcasted_iota(I32, (N_EXPERTS, N_EXPERTS), 1)
             < lax.broadcasted_iota(I32, (N_EXPERTS, N_EXPERTS), 0)).astype(BF16)
    t_hi = jnp.floor(tcount * (1.0 / BF16_EXACT_INT)) * BF16_EXACT_INT
    tile_off = (jnp.dot(below, t_hi.astype(BF16), preferred_element_type=F32)
                + jnp.dot(below, (tcount - t_hi).astype(BF16), preferred_element_type=F32))
    pos = prefix + tile_off[:, 0:1]
    loc_ref[...] = _rows8([jnp.sum(jnp.where(eid == ik, pos, 0.0), axis=0, keepdims=True).astype(I32)
                           for ik in top_i])
    carry = carry_ref[...]
    tcnt_ref[...] = tcount.astype(I32)
    tcar_ref[...] = carry.astype(I32)
    total = carry + tcount
    carry_ref[...] = total
    cnt_ref[...] = total


def _stage_merge(h_m, attn, sgm, sga, x2d, w_mb, w_ab, w_out, norm2_g, w_router, b_router, batch, seq, tm):
    n = x2d.shape[0]
    steps = seq // tm
    (o1, l1), (o2, l2), (o3, l3) = attn
    wm = w_mb.astype(BF16)
    wa = w_ab.astype(BF16)
    wo = w_out.astype(BF16)
    g2 = norm2_g.astype(F32).reshape(1, D_MODEL)
    wrh = w_router.astype(BF16).T
    br = b_router.astype(F32).reshape(N_EXPERTS, 1)
    row = lambda w: pl.BlockSpec((tm, w), lambda i: (i, 0))
    rowT = lambda r: pl.BlockSpec((r, tm), lambda i: (0, i))
    full = lambda a: pl.BlockSpec(a.shape, lambda i: (0,) * a.ndim)
    res = lambda d: pl.BlockSpec((1, d, tm // d, A_WIDTH), lambda i: (i // steps, 0, i % steps, 0))
    dils = [d for _, d in DILATED_PATTERNS]
    per_tile = pl.BlockSpec((N_EXPERTS, LANES), lambda i: (0, i))
    return pl.pallas_call(
        _merge_body,
        grid=(n // tm,),
        in_specs=[row(M_WIDTH), *[res(d) for d in dils], *[res(d) for d in dils],
                  row(D_MODEL), row(D_MODEL), row(D_MODEL),
                  full(wm), full(wa), full(wo), full(g2), full(wrh), full(br)],
        out_specs=(row(D_MODEL), row(D_MODEL), rowT(8), rowT(8), per_tile, per_tile,
                   pl.BlockSpec((N_EXPERTS, LANES), lambda i: (0, 0))),
        out_shape=(jax.ShapeDtypeStruct((n, D_MODEL), F32),
                   jax.ShapeDtypeStruct((n, D_MODEL), BF16),
                   jax.ShapeDtypeStruct((8, n), I32),
                   jax.ShapeDtypeStruct((8, n), F32),
                   jax.ShapeDtypeStruct((N_EXPERTS, (n // tm) * LANES), I32),
                   jax.ShapeDtypeStruct((N_EXPERTS, (n // tm) * LANES), I32),
                   jax.ShapeDtypeStruct((N_EXPERTS, LANES), F32)),
        scratch_shapes=[pltpu.VMEM((N_EXPERTS, LANES), F32), pltpu.VMEM((2, 2, tm, LANES), F32)],
        compiler_params=_params("arbitrary"),
        name="merge_route",
    )(h_m, o1, o2, o3, l1, l2, l3, sgm, sga, x2d, wm, wa, wo, g2, wrh, br)


def _offsets_body(cnt_ref, blk_ref, pstart_ref, zlo_ref, zhi_ref, *, nblk_pad):
    cnt = cnt_ref[...]
    padded = jnp.floor((cnt + (MOE_BLOCK - 1)) * (1.0 / MOE_BLOCK)) * MOE_BLOCK
    lower = (lax.broadcasted_iota(I32, (N_EXPERTS, N_EXPERTS), 1)
             <= lax.broadcasted_iota(I32, (N_EXPERTS, N_EXPERTS), 0)).astype(BF16)
    nb = padded * (1.0 / MOE_BLOCK)
    nb_hi = jnp.floor(nb * (1.0 / BF16_EXACT_INT)) * BF16_EXACT_INT
    pends = (jnp.dot(lower, nb_hi.astype(BF16), preferred_element_type=F32)
             + jnp.dot(lower, (nb - nb_hi).astype(BF16), preferred_element_type=F32)) * MOE_BLOCK
    pstart = pends - padded
    pstart_ref[...] = pstart.astype(I32)
    zlo_ref[...] = (pstart + cnt).astype(I32)
    zhi_ref[...] = pends.astype(I32)

    first_row = (lax.broadcasted_iota(I32, (N_EXPERTS, nblk_pad), 1) * MOE_BLOCK).astype(F32)
    pe = jnp.broadcast_to(pends[:, 0:1], (N_EXPERTS, nblk_pad))
    be = jnp.sum((pe <= first_row).astype(F32), axis=0, keepdims=True)
    be = jnp.minimum(be, float(N_EXPERTS - 1))
    nused = pends[N_EXPERTS - 1:N_EXPERTS, 0:1] * (1.0 / MOE_BLOCK)
    nonempty = jnp.broadcast_to(padded[:, 0:1], (N_EXPERTS, nblk_pad)) > 0.0
    runidx = jnp.sum(jnp.logical_and(pe <= first_row, nonempty).astype(F32), axis=0, keepdims=True)
    parity = runidx - 2.0 * jnp.floor(runidx * 0.5)
    eid = lax.broadcasted_iota(I32, (N_EXPERTS, nblk_pad), 0).astype(F32)
    later = jnp.logical_and(eid > be, nonempty)
    nxt = jnp.min(jnp.where(later, eid, float(N_EXPERTS)), axis=0, keepdims=True)
    blk_ref[...] = _rows8([be.astype(I32), jnp.broadcast_to(nused, (1, nblk_pad)).astype(I32),
                           parity.astype(I32), nxt.astype(I32)])


def _stage_offsets(cnt, nblk):
    nblk_pad = -(-nblk // LANES) * LANES
    const = lambda r, c: pl.BlockSpec((r, c), lambda i: (0, 0))
    per_expert = jax.ShapeDtypeStruct((N_EXPERTS, LANES), I32)
    return pl.pallas_call(
        functools.partial(_offsets_body, nblk_pad=nblk_pad),
        grid=(1,),
        in_specs=[const(N_EXPERTS, LANES)],
        out_specs=(const(8, nblk_pad), const(N_EXPERTS, LANES), const(N_EXPERTS, LANES), const(N_EXPERTS, LANES)),
        out_shape=(jax.ShapeDtypeStruct((8, nblk_pad), I32), per_expert, per_expert, per_expert),
        compiler_params=_params("arbitrary"),
        name="route_offsets",
    )(cnt)


RUN_BITS = 10


def _tile_rows(ref, first_row, nrows):
    return ref.at[pl.ds(first_row, nrows)]


def _for_each_piece(length, fn):
    for b in reversed(range(RUN_BITS)):
        @pl.when(((length >> b) & 1) == 1)
        def _(b=b):
            fn((length >> (b + 1)) << (b + 1), 1 << b)


def _for_each_run(tile, tcnt_ref, tcar_ref, pstart_ref, fn):
    def per_expert(e, local):
        count = tcnt_ref[tile, e]
        first = pstart_ref[e] + tcar_ref[tile, e]
        _for_each_piece(count, lambda off, size: fn(local + off, first + off, size))
        return local + count

    lax.fori_loop(0, N_EXPERTS, per_expert, 0)


PERM_CHUNK = 256


def _dispatch_body(tcnt_ref, tcar_ref, pstart_ref, zlo_ref, zhi_ref, loc_ref, xn_ref, xs_hbm,
                   buf_ref, sems, *, tm):
    step = pl.program_id(0)
    nloc = TOP_K * tm

    def wait_buffer(slot):
        pltpu.make_async_copy(buf_ref.at[slot], xs_hbm.at[pl.ds(0, nloc)], sems.at[slot]).wait()

    for slot in range(2):
        tile = 2 * step + slot

        @pl.when(step > 0)
        def _(slot=slot):
            wait_buffer(slot)

        loc = loc_ref[:, slot * tm:(slot + 1) * tm].astype(jnp.int16)
        xn = xn_ref[slot * tm:(slot + 1) * tm, :]
        for c in range(TOP_K * tm // PERM_CHUNK):
            lid = lax.broadcasted_iota(jnp.int16, (PERM_CHUNK, tm), 0) + jnp.int16(c * PERM_CHUNK)
            hit = lid == loc[0:1, :]
            for k in range(1, TOP_K):
                hit = jnp.logical_or(hit, lid == loc[k:k + 1, :])
            perm = jnp.where(hit, jnp.ones((), BF16), jnp.zeros((), BF16))
            rows = jnp.dot(perm, xn, preferred_element_type=F32)
            _store_packed(_flat(buf_ref), slot * nloc + c * PERM_CHUNK, _pack_rows(rows, exact=True))

        def run_copy(local, first, size, slot=slot):
            return pltpu.make_async_copy(_tile_rows(buf_ref.at[slot], local, size),
                                         _tile_rows(xs_hbm, first, size), sems.at[slot])

        _for_each_run(tile, tcnt_ref, tcar_ref, pstart_ref, lambda l, f, s: run_copy(l, f, s).start())

    @pl.when(step == pl.num_programs(0) - 1)
    def _():
        wait_buffer(0)
        wait_buffer(1)
        zsrc = buf_ref.at[0]
        zsrc[pl.ds(0, MOE_BLOCK)] = jnp.zeros((MOE_BLOCK, PACK_ROWS, LANES), U32)

        def zero_copy(first, size):
            return pltpu.make_async_copy(_tile_rows(zsrc, 0, size), _tile_rows(xs_hbm, first, size), sems.at[0])

        def per_expert(e, carry):
            lo = zlo_ref[e]
            npad = zhi_ref[e] - lo
            _for_each_piece(npad, lambda off, size: zero_copy(lo + off, size).start())
            _for_each_piece(npad, lambda off, size: zero_copy(lo + off, size).wait())
            return carry

        lax.fori_loop(0, N_EXPERTS, per_expert, 0)

        first_unused = zhi_ref[N_EXPERTS - 1] // MOE_BLOCK
        nblk = xs_hbm.shape[0] // MOE_BLOCK

        def tail(blk, carry):
            zero_copy(blk * MOE_BLOCK, MOE_BLOCK).start()
            zero_copy(blk * MOE_BLOCK, MOE_BLOCK).wait()
            return carry

        lax.fori_loop(first_unused, nblk, tail, 0)


def _stage_dispatch(tables, loc8, xn, nrows, tm):
    n = xn.shape[0]
    assert TOP_K * tm >= MOE_BLOCK and (n // tm) % 2 == 0
    grid_spec = pltpu.PrefetchScalarGridSpec(
        num_scalar_prefetch=5,
        grid=(n // (2 * tm),),
        in_specs=[pl.BlockSpec((8, 2 * tm), lambda i, *_: (0, i)),
                  pl.BlockSpec((2 * tm, D_MODEL), lambda i, *_: (i, 0))],
        out_specs=pl.BlockSpec(memory_space=pl.ANY),
        scratch_shapes=[pltpu.VMEM((2, TOP_K * tm, PACK_ROWS, LANES), U32), pltpu.SemaphoreType.DMA((2,))],
    )
    return pl.pallas_call(
        functools.partial(_dispatch_body, tm=tm),
        grid_spec=grid_spec,
        out_shape=jax.ShapeDtypeStruct((nrows, PACK_ROWS, LANES), U32),
        compiler_params=_params("arbitrary"),
        name="dispatch",
    )(*tables, loc8, xn)


EXPERT_BLOCKS_PER_STEP = 2


def _expert_body(be_ref, nu_ref, par_ref, nxt_ref, xs_ref, w1_hbm, b1_ref, w2_hbm, b2_ref, ys_ref,
                 w1f_ref, w2f_ref, w1b_ref, w2b_ref, sems):
    def fetch(expert, slot):
        return (pltpu.make_async_copy(w1_hbm.at[expert], w1f_ref.at[slot], sems.at[slot, 0]),
                pltpu.make_async_copy(w2_hbm.at[expert], w2f_ref.at[slot], sems.at[slot, 1]))

    for sub in range(EXPERT_BLOCKS_PER_STEP):
        j = pl.program_id(0) * EXPERT_BLOCKS_PER_STEP + sub
        used = j < nu_ref[0]
        jj = jnp.maximum(jnp.minimum(j, nu_ref[0] - 1), 0)
        e = be_ref[jj]
        fresh = jnp.logical_or(j == 0, e != be_ref[jnp.maximum(jj - 1, 0)])

        @pl.when(jnp.logical_and(used, fresh))
        def _(j=j, jj=jj, e=e):
            slot = par_ref[jj]

            @pl.when(j == 0)
            def _():
                for c in fetch(e, slot):
                    c.start()

            for c in fetch(e, slot):
                c.wait()
            nxt = nxt_ref[jj]

            @pl.when(nxt < N_EXPERTS)
            def _():
                for c in fetch(nxt, 1 - slot):
                    c.start()

            w1b_ref[...] = w1f_ref[slot].astype(BF16)
            w2b_ref[...] = w2f_ref[slot].astype(BF16)

        @pl.when(used)
        def _(e=e, sub=sub):
            xb = _unpack_rows(_load_packed(_flat(xs_ref), sub * MOE_BLOCK, MOE_BLOCK))
            gu = jnp.dot(xb, w1b_ref[...], preferred_element_type=F32) + b1_ref[pl.ds(e, 1), :]
            gate = jnp.minimum(gu[:, :D_FF], SWIGLU_LIMIT)
            lin = jnp.clip(gu[:, D_FF:], -SWIGLU_LIMIT, SWIGLU_LIMIT)
            act = (lin + 1.0) * (gate * jax.nn.sigmoid(SWIGLU_ALPHA * gate))
            ys = jnp.dot(act.astype(BF16), w2b_ref[...], preferred_element_type=F32) + b2_ref[pl.ds(e, 1), :]
            _store_packed(_flat(ys_ref), sub * MOE_BLOCK, _pack_rows(ys))

        @pl.when(jnp.logical_not(used))
        def _(sub=sub):
            ys_ref[pl.ds(sub * MOE_BLOCK, MOE_BLOCK)] = jnp.zeros((MOE_BLOCK, PACK_ROWS, LANES), U32)


def _stage_experts(blk8, xs, w1, b1, w2, b2):
    nrows = xs.shape[0]
    nblk = nrows // MOE_BLOCK
    assert nblk % EXPERT_BLOCKS_PER_STEP == 0
    block_e, nused, parity, nxt = blk8[0, :nblk], blk8[1, :1], blk8[2, :nblk], blk8[3, :nblk]
    tiles = (EXPERT_BLOCKS_PER_STEP * MOE_BLOCK, PACK_ROWS, LANES)
    full = lambda a: pl.BlockSpec(a.shape, lambda j, *_: (0,) * a.ndim)
    grid_spec = pltpu.PrefetchScalarGridSpec(
        num_scalar_prefetch=4,
        grid=(nblk // EXPERT_BLOCKS_PER_STEP,),
        in_specs=[pl.BlockSpec(tiles, lambda j, *_: (j, 0, 0)),
                  pl.BlockSpec(memory_space=pl.ANY), full(b1),
                  pl.BlockSpec(memory_space=pl.ANY), full(b2)],
        out_specs=pl.BlockSpec(tiles, lambda j, *_: (j, 0, 0)),
        scratch_shapes=[pltpu.VMEM((2, D_MODEL, 2 * D_FF), F32), pltpu.VMEM((2, D_FF, D_MODEL), F32),
                        pltpu.VMEM((D_MODEL, 2 * D_FF), BF16), pltpu.VMEM((D_FF, D_MODEL), BF16),
                        pltpu.SemaphoreType.DMA((2, 2))],
    )
    return pl.pallas_call(
        _expert_body,
        grid_spec=grid_spec,
        out_shape=jax.ShapeDtypeStruct((nrows, PACK_ROWS, LANES), U32),
        compiler_params=_params("arbitrary"),
        name="experts",
    )(block_e, nused, parity, nxt, xs, w1, b1, w2, b2)


COMBINE_CHUNK = 512


def _combine_body(tcnt_ref, tcar_ref, pstart_ref, loc_ref, gate_ref, x2_ref, ys_hbm, out_ref,
                  buf_ref, g_ref, sems, *, tm):
    step = pl.program_id(0)
    nloc = TOP_K * tm

    def start_runs(tile, slot):
        def run_copy(local, first, size):
            return pltpu.make_async_copy(_tile_rows(ys_hbm, first, size),
                                         _tile_rows(buf_ref.at[slot], local, size), sems.at[slot])
        _for_each_run(tile, tcnt_ref, tcar_ref, pstart_ref, lambda l, f, s: run_copy(l, f, s).start())

    def wait_buffer(slot):
        pltpu.make_async_copy(ys_hbm.at[pl.ds(0, nloc)], buf_ref.at[slot], sems.at[slot]).wait()

    def combine(slot):
        zpad = jnp.zeros((LANES - 16, LANES), F32)
        cols = []
        for c in range(tm // LANES):
            cols_in = slice(slot * tm + c * LANES, slot * tm + (c + 1) * LANES)
            cols.append(jnp.transpose(jnp.concatenate([loc_ref[:, cols_in].astype(F32), gate_ref[:, cols_in], zpad],
                                                      axis=0)))
        rows = slice(slot * tm, (slot + 1) * tm)
        acc = x2_ref[rows, :]
        for lc in range(nloc // COMBINE_CHUNK):
            lane = lax.broadcasted_iota(jnp.int16, (LANES, COMBINE_CHUNK), 1) + jnp.int16(lc * COMBINE_CHUNK)
            for c in range(tm // LANES):
                g = jnp.zeros((LANES, COMBINE_CHUNK), BF16)
                for k in range(TOP_K):
                    g = jnp.where(lane == cols[c][:, k:k + 1].astype(jnp.int16), cols[c][:, 8 + k:9 + k].astype(BF16), g)
                g_ref[c * LANES:(c + 1) * LANES, lc * COMBINE_CHUNK:(lc + 1) * COMBINE_CHUNK] = g
            if lc == 0:
                wait_buffer(slot)
            ys = _unpack_rows(_load_packed(_flat(buf_ref), slot * nloc + lc * COMBINE_CHUNK, COMBINE_CHUNK))
            acc = acc + jnp.dot(g_ref[:, lc * COMBINE_CHUNK:(lc + 1) * COMBINE_CHUNK], ys, preferred_element_type=F32)
        out_ref[rows, :] = acc

    @pl.when(step == 0)
    def _():
        start_runs(0, 0)

    start_runs(2 * step + 1, 1)
    combine(0)

    @pl.when(step + 1 < pl.num_programs(0))
    def _():
        start_runs(2 * step + 2, 0)

    combine(1)


def _stage_combine(tables, loc8, gate8, x2, ys, tm):
    n = x2.shape[0]
    assert (n // tm) % 2 == 0
    grid_spec = pltpu.PrefetchScalarGridSpec(
        num_scalar_prefetch=3,
        grid=(n // (2 * tm),),
        in_specs=[pl.BlockSpec((8, 2 * tm), lambda i, *_: (0, i)),
                  pl.BlockSpec((8, 2 * tm), lambda i, *_: (0, i)),
                  pl.BlockSpec((2 * tm, D_MODEL), lambda i, *_: (i, 0)),
                  pl.BlockSpec(memory_space=pl.ANY)],
        out_specs=pl.BlockSpec((2 * tm, D_MODEL), lambda i, *_: (i, 0)),
        scratch_shapes=[pltpu.VMEM((2, TOP_K * tm, PACK_ROWS, LANES), U32),
                        pltpu.VMEM((tm, TOP_K * tm), BF16),
                        pltpu.SemaphoreType.DMA((2,))],
    )
    return pl.pallas_call(
        functools.partial(_combine_body, tm=tm),
        grid_spec=grid_spec,
        out_shape=jax.ShapeDtypeStruct((n, D_MODEL), F32),
        compiler_params=_params("arbitrary"),
        name="combine",
    )(*tables, loc8, gate8, x2, ys)


def _moe(x2, xn, loc8, gate8, tcnt, tcar, cnt, w1, b1, w2, b2, tm):
    n = x2.shape[0]
    ntile = n // tm
    nblk = -(-(n * TOP_K) // MOE_BLOCK) + N_EXPERTS
    blk8, pstart, zlo, zhi = _stage_offsets(cnt, nblk)
    per_tile = lambda a: a.reshape(N_EXPERTS, ntile, LANES)[:, :, 0].T
    tables = (per_tile(tcnt), per_tile(tcar), pstart[:, 0])
    xs = _stage_dispatch(tables + (zlo[:, 0], zhi[:, 0]), loc8, xn, nblk * MOE_BLOCK, tm)
    ys = _stage_experts(blk8, xs, w1, b1, w2, b2)
    return _stage_combine(tables, loc8, gate8, x2, ys, tm)


def kernel(x, norm1_g, w_in, mlstm_gate_b, mlstm_norm_g, attn_q_norm_g, attn_k_norm_g, w_mlstm_branch,
           w_attn_branch, w_out, norm2_g, w_router, b_router, w1, b1, w2, b2):
    batch, seq, _ = x.shape
    n = batch * seq
    for l in range(norm1_g.shape[0]):
        x2d = x.reshape(n, D_MODEL)
        tm = min(512, seq)
        mq, kT, mv, so, gi, gf, aq, ak, av, sgm, sga = _stage_inproj(
            x2d, norm1_g[l], w_in[l], mlstm_gate_b[l], attn_q_norm_g[l], attn_k_norm_g[l], batch, seq, tm)
        h_m = _stage_mlstm(mq, kT, mv, so, gi, gf, mlstm_norm_g[l], batch, seq, tm)
        attn = [_stage_attn(aq[g], ak[g], av[g], batch, seq, g)
                for g in range(N_GROUPS)]
        x2, xn, loc8, gate8, tcnt, tcar, cnt = _stage_merge(
            h_m, attn, sgm, sga, x2d, w_mlstm_branch[l], w_attn_branch[l], w_out[l], norm2_g[l],
            w_router[l], b_router[l], batch, seq, tm)
        out = _moe(x2, xn, loc8, gate8, tcnt, tcar, cnt, w1[l], b1[l], w2[l], b2[l], tm)
        x = out.reshape(batch, seq, D_MODEL)
    return x
```

```python
import functools

import numpy as np
import jax
import jax.numpy as jnp
from jax import lax
from jax.experimental import pallas as pl
from jax.experimental.pallas import tpu as pltpu

F32 = jnp.float32
BF16 = jnp.bfloat16
I32 = jnp.int32

D_MODEL = 1024
M_HEADS = 4
M_QK_DIM = 64
M_V_DIM = 128
GATE_SOFTCAP = 15.0
A_HEADS = 4
A_HEAD_DIM = 64
DILATED_PATTERNS = ((128, 1), (512, 4), (2048, 16))
N_GROUPS = len(DILATED_PATTERNS)
N_BACK = 128
N_EXPERTS = 32
TOP_K = 4
D_FF = 1024
SWIGLU_LIMIT = 7.0
SWIGLU_ALPHA = 1.702
MOE_BLOCK = 512
EPS = 1e-6

M_WIDTH = M_HEADS * M_V_DIM
M_QK_WIDTH = M_HEADS * M_QK_DIM
A_WIDTH = A_HEADS * A_HEAD_DIM
IN_SPLITS = (M_QK_WIDTH, M_QK_WIDTH, M_WIDTH, M_WIDTH, 2 * M_HEADS,
             N_GROUPS * A_WIDTH, N_GROUPS * A_WIDTH, N_GROUPS * A_WIDTH, D_MODEL, D_MODEL)

LANES = 128
VMEM_LIMIT = 56 * 1024 * 1024

BF16_EXACT_INT = 256.0

_NT = (((1,), (1,)), ((), ()))
_TN = (((0,), (1,)), ((), ()))


def _alibi_slopes():
    n = N_GROUPS * A_HEADS
    s = np.exp2(-8.0 * np.arange(1, n + 1) / n).astype(np.float32)
    return s.reshape(N_GROUPS, A_HEADS)


def _params(*sem):
    return pltpu.CompilerParams(dimension_semantics=sem, vmem_limit_bytes=VMEM_LIMIT)


def _log_sigmoid(x):
    return jnp.minimum(x, 0.0) - jnp.log1p(jnp.exp(-jnp.abs(x)))


_A_WIDTH = sum(IN_SPLITS[:4])
_B_START = _A_WIDTH + IN_SPLITS[4]


def _piece_segments(widths):
    bounds, start = [], 0
    for width in widths:
        bounds.append((start, start + width))
        start += width
    return bounds


_C_MQ, _C_MK, _C_MV, _C_MO = _piece_segments(IN_SPLITS[:4])
_C_AQ, _C_AK, _C_AV, _C_GM, _C_GA = _piece_segments(IN_SPLITS[5:])


def _split_residues(val, d, out_ref, st_ref):
    t = val.shape[0]
    if d == 1:
        out_ref[0, 0] = val.astype(out_ref.dtype)
        return
    a_ref = st_ref.at[0]
    a_ref[0] = val[:, :LANES]
    a_ref[1] = val[:, LANES:]
    m = t // d
    if d == 16:
        b_ref = st_ref.at[1]
        for r0 in range(4):
            b_ref[0, r0 * 4 * m:(r0 + 1) * 4 * m, :] = a_ref[0, pl.ds(r0, 4 * m, stride=4), :]
            b_ref[1, r0 * 4 * m:(r0 + 1) * 4 * m, :] = a_ref[1, pl.ds(r0, 4 * m, stride=4), :]
        for r0 in range(4):
            for r1 in range(4):
                piece = jnp.concatenate([b_ref[0, pl.ds(r0 * 4 * m + r1, m, stride=4), :],
                                         b_ref[1, pl.ds(r0 * 4 * m + r1, m, stride=4), :]], axis=1)
                out_ref[0, 4 * r1 + r0] = piece.astype(out_ref.dtype)
        return
    for r in range(d):
        piece = jnp.concatenate([a_ref[0, pl.ds(r, m, stride=d), :], a_ref[1, pl.ds(r, m, stride=d), :]], axis=1)
        out_ref[0, r] = piece.astype(out_ref.dtype)


def _merge_residues(ref, d, st_ref):
    if d == 1:
        return ref[0, 0].astype(F32)
    m = ref.shape[2]
    if d == 16:
        a_ref, b_ref = st_ref.at[0], st_ref.at[1]
        for r0 in range(4):
            for r1 in range(4):
                blk = ref[0, 4 * r1 + r0].astype(F32)
                a_ref[0, pl.ds(r0 * 4 * m + r1, m, stride=4), :] = blk[:, :LANES]
                a_ref[1, pl.ds(r0 * 4 * m + r1, m, stride=4), :] = blk[:, LANES:]
        for r0 in range(4):
            b_ref[0, pl.ds(r0, 4 * m, stride=4), :] = a_ref[0, r0 * 4 * m:(r0 + 1) * 4 * m, :]
            b_ref[1, pl.ds(r0, 4 * m, stride=4), :] = a_ref[1, r0 * 4 * m:(r0 + 1) * 4 * m, :]
        return jnp.concatenate([b_ref[0], b_ref[1]], axis=1)
    a_ref = st_ref.at[0]
    for r in range(d):
        blk = ref[0, r].astype(F32)
        a_ref[0, pl.ds(r, m, stride=d), :] = blk[:, :LANES]
        a_ref[1, pl.ds(r, m, stride=d), :] = blk[:, LANES:]
    return jnp.concatenate([a_ref[0], a_ref[1]], axis=1)


def _inproj_body(x_ref, g1_ref, wa_ref, wg_ref, wb_ref, gb_ref, gq_ref, gk_ref,
                 mq_ref, kT_ref, mv_ref, so_ref, gi_ref, gf_ref,
                 q0_ref, q1_ref, q2_ref, k0_ref, k1_ref, k2_ref, v0_ref, v1_ref, v2_ref,
                 sgm_ref, sga_ref, st_ref):
    x = x_ref[...]
    h = x * lax.rsqrt(jnp.mean(x * x, axis=-1, keepdims=True) + EPS) * g1_ref[...]
    hb = h.astype(BF16)

    def seg(w_ref, c):
        return jnp.dot(hb, w_ref[:, c[0]:c[1]], preferred_element_type=F32)

    mq_ref[...] = seg(wa_ref, _C_MQ).astype(BF16)
    mv_ref[...] = seg(wa_ref, _C_MV).astype(BF16)
    so_ref[...] = jax.nn.sigmoid(seg(wa_ref, _C_MO)).astype(BF16)
    hid_r = lax.broadcasted_iota(I32, (A_WIDTH, A_WIDTH), 0) // A_HEAD_DIM
    hid_c = lax.broadcasted_iota(I32, (A_WIDTH, A_WIDTH), 1) // A_HEAD_DIM
    head_ones = (hid_r == hid_c).astype(BF16)
    for c, refs, gain_ref in ((_C_AQ, (q0_ref, q1_ref, q2_ref), gq_ref), (_C_AK, (k0_ref, k1_ref, k2_ref), gk_ref),
                              (_C_AV, (v0_ref, v1_ref, v2_ref), None)):
        val = seg(wb_ref, c)
        for g, ref in enumerate(refs):
            piece = val[:, g * A_WIDTH:(g + 1) * A_WIDTH]
            if gain_ref is not None:
                ss = jnp.dot((piece * piece).astype(BF16), head_ones, preferred_element_type=F32)
                piece = piece * lax.rsqrt(ss * (1.0 / A_HEAD_DIM) + EPS) * gain_ref[:, g * A_WIDTH:(g + 1) * A_WIDTH]
            _split_residues(piece, DILATED_PATTERNS[g][1], ref, st_ref)
    sgm_ref[...] = jax.nn.sigmoid(seg(wb_ref, _C_GM)).astype(BF16)
    sga_ref[...] = jax.nn.sigmoid(seg(wb_ref, _C_GA)).astype(BF16)

    kT_ref[...] = lax.dot_general(wa_ref[:, _C_MK[0]:_C_MK[1]], hb, _TN, preferred_element_type=F32).astype(BF16)
    zg = lax.dot_general(wg_ref[...], hb, _TN, preferred_element_type=F32)
    zi = zg[0:8] + gb_ref[0:8]
    zf = zg[M_HEADS:M_HEADS + 8] + gb_ref[8:16]
    gi_ref[...] = GATE_SOFTCAP * jnp.tanh(zi / GATE_SOFTCAP)
    gf_ref[...] = _log_sigmoid(GATE_SOFTCAP * jnp.tanh(zf / GATE_SOFTCAP))


def _stage_inproj(x2d, norm1_g, w_in, gate_b, gq, gk, batch, seq, tm):
    n = x2d.shape[0]
    steps = seq // tm
    wa = w_in[:, :_A_WIDTH].astype(BF16)
    wg = jnp.pad(w_in[:, _A_WIDTH:_B_START], ((0, 0), (0, LANES - IN_SPLITS[4]))).astype(BF16)
    wb = w_in[:, _B_START:].astype(BF16)
    gb = jnp.zeros((16, 1), F32)
    gb = gb.at[0:4, 0].set(gate_b[:M_HEADS].astype(F32)).at[8:12, 0].set(gate_b[M_HEADS:].astype(F32))
    g1 = norm1_g.astype(F32).reshape(1, D_MODEL)
    gq_t = (jnp.tile(gq.astype(F32), (1, A_HEADS)) * (A_HEAD_DIM ** -0.5)).reshape(1, N_GROUPS * A_WIDTH)
    gk_t = jnp.tile(gk.astype(F32), (1, A_HEADS)).reshape(1, N_GROUPS * A_WIDTH)

    row = lambda w: pl.BlockSpec((tm, w), lambda i: (i, 0))
    rowT = lambda r: pl.BlockSpec((r, tm), lambda i: (0, i))
    full = lambda a: pl.BlockSpec(a.shape, lambda i: (0,) * a.ndim)
    dils = [d for _, d in DILATED_PATTERNS]
    res_shape = lambda d: jax.ShapeDtypeStruct((batch, d, seq // d, A_WIDTH), BF16)
    res_spec = lambda d: pl.BlockSpec((1, d, tm // d, A_WIDTH), lambda i: (i // steps, 0, i % steps, 0))
    out_shapes = (
        jax.ShapeDtypeStruct((n, M_QK_WIDTH), BF16),
        jax.ShapeDtypeStruct((M_QK_WIDTH, n), BF16),
        jax.ShapeDtypeStruct((n, M_WIDTH), BF16),
        jax.ShapeDtypeStruct((n, M_WIDTH), BF16),
        jax.ShapeDtypeStruct((8, n), F32),
        jax.ShapeDtypeStruct((8, n), F32),
        *[res_shape(d) for d in dils], *[res_shape(d) for d in dils], *[res_shape(d) for d in dils],
        jax.ShapeDtypeStruct((n, D_MODEL), BF16),
        jax.ShapeDtypeStruct((n, D_MODEL), BF16),
    )
    out_specs = (row(M_QK_WIDTH), rowT(M_QK_WIDTH), row(M_WIDTH), row(M_WIDTH), rowT(8), rowT(8),
                 *[res_spec(d) for d in dils], *[res_spec(d) for d in dils], *[res_spec(d) for d in dils],
                 row(D_MODEL), row(D_MODEL))
    outs = pl.pallas_call(
        _inproj_body,
        grid=(n // tm,),
        in_specs=[row(D_MODEL), full(g1), full(wa), full(wg), full(wb), full(gb), full(gq_t), full(gk_t)],
        out_specs=out_specs,
        out_shape=out_shapes,
        scratch_shapes=[pltpu.VMEM((2, 2, tm, LANES), F32)],
        compiler_params=_params("parallel"),
        name="inproj",
    )(x2d, g1, wa, wg, wb, gb, gq_t, gk_t)
    mq, kT, mv, so, gi, gf = outs[:6]
    aq, ak, av = outs[6:9], outs[9:12], outs[12:15]
    return mq, kT, mv, so, gi, gf, aq, ak, av, outs[15], outs[16]


M_CHUNK_LEN = 128


def _mlstm_body(q_ref, v_ref, so_ref, ng_ref, *rest, nchunk, nseq):
    kT_refs, gi_refs, gf_refs = rest[0:nseq], rest[nseq:2 * nseq], rest[2 * nseq:3 * nseq]
    o_ref, c_ref, m_ref = rest[3 * nseq:]
    L = M_CHUNK_LEN

    @pl.when(pl.program_id(0) == 0)
    def _():
        c_ref[...] = jnp.zeros_like(c_ref)
        m_ref[...] = jnp.zeros_like(m_ref)

    lane8 = lax.broadcasted_iota(I32, (8, L), 1)
    causal = lax.broadcasted_iota(I32, (L, L), 1) <= lax.broadcasted_iota(I32, (L, L), 0)
    lo_half = lax.broadcasted_iota(I32, (L, LANES), 1) < M_QK_DIM
    ones = jnp.ones((L, M_V_DIM), BF16)

    heads = range(M_HEADS)
    cstate = [[c_ref[s, h * M_QK_DIM:(h + 1) * M_QK_DIM, :] for h in heads] for s in range(nseq)]
    m_prev = [m_ref[s, :, 0:1] for s in range(nseq)]
    units = []
    for c in range(nchunk):
        rows = slice(c * L, (c + 1) * L)
        for s in range(nseq):
            gi = gi_refs[s][:, rows]
            b = gf_refs[s][:, rows]
            sh = 1
            while sh < L:
                b = b + jnp.where(lane8 >= sh, pltpu.roll(b, sh, 1), 0.0)
                sh *= 2
            u = gi - b
            g = b[:, L - 1:L]
            a = g + u
            amax = jnp.max(a, axis=1, keepdims=True)
            m_new = jnp.maximum(g + m_prev[s], amax)
            w = jnp.exp(a - m_new) * (M_QK_DIM ** -0.5)
            s_old = jnp.exp(g + m_prev[s] - m_new)
            vext = [jnp.concatenate([v_ref[s, rows, h * M_V_DIM:(h + 1) * M_V_DIM], ones], axis=1) for h in heads]
            cloc = []
            for h in heads:
                hr = slice(h * M_QK_DIM, (h + 1) * M_QK_DIM)
                kw = (kT_refs[s][hr, rows].astype(F32) * w[h:h + 1, :]).astype(BF16)
                cloc.append(jnp.dot(kw, vext[h], preferred_element_type=F32))
            units.append(dict(seq=s, rows=rows, b=b, u=u, m_prev=m_prev[s], state=cstate[s], vext=vext))
            cstate[s] = [s_old[h:h + 1, :] * cstate[s][h] + cloc[h] for h in heads]
            m_prev[s] = m_new
    for s in range(nseq):
        for h in heads:
            c_ref[s, h * M_QK_DIM:(h + 1) * M_QK_DIM, :] = cstate[s][h]
        m_ref[s] = jnp.broadcast_to(m_prev[s], (8, LANES))

    for un in units:
        s, rows = un["seq"], un["rows"]
        un["s"], un["qc"] = [], []
        for p in range(M_HEADS // 2):
            lanes_p = slice(p * LANES, (p + 1) * LANES)
            q_pair = q_ref[s, rows, lanes_p]
            kT_pair = kT_refs[s][lanes_p, rows]
            c_pair = jnp.concatenate([un["state"][2 * p], un["state"][2 * p + 1]], axis=0).astype(BF16)
            for hh in range(2):
                qm = jnp.where(lo_half if hh == 0 else jnp.logical_not(lo_half), q_pair, jnp.zeros_like(q_pair))
                un["s"].append(jnp.dot(qm, kT_pair, preferred_element_type=F32) * (M_QK_DIM ** -0.5))
                un["qc"].append(jnp.dot(qm, c_pair, preferred_element_type=F32))

    for un in units:
        s, rows, b, u, mp = un["seq"], un["rows"], un["b"], un["u"], un["m_prev"]
        for h in heads:
            hl = slice(h * M_V_DIM, (h + 1) * M_V_DIM)
            bcol = jnp.transpose(jnp.broadcast_to(b[h:h + 1, :], (L, L)))
            dm = jnp.where(causal, bcol + u[h:h + 1, :], -jnp.inf)
            inter = bcol + mp[h:h + 1, :]
            m_t = jnp.maximum(inter, jnp.max(dm, axis=1, keepdims=True))
            pmat = (un["s"][h] * jnp.exp(dm - m_t)).astype(BF16)
            sc = jnp.exp(inter - m_t)
            out = (jnp.dot(pmat, un["vext"][h], preferred_element_type=F32)
                   + jnp.concatenate([sc, sc], axis=1) * un["qc"][h])
            hv = out[:, :M_V_DIM] / jnp.maximum(jnp.abs(out[:, M_V_DIM:]), jnp.exp(-m_t))
            hn = hv * lax.rsqrt(jnp.mean(hv * hv, axis=1, keepdims=True) + EPS)
            hn = hn * ng_ref[:, hl] * so_ref[s, rows, hl].astype(F32)
            o_ref[s, rows, hl] = hn.astype(BF16)


def _stage_mlstm(mq, kT, mv, so, gi, gf, norm_g, batch, seq, rows_per_step):
    n = batch * seq
    R = rows_per_step
    steps = seq // R
    ng = norm_g.astype(F32).reshape(1, M_WIDTH)
    per_seq = lambda a: a.reshape(batch, seq, a.shape[1])
    row = lambda w: pl.BlockSpec((batch, R, w), lambda i: (0, i, 0))
    colT = lambda r, s: pl.BlockSpec((r, R), lambda i, s=s: (0, s * steps + i))
    seqs = range(batch)
    out = pl.pallas_call(
        functools.partial(_mlstm_body, nchunk=R // M_CHUNK_LEN, nseq=batch),
        grid=(steps,),
        in_specs=[row(M_QK_WIDTH), row(M_WIDTH), row(M_WIDTH), pl.BlockSpec((1, M_WIDTH), lambda i: (0, 0)),
                  *[colT(M_QK_WIDTH, s) for s in seqs], *[colT(8, s) for s in seqs], *[colT(8, s) for s in seqs]],
        out_specs=row(M_WIDTH),
        out_shape=jax.ShapeDtypeStruct((batch, seq, M_WIDTH), BF16),
        scratch_shapes=[pltpu.VMEM((batch, M_QK_WIDTH, 2 * M_V_DIM), F32), pltpu.VMEM((batch, 8, LANES), F32)],
        compiler_params=_params("arbitrary"),
        name="mlstm",
    )(per_seq(mq), per_seq(mv), per_seq(so), ng, *[kT] * batch, *[gi] * batch, *[gf] * batch)
    return out.reshape(n, M_WIDTH)


def _attn_body(q_ref, kp_ref, kc_ref, vp_ref, vc_ref, o_ref, lse_ref, *, dil, slopes, lq):
    QB = N_BACK
    first = pl.program_id(2) == 0
    qn = q_ref[0, 0]
    kcn = kc_ref[0, 0]
    kpn = kp_ref[0, 0]
    vc = vc_ref[0, 0]
    vp = vp_ref[0, 0]

    qi = lax.broadcasted_iota(I32, (QB, 2 * QB), 0)
    kj = lax.broadcasted_iota(I32, (QB, 2 * QB), 1)
    dist = qi + QB - kj
    band = jnp.logical_and(dist >= 0, dist <= N_BACK)
    distf = (dist * dil).astype(F32)
    bias = [jnp.where(band, -float(slopes[h]) * distf, -jnp.inf) for h in range(A_HEADS)]
    no_prev = jnp.logical_and(first, kj < QB)
    lo_half = lax.broadcasted_iota(I32, (QB, LANES), 1) < A_HEAD_DIM
    ones = jnp.ones((2 * QB, LANES), BF16)

    units = []
    for j in range(lq // QB):
        rows = slice(j * QB, (j + 1) * QB)
        prow = slice((j - 1) * QB, j * QB)
        keys = jnp.concatenate([kpn if j == 0 else kcn[prow], kcn[rows]], axis=0)
        vals = jnp.concatenate([vp if j == 0 else vc[prow], vc[rows]], axis=0)
        for p in range(A_HEADS // 2):
            lanes_p = slice(p * LANES, (p + 1) * LANES)
            q_pair = qn[rows, lanes_p]
            k_pair = keys[:, lanes_p]
            vext = jnp.concatenate([vals[:, lanes_p], ones], axis=1)
            scores = []
            for hh in range(2):
                sel = lo_half if hh == 0 else jnp.logical_not(lo_half)
                qm = jnp.where(sel, q_pair, jnp.zeros_like(q_pair))
                scores.append(lax.dot_general(qm, k_pair, _NT, preferred_element_type=F32))
            units.append((j, rows, lanes_p, p, vext, scores))

    for j, rows, lanes_p, p, vext, scores in units:
        o_pair = None
        l_pair = None
        for hh in range(2):
            s = scores[hh] + bias[2 * p + hh]
            if j == 0:
                s = jnp.where(no_prev, -jnp.inf, s)
            m = jnp.max(s, axis=1, keepdims=True)
            pv = jnp.dot(jnp.exp(s - m).astype(BF16), vext, preferred_element_type=F32)
            den = pv[:, LANES:]
            o_h = pv[:, :LANES] / den
            l_h = m + jnp.log(den)
            o_pair = o_h if hh == 0 else jnp.where(lo_half, o_pair, o_h)
            l_pair = l_h if hh == 0 else jnp.where(lo_half, l_pair, l_h)
        o_ref[0, 0, rows, lanes_p] = o_pair.astype(BF16)
        lse_ref[0, 0, rows, lanes_p] = l_pair


def _stage_attn(aq, ak, av, batch, seq, group):
    _, dil = DILATED_PATTERNS[group]
    L = seq // dil
    assert L % N_BACK == 0
    lq = min(2048, L)
    nq = L // lq
    sub = lq // N_BACK
    cur = pl.BlockSpec((1, 1, lq, A_WIDTH), lambda b, r, i: (b, r, i, 0))
    prev = pl.BlockSpec((1, 1, N_BACK, A_WIDTH), lambda b, r, i: (b, r, jnp.maximum(i * sub - 1, 0), 0))
    return pl.pallas_call(
        functools.partial(_attn_body, dil=dil, slopes=tuple(_alibi_slopes()[group]), lq=lq),
        grid=(batch, dil, nq),
        in_specs=[cur, prev, cur, prev, cur],
        out_specs=(cur, cur),
        out_shape=(jax.ShapeDtypeStruct((batch, dil, L, A_WIDTH), BF16),
                   jax.ShapeDtypeStruct((batch, dil, L, A_WIDTH), F32)),
        compiler_params=_params("parallel", "parallel", "parallel"),
        name=f"dilated_attn_d{dil}",
    )(aq, ak, ak, av, av)


PACK_ROWS = D_MODEL // (2 * LANES)
U32 = jnp.uint32
_HIGH_HALF = 0xFFFF0000


def _pack_rows(val, exact=False):
    half = D_MODEL // 2
    lo, hi = val[:, :half], val[:, half:]
    if exact:
        return (lax.bitcast_convert_type(lo, U32) >> 16) | lax.bitcast_convert_type(hi, U32)

    def bits(v):
        return lax.bitcast_convert_type(v.astype(BF16).astype(F32), U32)

    return (bits(lo) >> 16) | (bits(hi) & U32(_HIGH_HALF))


def _unpack_rows(words):
    lo = lax.bitcast_convert_type(words << 16, F32).astype(BF16)
    hi = lax.bitcast_convert_type(words & U32(_HIGH_HALF), F32).astype(BF16)
    return jnp.concatenate([lo, hi], axis=1)


def _flat(ref):
    rows = 1
    for d in ref.shape[:-2]:
        rows *= d
    return ref.reshape(rows * PACK_ROWS, LANES)


def _store_packed(flat_ref, row0, words):
    t = words.shape[0]
    for s in range(PACK_ROWS):
        flat_ref[pl.ds(row0 * PACK_ROWS + s, t, stride=PACK_ROWS), :] = words[:, s * LANES:(s + 1) * LANES]


def _load_packed(flat_ref, row0, t):
    return jnp.concatenate([flat_ref[pl.ds(row0 * PACK_ROWS + s, t, stride=PACK_ROWS), :] for s in range(PACK_ROWS)],
                           axis=1)


def _rows8(vals):
    t = vals[0].shape[1]
    rid = lax.broadcasted_iota(I32, (8, t), 0)
    out = jnp.zeros((8, t), vals[0].dtype)
    for k, v in enumerate(vals):
        out = jnp.where(rid == k, jnp.broadcast_to(v, (8, t)), out)
    return out


def _merge_body(hm_ref, o1_ref, o2_ref, o3_ref, l1_ref, l2_ref, l3_ref, sgm_ref, sga_ref, x_ref,
                wm_ref, wa_ref, wo_ref, g2_ref, wrh_ref, br_ref,
                x2_ref, xn_ref, loc_ref, gate_ref, tcnt_ref, tcar_ref, cnt_ref, carry_ref, st_ref):
    @pl.when(pl.program_id(0) == 0)
    def _():
        carry_ref[...] = jnp.zeros_like(carry_ref)

    m_branch = jnp.dot(hm_ref[...], wm_ref[...], preferred_element_type=F32)
    dils = [d for _, d in DILATED_PATTERNS]
    l1, l2, l3 = [_merge_residues(r, d, st_ref) for r, d in zip((l1_ref, l2_ref, l3_ref), dils)]
    lmax = jnp.maximum(jnp.maximum(l1, l2), l3)
    e1, e2, e3 = jnp.exp(l1 - lmax), jnp.exp(l2 - lmax), jnp.exp(l3 - lmax)
    num = e1 * _merge_residues(o1_ref, dils[0], st_ref)
    num = num + e2 * _merge_residues(o2_ref, dils[1], st_ref)
    num = num + e3 * _merge_residues(o3_ref, dils[2], st_ref)
    h_a = num / (e1 + e2 + e3)
    y = (sgm_ref[...].astype(F32) * m_branch
         + sga_ref[...].astype(F32) * jnp.dot(h_a.astype(BF16), wa_ref[...], preferred_element_type=F32))
    x2 = x_ref[...] + jnp.dot(y.astype(BF16), wo_ref[...], preferred_element_type=F32)
    x2_ref[...] = x2
    xn = x2 * lax.rsqrt(jnp.mean(x2 * x2, axis=-1, keepdims=True) + EPS) * g2_ref[...]
    xh = xn.astype(BF16)
    xn_ref[...] = xh

    logits = lax.dot_general(wrh_ref[...], xh, _NT, preferred_element_type=F32) + br_ref[...]
    t = logits.shape[1]
    eid = lax.broadcasted_iota(I32, (N_EXPERTS, t), 0).astype(F32)
    vals = logits
    top_v, top_i = [], []
    for _ in range(TOP_K):
        mx = jnp.max(vals, axis=0, keepdims=True)
        ik = jnp.min(jnp.where(vals == mx, eid, float(N_EXPERTS)), axis=0, keepdims=True)
        top_v.append(mx)
        top_i.append(ik)
        vals = jnp.where(eid == ik, -jnp.inf, vals)
    ex = [jnp.exp(v - top_v[0]) for v in top_v]
    den = ex[0] + ex[1] + ex[2] + ex[3]
    gate_ref[...] = _rows8([e / den for e in ex])

    chosen = jnp.zeros((N_EXPERTS, t), F32)
    for ik in top_i:
        chosen = chosen + (eid == ik).astype(F32)
    before = jnp.where(lax.broadcasted_iota(jnp.int16, (t, t), 0) < lax.broadcasted_iota(jnp.int16, (t, t), 1),
                       jnp.ones((), BF16), jnp.zeros((), BF16))
    prefix = jnp.dot(chosen.astype(BF16), before, preferred_element_type=F32)
    tcount = jnp.broadcast_to(jnp.sum(chosen, axis=1, keepdims=True), (N_EXPERTS, LANES))
    below = (lax.broadcasted_iota(I32, (N_EXPERTS, N_EXPERTS), 1)
             < lax.broadcasted_iota(I32, (N_EXPERTS, N_EXPERTS), 0)).astype(BF16)
    t_hi = jnp.floor(tcount * (1.0 / BF16_EXACT_INT)) * BF16_EXACT_INT
    tile_off = (jnp.dot(below, t_hi.astype(BF16), preferred_element_type=F32)
                + jnp.dot(below, (tcount - t_hi).astype(BF16), preferred_element_type=F32))
    pos = prefix + tile_off[:, 0:1]
    loc_ref[...] = _rows8([jnp.sum(jnp.where(eid == ik, pos, 0.0), axis=0, keepdims=True).astype(I32)
                           for ik in top_i])
    carry = carry_ref[...]
    tcnt_ref[...] = tcount.astype(I32)
    tcar_ref[...] = carry.astype(I32)
    total = carry + tcount
    carry_ref[...] = total
    cnt_ref[...] = total


def _stage_merge(h_m, attn, sgm, sga, x2d, w_mb, w_ab, w_out, norm2_g, w_router, b_router, batch, seq, tm):
    n = x2d.shape[0]
    steps = seq // tm
    (o1, l1), (o2, l2), (o3, l3) = attn
    wm = w_mb.astype(BF16)
    wa = w_ab.astype(BF16)
    wo = w_out.astype(BF16)
    g2 = norm2_g.astype(F32).reshape(1, D_MODEL)
    wrh = w_router.astype(BF16).T
    br = b_router.astype(F32).reshape(N_EXPERTS, 1)
    row = lambda w: pl.BlockSpec((tm, w), lambda i: (i, 0))
    rowT = lambda r: pl.BlockSpec((r, tm), lambda i: (0, i))
    full = lambda a: pl.BlockSpec(a.shape, lambda i: (0,) * a.ndim)
    res = lambda d: pl.BlockSpec((1, d, tm // d, A_WIDTH), lambda i: (i // steps, 0, i % steps, 0))
    dils = [d for _, d in DILATED_PATTERNS]
    per_tile = pl.BlockSpec((N_EXPERTS, LANES), lambda i: (0, i))
    return pl.pallas_call(
        _merge_body,
        grid=(n // tm,),
        in_specs=[row(M_WIDTH), *[res(d) for d in dils], *[res(d) for d in dils],
                  row(D_MODEL), row(D_MODEL), row(D_MODEL),
                  full(wm), full(wa), full(wo), full(g2), full(wrh), full(br)],
        out_specs=(row(D_MODEL), row(D_MODEL), rowT(8), rowT(8), per_tile, per_tile,
                   pl.BlockSpec((N_EXPERTS, LANES), lambda i: (0, 0))),
        out_shape=(jax.ShapeDtypeStruct((n, D_MODEL), F32),
                   jax.ShapeDtypeStruct((n, D_MODEL), BF16),
                   jax.ShapeDtypeStruct((8, n), I32),
                   jax.ShapeDtypeStruct((8, n), F32),
                   jax.ShapeDtypeStruct((N_EXPERTS, (n // tm) * LANES), I32),
                   jax.ShapeDtypeStruct((N_EXPERTS, (n // tm) * LANES), I32),
                   jax.ShapeDtypeStruct((N_EXPERTS, LANES), F32)),
        scratch_shapes=[pltpu.VMEM((N_EXPERTS, LANES), F32), pltpu.VMEM((2, 2, tm, LANES), F32)],
        compiler_params=_params("arbitrary"),
        name="merge_route",
    )(h_m, o1, o2, o3, l1, l2, l3, sgm, sga, x2d, wm, wa, wo, g2, wrh, br)


def _offsets_body(cnt_ref, blk_ref, pstart_ref, zlo_ref, zhi_ref, *, nblk_pad):
    cnt = cnt_ref[...]
    padded = jnp.floor((cnt + (MOE_BLOCK - 1)) * (1.0 / MOE_BLOCK)) * MOE_BLOCK
    lower = (lax.broadcasted_iota(I32, (N_EXPERTS, N_EXPERTS), 1)
             <= lax.broadcasted_iota(I32, (N_EXPERTS, N_EXPERTS), 0)).astype(BF16)
    nb = padded * (1.0 / MOE_BLOCK)
    nb_hi = jnp.floor(nb * (1.0 / BF16_EXACT_INT)) * BF16_EXACT_INT
    pends = (jnp.dot(lower, nb_hi.astype(BF16), preferred_element_type=F32)
             + jnp.dot(lower, (nb - nb_hi).astype(BF16), preferred_element_type=F32)) * MOE_BLOCK
    pstart = pends - padded
    pstart_ref[...] = pstart.astype(I32)
    zlo_ref[...] = (pstart + cnt).astype(I32)
    zhi_ref[...] = pends.astype(I32)

    first_row = (lax.broadcasted_iota(I32, (N_EXPERTS, nblk_pad), 1) * MOE_BLOCK).astype(F32)
    pe = jnp.broadcast_to(pends[:, 0:1], (N_EXPERTS, nblk_pad))
    be = jnp.sum((pe <= first_row).astype(F32), axis=0, keepdims=True)
    be = jnp.minimum(be, float(N_EXPERTS - 1))
    nused = pends[N_EXPERTS - 1:N_EXPERTS, 0:1] * (1.0 / MOE_BLOCK)
    nonempty = jnp.broadcast_to(padded[:, 0:1], (N_EXPERTS, nblk_pad)) > 0.0
    runidx = jnp.sum(jnp.logical_and(pe <= first_row, nonempty).astype(F32), axis=0, keepdims=True)
    parity = runidx - 2.0 * jnp.floor(runidx * 0.5)
    eid = lax.broadcasted_iota(I32, (N_EXPERTS, nblk_pad), 0).astype(F32)
    later = jnp.logical_and(eid > be, nonempty)
    nxt = jnp.min(jnp.where(later, eid, float(N_EXPERTS)), axis=0, keepdims=True)
    blk_ref[...] = _rows8([be.astype(I32), jnp.broadcast_to(nused, (1, nblk_pad)).astype(I32),
                           parity.astype(I32), nxt.astype(I32)])


def _stage_offsets(cnt, nblk):
    nblk_pad = -(-nblk // LANES) * LANES
    const = lambda r, c: pl.BlockSpec((r, c), lambda i: (0, 0))
    per_expert = jax.ShapeDtypeStruct((N_EXPERTS, LANES), I32)
    return pl.pallas_call(
        functools.partial(_offsets_body, nblk_pad=nblk_pad),
        grid=(1,),
        in_specs=[const(N_EXPERTS, LANES)],
        out_specs=(const(8, nblk_pad), const(N_EXPERTS, LANES), const(N_EXPERTS, LANES), const(N_EXPERTS, LANES)),
        out_shape=(jax.ShapeDtypeStruct((8, nblk_pad), I32), per_expert, per_expert, per_expert),
        compiler_params=_params("arbitrary"),
        name="route_offsets",
    )(cnt)


RUN_BITS = 10


def _tile_rows(ref, first_row, nrows):
    return ref.at[pl.ds(first_row, nrows)]


def _for_each_piece(length, fn):
    for b in reversed(range(RUN_BITS)):
        @pl.when(((length >> b) & 1) == 1)
        def _(b=b):
            fn((length >> (b + 1)) << (b + 1), 1 << b)


def _for_each_run(tile, tcnt_ref, tcar_ref, pstart_ref, fn):
    def per_expert(e, local):
        count = tcnt_ref[tile, e]
        first = pstart_ref[e] + tcar_ref[tile, e]
        _for_each_piece(count, lambda off, size: fn(local + off, first + off, size))
        return local + count

    lax.fori_loop(0, N_EXPERTS, per_expert, 0)


PERM_CHUNK = 256


def _dispatch_body(tcnt_ref, tcar_ref, pstart_ref, zlo_ref, zhi_ref, loc_ref, xn_ref, xs_hbm,
                   buf_ref, sems, *, tm):
    step = pl.program_id(0)
    nloc = TOP_K * tm

    def wait_buffer(slot):
        pltpu.make_async_copy(buf_ref.at[slot], xs_hbm.at[pl.ds(0, nloc)], sems.at[slot]).wait()

    for slot in range(2):
        tile = 2 * step + slot

        @pl.when(step > 0)
        def _(slot=slot):
            wait_buffer(slot)

        loc = loc_ref[:, slot * tm:(slot + 1) * tm].astype(jnp.int16)
        xn = xn_ref[slot * tm:(slot + 1) * tm, :]
        for c in range(TOP_K * tm // PERM_CHUNK):
            lid = lax.broadcasted_iota(jnp.int16, (PERM_CHUNK, tm), 0) + jnp.int16(c * PERM_CHUNK)
            hit = lid == loc[0:1, :]
            for k in range(1, TOP_K):
                hit = jnp.logical_or(hit, lid == loc[k:k + 1, :])
            perm = jnp.where(hit, jnp.ones((), BF16), jnp.zeros((), BF16))
            rows = jnp.dot(perm, xn, preferred_element_type=F32)
            _store_packed(_flat(buf_ref), slot * nloc + c * PERM_CHUNK, _pack_rows(rows, exact=True))

        def run_copy(local, first, size, slot=slot):
            return pltpu.make_async_copy(_tile_rows(buf_ref.at[slot], local, size),
                                         _tile_rows(xs_hbm, first, size), sems.at[slot])

        _for_each_run(tile, tcnt_ref, tcar_ref, pstart_ref, lambda l, f, s: run_copy(l, f, s).start())

    @pl.when(step == pl.num_programs(0) - 1)
    def _():
        wait_buffer(0)
        wait_buffer(1)
        zsrc = buf_ref.at[0]
        zsrc[pl.ds(0, MOE_BLOCK)] = jnp.zeros((MOE_BLOCK, PACK_ROWS, LANES), U32)

        def zero_copy(first, size):
            return pltpu.make_async_copy(_tile_rows(zsrc, 0, size), _tile_rows(xs_hbm, first, size), sems.at[0])

        def per_expert(e, carry):
            lo = zlo_ref[e]
            npad = zhi_ref[e] - lo
            _for_each_piece(npad, lambda off, size: zero_copy(lo + off, size).start())
            _for_each_piece(npad, lambda off, size: zero_copy(lo + off, size).wait())
            return carry

        lax.fori_loop(0, N_EXPERTS, per_expert, 0)

        first_unused = zhi_ref[N_EXPERTS - 1] // MOE_BLOCK
        nblk = xs_hbm.shape[0] // MOE_BLOCK

        def tail(blk, carry):
            zero_copy(blk * MOE_BLOCK, MOE_BLOCK).start()
            zero_copy(blk * MOE_BLOCK, MOE_BLOCK).wait()
            return carry

        lax.fori_loop(first_unused, nblk, tail, 0)


def _stage_dispatch(tables, loc8, xn, nrows, tm):
    n = xn.shape[0]
    assert TOP_K * tm >= MOE_BLOCK and (n // tm) % 2 == 0
    grid_spec = pltpu.PrefetchScalarGridSpec(
        num_scalar_prefetch=5,
        grid=(n // (2 * tm),),
        in_specs=[pl.BlockSpec((8, 2 * tm), lambda i, *_: (0, i)),
                  pl.BlockSpec((2 * tm, D_MODEL), lambda i, *_: (i, 0))],
        out_specs=pl.BlockSpec(memory_space=pl.ANY),
        scratch_shapes=[pltpu.VMEM((2, TOP_K * tm, PACK_ROWS, LANES), U32), pltpu.SemaphoreType.DMA((2,))],
    )
    return pl.pallas_call(
        functools.partial(_dispatch_body, tm=tm),
        grid_spec=grid_spec,
        out_shape=jax.ShapeDtypeStruct((nrows, PACK_ROWS, LANES), U32),
        compiler_params=_params("arbitrary"),
        name="dispatch",
    )(*tables, loc8, xn)


EXPERT_BLOCKS_PER_STEP = 2


def _expert_body(be_ref, nu_ref, par_ref, nxt_ref, xs_ref, w1_hbm, b1_ref, w2_hbm, b2_ref, ys_ref,
                 w1f_ref, w2f_ref, w1b_ref, w2b_ref, sems):
    def fetch(expert, slot):
        return (pltpu.make_async_copy(w1_hbm.at[expert], w1f_ref.at[slot], sems.at[slot, 0]),
                pltpu.make_async_copy(w2_hbm.at[expert], w2f_ref.at[slot], sems.at[slot, 1]))

    def block_info(sub):
        j = pl.program_id(0) * EXPERT_BLOCKS_PER_STEP + sub
        used = j < nu_ref[0]
        jj = jnp.maximum(jnp.minimum(j, nu_ref[0] - 1), 0)
        e = be_ref[jj]
        fresh = jnp.logical_or(j == 0, e != be_ref[jnp.maximum(jj - 1, 0)])
        return j, jj, e, used, fresh

    def refresh(j, jj, e):
        slot = par_ref[jj]

        @pl.when(j == 0)
        def _():
            for c in fetch(e, slot):
                c.start()

        for c in fetch(e, slot):
            c.wait()
        nxt = nxt_ref[jj]

        @pl.when(nxt < N_EXPERTS)
        def _():
            for c in fetch(nxt, 1 - slot):
                c.start()

        w1b_ref[...] = w1f_ref[slot].astype(BF16)
        w2b_ref[...] = w2f_ref[slot].astype(BF16)

    def compute(e, row0, nrows):
        xb = _unpack_rows(_load_packed(_flat(xs_ref), row0, nrows))
        gu = jnp.dot(xb, w1b_ref[...], preferred_element_type=F32) + b1_ref[pl.ds(e, 1), :]
        gate = jnp.minimum(gu[:, :D_FF], SWIGLU_LIMIT)
        lin = jnp.clip(gu[:, D_FF:], -SWIGLU_LIMIT, SWIGLU_LIMIT)
        act = (lin + 1.0) * (gate * jax.nn.sigmoid(SWIGLU_ALPHA * gate))
        ys = jnp.dot(act.astype(BF16), w2b_ref[...], preferred_element_type=F32) + b2_ref[pl.ds(e, 1), :]
        _store_packed(_flat(ys_ref), row0, _pack_rows(ys))

    j0, jj0, e0, used0, fresh0 = block_info(0)
    j1, jj1, e1, used1, fresh1 = block_info(1)
    same = jnp.logical_and(used1, e1 == e0)

    @pl.when(jnp.logical_and(used0, fresh0))
    def _():
        refresh(j0, jj0, e0)

    @pl.when(jnp.logical_and(used0, same))
    def _():
        compute(e0, 0, 2 * MOE_BLOCK)

    @pl.when(jnp.logical_and(used0, jnp.logical_not(same)))
    def _():
        compute(e0, 0, MOE_BLOCK)

    @pl.when(jnp.logical_and(used1, fresh1))
    def _():
        refresh(j1, jj1, e1)

    @pl.when(jnp.logical_and(used1, jnp.logical_not(same)))
    def _():
        compute(e1, MOE_BLOCK, MOE_BLOCK)

    for sub, used in ((0, used0), (1, used1)):
        @pl.when(jnp.logical_not(used))
        def _(sub=sub):
            ys_ref[pl.ds(sub * MOE_BLOCK, MOE_BLOCK)] = jnp.zeros((MOE_BLOCK, PACK_ROWS, LANES), U32)


def _stage_experts(blk8, xs, w1, b1, w2, b2):
    nrows = xs.shape[0]
    nblk = nrows // MOE_BLOCK
    assert EXPERT_BLOCKS_PER_STEP == 2 and nblk % EXPERT_BLOCKS_PER_STEP == 0
    block_e, nused, parity, nxt = blk8[0, :nblk], blk8[1, :1], blk8[2, :nblk], blk8[3, :nblk]
    tiles = (EXPERT_BLOCKS_PER_STEP * MOE_BLOCK, PACK_ROWS, LANES)
    full = lambda a: pl.BlockSpec(a.shape, lambda j, *_: (0,) * a.ndim)
    grid_spec = pltpu.PrefetchScalarGridSpec(
        num_scalar_prefetch=4,
        grid=(nblk // EXPERT_BLOCKS_PER_STEP,),
        in_specs=[pl.BlockSpec(tiles, lambda j, *_: (j, 0, 0)),
                  pl.BlockSpec(memory_space=pl.ANY), full(b1),
                  pl.BlockSpec(memory_space=pl.ANY), full(b2)],
        out_specs=pl.BlockSpec(tiles, lambda j, *_: (j, 0, 0)),
        scratch_shapes=[pltpu.VMEM((2, D_MODEL, 2 * D_FF), F32), pltpu.VMEM((2, D_FF, D_MODEL), F32),
                        pltpu.VMEM((D_MODEL, 2 * D_FF), BF16), pltpu.VMEM((D_FF, D_MODEL), BF16),
                        pltpu.SemaphoreType.DMA((2, 2))],
    )
    return pl.pallas_call(
        _expert_body,
        grid_spec=grid_spec,
        out_shape=jax.ShapeDtypeStruct((nrows, PACK_ROWS, LANES), U32),
        compiler_params=_params("arbitrary"),
        name="experts",
    )(block_e, nused, parity, nxt, xs, w1, b1, w2, b2)


COMBINE_CHUNK = 512


def _combine_body(tcnt_ref, tcar_ref, pstart_ref, loc_ref, gate_ref, x2_ref, ys_hbm, out_ref,
                  buf_ref, g_ref, sems, *, tm):
    step = pl.program_id(0)
    nloc = TOP_K * tm

    def start_runs(tile, slot):
        def run_copy(local, first, size):
            return pltpu.make_async_copy(_tile_rows(ys_hbm, first, size),
                                         _tile_rows(buf_ref.at[slot], local, size), sems.at[slot])
        _for_each_run(tile, tcnt_ref, tcar_ref, pstart_ref, lambda l, f, s: run_copy(l, f, s).start())

    def wait_buffer(slot):
        pltpu.make_async_copy(ys_hbm.at[pl.ds(0, nloc)], buf_ref.at[slot], sems.at[slot]).wait()

    def combine(slot):
        zpad = jnp.zeros((LANES - 16, LANES), F32)
        cols = []
        for c in range(tm // LANES):
            cols_in = slice(slot * tm + c * LANES, slot * tm + (c + 1) * LANES)
            cols.append(jnp.transpose(jnp.concatenate([loc_ref[:, cols_in].astype(F32), gate_ref[:, cols_in], zpad],
                                                      axis=0)))
        rows = slice(slot * tm, (slot + 1) * tm)
        acc = x2_ref[rows, :]
        for lc in range(nloc // COMBINE_CHUNK):
            lane = lax.broadcasted_iota(jnp.int16, (LANES, COMBINE_CHUNK), 1) + jnp.int16(lc * COMBINE_CHUNK)
            for c in range(tm // LANES):
                g = jnp.zeros((LANES, COMBINE_CHUNK), BF16)
                for k in range(TOP_K):
                    g = jnp.where(lane == cols[c][:, k:k + 1].astype(jnp.int16), cols[c][:, 8 + k:9 + k].astype(BF16), g)
                g_ref[c * LANES:(c + 1) * LANES, lc * COMBINE_CHUNK:(lc + 1) * COMBINE_CHUNK] = g
            if lc == 0:
                wait_buffer(slot)
            ys = _unpack_rows(_load_packed(_flat(buf_ref), slot * nloc + lc * COMBINE_CHUNK, COMBINE_CHUNK))
            acc = acc + jnp.dot(g_ref[:, lc * COMBINE_CHUNK:(lc + 1) * COMBINE_CHUNK], ys, preferred_element_type=F32)
        out_ref[rows, :] = acc

    @pl.when(step == 0)
    def _():
        start_runs(0, 0)

    start_runs(2 * step + 1, 1)
    combine(0)

    @pl.when(step + 1 < pl.num_programs(0))
    def _():
        start_runs(2 * step + 2, 0)

    combine(1)


def _stage_combine(tables, loc8, gate8, x2, ys, tm):
    n = x2.shape[0]
    assert (n // tm) % 2 == 0
    grid_spec = pltpu.PrefetchScalarGridSpec(
        num_scalar_prefetch=3,
        grid=(n // (2 * tm),),
        in_specs=[pl.BlockSpec((8, 2 * tm), lambda i, *_: (0, i)),
                  pl.BlockSpec((8, 2 * tm), lambda i, *_: (0, i)),
                  pl.BlockSpec((2 * tm, D_MODEL), lambda i, *_: (i, 0)),
                  pl.BlockSpec(memory_space=pl.ANY)],
        out_specs=pl.BlockSpec((2 * tm, D_MODEL), lambda i, *_: (i, 0)),
        scratch_shapes=[pltpu.VMEM((2, TOP_K * tm, PACK_ROWS, LANES), U32),
                        pltpu.VMEM((tm, TOP_K * tm), BF16),
                        pltpu.SemaphoreType.DMA((2,))],
    )
    return pl.pallas_call(
        functools.partial(_combine_body, tm=tm),
        grid_spec=grid_spec,
        out_shape=jax.ShapeDtypeStruct((n, D_MODEL), F32),
        compiler_params=_params("arbitrary"),
        name="combine",
    )(*tables, loc8, gate8, x2, ys)


def _moe(x2, xn, loc8, gate8, tcnt, tcar, cnt, w1, b1, w2, b2, tm):
    n = x2.shape[0]
    ntile = n // tm
    nblk = -(-(n * TOP_K) // MOE_BLOCK) + N_EXPERTS
    blk8, pstart, zlo, zhi = _stage_offsets(cnt, nblk)
    per_tile = lambda a: a.reshape(N_EXPERTS, ntile, LANES)[:, :, 0].T
    tables = (per_tile(tcnt), per_tile(tcar), pstart[:, 0])
    xs = _stage_dispatch(tables + (zlo[:, 0], zhi[:, 0]), loc8, xn, nblk * MOE_BLOCK, tm)
    ys = _stage_experts(blk8, xs, w1, b1, w2, b2)
    return _stage_combine(tables, loc8, gate8, x2, ys, tm)


def kernel(x, norm1_g, w_in, mlstm_gate_b, mlstm_norm_g, attn_q_norm_g, attn_k_norm_g, w_mlstm_branch,
           w_attn_branch, w_out, norm2_g, w_router, b_router, w1, b1, w2, b2):
    batch, seq, _ = x.shape
    n = batch * seq
    for l in range(norm1_g.shape[0]):
        x2d = x.reshape(n, D_MODEL)
        tm = min(512, seq)
        mq, kT, mv, so, gi, gf, aq, ak, av, sgm, sga = _stage_inproj(
            x2d, norm1_g[l], w_in[l], mlstm_gate_b[l], attn_q_norm_g[l], attn_k_norm_g[l], batch, seq, tm)
        h_m = _stage_mlstm(mq, kT, mv, so, gi, gf, mlstm_norm_g[l], batch, seq, tm)
        attn = [_stage_attn(aq[g], ak[g], av[g], batch, seq, g)
                for g in range(N_GROUPS)]
        x2, xn, loc8, gate8, tcnt, tcar, cnt = _stage_merge(
            h_m, attn, sgm, sga, x2d, w_mlstm_branch[l], w_attn_branch[l], w_out[l], norm2_g[l],
            w_router[l], b_router[l], batch, seq, tm)
        out = _moe(x2, xn, loc8, gate8, tcnt, tcar, cnt, w1[l], b1[l], w2[l], b2[l], tm)
        x = out.reshape(batch, seq, D_MODEL)
    return x
```

```python
import functools

import numpy as np
import jax
import jax.numpy as jnp
from jax import lax
from jax.experimental import pallas as pl
from jax.experimental.pallas import tpu as pltpu

F32 = jnp.float32
BF16 = jnp.bfloat16
I32 = jnp.int32

D_MODEL = 1024
M_HEADS = 4
M_QK_DIM = 64
M_V_DIM = 128
GATE_SOFTCAP = 15.0
A_HEADS = 4
A_HEAD_DIM = 64
DILATED_PATTERNS = ((128, 1), (512, 4), (2048, 16))
N_GROUPS = len(DILATED_PATTERNS)
N_BACK = 128
N_EXPERTS = 32
TOP_K = 4
D_FF = 1024
SWIGLU_LIMIT = 7.0
SWIGLU_ALPHA = 1.702
MOE_BLOCK = 512
EPS = 1e-6

M_WIDTH = M_HEADS * M_V_DIM
M_QK_WIDTH = M_HEADS * M_QK_DIM
A_WIDTH = A_HEADS * A_HEAD_DIM
IN_SPLITS = (M_QK_WIDTH, M_QK_WIDTH, M_WIDTH, M_WIDTH, 2 * M_HEADS,
             N_GROUPS * A_WIDTH, N_GROUPS * A_WIDTH, N_GROUPS * A_WIDTH, D_MODEL, D_MODEL)

LANES = 128
VMEM_LIMIT = 56 * 1024 * 1024

BF16_EXACT_INT = 256.0

_NT = (((1,), (1,)), ((), ()))
_TN = (((0,), (1,)), ((), ()))


def _alibi_slopes():
    n = N_GROUPS * A_HEADS
    s = np.exp2(-8.0 * np.arange(1, n + 1) / n).astype(np.float32)
    return s.reshape(N_GROUPS, A_HEADS)


def _params(*sem):
    return pltpu.CompilerParams(dimension_semantics=sem, vmem_limit_bytes=VMEM_LIMIT)


def _log_sigmoid(x):
    return jnp.minimum(x, 0.0) - jnp.log1p(jnp.exp(-jnp.abs(x)))


_A_WIDTH = sum(IN_SPLITS[:4])
_B_START = _A_WIDTH + IN_SPLITS[4]


def _piece_segments(widths):
    bounds, start = [], 0
    for width in widths:
        bounds.append((start, start + width))
        start += width
    return bounds


_C_MQ, _C_MK, _C_MV, _C_MO = _piece_segments(IN_SPLITS[:4])
_C_AQ, _C_AK, _C_AV, _C_GM, _C_GA = _piece_segments(IN_SPLITS[5:])


def _split_residues(val, d, out_ref, st_ref):
    t = val.shape[0]
    if d == 1:
        out_ref[0, 0] = val.astype(out_ref.dtype)
        return
    a_ref = st_ref.at[0]
    a_ref[0] = val[:, :LANES]
    a_ref[1] = val[:, LANES:]
    m = t // d
    if d == 16:
        b_ref = st_ref.at[1]
        for r0 in range(4):
            b_ref[0, r0 * 4 * m:(r0 + 1) * 4 * m, :] = a_ref[0, pl.ds(r0, 4 * m, stride=4), :]
            b_ref[1, r0 * 4 * m:(r0 + 1) * 4 * m, :] = a_ref[1, pl.ds(r0, 4 * m, stride=4), :]
        for r0 in range(4):
            for r1 in range(4):
                piece = jnp.concatenate([b_ref[0, pl.ds(r0 * 4 * m + r1, m, stride=4), :],
                                         b_ref[1, pl.ds(r0 * 4 * m + r1, m, stride=4), :]], axis=1)
                out_ref[0, 4 * r1 + r0] = piece.astype(out_ref.dtype)
        return
    for r in range(d):
        piece = jnp.concatenate([a_ref[0, pl.ds(r, m, stride=d), :], a_ref[1, pl.ds(r, m, stride=d), :]], axis=1)
        out_ref[0, r] = piece.astype(out_ref.dtype)


def _merge_residues(ref, d, st_ref):
    if d == 1:
        return ref[0, 0].astype(F32)
    m = ref.shape[2]
    if d == 16:
        a_ref, b_ref = st_ref.at[0], st_ref.at[1]
        for r0 in range(4):
            for r1 in range(4):
                blk = ref[0, 4 * r1 + r0].astype(F32)
                a_ref[0, pl.ds(r0 * 4 * m + r1, m, stride=4), :] = blk[:, :LANES]
                a_ref[1, pl.ds(r0 * 4 * m + r1, m, stride=4), :] = blk[:, LANES:]
        for r0 in range(4):
            b_ref[0, pl.ds(r0, 4 * m, stride=4), :] = a_ref[0, r0 * 4 * m:(r0 + 1) * 4 * m, :]
            b_ref[1, pl.ds(r0, 4 * m, stride=4), :] = a_ref[1, r0 * 4 * m:(r0 + 1) * 4 * m, :]
        return jnp.concatenate([b_ref[0], b_ref[1]], axis=1)
    a_ref = st_ref.at[0]
    for r in range(d):
        blk = ref[0, r].astype(F32)
        a_ref[0, pl.ds(r, m, stride=d), :] = blk[:, :LANES]
        a_ref[1, pl.ds(r, m, stride=d), :] = blk[:, LANES:]
    return jnp.concatenate([a_ref[0], a_ref[1]], axis=1)


def _inproj_body(x_ref, g1_ref, wa_ref, wg_ref, wb_ref, gb_ref, gq_ref, gk_ref,
                 mq_ref, kT_ref, mv_ref, so_ref, gi_ref, gf_ref,
                 q0_ref, q1_ref, q2_ref, k0_ref, k1_ref, k2_ref, v0_ref, v1_ref, v2_ref,
                 sgm_ref, sga_ref, st_ref):
    x = x_ref[...]
    h = x * lax.rsqrt(jnp.mean(x * x, axis=-1, keepdims=True) + EPS) * g1_ref[...]
    hb = h.astype(BF16)

    def seg(w_ref, c):
        return jnp.dot(hb, w_ref[:, c[0]:c[1]], preferred_element_type=F32)

    mq_ref[...] = seg(wa_ref, _C_MQ).astype(BF16)
    mv_ref[...] = seg(wa_ref, _C_MV).astype(BF16)
    so_ref[...] = jax.nn.sigmoid(seg(wa_ref, _C_MO)).astype(BF16)
    hid_r = lax.broadcasted_iota(I32, (A_WIDTH, A_WIDTH), 0) // A_HEAD_DIM
    hid_c = lax.broadcasted_iota(I32, (A_WIDTH, A_WIDTH), 1) // A_HEAD_DIM
    head_ones = (hid_r == hid_c).astype(BF16)
    for c, refs, gain_ref in ((_C_AQ, (q0_ref, q1_ref, q2_ref), gq_ref), (_C_AK, (k0_ref, k1_ref, k2_ref), gk_ref),
                              (_C_AV, (v0_ref, v1_ref, v2_ref), None)):
        val = seg(wb_ref, c)
        for g, ref in enumerate(refs):
            piece = val[:, g * A_WIDTH:(g + 1) * A_WIDTH]
            if gain_ref is not None:
                ss = jnp.dot((piece * piece).astype(BF16), head_ones, preferred_element_type=F32)
                piece = piece * lax.rsqrt(ss * (1.0 / A_HEAD_DIM) + EPS) * gain_ref[:, g * A_WIDTH:(g + 1) * A_WIDTH]
            _split_residues(piece, DILATED_PATTERNS[g][1], ref, st_ref)
    sgm_ref[...] = jax.nn.sigmoid(seg(wb_ref, _C_GM)).astype(BF16)
    sga_ref[...] = jax.nn.sigmoid(seg(wb_ref, _C_GA)).astype(BF16)

    kT_ref[...] = lax.dot_general(wa_ref[:, _C_MK[0]:_C_MK[1]], hb, _TN, preferred_element_type=F32).astype(BF16)
    zg = lax.dot_general(wg_ref[...], hb, _TN, preferred_element_type=F32)
    zi = zg[0:8] + gb_ref[0:8]
    zf = zg[M_HEADS:M_HEADS + 8] + gb_ref[8:16]
    gi_ref[...] = GATE_SOFTCAP * jnp.tanh(zi / GATE_SOFTCAP)
    gf_ref[...] = _log_sigmoid(GATE_SOFTCAP * jnp.tanh(zf / GATE_SOFTCAP))


def _stage_inproj(x2d, norm1_g, w_in, gate_b, gq, gk, batch, seq, tm):
    n = x2d.shape[0]
    steps = seq // tm
    wa = w_in[:, :_A_WIDTH].astype(BF16)
    wg = jnp.pad(w_in[:, _A_WIDTH:_B_START], ((0, 0), (0, LANES - IN_SPLITS[4]))).astype(BF16)
    wb = w_in[:, _B_START:].astype(BF16)
    gb = jnp.zeros((16, 1), F32)
    gb = gb.at[0:4, 0].set(gate_b[:M_HEADS].astype(F32)).at[8:12, 0].set(gate_b[M_HEADS:].astype(F32))
    g1 = norm1_g.astype(F32).reshape(1, D_MODEL)
    gq_t = (jnp.tile(gq.astype(F32), (1, A_HEADS)) * (A_HEAD_DIM ** -0.5)).reshape(1, N_GROUPS * A_WIDTH)
    gk_t = jnp.tile(gk.astype(F32), (1, A_HEADS)).reshape(1, N_GROUPS * A_WIDTH)

    row = lambda w: pl.BlockSpec((tm, w), lambda i: (i, 0))
    rowT = lambda r: pl.BlockSpec((r, tm), lambda i: (0, i))
    full = lambda a: pl.BlockSpec(a.shape, lambda i: (0,) * a.ndim)
    dils = [d for _, d in DILATED_PATTERNS]
    res_shape = lambda d: jax.ShapeDtypeStruct((batch, d, seq // d, A_WIDTH), BF16)
    res_spec = lambda d: pl.BlockSpec((1, d, tm // d, A_WIDTH), lambda i: (i // steps, 0, i % steps, 0))
    out_shapes = (
        jax.ShapeDtypeStruct((n, M_QK_WIDTH), BF16),
        jax.ShapeDtypeStruct((M_QK_WIDTH, n), BF16),
        jax.ShapeDtypeStruct((n, M_WIDTH), BF16),
        jax.ShapeDtypeStruct((n, M_WIDTH), BF16),
        jax.ShapeDtypeStruct((8, n), F32),
        jax.ShapeDtypeStruct((8, n), F32),
        *[res_shape(d) for d in dils], *[res_shape(d) for d in dils], *[res_shape(d) for d in dils],
        jax.ShapeDtypeStruct((n, D_MODEL), BF16),
        jax.ShapeDtypeStruct((n, D_MODEL), BF16),
    )
    out_specs = (row(M_QK_WIDTH), rowT(M_QK_WIDTH), row(M_WIDTH), row(M_WIDTH), rowT(8), rowT(8),
                 *[res_spec(d) for d in dils], *[res_spec(d) for d in dils], *[res_spec(d) for d in dils],
                 row(D_MODEL), row(D_MODEL))
    outs = pl.pallas_call(
        _inproj_body,
        grid=(n // tm,),
        in_specs=[row(D_MODEL), full(g1), full(wa), full(wg), full(wb), full(gb), full(gq_t), full(gk_t)],
        out_specs=out_specs,
        out_shape=out_shapes,
        scratch_shapes=[pltpu.VMEM((2, 2, tm, LANES), F32)],
        compiler_params=_params("parallel"),
        name="inproj",
    )(x2d, g1, wa, wg, wb, gb, gq_t, gk_t)
    mq, kT, mv, so, gi, gf = outs[:6]
    aq, ak, av = outs[6:9], outs[9:12], outs[12:15]
    return mq, kT, mv, so, gi, gf, aq, ak, av, outs[15], outs[16]


M_CHUNK_LEN = 128


def _mlstm_body(q_ref, v_ref, so_ref, ng_ref, *rest, nchunk, nseq):
    kT_refs, gi_refs, gf_refs = rest[0:nseq], rest[nseq:2 * nseq], rest[2 * nseq:3 * nseq]
    o_ref, c_ref, m_ref = rest[3 * nseq:]
    L = M_CHUNK_LEN

    @pl.when(pl.program_id(0) == 0)
    def _():
        c_ref[...] = jnp.zeros_like(c_ref)
        m_ref[...] = jnp.zeros_like(m_ref)

    lane8 = lax.broadcasted_iota(I32, (8, L), 1)
    causal = lax.broadcasted_iota(I32, (L, L), 1) <= lax.broadcasted_iota(I32, (L, L), 0)
    lo_half = lax.broadcasted_iota(I32, (L, LANES), 1) < M_QK_DIM
    ones = jnp.ones((L, M_V_DIM), BF16)

    heads = range(M_HEADS)
    cstate = [[c_ref[s, h * M_QK_DIM:(h + 1) * M_QK_DIM, :] for h in heads] for s in range(nseq)]
    m_prev = [m_ref[s, :, 0:1] for s in range(nseq)]
    units = []
    for c in range(nchunk):
        rows = slice(c * L, (c + 1) * L)
        for s in range(nseq):
            gi = gi_refs[s][:, rows]
            b = gf_refs[s][:, rows]
            sh = 1
            while sh < L:
                b = b + jnp.where(lane8 >= sh, pltpu.roll(b, sh, 1), 0.0)
                sh *= 2
            u = gi - b
            g = b[:, L - 1:L]
            a = g + u
            amax = jnp.max(a, axis=1, keepdims=True)
            m_new = jnp.maximum(g + m_prev[s], amax)
            w = jnp.exp(a - m_new) * (M_QK_DIM ** -0.5)
            s_old = jnp.exp(g + m_prev[s] - m_new)
            vext = [jnp.concatenate([v_ref[s, rows, h * M_V_DIM:(h + 1) * M_V_DIM], ones], axis=1) for h in heads]
            cloc = []
            for h in heads:
                hr = slice(h * M_QK_DIM, (h + 1) * M_QK_DIM)
                kw = (kT_refs[s][hr, rows].astype(F32) * w[h:h + 1, :]).astype(BF16)
                cloc.append(jnp.dot(kw, vext[h], preferred_element_type=F32))
            units.append(dict(seq=s, rows=rows, b=b, u=u, m_prev=m_prev[s], state=cstate[s], vext=vext))
            cstate[s] = [s_old[h:h + 1, :] * cstate[s][h] + cloc[h] for h in heads]
            m_prev[s] = m_new
    for s in range(nseq):
        for h in heads:
            c_ref[s, h * M_QK_DIM:(h + 1) * M_QK_DIM, :] = cstate[s][h]
        m_ref[s] = jnp.broadcast_to(m_prev[s], (8, LANES))

    for un in units:
        s, rows = un["seq"], un["rows"]
        un["s"], un["qc"] = [], []
        for p in range(M_HEADS // 2):
            lanes_p = slice(p * LANES, (p + 1) * LANES)
            q_pair = q_ref[s, rows, lanes_p]
            kT_pair = kT_refs[s][lanes_p, rows]
            c_pair = jnp.concatenate([un["state"][2 * p], un["state"][2 * p + 1]], axis=0).astype(BF16)
            for hh in range(2):
                qm = jnp.where(lo_half if hh == 0 else jnp.logical_not(lo_half), q_pair, jnp.zeros_like(q_pair))
                un["s"].append(jnp.dot(qm, kT_pair, preferred_element_type=F32) * (M_QK_DIM ** -0.5))
                un["qc"].append(jnp.dot(qm, c_pair, preferred_element_type=F32))

    for un in units:
        s, rows, b, u, mp = un["seq"], un["rows"], un["b"], un["u"], un["m_prev"]
        for h in heads:
            hl = slice(h * M_V_DIM, (h + 1) * M_V_DIM)
            bcol = jnp.transpose(jnp.broadcast_to(b[h:h + 1, :], (L, L)))
            dm = jnp.where(causal, bcol + u[h:h + 1, :], -jnp.inf)
            inter = bcol + mp[h:h + 1, :]
            m_t = jnp.maximum(inter, jnp.max(dm, axis=1, keepdims=True))
            pmat = (un["s"][h] * jnp.exp(dm - m_t)).astype(BF16)
            sc = jnp.exp(inter - m_t)
            out = (jnp.dot(pmat, un["vext"][h], preferred_element_type=F32)
                   + jnp.concatenate([sc, sc], axis=1) * un["qc"][h])
            hv = out[:, :M_V_DIM] / jnp.maximum(jnp.abs(out[:, M_V_DIM:]), jnp.exp(-m_t))
            hn = hv * lax.rsqrt(jnp.mean(hv * hv, axis=1, keepdims=True) + EPS)
            hn = hn * ng_ref[:, hl] * so_ref[s, rows, hl].astype(F32)
            o_ref[s, rows, hl] = hn.astype(BF16)


def _stage_mlstm(mq, kT, mv, so, gi, gf, norm_g, batch, seq, rows_per_step):
    n = batch * seq
    R = rows_per_step
    steps = seq // R
    ng = norm_g.astype(F32).reshape(1, M_WIDTH)
    per_seq = lambda a: a.reshape(batch, seq, a.shape[1])
    row = lambda w: pl.BlockSpec((batch, R, w), lambda i: (0, i, 0))
    colT = lambda r, s: pl.BlockSpec((r, R), lambda i, s=s: (0, s * steps + i))
    seqs = range(batch)
    out = pl.pallas_call(
        functools.partial(_mlstm_body, nchunk=R // M_CHUNK_LEN, nseq=batch),
        grid=(steps,),
        in_specs=[row(M_QK_WIDTH), row(M_WIDTH), row(M_WIDTH), pl.BlockSpec((1, M_WIDTH), lambda i: (0, 0)),
                  *[colT(M_QK_WIDTH, s) for s in seqs], *[colT(8, s) for s in seqs], *[colT(8, s) for s in seqs]],
        out_specs=row(M_WIDTH),
        out_shape=jax.ShapeDtypeStruct((batch, seq, M_WIDTH), BF16),
        scratch_shapes=[pltpu.VMEM((batch, M_QK_WIDTH, 2 * M_V_DIM), F32), pltpu.VMEM((batch, 8, LANES), F32)],
        compiler_params=_params("arbitrary"),
        name="mlstm",
    )(per_seq(mq), per_seq(mv), per_seq(so), ng, *[kT] * batch, *[gi] * batch, *[gf] * batch)
    return out.reshape(n, M_WIDTH)


def _attn_body(q_ref, kp_ref, kc_ref, vp_ref, vc_ref, o_ref, lse_ref, *, dil, slopes, lq):
    QB = N_BACK
    first = pl.program_id(2) == 0
    qn = q_ref[0, 0]
    kcn = kc_ref[0, 0]
    kpn = kp_ref[0, 0]
    vc = vc_ref[0, 0]
    vp = vp_ref[0, 0]

    qi = lax.broadcasted_iota(I32, (QB, 2 * QB), 0)
    kj = lax.broadcasted_iota(I32, (QB, 2 * QB), 1)
    dist = qi + QB - kj
    band = jnp.logical_and(dist >= 0, dist <= N_BACK)
    distf = (dist * dil).astype(F32)
    bias = [jnp.where(band, -float(slopes[h]) * distf, -jnp.inf) for h in range(A_HEADS)]
    no_prev = jnp.logical_and(first, kj < QB)
    lo_half = lax.broadcasted_iota(I32, (QB, LANES), 1) < A_HEAD_DIM
    ones = jnp.ones((2 * QB, LANES), BF16)

    units = []
    for j in range(lq // QB):
        rows = slice(j * QB, (j + 1) * QB)
        prow = slice((j - 1) * QB, j * QB)
        keys = jnp.concatenate([kpn if j == 0 else kcn[prow], kcn[rows]], axis=0)
        vals = jnp.concatenate([vp if j == 0 else vc[prow], vc[rows]], axis=0)
        for p in range(A_HEADS // 2):
            lanes_p = slice(p * LANES, (p + 1) * LANES)
            q_pair = qn[rows, lanes_p]
            k_pair = keys[:, lanes_p]
            vext = jnp.concatenate([vals[:, lanes_p], ones], axis=1)
            scores = []
            for hh in range(2):
                sel = lo_half if hh == 0 else jnp.logical_not(lo_half)
                qm = jnp.where(sel, q_pair, jnp.zeros_like(q_pair))
                scores.append(lax.dot_general(qm, k_pair, _NT, preferred_element_type=F32))
            units.append((j, rows, lanes_p, p, vext, scores))

    for j, rows, lanes_p, p, vext, scores in units:
        o_pair = None
        l_pair = None
        for hh in range(2):
            s = scores[hh] + bias[2 * p + hh]
            if j == 0:
                s = jnp.where(no_prev, -jnp.inf, s)
            m = jnp.max(s, axis=1, keepdims=True)
            pv = jnp.dot(jnp.exp(s - m).astype(BF16), vext, preferred_element_type=F32)
            den = pv[:, LANES:]
            o_h = pv[:, :LANES] / den
            l_h = m + jnp.log(den)
            o_pair = o_h if hh == 0 else jnp.where(lo_half, o_pair, o_h)
            l_pair = l_h if hh == 0 else jnp.where(lo_half, l_pair, l_h)
        o_ref[0, 0, rows, lanes_p] = o_pair.astype(BF16)
        lse_ref[0, 0, rows, lanes_p] = l_pair


def _stage_attn(aq, ak, av, batch, seq, group):
    _, dil = DILATED_PATTERNS[group]
    L = seq // dil
    assert L % N_BACK == 0
    lq = min(2048, L)
    nq = L // lq
    sub = lq // N_BACK
    cur = pl.BlockSpec((1, 1, lq, A_WIDTH), lambda b, r, i: (b, r, i, 0))
    prev = pl.BlockSpec((1, 1, N_BACK, A_WIDTH), lambda b, r, i: (b, r, jnp.maximum(i * sub - 1, 0), 0))
    return pl.pallas_call(
        functools.partial(_attn_body, dil=dil, slopes=tuple(_alibi_slopes()[group]), lq=lq),
        grid=(batch, dil, nq),
        in_specs=[cur, prev, cur, prev, cur],
        out_specs=(cur, cur),
        out_shape=(jax.ShapeDtypeStruct((batch, dil, L, A_WIDTH), BF16),
                   jax.ShapeDtypeStruct((batch, dil, L, A_WIDTH), F32)),
        compiler_params=_params("parallel", "parallel", "parallel"),
        name=f"dilated_attn_d{dil}",
    )(aq, ak, ak, av, av)


PACK_ROWS = D_MODEL // (2 * LANES)
U32 = jnp.uint32
_HIGH_HALF = 0xFFFF0000


def _pack_rows(val, exact=False):
    half = D_MODEL // 2
    lo, hi = val[:, :half], val[:, half:]
    if exact:
        return (lax.bitcast_convert_type(lo, U32) >> 16) | lax.bitcast_convert_type(hi, U32)

    def bits(v):
        return lax.bitcast_convert_type(v.astype(BF16).astype(F32), U32)

    return (bits(lo) >> 16) | (bits(hi) & U32(_HIGH_HALF))


def _unpack_rows(words):
    lo = lax.bitcast_convert_type(words << 16, F32).astype(BF16)
    hi = lax.bitcast_convert_type(words & U32(_HIGH_HALF), F32).astype(BF16)
    return jnp.concatenate([lo, hi], axis=1)


def _flat(ref):
    rows = 1
    for d in ref.shape[:-2]:
        rows *= d
    return ref.reshape(rows * PACK_ROWS, LANES)


def _store_packed(flat_ref, row0, words):
    t = words.shape[0]
    for s in range(PACK_ROWS):
        flat_ref[pl.ds(row0 * PACK_ROWS + s, t, stride=PACK_ROWS), :] = words[:, s * LANES:(s + 1) * LANES]


def _load_packed(flat_ref, row0, t):
    return jnp.concatenate([flat_ref[pl.ds(row0 * PACK_ROWS + s, t, stride=PACK_ROWS), :] for s in range(PACK_ROWS)],
                           axis=1)


def _rows8(vals):
    t = vals[0].shape[1]
    rid = lax.broadcasted_iota(I32, (8, t), 0)
    out = jnp.zeros((8, t), vals[0].dtype)
    for k, v in enumerate(vals):
        out = jnp.where(rid == k, jnp.broadcast_to(v, (8, t)), out)
    return out


def _merge_body(hm_ref, o1_ref, o2_ref, o3_ref, l1_ref, l2_ref, l3_ref, sgm_ref, sga_ref, x_ref,
                wm_ref, wa_ref, wo_ref, g2_ref, wrh_ref, br_ref,
                x2_ref, xn_ref, loc_ref, gate_ref, tcnt_ref, tcar_ref, cnt_ref, carry_ref, st_ref):
    @pl.when(pl.program_id(0) == 0)
    def _():
        carry_ref[...] = jnp.zeros_like(carry_ref)

    m_branch = jnp.dot(hm_ref[...], wm_ref[...], preferred_element_type=F32)
    dils = [d for _, d in DILATED_PATTERNS]
    l1, l2, l3 = [_merge_residues(r, d, st_ref) for r, d in zip((l1_ref, l2_ref, l3_ref), dils)]
    lmax = jnp.maximum(jnp.maximum(l1, l2), l3)
    e1, e2, e3 = jnp.exp(l1 - lmax), jnp.exp(l2 - lmax), jnp.exp(l3 - lmax)
    num = e1 * _merge_residues(o1_ref, dils[0], st_ref)
    num = num + e2 * _merge_residues(o2_ref, dils[1], st_ref)
    num = num + e3 * _merge_residues(o3_ref, dils[2], st_ref)
    h_a = num / (e1 + e2 + e3)
    y = (sgm_ref[...].astype(F32) * m_branch
         + sga_ref[...].astype(F32) * jnp.dot(h_a.astype(BF16), wa_ref[...], preferred_element_type=F32))
    x2 = x_ref[...] + jnp.dot(y.astype(BF16), wo_ref[...], preferred_element_type=F32)
    x2_ref[...] = x2
    xn = x2 * lax.rsqrt(jnp.mean(x2 * x2, axis=-1, keepdims=True) + EPS) * g2_ref[...]
    xh = xn.astype(BF16)
    xn_ref[...] = xh

    logits = lax.dot_general(wrh_ref[...], xh, _NT, preferred_element_type=F32) + br_ref[...]
    t = logits.shape[1]
    eid = lax.broadcasted_iota(I32, (N_EXPERTS, t), 0).astype(F32)
    vals = logits
    top_v, top_i = [], []
    for _ in range(TOP_K):
        mx = jnp.max(vals, axis=0, keepdims=True)
        ik = jnp.min(jnp.where(vals == mx, eid, float(N_EXPERTS)), axis=0, keepdims=True)
        top_v.append(mx)
        top_i.append(ik)
        vals = jnp.where(eid == ik, -jnp.inf, vals)
    ex = [jnp.exp(v - top_v[0]) for v in top_v]
    den = ex[0] + ex[1] + ex[2] + ex[3]
    gate_ref[...] = _rows8([e / den for e in ex])

    chosen = jnp.zeros((N_EXPERTS, t), F32)
    for ik in top_i:
        chosen = chosen + (eid == ik).astype(F32)
    before = jnp.where(lax.broadcasted_iota(jnp.int16, (t, t), 0) < lax.broadcasted_iota(jnp.int16, (t, t), 1),
                       jnp.ones((), BF16), jnp.zeros((), BF16))
    prefix = jnp.dot(chosen.astype(BF16), before, preferred_element_type=F32)
    tcount = jnp.broadcast_to(jnp.sum(chosen, axis=1, keepdims=True), (N_EXPERTS, LANES))
    below = (lax.broadcasted_iota(I32, (N_EXPERTS, N_EXPERTS), 1)
             < lax.broadcasted_iota(I32, (N_EXPERTS, N_EXPERTS), 0)).astype(BF16)
    t_hi = jnp.floor(tcount * (1.0 / BF16_EXACT_INT)) * BF16_EXACT_INT
    tile_off = (jnp.dot(below, t_hi.astype(BF16), preferred_element_type=F32)
                + jnp.dot(below, (tcount - t_hi).astype(BF16), preferred_element_type=F32))
    pos = prefix + tile_off[:, 0:1]
    loc_ref[...] = _rows8([jnp.sum(jnp.where(eid == ik, pos, 0.0), axis=0, keepdims=True).astype(I32)
                           for ik in top_i])
    carry = carry_ref[...]
    tcnt_ref[...] = tcount.astype(I32)
    tcar_ref[...] = carry.astype(I32)
    total = carry + tcount
    carry_ref[...] = total
    cnt_ref[...] = total


def _stage_merge(h_m, attn, sgm, sga, x2d, w_mb, w_ab, w_out, norm2_g, w_router, b_router, batch, seq, tm):
    n = x2d.shape[0]
    steps = seq // tm
    (o1, l1), (o2, l2), (o3, l3) = attn
    wm = w_mb.astype(BF16)
    wa = w_ab.astype(BF16)
    wo = w_out.astype(BF16)
    g2 = norm2_g.astype(F32).reshape(1, D_MODEL)
    wrh = w_router.astype(BF16).T
    br = b_router.astype(F32).reshape(N_EXPERTS, 1)
    row = lambda w: pl.BlockSpec((tm, w), lambda i: (i, 0))
    rowT = lambda r: pl.BlockSpec((r, tm), lambda i: (0, i))
    full = lambda a: pl.BlockSpec(a.shape, lambda i: (0,) * a.ndim)
    res = lambda d: pl.BlockSpec((1, d, tm // d, A_WIDTH), lambda i: (i // steps, 0, i % steps, 0))
    dils = [d for _, d in DILATED_PATTERNS]
    per_tile = pl.BlockSpec((N_EXPERTS, LANES), lambda i: (0, i))
    return pl.pallas_call(
        _merge_body,
        grid=(n // tm,),
        in_specs=[row(M_WIDTH), *[res(d) for d in dils], *[res(d) for d in dils],
                  row(D_MODEL), row(D_MODEL), row(D_MODEL),
                  full(wm), full(wa), full(wo), full(g2), full(wrh), full(br)],
        out_specs=(row(D_MODEL), row(D_MODEL), rowT(8), rowT(8), per_tile, per_tile,
                   pl.BlockSpec((N_EXPERTS, LANES), lambda i: (0, 0))),
        out_shape=(jax.ShapeDtypeStruct((n, D_MODEL), F32),
                   jax.ShapeDtypeStruct((n, D_MODEL), BF16),
                   jax.ShapeDtypeStruct((8, n), I32),
                   jax.ShapeDtypeStruct((8, n), F32),
                   jax.ShapeDtypeStruct((N_EXPERTS, (n // tm) * LANES), I32),
                   jax.ShapeDtypeStruct((N_EXPERTS, (n // tm) * LANES), I32),
                   jax.ShapeDtypeStruct((N_EXPERTS, LANES), F32)),
        scratch_shapes=[pltpu.VMEM((N_EXPERTS, LANES), F32), pltpu.VMEM((2, 2, tm, LANES), F32)],
        compiler_params=_params("arbitrary"),
        name="merge_route",
    )(h_m, o1, o2, o3, l1, l2, l3, sgm, sga, x2d, wm, wa, wo, g2, wrh, br)


def _offsets_body(cnt_ref, blk_ref, pstart_ref, zlo_ref, zhi_ref, *, nblk_pad):
    cnt = cnt_ref[...]
    padded = jnp.floor((cnt + (MOE_BLOCK - 1)) * (1.0 / MOE_BLOCK)) * MOE_BLOCK
    lower = (lax.broadcasted_iota(I32, (N_EXPERTS, N_EXPERTS), 1)
             <= lax.broadcasted_iota(I32, (N_EXPERTS, N_EXPERTS), 0)).astype(BF16)
    nb = padded * (1.0 / MOE_BLOCK)
    nb_hi = jnp.floor(nb * (1.0 / BF16_EXACT_INT)) * BF16_EXACT_INT
    pends = (jnp.dot(lower, nb_hi.astype(BF16), preferred_element_type=F32)
             + jnp.dot(lower, (nb - nb_hi).astype(BF16), preferred_element_type=F32)) * MOE_BLOCK
    pstart = pends - padded
    pstart_ref[...] = pstart.astype(I32)
    zlo_ref[...] = (pstart + cnt).astype(I32)
    zhi_ref[...] = pends.astype(I32)

    first_row = (lax.broadcasted_iota(I32, (N_EXPERTS, nblk_pad), 1) * MOE_BLOCK).astype(F32)
    pe = jnp.broadcast_to(pends[:, 0:1], (N_EXPERTS, nblk_pad))
    be = jnp.sum((pe <= first_row).astype(F32), axis=0, keepdims=True)
    be = jnp.minimum(be, float(N_EXPERTS - 1))
    nused = pends[N_EXPERTS - 1:N_EXPERTS, 0:1] * (1.0 / MOE_BLOCK)
    nonempty = jnp.broadcast_to(padded[:, 0:1], (N_EXPERTS, nblk_pad)) > 0.0
    runidx = jnp.sum(jnp.logical_and(pe <= first_row, nonempty).astype(F32), axis=0, keepdims=True)
    parity = runidx - 2.0 * jnp.floor(runidx * 0.5)
    eid = lax.broadcasted_iota(I32, (N_EXPERTS, nblk_pad), 0).astype(F32)
    later = jnp.logical_and(eid > be, nonempty)
    nxt = jnp.min(jnp.where(later, eid, float(N_EXPERTS)), axis=0, keepdims=True)
    blk_ref[...] = _rows8([be.astype(I32), jnp.broadcast_to(nused, (1, nblk_pad)).astype(I32),
                           parity.astype(I32), nxt.astype(I32)])


def _stage_offsets(cnt, nblk):
    nblk_pad = -(-nblk // LANES) * LANES
    const = lambda r, c: pl.BlockSpec((r, c), lambda i: (0, 0))
    per_expert = jax.ShapeDtypeStruct((N_EXPERTS, LANES), I32)
    return pl.pallas_call(
        functools.partial(_offsets_body, nblk_pad=nblk_pad),
        grid=(1,),
        in_specs=[const(N_EXPERTS, LANES)],
        out_specs=(const(8, nblk_pad), const(N_EXPERTS, LANES), const(N_EXPERTS, LANES), const(N_EXPERTS, LANES)),
        out_shape=(jax.ShapeDtypeStruct((8, nblk_pad), I32), per_expert, per_expert, per_expert),
        compiler_params=_params("arbitrary"),
        name="route_offsets",
    )(cnt)


RUN_BITS = 10


def _tile_rows(ref, first_row, nrows):
    return ref.at[pl.ds(first_row, nrows)]


def _for_each_piece(length, fn):
    for b in reversed(range(RUN_BITS)):
        @pl.when(((length >> b) & 1) == 1)
        def _(b=b):
            fn((length >> (b + 1)) << (b + 1), 1 << b)


def _for_each_run(tile, tcnt_ref, tcar_ref, pstart_ref, fn):
    def per_expert(e, local):
        count = tcnt_ref[tile, e]
        first = pstart_ref[e] + tcar_ref[tile, e]
        _for_each_piece(count, lambda off, size: fn(local + off, first + off, size))
        return local + count

    lax.fori_loop(0, N_EXPERTS, per_expert, 0)


PERM_CHUNK = 256


def _dispatch_body(tcnt_ref, tcar_ref, pstart_ref, zlo_ref, zhi_ref, loc_ref, xn_ref, xs_hbm,
                   buf_ref, sems, *, tm):
    step = pl.program_id(0)
    nloc = TOP_K * tm

    def wait_buffer(slot):
        pltpu.make_async_copy(buf_ref.at[slot], xs_hbm.at[pl.ds(0, nloc)], sems.at[slot]).wait()

    for slot in range(2):
        tile = 2 * step + slot

        @pl.when(step > 0)
        def _(slot=slot):
            wait_buffer(slot)

        loc = loc_ref[:, slot * tm:(slot + 1) * tm].astype(jnp.int16)
        xn = xn_ref[slot * tm:(slot + 1) * tm, :]
        for c in range(TOP_K * tm // PERM_CHUNK):
            lid = lax.broadcasted_iota(jnp.int16, (PERM_CHUNK, tm), 0) + jnp.int16(c * PERM_CHUNK)
            hit = lid == loc[0:1, :]
            for k in range(1, TOP_K):
                hit = jnp.logical_or(hit, lid == loc[k:k + 1, :])
            perm = jnp.where(hit, jnp.ones((), BF16), jnp.zeros((), BF16))
            rows = jnp.dot(perm, xn, preferred_element_type=F32)
            _store_packed(_flat(buf_ref), slot * nloc + c * PERM_CHUNK, _pack_rows(rows, exact=True))

        def run_copy(local, first, size, slot=slot):
            return pltpu.make_async_copy(_tile_rows(buf_ref.at[slot], local, size),
                                         _tile_rows(xs_hbm, first, size), sems.at[slot])

        _for_each_run(tile, tcnt_ref, tcar_ref, pstart_ref, lambda l, f, s: run_copy(l, f, s).start())

    @pl.when(step == pl.num_programs(0) - 1)
    def _():
        wait_buffer(0)
        wait_buffer(1)
        zsrc = buf_ref.at[0]
        zsrc[pl.ds(0, MOE_BLOCK)] = jnp.zeros((MOE_BLOCK, PACK_ROWS, LANES), U32)

        def zero_copy(first, size):
            return pltpu.make_async_copy(_tile_rows(zsrc, 0, size), _tile_rows(xs_hbm, first, size), sems.at[0])

        first_unused = zhi_ref[N_EXPERTS - 1] // MOE_BLOCK
        nblk = xs_hbm.shape[0] // MOE_BLOCK

        def for_each_zero_copy(action):
            def per_expert(e, carry):
                lo = zlo_ref[e]
                _for_each_piece(zhi_ref[e] - lo, lambda off, size: action(zero_copy(lo + off, size)))
                return carry

            def tail(blk, carry):
                action(zero_copy(blk * MOE_BLOCK, MOE_BLOCK))
                return carry

            lax.fori_loop(0, N_EXPERTS, per_expert, 0)
            lax.fori_loop(first_unused, nblk, tail, 0)

        for_each_zero_copy(lambda copy: copy.start())
        for_each_zero_copy(lambda copy: copy.wait())


def _stage_dispatch(tables, loc8, xn, nrows, tm):
    n = xn.shape[0]
    assert TOP_K * tm >= MOE_BLOCK and (n // tm) % 2 == 0
    grid_spec = pltpu.PrefetchScalarGridSpec(
        num_scalar_prefetch=5,
        grid=(n // (2 * tm),),
        in_specs=[pl.BlockSpec((8, 2 * tm), lambda i, *_: (0, i)),
                  pl.BlockSpec((2 * tm, D_MODEL), lambda i, *_: (i, 0))],
        out_specs=pl.BlockSpec(memory_space=pl.ANY),
        scratch_shapes=[pltpu.VMEM((2, TOP_K * tm, PACK_ROWS, LANES), U32), pltpu.SemaphoreType.DMA((2,))],
    )
    return pl.pallas_call(
        functools.partial(_dispatch_body, tm=tm),
        grid_spec=grid_spec,
        out_shape=jax.ShapeDtypeStruct((nrows, PACK_ROWS, LANES), U32),
        compiler_params=_params("arbitrary"),
        name="dispatch",
    )(*tables, loc8, xn)


EXPERT_BLOCKS_PER_STEP = 2


def _expert_body(be_ref, nu_ref, par_ref, nxt_ref, xs_ref, w1_hbm, b1_ref, w2_hbm, b2_ref, ys_ref,
                 w1f_ref, w2f_ref, w1b_ref, w2b_ref, sems):
    def fetch(expert, slot):
        return (pltpu.make_async_copy(w1_hbm.at[expert], w1f_ref.at[slot], sems.at[slot, 0]),
                pltpu.make_async_copy(w2_hbm.at[expert], w2f_ref.at[slot], sems.at[slot, 1]))

    def block_info(sub):
        j = pl.program_id(0) * EXPERT_BLOCKS_PER_STEP + sub
        used = j < nu_ref[0]
        jj = jnp.maximum(jnp.minimum(j, nu_ref[0] - 1), 0)
        e = be_ref[jj]
        fresh = jnp.logical_or(j == 0, e != be_ref[jnp.maximum(jj - 1, 0)])
        return j, jj, e, used, fresh

    def refresh(j, jj, e):
        slot = par_ref[jj]

        @pl.when(j == 0)
        def _():
            for c in fetch(e, slot):
                c.start()

        for c in fetch(e, slot):
            c.wait()
        nxt = nxt_ref[jj]

        @pl.when(nxt < N_EXPERTS)
        def _():
            for c in fetch(nxt, 1 - slot):
                c.start()

        w1b_ref[...] = w1f_ref[slot].astype(BF16)
        w2b_ref[...] = w2f_ref[slot].astype(BF16)

    def compute(e, row0, nrows):
        xb = _unpack_rows(_load_packed(_flat(xs_ref), row0, nrows))
        gu = jnp.dot(xb, w1b_ref[...], preferred_element_type=F32) + b1_ref[pl.ds(e, 1), :]
        gate = jnp.minimum(gu[:, :D_FF], SWIGLU_LIMIT)
        lin = jnp.clip(gu[:, D_FF:], -SWIGLU_LIMIT, SWIGLU_LIMIT)
        act = (lin + 1.0) * (gate * jax.nn.sigmoid(SWIGLU_ALPHA * gate))
        ys = jnp.dot(act.astype(BF16), w2b_ref[...], preferred_element_type=F32) + b2_ref[pl.ds(e, 1), :]
        _store_packed(_flat(ys_ref), row0, _pack_rows(ys))

    j0, jj0, e0, used0, fresh0 = block_info(0)
    j1, jj1, e1, used1, fresh1 = block_info(1)
    same = jnp.logical_and(used1, e1 == e0)

    @pl.when(jnp.logical_and(used0, fresh0))
    def _():
        refresh(j0, jj0, e0)

    @pl.when(jnp.logical_and(used0, same))
    def _():
        compute(e0, 0, 2 * MOE_BLOCK)

    @pl.when(jnp.logical_and(used0, jnp.logical_not(same)))
    def _():
        compute(e0, 0, MOE_BLOCK)

    @pl.when(jnp.logical_and(used1, fresh1))
    def _():
        refresh(j1, jj1, e1)

    @pl.when(jnp.logical_and(used1, jnp.logical_not(same)))
    def _():
        compute(e1, MOE_BLOCK, MOE_BLOCK)

    for sub, used in ((0, used0), (1, used1)):
        @pl.when(jnp.logical_not(used))
        def _(sub=sub):
            ys_ref[pl.ds(sub * MOE_BLOCK, MOE_BLOCK)] = jnp.zeros((MOE_BLOCK, PACK_ROWS, LANES), U32)


def _stage_experts(blk8, xs, w1, b1, w2, b2):
    nrows = xs.shape[0]
    nblk = nrows // MOE_BLOCK
    assert EXPERT_BLOCKS_PER_STEP == 2 and nblk % EXPERT_BLOCKS_PER_STEP == 0
    block_e, nused, parity, nxt = blk8[0, :nblk], blk8[1, :1], blk8[2, :nblk], blk8[3, :nblk]
    tiles = (EXPERT_BLOCKS_PER_STEP * MOE_BLOCK, PACK_ROWS, LANES)
    full = lambda a: pl.BlockSpec(a.shape, lambda j, *_: (0,) * a.ndim)
    grid_spec = pltpu.PrefetchScalarGridSpec(
        num_scalar_prefetch=4,
        grid=(nblk // EXPERT_BLOCKS_PER_STEP,),
        in_specs=[pl.BlockSpec(tiles, lambda j, *_: (j, 0, 0)),
                  pl.BlockSpec(memory_space=pl.ANY), full(b1),
                  pl.BlockSpec(memory_space=pl.ANY), full(b2)],
        out_specs=pl.BlockSpec(tiles, lambda j, *_: (j, 0, 0)),
        scratch_shapes=[pltpu.VMEM((2, D_MODEL, 2 * D_FF), F32), pltpu.VMEM((2, D_FF, D_MODEL), F32),
                        pltpu.VMEM((D_MODEL, 2 * D_FF), BF16), pltpu.VMEM((D_FF, D_MODEL), BF16),
                        pltpu.SemaphoreType.DMA((2, 2))],
    )
    return pl.pallas_call(
        _expert_body,
        grid_spec=grid_spec,
        out_shape=jax.ShapeDtypeStruct((nrows, PACK_ROWS, LANES), U32),
        compiler_params=_params("arbitrary"),
        name="experts",
    )(block_e, nused, parity, nxt, xs, w1, b1, w2, b2)


COMBINE_CHUNK = 512


def _combine_body(tcnt_ref, tcar_ref, pstart_ref, loc_ref, gate_ref, x2_ref, ys_hbm, out_ref,
                  buf_ref, g_ref, sems, *, tm):
    step = pl.program_id(0)
    nloc = TOP_K * tm

    def start_runs(tile, slot):
        def run_copy(local, first, size):
            return pltpu.make_async_copy(_tile_rows(ys_hbm, first, size),
                                         _tile_rows(buf_ref.at[slot], local, size), sems.at[slot])
        _for_each_run(tile, tcnt_ref, tcar_ref, pstart_ref, lambda l, f, s: run_copy(l, f, s).start())

    def wait_buffer(slot):
        pltpu.make_async_copy(ys_hbm.at[pl.ds(0, nloc)], buf_ref.at[slot], sems.at[slot]).wait()

    def combine(slot):
        zpad = jnp.zeros((LANES - 16, LANES), F32)
        cols = []
        for c in range(tm // LANES):
            cols_in = slice(slot * tm + c * LANES, slot * tm + (c + 1) * LANES)
            cols.append(jnp.transpose(jnp.concatenate([loc_ref[:, cols_in].astype(F32), gate_ref[:, cols_in], zpad],
                                                      axis=0)))
        rows = slice(slot * tm, (slot + 1) * tm)
        acc = x2_ref[rows, :]
        for lc in range(nloc // COMBINE_CHUNK):
            lane = lax.broadcasted_iota(jnp.int16, (LANES, COMBINE_CHUNK), 1) + jnp.int16(lc * COMBINE_CHUNK)
            for c in range(tm // LANES):
                g = jnp.zeros((LANES, COMBINE_CHUNK), BF16)
                for k in range(TOP_K):
                    g = jnp.where(lane == cols[c][:, k:k + 1].astype(jnp.int16), cols[c][:, 8 + k:9 + k].astype(BF16), g)
                g_ref[c * LANES:(c + 1) * LANES, lc * COMBINE_CHUNK:(lc + 1) * COMBINE_CHUNK] = g
            if lc == 0:
                wait_buffer(slot)
            ys = _unpack_rows(_load_packed(_flat(buf_ref), slot * nloc + lc * COMBINE_CHUNK, COMBINE_CHUNK))
            acc = acc + jnp.dot(g_ref[:, lc * COMBINE_CHUNK:(lc + 1) * COMBINE_CHUNK], ys, preferred_element_type=F32)
        out_ref[rows, :] = acc

    @pl.when(step == 0)
    def _():
        start_runs(0, 0)

    start_runs(2 * step + 1, 1)
    combine(0)

    @pl.when(step + 1 < pl.num_programs(0))
    def _():
        start_runs(2 * step + 2, 0)

    combine(1)


def _stage_combine(tables, loc8, gate8, x2, ys, tm):
    n = x2.shape[0]
    assert (n // tm) % 2 == 0
    grid_spec = pltpu.PrefetchScalarGridSpec(
        num_scalar_prefetch=3,
        grid=(n // (2 * tm),),
        in_specs=[pl.BlockSpec((8, 2 * tm), lambda i, *_: (0, i)),
                  pl.BlockSpec((8, 2 * tm), lambda i, *_: (0, i)),
                  pl.BlockSpec((2 * tm, D_MODEL), lambda i, *_: (i, 0)),
                  pl.BlockSpec(memory_space=pl.ANY)],
        out_specs=pl.BlockSpec((2 * tm, D_MODEL), lambda i, *_: (i, 0)),
        scratch_shapes=[pltpu.VMEM((2, TOP_K * tm, PACK_ROWS, LANES), U32),
                        pltpu.VMEM((tm, TOP_K * tm), BF16),
                        pltpu.SemaphoreType.DMA((2,))],
    )
    return pl.pallas_call(
        functools.partial(_combine_body, tm=tm),
        grid_spec=grid_spec,
        out_shape=jax.ShapeDtypeStruct((n, D_MODEL), F32),
        compiler_params=_params("arbitrary"),
        name="combine",
    )(*tables, loc8, gate8, x2, ys)


def _moe(x2, xn, loc8, gate8, tcnt, tcar, cnt, w1, b1, w2, b2, tm):
    n = x2.shape[0]
    ntile = n // tm
    nblk = -(-(n * TOP_K) // MOE_BLOCK) + N_EXPERTS
    blk8, pstart, zlo, zhi = _stage_offsets(cnt, nblk)
    per_tile = lambda a: a.reshape(N_EXPERTS, ntile, LANES)[:, :, 0].T
    tables = (per_tile(tcnt), per_tile(tcar), pstart[:, 0])
    xs = _stage_dispatch(tables + (zlo[:, 0], zhi[:, 0]), loc8, xn, nblk * MOE_BLOCK, tm)
    ys = _stage_experts(blk8, xs, w1, b1, w2, b2)
    return _stage_combine(tables, loc8, gate8, x2, ys, tm)


def kernel(x, norm1_g, w_in, mlstm_gate_b, mlstm_norm_g, attn_q_norm_g, attn_k_norm_g, w_mlstm_branch,
           w_attn_branch, w_out, norm2_g, w_router, b_router, w1, b1, w2, b2):
    batch, seq, _ = x.shape
    n = batch * seq
    for l in range(norm1_g.shape[0]):
        x2d = x.reshape(n, D_MODEL)
        tm = min(512, seq)
        mq, kT, mv, so, gi, gf, aq, ak, av, sgm, sga = _stage_inproj(
            x2d, norm1_g[l], w_in[l], mlstm_gate_b[l], attn_q_norm_g[l], attn_k_norm_g[l], batch, seq, tm)
        h_m = _stage_mlstm(mq, kT, mv, so, gi, gf, mlstm_norm_g[l], batch, seq, tm)
        attn = [_stage_attn(aq[g], ak[g], av[g], batch, seq, g)
                for g in range(N_GROUPS)]
        x2, xn, loc8, gate8, tcnt, tcar, cnt = _stage_merge(
            h_m, attn, sgm, sga, x2d, w_mlstm_branch[l], w_attn_branch[l], w_out[l], norm2_g[l],
            w_router[l], b_router[l], batch, seq, tm)
        out = _moe(x2, xn, loc8, gate8, tcnt, tcar, cnt, w1[l], b1[l], w2[l], b2[l], tm)
        x = out.reshape(batch, seq, D_MODEL)
    return x
```

```python
import functools

import numpy as np
import jax
import jax.numpy as jnp
from jax import lax
from jax.experimental import pallas as pl
from jax.experimental.pallas import tpu as pltpu

F32 = jnp.float32
BF16 = jnp.bfloat16
I32 = jnp.int32

D_MODEL = 1024
M_HEADS = 4
M_QK_DIM = 64
M_V_DIM = 128
GATE_SOFTCAP = 15.0
A_HEADS = 4
A_HEAD_DIM = 64
DILATED_PATTERNS = ((128, 1), (512, 4), (2048, 16))
N_GROUPS = len(DILATED_PATTERNS)
N_BACK = 128
N_EXPERTS = 32
TOP_K = 4
D_FF = 1024
SWIGLU_LIMIT = 7.0
SWIGLU_ALPHA = 1.702
MOE_BLOCK = 512
EPS = 1e-6

M_WIDTH = M_HEADS * M_V_DIM
M_QK_WIDTH = M_HEADS * M_QK_DIM
A_WIDTH = A_HEADS * A_HEAD_DIM
IN_SPLITS = (M_QK_WIDTH, M_QK_WIDTH, M_WIDTH, M_WIDTH, 2 * M_HEADS,
             N_GROUPS * A_WIDTH, N_GROUPS * A_WIDTH, N_GROUPS * A_WIDTH, D_MODEL, D_MODEL)

LANES = 128
VMEM_LIMIT = 56 * 1024 * 1024

BF16_EXACT_INT = 256.0

_NT = (((1,), (1,)), ((), ()))
_TN = (((0,), (1,)), ((), ()))


def _alibi_slopes():
    n = N_GROUPS * A_HEADS
    s = np.exp2(-8.0 * np.arange(1, n + 1) / n).astype(np.float32)
    return s.reshape(N_GROUPS, A_HEADS)


def _params(*sem):
    return pltpu.CompilerParams(dimension_semantics=sem, vmem_limit_bytes=VMEM_LIMIT)


def _log_sigmoid(x):
    return jnp.minimum(x, 0.0) - jnp.log1p(jnp.exp(-jnp.abs(x)))


_A_WIDTH = sum(IN_SPLITS[:4])
_B_START = _A_WIDTH + IN_SPLITS[4]


def _piece_segments(widths):
    bounds, start = [], 0
    for width in widths:
        bounds.append((start, start + width))
        start += width
    return bounds


_C_MQ, _C_MK, _C_MV, _C_MO = _piece_segments(IN_SPLITS[:4])
_C_AQ, _C_AK, _C_AV, _C_GM, _C_GA = _piece_segments(IN_SPLITS[5:])


def _split_residues(val, d, out_ref, st_ref):
    t = val.shape[0]
    if d == 1:
        out_ref[0, 0] = val.astype(out_ref.dtype)
        return
    a_ref = st_ref.at[0]
    a_ref[0] = val[:, :LANES]
    a_ref[1] = val[:, LANES:]
    m = t // d
    if d == 16:
        b_ref = st_ref.at[1]
        for r0 in range(4):
            b_ref[0, r0 * 4 * m:(r0 + 1) * 4 * m, :] = a_ref[0, pl.ds(r0, 4 * m, stride=4), :]
            b_ref[1, r0 * 4 * m:(r0 + 1) * 4 * m, :] = a_ref[1, pl.ds(r0, 4 * m, stride=4), :]
        for r0 in range(4):
            for r1 in range(4):
                piece = jnp.concatenate([b_ref[0, pl.ds(r0 * 4 * m + r1, m, stride=4), :],
                                         b_ref[1, pl.ds(r0 * 4 * m + r1, m, stride=4), :]], axis=1)
                out_ref[0, 4 * r1 + r0] = piece.astype(out_ref.dtype)
        return
    for r in range(d):
        piece = jnp.concatenate([a_ref[0, pl.ds(r, m, stride=d), :], a_ref[1, pl.ds(r, m, stride=d), :]], axis=1)
        out_ref[0, r] = piece.astype(out_ref.dtype)


def _merge_residues(ref, d, st_ref):
    if d == 1:
        return ref[0, 0].astype(F32)
    m = ref.shape[2]
    if d == 16:
        a_ref, b_ref = st_ref.at[0], st_ref.at[1]
        for r0 in range(4):
            for r1 in range(4):
                blk = ref[0, 4 * r1 + r0].astype(F32)
                a_ref[0, pl.ds(r0 * 4 * m + r1, m, stride=4), :] = blk[:, :LANES]
                a_ref[1, pl.ds(r0 * 4 * m + r1, m, stride=4), :] = blk[:, LANES:]
        for r0 in range(4):
            b_ref[0, pl.ds(r0, 4 * m, stride=4), :] = a_ref[0, r0 * 4 * m:(r0 + 1) * 4 * m, :]
            b_ref[1, pl.ds(r0, 4 * m, stride=4), :] = a_ref[1, r0 * 4 * m:(r0 + 1) * 4 * m, :]
        return jnp.concatenate([b_ref[0], b_ref[1]], axis=1)
    a_ref = st_ref.at[0]
    for r in range(d):
        blk = ref[0, r].astype(F32)
        a_ref[0, pl.ds(r, m, stride=d), :] = blk[:, :LANES]
        a_ref[1, pl.ds(r, m, stride=d), :] = blk[:, LANES:]
    return jnp.concatenate([a_ref[0], a_ref[1]], axis=1)


def _inproj_body(x_ref, g1_ref, wa_ref, wg_ref, wb_ref, gb_ref, gq_ref, gk_ref,
                 mq_ref, kT_ref, mv_ref, so_ref, gi_ref, gf_ref,
                 q0_ref, q1_ref, q2_ref, k0_ref, k1_ref, k2_ref, v0_ref, v1_ref, v2_ref,
                 sgm_ref, sga_ref, st_ref):
    x = x_ref[...]
    h = x * lax.rsqrt(jnp.mean(x * x, axis=-1, keepdims=True) + EPS) * g1_ref[...]
    hb = h.astype(BF16)

    def seg(w_ref, c):
        return jnp.dot(hb, w_ref[:, c[0]:c[1]], preferred_element_type=F32)

    mq_ref[...] = seg(wa_ref, _C_MQ).astype(BF16)
    mv_ref[...] = seg(wa_ref, _C_MV).astype(BF16)
    so_ref[...] = jax.nn.sigmoid(seg(wa_ref, _C_MO)).astype(BF16)
    hid_r = lax.broadcasted_iota(I32, (A_WIDTH, A_WIDTH), 0) // A_HEAD_DIM
    hid_c = lax.broadcasted_iota(I32, (A_WIDTH, A_WIDTH), 1) // A_HEAD_DIM
    head_ones = (hid_r == hid_c).astype(BF16)
    for c, refs, gain_ref in ((_C_AQ, (q0_ref, q1_ref, q2_ref), gq_ref), (_C_AK, (k0_ref, k1_ref, k2_ref), gk_ref),
                              (_C_AV, (v0_ref, v1_ref, v2_ref), None)):
        val = seg(wb_ref, c)
        for g, ref in enumerate(refs):
            piece = val[:, g * A_WIDTH:(g + 1) * A_WIDTH]
            if gain_ref is not None:
                ss = jnp.dot((piece * piece).astype(BF16), head_ones, preferred_element_type=F32)
                piece = piece * lax.rsqrt(ss * (1.0 / A_HEAD_DIM) + EPS) * gain_ref[:, g * A_WIDTH:(g + 1) * A_WIDTH]
            _split_residues(piece, DILATED_PATTERNS[g][1], ref, st_ref)
    sgm_ref[...] = jax.nn.sigmoid(seg(wb_ref, _C_GM)).astype(BF16)
    sga_ref[...] = jax.nn.sigmoid(seg(wb_ref, _C_GA)).astype(BF16)

    kT_ref[...] = lax.dot_general(wa_ref[:, _C_MK[0]:_C_MK[1]], hb, _TN, preferred_element_type=F32).astype(BF16)
    zg = lax.dot_general(wg_ref[...], hb, _TN, preferred_element_type=F32)
    zi = zg[0:8] + gb_ref[0:8]
    zf = zg[M_HEADS:M_HEADS + 8] + gb_ref[8:16]
    gi_ref[...] = GATE_SOFTCAP * jnp.tanh(zi / GATE_SOFTCAP)
    gf_ref[...] = _log_sigmoid(GATE_SOFTCAP * jnp.tanh(zf / GATE_SOFTCAP))


def _stage_inproj(x2d, norm1_g, w_in, gate_b, gq, gk, batch, seq, tm):
    n = x2d.shape[0]
    steps = seq // tm
    wa = w_in[:, :_A_WIDTH].astype(BF16)
    wg = jnp.pad(w_in[:, _A_WIDTH:_B_START], ((0, 0), (0, LANES - IN_SPLITS[4]))).astype(BF16)
    wb = w_in[:, _B_START:].astype(BF16)
    gb = jnp.zeros((16, 1), F32)
    gb = gb.at[0:4, 0].set(gate_b[:M_HEADS].astype(F32)).at[8:12, 0].set(gate_b[M_HEADS:].astype(F32))
    g1 = norm1_g.astype(F32).reshape(1, D_MODEL)
    gq_t = (jnp.tile(gq.astype(F32), (1, A_HEADS)) * (A_HEAD_DIM ** -0.5)).reshape(1, N_GROUPS * A_WIDTH)
    gk_t = jnp.tile(gk.astype(F32), (1, A_HEADS)).reshape(1, N_GROUPS * A_WIDTH)

    row = lambda w: pl.BlockSpec((tm, w), lambda i: (i, 0))
    rowT = lambda r: pl.BlockSpec((r, tm), lambda i: (0, i))
    full = lambda a: pl.BlockSpec(a.shape, lambda i: (0,) * a.ndim)
    dils = [d for _, d in DILATED_PATTERNS]
    res_shape = lambda d: jax.ShapeDtypeStruct((batch, d, seq // d, A_WIDTH), BF16)
    res_spec = lambda d: pl.BlockSpec((1, d, tm // d, A_WIDTH), lambda i: (i // steps, 0, i % steps, 0))
    out_shapes = (
        jax.ShapeDtypeStruct((n, M_QK_WIDTH), BF16),
        jax.ShapeDtypeStruct((M_QK_WIDTH, n), BF16),
        jax.ShapeDtypeStruct((n, M_WIDTH), BF16),
        jax.ShapeDtypeStruct((n, M_WIDTH), BF16),
        jax.ShapeDtypeStruct((8, n), F32),
        jax.ShapeDtypeStruct((8, n), F32),
        *[res_shape(d) for d in dils], *[res_shape(d) for d in dils], *[res_shape(d) for d in dils],
        jax.ShapeDtypeStruct((n, D_MODEL), BF16),
        jax.ShapeDtypeStruct((n, D_MODEL), BF16),
    )
    out_specs = (row(M_QK_WIDTH), rowT(M_QK_WIDTH), row(M_WIDTH), row(M_WIDTH), rowT(8), rowT(8),
                 *[res_spec(d) for d in dils], *[res_spec(d) for d in dils], *[res_spec(d) for d in dils],
                 row(D_MODEL), row(D_MODEL))
    outs = pl.pallas_call(
        _inproj_body,
        grid=(n // tm,),
        in_specs=[row(D_MODEL), full(g1), full(wa), full(wg), full(wb), full(gb), full(gq_t), full(gk_t)],
        out_specs=out_specs,
        out_shape=out_shapes,
        scratch_shapes=[pltpu.VMEM((2, 2, tm, LANES), F32)],
        compiler_params=_params("parallel"),
        name="inproj",
    )(x2d, g1, wa, wg, wb, gb, gq_t, gk_t)
    mq, kT, mv, so, gi, gf = outs[:6]
    aq, ak, av = outs[6:9], outs[9:12], outs[12:15]
    return mq, kT, mv, so, gi, gf, aq, ak, av, outs[15], outs[16]


M_CHUNK_LEN = 128


def _mlstm_body(q_ref, v_ref, so_ref, ng_ref, *rest, nchunk, nseq):
    kT_refs, gi_refs, gf_refs = rest[0:nseq], rest[nseq:2 * nseq], rest[2 * nseq:3 * nseq]
    o_ref, c_ref, m_ref = rest[3 * nseq:]
    L = M_CHUNK_LEN

    @pl.when(pl.program_id(0) == 0)
    def _():
        c_ref[...] = jnp.zeros_like(c_ref)
        m_ref[...] = jnp.zeros_like(m_ref)

    lane8 = lax.broadcasted_iota(I32, (8, L), 1)
    causal = lax.broadcasted_iota(I32, (L, L), 1) <= lax.broadcasted_iota(I32, (L, L), 0)
    lo_half = lax.broadcasted_iota(I32, (L, LANES), 1) < M_QK_DIM
    ones = jnp.ones((L, M_V_DIM), BF16)

    heads = range(M_HEADS)
    cstate = [[c_ref[s, h * M_QK_DIM:(h + 1) * M_QK_DIM, :] for h in heads] for s in range(nseq)]
    m_prev = [m_ref[s, :, 0:1] for s in range(nseq)]
    units = []
    for c in range(nchunk):
        rows = slice(c * L, (c + 1) * L)
        for s in range(nseq):
            gi = gi_refs[s][:, rows]
            b = gf_refs[s][:, rows]
            sh = 1
            while sh < L:
                b = b + jnp.where(lane8 >= sh, pltpu.roll(b, sh, 1), 0.0)
                sh *= 2
            u = gi - b
            g = b[:, L - 1:L]
            a = g + u
            amax = jnp.max(a, axis=1, keepdims=True)
            m_new = jnp.maximum(g + m_prev[s], amax)
            w = jnp.exp(a - m_new) * (M_QK_DIM ** -0.5)
            s_old = jnp.exp(g + m_prev[s] - m_new)
            vext = [jnp.concatenate([v_ref[s, rows, h * M_V_DIM:(h + 1) * M_V_DIM], ones], axis=1) for h in heads]
            cloc = []
            for h in heads:
                hr = slice(h * M_QK_DIM, (h + 1) * M_QK_DIM)
                kw = (kT_refs[s][hr, rows].astype(F32) * w[h:h + 1, :]).astype(BF16)
                cloc.append(jnp.dot(kw, vext[h], preferred_element_type=F32))
            units.append(dict(seq=s, rows=rows, b=b, u=u, m_prev=m_prev[s], state=cstate[s], vext=vext))
            cstate[s] = [s_old[h:h + 1, :] * cstate[s][h] + cloc[h] for h in heads]
            m_prev[s] = m_new
    for s in range(nseq):
        for h in heads:
            c_ref[s, h * M_QK_DIM:(h + 1) * M_QK_DIM, :] = cstate[s][h]
        m_ref[s] = jnp.broadcast_to(m_prev[s], (8, LANES))

    for un in units:
        s, rows = un["seq"], un["rows"]
        un["s"], un["qc"] = [], []
        for p in range(M_HEADS // 2):
            lanes_p = slice(p * LANES, (p + 1) * LANES)
            q_pair = q_ref[s, rows, lanes_p]
            kT_pair = kT_refs[s][lanes_p, rows]
            c_pair = jnp.concatenate([un["state"][2 * p], un["state"][2 * p + 1]], axis=0).astype(BF16)
            for hh in range(2):
                qm = jnp.where(lo_half if hh == 0 else jnp.logical_not(lo_half), q_pair, jnp.zeros_like(q_pair))
                un["s"].append(jnp.dot(qm, kT_pair, preferred_element_type=F32) * (M_QK_DIM ** -0.5))
                un["qc"].append(jnp.dot(qm, c_pair, preferred_element_type=F32))

    for un in units:
        s, rows, b, u, mp = un["seq"], un["rows"], un["b"], un["u"], un["m_prev"]
        for h in heads:
            hl = slice(h * M_V_DIM, (h + 1) * M_V_DIM)
            bcol = jnp.transpose(jnp.broadcast_to(b[h:h + 1, :], (L, L)))
            dm = jnp.where(causal, bcol + u[h:h + 1, :], -jnp.inf)
            inter = bcol + mp[h:h + 1, :]
            m_t = jnp.maximum(inter, jnp.max(dm, axis=1, keepdims=True))
            pmat = (un["s"][h] * jnp.exp(dm - m_t)).astype(BF16)
            sc = jnp.exp(inter - m_t)
            out = (jnp.dot(pmat, un["vext"][h], preferred_element_type=F32)
                   + jnp.concatenate([sc, sc], axis=1) * un["qc"][h])
            hv = out[:, :M_V_DIM] / jnp.maximum(jnp.abs(out[:, M_V_DIM:]), jnp.exp(-m_t))
            hn = hv * lax.rsqrt(jnp.mean(hv * hv, axis=1, keepdims=True) + EPS)
            hn = hn * ng_ref[:, hl] * so_ref[s, rows, hl].astype(F32)
            o_ref[s, rows, hl] = hn.astype(BF16)


def _stage_mlstm(mq, kT, mv, so, gi, gf, norm_g, batch, seq, rows_per_step):
    n = batch * seq
    R = rows_per_step
    steps = seq // R
    ng = norm_g.astype(F32).reshape(1, M_WIDTH)
    per_seq = lambda a: a.reshape(batch, seq, a.shape[1])
    row = lambda w: pl.BlockSpec((batch, R, w), lambda i: (0, i, 0))
    colT = lambda r, s: pl.BlockSpec((r, R), lambda i, s=s: (0, s * steps + i))
    seqs = range(batch)
    out = pl.pallas_call(
        functools.partial(_mlstm_body, nchunk=R // M_CHUNK_LEN, nseq=batch),
        grid=(steps,),
        in_specs=[row(M_QK_WIDTH), row(M_WIDTH), row(M_WIDTH), pl.BlockSpec((1, M_WIDTH), lambda i: (0, 0)),
                  *[colT(M_QK_WIDTH, s) for s in seqs], *[colT(8, s) for s in seqs], *[colT(8, s) for s in seqs]],
        out_specs=row(M_WIDTH),
        out_shape=jax.ShapeDtypeStruct((batch, seq, M_WIDTH), BF16),
        scratch_shapes=[pltpu.VMEM((batch, M_QK_WIDTH, 2 * M_V_DIM), F32), pltpu.VMEM((batch, 8, LANES), F32)],
        compiler_params=_params("arbitrary"),
        name="mlstm",
    )(per_seq(mq), per_seq(mv), per_seq(so), ng, *[kT] * batch, *[gi] * batch, *[gf] * batch)
    return out.reshape(n, M_WIDTH)


def _attn_body(q_ref, kp_ref, kc_ref, vp_ref, vc_ref, o_ref, lse_ref, *, dil, slopes, lq):
    QB = N_BACK
    first = pl.program_id(2) == 0
    qn = q_ref[0, 0]
    kcn = kc_ref[0, 0]
    kpn = kp_ref[0, 0]
    vc = vc_ref[0, 0]
    vp = vp_ref[0, 0]

    qi = lax.broadcasted_iota(I32, (QB, 2 * QB), 0)
    kj = lax.broadcasted_iota(I32, (QB, 2 * QB), 1)
    dist = qi + QB - kj
    band = jnp.logical_and(dist >= 0, dist <= N_BACK)
    distf = (dist * dil).astype(F32)
    bias = [jnp.where(band, -float(slopes[h]) * distf, -jnp.inf) for h in range(A_HEADS)]
    no_prev = jnp.logical_and(first, kj < QB)
    lo_half = lax.broadcasted_iota(I32, (QB, LANES), 1) < A_HEAD_DIM
    ones = jnp.ones((2 * QB, LANES), BF16)

    units = []
    for j in range(lq // QB):
        rows = slice(j * QB, (j + 1) * QB)
        prow = slice((j - 1) * QB, j * QB)
        keys = jnp.concatenate([kpn if j == 0 else kcn[prow], kcn[rows]], axis=0)
        vals = jnp.concatenate([vp if j == 0 else vc[prow], vc[rows]], axis=0)
        for p in range(A_HEADS // 2):
            lanes_p = slice(p * LANES, (p + 1) * LANES)
            q_pair = qn[rows, lanes_p]
            k_pair = keys[:, lanes_p]
            vext = jnp.concatenate([vals[:, lanes_p], ones], axis=1)
            scores = []
            for hh in range(2):
                sel = lo_half if hh == 0 else jnp.logical_not(lo_half)
                qm = jnp.where(sel, q_pair, jnp.zeros_like(q_pair))
                scores.append(lax.dot_general(qm, k_pair, _NT, preferred_element_type=F32))
            units.append((j, rows, lanes_p, p, vext, scores))

    for j, rows, lanes_p, p, vext, scores in units:
        o_pair = None
        l_pair = None
        for hh in range(2):
            s = scores[hh] + bias[2 * p + hh]
            if j == 0:
                s = jnp.where(no_prev, -jnp.inf, s)
            m = jnp.max(s, axis=1, keepdims=True)
            pv = jnp.dot(jnp.exp(s - m).astype(BF16), vext, preferred_element_type=F32)
            den = pv[:, LANES:]
            o_h = pv[:, :LANES] / den
            l_h = m + jnp.log(den)
            o_pair = o_h if hh == 0 else jnp.where(lo_half, o_pair, o_h)
            l_pair = l_h if hh == 0 else jnp.where(lo_half, l_pair, l_h)
        o_ref[0, 0, rows, lanes_p] = o_pair.astype(BF16)
        lse_ref[0, 0, rows, lanes_p] = l_pair


def _stage_attn(aq, ak, av, batch, seq, group):
    _, dil = DILATED_PATTERNS[group]
    L = seq // dil
    assert L % N_BACK == 0
    lq = min(2048, L)
    nq = L // lq
    sub = lq // N_BACK
    cur = pl.BlockSpec((1, 1, lq, A_WIDTH), lambda b, r, i: (b, r, i, 0))
    prev = pl.BlockSpec((1, 1, N_BACK, A_WIDTH), lambda b, r, i: (b, r, jnp.maximum(i * sub - 1, 0), 0))
    return pl.pallas_call(
        functools.partial(_attn_body, dil=dil, slopes=tuple(_alibi_slopes()[group]), lq=lq),
        grid=(batch, dil, nq),
        in_specs=[cur, prev, cur, prev, cur],
        out_specs=(cur, cur),
        out_shape=(jax.ShapeDtypeStruct((batch, dil, L, A_WIDTH), BF16),
                   jax.ShapeDtypeStruct((batch, dil, L, A_WIDTH), F32)),
        compiler_params=_params("parallel", "parallel", "parallel"),
        name=f"dilated_attn_d{dil}",
    )(aq, ak, ak, av, av)


PACK_ROWS = D_MODEL // (2 * LANES)
U32 = jnp.uint32
_HIGH_HALF = 0xFFFF0000


def _pack_rows(val, exact=False):
    half = D_MODEL // 2
    lo, hi = val[:, :half], val[:, half:]
    if exact:
        return (lax.bitcast_convert_type(lo, U32) >> 16) | lax.bitcast_convert_type(hi, U32)

    def bits(v):
        return lax.bitcast_convert_type(v.astype(BF16).astype(F32), U32)

    return (bits(lo) >> 16) | (bits(hi) & U32(_HIGH_HALF))


def _unpack_rows(words):
    lo = lax.bitcast_convert_type(words << 16, F32).astype(BF16)
    hi = lax.bitcast_convert_type(words & U32(_HIGH_HALF), F32).astype(BF16)
    return jnp.concatenate([lo, hi], axis=1)


def _flat(ref):
    rows = 1
    for d in ref.shape[:-2]:
        rows *= d
    return ref.reshape(rows * PACK_ROWS, LANES)


def _store_packed(flat_ref, row0, words):
    t = words.shape[0]
    for s in range(PACK_ROWS):
        flat_ref[pl.ds(row0 * PACK_ROWS + s, t, stride=PACK_ROWS), :] = words[:, s * LANES:(s + 1) * LANES]


def _load_packed(flat_ref, row0, t):
    return jnp.concatenate([flat_ref[pl.ds(row0 * PACK_ROWS + s, t, stride=PACK_ROWS), :] for s in range(PACK_ROWS)],
                           axis=1)


def _rows8(vals):
    t = vals[0].shape[1]
    rid = lax.broadcasted_iota(I32, (8, t), 0)
    out = jnp.zeros((8, t), vals[0].dtype)
    for k, v in enumerate(vals):
        out = jnp.where(rid == k, jnp.broadcast_to(v, (8, t)), out)
    return out


def _merge_body(hm_ref, o1_ref, o2_ref, o3_ref, l1_ref, l2_ref, l3_ref, sgm_ref, sga_ref, x_ref,
                wm_ref, wa_ref, wo_ref, g2_ref, wrh_ref, br_ref,
                x2_ref, xn_ref, loc_ref, gate_ref, tcnt_ref, tcar_ref, cnt_ref, carry_ref, st_ref, logits_ref):
    step = pl.program_id(0)

    @pl.when(step == 0)
    def _():
        carry_ref[...] = jnp.zeros_like(carry_ref)
        logits_ref[...] = jnp.zeros_like(logits_ref)

    m_branch = jnp.dot(hm_ref[...], wm_ref[...], preferred_element_type=F32)
    dils = [d for _, d in DILATED_PATTERNS]
    l1, l2, l3 = [_merge_residues(r, d, st_ref) for r, d in zip((l1_ref, l2_ref, l3_ref), dils)]
    lmax = jnp.maximum(jnp.maximum(l1, l2), l3)
    e1, e2, e3 = jnp.exp(l1 - lmax), jnp.exp(l2 - lmax), jnp.exp(l3 - lmax)
    num = e1 * _merge_residues(o1_ref, dils[0], st_ref)
    num = num + e2 * _merge_residues(o2_ref, dils[1], st_ref)
    num = num + e3 * _merge_residues(o3_ref, dils[2], st_ref)
    h_a = num / (e1 + e2 + e3)
    y = (sgm_ref[...].astype(F32) * m_branch
         + sga_ref[...].astype(F32) * jnp.dot(h_a.astype(BF16), wa_ref[...], preferred_element_type=F32))
    x2 = x_ref[...] + jnp.dot(y.astype(BF16), wo_ref[...], preferred_element_type=F32)
    x2_ref[...] = x2
    xn = x2 * lax.rsqrt(jnp.mean(x2 * x2, axis=-1, keepdims=True) + EPS) * g2_ref[...]
    xh = xn.astype(BF16)
    xn_ref[...] = xh

    logits = logits_ref[...]
    t = logits.shape[1]
    eid = lax.broadcasted_iota(I32, (N_EXPERTS, t), 0).astype(F32)
    vals = logits
    top_v, top_i = [], []
    for _ in range(TOP_K):
        mx = jnp.max(vals, axis=0, keepdims=True)
        ik = jnp.min(jnp.where(vals == mx, eid, float(N_EXPERTS)), axis=0, keepdims=True)
        top_v.append(mx)
        top_i.append(ik)
        vals = jnp.where(eid == ik, -jnp.inf, vals)
    ex = [jnp.exp(v - top_v[0]) for v in top_v]
    den = ex[0] + ex[1] + ex[2] + ex[3]
    gate_ref[...] = _rows8([e / den for e in ex])

    chosen = jnp.zeros((N_EXPERTS, t), F32)
    for ik in top_i:
        chosen = chosen + (eid == ik).astype(F32)
    before = jnp.where(lax.broadcasted_iota(jnp.int16, (t, t), 0) < lax.broadcasted_iota(jnp.int16, (t, t), 1),
                       jnp.ones((), BF16), jnp.zeros((), BF16))
    prefix = jnp.dot(chosen.astype(BF16), before, preferred_element_type=F32)
    tcount = jnp.broadcast_to(jnp.sum(chosen, axis=1, keepdims=True), (N_EXPERTS, LANES))
    tcount = jnp.where(step > 0, tcount, 0.0)
    below = (lax.broadcasted_iota(I32, (N_EXPERTS, N_EXPERTS), 1)
             < lax.broadcasted_iota(I32, (N_EXPERTS, N_EXPERTS), 0)).astype(BF16)
    t_hi = jnp.floor(tcount * (1.0 / BF16_EXACT_INT)) * BF16_EXACT_INT
    tile_off = (jnp.dot(below, t_hi.astype(BF16), preferred_element_type=F32)
                + jnp.dot(below, (tcount - t_hi).astype(BF16), preferred_element_type=F32))
    pos = prefix + tile_off[:, 0:1]
    loc_ref[...] = _rows8([jnp.sum(jnp.where(eid == ik, pos, 0.0), axis=0, keepdims=True).astype(I32)
                           for ik in top_i])
    carry = carry_ref[...]
    tcnt_ref[...] = tcount.astype(I32)
    tcar_ref[...] = carry.astype(I32)
    total = carry + tcount
    carry_ref[...] = total
    cnt_ref[...] = total

    logits_ref[...] = lax.dot_general(wrh_ref[...], xh, _NT, preferred_element_type=F32) + br_ref[...]


def _stage_merge(h_m, attn, sgm, sga, x2d, w_mb, w_ab, w_out, norm2_g, w_router, b_router, batch, seq, tm):
    n = x2d.shape[0]
    steps = seq // tm
    (o1, l1), (o2, l2), (o3, l3) = attn
    wm = w_mb.astype(BF16)
    wa = w_ab.astype(BF16)
    wo = w_out.astype(BF16)
    g2 = norm2_g.astype(F32).reshape(1, D_MODEL)
    wrh = w_router.astype(BF16).T
    br = b_router.astype(F32).reshape(N_EXPERTS, 1)
    mixed = lambda i: jnp.minimum(i, n // tm - 1)
    routed = lambda i: jnp.maximum(i - 1, 0)
    row = lambda w: pl.BlockSpec((tm, w), lambda i: (mixed(i), 0))
    rowT = lambda r: pl.BlockSpec((r, tm), lambda i: (0, routed(i)))
    full = lambda a: pl.BlockSpec(a.shape, lambda i: (0,) * a.ndim)
    res = lambda d: pl.BlockSpec((1, d, tm // d, A_WIDTH), lambda i: (mixed(i) // steps, 0, mixed(i) % steps, 0))
    dils = [d for _, d in DILATED_PATTERNS]
    per_tile = pl.BlockSpec((N_EXPERTS, LANES), lambda i: (0, routed(i)))
    return pl.pallas_call(
        _merge_body,
        grid=(n // tm + 1,),
        in_specs=[row(M_WIDTH), *[res(d) for d in dils], *[res(d) for d in dils],
                  row(D_MODEL), row(D_MODEL), row(D_MODEL),
                  full(wm), full(wa), full(wo), full(g2), full(wrh), full(br)],
        out_specs=(row(D_MODEL), row(D_MODEL), rowT(8), rowT(8), per_tile, per_tile,
                   pl.BlockSpec((N_EXPERTS, LANES), lambda i: (0, 0))),
        out_shape=(jax.ShapeDtypeStruct((n, D_MODEL), F32),
                   jax.ShapeDtypeStruct((n, D_MODEL), BF16),
                   jax.ShapeDtypeStruct((8, n), I32),
                   jax.ShapeDtypeStruct((8, n), F32),
                   jax.ShapeDtypeStruct((N_EXPERTS, (n // tm) * LANES), I32),
                   jax.ShapeDtypeStruct((N_EXPERTS, (n // tm) * LANES), I32),
                   jax.ShapeDtypeStruct((N_EXPERTS, LANES), F32)),
        scratch_shapes=[pltpu.VMEM((N_EXPERTS, LANES), F32), pltpu.VMEM((2, 2, tm, LANES), F32),
                        pltpu.VMEM((N_EXPERTS, tm), F32)],
        compiler_params=_params("arbitrary"),
        name="merge_route",
    )(h_m, o1, o2, o3, l1, l2, l3, sgm, sga, x2d, wm, wa, wo, g2, wrh, br)


def _offsets_body(cnt_ref, blk_ref, pstart_ref, zlo_ref, zhi_ref, *, nblk_pad):
    cnt = cnt_ref[...]
    padded = jnp.floor((cnt + (MOE_BLOCK - 1)) * (1.0 / MOE_BLOCK)) * MOE_BLOCK
    lower = (lax.broadcasted_iota(I32, (N_EXPERTS, N_EXPERTS), 1)
             <= lax.broadcasted_iota(I32, (N_EXPERTS, N_EXPERTS), 0)).astype(BF16)
    nb = padded * (1.0 / MOE_BLOCK)
    nb_hi = jnp.floor(nb * (1.0 / BF16_EXACT_INT)) * BF16_EXACT_INT
    pends = (jnp.dot(lower, nb_hi.astype(BF16), preferred_element_type=F32)
             + jnp.dot(lower, (nb - nb_hi).astype(BF16), preferred_element_type=F32)) * MOE_BLOCK
    pstart = pends - padded
    pstart_ref[...] = pstart.astype(I32)
    zlo_ref[...] = (pstart + cnt).astype(I32)
    zhi_ref[...] = pends.astype(I32)

    first_row = (lax.broadcasted_iota(I32, (N_EXPERTS, nblk_pad), 1) * MOE_BLOCK).astype(F32)
    pe = jnp.broadcast_to(pends[:, 0:1], (N_EXPERTS, nblk_pad))
    be = jnp.sum((pe <= first_row).astype(F32), axis=0, keepdims=True)
    be = jnp.minimum(be, float(N_EXPERTS - 1))
    nused = pends[N_EXPERTS - 1:N_EXPERTS, 0:1] * (1.0 / MOE_BLOCK)
    nonempty = jnp.broadcast_to(padded[:, 0:1], (N_EXPERTS, nblk_pad)) > 0.0
    runidx = jnp.sum(jnp.logical_and(pe <= first_row, nonempty).astype(F32), axis=0, keepdims=True)
    parity = runidx - 2.0 * jnp.floor(runidx * 0.5)
    eid = lax.broadcasted_iota(I32, (N_EXPERTS, nblk_pad), 0).astype(F32)
    later = jnp.logical_and(eid > be, nonempty)
    nxt = jnp.min(jnp.where(later, eid, float(N_EXPERTS)), axis=0, keepdims=True)
    blk_ref[...] = _rows8([be.astype(I32), jnp.broadcast_to(nused, (1, nblk_pad)).astype(I32),
                           parity.astype(I32), nxt.astype(I32)])


def _stage_offsets(cnt, nblk):
    nblk_pad = -(-nblk // LANES) * LANES
    const = lambda r, c: pl.BlockSpec((r, c), lambda i: (0, 0))
    per_expert = jax.ShapeDtypeStruct((N_EXPERTS, LANES), I32)
    return pl.pallas_call(
        functools.partial(_offsets_body, nblk_pad=nblk_pad),
        grid=(1,),
        in_specs=[const(N_EXPERTS, LANES)],
        out_specs=(const(8, nblk_pad), const(N_EXPERTS, LANES), const(N_EXPERTS, LANES), const(N_EXPERTS, LANES)),
        out_shape=(jax.ShapeDtypeStruct((8, nblk_pad), I32), per_expert, per_expert, per_expert),
        compiler_params=_params("arbitrary"),
        name="route_offsets",
    )(cnt)


RUN_BITS = 10


def _tile_rows(ref, first_row, nrows):
    return ref.at[pl.ds(first_row, nrows)]


def _for_each_piece(length, fn):
    for b in reversed(range(RUN_BITS)):
        @pl.when(((length >> b) & 1) == 1)
        def _(b=b):
            fn((length >> (b + 1)) << (b + 1), 1 << b)


def _for_each_run(tile, tcnt_ref, tcar_ref, pstart_ref, fn):
    def per_expert(e, local):
        count = tcnt_ref[tile, e]
        first = pstart_ref[e] + tcar_ref[tile, e]
        _for_each_piece(count, lambda off, size: fn(local + off, first + off, size))
        return local + count

    lax.fori_loop(0, N_EXPERTS, per_expert, 0)


PERM_CHUNK = 256


def _dispatch_body(tcnt_ref, tcar_ref, pstart_ref, zlo_ref, zhi_ref, loc_ref, xn_ref, xs_hbm,
                   buf_ref, sems, *, tm):
    step = pl.program_id(0)
    nloc = TOP_K * tm

    def wait_buffer(slot):
        pltpu.make_async_copy(buf_ref.at[slot], xs_hbm.at[pl.ds(0, nloc)], sems.at[slot]).wait()

    for slot in range(2):
        tile = 2 * step + slot

        @pl.when(step > 0)
        def _(slot=slot):
            wait_buffer(slot)

        loc = loc_ref[:, slot * tm:(slot + 1) * tm].astype(jnp.int16)
        xn = xn_ref[slot * tm:(slot + 1) * tm, :]
        for c in range(TOP_K * tm // PERM_CHUNK):
            lid = lax.broadcasted_iota(jnp.int16, (PERM_CHUNK, tm), 0) + jnp.int16(c * PERM_CHUNK)
            hit = lid == loc[0:1, :]
            for k in range(1, TOP_K):
                hit = jnp.logical_or(hit, lid == loc[k:k + 1, :])
            perm = jnp.where(hit, jnp.ones((), BF16), jnp.zeros((), BF16))
            rows = jnp.dot(perm, xn, preferred_element_type=F32)
            _store_packed(_flat(buf_ref), slot * nloc + c * PERM_CHUNK, _pack_rows(rows, exact=True))

        def run_copy(local, first, size, slot=slot):
            return pltpu.make_async_copy(_tile_rows(buf_ref.at[slot], local, size),
                                         _tile_rows(xs_hbm, first, size), sems.at[slot])

        _for_each_run(tile, tcnt_ref, tcar_ref, pstart_ref, lambda l, f, s: run_copy(l, f, s).start())

    @pl.when(step == pl.num_programs(0) - 1)
    def _():
        wait_buffer(0)
        wait_buffer(1)
        zsrc = buf_ref.at[0]
        zsrc[pl.ds(0, MOE_BLOCK)] = jnp.zeros((MOE_BLOCK, PACK_ROWS, LANES), U32)

        def zero_copy(first, size):
            return pltpu.make_async_copy(_tile_rows(zsrc, 0, size), _tile_rows(xs_hbm, first, size), sems.at[0])

        first_unused = zhi_ref[N_EXPERTS - 1] // MOE_BLOCK
        nblk = xs_hbm.shape[0] // MOE_BLOCK

        def for_each_zero_copy(action):
            def per_expert(e, carry):
                lo = zlo_ref[e]
                _for_each_piece(zhi_ref[e] - lo, lambda off, size: action(zero_copy(lo + off, size)))
                return carry

            def tail(blk, carry):
                action(zero_copy(blk * MOE_BLOCK, MOE_BLOCK))
                return carry

            lax.fori_loop(0, N_EXPERTS, per_expert, 0)
            lax.fori_loop(first_unused, nblk, tail, 0)

        for_each_zero_copy(lambda copy: copy.start())
        for_each_zero_copy(lambda copy: copy.wait())


def _stage_dispatch(tables, loc8, xn, nrows, tm):
    n = xn.shape[0]
    assert TOP_K * tm >= MOE_BLOCK and (n // tm) % 2 == 0
    grid_spec = pltpu.PrefetchScalarGridSpec(
        num_scalar_prefetch=5,
        grid=(n // (2 * tm),),
        in_specs=[pl.BlockSpec((8, 2 * tm), lambda i, *_: (0, i)),
                  pl.BlockSpec((2 * tm, D_MODEL), lambda i, *_: (i, 0))],
        out_specs=pl.BlockSpec(memory_space=pl.ANY),
        scratch_shapes=[pltpu.VMEM((2, TOP_K * tm, PACK_ROWS, LANES), U32), pltpu.SemaphoreType.DMA((2,))],
    )
    return pl.pallas_call(
        functools.partial(_dispatch_body, tm=tm),
        grid_spec=grid_spec,
        out_shape=jax.ShapeDtypeStruct((nrows, PACK_ROWS, LANES), U32),
        compiler_params=_params("arbitrary"),
        name="dispatch",
    )(*tables, loc8, xn)


EXPERT_BLOCKS_PER_STEP = 2


def _expert_body(be_ref, nu_ref, par_ref, nxt_ref, xs_ref, w1_hbm, b1_ref, w2_hbm, b2_ref, ys_ref,
                 w1f_ref, w2f_ref, w1b_ref, w2b_ref, sems):
    def fetch(expert, slot):
        return (pltpu.make_async_copy(w1_hbm.at[expert], w1f_ref.at[slot], sems.at[slot, 0]),
                pltpu.make_async_copy(w2_hbm.at[expert], w2f_ref.at[slot], sems.at[slot, 1]))

    def block_info(sub):
        j = pl.program_id(0) * EXPERT_BLOCKS_PER_STEP + sub
        used = j < nu_ref[0]
        jj = jnp.maximum(jnp.minimum(j, nu_ref[0] - 1), 0)
        e = be_ref[jj]
        fresh = jnp.logical_or(j == 0, e != be_ref[jnp.maximum(jj - 1, 0)])
        return j, jj, e, used, fresh

    def refresh(j, jj, e):
        slot = par_ref[jj]

        @pl.when(j == 0)
        def _():
            for c in fetch(e, slot):
                c.start()

        for c in fetch(e, slot):
            c.wait()
        nxt = nxt_ref[jj]

        @pl.when(nxt < N_EXPERTS)
        def _():
            for c in fetch(nxt, 1 - slot):
                c.start()

        w1b_ref[...] = w1f_ref[slot].astype(BF16)
        w2b_ref[...] = w2f_ref[slot].astype(BF16)

    def compute(e, row0, nrows):
        xb = _unpack_rows(_load_packed(_flat(xs_ref), row0, nrows))
        gu = jnp.dot(xb, w1b_ref[...], preferred_element_type=F32) + b1_ref[pl.ds(e, 1), :]
        gate = jnp.minimum(gu[:, :D_FF], SWIGLU_LIMIT)
        lin = jnp.clip(gu[:, D_FF:], -SWIGLU_LIMIT, SWIGLU_LIMIT)
        act = (lin + 1.0) * (gate * jax.nn.sigmoid(SWIGLU_ALPHA * gate))
        ys = jnp.dot(act.astype(BF16), w2b_ref[...], preferred_element_type=F32) + b2_ref[pl.ds(e, 1), :]
        _store_packed(_flat(ys_ref), row0, _pack_rows(ys))

    j0, jj0, e0, used0, fresh0 = block_info(0)
    j1, jj1, e1, used1, fresh1 = block_info(1)
    same = jnp.logical_and(used1, e1 == e0)

    @pl.when(jnp.logical_and(used0, fresh0))
    def _():
        refresh(j0, jj0, e0)

    @pl.when(jnp.logical_and(used0, same))
    def _():
        compute(e0, 0, 2 * MOE_BLOCK)

    @pl.when(jnp.logical_and(used0, jnp.logical_not(same)))
    def _():
        compute(e0, 0, MOE_BLOCK)

    @pl.when(jnp.logical_and(used1, fresh1))
    def _():
        refresh(j1, jj1, e1)

    @pl.when(jnp.logical_and(used1, jnp.logical_not(same)))
    def _():
        compute(e1, MOE_BLOCK, MOE_BLOCK)

    for sub, used in ((0, used0), (1, used1)):
        @pl.when(jnp.logical_not(used))
        def _(sub=sub):
            ys_ref[pl.ds(sub * MOE_BLOCK, MOE_BLOCK)] = jnp.zeros((MOE_BLOCK, PACK_ROWS, LANES), U32)


def _stage_experts(blk8, xs, w1, b1, w2, b2):
    nrows = xs.shape[0]
    nblk = nrows // MOE_BLOCK
    assert EXPERT_BLOCKS_PER_STEP == 2 and nblk % EXPERT_BLOCKS_PER_STEP == 0
    block_e, nused, parity, nxt = blk8[0, :nblk], blk8[1, :1], blk8[2, :nblk], blk8[3, :nblk]
    tiles = (EXPERT_BLOCKS_PER_STEP * MOE_BLOCK, PACK_ROWS, LANES)
    full = lambda a: pl.BlockSpec(a.shape, lambda j, *_: (0,) * a.ndim)
    grid_spec = pltpu.PrefetchScalarGridSpec(
        num_scalar_prefetch=4,
        grid=(nblk // EXPERT_BLOCKS_PER_STEP,),
        in_specs=[pl.BlockSpec(tiles, lambda j, *_: (j, 0, 0)),
                  pl.BlockSpec(memory_space=pl.ANY), full(b1),
                  pl.BlockSpec(memory_space=pl.ANY), full(b2)],
        out_specs=pl.BlockSpec(tiles, lambda j, *_: (j, 0, 0)),
        scratch_shapes=[pltpu.VMEM((2, D_MODEL, 2 * D_FF), F32), pltpu.VMEM((2, D_FF, D_MODEL), F32),
                        pltpu.VMEM((D_MODEL, 2 * D_FF), BF16), pltpu.VMEM((D_FF, D_MODEL), BF16),
                        pltpu.SemaphoreType.DMA((2, 2))],
    )
    return pl.pallas_call(
        _expert_body,
        grid_spec=grid_spec,
        out_shape=jax.ShapeDtypeStruct((nrows, PACK_ROWS, LANES), U32),
        compiler_params=_params("arbitrary"),
        name="experts",
    )(block_e, nused, parity, nxt, xs, w1, b1, w2, b2)


COMBINE_CHUNK = 512


def _combine_body(tcnt_ref, tcar_ref, pstart_ref, loc_ref, gate_ref, x2_ref, ys_hbm, out_ref,
                  buf_ref, g_ref, sems, *, tm):
    step = pl.program_id(0)
    nloc = TOP_K * tm

    def start_runs(tile, slot):
        def run_copy(local, first, size):
            return pltpu.make_async_copy(_tile_rows(ys_hbm, first, size),
                                         _tile_rows(buf_ref.at[slot], local, size), sems.at[slot])
        _for_each_run(tile, tcnt_ref, tcar_ref, pstart_ref, lambda l, f, s: run_copy(l, f, s).start())

    def wait_buffer(slot):
        pltpu.make_async_copy(ys_hbm.at[pl.ds(0, nloc)], buf_ref.at[slot], sems.at[slot]).wait()

    def combine(slot):
        zpad = jnp.zeros((LANES - 16, LANES), F32)
        cols = []
        for c in range(tm // LANES):
            cols_in = slice(slot * tm + c * LANES, slot * tm + (c + 1) * LANES)
            cols.append(jnp.transpose(jnp.concatenate([loc_ref[:, cols_in].astype(F32), gate_ref[:, cols_in], zpad],
                                                      axis=0)))
        rows = slice(slot * tm, (slot + 1) * tm)
        acc = x2_ref[rows, :]
        for lc in range(nloc // COMBINE_CHUNK):
            lane = lax.broadcasted_iota(jnp.int16, (LANES, COMBINE_CHUNK), 1) + jnp.int16(lc * COMBINE_CHUNK)
            for c in range(tm // LANES):
                g = jnp.zeros((LANES, COMBINE_CHUNK), BF16)
                for k in range(TOP_K):
                    g = jnp.where(lane == cols[c][:, k:k + 1].astype(jnp.int16), cols[c][:, 8 + k:9 + k].astype(BF16), g)
                g_ref[c * LANES:(c + 1) * LANES, lc * COMBINE_CHUNK:(lc + 1) * COMBINE_CHUNK] = g
            if lc == 0:
                wait_buffer(slot)
            ys = _unpack_rows(_load_packed(_flat(buf_ref), slot * nloc + lc * COMBINE_CHUNK, COMBINE_CHUNK))
            acc = acc + jnp.dot(g_ref[:, lc * COMBINE_CHUNK:(lc + 1) * COMBINE_CHUNK], ys, preferred_element_type=F32)
        out_ref[rows, :] = acc

    @pl.when(step == 0)
    def _():
        start_runs(0, 0)

    start_runs(2 * step + 1, 1)
    combine(0)

    @pl.when(step + 1 < pl.num_programs(0))
    def _():
        start_runs(2 * step + 2, 0)

    combine(1)


def _stage_combine(tables, loc8, gate8, x2, ys, tm):
    n = x2.shape[0]
    assert (n // tm) % 2 == 0
    grid_spec = pltpu.PrefetchScalarGridSpec(
        num_scalar_prefetch=3,
        grid=(n // (2 * tm),),
        in_specs=[pl.BlockSpec((8, 2 * tm), lambda i, *_: (0, i)),
                  pl.BlockSpec((8, 2 * tm), lambda i, *_: (0, i)),
                  pl.BlockSpec((2 * tm, D_MODEL), lambda i, *_: (i, 0)),
                  pl.BlockSpec(memory_space=pl.ANY)],
        out_specs=pl.BlockSpec((2 * tm, D_MODEL), lambda i, *_: (i, 0)),
        scratch_shapes=[pltpu.VMEM((2, TOP_K * tm, PACK_ROWS, LANES), U32),
                        pltpu.VMEM((tm, TOP_K * tm), BF16),
                        pltpu.SemaphoreType.DMA((2,))],
    )
    return pl.pallas_call(
        functools.partial(_combine_body, tm=tm),
        grid_spec=grid_spec,
        out_shape=jax.ShapeDtypeStruct((n, D_MODEL), F32),
        compiler_params=_params("arbitrary"),
        name="combine",
    )(*tables, loc8, gate8, x2, ys)


def _moe(x2, xn, loc8, gate8, tcnt, tcar, cnt, w1, b1, w2, b2, tm):
    n = x2.shape[0]
    ntile = n // tm
    nblk = -(-(n * TOP_K) // MOE_BLOCK) + N_EXPERTS
    blk8, pstart, zlo, zhi = _stage_offsets(cnt, nblk)
    per_tile = lambda a: a.reshape(N_EXPERTS, ntile, LANES)[:, :, 0].T
    tables = (per_tile(tcnt), per_tile(tcar), pstart[:, 0])
    xs = _stage_dispatch(tables + (zlo[:, 0], zhi[:, 0]), loc8, xn, nblk * MOE_BLOCK, tm)
    ys = _stage_experts(blk8, xs, w1, b1, w2, b2)
    return _stage_combine(tables, loc8, gate8, x2, ys, tm)


def kernel(x, norm1_g, w_in, mlstm_gate_b, mlstm_norm_g, attn_q_norm_g, attn_k_norm_g, w_mlstm_branch,
           w_attn_branch, w_out, norm2_g, w_router, b_router, w1, b1, w2, b2):
    batch, seq, _ = x.shape
    n = batch * seq
    for l in range(norm1_g.shape[0]):
        x2d = x.reshape(n, D_MODEL)
        tm = min(512, seq)
        mq, kT, mv, so, gi, gf, aq, ak, av, sgm, sga = _stage_inproj(
            x2d, norm1_g[l], w_in[l], mlstm_gate_b[l], attn_q_norm_g[l], attn_k_norm_g[l], batch, seq, tm)
        h_m = _stage_mlstm(mq, kT, mv, so, gi, gf, mlstm_norm_g[l], batch, seq, tm)
        attn = [_stage_attn(aq[g], ak[g], av[g], batch, seq, g)
                for g in range(N_GROUPS)]
        x2, xn, loc8, gate8, tcnt, tcar, cnt = _stage_merge(
            h_m, attn, sgm, sga, x2d, w_mlstm_branch[l], w_attn_branch[l], w_out[l], norm2_g[l],
            w_router[l], b_router[l], batch, seq, tm)
        out = _moe(x2, xn, loc8, gate8, tcnt, tcar, cnt, w1[l], b1[l], w2[l], b2[l], tm)
        x = out.reshape(batch, seq, D_MODEL)
    return x
```

```python
import functools

import numpy as np
import jax
import jax.numpy as jnp
from jax import lax
from jax.experimental import pallas as pl
from jax.experimental.pallas import tpu as pltpu

F32 = jnp.float32
BF16 = jnp.bfloat16
I32 = jnp.int32

D_MODEL = 1024
M_HEADS = 4
M_QK_DIM = 64
M_V_DIM = 128
GATE_SOFTCAP = 15.0
A_HEADS = 4
A_HEAD_DIM = 64
DILATED_PATTERNS = ((128, 1), (512, 4), (2048, 16))
N_GROUPS = len(DILATED_PATTERNS)
N_BACK = 128
N_EXPERTS = 32
TOP_K = 4
D_FF = 1024
SWIGLU_LIMIT = 7.0
SWIGLU_ALPHA = 1.702
MOE_BLOCK = 512
EPS = 1e-6

M_WIDTH = M_HEADS * M_V_DIM
M_QK_WIDTH = M_HEADS * M_QK_DIM
A_WIDTH = A_HEADS * A_HEAD_DIM
IN_SPLITS = (M_QK_WIDTH, M_QK_WIDTH, M_WIDTH, M_WIDTH, 2 * M_HEADS,
             N_GROUPS * A_WIDTH, N_GROUPS * A_WIDTH, N_GROUPS * A_WIDTH, D_MODEL, D_MODEL)

LANES = 128
VMEM_LIMIT = 56 * 1024 * 1024

BF16_EXACT_INT = 256.0

_NT = (((1,), (1,)), ((), ()))
_TN = (((0,), (1,)), ((), ()))


def _alibi_slopes():
    n = N_GROUPS * A_HEADS
    s = np.exp2(-8.0 * np.arange(1, n + 1) / n).astype(np.float32)
    return s.reshape(N_GROUPS, A_HEADS)


def _params(*sem):
    return pltpu.CompilerParams(dimension_semantics=sem, vmem_limit_bytes=VMEM_LIMIT)


def _log_sigmoid(x):
    return jnp.minimum(x, 0.0) - jnp.log1p(jnp.exp(-jnp.abs(x)))


_A_WIDTH = sum(IN_SPLITS[:4])
_B_START = _A_WIDTH + IN_SPLITS[4]


def _piece_segments(widths):
    bounds, start = [], 0
    for width in widths:
        bounds.append((start, start + width))
        start += width
    return bounds


_C_MQ, _C_MK, _C_MV, _C_MO = _piece_segments(IN_SPLITS[:4])
_C_AQ, _C_AK, _C_AV, _C_GM, _C_GA = _piece_segments(IN_SPLITS[5:])


def _split_residues(val, d, out_ref, st_ref):
    t = val.shape[0]
    if d == 1:
        out_ref[0, 0] = val.astype(out_ref.dtype)
        return
    a_ref = st_ref.at[0]
    a_ref[0] = val[:, :LANES]
    a_ref[1] = val[:, LANES:]
    m = t // d
    if d == 16:
        b_ref = st_ref.at[1]
        for r0 in range(4):
            b_ref[0, r0 * 4 * m:(r0 + 1) * 4 * m, :] = a_ref[0, pl.ds(r0, 4 * m, stride=4), :]
            b_ref[1, r0 * 4 * m:(r0 + 1) * 4 * m, :] = a_ref[1, pl.ds(r0, 4 * m, stride=4), :]
        for r0 in range(4):
            for r1 in range(4):
                piece = jnp.concatenate([b_ref[0, pl.ds(r0 * 4 * m + r1, m, stride=4), :],
                                         b_ref[1, pl.ds(r0 * 4 * m + r1, m, stride=4), :]], axis=1)
                out_ref[0, 4 * r1 + r0] = piece.astype(out_ref.dtype)
        return
    for r in range(d):
        piece = jnp.concatenate([a_ref[0, pl.ds(r, m, stride=d), :], a_ref[1, pl.ds(r, m, stride=d), :]], axis=1)
        out_ref[0, r] = piece.astype(out_ref.dtype)


def _merge_residues(ref, d, st_ref):
    if d == 1:
        return ref[0, 0].astype(F32)
    m = ref.shape[2]
    halves = range(ref.shape[3] // LANES)
    if d == 16:
        a_ref, b_ref = st_ref.at[0], st_ref.at[1]
        for r0 in range(4):
            for r1 in range(4):
                blk = ref[0, 4 * r1 + r0].astype(F32)
                for c in halves:
                    a_ref[c, pl.ds(r0 * 4 * m + r1, m, stride=4), :] = blk[:, c * LANES:(c + 1) * LANES]
        for r0 in range(4):
            for c in halves:
                b_ref[c, pl.ds(r0, 4 * m, stride=4), :] = a_ref[c, r0 * 4 * m:(r0 + 1) * 4 * m, :]
        return jnp.concatenate([b_ref[c] for c in halves], axis=1)
    a_ref = st_ref.at[0]
    for r in range(d):
        blk = ref[0, r].astype(F32)
        for c in halves:
            a_ref[c, pl.ds(r, m, stride=d), :] = blk[:, c * LANES:(c + 1) * LANES]
    return jnp.concatenate([a_ref[c] for c in halves], axis=1)


def _inproj_body(x_ref, g1_ref, wa_ref, wg_ref, wb_ref, gb_ref, gq_ref, gk_ref,
                 mq_ref, kT_ref, mv_ref, so_ref, gi_ref, gf_ref,
                 q0_ref, q1_ref, q2_ref, k0_ref, k1_ref, k2_ref, v0_ref, v1_ref, v2_ref,
                 sgm_ref, sga_ref, st_ref):
    x = x_ref[...]
    h = x * lax.rsqrt(jnp.mean(x * x, axis=-1, keepdims=True) + EPS) * g1_ref[...]
    hb = h.astype(BF16)

    def seg(w_ref, c):
        return jnp.dot(hb, w_ref[:, c[0]:c[1]], preferred_element_type=F32)

    mq_ref[...] = seg(wa_ref, _C_MQ).astype(BF16)
    mv_ref[...] = seg(wa_ref, _C_MV).astype(BF16)
    so_ref[...] = jax.nn.sigmoid(seg(wa_ref, _C_MO)).astype(BF16)
    hid_r = lax.broadcasted_iota(I32, (A_WIDTH, A_WIDTH), 0) // A_HEAD_DIM
    hid_c = lax.broadcasted_iota(I32, (A_WIDTH, A_WIDTH), 1) // A_HEAD_DIM
    head_ones = (hid_r == hid_c).astype(BF16)
    for c, refs, gain_ref in ((_C_AQ, (q0_ref, q1_ref, q2_ref), gq_ref), (_C_AK, (k0_ref, k1_ref, k2_ref), gk_ref),
                              (_C_AV, (v0_ref, v1_ref, v2_ref), None)):
        val = seg(wb_ref, c)
        for g, ref in enumerate(refs):
            piece = val[:, g * A_WIDTH:(g + 1) * A_WIDTH]
            if gain_ref is not None:
                ss = jnp.dot((piece * piece).astype(BF16), head_ones, preferred_element_type=F32)
                piece = piece * lax.rsqrt(ss * (1.0 / A_HEAD_DIM) + EPS) * gain_ref[:, g * A_WIDTH:(g + 1) * A_WIDTH]
            _split_residues(piece, DILATED_PATTERNS[g][1], ref, st_ref)
    sgm_ref[...] = jax.nn.sigmoid(seg(wb_ref, _C_GM)).astype(BF16)
    sga_ref[...] = jax.nn.sigmoid(seg(wb_ref, _C_GA)).astype(BF16)

    kT_ref[...] = lax.dot_general(wa_ref[:, _C_MK[0]:_C_MK[1]], hb, _TN, preferred_element_type=F32).astype(BF16)
    zg = lax.dot_general(wg_ref[...], hb, _TN, preferred_element_type=F32)
    zi = zg[0:8] + gb_ref[0:8]
    zf = zg[M_HEADS:M_HEADS + 8] + gb_ref[8:16]
    gi_ref[...] = GATE_SOFTCAP * jnp.tanh(zi / GATE_SOFTCAP)
    gf_ref[...] = _log_sigmoid(GATE_SOFTCAP * jnp.tanh(zf / GATE_SOFTCAP))


def _stage_inproj(x2d, norm1_g, w_in, gate_b, gq, gk, batch, seq, tm):
    n = x2d.shape[0]
    steps = seq // tm
    wa = w_in[:, :_A_WIDTH].astype(BF16)
    wg = jnp.pad(w_in[:, _A_WIDTH:_B_START], ((0, 0), (0, LANES - IN_SPLITS[4]))).astype(BF16)
    wb = w_in[:, _B_START:].astype(BF16)
    gb = jnp.zeros((16, 1), F32)
    gb = gb.at[0:4, 0].set(gate_b[:M_HEADS].astype(F32)).at[8:12, 0].set(gate_b[M_HEADS:].astype(F32))
    g1 = norm1_g.astype(F32).reshape(1, D_MODEL)
    gq_t = (jnp.tile(gq.astype(F32), (1, A_HEADS)) * (A_HEAD_DIM ** -0.5)).reshape(1, N_GROUPS * A_WIDTH)
    gk_t = jnp.tile(gk.astype(F32), (1, A_HEADS)).reshape(1, N_GROUPS * A_WIDTH)

    row = lambda w: pl.BlockSpec((tm, w), lambda i: (i, 0))
    rowT = lambda r: pl.BlockSpec((r, tm), lambda i: (0, i))
    full = lambda a: pl.BlockSpec(a.shape, lambda i: (0,) * a.ndim)
    dils = [d for _, d in DILATED_PATTERNS]
    res_shape = lambda d: jax.ShapeDtypeStruct((batch, d, seq // d, A_WIDTH), BF16)
    res_spec = lambda d: pl.BlockSpec((1, d, tm // d, A_WIDTH), lambda i: (i // steps, 0, i % steps, 0))
    out_shapes = (
        jax.ShapeDtypeStruct((n, M_QK_WIDTH), BF16),
        jax.ShapeDtypeStruct((M_QK_WIDTH, n), BF16),
        jax.ShapeDtypeStruct((n, M_WIDTH), BF16),
        jax.ShapeDtypeStruct((n, M_WIDTH), BF16),
        jax.ShapeDtypeStruct((8, n), F32),
        jax.ShapeDtypeStruct((8, n), F32),
        *[res_shape(d) for d in dils], *[res_shape(d) for d in dils], *[res_shape(d) for d in dils],
        jax.ShapeDtypeStruct((n, D_MODEL), BF16),
        jax.ShapeDtypeStruct((n, D_MODEL), BF16),
    )
    out_specs = (row(M_QK_WIDTH), rowT(M_QK_WIDTH), row(M_WIDTH), row(M_WIDTH), rowT(8), rowT(8),
                 *[res_spec(d) for d in dils], *[res_spec(d) for d in dils], *[res_spec(d) for d in dils],
                 row(D_MODEL), row(D_MODEL))
    outs = pl.pallas_call(
        _inproj_body,
        grid=(n // tm,),
        in_specs=[row(D_MODEL), full(g1), full(wa), full(wg), full(wb), full(gb), full(gq_t), full(gk_t)],
        out_specs=out_specs,
        out_shape=out_shapes,
        scratch_shapes=[pltpu.VMEM((2, 2, tm, LANES), F32)],
        compiler_params=_params("parallel"),
        name="inproj",
    )(x2d, g1, wa, wg, wb, gb, gq_t, gk_t)
    mq, kT, mv, so, gi, gf = outs[:6]
    aq, ak, av = outs[6:9], outs[9:12], outs[12:15]
    return mq, kT, mv, so, gi, gf, aq, ak, av, outs[15], outs[16]


M_CHUNK_LEN = 128


def _mlstm_body(q_ref, v_ref, so_ref, ng_ref, *rest, nchunk, nseq):
    kT_refs, gi_refs, gf_refs = rest[0:nseq], rest[nseq:2 * nseq], rest[2 * nseq:3 * nseq]
    o_ref, c_ref, m_ref = rest[3 * nseq:]
    L = M_CHUNK_LEN

    @pl.when(pl.program_id(0) == 0)
    def _():
        c_ref[...] = jnp.zeros_like(c_ref)
        m_ref[...] = jnp.zeros_like(m_ref)

    lane8 = lax.broadcasted_iota(I32, (8, L), 1)
    causal = lax.broadcasted_iota(I32, (L, L), 1) <= lax.broadcasted_iota(I32, (L, L), 0)
    lo_half = lax.broadcasted_iota(I32, (L, LANES), 1) < M_QK_DIM
    ones = jnp.ones((L, M_V_DIM), BF16)

    heads = range(M_HEADS)
    cstate = [[c_ref[s, h * M_QK_DIM:(h + 1) * M_QK_DIM, :] for h in heads] for s in range(nseq)]
    m_prev = [m_ref[s, :, 0:1] for s in range(nseq)]
    units = []
    for c in range(nchunk):
        rows = slice(c * L, (c + 1) * L)
        for s in range(nseq):
            gi = gi_refs[s][:, rows]
            b = gf_refs[s][:, rows]
            sh = 1
            while sh < L:
                b = b + jnp.where(lane8 >= sh, pltpu.roll(b, sh, 1), 0.0)
                sh *= 2
            u = gi - b
            g = b[:, L - 1:L]
            a = g + u
            amax = jnp.max(a, axis=1, keepdims=True)
            m_new = jnp.maximum(g + m_prev[s], amax)
            w = jnp.exp(a - m_new) * (M_QK_DIM ** -0.5)
            s_old = jnp.exp(g + m_prev[s] - m_new)
            vext = [jnp.concatenate([v_ref[s, rows, h * M_V_DIM:(h + 1) * M_V_DIM], ones], axis=1) for h in heads]
            cloc = []
            for h in heads:
                hr = slice(h * M_QK_DIM, (h + 1) * M_QK_DIM)
                kw = (kT_refs[s][hr, rows].astype(F32) * w[h:h + 1, :]).astype(BF16)
                cloc.append(jnp.dot(kw, vext[h], preferred_element_type=F32))
            units.append(dict(seq=s, rows=rows, b=b, u=u, m_prev=m_prev[s], state=cstate[s], vext=vext))
            cstate[s] = [s_old[h:h + 1, :] * cstate[s][h] + cloc[h] for h in heads]
            m_prev[s] = m_new
    for s in range(nseq):
        for h in heads:
            c_ref[s, h * M_QK_DIM:(h + 1) * M_QK_DIM, :] = cstate[s][h]
        m_ref[s] = jnp.broadcast_to(m_prev[s], (8, LANES))

    for un in units:
        s, rows = un["seq"], un["rows"]
        un["s"], un["qc"] = [], []
        for p in range(M_HEADS // 2):
            lanes_p = slice(p * LANES, (p + 1) * LANES)
            q_pair = q_ref[s, rows, lanes_p]
            kT_pair = kT_refs[s][lanes_p, rows]
            c_pair = jnp.concatenate([un["state"][2 * p], un["state"][2 * p + 1]], axis=0).astype(BF16)
            for hh in range(2):
                qm = jnp.where(lo_half if hh == 0 else jnp.logical_not(lo_half), q_pair, jnp.zeros_like(q_pair))
                un["s"].append(jnp.dot(qm, kT_pair, preferred_element_type=F32) * (M_QK_DIM ** -0.5))
                un["qc"].append(jnp.dot(qm, c_pair, preferred_element_type=F32))

    for un in units:
        s, rows, b, u, mp = un["seq"], un["rows"], un["b"], un["u"], un["m_prev"]
        for h in heads:
            hl = slice(h * M_V_DIM, (h + 1) * M_V_DIM)
            bcol = jnp.transpose(jnp.broadcast_to(b[h:h + 1, :], (L, L)))
            dm = jnp.where(causal, bcol + u[h:h + 1, :], -jnp.inf)
            inter = bcol + mp[h:h + 1, :]
            m_t = jnp.maximum(inter, jnp.max(dm, axis=1, keepdims=True))
            pmat = (un["s"][h] * jnp.exp(dm - m_t)).astype(BF16)
            sc = jnp.exp(inter - m_t)
            out = (jnp.dot(pmat, un["vext"][h], preferred_element_type=F32)
                   + jnp.concatenate([sc, sc], axis=1) * un["qc"][h])
            hv = out[:, :M_V_DIM] / jnp.maximum(jnp.abs(out[:, M_V_DIM:]), jnp.exp(-m_t))
            hn = hv * lax.rsqrt(jnp.mean(hv * hv, axis=1, keepdims=True) + EPS)
            hn = hn * ng_ref[:, hl] * so_ref[s, rows, hl].astype(F32)
            o_ref[s, rows, hl] = hn.astype(BF16)


def _stage_mlstm(mq, kT, mv, so, gi, gf, norm_g, batch, seq, rows_per_step):
    n = batch * seq
    R = rows_per_step
    steps = seq // R
    ng = norm_g.astype(F32).reshape(1, M_WIDTH)
    per_seq = lambda a: a.reshape(batch, seq, a.shape[1])
    row = lambda w: pl.BlockSpec((batch, R, w), lambda i: (0, i, 0))
    colT = lambda r, s: pl.BlockSpec((r, R), lambda i, s=s: (0, s * steps + i))
    seqs = range(batch)
    out = pl.pallas_call(
        functools.partial(_mlstm_body, nchunk=R // M_CHUNK_LEN, nseq=batch),
        grid=(steps,),
        in_specs=[row(M_QK_WIDTH), row(M_WIDTH), row(M_WIDTH), pl.BlockSpec((1, M_WIDTH), lambda i: (0, 0)),
                  *[colT(M_QK_WIDTH, s) for s in seqs], *[colT(8, s) for s in seqs], *[colT(8, s) for s in seqs]],
        out_specs=row(M_WIDTH),
        out_shape=jax.ShapeDtypeStruct((batch, seq, M_WIDTH), BF16),
        scratch_shapes=[pltpu.VMEM((batch, M_QK_WIDTH, 2 * M_V_DIM), F32), pltpu.VMEM((batch, 8, LANES), F32)],
        compiler_params=_params("arbitrary"),
        name="mlstm",
    )(per_seq(mq), per_seq(mv), per_seq(so), ng, *[kT] * batch, *[gi] * batch, *[gf] * batch)
    return out.reshape(n, M_WIDTH)


LSE_LANES = LANES // A_HEADS


def _attn_body(q_ref, kp_ref, kc_ref, vp_ref, vc_ref, o_ref, lse_ref, *, dil, slopes, lq):
    QB = N_BACK
    first = pl.program_id(2) == 0
    qn = q_ref[0, 0]
    kcn = kc_ref[0, 0]
    kpn = kp_ref[0, 0]
    vc = vc_ref[0, 0]
    vp = vp_ref[0, 0]

    qi = lax.broadcasted_iota(I32, (QB, 2 * QB), 0)
    kj = lax.broadcasted_iota(I32, (QB, 2 * QB), 1)
    dist = qi + QB - kj
    band = jnp.logical_and(dist >= 0, dist <= N_BACK)
    distf = (dist * dil).astype(F32)
    bias = [jnp.where(band, -float(slopes[h]) * distf, -jnp.inf) for h in range(A_HEADS)]
    no_prev = jnp.logical_and(first, kj < QB)
    lo_half = lax.broadcasted_iota(I32, (QB, LANES), 1) < A_HEAD_DIM
    ones = jnp.ones((2 * QB, LANES), BF16)

    units = []
    for j in range(lq // QB):
        rows = slice(j * QB, (j + 1) * QB)
        prow = slice((j - 1) * QB, j * QB)
        keys = jnp.concatenate([kpn if j == 0 else kcn[prow], kcn[rows]], axis=0)
        vals = jnp.concatenate([vp if j == 0 else vc[prow], vc[rows]], axis=0)
        for p in range(A_HEADS // 2):
            lanes_p = slice(p * LANES, (p + 1) * LANES)
            q_pair = qn[rows, lanes_p]
            k_pair = keys[:, lanes_p]
            vext = jnp.concatenate([vals[:, lanes_p], ones], axis=1)
            scores = []
            for hh in range(2):
                sel = lo_half if hh == 0 else jnp.logical_not(lo_half)
                qm = jnp.where(sel, q_pair, jnp.zeros_like(q_pair))
                scores.append(lax.dot_general(qm, k_pair, _NT, preferred_element_type=F32))
            units.append((j, rows, lanes_p, p, vext, scores))

    lane = lax.broadcasted_iota(I32, (QB, LANES), 1)
    l_first = None
    for j, rows, lanes_p, p, vext, scores in units:
        o_pair = None
        l_pair = None
        for hh in range(2):
            s = scores[hh] + bias[2 * p + hh]
            if j == 0:
                s = jnp.where(no_prev, -jnp.inf, s)
            m = jnp.max(s, axis=1, keepdims=True)
            pv = jnp.dot(jnp.exp(s - m).astype(BF16), vext, preferred_element_type=F32)
            den = pv[:, LANES:]
            o_h = pv[:, :LANES] / den
            l_h = m + jnp.log(den)
            o_pair = o_h if hh == 0 else jnp.where(lo_half, o_pair, o_h)
            l_pair = l_h if hh == 0 else jnp.where(lo_half, l_pair, l_h)
        o_ref[0, 0, rows, lanes_p] = o_pair.astype(BF16)
        if p == 0:
            l_first = l_pair
        else:
            lse_ref[0, 0, rows, :] = jnp.where(
                lane < LSE_LANES, l_first,
                jnp.where(lane < 2 * LSE_LANES, pltpu.roll(l_first, 2 * LSE_LANES, 1),
                          jnp.where(lane < 3 * LSE_LANES, pltpu.roll(l_pair, 2 * LSE_LANES, 1), l_pair)))


def _stage_attn(aq, ak, av, batch, seq, group):
    _, dil = DILATED_PATTERNS[group]
    L = seq // dil
    assert L % N_BACK == 0
    lq = min(2048, L)
    nq = L // lq
    sub = lq // N_BACK
    cur = pl.BlockSpec((1, 1, lq, A_WIDTH), lambda b, r, i: (b, r, i, 0))
    prev = pl.BlockSpec((1, 1, N_BACK, A_WIDTH), lambda b, r, i: (b, r, jnp.maximum(i * sub - 1, 0), 0))
    cur_lse = pl.BlockSpec((1, 1, lq, LANES), lambda b, r, i: (b, r, i, 0))
    return pl.pallas_call(
        functools.partial(_attn_body, dil=dil, slopes=tuple(_alibi_slopes()[group]), lq=lq),
        grid=(batch, dil, nq),
        in_specs=[cur, prev, cur, prev, cur],
        out_specs=(cur, cur_lse),
        out_shape=(jax.ShapeDtypeStruct((batch, dil, L, A_WIDTH), BF16),
                   jax.ShapeDtypeStruct((batch, dil, L, LANES), F32)),
        compiler_params=_params("parallel", "parallel", "parallel"),
        name=f"dilated_attn_d{dil}",
    )(aq, ak, ak, av, av)


PACK_ROWS = D_MODEL // (2 * LANES)
U32 = jnp.uint32
_HIGH_HALF = 0xFFFF0000


def _pack_rows(val, exact=False):
    half = D_MODEL // 2
    lo, hi = val[:, :half], val[:, half:]
    if exact:
        return (lax.bitcast_convert_type(lo, U32) >> 16) | lax.bitcast_convert_type(hi, U32)

    def bits(v):
        return lax.bitcast_convert_type(v.astype(BF16).astype(F32), U32)

    return (bits(lo) >> 16) | (bits(hi) & U32(_HIGH_HALF))


def _unpack_rows(words):
    lo = lax.bitcast_convert_type(words << 16, F32).astype(BF16)
    hi = lax.bitcast_convert_type(words & U32(_HIGH_HALF), F32).astype(BF16)
    return jnp.concatenate([lo, hi], axis=1)


def _flat(ref):
    rows = 1
    for d in ref.shape[:-2]:
        rows *= d
    return ref.reshape(rows * PACK_ROWS, LANES)


def _store_packed(flat_ref, row0, words):
    t = words.shape[0]
    for s in range(PACK_ROWS):
        flat_ref[pl.ds(row0 * PACK_ROWS + s, t, stride=PACK_ROWS), :] = words[:, s * LANES:(s + 1) * LANES]


def _load_packed(flat_ref, row0, t):
    return jnp.concatenate([flat_ref[pl.ds(row0 * PACK_ROWS + s, t, stride=PACK_ROWS), :] for s in range(PACK_ROWS)],
                           axis=1)


def _rows8(vals):
    t = vals[0].shape[1]
    rid = lax.broadcasted_iota(I32, (8, t), 0)
    out = jnp.zeros((8, t), vals[0].dtype)
    for k, v in enumerate(vals):
        out = jnp.where(rid == k, jnp.broadcast_to(v, (8, t)), out)
    return out


def _head_lanes(c):
    lane = lax.broadcasted_iota(I32, c.shape, 1)
    r1, r2, r3 = [pltpu.roll(c, k * LSE_LANES, 1) for k in (1, 2, 3)]
    left = jnp.where(lane < LSE_LANES, c, jnp.where(lane < 3 * LSE_LANES, r1, r2))
    right = jnp.where(lane < LSE_LANES, r2, jnp.where(lane < 3 * LSE_LANES, r3, c))
    return jnp.concatenate([left, right], axis=1)


def _merge_body(hm_ref, o1_ref, o2_ref, o3_ref, l1_ref, l2_ref, l3_ref, sgm_ref, sga_ref, x_ref,
                wm_ref, wa_ref, wo_ref, g2_ref, wrh_ref, br_ref,
                x2_ref, xn_ref, loc_ref, gate_ref, tcnt_ref, tcar_ref, cnt_ref, carry_ref, st_ref, logits_ref):
    step = pl.program_id(0)

    @pl.when(step == 0)
    def _():
        carry_ref[...] = jnp.zeros_like(carry_ref)
        logits_ref[...] = jnp.zeros_like(logits_ref)

    m_branch = jnp.dot(hm_ref[...], wm_ref[...], preferred_element_type=F32)
    dils = [d for _, d in DILATED_PATTERNS]
    l1, l2, l3 = [_merge_residues(r, d, st_ref) for r, d in zip((l1_ref, l2_ref, l3_ref), dils)]
    lmax = jnp.maximum(jnp.maximum(l1, l2), l3)
    e1, e2, e3 = jnp.exp(l1 - lmax), jnp.exp(l2 - lmax), jnp.exp(l3 - lmax)
    inv = 1.0 / (e1 + e2 + e3)
    h_a = _head_lanes(e1 * inv) * _merge_residues(o1_ref, dils[0], st_ref)
    h_a = h_a + _head_lanes(e2 * inv) * _merge_residues(o2_ref, dils[1], st_ref)
    h_a = h_a + _head_lanes(e3 * inv) * _merge_residues(o3_ref, dils[2], st_ref)
    y = (sgm_ref[...].astype(F32) * m_branch
         + sga_ref[...].astype(F32) * jnp.dot(h_a.astype(BF16), wa_ref[...], preferred_element_type=F32))
    x2 = x_ref[...] + jnp.dot(y.astype(BF16), wo_ref[...], preferred_element_type=F32)
    x2_ref[...] = x2
    xn = x2 * lax.rsqrt(jnp.mean(x2 * x2, axis=-1, keepdims=True) + EPS) * g2_ref[...]
    xh = xn.astype(BF16)
    xn_ref[...] = xh

    logits = logits_ref[...]
    t = logits.shape[1]
    eid = lax.broadcasted_iota(I32, (N_EXPERTS, t), 0).astype(F32)
    vals = logits
    top_v, top_i = [], []
    for _ in range(TOP_K):
        mx = jnp.max(vals, axis=0, keepdims=True)
        ik = jnp.min(jnp.where(vals == mx, eid, float(N_EXPERTS)), axis=0, keepdims=True)
        top_v.append(mx)
        top_i.append(ik)
        vals = jnp.where(eid == ik, -jnp.inf, vals)
    ex = [jnp.exp(v - top_v[0]) for v in top_v]
    den = ex[0] + ex[1] + ex[2] + ex[3]
    gate_ref[...] = _rows8([e / den for e in ex])

    chosen = jnp.zeros((N_EXPERTS, t), F32)
    for ik in top_i:
        chosen = chosen + (eid == ik).astype(F32)
    before = jnp.where(lax.broadcasted_iota(jnp.int16, (t, t), 0) < lax.broadcasted_iota(jnp.int16, (t, t), 1),
                       jnp.ones((), BF16), jnp.zeros((), BF16))
    prefix = jnp.dot(chosen.astype(BF16), before, preferred_element_type=F32)
    tcount = jnp.broadcast_to(jnp.sum(chosen, axis=1, keepdims=True), (N_EXPERTS, LANES))
    tcount = jnp.where(step > 0, tcount, 0.0)
    below = (lax.broadcasted_iota(I32, (N_EXPERTS, N_EXPERTS), 1)
             < lax.broadcasted_iota(I32, (N_EXPERTS, N_EXPERTS), 0)).astype(BF16)
    t_hi = jnp.floor(tcount * (1.0 / BF16_EXACT_INT)) * BF16_EXACT_INT
    tile_off = (jnp.dot(below, t_hi.astype(BF16), preferred_element_type=F32)
                + jnp.dot(below, (tcount - t_hi).astype(BF16), preferred_element_type=F32))
    pos = prefix + tile_off[:, 0:1]
    loc_ref[...] = _rows8([jnp.sum(jnp.where(eid == ik, pos, 0.0), axis=0, keepdims=True).astype(I32)
                           for ik in top_i])
    carry = carry_ref[...]
    tcnt_ref[...] = tcount.astype(I32)
    tcar_ref[...] = carry.astype(I32)
    total = carry + tcount
    carry_ref[...] = total
    cnt_ref[...] = total

    logits_ref[...] = lax.dot_general(wrh_ref[...], xh, _NT, preferred_element_type=F32) + br_ref[...]


def _stage_merge(h_m, attn, sgm, sga, x2d, w_mb, w_ab, w_out, norm2_g, w_router, b_router, batch, seq, tm):
    n = x2d.shape[0]
    steps = seq // tm
    (o1, l1), (o2, l2), (o3, l3) = attn
    wm = w_mb.astype(BF16)
    wa = w_ab.astype(BF16)
    wo = w_out.astype(BF16)
    g2 = norm2_g.astype(F32).reshape(1, D_MODEL)
    wrh = w_router.astype(BF16).T
    br = b_router.astype(F32).reshape(N_EXPERTS, 1)
    mixed = lambda i: jnp.minimum(i, n // tm - 1)
    routed = lambda i: jnp.maximum(i - 1, 0)
    row = lambda w: pl.BlockSpec((tm, w), lambda i: (mixed(i), 0))
    rowT = lambda r: pl.BlockSpec((r, tm), lambda i: (0, routed(i)))
    full = lambda a: pl.BlockSpec(a.shape, lambda i: (0,) * a.ndim)
    res = lambda d, w: pl.BlockSpec((1, d, tm // d, w), lambda i: (mixed(i) // steps, 0, mixed(i) % steps, 0))
    dils = [d for _, d in DILATED_PATTERNS]
    per_tile = pl.BlockSpec((N_EXPERTS, LANES), lambda i: (0, routed(i)))
    return pl.pallas_call(
        _merge_body,
        grid=(n // tm + 1,),
        in_specs=[row(M_WIDTH), *[res(d, A_WIDTH) for d in dils], *[res(d, LANES) for d in dils],
                  row(D_MODEL), row(D_MODEL), row(D_MODEL),
                  full(wm), full(wa), full(wo), full(g2), full(wrh), full(br)],
        out_specs=(row(D_MODEL), row(D_MODEL), rowT(8), rowT(8), per_tile, per_tile,
                   pl.BlockSpec((N_EXPERTS, LANES), lambda i: (0, 0))),
        out_shape=(jax.ShapeDtypeStruct((n, D_MODEL), F32),
                   jax.ShapeDtypeStruct((n, D_MODEL), BF16),
                   jax.ShapeDtypeStruct((8, n), I32),
                   jax.ShapeDtypeStruct((8, n), F32),
                   jax.ShapeDtypeStruct((N_EXPERTS, (n // tm) * LANES), I32),
                   jax.ShapeDtypeStruct((N_EXPERTS, (n // tm) * LANES), I32),
                   jax.ShapeDtypeStruct((N_EXPERTS, LANES), F32)),
        scratch_shapes=[pltpu.VMEM((N_EXPERTS, LANES), F32), pltpu.VMEM((2, 2, tm, LANES), F32),
                        pltpu.VMEM((N_EXPERTS, tm), F32)],
        compiler_params=_params("arbitrary"),
        name="merge_route",
    )(h_m, o1, o2, o3, l1, l2, l3, sgm, sga, x2d, wm, wa, wo, g2, wrh, br)


def _offsets_body(cnt_ref, blk_ref, pstart_ref, zlo_ref, zhi_ref, *, nblk_pad):
    cnt = cnt_ref[...]
    padded = jnp.floor((cnt + (MOE_BLOCK - 1)) * (1.0 / MOE_BLOCK)) * MOE_BLOCK
    lower = (lax.broadcasted_iota(I32, (N_EXPERTS, N_EXPERTS), 1)
             <= lax.broadcasted_iota(I32, (N_EXPERTS, N_EXPERTS), 0)).astype(BF16)
    nb = padded * (1.0 / MOE_BLOCK)
    nb_hi = jnp.floor(nb * (1.0 / BF16_EXACT_INT)) * BF16_EXACT_INT
    pends = (jnp.dot(lower, nb_hi.astype(BF16), preferred_element_type=F32)
             + jnp.dot(lower, (nb - nb_hi).astype(BF16), preferred_element_type=F32)) * MOE_BLOCK
    pstart = pends - padded
    pstart_ref[...] = pstart.astype(I32)
    zlo_ref[...] = (pstart + cnt).astype(I32)
    zhi_ref[...] = pends.astype(I32)

    first_row = (lax.broadcasted_iota(I32, (N_EXPERTS, nblk_pad), 1) * MOE_BLOCK).astype(F32)
    pe = jnp.broadcast_to(pends[:, 0:1], (N_EXPERTS, nblk_pad))
    be = jnp.sum((pe <= first_row).astype(F32), axis=0, keepdims=True)
    be = jnp.minimum(be, float(N_EXPERTS - 1))
    nused = pends[N_EXPERTS - 1:N_EXPERTS, 0:1] * (1.0 / MOE_BLOCK)
    nonempty = jnp.broadcast_to(padded[:, 0:1], (N_EXPERTS, nblk_pad)) > 0.0
    runidx = jnp.sum(jnp.logical_and(pe <= first_row, nonempty).astype(F32), axis=0, keepdims=True)
    parity = runidx - 2.0 * jnp.floor(runidx * 0.5)
    eid = lax.broadcasted_iota(I32, (N_EXPERTS, nblk_pad), 0).astype(F32)
    later = jnp.logical_and(eid > be, nonempty)
    nxt = jnp.min(jnp.where(later, eid, float(N_EXPERTS)), axis=0, keepdims=True)
    blk_ref[...] = _rows8([be.astype(I32), jnp.broadcast_to(nused, (1, nblk_pad)).astype(I32),
                           parity.astype(I32), nxt.astype(I32)])


def _stage_offsets(cnt, nblk):
    nblk_pad = -(-nblk // LANES) * LANES
    const = lambda r, c: pl.BlockSpec((r, c), lambda i: (0, 0))
    per_expert = jax.ShapeDtypeStruct((N_EXPERTS, LANES), I32)
    return pl.pallas_call(
        functools.partial(_offsets_body, nblk_pad=nblk_pad),
        grid=(1,),
        in_specs=[const(N_EXPERTS, LANES)],
        out_specs=(const(8, nblk_pad), const(N_EXPERTS, LANES), const(N_EXPERTS, LANES), const(N_EXPERTS, LANES)),
        out_shape=(jax.ShapeDtypeStruct((8, nblk_pad), I32), per_expert, per_expert, per_expert),
        compiler_params=_params("arbitrary"),
        name="route_offsets",
    )(cnt)


RUN_BITS = 10


def _tile_rows(ref, first_row, nrows):
    return ref.at[pl.ds(first_row, nrows)]


def _for_each_piece(length, fn):
    for b in reversed(range(RUN_BITS)):
        @pl.when(((length >> b) & 1) == 1)
        def _(b=b):
            fn((length >> (b + 1)) << (b + 1), 1 << b)


def _for_each_run(tile, tcnt_ref, tcar_ref, pstart_ref, fn):
    def per_expert(e, local):
        count = tcnt_ref[tile, e]
        first = pstart_ref[e] + tcar_ref[tile, e]
        _for_each_piece(count, lambda off, size: fn(local + off, first + off, size))
        return local + count

    lax.fori_loop(0, N_EXPERTS, per_expert, 0)


PERM_CHUNK = 256


def _dispatch_body(tcnt_ref, tcar_ref, pstart_ref, zlo_ref, zhi_ref, loc_ref, xn_ref, xs_hbm,
                   buf_ref, sems, *, tm):
    step = pl.program_id(0)
    nloc = TOP_K * tm

    def wait_buffer(slot):
        pltpu.make_async_copy(buf_ref.at[slot], xs_hbm.at[pl.ds(0, nloc)], sems.at[slot]).wait()

    for slot in range(2):
        tile = 2 * step + slot

        @pl.when(step > 0)
        def _(slot=slot):
            wait_buffer(slot)

        loc = loc_ref[:, slot * tm:(slot + 1) * tm].astype(jnp.int16)
        xn = xn_ref[slot * tm:(slot + 1) * tm, :]
        for c in range(TOP_K * tm // PERM_CHUNK):
            lid = lax.broadcasted_iota(jnp.int16, (PERM_CHUNK, tm), 0) + jnp.int16(c * PERM_CHUNK)
            hit = lid == loc[0:1, :]
            for k in range(1, TOP_K):
                hit = jnp.logical_or(hit, lid == loc[k:k + 1, :])
            perm = jnp.where(hit, jnp.ones((), BF16), jnp.zeros((), BF16))
            rows = jnp.dot(perm, xn, preferred_element_type=F32)
            _store_packed(_flat(buf_ref), slot * nloc + c * PERM_CHUNK, _pack_rows(rows, exact=True))

        def run_copy(local, first, size, slot=slot):
            return pltpu.make_async_copy(_tile_rows(buf_ref.at[slot], local, size),
                                         _tile_rows(xs_hbm, first, size), sems.at[slot])

        _for_each_run(tile, tcnt_ref, tcar_ref, pstart_ref, lambda l, f, s: run_copy(l, f, s).start())

    @pl.when(step == pl.num_programs(0) - 1)
    def _():
        wait_buffer(0)
        wait_buffer(1)
        zsrc = buf_ref.at[0]
        zsrc[pl.ds(0, MOE_BLOCK)] = jnp.zeros((MOE_BLOCK, PACK_ROWS, LANES), U32)

        def zero_copy(first, size):
            return pltpu.make_async_copy(_tile_rows(zsrc, 0, size), _tile_rows(xs_hbm, first, size), sems.at[0])

        first_unused = zhi_ref[N_EXPERTS - 1] // MOE_BLOCK
        nblk = xs_hbm.shape[0] // MOE_BLOCK

        def for_each_zero_copy(action):
            def per_expert(e, carry):
                lo = zlo_ref[e]
                _for_each_piece(zhi_ref[e] - lo, lambda off, size: action(zero_copy(lo + off, size)))
                return carry

            def tail(blk, carry):
                action(zero_copy(blk * MOE_BLOCK, MOE_BLOCK))
                return carry

            lax.fori_loop(0, N_EXPERTS, per_expert, 0)
            lax.fori_loop(first_unused, nblk, tail, 0)

        for_each_zero_copy(lambda copy: copy.start())
        for_each_zero_copy(lambda copy: copy.wait())


def _stage_dispatch(tables, loc8, xn, nrows, tm):
    n = xn.shape[0]
    assert TOP_K * tm >= MOE_BLOCK and (n // tm) % 2 == 0
    grid_spec = pltpu.PrefetchScalarGridSpec(
        num_scalar_prefetch=5,
        grid=(n // (2 * tm),),
        in_specs=[pl.BlockSpec((8, 2 * tm), lambda i, *_: (0, i)),
                  pl.BlockSpec((2 * tm, D_MODEL), lambda i, *_: (i, 0))],
        out_specs=pl.BlockSpec(memory_space=pl.ANY),
        scratch_shapes=[pltpu.VMEM((2, TOP_K * tm, PACK_ROWS, LANES), U32), pltpu.SemaphoreType.DMA((2,))],
    )
    return pl.pallas_call(
        functools.partial(_dispatch_body, tm=tm),
        grid_spec=grid_spec,
        out_shape=jax.ShapeDtypeStruct((nrows, PACK_ROWS, LANES), U32),
        compiler_params=_params("arbitrary"),
        name="dispatch",
    )(*tables, loc8, xn)


EXPERT_BLOCKS_PER_STEP = 2


def _expert_body(be_ref, nu_ref, par_ref, nxt_ref, xs_ref, w1_hbm, b1_ref, w2_hbm, b2_ref, ys_ref,
                 w1f_ref, w2f_ref, w1b_ref, w2b_ref, sems):
    def fetch(expert, slot):
        return (pltpu.make_async_copy(w1_hbm.at[expert], w1f_ref.at[slot], sems.at[slot, 0]),
                pltpu.make_async_copy(w2_hbm.at[expert], w2f_ref.at[slot], sems.at[slot, 1]))

    def block_info(sub):
        j = pl.program_id(0) * EXPERT_BLOCKS_PER_STEP + sub
        used = j < nu_ref[0]
        jj = jnp.maximum(jnp.minimum(j, nu_ref[0] - 1), 0)
        e = be_ref[jj]
        fresh = jnp.logical_or(j == 0, e != be_ref[jnp.maximum(jj - 1, 0)])
        return j, jj, e, used, fresh

    def refresh(j, jj, e):
        slot = par_ref[jj]

        @pl.when(j == 0)
        def _():
            for c in fetch(e, slot):
                c.start()

        for c in fetch(e, slot):
            c.wait()
        nxt = nxt_ref[jj]

        @pl.when(nxt < N_EXPERTS)
        def _():
            for c in fetch(nxt, 1 - slot):
                c.start()

        w1b_ref[...] = w1f_ref[slot].astype(BF16)
        w2b_ref[...] = w2f_ref[slot].astype(BF16)

    def compute(e, row0, nrows):
        xb = _unpack_rows(_load_packed(_flat(xs_ref), row0, nrows))
        gu = jnp.dot(xb, w1b_ref[...], preferred_element_type=F32) + b1_ref[pl.ds(e, 1), :]
        gate = jnp.minimum(gu[:, :D_FF], SWIGLU_LIMIT)
        lin = jnp.clip(gu[:, D_FF:], -SWIGLU_LIMIT, SWIGLU_LIMIT)
        act = (lin + 1.0) * (gate * jax.nn.sigmoid(SWIGLU_ALPHA * gate))
        ys = jnp.dot(act.astype(BF16), w2b_ref[...], preferred_element_type=F32) + b2_ref[pl.ds(e, 1), :]
        _store_packed(_flat(ys_ref), row0, _pack_rows(ys))

    j0, jj0, e0, used0, fresh0 = block_info(0)
    j1, jj1, e1, used1, fresh1 = block_info(1)
    same = jnp.logical_and(used1, e1 == e0)

    @pl.when(jnp.logical_and(used0, fresh0))
    def _():
        refresh(j0, jj0, e0)

    @pl.when(jnp.logical_and(used0, same))
    def _():
        compute(e0, 0, 2 * MOE_BLOCK)

    @pl.when(jnp.logical_and(used0, jnp.logical_not(same)))
    def _():
        compute(e0, 0, MOE_BLOCK)

    @pl.when(jnp.logical_and(used1, fresh1))
    def _():
        refresh(j1, jj1, e1)

    @pl.when(jnp.logical_and(used1, jnp.logical_not(same)))
    def _():
        compute(e1, MOE_BLOCK, MOE_BLOCK)

    for sub, used in ((0, used0), (1, used1)):
        @pl.when(jnp.logical_not(used))
        def _(sub=sub):
            ys_ref[pl.ds(sub * MOE_BLOCK, MOE_BLOCK)] = jnp.zeros((MOE_BLOCK, PACK_ROWS, LANES), U32)


def _stage_experts(blk8, xs, w1, b1, w2, b2):
    nrows = xs.shape[0]
    nblk = nrows // MOE_BLOCK
    assert EXPERT_BLOCKS_PER_STEP == 2 and nblk % EXPERT_BLOCKS_PER_STEP == 0
    block_e, nused, parity, nxt = blk8[0, :nblk], blk8[1, :1], blk8[2, :nblk], blk8[3, :nblk]
    tiles = (EXPERT_BLOCKS_PER_STEP * MOE_BLOCK, PACK_ROWS, LANES)
    full = lambda a: pl.BlockSpec(a.shape, lambda j, *_: (0,) * a.ndim)
    grid_spec = pltpu.PrefetchScalarGridSpec(
        num_scalar_prefetch=4,
        grid=(nblk // EXPERT_BLOCKS_PER_STEP,),
        in_specs=[pl.BlockSpec(tiles, lambda j, *_: (j, 0, 0)),
                  pl.BlockSpec(memory_space=pl.ANY), full(b1),
                  pl.BlockSpec(memory_space=pl.ANY), full(b2)],
        out_specs=pl.BlockSpec(tiles, lambda j, *_: (j, 0, 0)),
        scratch_shapes=[pltpu.VMEM((2, D_MODEL, 2 * D_FF), F32), pltpu.VMEM((2, D_FF, D_MODEL), F32),
                        pltpu.VMEM((D_MODEL, 2 * D_FF), BF16), pltpu.VMEM((D_FF, D_MODEL), BF16),
                        pltpu.SemaphoreType.DMA((2, 2))],
    )
    return pl.pallas_call(
        _expert_body,
        grid_spec=grid_spec,
        out_shape=jax.ShapeDtypeStruct((nrows, PACK_ROWS, LANES), U32),
        compiler_params=_params("arbitrary"),
        name="experts",
    )(block_e, nused, parity, nxt, xs, w1, b1, w2, b2)


COMBINE_CHUNK = 512


def _combine_body(tcnt_ref, tcar_ref, pstart_ref, loc_ref, gate_ref, x2_ref, ys_hbm, out_ref,
                  buf_ref, g_ref, sems, *, tm):
    step = pl.program_id(0)
    nloc = TOP_K * tm

    def start_runs(tile, slot):
        def run_copy(local, first, size):
            return pltpu.make_async_copy(_tile_rows(ys_hbm, first, size),
                                         _tile_rows(buf_ref.at[slot], local, size), sems.at[slot])
        _for_each_run(tile, tcnt_ref, tcar_ref, pstart_ref, lambda l, f, s: run_copy(l, f, s).start())

    def wait_buffer(slot):
        pltpu.make_async_copy(ys_hbm.at[pl.ds(0, nloc)], buf_ref.at[slot], sems.at[slot]).wait()

    def combine(slot):
        zpad = jnp.zeros((LANES - 16, LANES), F32)
        cols = []
        for c in range(tm // LANES):
            cols_in = slice(slot * tm + c * LANES, slot * tm + (c + 1) * LANES)
            cols.append(jnp.transpose(jnp.concatenate([loc_ref[:, cols_in].astype(F32), gate_ref[:, cols_in], zpad],
                                                      axis=0)))
        rows = slice(slot * tm, (slot + 1) * tm)
        acc = x2_ref[rows, :]
        for lc in range(nloc // COMBINE_CHUNK):
            lane = lax.broadcasted_iota(jnp.int16, (LANES, COMBINE_CHUNK), 1) + jnp.int16(lc * COMBINE_CHUNK)
            for c in range(tm // LANES):
                g = jnp.zeros((LANES, COMBINE_CHUNK), BF16)
                for k in range(TOP_K):
                    g = jnp.where(lane == cols[c][:, k:k + 1].astype(jnp.int16), cols[c][:, 8 + k:9 + k].astype(BF16), g)
                g_ref[c * LANES:(c + 1) * LANES, lc * COMBINE_CHUNK:(lc + 1) * COMBINE_CHUNK] = g
            if lc == 0:
                wait_buffer(slot)
            ys = _unpack_rows(_load_packed(_flat(buf_ref), slot * nloc + lc * COMBINE_CHUNK, COMBINE_CHUNK))
            acc = acc + jnp.dot(g_ref[:, lc * COMBINE_CHUNK:(lc + 1) * COMBINE_CHUNK], ys, preferred_element_type=F32)
        out_ref[rows, :] = acc

    @pl.when(step == 0)
    def _():
        start_runs(0, 0)

    start_runs(2 * step + 1, 1)
    combine(0)

    @pl.when(step + 1 < pl.num_programs(0))
    def _():
        start_runs(2 * step + 2, 0)

    combine(1)


def _stage_combine(tables, loc8, gate8, x2, ys, tm):
    n = x2.shape[0]
    assert (n // tm) % 2 == 0
    grid_spec = pltpu.PrefetchScalarGridSpec(
        num_scalar_prefetch=3,
        grid=(n // (2 * tm),),
        in_specs=[pl.BlockSpec((8, 2 * tm), lambda i, *_: (0, i)),
                  pl.BlockSpec((8, 2 * tm), lambda i, *_: (0, i)),
                  pl.BlockSpec((2 * tm, D_MODEL), lambda i, *_: (i, 0)),
                  pl.BlockSpec(memory_space=pl.ANY)],
        out_specs=pl.BlockSpec((2 * tm, D_MODEL), lambda i, *_: (i, 0)),
        scratch_shapes=[pltpu.VMEM((2, TOP_K * tm, PACK_ROWS, LANES), U32),
                        pltpu.VMEM((tm, TOP_K * tm), BF16),
                        pltpu.SemaphoreType.DMA((2,))],
    )
    return pl.pallas_call(
        functools.partial(_combine_body, tm=tm),
        grid_spec=grid_spec,
        out_shape=jax.ShapeDtypeStruct((n, D_MODEL), F32),
        compiler_params=_params("arbitrary"),
        name="combine",
    )(*tables, loc8, gate8, x2, ys)


def _moe(x2, xn, loc8, gate8, tcnt, tcar, cnt, w1, b1, w2, b2, tm):
    n = x2.shape[0]
    ntile = n // tm
    nblk = -(-(n * TOP_K) // MOE_BLOCK) + N_EXPERTS
    blk8, pstart, zlo, zhi = _stage_offsets(cnt, nblk)
    per_tile = lambda a: a.reshape(N_EXPERTS, ntile, LANES)[:, :, 0].T
    tables = (per_tile(tcnt), per_tile(tcar), pstart[:, 0])
    xs = _stage_dispatch(tables + (zlo[:, 0], zhi[:, 0]), loc8, xn, nblk * MOE_BLOCK, tm)
    ys = _stage_experts(blk8, xs, w1, b1, w2, b2)
    return _stage_combine(tables, loc8, gate8, x2, ys, tm)


def kernel(x, norm1_g, w_in, mlstm_gate_b, mlstm_norm_g, attn_q_norm_g, attn_k_norm_g, w_mlstm_branch,
           w_attn_branch, w_out, norm2_g, w_router, b_router, w1, b1, w2, b2):
    batch, seq, _ = x.shape
    n = batch * seq
    for l in range(norm1_g.shape[0]):
        x2d = x.reshape(n, D_MODEL)
        tm = min(512, seq)
        mq, kT, mv, so, gi, gf, aq, ak, av, sgm, sga = _stage_inproj(
            x2d, norm1_g[l], w_in[l], mlstm_gate_b[l], attn_q_norm_g[l], attn_k_norm_g[l], batch, seq, tm)
        h_m = _stage_mlstm(mq, kT, mv, so, gi, gf, mlstm_norm_g[l], batch, seq, tm)
        attn = [_stage_attn(aq[g], ak[g], av[g], batch, seq, g)
                for g in range(N_GROUPS)]
        x2, xn, loc8, gate8, tcnt, tcar, cnt = _stage_merge(
            h_m, attn, sgm, sga, x2d, w_mlstm_branch[l], w_attn_branch[l], w_out[l], norm2_g[l],
            w_router[l], b_router[l], batch, seq, tm)
        out = _moe(x2, xn, loc8, gate8, tcnt, tcar, cnt, w1[l], b1[l], w2[l], b2[l], tm)
        x = out.reshape(batch, seq, D_MODEL)
    return x
```

```python
import functools

import numpy as np
import jax
import jax.numpy as jnp
from jax import lax
from jax.experimental import pallas as pl
from jax.experimental.pallas import tpu as pltpu

F32 = jnp.float32
BF16 = jnp.bfloat16
I32 = jnp.int32

D_MODEL = 1024
M_HEADS = 4
M_QK_DIM = 64
M_V_DIM = 128
GATE_SOFTCAP = 15.0
A_HEADS = 4
A_HEAD_DIM = 64
DILATED_PATTERNS = ((128, 1), (512, 4), (2048, 16))
N_GROUPS = len(DILATED_PATTERNS)
N_BACK = 128
N_EXPERTS = 32
TOP_K = 4
D_FF = 1024
SWIGLU_LIMIT = 7.0
SWIGLU_ALPHA = 1.702
MOE_BLOCK = 512
EPS = 1e-6

M_WIDTH = M_HEADS * M_V_DIM
M_QK_WIDTH = M_HEADS * M_QK_DIM
A_WIDTH = A_HEADS * A_HEAD_DIM
IN_SPLITS = (M_QK_WIDTH, M_QK_WIDTH, M_WIDTH, M_WIDTH, 2 * M_HEADS,
             N_GROUPS * A_WIDTH, N_GROUPS * A_WIDTH, N_GROUPS * A_WIDTH, D_MODEL, D_MODEL)

LANES = 128
VMEM_LIMIT = 56 * 1024 * 1024

BF16_EXACT_INT = 256.0

_NT = (((1,), (1,)), ((), ()))
_TN = (((0,), (1,)), ((), ()))


def _alibi_slopes():
    n = N_GROUPS * A_HEADS
    s = np.exp2(-8.0 * np.arange(1, n + 1) / n).astype(np.float32)
    return s.reshape(N_GROUPS, A_HEADS)


def _params(*sem):
    return pltpu.CompilerParams(dimension_semantics=sem, vmem_limit_bytes=VMEM_LIMIT)


def _log_sigmoid(x):
    return jnp.minimum(x, 0.0) - jnp.log1p(jnp.exp(-jnp.abs(x)))


_A_WIDTH = sum(IN_SPLITS[:4])
_B_START = _A_WIDTH + IN_SPLITS[4]


def _piece_segments(widths):
    bounds, start = [], 0
    for width in widths:
        bounds.append((start, start + width))
        start += width
    return bounds


_C_MQ, _C_MK, _C_MV, _C_MO = _piece_segments(IN_SPLITS[:4])
_C_AQ, _C_AK, _C_AV, _C_GM, _C_GA = _piece_segments(IN_SPLITS[5:])


def _split_residues(val, d, out_ref, st_ref):
    t = val.shape[0]
    if d == 1:
        out_ref[0, 0] = val.astype(out_ref.dtype)
        return
    a_ref = st_ref.at[0]
    a_ref[0] = val[:, :LANES]
    a_ref[1] = val[:, LANES:]
    m = t // d
    if d == 16:
        b_ref = st_ref.at[1]
        for r0 in range(4):
            b_ref[0, r0 * 4 * m:(r0 + 1) * 4 * m, :] = a_ref[0, pl.ds(r0, 4 * m, stride=4), :]
            b_ref[1, r0 * 4 * m:(r0 + 1) * 4 * m, :] = a_ref[1, pl.ds(r0, 4 * m, stride=4), :]
        for r0 in range(4):
            for r1 in range(4):
                piece = jnp.concatenate([b_ref[0, pl.ds(r0 * 4 * m + r1, m, stride=4), :],
                                         b_ref[1, pl.ds(r0 * 4 * m + r1, m, stride=4), :]], axis=1)
                out_ref[0, 4 * r1 + r0] = piece.astype(out_ref.dtype)
        return
    for r in range(d):
        piece = jnp.concatenate([a_ref[0, pl.ds(r, m, stride=d), :], a_ref[1, pl.ds(r, m, stride=d), :]], axis=1)
        out_ref[0, r] = piece.astype(out_ref.dtype)


def _merge_residues(ref, d, st_ref):
    if d == 1:
        return ref[0, 0].astype(F32)
    m = ref.shape[2]
    halves = range(ref.shape[3] // LANES)
    if d == 16:
        a_ref, b_ref = st_ref.at[0], st_ref.at[1]
        for r0 in range(4):
            for r1 in range(4):
                blk = ref[0, 4 * r1 + r0].astype(F32)
                for c in halves:
                    a_ref[c, pl.ds(r0 * 4 * m + r1, m, stride=4), :] = blk[:, c * LANES:(c + 1) * LANES]
        for r0 in range(4):
            for c in halves:
                b_ref[c, pl.ds(r0, 4 * m, stride=4), :] = a_ref[c, r0 * 4 * m:(r0 + 1) * 4 * m, :]
        return jnp.concatenate([b_ref[c] for c in halves], axis=1)
    a_ref = st_ref.at[0]
    for r in range(d):
        blk = ref[0, r].astype(F32)
        for c in halves:
            a_ref[c, pl.ds(r, m, stride=d), :] = blk[:, c * LANES:(c + 1) * LANES]
    return jnp.concatenate([a_ref[c] for c in halves], axis=1)


def _inproj_body(x_ref, g1_ref, wa_ref, wg_ref, wb_ref, gb_ref, gq_ref, gk_ref,
                 mq_ref, kT_ref, mv_ref, so_ref, gi_ref, gf_ref,
                 q0_ref, q1_ref, q2_ref, k0_ref, k1_ref, k2_ref, v0_ref, v1_ref, v2_ref,
                 sgm_ref, sga_ref, st_ref):
    x = x_ref[...]
    h = x * lax.rsqrt(jnp.mean(x * x, axis=-1, keepdims=True) + EPS) * g1_ref[...]
    hb = h.astype(BF16)

    def seg(w_ref, c):
        return jnp.dot(hb, w_ref[:, c[0]:c[1]], preferred_element_type=F32)

    mq_ref[...] = seg(wa_ref, _C_MQ).astype(BF16)
    mv_ref[...] = seg(wa_ref, _C_MV).astype(BF16)
    so_ref[...] = jax.nn.sigmoid(seg(wa_ref, _C_MO)).astype(BF16)
    hid_r = lax.broadcasted_iota(I32, (A_WIDTH, A_WIDTH), 0) // A_HEAD_DIM
    hid_c = lax.broadcasted_iota(I32, (A_WIDTH, A_WIDTH), 1) // A_HEAD_DIM
    head_ones = (hid_r == hid_c).astype(BF16)
    for c, refs, gain_ref in ((_C_AQ, (q0_ref, q1_ref, q2_ref), gq_ref), (_C_AK, (k0_ref, k1_ref, k2_ref), gk_ref),
                              (_C_AV, (v0_ref, v1_ref, v2_ref), None)):
        val = seg(wb_ref, c)
        for g, ref in enumerate(refs):
            piece = val[:, g * A_WIDTH:(g + 1) * A_WIDTH]
            if gain_ref is not None:
                ss = jnp.dot((piece * piece).astype(BF16), head_ones, preferred_element_type=F32)
                piece = piece * lax.rsqrt(ss * (1.0 / A_HEAD_DIM) + EPS) * gain_ref[:, g * A_WIDTH:(g + 1) * A_WIDTH]
            _split_residues(piece, DILATED_PATTERNS[g][1], ref, st_ref)
    sgm_ref[...] = jax.nn.sigmoid(seg(wb_ref, _C_GM)).astype(BF16)
    sga_ref[...] = jax.nn.sigmoid(seg(wb_ref, _C_GA)).astype(BF16)

    kT_ref[...] = lax.dot_general(wa_ref[:, _C_MK[0]:_C_MK[1]], hb, _TN, preferred_element_type=F32).astype(BF16)
    zg = lax.dot_general(wg_ref[...], hb, _TN, preferred_element_type=F32)
    zi = zg[0:8] + gb_ref[0:8]
    zf = zg[M_HEADS:M_HEADS + 8] + gb_ref[8:16]
    gi_ref[...] = GATE_SOFTCAP * jnp.tanh(zi / GATE_SOFTCAP)
    gf_ref[...] = _log_sigmoid(GATE_SOFTCAP * jnp.tanh(zf / GATE_SOFTCAP))


def _stage_inproj(x2d, norm1_g, w_in, gate_b, gq, gk, batch, seq, tm):
    n = x2d.shape[0]
    steps = seq // tm
    wa = w_in[:, :_A_WIDTH].astype(BF16)
    wg = jnp.pad(w_in[:, _A_WIDTH:_B_START], ((0, 0), (0, LANES - IN_SPLITS[4]))).astype(BF16)
    wb = w_in[:, _B_START:].astype(BF16)
    gb = jnp.zeros((16, 1), F32)
    gb = gb.at[0:4, 0].set(gate_b[:M_HEADS].astype(F32)).at[8:12, 0].set(gate_b[M_HEADS:].astype(F32))
    g1 = norm1_g.astype(F32).reshape(1, D_MODEL)
    gq_t = (jnp.tile(gq.astype(F32), (1, A_HEADS)) * (A_HEAD_DIM ** -0.5)).reshape(1, N_GROUPS * A_WIDTH)
    gk_t = jnp.tile(gk.astype(F32), (1, A_HEADS)).reshape(1, N_GROUPS * A_WIDTH)

    row = lambda w: pl.BlockSpec((tm, w), lambda i: (i, 0))
    rowT = lambda r: pl.BlockSpec((r, tm), lambda i: (0, i))
    full = lambda a: pl.BlockSpec(a.shape, lambda i: (0,) * a.ndim)
    dils = [d for _, d in DILATED_PATTERNS]
    res_shape = lambda d: jax.ShapeDtypeStruct((batch, d, seq // d, A_WIDTH), BF16)
    res_spec = lambda d: pl.BlockSpec((1, d, tm // d, A_WIDTH), lambda i: (i // steps, 0, i % steps, 0))
    out_shapes = (
        jax.ShapeDtypeStruct((n, M_QK_WIDTH), BF16),
        jax.ShapeDtypeStruct((M_QK_WIDTH, n), BF16),
        jax.ShapeDtypeStruct((n, M_WIDTH), BF16),
        jax.ShapeDtypeStruct((n, M_WIDTH), BF16),
        jax.ShapeDtypeStruct((8, n), F32),
        jax.ShapeDtypeStruct((8, n), F32),
        *[res_shape(d) for d in dils], *[res_shape(d) for d in dils], *[res_shape(d) for d in dils],
        jax.ShapeDtypeStruct((n, D_MODEL), BF16),
        jax.ShapeDtypeStruct((n, D_MODEL), BF16),
    )
    out_specs = (row(M_QK_WIDTH), rowT(M_QK_WIDTH), row(M_WIDTH), row(M_WIDTH), rowT(8), rowT(8),
                 *[res_spec(d) for d in dils], *[res_spec(d) for d in dils], *[res_spec(d) for d in dils],
                 row(D_MODEL), row(D_MODEL))
    outs = pl.pallas_call(
        _inproj_body,
        grid=(n // tm,),
        in_specs=[row(D_MODEL), full(g1), full(wa), full(wg), full(wb), full(gb), full(gq_t), full(gk_t)],
        out_specs=out_specs,
        out_shape=out_shapes,
        scratch_shapes=[pltpu.VMEM((2, 2, tm, LANES), F32)],
        compiler_params=_params("parallel"),
        name="inproj",
    )(x2d, g1, wa, wg, wb, gb, gq_t, gk_t)
    mq, kT, mv, so, gi, gf = outs[:6]
    aq, ak, av = outs[6:9], outs[9:12], outs[12:15]
    return mq, kT, mv, so, gi, gf, aq, ak, av, outs[15], outs[16]


M_CHUNK_LEN = 128


def _mlstm_body(q_ref, v_ref, so_ref, ng_ref, *rest, nchunk, nseq):
    kT_refs, gi_refs, gf_refs = rest[0:nseq], rest[nseq:2 * nseq], rest[2 * nseq:3 * nseq]
    o_ref, c_ref, m_ref = rest[3 * nseq:]
    L = M_CHUNK_LEN

    @pl.when(pl.program_id(0) == 0)
    def _():
        c_ref[...] = jnp.zeros_like(c_ref)
        m_ref[...] = jnp.zeros_like(m_ref)

    lane8 = lax.broadcasted_iota(I32, (8, L), 1)
    causal = lax.broadcasted_iota(I32, (L, L), 1) <= lax.broadcasted_iota(I32, (L, L), 0)
    lo_half = lax.broadcasted_iota(I32, (L, LANES), 1) < M_QK_DIM
    ones = jnp.ones((L, M_V_DIM), BF16)

    heads = range(M_HEADS)
    cstate = [[c_ref[s, h * M_QK_DIM:(h + 1) * M_QK_DIM, :] for h in heads] for s in range(nseq)]
    m_prev = [m_ref[s, :, 0:1] for s in range(nseq)]
    units = []
    for c in range(nchunk):
        rows = slice(c * L, (c + 1) * L)
        for s in range(nseq):
            gi = gi_refs[s][:, rows]
            b = gf_refs[s][:, rows]
            sh = 1
            while sh < L:
                b = b + jnp.where(lane8 >= sh, pltpu.roll(b, sh, 1), 0.0)
                sh *= 2
            u = gi - b
            g = b[:, L - 1:L]
            a = g + u
            amax = jnp.max(a, axis=1, keepdims=True)
            m_new = jnp.maximum(g + m_prev[s], amax)
            w = jnp.exp(a - m_new) * (M_QK_DIM ** -0.5)
            s_old = jnp.exp(g + m_prev[s] - m_new)
            vext = [jnp.concatenate([v_ref[s, rows, h * M_V_DIM:(h + 1) * M_V_DIM], ones], axis=1) for h in heads]
            cloc = []
            for h in heads:
                hr = slice(h * M_QK_DIM, (h + 1) * M_QK_DIM)
                kw = (kT_refs[s][hr, rows].astype(F32) * w[h:h + 1, :]).astype(BF16)
                cloc.append(jnp.dot(kw, vext[h], preferred_element_type=F32))
            units.append(dict(seq=s, rows=rows, b=b, u=u, m_prev=m_prev[s], state=cstate[s], vext=vext))
            cstate[s] = [s_old[h:h + 1, :] * cstate[s][h] + cloc[h] for h in heads]
            m_prev[s] = m_new
    for s in range(nseq):
        for h in heads:
            c_ref[s, h * M_QK_DIM:(h + 1) * M_QK_DIM, :] = cstate[s][h]
        m_ref[s] = jnp.broadcast_to(m_prev[s], (8, LANES))

    for un in units:
        s, rows = un["seq"], un["rows"]
        un["s"], un["qc"] = [], []
        for p in range(M_HEADS // 2):
            lanes_p = slice(p * LANES, (p + 1) * LANES)
            q_pair = q_ref[s, rows, lanes_p]
            kT_pair = kT_refs[s][lanes_p, rows]
            c_pair = jnp.concatenate([un["state"][2 * p], un["state"][2 * p + 1]], axis=0).astype(BF16)
            for hh in range(2):
                qm = jnp.where(lo_half if hh == 0 else jnp.logical_not(lo_half), q_pair, jnp.zeros_like(q_pair))
                un["s"].append(jnp.dot(qm, kT_pair, preferred_element_type=F32) * (M_QK_DIM ** -0.5))
                un["qc"].append(jnp.dot(qm, c_pair, preferred_element_type=F32))

    for un in units:
        s, rows, b, u, mp = un["seq"], un["rows"], un["b"], un["u"], un["m_prev"]
        for h in heads:
            hl = slice(h * M_V_DIM, (h + 1) * M_V_DIM)
            bcol = jnp.transpose(jnp.broadcast_to(b[h:h + 1, :], (L, L)))
            dm = jnp.where(causal, bcol + u[h:h + 1, :], -jnp.inf)
            inter = bcol + mp[h:h + 1, :]
            m_t = jnp.maximum(inter, jnp.max(dm, axis=1, keepdims=True))
            pmat = (un["s"][h] * jnp.exp(dm - m_t)).astype(BF16)
            sc = jnp.exp(inter - m_t)
            out = (jnp.dot(pmat, un["vext"][h], preferred_element_type=F32)
                   + jnp.concatenate([sc, sc], axis=1) * un["qc"][h])
            hv = out[:, :M_V_DIM] / jnp.maximum(jnp.abs(out[:, M_V_DIM:]), jnp.exp(-m_t))
            hn = hv * lax.rsqrt(jnp.mean(hv * hv, axis=1, keepdims=True) + EPS)
            hn = hn * ng_ref[:, hl] * so_ref[s, rows, hl].astype(F32)
            o_ref[s, rows, hl] = hn.astype(BF16)


def _stage_mlstm(mq, kT, mv, so, gi, gf, norm_g, batch, seq, rows_per_step):
    n = batch * seq
    R = rows_per_step
    steps = seq // R
    ng = norm_g.astype(F32).reshape(1, M_WIDTH)
    per_seq = lambda a: a.reshape(batch, seq, a.shape[1])
    row = lambda w: pl.BlockSpec((batch, R, w), lambda i: (0, i, 0))
    colT = lambda r, s: pl.BlockSpec((r, R), lambda i, s=s: (0, s * steps + i))
    seqs = range(batch)
    out = pl.pallas_call(
        functools.partial(_mlstm_body, nchunk=R // M_CHUNK_LEN, nseq=batch),
        grid=(steps,),
        in_specs=[row(M_QK_WIDTH), row(M_WIDTH), row(M_WIDTH), pl.BlockSpec((1, M_WIDTH), lambda i: (0, 0)),
                  *[colT(M_QK_WIDTH, s) for s in seqs], *[colT(8, s) for s in seqs], *[colT(8, s) for s in seqs]],
        out_specs=row(M_WIDTH),
        out_shape=jax.ShapeDtypeStruct((batch, seq, M_WIDTH), BF16),
        scratch_shapes=[pltpu.VMEM((batch, M_QK_WIDTH, 2 * M_V_DIM), F32), pltpu.VMEM((batch, 8, LANES), F32)],
        compiler_params=_params("arbitrary"),
        name="mlstm",
    )(per_seq(mq), per_seq(mv), per_seq(so), ng, *[kT] * batch, *[gi] * batch, *[gf] * batch)
    return out.reshape(n, M_WIDTH)


LSE_LANES = LANES // A_HEADS


def _attn_body(q_ref, kp_ref, kc_ref, vp_ref, vc_ref, o_ref, lse_ref, *, dil, slopes, lq):
    QB = N_BACK
    first = pl.program_id(2) == 0
    qn = q_ref[0, 0]
    kcn = kc_ref[0, 0]
    kpn = kp_ref[0, 0]
    vc = vc_ref[0, 0]
    vp = vp_ref[0, 0]

    qi = lax.broadcasted_iota(I32, (QB, 2 * QB), 0)
    kj = lax.broadcasted_iota(I32, (QB, 2 * QB), 1)
    dist = qi + QB - kj
    band = jnp.logical_and(dist >= 0, dist <= N_BACK)
    distf = (dist * dil).astype(F32)
    bias = [jnp.where(band, -float(slopes[h]) * distf, -jnp.inf) for h in range(A_HEADS)]
    no_prev = jnp.logical_and(first, kj < QB)
    lo_half = lax.broadcasted_iota(I32, (QB, LANES), 1) < A_HEAD_DIM
    ones = jnp.ones((2 * QB, LANES), BF16)

    units = []
    for j in range(lq // QB):
        rows = slice(j * QB, (j + 1) * QB)
        prow = slice((j - 1) * QB, j * QB)
        keys = jnp.concatenate([kpn if j == 0 else kcn[prow], kcn[rows]], axis=0)
        vals = jnp.concatenate([vp if j == 0 else vc[prow], vc[rows]], axis=0)
        for p in range(A_HEADS // 2):
            lanes_p = slice(p * LANES, (p + 1) * LANES)
            q_pair = qn[rows, lanes_p]
            k_pair = keys[:, lanes_p]
            vext = jnp.concatenate([vals[:, lanes_p], ones], axis=1)
            scores = []
            for hh in range(2):
                sel = lo_half if hh == 0 else jnp.logical_not(lo_half)
                qm = jnp.where(sel, q_pair, jnp.zeros_like(q_pair))
                scores.append(lax.dot_general(qm, k_pair, _NT, preferred_element_type=F32))
            units.append((j, rows, lanes_p, p, vext, scores))

    lane = lax.broadcasted_iota(I32, (QB, LANES), 1)
    den_all = m_all = None
    for j, rows, lanes_p, p, vext, scores in units:
        o_pair = None
        for hh in range(2):
            s = scores[hh] + bias[2 * p + hh]
            if j == 0:
                s = jnp.where(no_prev, -jnp.inf, s)
            m = jnp.max(s, axis=1, keepdims=True)
            pv = jnp.dot(jnp.exp(s - m).astype(BF16), vext, preferred_element_type=F32)
            den = pv[:, LANES:]
            o_h = pv[:, :LANES] / den
            o_pair = o_h if hh == 0 else jnp.where(lo_half, o_pair, o_h)
            head = 2 * p + hh
            earlier = lane < head * LSE_LANES
            den_all = den if head == 0 else jnp.where(earlier, den_all, den)
            m_all = jnp.broadcast_to(m, den.shape) if head == 0 else jnp.where(earlier, m_all, m)
        o_ref[0, 0, rows, lanes_p] = o_pair.astype(BF16)
        if p == A_HEADS // 2 - 1:
            lse_ref[0, 0, rows, :] = m_all + jnp.log(den_all)


def _stage_attn(aq, ak, av, batch, seq, group):
    _, dil = DILATED_PATTERNS[group]
    L = seq // dil
    assert L % N_BACK == 0
    lq = min(2048, L)
    nq = L // lq
    sub = lq // N_BACK
    cur = pl.BlockSpec((1, 1, lq, A_WIDTH), lambda b, r, i: (b, r, i, 0))
    prev = pl.BlockSpec((1, 1, N_BACK, A_WIDTH), lambda b, r, i: (b, r, jnp.maximum(i * sub - 1, 0), 0))
    cur_lse = pl.BlockSpec((1, 1, lq, LANES), lambda b, r, i: (b, r, i, 0))
    return pl.pallas_call(
        functools.partial(_attn_body, dil=dil, slopes=tuple(_alibi_slopes()[group]), lq=lq),
        grid=(batch, dil, nq),
        in_specs=[cur, prev, cur, prev, cur],
        out_specs=(cur, cur_lse),
        out_shape=(jax.ShapeDtypeStruct((batch, dil, L, A_WIDTH), BF16),
                   jax.ShapeDtypeStruct((batch, dil, L, LANES), F32)),
        compiler_params=_params("parallel", "parallel", "parallel"),
        name=f"dilated_attn_d{dil}",
    )(aq, ak, ak, av, av)


PACK_ROWS = D_MODEL // (2 * LANES)
U32 = jnp.uint32
_HIGH_HALF = 0xFFFF0000


def _pack_rows(val, exact=False):
    half = D_MODEL // 2
    lo, hi = val[:, :half], val[:, half:]
    if exact:
        return (lax.bitcast_convert_type(lo, U32) >> 16) | lax.bitcast_convert_type(hi, U32)

    def bits(v):
        return lax.bitcast_convert_type(v.astype(BF16).astype(F32), U32)

    return (bits(lo) >> 16) | (bits(hi) & U32(_HIGH_HALF))


def _unpack_rows(words):
    lo = lax.bitcast_convert_type(words << 16, F32).astype(BF16)
    hi = lax.bitcast_convert_type(words & U32(_HIGH_HALF), F32).astype(BF16)
    return jnp.concatenate([lo, hi], axis=1)


def _flat(ref):
    rows = 1
    for d in ref.shape[:-2]:
        rows *= d
    return ref.reshape(rows * PACK_ROWS, LANES)


def _store_packed(flat_ref, row0, words):
    t = words.shape[0]
    for s in range(PACK_ROWS):
        flat_ref[pl.ds(row0 * PACK_ROWS + s, t, stride=PACK_ROWS), :] = words[:, s * LANES:(s + 1) * LANES]


def _load_packed(flat_ref, row0, t):
    return jnp.concatenate([flat_ref[pl.ds(row0 * PACK_ROWS + s, t, stride=PACK_ROWS), :] for s in range(PACK_ROWS)],
                           axis=1)


def _rows8(vals):
    t = vals[0].shape[1]
    rid = lax.broadcasted_iota(I32, (8, t), 0)
    out = jnp.zeros((8, t), vals[0].dtype)
    for k, v in enumerate(vals):
        out = jnp.where(rid == k, jnp.broadcast_to(v, (8, t)), out)
    return out


def _head_lanes(c):
    lane = lax.broadcasted_iota(I32, c.shape, 1)
    r1, r2, r3 = [pltpu.roll(c, k * LSE_LANES, 1) for k in (1, 2, 3)]
    left = jnp.where(lane < LSE_LANES, c, jnp.where(lane < 3 * LSE_LANES, r1, r2))
    right = jnp.where(lane < LSE_LANES, r2, jnp.where(lane < 3 * LSE_LANES, r3, c))
    return jnp.concatenate([left, right], axis=1)


def _merge_body(hm_ref, o1_ref, o2_ref, o3_ref, l1_ref, l2_ref, l3_ref, sgm_ref, sga_ref, x_ref,
                wm_ref, wa_ref, wo_ref, g2_ref, wrh_ref, br_ref,
                x2_ref, xn_ref, loc_ref, gate_ref, tcnt_ref, tcar_ref, cnt_ref, carry_ref, st_ref, logits_ref):
    step = pl.program_id(0)

    @pl.when(step == 0)
    def _():
        carry_ref[...] = jnp.zeros_like(carry_ref)
        logits_ref[...] = jnp.zeros_like(logits_ref)

    m_branch = jnp.dot(hm_ref[...], wm_ref[...], preferred_element_type=F32)
    dils = [d for _, d in DILATED_PATTERNS]
    l1, l2, l3 = [_merge_residues(r, d, st_ref) for r, d in zip((l1_ref, l2_ref, l3_ref), dils)]
    lmax = jnp.maximum(jnp.maximum(l1, l2), l3)
    e1, e2, e3 = jnp.exp(l1 - lmax), jnp.exp(l2 - lmax), jnp.exp(l3 - lmax)
    inv = 1.0 / (e1 + e2 + e3)
    h_a = _head_lanes(e1 * inv) * _merge_residues(o1_ref, dils[0], st_ref)
    h_a = h_a + _head_lanes(e2 * inv) * _merge_residues(o2_ref, dils[1], st_ref)
    h_a = h_a + _head_lanes(e3 * inv) * _merge_residues(o3_ref, dils[2], st_ref)
    y = (sgm_ref[...].astype(F32) * m_branch
         + sga_ref[...].astype(F32) * jnp.dot(h_a.astype(BF16), wa_ref[...], preferred_element_type=F32))
    x2 = x_ref[...] + jnp.dot(y.astype(BF16), wo_ref[...], preferred_element_type=F32)
    x2_ref[...] = x2
    xn = x2 * lax.rsqrt(jnp.mean(x2 * x2, axis=-1, keepdims=True) + EPS) * g2_ref[...]
    xh = xn.astype(BF16)
    xn_ref[...] = xh

    logits = logits_ref[...]
    t = logits.shape[1]
    eid = lax.broadcasted_iota(I32, (N_EXPERTS, t), 0).astype(F32)
    vals = logits
    top_v, top_i = [], []
    for _ in range(TOP_K):
        mx = jnp.max(vals, axis=0, keepdims=True)
        ik = jnp.min(jnp.where(vals == mx, eid, float(N_EXPERTS)), axis=0, keepdims=True)
        top_v.append(mx)
        top_i.append(ik)
        vals = jnp.where(eid == ik, -jnp.inf, vals)
    ex = [jnp.exp(v - top_v[0]) for v in top_v]
    den = ex[0] + ex[1] + ex[2] + ex[3]
    gate_ref[...] = _rows8([e / den for e in ex])

    chosen = jnp.zeros((N_EXPERTS, t), F32)
    for ik in top_i:
        chosen = chosen + (eid == ik).astype(F32)
    before = jnp.where(lax.broadcasted_iota(jnp.int16, (t, t), 0) < lax.broadcasted_iota(jnp.int16, (t, t), 1),
                       jnp.ones((), BF16), jnp.zeros((), BF16))
    prefix = jnp.dot(chosen.astype(BF16), before, preferred_element_type=F32)
    tcount = jnp.broadcast_to(jnp.sum(chosen, axis=1, keepdims=True), (N_EXPERTS, LANES))
    tcount = jnp.where(step > 0, tcount, 0.0)
    below = (lax.broadcasted_iota(I32, (N_EXPERTS, N_EXPERTS), 1)
             < lax.broadcasted_iota(I32, (N_EXPERTS, N_EXPERTS), 0)).astype(BF16)
    t_hi = jnp.floor(tcount * (1.0 / BF16_EXACT_INT)) * BF16_EXACT_INT
    tile_off = (jnp.dot(below, t_hi.astype(BF16), preferred_element_type=F32)
                + jnp.dot(below, (tcount - t_hi).astype(BF16), preferred_element_type=F32))
    pos = prefix + tile_off[:, 0:1]
    loc_ref[...] = _rows8([jnp.sum(jnp.where(eid == ik, pos, 0.0), axis=0, keepdims=True).astype(I32)
                           for ik in top_i])
    carry = carry_ref[...]
    tcnt_ref[...] = tcount.astype(I32)
    tcar_ref[...] = carry.astype(I32)
    total = carry + tcount
    carry_ref[...] = total
    cnt_ref[...] = total

    logits_ref[...] = lax.dot_general(wrh_ref[...], xh, _NT, preferred_element_type=F32) + br_ref[...]


def _stage_merge(h_m, attn, sgm, sga, x2d, w_mb, w_ab, w_out, norm2_g, w_router, b_router, batch, seq, tm):
    n = x2d.shape[0]
    steps = seq // tm
    (o1, l1), (o2, l2), (o3, l3) = attn
    wm = w_mb.astype(BF16)
    wa = w_ab.astype(BF16)
    wo = w_out.astype(BF16)
    g2 = norm2_g.astype(F32).reshape(1, D_MODEL)
    wrh = w_router.astype(BF16).T
    br = b_router.astype(F32).reshape(N_EXPERTS, 1)
    mixed = lambda i: jnp.minimum(i, n // tm - 1)
    routed = lambda i: jnp.maximum(i - 1, 0)
    row = lambda w: pl.BlockSpec((tm, w), lambda i: (mixed(i), 0))
    rowT = lambda r: pl.BlockSpec((r, tm), lambda i: (0, routed(i)))
    full = lambda a: pl.BlockSpec(a.shape, lambda i: (0,) * a.ndim)
    res = lambda d, w: pl.BlockSpec((1, d, tm // d, w), lambda i: (mixed(i) // steps, 0, mixed(i) % steps, 0))
    dils = [d for _, d in DILATED_PATTERNS]
    per_tile = pl.BlockSpec((N_EXPERTS, LANES), lambda i: (0, routed(i)))
    return pl.pallas_call(
        _merge_body,
        grid=(n // tm + 1,),
        in_specs=[row(M_WIDTH), *[res(d, A_WIDTH) for d in dils], *[res(d, LANES) for d in dils],
                  row(D_MODEL), row(D_MODEL), row(D_MODEL),
                  full(wm), full(wa), full(wo), full(g2), full(wrh), full(br)],
        out_specs=(row(D_MODEL), row(D_MODEL), rowT(8), rowT(8), per_tile, per_tile,
                   pl.BlockSpec((N_EXPERTS, LANES), lambda i: (0, 0))),
        out_shape=(jax.ShapeDtypeStruct((n, D_MODEL), F32),
                   jax.ShapeDtypeStruct((n, D_MODEL), BF16),
                   jax.ShapeDtypeStruct((8, n), I32),
                   jax.ShapeDtypeStruct((8, n), F32),
                   jax.ShapeDtypeStruct((N_EXPERTS, (n // tm) * LANES), I32),
                   jax.ShapeDtypeStruct((N_EXPERTS, (n // tm) * LANES), I32),
                   jax.ShapeDtypeStruct((N_EXPERTS, LANES), F32)),
        scratch_shapes=[pltpu.VMEM((N_EXPERTS, LANES), F32), pltpu.VMEM((2, 2, tm, LANES), F32),
                        pltpu.VMEM((N_EXPERTS, tm), F32)],
        compiler_params=_params("arbitrary"),
        name="merge_route",
    )(h_m, o1, o2, o3, l1, l2, l3, sgm, sga, x2d, wm, wa, wo, g2, wrh, br)


def _offsets_body(cnt_ref, blk_ref, pstart_ref, zlo_ref, zhi_ref, *, nblk_pad):
    cnt = cnt_ref[...]
    padded = jnp.floor((cnt + (MOE_BLOCK - 1)) * (1.0 / MOE_BLOCK)) * MOE_BLOCK
    lower = (lax.broadcasted_iota(I32, (N_EXPERTS, N_EXPERTS), 1)
             <= lax.broadcasted_iota(I32, (N_EXPERTS, N_EXPERTS), 0)).astype(BF16)
    nb = padded * (1.0 / MOE_BLOCK)
    nb_hi = jnp.floor(nb * (1.0 / BF16_EXACT_INT)) * BF16_EXACT_INT
    pends = (jnp.dot(lower, nb_hi.astype(BF16), preferred_element_type=F32)
             + jnp.dot(lower, (nb - nb_hi).astype(BF16), preferred_element_type=F32)) * MOE_BLOCK
    pstart = pends - padded
    pstart_ref[...] = pstart.astype(I32)
    zlo_ref[...] = (pstart + cnt).astype(I32)
    zhi_ref[...] = pends.astype(I32)

    first_row = (lax.broadcasted_iota(I32, (N_EXPERTS, nblk_pad), 1) * MOE_BLOCK).astype(F32)
    pe = jnp.broadcast_to(pends[:, 0:1], (N_EXPERTS, nblk_pad))
    be = jnp.sum((pe <= first_row).astype(F32), axis=0, keepdims=True)
    be = jnp.minimum(be, float(N_EXPERTS - 1))
    nused = pends[N_EXPERTS - 1:N_EXPERTS, 0:1] * (1.0 / MOE_BLOCK)
    nonempty = jnp.broadcast_to(padded[:, 0:1], (N_EXPERTS, nblk_pad)) > 0.0
    runidx = jnp.sum(jnp.logical_and(pe <= first_row, nonempty).astype(F32), axis=0, keepdims=True)
    parity = runidx - 2.0 * jnp.floor(runidx * 0.5)
    eid = lax.broadcasted_iota(I32, (N_EXPERTS, nblk_pad), 0).astype(F32)
    later = jnp.logical_and(eid > be, nonempty)
    nxt = jnp.min(jnp.where(later, eid, float(N_EXPERTS)), axis=0, keepdims=True)
    blk_ref[...] = _rows8([be.astype(I32), jnp.broadcast_to(nused, (1, nblk_pad)).astype(I32),
                           parity.astype(I32), nxt.astype(I32)])


def _stage_offsets(cnt, nblk):
    nblk_pad = -(-nblk // LANES) * LANES
    const = lambda r, c: pl.BlockSpec((r, c), lambda i: (0, 0))
    per_expert = jax.ShapeDtypeStruct((N_EXPERTS, LANES), I32)
    return pl.pallas_call(
        functools.partial(_offsets_body, nblk_pad=nblk_pad),
        grid=(1,),
        in_specs=[const(N_EXPERTS, LANES)],
        out_specs=(const(8, nblk_pad), const(N_EXPERTS, LANES), const(N_EXPERTS, LANES), const(N_EXPERTS, LANES)),
        out_shape=(jax.ShapeDtypeStruct((8, nblk_pad), I32), per_expert, per_expert, per_expert),
        compiler_params=_params("arbitrary"),
        name="route_offsets",
    )(cnt)


RUN_BITS = 10


def _tile_rows(ref, first_row, nrows):
    return ref.at[pl.ds(first_row, nrows)]


def _for_each_piece(length, fn):
    for b in reversed(range(RUN_BITS)):
        @pl.when(((length >> b) & 1) == 1)
        def _(b=b):
            fn((length >> (b + 1)) << (b + 1), 1 << b)


def _for_each_run(tile, tcnt_ref, tcar_ref, pstart_ref, fn):
    def per_expert(e, local):
        count = tcnt_ref[tile, e]
        first = pstart_ref[e] + tcar_ref[tile, e]
        _for_each_piece(count, lambda off, size: fn(local + off, first + off, size))
        return local + count

    lax.fori_loop(0, N_EXPERTS, per_expert, 0)


PERM_CHUNK = 256


def _dispatch_body(tcnt_ref, tcar_ref, pstart_ref, zlo_ref, zhi_ref, loc_ref, xn_ref, xs_hbm,
                   buf_ref, sems, *, tm):
    step = pl.program_id(0)
    nloc = TOP_K * tm

    def wait_buffer(slot):
        pltpu.make_async_copy(buf_ref.at[slot], xs_hbm.at[pl.ds(0, nloc)], sems.at[slot]).wait()

    for slot in range(2):
        tile = 2 * step + slot

        @pl.when(step > 0)
        def _(slot=slot):
            wait_buffer(slot)

        loc = loc_ref[:, slot * tm:(slot + 1) * tm].astype(jnp.int16)
        xn = xn_ref[slot * tm:(slot + 1) * tm, :]
        for c in range(TOP_K * tm // PERM_CHUNK):
            lid = lax.broadcasted_iota(jnp.int16, (PERM_CHUNK, tm), 0) + jnp.int16(c * PERM_CHUNK)
            hit = lid == loc[0:1, :]
            for k in range(1, TOP_K):
                hit = jnp.logical_or(hit, lid == loc[k:k + 1, :])
            perm = jnp.where(hit, jnp.ones((), BF16), jnp.zeros((), BF16))
            rows = jnp.dot(perm, xn, preferred_element_type=F32)
            _store_packed(_flat(buf_ref), slot * nloc + c * PERM_CHUNK, _pack_rows(rows, exact=True))

        def run_copy(local, first, size, slot=slot):
            return pltpu.make_async_copy(_tile_rows(buf_ref.at[slot], local, size),
                                         _tile_rows(xs_hbm, first, size), sems.at[slot])

        _for_each_run(tile, tcnt_ref, tcar_ref, pstart_ref, lambda l, f, s: run_copy(l, f, s).start())

    @pl.when(step == pl.num_programs(0) - 1)
    def _():
        wait_buffer(0)
        wait_buffer(1)
        zsrc = buf_ref.at[0]
        zsrc[pl.ds(0, MOE_BLOCK)] = jnp.zeros((MOE_BLOCK, PACK_ROWS, LANES), U32)

        def zero_copy(first, size):
            return pltpu.make_async_copy(_tile_rows(zsrc, 0, size), _tile_rows(xs_hbm, first, size), sems.at[0])

        first_unused = zhi_ref[N_EXPERTS - 1] // MOE_BLOCK
        nblk = xs_hbm.shape[0] // MOE_BLOCK

        def for_each_zero_copy(action):
            def per_expert(e, carry):
                lo = zlo_ref[e]
                _for_each_piece(zhi_ref[e] - lo, lambda off, size: action(zero_copy(lo + off, size)))
                return carry

            def tail(blk, carry):
                action(zero_copy(blk * MOE_BLOCK, MOE_BLOCK))
                return carry

            lax.fori_loop(0, N_EXPERTS, per_expert, 0)
            lax.fori_loop(first_unused, nblk, tail, 0)

        for_each_zero_copy(lambda copy: copy.start())
        for_each_zero_copy(lambda copy: copy.wait())


def _stage_dispatch(tables, loc8, xn, nrows, tm):
    n = xn.shape[0]
    assert TOP_K * tm >= MOE_BLOCK and (n // tm) % 2 == 0
    grid_spec = pltpu.PrefetchScalarGridSpec(
        num_scalar_prefetch=5,
        grid=(n // (2 * tm),),
        in_specs=[pl.BlockSpec((8, 2 * tm), lambda i, *_: (0, i)),
                  pl.BlockSpec((2 * tm, D_MODEL), lambda i, *_: (i, 0))],
        out_specs=pl.BlockSpec(memory_space=pl.ANY),
        scratch_shapes=[pltpu.VMEM((2, TOP_K * tm, PACK_ROWS, LANES), U32), pltpu.SemaphoreType.DMA((2,))],
    )
    return pl.pallas_call(
        functools.partial(_dispatch_body, tm=tm),
        grid_spec=grid_spec,
        out_shape=jax.ShapeDtypeStruct((nrows, PACK_ROWS, LANES), U32),
        compiler_params=_params("arbitrary"),
        name="dispatch",
    )(*tables, loc8, xn)


EXPERT_BLOCKS_PER_STEP = 2


def _expert_body(be_ref, nu_ref, par_ref, nxt_ref, xs_ref, w1_hbm, b1_ref, w2_hbm, b2_ref, ys_ref,
                 w1f_ref, w2f_ref, w1b_ref, w2b_ref, sems):
    def fetch(expert, slot):
        return (pltpu.make_async_copy(w1_hbm.at[expert], w1f_ref.at[slot], sems.at[slot, 0]),
                pltpu.make_async_copy(w2_hbm.at[expert], w2f_ref.at[slot], sems.at[slot, 1]))

    def block_info(sub):
        j = pl.program_id(0) * EXPERT_BLOCKS_PER_STEP + sub
        used = j < nu_ref[0]
        jj = jnp.maximum(jnp.minimum(j, nu_ref[0] - 1), 0)
        e = be_ref[jj]
        fresh = jnp.logical_or(j == 0, e != be_ref[jnp.maximum(jj - 1, 0)])
        return j, jj, e, used, fresh

    def refresh(j, jj, e):
        slot = par_ref[jj]

        @pl.when(j == 0)
        def _():
            for c in fetch(e, slot):
                c.start()

        for c in fetch(e, slot):
            c.wait()
        nxt = nxt_ref[jj]

        @pl.when(nxt < N_EXPERTS)
        def _():
            for c in fetch(nxt, 1 - slot):
                c.start()

        w1b_ref[...] = w1f_ref[slot].astype(BF16)
        w2b_ref[...] = w2f_ref[slot].astype(BF16)

    def compute(e, row0, nrows):
        xb = _unpack_rows(_load_packed(_flat(xs_ref), row0, nrows))
        gu = jnp.dot(xb, w1b_ref[...], preferred_element_type=F32) + b1_ref[pl.ds(e, 1), :]
        gate = jnp.minimum(gu[:, :D_FF], SWIGLU_LIMIT)
        lin = jnp.clip(gu[:, D_FF:], -SWIGLU_LIMIT, SWIGLU_LIMIT)
        act = (lin + 1.0) * (gate * jax.nn.sigmoid(SWIGLU_ALPHA * gate))
        ys = jnp.dot(act.astype(BF16), w2b_ref[...], preferred_element_type=F32) + b2_ref[pl.ds(e, 1), :]
        _store_packed(_flat(ys_ref), row0, _pack_rows(ys))

    j0, jj0, e0, used0, fresh0 = block_info(0)
    j1, jj1, e1, used1, fresh1 = block_info(1)
    same = jnp.logical_and(used1, e1 == e0)

    @pl.when(jnp.logical_and(used0, fresh0))
    def _():
        refresh(j0, jj0, e0)

    @pl.when(jnp.logical_and(used0, same))
    def _():
        compute(e0, 0, 2 * MOE_BLOCK)

    @pl.when(jnp.logical_and(used0, jnp.logical_not(same)))
    def _():
        compute(e0, 0, MOE_BLOCK)

    @pl.when(jnp.logical_and(used1, fresh1))
    def _():
        refresh(j1, jj1, e1)

    @pl.when(jnp.logical_and(used1, jnp.logical_not(same)))
    def _():
        compute(e1, MOE_BLOCK, MOE_BLOCK)

    for sub, used in ((0, used0), (1, used1)):
        @pl.when(jnp.logical_not(used))
        def _(sub=sub):
            ys_ref[pl.ds(sub * MOE_BLOCK, MOE_BLOCK)] = jnp.zeros((MOE_BLOCK, PACK_ROWS, LANES), U32)


def _stage_experts(blk8, xs, w1, b1, w2, b2):
    nrows = xs.shape[0]
    nblk = nrows // MOE_BLOCK
    assert EXPERT_BLOCKS_PER_STEP == 2 and nblk % EXPERT_BLOCKS_PER_STEP == 0
    block_e, nused, parity, nxt = blk8[0, :nblk], blk8[1, :1], blk8[2, :nblk], blk8[3, :nblk]
    tiles = (EXPERT_BLOCKS_PER_STEP * MOE_BLOCK, PACK_ROWS, LANES)
    full = lambda a: pl.BlockSpec(a.shape, lambda j, *_: (0,) * a.ndim)
    grid_spec = pltpu.PrefetchScalarGridSpec(
        num_scalar_prefetch=4,
        grid=(nblk // EXPERT_BLOCKS_PER_STEP,),
        in_specs=[pl.BlockSpec(tiles, lambda j, *_: (j, 0, 0)),
                  pl.BlockSpec(memory_space=pl.ANY), full(b1),
                  pl.BlockSpec(memory_space=pl.ANY), full(b2)],
        out_specs=pl.BlockSpec(tiles, lambda j, *_: (j, 0, 0)),
        scratch_shapes=[pltpu.VMEM((2, D_MODEL, 2 * D_FF), F32), pltpu.VMEM((2, D_FF, D_MODEL), F32),
                        pltpu.VMEM((D_MODEL, 2 * D_FF), BF16), pltpu.VMEM((D_FF, D_MODEL), BF16),
                        pltpu.SemaphoreType.DMA((2, 2))],
    )
    return pl.pallas_call(
        _expert_body,
        grid_spec=grid_spec,
        out_shape=jax.ShapeDtypeStruct((nrows, PACK_ROWS, LANES), U32),
        compiler_params=_params("arbitrary"),
        name="experts",
    )(block_e, nused, parity, nxt, xs, w1, b1, w2, b2)


COMBINE_CHUNK = 512


def _combine_body(tcnt_ref, tcar_ref, pstart_ref, loc_ref, gate_ref, x2_ref, ys_hbm, out_ref,
                  buf_ref, g_ref, sems, *, tm):
    step = pl.program_id(0)
    nloc = TOP_K * tm

    def start_runs(tile, slot):
        def run_copy(local, first, size):
            return pltpu.make_async_copy(_tile_rows(ys_hbm, first, size),
                                         _tile_rows(buf_ref.at[slot], local, size), sems.at[slot])
        _for_each_run(tile, tcnt_ref, tcar_ref, pstart_ref, lambda l, f, s: run_copy(l, f, s).start())

    def wait_buffer(slot):
        pltpu.make_async_copy(ys_hbm.at[pl.ds(0, nloc)], buf_ref.at[slot], sems.at[slot]).wait()

    def combine(slot):
        zpad = jnp.zeros((LANES - 16, LANES), F32)
        cols = []
        for c in range(tm // LANES):
            cols_in = slice(slot * tm + c * LANES, slot * tm + (c + 1) * LANES)
            cols.append(jnp.transpose(jnp.concatenate([loc_ref[:, cols_in].astype(F32), gate_ref[:, cols_in], zpad],
                                                      axis=0)))
        rows = slice(slot * tm, (slot + 1) * tm)
        acc = x2_ref[rows, :]
        for lc in range(nloc // COMBINE_CHUNK):
            lane = lax.broadcasted_iota(jnp.int16, (LANES, COMBINE_CHUNK), 1) + jnp.int16(lc * COMBINE_CHUNK)
            for c in range(tm // LANES):
                g = jnp.zeros((LANES, COMBINE_CHUNK), BF16)
                for k in range(TOP_K):
                    g = jnp.where(lane == cols[c][:, k:k + 1].astype(jnp.int16), cols[c][:, 8 + k:9 + k].astype(BF16), g)
                g_ref[c * LANES:(c + 1) * LANES, lc * COMBINE_CHUNK:(lc + 1) * COMBINE_CHUNK] = g
            if lc == 0:
                wait_buffer(slot)
            ys = _unpack_rows(_load_packed(_flat(buf_ref), slot * nloc + lc * COMBINE_CHUNK, COMBINE_CHUNK))
            acc = acc + jnp.dot(g_ref[:, lc * COMBINE_CHUNK:(lc + 1) * COMBINE_CHUNK], ys, preferred_element_type=F32)
        out_ref[rows, :] = acc

    @pl.when(step == 0)
    def _():
        start_runs(0, 0)

    start_runs(2 * step + 1, 1)
    combine(0)

    @pl.when(step + 1 < pl.num_programs(0))
    def _():
        start_runs(2 * step + 2, 0)

    combine(1)


def _stage_combine(tables, loc8, gate8, x2, ys, tm):
    n = x2.shape[0]
    assert (n // tm) % 2 == 0
    grid_spec = pltpu.PrefetchScalarGridSpec(
        num_scalar_prefetch=3,
        grid=(n // (2 * tm),),
        in_specs=[pl.BlockSpec((8, 2 * tm), lambda i, *_: (0, i)),
                  pl.BlockSpec((8, 2 * tm), lambda i, *_: (0, i)),
                  pl.BlockSpec((2 * tm, D_MODEL), lambda i, *_: (i, 0)),
                  pl.BlockSpec(memory_space=pl.ANY)],
        out_specs=pl.BlockSpec((2 * tm, D_MODEL), lambda i, *_: (i, 0)),
        scratch_shapes=[pltpu.VMEM((2, TOP_K * tm, PACK_ROWS, LANES), U32),
                        pltpu.VMEM((tm, TOP_K * tm), BF16),
                        pltpu.SemaphoreType.DMA((2,))],
    )
    return pl.pallas_call(
        functools.partial(_combine_body, tm=tm),
        grid_spec=grid_spec,
        out_shape=jax.ShapeDtypeStruct((n, D_MODEL), F32),
        compiler_params=_params("arbitrary"),
        name="combine",
    )(*tables, loc8, gate8, x2, ys)


def _moe(x2, xn, loc8, gate8, tcnt, tcar, cnt, w1, b1, w2, b2, tm):
    n = x2.shape[0]
    ntile = n // tm
    nblk = -(-(n * TOP_K) // MOE_BLOCK) + N_EXPERTS
    blk8, pstart, zlo, zhi = _stage_offsets(cnt, nblk)
    per_tile = lambda a: a.reshape(N_EXPERTS, ntile, LANES)[:, :, 0].T
    tables = (per_tile(tcnt), per_tile(tcar), pstart[:, 0])
    xs = _stage_dispatch(tables + (zlo[:, 0], zhi[:, 0]), loc8, xn, nblk * MOE_BLOCK, tm)
    ys = _stage_experts(blk8, xs, w1, b1, w2, b2)
    return _stage_combine(tables, loc8, gate8, x2, ys, tm)


def kernel(x, norm1_g, w_in, mlstm_gate_b, mlstm_norm_g, attn_q_norm_g, attn_k_norm_g, w_mlstm_branch,
           w_attn_branch, w_out, norm2_g, w_router, b_router, w1, b1, w2, b2):
    batch, seq, _ = x.shape
    n = batch * seq
    for l in range(norm1_g.shape[0]):
        x2d = x.reshape(n, D_MODEL)
        tm = min(512, seq)
        mq, kT, mv, so, gi, gf, aq, ak, av, sgm, sga = _stage_inproj(
            x2d, norm1_g[l], w_in[l], mlstm_gate_b[l], attn_q_norm_g[l], attn_k_norm_g[l], batch, seq, tm)
        h_m = _stage_mlstm(mq, kT, mv, so, gi, gf, mlstm_norm_g[l], batch, seq, tm)
        attn = [_stage_attn(aq[g], ak[g], av[g], batch, seq, g)
                for g in range(N_GROUPS)]
        x2, xn, loc8, gate8, tcnt, tcar, cnt = _stage_merge(
            h_m, attn, sgm, sga, x2d, w_mlstm_branch[l], w_attn_branch[l], w_out[l], norm2_g[l],
            w_router[l], b_router[l], batch, seq, tm)
        out = _moe(x2, xn, loc8, gate8, tcnt, tcar, cnt, w1[l], b1[l], w2[l], b2[l], tm)
        x = out.reshape(batch, seq, D_MODEL)
    return x
```

```python
import functools

import numpy as np
import jax
import jax.numpy as jnp
from jax import lax
from jax.experimental import pallas as pl
from jax.experimental.pallas import tpu as pltpu

F32 = jnp.float32
BF16 = jnp.bfloat16
I32 = jnp.int32

D_MODEL = 1024
M_HEADS = 4
M_QK_DIM = 64
M_V_DIM = 128
GATE_SOFTCAP = 15.0
A_HEADS = 4
A_HEAD_DIM = 64
DILATED_PATTERNS = ((128, 1), (512, 4), (2048, 16))
N_GROUPS = len(DILATED_PATTERNS)
N_BACK = 128
N_EXPERTS = 32
TOP_K = 4
D_FF = 1024
SWIGLU_LIMIT = 7.0
SWIGLU_ALPHA = 1.702
MOE_BLOCK = 512
EPS = 1e-6

M_WIDTH = M_HEADS * M_V_DIM
M_QK_WIDTH = M_HEADS * M_QK_DIM
A_WIDTH = A_HEADS * A_HEAD_DIM
IN_SPLITS = (M_QK_WIDTH, M_QK_WIDTH, M_WIDTH, M_WIDTH, 2 * M_HEADS,
             N_GROUPS * A_WIDTH, N_GROUPS * A_WIDTH, N_GROUPS * A_WIDTH, D_MODEL, D_MODEL)

LANES = 128
VMEM_LIMIT = 56 * 1024 * 1024

BF16_EXACT_INT = 256.0

_NT = (((1,), (1,)), ((), ()))
_TN = (((0,), (1,)), ((), ()))


def _alibi_slopes():
    n = N_GROUPS * A_HEADS
    s = np.exp2(-8.0 * np.arange(1, n + 1) / n).astype(np.float32)
    return s.reshape(N_GROUPS, A_HEADS)


def _params(*sem):
    return pltpu.CompilerParams(dimension_semantics=sem, vmem_limit_bytes=VMEM_LIMIT)


def _log_sigmoid(x):
    return jnp.minimum(x, 0.0) - jnp.log1p(jnp.exp(-jnp.abs(x)))


_A_WIDTH = sum(IN_SPLITS[:4])
_B_START = _A_WIDTH + IN_SPLITS[4]


def _piece_segments(widths):
    bounds, start = [], 0
    for width in widths:
        bounds.append((start, start + width))
        start += width
    return bounds


_C_MQ, _C_MK, _C_MV, _C_MO = _piece_segments(IN_SPLITS[:4])
_C_AQ, _C_AK, _C_AV, _C_GM, _C_GA = _piece_segments(IN_SPLITS[5:])


def _split_residues(val, d, out_ref, st_ref):
    t = val.shape[0]
    if d == 1:
        out_ref[0, 0] = val.astype(out_ref.dtype)
        return
    a_ref = st_ref.at[0]
    a_ref[0] = val[:, :LANES]
    a_ref[1] = val[:, LANES:]
    m = t // d
    if d == 16:
        b_ref = st_ref.at[1]
        for r0 in range(4):
            b_ref[0, r0 * 4 * m:(r0 + 1) * 4 * m, :] = a_ref[0, pl.ds(r0, 4 * m, stride=4), :]
            b_ref[1, r0 * 4 * m:(r0 + 1) * 4 * m, :] = a_ref[1, pl.ds(r0, 4 * m, stride=4), :]
        for r0 in range(4):
            for r1 in range(4):
                piece = jnp.concatenate([b_ref[0, pl.ds(r0 * 4 * m + r1, m, stride=4), :],
                                         b_ref[1, pl.ds(r0 * 4 * m + r1, m, stride=4), :]], axis=1)
                out_ref[0, 4 * r1 + r0] = piece.astype(out_ref.dtype)
        return
    for r in range(d):
        piece = jnp.concatenate([a_ref[0, pl.ds(r, m, stride=d), :], a_ref[1, pl.ds(r, m, stride=d), :]], axis=1)
        out_ref[0, r] = piece.astype(out_ref.dtype)


def _merge_residues(ref, d, st_ref):
    if d == 1:
        return ref[0, 0].astype(F32)
    m = ref.shape[2]
    halves = range(ref.shape[3] // LANES)
    if d == 16:
        a_ref, b_ref = st_ref.at[0], st_ref.at[1]
        for r0 in range(4):
            for r1 in range(4):
                blk = ref[0, 4 * r1 + r0].astype(F32)
                for c in halves:
                    a_ref[c, pl.ds(r0 * 4 * m + r1, m, stride=4), :] = blk[:, c * LANES:(c + 1) * LANES]
        for r0 in range(4):
            for c in halves:
                b_ref[c, pl.ds(r0, 4 * m, stride=4), :] = a_ref[c, r0 * 4 * m:(r0 + 1) * 4 * m, :]
        return jnp.concatenate([b_ref[c] for c in halves], axis=1)
    a_ref = st_ref.at[0]
    for r in range(d):
        blk = ref[0, r].astype(F32)
        for c in halves:
            a_ref[c, pl.ds(r, m, stride=d), :] = blk[:, c * LANES:(c + 1) * LANES]
    return jnp.concatenate([a_ref[c] for c in halves], axis=1)


def _inproj_body(x_ref, g1_ref, wa_ref, wg_ref, wb_ref, gb_ref, gq_ref, gk_ref,
                 mq_ref, kT_ref, mv_ref, so_ref, gi_ref, gf_ref,
                 q0_ref, q1_ref, q2_ref, k0_ref, k1_ref, k2_ref, v0_ref, v1_ref, v2_ref,
                 sgm_ref, sga_ref, st_ref):
    x = x_ref[...]
    h = x * lax.rsqrt(jnp.mean(x * x, axis=-1, keepdims=True) + EPS) * g1_ref[...]
    hb = h.astype(BF16)

    def seg(w_ref, c):
        return jnp.dot(hb, w_ref[:, c[0]:c[1]], preferred_element_type=F32)

    mq_ref[...] = seg(wa_ref, _C_MQ).astype(BF16)
    mv_ref[...] = seg(wa_ref, _C_MV).astype(BF16)
    so_ref[...] = jax.nn.sigmoid(seg(wa_ref, _C_MO)).astype(BF16)
    hid_r = lax.broadcasted_iota(I32, (A_WIDTH, A_WIDTH), 0) // A_HEAD_DIM
    hid_c = lax.broadcasted_iota(I32, (A_WIDTH, A_WIDTH), 1) // A_HEAD_DIM
    head_ones = (hid_r == hid_c).astype(BF16)
    for c, refs, gain_ref in ((_C_AQ, (q0_ref, q1_ref, q2_ref), gq_ref), (_C_AK, (k0_ref, k1_ref, k2_ref), gk_ref),
                              (_C_AV, (v0_ref, v1_ref, v2_ref), None)):
        val = seg(wb_ref, c)
        for g, ref in enumerate(refs):
            piece = val[:, g * A_WIDTH:(g + 1) * A_WIDTH]
            if gain_ref is not None:
                ss = jnp.dot((piece * piece).astype(BF16), head_ones, preferred_element_type=F32)
                piece = piece * lax.rsqrt(ss * (1.0 / A_HEAD_DIM) + EPS) * gain_ref[:, g * A_WIDTH:(g + 1) * A_WIDTH]
            _split_residues(piece, DILATED_PATTERNS[g][1], ref, st_ref)
    sgm_ref[...] = jax.nn.sigmoid(seg(wb_ref, _C_GM)).astype(BF16)
    sga_ref[...] = jax.nn.sigmoid(seg(wb_ref, _C_GA)).astype(BF16)

    kT_ref[...] = lax.dot_general(wa_ref[:, _C_MK[0]:_C_MK[1]], hb, _TN, preferred_element_type=F32).astype(BF16)
    zg = lax.dot_general(wg_ref[...], hb, _TN, preferred_element_type=F32)
    zi = zg[0:8] + gb_ref[0:8]
    zf = zg[M_HEADS:M_HEADS + 8] + gb_ref[8:16]
    gi_ref[...] = GATE_SOFTCAP * jnp.tanh(zi / GATE_SOFTCAP)
    gf_ref[...] = _log_sigmoid(GATE_SOFTCAP * jnp.tanh(zf / GATE_SOFTCAP))


def _stage_inproj(x2d, norm1_g, w_in, gate_b, gq, gk, batch, seq, tm):
    n = x2d.shape[0]
    steps = seq // tm
    wa = w_in[:, :_A_WIDTH].astype(BF16)
    wg = jnp.pad(w_in[:, _A_WIDTH:_B_START], ((0, 0), (0, LANES - IN_SPLITS[4]))).astype(BF16)
    wb = w_in[:, _B_START:].astype(BF16)
    gb = jnp.zeros((16, 1), F32)
    gb = gb.at[0:4, 0].set(gate_b[:M_HEADS].astype(F32)).at[8:12, 0].set(gate_b[M_HEADS:].astype(F32))
    g1 = norm1_g.astype(F32).reshape(1, D_MODEL)
    gq_t = (jnp.tile(gq.astype(F32), (1, A_HEADS)) * (A_HEAD_DIM ** -0.5)).reshape(1, N_GROUPS * A_WIDTH)
    gk_t = jnp.tile(gk.astype(F32), (1, A_HEADS)).reshape(1, N_GROUPS * A_WIDTH)

    row = lambda w: pl.BlockSpec((tm, w), lambda i: (i, 0))
    rowT = lambda r: pl.BlockSpec((r, tm), lambda i: (0, i))
    full = lambda a: pl.BlockSpec(a.shape, lambda i: (0,) * a.ndim)
    dils = [d for _, d in DILATED_PATTERNS]
    res_shape = lambda d: jax.ShapeDtypeStruct((batch, d, seq // d, A_WIDTH), BF16)
    res_spec = lambda d: pl.BlockSpec((1, d, tm // d, A_WIDTH), lambda i: (i // steps, 0, i % steps, 0))
    out_shapes = (
        jax.ShapeDtypeStruct((n, M_QK_WIDTH), BF16),
        jax.ShapeDtypeStruct((M_QK_WIDTH, n), BF16),
        jax.ShapeDtypeStruct((n, M_WIDTH), BF16),
        jax.ShapeDtypeStruct((n, M_WIDTH), BF16),
        jax.ShapeDtypeStruct((8, n), F32),
        jax.ShapeDtypeStruct((8, n), F32),
        *[res_shape(d) for d in dils], *[res_shape(d) for d in dils], *[res_shape(d) for d in dils],
        jax.ShapeDtypeStruct((n, D_MODEL), BF16),
        jax.ShapeDtypeStruct((n, D_MODEL), BF16),
    )
    out_specs = (row(M_QK_WIDTH), rowT(M_QK_WIDTH), row(M_WIDTH), row(M_WIDTH), rowT(8), rowT(8),
                 *[res_spec(d) for d in dils], *[res_spec(d) for d in dils], *[res_spec(d) for d in dils],
                 row(D_MODEL), row(D_MODEL))
    outs = pl.pallas_call(
        _inproj_body,
        grid=(n // tm,),
        in_specs=[row(D_MODEL), full(g1), full(wa), full(wg), full(wb), full(gb), full(gq_t), full(gk_t)],
        out_specs=out_specs,
        out_shape=out_shapes,
        scratch_shapes=[pltpu.VMEM((2, 2, tm, LANES), F32)],
        compiler_params=_params("parallel"),
        name="inproj",
    )(x2d, g1, wa, wg, wb, gb, gq_t, gk_t)
    mq, kT, mv, so, gi, gf = outs[:6]
    aq, ak, av = outs[6:9], outs[9:12], outs[12:15]
    return mq, kT, mv, so, gi, gf, aq, ak, av, outs[15], outs[16]


M_CHUNK_LEN = 128


def _mlstm_body(q_ref, v_ref, so_ref, ng_ref, *rest, nchunk, nseq):
    kT_refs, gi_refs, gf_refs = rest[0:nseq], rest[nseq:2 * nseq], rest[2 * nseq:3 * nseq]
    o_ref, c_ref, m_ref = rest[3 * nseq:]
    L = M_CHUNK_LEN

    @pl.when(pl.program_id(0) == 0)
    def _():
        c_ref[...] = jnp.zeros_like(c_ref)
        m_ref[...] = jnp.zeros_like(m_ref)

    lane8 = lax.broadcasted_iota(I32, (8, L), 1)
    causal = lax.broadcasted_iota(I32, (L, L), 1) <= lax.broadcasted_iota(I32, (L, L), 0)
    lo_half = lax.broadcasted_iota(I32, (L, LANES), 1) < M_QK_DIM
    ones = jnp.ones((L, M_V_DIM), BF16)

    heads = range(M_HEADS)
    cstate = [[c_ref[s, h * M_QK_DIM:(h + 1) * M_QK_DIM, :] for h in heads] for s in range(nseq)]
    m_prev = [m_ref[s, :, 0:1] for s in range(nseq)]
    units = []
    for c in range(nchunk):
        rows = slice(c * L, (c + 1) * L)
        for s in range(nseq):
            gi = gi_refs[s][:, rows]
            b = gf_refs[s][:, rows]
            sh = 1
            while sh < L:
                b = b + jnp.where(lane8 >= sh, pltpu.roll(b, sh, 1), 0.0)
                sh *= 2
            u = gi - b
            g = b[:, L - 1:L]
            a = g + u
            amax = jnp.max(a, axis=1, keepdims=True)
            m_new = jnp.maximum(g + m_prev[s], amax)
            w = jnp.exp(a - m_new) * (M_QK_DIM ** -0.5)
            s_old = jnp.exp(g + m_prev[s] - m_new)
            vext = [jnp.concatenate([v_ref[s, rows, h * M_V_DIM:(h + 1) * M_V_DIM], ones], axis=1) for h in heads]
            cloc = []
            for h in heads:
                hr = slice(h * M_QK_DIM, (h + 1) * M_QK_DIM)
                kw = (kT_refs[s][hr, rows].astype(F32) * w[h:h + 1, :]).astype(BF16)
                cloc.append(jnp.dot(kw, vext[h], preferred_element_type=F32))
            units.append(dict(seq=s, rows=rows, b=b, u=u, m_prev=m_prev[s], state=cstate[s], vext=vext))
            cstate[s] = [s_old[h:h + 1, :] * cstate[s][h] + cloc[h] for h in heads]
            m_prev[s] = m_new
    for s in range(nseq):
        for h in heads:
            c_ref[s, h * M_QK_DIM:(h + 1) * M_QK_DIM, :] = cstate[s][h]
        m_ref[s] = jnp.broadcast_to(m_prev[s], (8, LANES))

    for un in units:
        s, rows = un["seq"], un["rows"]
        un["s"], un["qc"] = [], []
        for p in range(M_HEADS // 2):
            lanes_p = slice(p * LANES, (p + 1) * LANES)
            q_pair = q_ref[s, rows, lanes_p]
            kT_pair = kT_refs[s][lanes_p, rows]
            c_pair = jnp.concatenate([un["state"][2 * p], un["state"][2 * p + 1]], axis=0).astype(BF16)
            for hh in range(2):
                qm = jnp.where(lo_half if hh == 0 else jnp.logical_not(lo_half), q_pair, jnp.zeros_like(q_pair))
                un["s"].append(jnp.dot(qm, kT_pair, preferred_element_type=F32) * (M_QK_DIM ** -0.5))
                un["qc"].append(jnp.dot(qm, c_pair, preferred_element_type=F32))

    for un in units:
        s, rows, b, u, mp = un["seq"], un["rows"], un["b"], un["u"], un["m_prev"]
        for h in heads:
            hl = slice(h * M_V_DIM, (h + 1) * M_V_DIM)
            bcol = jnp.transpose(jnp.broadcast_to(b[h:h + 1, :], (L, L)))
            dm = jnp.where(causal, bcol + u[h:h + 1, :], -jnp.inf)
            inter = bcol + mp[h:h + 1, :]
            m_t = jnp.maximum(inter, jnp.max(dm, axis=1, keepdims=True))
            pmat = (un["s"][h] * jnp.exp(dm - m_t)).astype(BF16)
            sc = jnp.exp(inter - m_t)
            out = (jnp.dot(pmat, un["vext"][h], preferred_element_type=F32)
                   + jnp.concatenate([sc, sc], axis=1) * un["qc"][h])
            hv = out[:, :M_V_DIM] / jnp.maximum(jnp.abs(out[:, M_V_DIM:]), jnp.exp(-m_t))
            hn = hv * lax.rsqrt(jnp.mean(hv * hv, axis=1, keepdims=True) + EPS)
            hn = hn * ng_ref[:, hl] * so_ref[s, rows, hl].astype(F32)
            o_ref[s, rows, hl] = hn.astype(BF16)


def _stage_mlstm(mq, kT, mv, so, gi, gf, norm_g, batch, seq, rows_per_step):
    n = batch * seq
    R = rows_per_step
    steps = seq // R
    ng = norm_g.astype(F32).reshape(1, M_WIDTH)
    per_seq = lambda a: a.reshape(batch, seq, a.shape[1])
    row = lambda w: pl.BlockSpec((batch, R, w), lambda i: (0, i, 0))
    colT = lambda r, s: pl.BlockSpec((r, R), lambda i, s=s: (0, s * steps + i))
    seqs = range(batch)
    out = pl.pallas_call(
        functools.partial(_mlstm_body, nchunk=R // M_CHUNK_LEN, nseq=batch),
        grid=(steps,),
        in_specs=[row(M_QK_WIDTH), row(M_WIDTH), row(M_WIDTH), pl.BlockSpec((1, M_WIDTH), lambda i: (0, 0)),
                  *[colT(M_QK_WIDTH, s) for s in seqs], *[colT(8, s) for s in seqs], *[colT(8, s) for s in seqs]],
        out_specs=row(M_WIDTH),
        out_shape=jax.ShapeDtypeStruct((batch, seq, M_WIDTH), BF16),
        scratch_shapes=[pltpu.VMEM((batch, M_QK_WIDTH, 2 * M_V_DIM), F32), pltpu.VMEM((batch, 8, LANES), F32)],
        compiler_params=_params("arbitrary"),
        name="mlstm",
    )(per_seq(mq), per_seq(mv), per_seq(so), ng, *[kT] * batch, *[gi] * batch, *[gf] * batch)
    return out.reshape(n, M_WIDTH)


LSE_LANES = LANES // A_HEADS


def _attn_body(q_ref, kp_ref, kc_ref, vp_ref, vc_ref, o_ref, lse_ref, *, dil, slopes, lq):
    QB = N_BACK
    first = pl.program_id(2) == 0
    qn = q_ref[0, 0]
    kcn = kc_ref[0, 0]
    kpn = kp_ref[0, 0]
    vc = vc_ref[0, 0]
    vp = vp_ref[0, 0]

    qi = lax.broadcasted_iota(I32, (QB, 2 * QB), 0)
    kj = lax.broadcasted_iota(I32, (QB, 2 * QB), 1)
    dist = qi + QB - kj
    band = jnp.logical_and(dist >= 0, dist <= N_BACK)
    distf = (dist * dil).astype(F32)
    bias = [jnp.where(band, -float(slopes[h]) * distf, -jnp.inf) for h in range(A_HEADS)]
    no_prev = jnp.logical_and(first, kj < QB)
    lo_half = lax.broadcasted_iota(I32, (QB, LANES), 1) < A_HEAD_DIM
    ones = jnp.ones((2 * QB, LANES), BF16)

    units = []
    for j in range(lq // QB):
        rows = slice(j * QB, (j + 1) * QB)
        prow = slice((j - 1) * QB, j * QB)
        keys = jnp.concatenate([kpn if j == 0 else kcn[prow], kcn[rows]], axis=0)
        vals = jnp.concatenate([vp if j == 0 else vc[prow], vc[rows]], axis=0)
        for p in range(A_HEADS // 2):
            lanes_p = slice(p * LANES, (p + 1) * LANES)
            q_pair = qn[rows, lanes_p]
            k_pair = keys[:, lanes_p]
            vext = jnp.concatenate([vals[:, lanes_p], ones], axis=1)
            scores = []
            for hh in range(2):
                sel = lo_half if hh == 0 else jnp.logical_not(lo_half)
                qm = jnp.where(sel, q_pair, jnp.zeros_like(q_pair))
                scores.append(lax.dot_general(qm, k_pair, _NT, preferred_element_type=F32))
            units.append((j, rows, lanes_p, p, vext, scores))

    lane = lax.broadcasted_iota(I32, (QB, LANES), 1)
    den_all = m_all = None
    for j, rows, lanes_p, p, vext, scores in units:
        o_pair = None
        for hh in range(2):
            s = scores[hh] + bias[2 * p + hh]
            if j == 0:
                s = jnp.where(no_prev, -jnp.inf, s)
            m = jnp.max(s, axis=1, keepdims=True)
            pv = jnp.dot(jnp.exp(s - m).astype(BF16), vext, preferred_element_type=F32)
            den = pv[:, LANES:]
            o_h = pv[:, :LANES] / den
            o_pair = o_h if hh == 0 else jnp.where(lo_half, o_pair, o_h)
            head = 2 * p + hh
            earlier = lane < head * LSE_LANES
            den_all = den if head == 0 else jnp.where(earlier, den_all, den)
            m_all = jnp.broadcast_to(m, den.shape) if head == 0 else jnp.where(earlier, m_all, m)
        o_ref[0, 0, rows, lanes_p] = o_pair.astype(BF16)
        if p == A_HEADS // 2 - 1:
            lse_ref[0, 0, rows, :] = m_all + jnp.log(den_all)


def _stage_attn(aq, ak, av, batch, seq, group):
    _, dil = DILATED_PATTERNS[group]
    L = seq // dil
    assert L % N_BACK == 0
    lq = min(2048, L)
    nq = L // lq
    sub = lq // N_BACK
    cur = pl.BlockSpec((1, 1, lq, A_WIDTH), lambda b, r, i: (b, r, i, 0))
    prev = pl.BlockSpec((1, 1, N_BACK, A_WIDTH), lambda b, r, i: (b, r, jnp.maximum(i * sub - 1, 0), 0))
    cur_lse = pl.BlockSpec((1, 1, lq, LANES), lambda b, r, i: (b, r, i, 0))
    return pl.pallas_call(
        functools.partial(_attn_body, dil=dil, slopes=tuple(_alibi_slopes()[group]), lq=lq),
        grid=(batch, dil, nq),
        in_specs=[cur, prev, cur, prev, cur],
        out_specs=(cur, cur_lse),
        out_shape=(jax.ShapeDtypeStruct((batch, dil, L, A_WIDTH), BF16),
                   jax.ShapeDtypeStruct((batch, dil, L, LANES), F32)),
        compiler_params=_params("parallel", "parallel", "parallel"),
        name=f"dilated_attn_d{dil}",
    )(aq, ak, ak, av, av)


PACK_ROWS = D_MODEL // (2 * LANES)
U32 = jnp.uint32
_HIGH_HALF = 0xFFFF0000


def _pack_rows(val, exact=False):
    half = D_MODEL // 2
    lo, hi = val[:, :half], val[:, half:]
    if exact:
        return (lax.bitcast_convert_type(lo, U32) >> 16) | lax.bitcast_convert_type(hi, U32)

    def bits(v):
        return lax.bitcast_convert_type(v.astype(BF16).astype(F32), U32)

    return (bits(lo) >> 16) | (bits(hi) & U32(_HIGH_HALF))


def _unpack_rows(words):
    lo = lax.bitcast_convert_type(words << 16, F32).astype(BF16)
    hi = lax.bitcast_convert_type(words & U32(_HIGH_HALF), F32).astype(BF16)
    return jnp.concatenate([lo, hi], axis=1)


def _flat(ref):
    rows = 1
    for d in ref.shape[:-2]:
        rows *= d
    return ref.reshape(rows * PACK_ROWS, LANES)


def _store_packed(flat_ref, row0, words):
    t = words.shape[0]
    for s in range(PACK_ROWS):
        flat_ref[pl.ds(row0 * PACK_ROWS + s, t, stride=PACK_ROWS), :] = words[:, s * LANES:(s + 1) * LANES]


def _load_packed(flat_ref, row0, t):
    return jnp.concatenate([flat_ref[pl.ds(row0 * PACK_ROWS + s, t, stride=PACK_ROWS), :] for s in range(PACK_ROWS)],
                           axis=1)


def _rows8(vals):
    t = vals[0].shape[1]
    rid = lax.broadcasted_iota(I32, (8, t), 0)
    out = jnp.zeros((8, t), vals[0].dtype)
    for k, v in enumerate(vals):
        out = jnp.where(rid == k, jnp.broadcast_to(v, (8, t)), out)
    return out


def _head_lanes(c):
    lane = lax.broadcasted_iota(I32, c.shape, 1)
    r1, r2, r3 = [pltpu.roll(c, k * LSE_LANES, 1) for k in (1, 2, 3)]
    left = jnp.where(lane < LSE_LANES, c, jnp.where(lane < 3 * LSE_LANES, r1, r2))
    right = jnp.where(lane < LSE_LANES, r2, jnp.where(lane < 3 * LSE_LANES, r3, c))
    return jnp.concatenate([left, right], axis=1)


X_RING = 3


def _merge_body(hm_ref, o1_ref, o2_ref, o3_ref, l1_ref, l2_ref, l3_ref, sgm_ref, sga_ref, x_hbm,
                wm_ref, wa_ref, wo_ref, g2_ref, wrh_ref, br_ref,
                x2_ref, xn_ref, loc_ref, gate_ref, tcnt_ref, tcar_ref, cnt_ref, carry_ref, st_ref, logits_ref,
                xbuf_ref, xsem):
    step = pl.program_id(0)
    ntile = pl.num_programs(0) - 1
    tm = xbuf_ref.shape[1]

    def x_copy(tile):
        slot = tile % X_RING
        return pltpu.make_async_copy(x_hbm.at[pl.ds(tile * tm, tm)], xbuf_ref.at[slot], xsem.at[slot])

    @pl.when(step == 0)
    def _():
        carry_ref[...] = jnp.zeros_like(carry_ref)
        logits_ref[...] = jnp.zeros_like(logits_ref)
        for t in range(X_RING - 1):
            x_copy(t).start()

    @pl.when(step + X_RING - 1 < ntile)
    def _():
        x_copy(step + X_RING - 1).start()

    @pl.when(step < ntile)
    def _():
        x_copy(step).wait()

    m_branch = jnp.dot(hm_ref[...], wm_ref[...], preferred_element_type=F32)
    dils = [d for _, d in DILATED_PATTERNS]
    l1, l2, l3 = [_merge_residues(r, d, st_ref) for r, d in zip((l1_ref, l2_ref, l3_ref), dils)]
    lmax = jnp.maximum(jnp.maximum(l1, l2), l3)
    e1, e2, e3 = jnp.exp(l1 - lmax), jnp.exp(l2 - lmax), jnp.exp(l3 - lmax)
    inv = 1.0 / (e1 + e2 + e3)
    h_a = _head_lanes(e1 * inv) * _merge_residues(o1_ref, dils[0], st_ref)
    h_a = h_a + _head_lanes(e2 * inv) * _merge_residues(o2_ref, dils[1], st_ref)
    h_a = h_a + _head_lanes(e3 * inv) * _merge_residues(o3_ref, dils[2], st_ref)
    y = (sgm_ref[...].astype(F32) * m_branch
         + sga_ref[...].astype(F32) * jnp.dot(h_a.astype(BF16), wa_ref[...], preferred_element_type=F32))
    x2 = xbuf_ref[jnp.minimum(step, ntile - 1) % X_RING] + jnp.dot(y.astype(BF16), wo_ref[...], preferred_element_type=F32)
    x2_ref[...] = x2
    xn = x2 * lax.rsqrt(jnp.mean(x2 * x2, axis=-1, keepdims=True) + EPS) * g2_ref[...]
    xh = xn.astype(BF16)
    xn_ref[...] = xh

    logits = logits_ref[...]
    t = logits.shape[1]
    eid = lax.broadcasted_iota(I32, (N_EXPERTS, t), 0).astype(F32)
    vals = logits
    top_v, top_i = [], []
    for _ in range(TOP_K):
        mx = jnp.max(vals, axis=0, keepdims=True)
        ik = jnp.min(jnp.where(vals == mx, eid, float(N_EXPERTS)), axis=0, keepdims=True)
        top_v.append(mx)
        top_i.append(ik)
        vals = jnp.where(eid == ik, -jnp.inf, vals)
    ex = [jnp.exp(v - top_v[0]) for v in top_v]
    den = ex[0] + ex[1] + ex[2] + ex[3]
    gate_ref[...] = _rows8([e / den for e in ex])

    chosen = jnp.zeros((N_EXPERTS, t), F32)
    for ik in top_i:
        chosen = chosen + (eid == ik).astype(F32)
    before = jnp.where(lax.broadcasted_iota(jnp.int16, (t, t), 0) < lax.broadcasted_iota(jnp.int16, (t, t), 1),
                       jnp.ones((), BF16), jnp.zeros((), BF16))
    prefix = jnp.dot(chosen.astype(BF16), before, preferred_element_type=F32)
    tcount = jnp.broadcast_to(jnp.sum(chosen, axis=1, keepdims=True), (N_EXPERTS, LANES))
    tcount = jnp.where(step > 0, tcount, 0.0)
    below = (lax.broadcasted_iota(I32, (N_EXPERTS, N_EXPERTS), 1)
             < lax.broadcasted_iota(I32, (N_EXPERTS, N_EXPERTS), 0)).astype(BF16)
    t_hi = jnp.floor(tcount * (1.0 / BF16_EXACT_INT)) * BF16_EXACT_INT
    tile_off = (jnp.dot(below, t_hi.astype(BF16), preferred_element_type=F32)
                + jnp.dot(below, (tcount - t_hi).astype(BF16), preferred_element_type=F32))
    pos = prefix + tile_off[:, 0:1]
    loc_ref[...] = _rows8([jnp.sum(jnp.where(eid == ik, pos, 0.0), axis=0, keepdims=True).astype(I32)
                           for ik in top_i])
    carry = carry_ref[...]
    tcnt_ref[...] = tcount.astype(I32)
    tcar_ref[...] = carry.astype(I32)
    total = carry + tcount
    carry_ref[...] = total
    cnt_ref[...] = total

    logits_ref[...] = lax.dot_general(wrh_ref[...], xh, _NT, preferred_element_type=F32) + br_ref[...]


def _stage_merge(h_m, attn, sgm, sga, x2d, w_mb, w_ab, w_out, norm2_g, w_router, b_router, batch, seq, tm):
    n = x2d.shape[0]
    steps = seq // tm
    assert n // tm >= X_RING - 1
    (o1, l1), (o2, l2), (o3, l3) = attn
    wm = w_mb.astype(BF16)
    wa = w_ab.astype(BF16)
    wo = w_out.astype(BF16)
    g2 = norm2_g.astype(F32).reshape(1, D_MODEL)
    wrh = w_router.astype(BF16).T
    br = b_router.astype(F32).reshape(N_EXPERTS, 1)
    mixed = lambda i: jnp.minimum(i, n // tm - 1)
    routed = lambda i: jnp.maximum(i - 1, 0)
    row = lambda w: pl.BlockSpec((tm, w), lambda i: (mixed(i), 0))
    rowT = lambda r: pl.BlockSpec((r, tm), lambda i: (0, routed(i)))
    full = lambda a: pl.BlockSpec(a.shape, lambda i: (0,) * a.ndim)
    res = lambda d, w: pl.BlockSpec((1, d, tm // d, w), lambda i: (mixed(i) // steps, 0, mixed(i) % steps, 0))
    dils = [d for _, d in DILATED_PATTERNS]
    per_tile = pl.BlockSpec((N_EXPERTS, LANES), lambda i: (0, routed(i)))
    return pl.pallas_call(
        _merge_body,
        grid=(n // tm + 1,),
        in_specs=[row(M_WIDTH), *[res(d, A_WIDTH) for d in dils], *[res(d, LANES) for d in dils],
                  row(D_MODEL), row(D_MODEL), pl.BlockSpec(memory_space=pl.ANY),
                  full(wm), full(wa), full(wo), full(g2), full(wrh), full(br)],
        out_specs=(row(D_MODEL), row(D_MODEL), rowT(8), rowT(8), per_tile, per_tile,
                   pl.BlockSpec((N_EXPERTS, LANES), lambda i: (0, 0))),
        out_shape=(jax.ShapeDtypeStruct((n, D_MODEL), F32),
                   jax.ShapeDtypeStruct((n, D_MODEL), BF16),
                   jax.ShapeDtypeStruct((8, n), I32),
                   jax.ShapeDtypeStruct((8, n), F32),
                   jax.ShapeDtypeStruct((N_EXPERTS, (n // tm) * LANES), I32),
                   jax.ShapeDtypeStruct((N_EXPERTS, (n // tm) * LANES), I32),
                   jax.ShapeDtypeStruct((N_EXPERTS, LANES), F32)),
        scratch_shapes=[pltpu.VMEM((N_EXPERTS, LANES), F32), pltpu.VMEM((2, 2, tm, LANES), F32),
                        pltpu.VMEM((N_EXPERTS, tm), F32),
                        pltpu.VMEM((X_RING, tm, D_MODEL), F32), pltpu.SemaphoreType.DMA((X_RING,))],
        compiler_params=_params("arbitrary"),
        name="merge_route",
    )(h_m, o1, o2, o3, l1, l2, l3, sgm, sga, x2d, wm, wa, wo, g2, wrh, br)


def _offsets_body(cnt_ref, blk_ref, pstart_ref, zlo_ref, zhi_ref, *, nblk_pad):
    cnt = cnt_ref[...]
    padded = jnp.floor((cnt + (MOE_BLOCK - 1)) * (1.0 / MOE_BLOCK)) * MOE_BLOCK
    lower = (lax.broadcasted_iota(I32, (N_EXPERTS, N_EXPERTS), 1)
             <= lax.broadcasted_iota(I32, (N_EXPERTS, N_EXPERTS), 0)).astype(BF16)
    nb = padded * (1.0 / MOE_BLOCK)
    nb_hi = jnp.floor(nb * (1.0 / BF16_EXACT_INT)) * BF16_EXACT_INT
    pends = (jnp.dot(lower, nb_hi.astype(BF16), preferred_element_type=F32)
             + jnp.dot(lower, (nb - nb_hi).astype(BF16), preferred_element_type=F32)) * MOE_BLOCK
    pstart = pends - padded
    pstart_ref[...] = pstart.astype(I32)
    zlo_ref[...] = (pstart + cnt).astype(I32)
    zhi_ref[...] = pends.astype(I32)

    first_row = (lax.broadcasted_iota(I32, (N_EXPERTS, nblk_pad), 1) * MOE_BLOCK).astype(F32)
    pe = jnp.broadcast_to(pends[:, 0:1], (N_EXPERTS, nblk_pad))
    be = jnp.sum((pe <= first_row).astype(F32), axis=0, keepdims=True)
    be = jnp.minimum(be, float(N_EXPERTS - 1))
    nused = pends[N_EXPERTS - 1:N_EXPERTS, 0:1] * (1.0 / MOE_BLOCK)
    nonempty = jnp.broadcast_to(padded[:, 0:1], (N_EXPERTS, nblk_pad)) > 0.0
    runidx = jnp.sum(jnp.logical_and(pe <= first_row, nonempty).astype(F32), axis=0, keepdims=True)
    parity = runidx - 2.0 * jnp.floor(runidx * 0.5)
    eid = lax.broadcasted_iota(I32, (N_EXPERTS, nblk_pad), 0).astype(F32)
    later = jnp.logical_and(eid > be, nonempty)
    nxt = jnp.min(jnp.where(later, eid, float(N_EXPERTS)), axis=0, keepdims=True)
    blk_ref[...] = _rows8([be.astype(I32), jnp.broadcast_to(nused, (1, nblk_pad)).astype(I32),
                           parity.astype(I32), nxt.astype(I32)])


def _stage_offsets(cnt, nblk):
    nblk_pad = -(-nblk // LANES) * LANES
    const = lambda r, c: pl.BlockSpec((r, c), lambda i: (0, 0))
    per_expert = jax.ShapeDtypeStruct((N_EXPERTS, LANES), I32)
    return pl.pallas_call(
        functools.partial(_offsets_body, nblk_pad=nblk_pad),
        grid=(1,),
        in_specs=[const(N_EXPERTS, LANES)],
        out_specs=(const(8, nblk_pad), const(N_EXPERTS, LANES), const(N_EXPERTS, LANES), const(N_EXPERTS, LANES)),
        out_shape=(jax.ShapeDtypeStruct((8, nblk_pad), I32), per_expert, per_expert, per_expert),
        compiler_params=_params("arbitrary"),
        name="route_offsets",
    )(cnt)


RUN_BITS = 10


def _tile_rows(ref, first_row, nrows):
    return ref.at[pl.ds(first_row, nrows)]


def _for_each_piece(length, fn):
    for b in reversed(range(RUN_BITS)):
        @pl.when(((length >> b) & 1) == 1)
        def _(b=b):
            fn((length >> (b + 1)) << (b + 1), 1 << b)


def _for_each_run(tile, tcnt_ref, tcar_ref, pstart_ref, fn):
    def per_expert(e, local):
        count = tcnt_ref[tile, e]
        first = pstart_ref[e] + tcar_ref[tile, e]
        _for_each_piece(count, lambda off, size: fn(local + off, first + off, size))
        return local + count

    lax.fori_loop(0, N_EXPERTS, per_expert, 0)


PERM_CHUNK = 256


def _dispatch_body(tcnt_ref, tcar_ref, pstart_ref, zlo_ref, zhi_ref, loc_ref, xn_ref, xs_hbm,
                   buf_ref, sems, *, tm):
    step = pl.program_id(0)
    nloc = TOP_K * tm

    def wait_buffer(slot):
        pltpu.make_async_copy(buf_ref.at[slot], xs_hbm.at[pl.ds(0, nloc)], sems.at[slot]).wait()

    for slot in range(2):
        tile = 2 * step + slot

        @pl.when(step > 0)
        def _(slot=slot):
            wait_buffer(slot)

        loc = loc_ref[:, slot * tm:(slot + 1) * tm].astype(jnp.int16)
        xn = xn_ref[slot * tm:(slot + 1) * tm, :]
        for c in range(TOP_K * tm // PERM_CHUNK):
            lid = lax.broadcasted_iota(jnp.int16, (PERM_CHUNK, tm), 0) + jnp.int16(c * PERM_CHUNK)
            hit = lid == loc[0:1, :]
            for k in range(1, TOP_K):
                hit = jnp.logical_or(hit, lid == loc[k:k + 1, :])
            perm = jnp.where(hit, jnp.ones((), BF16), jnp.zeros((), BF16))
            rows = jnp.dot(perm, xn, preferred_element_type=F32)
            _store_packed(_flat(buf_ref), slot * nloc + c * PERM_CHUNK, _pack_rows(rows, exact=True))

        def run_copy(local, first, size, slot=slot):
            return pltpu.make_async_copy(_tile_rows(buf_ref.at[slot], local, size),
                                         _tile_rows(xs_hbm, first, size), sems.at[slot])

        _for_each_run(tile, tcnt_ref, tcar_ref, pstart_ref, lambda l, f, s: run_copy(l, f, s).start())

    @pl.when(step == pl.num_programs(0) - 1)
    def _():
        wait_buffer(0)
        wait_buffer(1)
        zsrc = buf_ref.at[0]
        zsrc[pl.ds(0, MOE_BLOCK)] = jnp.zeros((MOE_BLOCK, PACK_ROWS, LANES), U32)

        def zero_copy(first, size):
            return pltpu.make_async_copy(_tile_rows(zsrc, 0, size), _tile_rows(xs_hbm, first, size), sems.at[0])

        first_unused = zhi_ref[N_EXPERTS - 1] // MOE_BLOCK
        nblk = xs_hbm.shape[0] // MOE_BLOCK

        def for_each_zero_copy(action):
            def per_expert(e, carry):
                lo = zlo_ref[e]
                _for_each_piece(zhi_ref[e] - lo, lambda off, size: action(zero_copy(lo + off, size)))
                return carry

            def tail(blk, carry):
                action(zero_copy(blk * MOE_BLOCK, MOE_BLOCK))
                return carry

            lax.fori_loop(0, N_EXPERTS, per_expert, 0)
            lax.fori_loop(first_unused, nblk, tail, 0)

        for_each_zero_copy(lambda copy: copy.start())
        for_each_zero_copy(lambda copy: copy.wait())


def _stage_dispatch(tables, loc8, xn, nrows, tm):
    n = xn.shape[0]
    assert TOP_K * tm >= MOE_BLOCK and (n // tm) % 2 == 0
    grid_spec = pltpu.PrefetchScalarGridSpec(
        num_scalar_prefetch=5,
        grid=(n // (2 * tm),),
        in_specs=[pl.BlockSpec((8, 2 * tm), lambda i, *_: (0, i)),
                  pl.BlockSpec((2 * tm, D_MODEL), lambda i, *_: (i, 0))],
        out_specs=pl.BlockSpec(memory_space=pl.ANY),
        scratch_shapes=[pltpu.VMEM((2, TOP_K * tm, PACK_ROWS, LANES), U32), pltpu.SemaphoreType.DMA((2,))],
    )
    return pl.pallas_call(
        functools.partial(_dispatch_body, tm=tm),
        grid_spec=grid_spec,
        out_shape=jax.ShapeDtypeStruct((nrows, PACK_ROWS, LANES), U32),
        compiler_params=_params("arbitrary"),
        name="dispatch",
    )(*tables, loc8, xn)


EXPERT_BLOCKS_PER_STEP = 2


def _expert_body(be_ref, nu_ref, par_ref, nxt_ref, xs_ref, w1_hbm, b1_ref, w2_hbm, b2_ref, ys_ref,
                 w1f_ref, w2f_ref, w1b_ref, w2b_ref, sems):
    def fetch(expert, slot):
        return (pltpu.make_async_copy(w1_hbm.at[expert], w1f_ref.at[slot], sems.at[slot, 0]),
                pltpu.make_async_copy(w2_hbm.at[expert], w2f_ref.at[slot], sems.at[slot, 1]))

    def block_info(sub):
        j = pl.program_id(0) * EXPERT_BLOCKS_PER_STEP + sub
        used = j < nu_ref[0]
        jj = jnp.maximum(jnp.minimum(j, nu_ref[0] - 1), 0)
        e = be_ref[jj]
        fresh = jnp.logical_or(j == 0, e != be_ref[jnp.maximum(jj - 1, 0)])
        return j, jj, e, used, fresh

    def refresh(j, jj, e):
        slot = par_ref[jj]

        @pl.when(j == 0)
        def _():
            for c in fetch(e, slot):
                c.start()

        for c in fetch(e, slot):
            c.wait()
        nxt = nxt_ref[jj]

        @pl.when(nxt < N_EXPERTS)
        def _():
            for c in fetch(nxt, 1 - slot):
                c.start()

        w1b_ref[...] = w1f_ref[slot].astype(BF16)
        w2b_ref[...] = w2f_ref[slot].astype(BF16)

    def compute(e, row0, nrows):
        xb = _unpack_rows(_load_packed(_flat(xs_ref), row0, nrows))
        gu = jnp.dot(xb, w1b_ref[...], preferred_element_type=F32) + b1_ref[pl.ds(e, 1), :]
        gate = jnp.minimum(gu[:, :D_FF], SWIGLU_LIMIT)
        lin = jnp.clip(gu[:, D_FF:], -SWIGLU_LIMIT, SWIGLU_LIMIT)
        act = (lin + 1.0) * (gate * jax.nn.sigmoid(SWIGLU_ALPHA * gate))
        ys = jnp.dot(act.astype(BF16), w2b_ref[...], preferred_element_type=F32) + b2_ref[pl.ds(e, 1), :]
        _store_packed(_flat(ys_ref), row0, _pack_rows(ys))

    j0, jj0, e0, used0, fresh0 = block_info(0)
    j1, jj1, e1, used1, fresh1 = block_info(1)
    same = jnp.logical_and(used1, e1 == e0)

    @pl.when(jnp.logical_and(used0, fresh0))
    def _():
        refresh(j0, jj0, e0)

    @pl.when(jnp.logical_and(used0, same))
    def _():
        compute(e0, 0, 2 * MOE_BLOCK)

    @pl.when(jnp.logical_and(used0, jnp.logical_not(same)))
    def _():
        compute(e0, 0, MOE_BLOCK)

    @pl.when(jnp.logical_and(used1, fresh1))
    def _():
        refresh(j1, jj1, e1)

    @pl.when(jnp.logical_and(used1, jnp.logical_not(same)))
    def _():
        compute(e1, MOE_BLOCK, MOE_BLOCK)

    for sub, used in ((0, used0), (1, used1)):
        @pl.when(jnp.logical_not(used))
        def _(sub=sub):
            ys_ref[pl.ds(sub * MOE_BLOCK, MOE_BLOCK)] = jnp.zeros((MOE_BLOCK, PACK_ROWS, LANES), U32)


def _stage_experts(blk8, xs, w1, b1, w2, b2):
    nrows = xs.shape[0]
    nblk = nrows // MOE_BLOCK
    assert EXPERT_BLOCKS_PER_STEP == 2 and nblk % EXPERT_BLOCKS_PER_STEP == 0
    block_e, nused, parity, nxt = blk8[0, :nblk], blk8[1, :1], blk8[2, :nblk], blk8[3, :nblk]
    tiles = (EXPERT_BLOCKS_PER_STEP * MOE_BLOCK, PACK_ROWS, LANES)
    full = lambda a: pl.BlockSpec(a.shape, lambda j, *_: (0,) * a.ndim)
    grid_spec = pltpu.PrefetchScalarGridSpec(
        num_scalar_prefetch=4,
        grid=(nblk // EXPERT_BLOCKS_PER_STEP,),
        in_specs=[pl.BlockSpec(tiles, lambda j, *_: (j, 0, 0)),
                  pl.BlockSpec(memory_space=pl.ANY), full(b1),
                  pl.BlockSpec(memory_space=pl.ANY), full(b2)],
        out_specs=pl.BlockSpec(tiles, lambda j, *_: (j, 0, 0)),
        scratch_shapes=[pltpu.VMEM((2, D_MODEL, 2 * D_FF), F32), pltpu.VMEM((2, D_FF, D_MODEL), F32),
                        pltpu.VMEM((D_MODEL, 2 * D_FF), BF16), pltpu.VMEM((D_FF, D_MODEL), BF16),
                        pltpu.SemaphoreType.DMA((2, 2))],
    )
    return pl.pallas_call(
        _expert_body,
        grid_spec=grid_spec,
        out_shape=jax.ShapeDtypeStruct((nrows, PACK_ROWS, LANES), U32),
        compiler_params=_params("arbitrary"),
        name="experts",
    )(block_e, nused, parity, nxt, xs, w1, b1, w2, b2)


COMBINE_CHUNK = 512


def _combine_body(tcnt_ref, tcar_ref, pstart_ref, loc_ref, gate_ref, x2_ref, ys_hbm, out_ref,
                  buf_ref, g_ref, sems, *, tm):
    step = pl.program_id(0)
    nloc = TOP_K * tm

    def start_runs(tile, slot):
        def run_copy(local, first, size):
            return pltpu.make_async_copy(_tile_rows(ys_hbm, first, size),
                                         _tile_rows(buf_ref.at[slot], local, size), sems.at[slot])
        _for_each_run(tile, tcnt_ref, tcar_ref, pstart_ref, lambda l, f, s: run_copy(l, f, s).start())

    def wait_buffer(slot):
        pltpu.make_async_copy(ys_hbm.at[pl.ds(0, nloc)], buf_ref.at[slot], sems.at[slot]).wait()

    def combine(slot):
        zpad = jnp.zeros((LANES - 16, LANES), F32)
        cols = []
        for c in range(tm // LANES):
            cols_in = slice(slot * tm + c * LANES, slot * tm + (c + 1) * LANES)
            cols.append(jnp.transpose(jnp.concatenate([loc_ref[:, cols_in].astype(F32), gate_ref[:, cols_in], zpad],
                                                      axis=0)))
        rows = slice(slot * tm, (slot + 1) * tm)
        acc = x2_ref[rows, :]
        for lc in range(nloc // COMBINE_CHUNK):
            lane = lax.broadcasted_iota(jnp.int16, (LANES, COMBINE_CHUNK), 1) + jnp.int16(lc * COMBINE_CHUNK)
            for c in range(tm // LANES):
                g = jnp.zeros((LANES, COMBINE_CHUNK), BF16)
                for k in range(TOP_K):
                    g = jnp.where(lane == cols[c][:, k:k + 1].astype(jnp.int16), cols[c][:, 8 + k:9 + k].astype(BF16), g)
                g_ref[c * LANES:(c + 1) * LANES, lc * COMBINE_CHUNK:(lc + 1) * COMBINE_CHUNK] = g
            if lc == 0:
                wait_buffer(slot)
            ys = _unpack_rows(_load_packed(_flat(buf_ref), slot * nloc + lc * COMBINE_CHUNK, COMBINE_CHUNK))
            acc = acc + jnp.dot(g_ref[:, lc * COMBINE_CHUNK:(lc + 1) * COMBINE_CHUNK], ys, preferred_element_type=F32)
        out_ref[rows, :] = acc

    @pl.when(step == 0)
    def _():
        start_runs(0, 0)

    start_runs(2 * step + 1, 1)
    combine(0)

    @pl.when(step + 1 < pl.num_programs(0))
    def _():
        start_runs(2 * step + 2, 0)

    combine(1)


def _stage_combine(tables, loc8, gate8, x2, ys, tm):
    n = x2.shape[0]
    assert (n // tm) % 2 == 0
    grid_spec = pltpu.PrefetchScalarGridSpec(
        num_scalar_prefetch=3,
        grid=(n // (2 * tm),),
        in_specs=[pl.BlockSpec((8, 2 * tm), lambda i, *_: (0, i)),
                  pl.BlockSpec((8, 2 * tm), lambda i, *_: (0, i)),
                  pl.BlockSpec((2 * tm, D_MODEL), lambda i, *_: (i, 0)),
                  pl.BlockSpec(memory_space=pl.ANY)],
        out_specs=pl.BlockSpec((2 * tm, D_MODEL), lambda i, *_: (i, 0)),
        scratch_shapes=[pltpu.VMEM((2, TOP_K * tm, PACK_ROWS, LANES), U32),
                        pltpu.VMEM((tm, TOP_K * tm), BF16),
                        pltpu.SemaphoreType.DMA((2,))],
    )
    return pl.pallas_call(
        functools.partial(_combine_body, tm=tm),
        grid_spec=grid_spec,
        out_shape=jax.ShapeDtypeStruct((n, D_MODEL), F32),
        compiler_params=_params("arbitrary"),
        name="combine",
    )(*tables, loc8, gate8, x2, ys)


def _moe(x2, xn, loc8, gate8, tcnt, tcar, cnt, w1, b1, w2, b2, tm):
    n = x2.shape[0]
    ntile = n // tm
    nblk = -(-(n * TOP_K) // MOE_BLOCK) + N_EXPERTS
    blk8, pstart, zlo, zhi = _stage_offsets(cnt, nblk)
    per_tile = lambda a: a.reshape(N_EXPERTS, ntile, LANES)[:, :, 0].T
    tables = (per_tile(tcnt), per_tile(tcar), pstart[:, 0])
    xs = _stage_dispatch(tables + (zlo[:, 0], zhi[:, 0]), loc8, xn, nblk * MOE_BLOCK, tm)
    ys = _stage_experts(blk8, xs, w1, b1, w2, b2)
    return _stage_combine(tables, loc8, gate8, x2, ys, tm)


def kernel(x, norm1_g, w_in, mlstm_gate_b, mlstm_norm_g, attn_q_norm_g, attn_k_norm_g, w_mlstm_branch,
           w_attn_branch, w_out, norm2_g, w_router, b_router, w1, b1, w2, b2):
    batch, seq, _ = x.shape
    n = batch * seq
    for l in range(norm1_g.shape[0]):
        x2d = x.reshape(n, D_MODEL)
        tm = min(512, seq)
        mq, kT, mv, so, gi, gf, aq, ak, av, sgm, sga = _stage_inproj(
            x2d, norm1_g[l], w_in[l], mlstm_gate_b[l], attn_q_norm_g[l], attn_k_norm_g[l], batch, seq, tm)
        h_m = _stage_mlstm(mq, kT, mv, so, gi, gf, mlstm_norm_g[l], batch, seq, tm)
        attn = [_stage_attn(aq[g], ak[g], av[g], batch, seq, g)
                for g in range(N_GROUPS)]
        x2, xn, loc8, gate8, tcnt, tcar, cnt = _stage_merge(
            h_m, attn, sgm, sga, x2d, w_mlstm_branch[l], w_attn_branch[l], w_out[l], norm2_g[l],
            w_router[l], b_router[l], batch, seq, tm)
        out = _moe(x2, xn, loc8, gate8, tcnt, tcar, cnt, w1[l], b1[l], w2[l], b2[l], tm)
        x = out.reshape(batch, seq, D_MODEL)
    return x
```

```python
import functools

import numpy as np
import jax
import jax.numpy as jnp
from jax import lax
from jax.experimental import pallas as pl
from jax.experimental.pallas import tpu as pltpu

F32 = jnp.float32
BF16 = jnp.bfloat16
I32 = jnp.int32

D_MODEL = 1024
M_HEADS = 4
M_QK_DIM = 64
M_V_DIM = 128
GATE_SOFTCAP = 15.0
A_HEADS = 4
A_HEAD_DIM = 64
DILATED_PATTERNS = ((128, 1), (512, 4), (2048, 16))
N_GROUPS = len(DILATED_PATTERNS)
N_BACK = 128
N_EXPERTS = 32
TOP_K = 4
D_FF = 1024
SWIGLU_LIMIT = 7.0
SWIGLU_ALPHA = 1.702
MOE_BLOCK = 512
EPS = 1e-6

M_WIDTH = M_HEADS * M_V_DIM
M_QK_WIDTH = M_HEADS * M_QK_DIM
A_WIDTH = A_HEADS * A_HEAD_DIM
IN_SPLITS = (M_QK_WIDTH, M_QK_WIDTH, M_WIDTH, M_WIDTH, 2 * M_HEADS,
             N_GROUPS * A_WIDTH, N_GROUPS * A_WIDTH, N_GROUPS * A_WIDTH, D_MODEL, D_MODEL)

LANES = 128
VMEM_LIMIT = 56 * 1024 * 1024

BF16_EXACT_INT = 256.0

_NT = (((1,), (1,)), ((), ()))
_TN = (((0,), (1,)), ((), ()))


def _alibi_slopes():
    n = N_GROUPS * A_HEADS
    s = np.exp2(-8.0 * np.arange(1, n + 1) / n).astype(np.float32)
    return s.reshape(N_GROUPS, A_HEADS)


def _params(*sem):
    return pltpu.CompilerParams(dimension_semantics=sem, vmem_limit_bytes=VMEM_LIMIT)


def _log_sigmoid(x):
    return jnp.minimum(x, 0.0) - jnp.log1p(jnp.exp(-jnp.abs(x)))


_A_WIDTH = sum(IN_SPLITS[:4])
_B_START = _A_WIDTH + IN_SPLITS[4]


def _piece_segments(widths):
    bounds, start = [], 0
    for width in widths:
        bounds.append((start, start + width))
        start += width
    return bounds


_C_MQ, _C_MK, _C_MV, _C_MO = _piece_segments(IN_SPLITS[:4])
_C_AQ, _C_AK, _C_AV, _C_GM, _C_GA = _piece_segments(IN_SPLITS[5:])


def _split_residues(val, d, out_ref, st_ref):
    t = val.shape[0]
    if d == 1:
        out_ref[0, 0] = val.astype(out_ref.dtype)
        return
    a_ref = st_ref.at[0]
    a_ref[0] = val[:, :LANES]
    a_ref[1] = val[:, LANES:]
    m = t // d
    if d == 16:
        b_ref = st_ref.at[1]
        for r0 in range(4):
            b_ref[0, r0 * 4 * m:(r0 + 1) * 4 * m, :] = a_ref[0, pl.ds(r0, 4 * m, stride=4), :]
            b_ref[1, r0 * 4 * m:(r0 + 1) * 4 * m, :] = a_ref[1, pl.ds(r0, 4 * m, stride=4), :]
        for r0 in range(4):
            for r1 in range(4):
                piece = jnp.concatenate([b_ref[0, pl.ds(r0 * 4 * m + r1, m, stride=4), :],
                                         b_ref[1, pl.ds(r0 * 4 * m + r1, m, stride=4), :]], axis=1)
                out_ref[0, 4 * r1 + r0] = piece.astype(out_ref.dtype)
        return
    for r in range(d):
        piece = jnp.concatenate([a_ref[0, pl.ds(r, m, stride=d), :], a_ref[1, pl.ds(r, m, stride=d), :]], axis=1)
        out_ref[0, r] = piece.astype(out_ref.dtype)


def _merge_residues(ref, d, st_ref):
    if d == 1:
        return ref[0, 0].astype(F32)
    m = ref.shape[2]
    halves = range(ref.shape[3] // LANES)
    if d == 16:
        a_ref, b_ref = st_ref.at[0], st_ref.at[1]
        for r0 in range(4):
            for r1 in range(4):
                blk = ref[0, 4 * r1 + r0].astype(F32)
                for c in halves:
                    a_ref[c, pl.ds(r0 * 4 * m + r1, m, stride=4), :] = blk[:, c * LANES:(c + 1) * LANES]
        for r0 in range(4):
            for c in halves:
                b_ref[c, pl.ds(r0, 4 * m, stride=4), :] = a_ref[c, r0 * 4 * m:(r0 + 1) * 4 * m, :]
        return jnp.concatenate([b_ref[c] for c in halves], axis=1)
    a_ref = st_ref.at[0]
    for r in range(d):
        blk = ref[0, r].astype(F32)
        for c in halves:
            a_ref[c, pl.ds(r, m, stride=d), :] = blk[:, c * LANES:(c + 1) * LANES]
    return jnp.concatenate([a_ref[c] for c in halves], axis=1)


def _inproj_body(x_ref, g1_ref, wa_ref, wg_ref, wb_ref, gb_ref, gq_ref, gk_ref,
                 mq_ref, kT_ref, mv_ref, so_ref, gi_ref, gf_ref,
                 q0_ref, q1_ref, q2_ref, k0_ref, k1_ref, k2_ref, v0_ref, v1_ref, v2_ref,
                 sgm_ref, sga_ref, st_ref):
    x = x_ref[...]
    h = x * lax.rsqrt(jnp.mean(x * x, axis=-1, keepdims=True) + EPS) * g1_ref[...]
    hb = h.astype(BF16)

    def seg(w_ref, c):
        return jnp.dot(hb, w_ref[:, c[0]:c[1]], preferred_element_type=F32)

    mq_ref[...] = seg(wa_ref, _C_MQ).astype(BF16)
    mv_ref[...] = seg(wa_ref, _C_MV).astype(BF16)
    so_ref[...] = jax.nn.sigmoid(seg(wa_ref, _C_MO)).astype(BF16)
    hid_r = lax.broadcasted_iota(I32, (A_WIDTH, A_WIDTH), 0) // A_HEAD_DIM
    hid_c = lax.broadcasted_iota(I32, (A_WIDTH, A_WIDTH), 1) // A_HEAD_DIM
    head_ones = (hid_r == hid_c).astype(BF16)
    for c, refs, gain_ref in ((_C_AQ, (q0_ref, q1_ref, q2_ref), gq_ref), (_C_AK, (k0_ref, k1_ref, k2_ref), gk_ref),
                              (_C_AV, (v0_ref, v1_ref, v2_ref), None)):
        val = seg(wb_ref, c)
        for g, ref in enumerate(refs):
            piece = val[:, g * A_WIDTH:(g + 1) * A_WIDTH]
            if gain_ref is not None:
                ss = jnp.dot((piece * piece).astype(BF16), head_ones, preferred_element_type=F32)
                piece = piece * lax.rsqrt(ss * (1.0 / A_HEAD_DIM) + EPS) * gain_ref[:, g * A_WIDTH:(g + 1) * A_WIDTH]
            _split_residues(piece, DILATED_PATTERNS[g][1], ref, st_ref)
    sgm_ref[...] = jax.nn.sigmoid(seg(wb_ref, _C_GM)).astype(BF16)
    sga_ref[...] = jax.nn.sigmoid(seg(wb_ref, _C_GA)).astype(BF16)

    kT_ref[...] = lax.dot_general(wa_ref[:, _C_MK[0]:_C_MK[1]], hb, _TN, preferred_element_type=F32).astype(BF16)
    zg = lax.dot_general(wg_ref[...], hb, _TN, preferred_element_type=F32)
    zi = zg[0:8] + gb_ref[0:8]
    zf = zg[M_HEADS:M_HEADS + 8] + gb_ref[8:16]
    gi_ref[...] = GATE_SOFTCAP * jnp.tanh(zi / GATE_SOFTCAP)
    gf_ref[...] = _log_sigmoid(GATE_SOFTCAP * jnp.tanh(zf / GATE_SOFTCAP))


def _stage_inproj(x2d, norm1_g, w_in, gate_b, gq, gk, batch, seq, tm):
    n = x2d.shape[0]
    steps = seq // tm
    wa = w_in[:, :_A_WIDTH].astype(BF16)
    wg = jnp.pad(w_in[:, _A_WIDTH:_B_START], ((0, 0), (0, LANES - IN_SPLITS[4]))).astype(BF16)
    wb = w_in[:, _B_START:].astype(BF16)
    gb = jnp.zeros((16, 1), F32)
    gb = gb.at[0:4, 0].set(gate_b[:M_HEADS].astype(F32)).at[8:12, 0].set(gate_b[M_HEADS:].astype(F32))
    g1 = norm1_g.astype(F32).reshape(1, D_MODEL)
    gq_t = (jnp.tile(gq.astype(F32), (1, A_HEADS)) * (A_HEAD_DIM ** -0.5)).reshape(1, N_GROUPS * A_WIDTH)
    gk_t = jnp.tile(gk.astype(F32), (1, A_HEADS)).reshape(1, N_GROUPS * A_WIDTH)

    row = lambda w: pl.BlockSpec((tm, w), lambda i: (i, 0))
    rowT = lambda r: pl.BlockSpec((r, tm), lambda i: (0, i))
    full = lambda a: pl.BlockSpec(a.shape, lambda i: (0,) * a.ndim)
    dils = [d for _, d in DILATED_PATTERNS]
    res_shape = lambda d: jax.ShapeDtypeStruct((batch, d, seq // d, A_WIDTH), BF16)
    res_spec = lambda d: pl.BlockSpec((1, d, tm // d, A_WIDTH), lambda i: (i // steps, 0, i % steps, 0))
    out_shapes = (
        jax.ShapeDtypeStruct((n, M_QK_WIDTH), BF16),
        jax.ShapeDtypeStruct((M_QK_WIDTH, n), BF16),
        jax.ShapeDtypeStruct((n, M_WIDTH), BF16),
        jax.ShapeDtypeStruct((n, M_WIDTH), BF16),
        jax.ShapeDtypeStruct((8, n), F32),
        jax.ShapeDtypeStruct((8, n), F32),
        *[res_shape(d) for d in dils], *[res_shape(d) for d in dils], *[res_shape(d) for d in dils],
        jax.ShapeDtypeStruct((n, D_MODEL), BF16),
        jax.ShapeDtypeStruct((n, D_MODEL), BF16),
    )
    out_specs = (row(M_QK_WIDTH), rowT(M_QK_WIDTH), row(M_WIDTH), row(M_WIDTH), rowT(8), rowT(8),
                 *[res_spec(d) for d in dils], *[res_spec(d) for d in dils], *[res_spec(d) for d in dils],
                 row(D_MODEL), row(D_MODEL))
    outs = pl.pallas_call(
        _inproj_body,
        grid=(n // tm,),
        in_specs=[row(D_MODEL), full(g1), full(wa), full(wg), full(wb), full(gb), full(gq_t), full(gk_t)],
        out_specs=out_specs,
        out_shape=out_shapes,
        scratch_shapes=[pltpu.VMEM((2, 2, tm, LANES), F32)],
        compiler_params=_params("parallel"),
        name="inproj",
    )(x2d, g1, wa, wg, wb, gb, gq_t, gk_t)
    mq, kT, mv, so, gi, gf = outs[:6]
    aq, ak, av = outs[6:9], outs[9:12], outs[12:15]
    return mq, kT, mv, so, gi, gf, aq, ak, av, outs[15], outs[16]


M_CHUNK_LEN = 128


def _mlstm_body(q_ref, v_ref, so_ref, ng_ref, *rest, nchunk, nseq):
    kT_refs, gi_refs, gf_refs = rest[0:nseq], rest[nseq:2 * nseq], rest[2 * nseq:3 * nseq]
    o_ref, c_ref, m_ref = rest[3 * nseq:]
    L = M_CHUNK_LEN

    @pl.when(pl.program_id(0) == 0)
    def _():
        c_ref[...] = jnp.zeros_like(c_ref)
        m_ref[...] = jnp.zeros_like(m_ref)

    lane8 = lax.broadcasted_iota(I32, (8, L), 1)
    causal = lax.broadcasted_iota(I32, (L, L), 1) <= lax.broadcasted_iota(I32, (L, L), 0)
    lo_half = lax.broadcasted_iota(I32, (L, LANES), 1) < M_QK_DIM
    ones = jnp.ones((L, M_V_DIM), BF16)

    heads = range(M_HEADS)
    cstate = [[c_ref[s, h * M_QK_DIM:(h + 1) * M_QK_DIM, :] for h in heads] for s in range(nseq)]
    m_prev = [m_ref[s, :, 0:1] for s in range(nseq)]
    units = []
    for c in range(nchunk):
        rows = slice(c * L, (c + 1) * L)
        for s in range(nseq):
            gi = gi_refs[s][:, rows]
            b = gf_refs[s][:, rows]
            sh = 1
            while sh < L:
                b = b + jnp.where(lane8 >= sh, pltpu.roll(b, sh, 1), 0.0)
                sh *= 2
            u = gi - b
            g = b[:, L - 1:L]
            a = g + u
            amax = jnp.max(a, axis=1, keepdims=True)
            m_new = jnp.maximum(g + m_prev[s], amax)
            w = jnp.exp(a - m_new) * (M_QK_DIM ** -0.5)
            s_old = jnp.exp(g + m_prev[s] - m_new)
            vext = [jnp.concatenate([v_ref[s, rows, h * M_V_DIM:(h + 1) * M_V_DIM], ones], axis=1) for h in heads]
            cloc = []
            for h in heads:
                hr = slice(h * M_QK_DIM, (h + 1) * M_QK_DIM)
                kw = (kT_refs[s][hr, rows].astype(F32) * w[h:h + 1, :]).astype(BF16)
                cloc.append(jnp.dot(kw, vext[h], preferred_element_type=F32))
            units.append(dict(seq=s, rows=rows, b=b, u=u, m_prev=m_prev[s], state=cstate[s], vext=vext))
            cstate[s] = [s_old[h:h + 1, :] * cstate[s][h] + cloc[h] for h in heads]
            m_prev[s] = m_new
    for s in range(nseq):
        for h in heads:
            c_ref[s, h * M_QK_DIM:(h + 1) * M_QK_DIM, :] = cstate[s][h]
        m_ref[s] = jnp.broadcast_to(m_prev[s], (8, LANES))

    for un in units:
        s, rows = un["seq"], un["rows"]
        un["s"], un["qc"] = [], []
        for p in range(M_HEADS // 2):
            lanes_p = slice(p * LANES, (p + 1) * LANES)
            q_pair = q_ref[s, rows, lanes_p]
            kT_pair = kT_refs[s][lanes_p, rows]
            c_pair = jnp.concatenate([un["state"][2 * p], un["state"][2 * p + 1]], axis=0).astype(BF16)
            for hh in range(2):
                qm = jnp.where(lo_half if hh == 0 else jnp.logical_not(lo_half), q_pair, jnp.zeros_like(q_pair))
                un["s"].append(jnp.dot(qm, kT_pair, preferred_element_type=F32) * (M_QK_DIM ** -0.5))
                un["qc"].append(jnp.dot(qm, c_pair, preferred_element_type=F32))

    for un in units:
        s, rows, b, u, mp = un["seq"], un["rows"], un["b"], un["u"], un["m_prev"]
        for h in heads:
            hl = slice(h * M_V_DIM, (h + 1) * M_V_DIM)
            bcol = jnp.transpose(jnp.broadcast_to(b[h:h + 1, :], (L, L)))
            dm = jnp.where(causal, bcol + u[h:h + 1, :], -jnp.inf)
            inter = bcol + mp[h:h + 1, :]
            m_t = jnp.maximum(inter, jnp.max(dm, axis=1, keepdims=True))
            pmat = (un["s"][h] * jnp.exp(dm - m_t)).astype(BF16)
            sc = jnp.exp(inter - m_t)
            out = (jnp.dot(pmat, un["vext"][h], preferred_element_type=F32)
                   + jnp.concatenate([sc, sc], axis=1) * un["qc"][h])
            hv = out[:, :M_V_DIM] / jnp.maximum(jnp.abs(out[:, M_V_DIM:]), jnp.exp(-m_t))
            hn = hv * lax.rsqrt(jnp.mean(hv * hv, axis=1, keepdims=True) + EPS)
            hn = hn * ng_ref[:, hl] * so_ref[s, rows, hl].astype(F32)
            o_ref[s, rows, hl] = hn.astype(BF16)


def _stage_mlstm(mq, kT, mv, so, gi, gf, norm_g, batch, seq, rows_per_step):
    n = batch * seq
    R = rows_per_step
    steps = seq // R
    ng = norm_g.astype(F32).reshape(1, M_WIDTH)
    per_seq = lambda a: a.reshape(batch, seq, a.shape[1])
    row = lambda w: pl.BlockSpec((batch, R, w), lambda i: (0, i, 0))
    colT = lambda r, s: pl.BlockSpec((r, R), lambda i, s=s: (0, s * steps + i))
    seqs = range(batch)
    out = pl.pallas_call(
        functools.partial(_mlstm_body, nchunk=R // M_CHUNK_LEN, nseq=batch),
        grid=(steps,),
        in_specs=[row(M_QK_WIDTH), row(M_WIDTH), row(M_WIDTH), pl.BlockSpec((1, M_WIDTH), lambda i: (0, 0)),
                  *[colT(M_QK_WIDTH, s) for s in seqs], *[colT(8, s) for s in seqs], *[colT(8, s) for s in seqs]],
        out_specs=row(M_WIDTH),
        out_shape=jax.ShapeDtypeStruct((batch, seq, M_WIDTH), BF16),
        scratch_shapes=[pltpu.VMEM((batch, M_QK_WIDTH, 2 * M_V_DIM), F32), pltpu.VMEM((batch, 8, LANES), F32)],
        compiler_params=_params("arbitrary"),
        name="mlstm",
    )(per_seq(mq), per_seq(mv), per_seq(so), ng, *[kT] * batch, *[gi] * batch, *[gf] * batch)
    return out.reshape(n, M_WIDTH)


LSE_LANES = LANES // A_HEADS


def _attn_body(q_ref, kp_ref, kc_ref, vp_ref, vc_ref, o_ref, lse_ref, *, dil, slopes, lq):
    QB = N_BACK
    first = pl.program_id(2) == 0
    qn = q_ref[0, 0]
    kcn = kc_ref[0, 0]
    kpn = kp_ref[0, 0]
    vc = vc_ref[0, 0]
    vp = vp_ref[0, 0]

    qi = lax.broadcasted_iota(I32, (QB, 2 * QB), 0)
    kj = lax.broadcasted_iota(I32, (QB, 2 * QB), 1)
    dist = qi + QB - kj
    band = jnp.logical_and(dist >= 0, dist <= N_BACK)
    distf = (dist * dil).astype(F32)
    bias = [jnp.where(band, -float(slopes[h]) * distf, -jnp.inf) for h in range(A_HEADS)]
    no_prev = jnp.logical_and(first, kj < QB)
    lo_half = lax.broadcasted_iota(I32, (QB, LANES), 1) < A_HEAD_DIM
    ones = jnp.ones((2 * QB, LANES), BF16)

    units = []
    for j in range(lq // QB):
        rows = slice(j * QB, (j + 1) * QB)
        prow = slice((j - 1) * QB, j * QB)
        keys = jnp.concatenate([kpn if j == 0 else kcn[prow], kcn[rows]], axis=0)
        vals = jnp.concatenate([vp if j == 0 else vc[prow], vc[rows]], axis=0)
        for p in range(A_HEADS // 2):
            lanes_p = slice(p * LANES, (p + 1) * LANES)
            q_pair = qn[rows, lanes_p]
            k_pair = keys[:, lanes_p]
            vext = jnp.concatenate([vals[:, lanes_p], ones], axis=1)
            scores = []
            for hh in range(2):
                sel = lo_half if hh == 0 else jnp.logical_not(lo_half)
                qm = jnp.where(sel, q_pair, jnp.zeros_like(q_pair))
                scores.append(lax.dot_general(qm, k_pair, _NT, preferred_element_type=F32))
            units.append((j, rows, lanes_p, p, vext, scores))

    lane = lax.broadcasted_iota(I32, (QB, LANES), 1)
    den_all = m_all = None
    for j, rows, lanes_p, p, vext, scores in units:
        o_pair = None
        for hh in range(2):
            s = scores[hh] + bias[2 * p + hh]
            if j == 0:
                s = jnp.where(no_prev, -jnp.inf, s)
            m = jnp.max(s, axis=1, keepdims=True)
            pv = jnp.dot(jnp.exp(s - m).astype(BF16), vext, preferred_element_type=F32)
            den = pv[:, LANES:]
            o_h = pv[:, :LANES] / den
            o_pair = o_h if hh == 0 else jnp.where(lo_half, o_pair, o_h)
            head = 2 * p + hh
            earlier = lane < head * LSE_LANES
            den_all = den if head == 0 else jnp.where(earlier, den_all, den)
            m_all = jnp.broadcast_to(m, den.shape) if head == 0 else jnp.where(earlier, m_all, m)
        o_ref[0, 0, rows, lanes_p] = o_pair.astype(BF16)
        if p == A_HEADS // 2 - 1:
            lse_ref[0, 0, rows, :] = m_all + jnp.log(den_all)


def _stage_attn(aq, ak, av, batch, seq, group):
    _, dil = DILATED_PATTERNS[group]
    L = seq // dil
    assert L % N_BACK == 0
    lq = min(2048, L)
    nq = L // lq
    sub = lq // N_BACK
    cur = pl.BlockSpec((1, 1, lq, A_WIDTH), lambda b, r, i: (b, r, i, 0))
    prev = pl.BlockSpec((1, 1, N_BACK, A_WIDTH), lambda b, r, i: (b, r, jnp.maximum(i * sub - 1, 0), 0))
    cur_lse = pl.BlockSpec((1, 1, lq, LANES), lambda b, r, i: (b, r, i, 0))
    return pl.pallas_call(
        functools.partial(_attn_body, dil=dil, slopes=tuple(_alibi_slopes()[group]), lq=lq),
        grid=(batch, dil, nq),
        in_specs=[cur, prev, cur, prev, cur],
        out_specs=(cur, cur_lse),
        out_shape=(jax.ShapeDtypeStruct((batch, dil, L, A_WIDTH), BF16),
                   jax.ShapeDtypeStruct((batch, dil, L, LANES), F32)),
        compiler_params=_params("parallel", "parallel", "parallel"),
        name=f"dilated_attn_d{dil}",
    )(aq, ak, ak, av, av)


PACK_ROWS = D_MODEL // (2 * LANES)
U32 = jnp.uint32
_HIGH_HALF = 0xFFFF0000


def _pack_rows(val, exact=False):
    half = D_MODEL // 2
    lo, hi = val[:, :half], val[:, half:]
    if exact:
        return (lax.bitcast_convert_type(lo, U32) >> 16) | lax.bitcast_convert_type(hi, U32)

    def bits(v):
        return lax.bitcast_convert_type(v.astype(BF16).astype(F32), U32)

    return (bits(lo) >> 16) | (bits(hi) & U32(_HIGH_HALF))


def _unpack_rows(words):
    lo = lax.bitcast_convert_type(words << 16, F32).astype(BF16)
    hi = lax.bitcast_convert_type(words & U32(_HIGH_HALF), F32).astype(BF16)
    return jnp.concatenate([lo, hi], axis=1)


def _flat(ref):
    rows = 1
    for d in ref.shape[:-2]:
        rows *= d
    return ref.reshape(rows * PACK_ROWS, LANES)


def _store_packed(flat_ref, row0, words):
    t = words.shape[0]
    for s in range(PACK_ROWS):
        flat_ref[pl.ds(row0 * PACK_ROWS + s, t, stride=PACK_ROWS), :] = words[:, s * LANES:(s + 1) * LANES]


def _load_packed(flat_ref, row0, t):
    return jnp.concatenate([flat_ref[pl.ds(row0 * PACK_ROWS + s, t, stride=PACK_ROWS), :] for s in range(PACK_ROWS)],
                           axis=1)


def _rows8(vals):
    t = vals[0].shape[1]
    rid = lax.broadcasted_iota(I32, (8, t), 0)
    out = jnp.zeros((8, t), vals[0].dtype)
    for k, v in enumerate(vals):
        out = jnp.where(rid == k, jnp.broadcast_to(v, (8, t)), out)
    return out


def _head_lanes(c):
    lane = lax.broadcasted_iota(I32, c.shape, 1)
    r1, r2, r3 = [pltpu.roll(c, k * LSE_LANES, 1) for k in (1, 2, 3)]
    left = jnp.where(lane < LSE_LANES, c, jnp.where(lane < 3 * LSE_LANES, r1, r2))
    right = jnp.where(lane < LSE_LANES, r2, jnp.where(lane < 3 * LSE_LANES, r3, c))
    return jnp.concatenate([left, right], axis=1)


X_RING = 3


def _merge_body(hm_ref, o1_ref, o2_ref, o3_ref, l1_ref, l2_ref, l3_ref, sgm_hbm, sga_hbm, x_hbm,
                wm_ref, wa_ref, wo_ref, g2_ref, wrh_ref, br_ref,
                x2_ref, xn_ref, loc_ref, gate_ref, tcnt_ref, tcar_ref, cnt_ref, carry_ref, st_ref, logits_ref,
                xbuf_ref, gmbuf_ref, gabuf_ref, xsem):
    step = pl.program_id(0)
    ntile = pl.num_programs(0) - 1
    tm = xbuf_ref.shape[1]

    class x_copy:
        def __init__(self, tile):
            slot = tile % X_RING
            self.copies = [pltpu.make_async_copy(hbm.at[pl.ds(tile * tm, tm)], buf.at[slot], xsem.at[k, slot])
                           for k, (hbm, buf) in enumerate(((x_hbm, xbuf_ref), (sgm_hbm, gmbuf_ref), (sga_hbm, gabuf_ref)))]

        def start(self):
            for c in self.copies:
                c.start()

        def wait(self):
            for c in self.copies:
                c.wait()

    @pl.when(step == 0)
    def _():
        carry_ref[...] = jnp.zeros_like(carry_ref)
        logits_ref[...] = jnp.zeros_like(logits_ref)
        for t in range(X_RING - 1):
            x_copy(t).start()

    @pl.when(step + X_RING - 1 < ntile)
    def _():
        x_copy(step + X_RING - 1).start()

    @pl.when(step < ntile)
    def _():
        x_copy(step).wait()

    m_branch = jnp.dot(hm_ref[...], wm_ref[...], preferred_element_type=F32)
    dils = [d for _, d in DILATED_PATTERNS]
    l1, l2, l3 = [_merge_residues(r, d, st_ref) for r, d in zip((l1_ref, l2_ref, l3_ref), dils)]
    lmax = jnp.maximum(jnp.maximum(l1, l2), l3)
    e1, e2, e3 = jnp.exp(l1 - lmax), jnp.exp(l2 - lmax), jnp.exp(l3 - lmax)
    inv = 1.0 / (e1 + e2 + e3)
    h_a = _head_lanes(e1 * inv) * _merge_residues(o1_ref, dils[0], st_ref)
    h_a = h_a + _head_lanes(e2 * inv) * _merge_residues(o2_ref, dils[1], st_ref)
    h_a = h_a + _head_lanes(e3 * inv) * _merge_residues(o3_ref, dils[2], st_ref)
    slot = jnp.minimum(step, ntile - 1) % X_RING
    y = (gmbuf_ref[slot].astype(F32) * m_branch
         + gabuf_ref[slot].astype(F32) * jnp.dot(h_a.astype(BF16), wa_ref[...], preferred_element_type=F32))
    x2 = xbuf_ref[slot] + jnp.dot(y.astype(BF16), wo_ref[...], preferred_element_type=F32)
    x2_ref[...] = x2
    xn = x2 * lax.rsqrt(jnp.mean(x2 * x2, axis=-1, keepdims=True) + EPS) * g2_ref[...]
    xh = xn.astype(BF16)
    xn_ref[...] = xh

    logits = logits_ref[...]
    t = logits.shape[1]
    eid = lax.broadcasted_iota(I32, (N_EXPERTS, t), 0).astype(F32)
    vals = logits
    top_v, top_i = [], []
    for _ in range(TOP_K):
        mx = jnp.max(vals, axis=0, keepdims=True)
        ik = jnp.min(jnp.where(vals == mx, eid, float(N_EXPERTS)), axis=0, keepdims=True)
        top_v.append(mx)
        top_i.append(ik)
        vals = jnp.where(eid == ik, -jnp.inf, vals)
    ex = [jnp.exp(v - top_v[0]) for v in top_v]
    den = ex[0] + ex[1] + ex[2] + ex[3]
    gate_ref[...] = _rows8([e / den for e in ex])

    chosen = jnp.zeros((N_EXPERTS, t), F32)
    for ik in top_i:
        chosen = chosen + (eid == ik).astype(F32)
    before = jnp.where(lax.broadcasted_iota(jnp.int16, (t, t), 0) < lax.broadcasted_iota(jnp.int16, (t, t), 1),
                       jnp.ones((), BF16), jnp.zeros((), BF16))
    prefix = jnp.dot(chosen.astype(BF16), before, preferred_element_type=F32)
    tcount = jnp.broadcast_to(jnp.sum(chosen, axis=1, keepdims=True), (N_EXPERTS, LANES))
    tcount = jnp.where(step > 0, tcount, 0.0)
    below = (lax.broadcasted_iota(I32, (N_EXPERTS, N_EXPERTS), 1)
             < lax.broadcasted_iota(I32, (N_EXPERTS, N_EXPERTS), 0)).astype(BF16)
    t_hi = jnp.floor(tcount * (1.0 / BF16_EXACT_INT)) * BF16_EXACT_INT
    tile_off = (jnp.dot(below, t_hi.astype(BF16), preferred_element_type=F32)
                + jnp.dot(below, (tcount - t_hi).astype(BF16), preferred_element_type=F32))
    pos = prefix + tile_off[:, 0:1]
    loc_ref[...] = _rows8([jnp.sum(jnp.where(eid == ik, pos, 0.0), axis=0, keepdims=True).astype(I32)
                           for ik in top_i])
    carry = carry_ref[...]
    tcnt_ref[...] = tcount.astype(I32)
    tcar_ref[...] = carry.astype(I32)
    total = carry + tcount
    carry_ref[...] = total
    cnt_ref[...] = total

    logits_ref[...] = lax.dot_general(wrh_ref[...], xh, _NT, preferred_element_type=F32) + br_ref[...]


def _stage_merge(h_m, attn, sgm, sga, x2d, w_mb, w_ab, w_out, norm2_g, w_router, b_router, batch, seq, tm):
    n = x2d.shape[0]
    steps = seq // tm
    assert n // tm >= X_RING - 1
    (o1, l1), (o2, l2), (o3, l3) = attn
    wm = w_mb.astype(BF16)
    wa = w_ab.astype(BF16)
    wo = w_out.astype(BF16)
    g2 = norm2_g.astype(F32).reshape(1, D_MODEL)
    wrh = w_router.astype(BF16).T
    br = b_router.astype(F32).reshape(N_EXPERTS, 1)
    mixed = lambda i: jnp.minimum(i, n // tm - 1)
    routed = lambda i: jnp.maximum(i - 1, 0)
    row = lambda w: pl.BlockSpec((tm, w), lambda i: (mixed(i), 0))
    rowT = lambda r: pl.BlockSpec((r, tm), lambda i: (0, routed(i)))
    full = lambda a: pl.BlockSpec(a.shape, lambda i: (0,) * a.ndim)
    res = lambda d, w: pl.BlockSpec((1, d, tm // d, w), lambda i: (mixed(i) // steps, 0, mixed(i) % steps, 0))
    dils = [d for _, d in DILATED_PATTERNS]
    per_tile = pl.BlockSpec((N_EXPERTS, LANES), lambda i: (0, routed(i)))
    return pl.pallas_call(
        _merge_body,
        grid=(n // tm + 1,),
        in_specs=[row(M_WIDTH), *[res(d, A_WIDTH) for d in dils], *[res(d, LANES) for d in dils],
                  *[pl.BlockSpec(memory_space=pl.ANY)] * 3,
                  full(wm), full(wa), full(wo), full(g2), full(wrh), full(br)],
        out_specs=(row(D_MODEL), row(D_MODEL), rowT(8), rowT(8), per_tile, per_tile,
                   pl.BlockSpec((N_EXPERTS, LANES), lambda i: (0, 0))),
        out_shape=(jax.ShapeDtypeStruct((n, D_MODEL), F32),
                   jax.ShapeDtypeStruct((n, D_MODEL), BF16),
                   jax.ShapeDtypeStruct((8, n), I32),
                   jax.ShapeDtypeStruct((8, n), F32),
                   jax.ShapeDtypeStruct((N_EXPERTS, (n // tm) * LANES), I32),
                   jax.ShapeDtypeStruct((N_EXPERTS, (n // tm) * LANES), I32),
                   jax.ShapeDtypeStruct((N_EXPERTS, LANES), F32)),
        scratch_shapes=[pltpu.VMEM((N_EXPERTS, LANES), F32), pltpu.VMEM((2, 2, tm, LANES), F32),
                        pltpu.VMEM((N_EXPERTS, tm), F32),
                        pltpu.VMEM((X_RING, tm, D_MODEL), F32), pltpu.VMEM((X_RING, tm, D_MODEL), BF16),
                        pltpu.VMEM((X_RING, tm, D_MODEL), BF16), pltpu.SemaphoreType.DMA((3, X_RING))],
        compiler_params=_params("arbitrary"),
        name="merge_route",
    )(h_m, o1, o2, o3, l1, l2, l3, sgm, sga, x2d, wm, wa, wo, g2, wrh, br)


def _offsets_body(cnt_ref, blk_ref, pstart_ref, zlo_ref, zhi_ref, *, nblk_pad):
    cnt = cnt_ref[...]
    padded = jnp.floor((cnt + (MOE_BLOCK - 1)) * (1.0 / MOE_BLOCK)) * MOE_BLOCK
    lower = (lax.broadcasted_iota(I32, (N_EXPERTS, N_EXPERTS), 1)
             <= lax.broadcasted_iota(I32, (N_EXPERTS, N_EXPERTS), 0)).astype(BF16)
    nb = padded * (1.0 / MOE_BLOCK)
    nb_hi = jnp.floor(nb * (1.0 / BF16_EXACT_INT)) * BF16_EXACT_INT
    pends = (jnp.dot(lower, nb_hi.astype(BF16), preferred_element_type=F32)
             + jnp.dot(lower, (nb - nb_hi).astype(BF16), preferred_element_type=F32)) * MOE_BLOCK
    pstart = pends - padded
    pstart_ref[...] = pstart.astype(I32)
    zlo_ref[...] = (pstart + cnt).astype(I32)
    zhi_ref[...] = pends.astype(I32)

    first_row = (lax.broadcasted_iota(I32, (N_EXPERTS, nblk_pad), 1) * MOE_BLOCK).astype(F32)
    pe = jnp.broadcast_to(pends[:, 0:1], (N_EXPERTS, nblk_pad))
    be = jnp.sum((pe <= first_row).astype(F32), axis=0, keepdims=True)
    be = jnp.minimum(be, float(N_EXPERTS - 1))
    nused = pends[N_EXPERTS - 1:N_EXPERTS, 0:1] * (1.0 / MOE_BLOCK)
    nonempty = jnp.broadcast_to(padded[:, 0:1], (N_EXPERTS, nblk_pad)) > 0.0
    runidx = jnp.sum(jnp.logical_and(pe <= first_row, nonempty).astype(F32), axis=0, keepdims=True)
    parity = runidx - 2.0 * jnp.floor(runidx * 0.5)
    eid = lax.broadcasted_iota(I32, (N_EXPERTS, nblk_pad), 0).astype(F32)
    later = jnp.logical_and(eid > be, nonempty)
    nxt = jnp.min(jnp.where(later, eid, float(N_EXPERTS)), axis=0, keepdims=True)
    blk_ref[...] = _rows8([be.astype(I32), jnp.broadcast_to(nused, (1, nblk_pad)).astype(I32),
                           parity.astype(I32), nxt.astype(I32)])


def _stage_offsets(cnt, nblk):
    nblk_pad = -(-nblk // LANES) * LANES
    const = lambda r, c: pl.BlockSpec((r, c), lambda i: (0, 0))
    per_expert = jax.ShapeDtypeStruct((N_EXPERTS, LANES), I32)
    return pl.pallas_call(
        functools.partial(_offsets_body, nblk_pad=nblk_pad),
        grid=(1,),
        in_specs=[const(N_EXPERTS, LANES)],
        out_specs=(const(8, nblk_pad), const(N_EXPERTS, LANES), const(N_EXPERTS, LANES), const(N_EXPERTS, LANES)),
        out_shape=(jax.ShapeDtypeStruct((8, nblk_pad), I32), per_expert, per_expert, per_expert),
        compiler_params=_params("arbitrary"),
        name="route_offsets",
    )(cnt)


RUN_BITS = 10


def _tile_rows(ref, first_row, nrows):
    return ref.at[pl.ds(first_row, nrows)]


def _for_each_piece(length, fn):
    for b in reversed(range(RUN_BITS)):
        @pl.when(((length >> b) & 1) == 1)
        def _(b=b):
            fn((length >> (b + 1)) << (b + 1), 1 << b)


def _for_each_run(tile, tcnt_ref, tcar_ref, pstart_ref, fn):
    def per_expert(e, local):
        count = tcnt_ref[tile, e]
        first = pstart_ref[e] + tcar_ref[tile, e]
        _for_each_piece(count, lambda off, size: fn(local + off, first + off, size))
        return local + count

    lax.fori_loop(0, N_EXPERTS, per_expert, 0)


PERM_CHUNK = 256


def _dispatch_body(tcnt_ref, tcar_ref, pstart_ref, zlo_ref, zhi_ref, loc_ref, xn_ref, xs_hbm,
                   buf_ref, sems, *, tm):
    step = pl.program_id(0)
    nloc = TOP_K * tm

    def wait_buffer(slot):
        pltpu.make_async_copy(buf_ref.at[slot], xs_hbm.at[pl.ds(0, nloc)], sems.at[slot]).wait()

    for slot in range(2):
        tile = 2 * step + slot

        @pl.when(step > 0)
        def _(slot=slot):
            wait_buffer(slot)

        loc = loc_ref[:, slot * tm:(slot + 1) * tm].astype(jnp.int16)
        xn = xn_ref[slot * tm:(slot + 1) * tm, :]
        for c in range(TOP_K * tm // PERM_CHUNK):
            lid = lax.broadcasted_iota(jnp.int16, (PERM_CHUNK, tm), 0) + jnp.int16(c * PERM_CHUNK)
            hit = lid == loc[0:1, :]
            for k in range(1, TOP_K):
                hit = jnp.logical_or(hit, lid == loc[k:k + 1, :])
            perm = jnp.where(hit, jnp.ones((), BF16), jnp.zeros((), BF16))
            rows = jnp.dot(perm, xn, preferred_element_type=F32)
            _store_packed(_flat(buf_ref), slot * nloc + c * PERM_CHUNK, _pack_rows(rows, exact=True))

        def run_copy(local, first, size, slot=slot):
            return pltpu.make_async_copy(_tile_rows(buf_ref.at[slot], local, size),
                                         _tile_rows(xs_hbm, first, size), sems.at[slot])

        _for_each_run(tile, tcnt_ref, tcar_ref, pstart_ref, lambda l, f, s: run_copy(l, f, s).start())

    @pl.when(step == pl.num_programs(0) - 1)
    def _():
        wait_buffer(0)
        wait_buffer(1)
        zsrc = buf_ref.at[0]
        zsrc[pl.ds(0, MOE_BLOCK)] = jnp.zeros((MOE_BLOCK, PACK_ROWS, LANES), U32)

        def zero_copy(first, size):
            return pltpu.make_async_copy(_tile_rows(zsrc, 0, size), _tile_rows(xs_hbm, first, size), sems.at[0])

        first_unused = zhi_ref[N_EXPERTS - 1] // MOE_BLOCK
        nblk = xs_hbm.shape[0] // MOE_BLOCK

        def for_each_zero_copy(action):
            def per_expert(e, carry):
                lo = zlo_ref[e]
                _for_each_piece(zhi_ref[e] - lo, lambda off, size: action(zero_copy(lo + off, size)))
                return carry

            def tail(blk, carry):
                action(zero_copy(blk * MOE_BLOCK, MOE_BLOCK))
                return carry

            lax.fori_loop(0, N_EXPERTS, per_expert, 0)
            lax.fori_loop(first_unused, nblk, tail, 0)

        for_each_zero_copy(lambda copy: copy.start())
        for_each_zero_copy(lambda copy: copy.wait())


def _stage_dispatch(tables, loc8, xn, nrows, tm):
    n = xn.shape[0]
    assert TOP_K * tm >= MOE_BLOCK and (n // tm) % 2 == 0
    grid_spec = pltpu.PrefetchScalarGridSpec(
        num_scalar_prefetch=5,
        grid=(n // (2 * tm),),
        in_specs=[pl.BlockSpec((8, 2 * tm), lambda i, *_: (0, i)),
                  pl.BlockSpec((2 * tm, D_MODEL), lambda i, *_: (i, 0))],
        out_specs=pl.BlockSpec(memory_space=pl.ANY),
        scratch_shapes=[pltpu.VMEM((2, TOP_K * tm, PACK_ROWS, LANES), U32), pltpu.SemaphoreType.DMA((2,))],
    )
    return pl.pallas_call(
        functools.partial(_dispatch_body, tm=tm),
        grid_spec=grid_spec,
        out_shape=jax.ShapeDtypeStruct((nrows, PACK_ROWS, LANES), U32),
        compiler_params=_params("arbitrary"),
        name="dispatch",
    )(*tables, loc8, xn)


EXPERT_BLOCKS_PER_STEP = 2


def _expert_body(be_ref, nu_ref, par_ref, nxt_ref, xs_ref, w1_hbm, b1_ref, w2_hbm, b2_ref, ys_ref,
                 w1f_ref, w2f_ref, w1b_ref, w2b_ref, sems):
    def fetch(expert, slot):
        return (pltpu.make_async_copy(w1_hbm.at[expert], w1f_ref.at[slot], sems.at[slot, 0]),
                pltpu.make_async_copy(w2_hbm.at[expert], w2f_ref.at[slot], sems.at[slot, 1]))

    def block_info(sub):
        j = pl.program_id(0) * EXPERT_BLOCKS_PER_STEP + sub
        used = j < nu_ref[0]
        jj = jnp.maximum(jnp.minimum(j, nu_ref[0] - 1), 0)
        e = be_ref[jj]
        fresh = jnp.logical_or(j == 0, e != be_ref[jnp.maximum(jj - 1, 0)])
        return j, jj, e, used, fresh

    def refresh(j, jj, e):
        slot = par_ref[jj]

        @pl.when(j == 0)
        def _():
            for c in fetch(e, slot):
                c.start()

        for c in fetch(e, slot):
            c.wait()
        nxt = nxt_ref[jj]

        @pl.when(nxt < N_EXPERTS)
        def _():
            for c in fetch(nxt, 1 - slot):
                c.start()

        w1b_ref[...] = w1f_ref[slot].astype(BF16)
        w2b_ref[...] = w2f_ref[slot].astype(BF16)

    def compute(e, row0, nrows):
        xb = _unpack_rows(_load_packed(_flat(xs_ref), row0, nrows))
        gu = jnp.dot(xb, w1b_ref[...], preferred_element_type=F32) + b1_ref[pl.ds(e, 1), :]
        gate = jnp.minimum(gu[:, :D_FF], SWIGLU_LIMIT)
        lin = jnp.clip(gu[:, D_FF:], -SWIGLU_LIMIT, SWIGLU_LIMIT)
        act = (lin + 1.0) * (gate * jax.nn.sigmoid(SWIGLU_ALPHA * gate))
        ys = jnp.dot(act.astype(BF16), w2b_ref[...], preferred_element_type=F32) + b2_ref[pl.ds(e, 1), :]
        _store_packed(_flat(ys_ref), row0, _pack_rows(ys))

    j0, jj0, e0, used0, fresh0 = block_info(0)
    j1, jj1, e1, used1, fresh1 = block_info(1)
    same = jnp.logical_and(used1, e1 == e0)

    @pl.when(jnp.logical_and(used0, fresh0))
    def _():
        refresh(j0, jj0, e0)

    @pl.when(jnp.logical_and(used0, same))
    def _():
        compute(e0, 0, 2 * MOE_BLOCK)

    @pl.when(jnp.logical_and(used0, jnp.logical_not(same)))
    def _():
        compute(e0, 0, MOE_BLOCK)

    @pl.when(jnp.logical_and(used1, fresh1))
    def _():
        refresh(j1, jj1, e1)

    @pl.when(jnp.logical_and(used1, jnp.logical_not(same)))
    def _():
        compute(e1, MOE_BLOCK, MOE_BLOCK)

    for sub, used in ((0, used0), (1, used1)):
        @pl.when(jnp.logical_not(used))
        def _(sub=sub):
            ys_ref[pl.ds(sub * MOE_BLOCK, MOE_BLOCK)] = jnp.zeros((MOE_BLOCK, PACK_ROWS, LANES), U32)


def _stage_experts(blk8, xs, w1, b1, w2, b2):
    nrows = xs.shape[0]
    nblk = nrows // MOE_BLOCK
    assert EXPERT_BLOCKS_PER_STEP == 2 and nblk % EXPERT_BLOCKS_PER_STEP == 0
    block_e, nused, parity, nxt = blk8[0, :nblk], blk8[1, :1], blk8[2, :nblk], blk8[3, :nblk]
    tiles = (EXPERT_BLOCKS_PER_STEP * MOE_BLOCK, PACK_ROWS, LANES)
    full = lambda a: pl.BlockSpec(a.shape, lambda j, *_: (0,) * a.ndim)
    grid_spec = pltpu.PrefetchScalarGridSpec(
        num_scalar_prefetch=4,
        grid=(nblk // EXPERT_BLOCKS_PER_STEP,),
        in_specs=[pl.BlockSpec(tiles, lambda j, *_: (j, 0, 0)),
                  pl.BlockSpec(memory_space=pl.ANY), full(b1),
                  pl.BlockSpec(memory_space=pl.ANY), full(b2)],
        out_specs=pl.BlockSpec(tiles, lambda j, *_: (j, 0, 0)),
        scratch_shapes=[pltpu.VMEM((2, D_MODEL, 2 * D_FF), F32), pltpu.VMEM((2, D_FF, D_MODEL), F32),
                        pltpu.VMEM((D_MODEL, 2 * D_FF), BF16), pltpu.VMEM((D_FF, D_MODEL), BF16),
                        pltpu.SemaphoreType.DMA((2, 2))],
    )
    return pl.pallas_call(
        _expert_body,
        grid_spec=grid_spec,
        out_shape=jax.ShapeDtypeStruct((nrows, PACK_ROWS, LANES), U32),
        compiler_params=_params("arbitrary"),
        name="experts",
    )(block_e, nused, parity, nxt, xs, w1, b1, w2, b2)


COMBINE_CHUNK = 512


def _combine_body(tcnt_ref, tcar_ref, pstart_ref, loc_ref, gate_ref, x2_ref, ys_hbm, out_ref,
                  buf_ref, g_ref, sems, *, tm):
    step = pl.program_id(0)
    nloc = TOP_K * tm

    def start_runs(tile, slot):
        def run_copy(local, first, size):
            return pltpu.make_async_copy(_tile_rows(ys_hbm, first, size),
                                         _tile_rows(buf_ref.at[slot], local, size), sems.at[slot])
        _for_each_run(tile, tcnt_ref, tcar_ref, pstart_ref, lambda l, f, s: run_copy(l, f, s).start())

    def wait_buffer(slot):
        pltpu.make_async_copy(ys_hbm.at[pl.ds(0, nloc)], buf_ref.at[slot], sems.at[slot]).wait()

    def combine(slot):
        zpad = jnp.zeros((LANES - 16, LANES), F32)
        cols = []
        for c in range(tm // LANES):
            cols_in = slice(slot * tm + c * LANES, slot * tm + (c + 1) * LANES)
            cols.append(jnp.transpose(jnp.concatenate([loc_ref[:, cols_in].astype(F32), gate_ref[:, cols_in], zpad],
                                                      axis=0)))
        rows = slice(slot * tm, (slot + 1) * tm)
        acc = x2_ref[rows, :]
        for lc in range(nloc // COMBINE_CHUNK):
            lane = lax.broadcasted_iota(jnp.int16, (LANES, COMBINE_CHUNK), 1) + jnp.int16(lc * COMBINE_CHUNK)
            for c in range(tm // LANES):
                g = jnp.zeros((LANES, COMBINE_CHUNK), BF16)
                for k in range(TOP_K):
                    g = jnp.where(lane == cols[c][:, k:k + 1].astype(jnp.int16), cols[c][:, 8 + k:9 + k].astype(BF16), g)
                g_ref[c * LANES:(c + 1) * LANES, lc * COMBINE_CHUNK:(lc + 1) * COMBINE_CHUNK] = g
            if lc == 0:
                wait_buffer(slot)
            ys = _unpack_rows(_load_packed(_flat(buf_ref), slot * nloc + lc * COMBINE_CHUNK, COMBINE_CHUNK))
            acc = acc + jnp.dot(g_ref[:, lc * COMBINE_CHUNK:(lc + 1) * COMBINE_CHUNK], ys, preferred_element_type=F32)
        out_ref[rows, :] = acc

    @pl.when(step == 0)
    def _():
        start_runs(0, 0)

    start_runs(2 * step + 1, 1)
    combine(0)

    @pl.when(step + 1 < pl.num_programs(0))
    def _():
        start_runs(2 * step + 2, 0)

    combine(1)


def _stage_combine(tables, loc8, gate8, x2, ys, tm):
    n = x2.shape[0]
    assert (n // tm) % 2 == 0
    grid_spec = pltpu.PrefetchScalarGridSpec(
        num_scalar_prefetch=3,
        grid=(n // (2 * tm),),
        in_specs=[pl.BlockSpec((8, 2 * tm), lambda i, *_: (0, i)),
                  pl.BlockSpec((8, 2 * tm), lambda i, *_: (0, i)),
                  pl.BlockSpec((2 * tm, D_MODEL), lambda i, *_: (i, 0)),
                  pl.BlockSpec(memory_space=pl.ANY)],
        out_specs=pl.BlockSpec((2 * tm, D_MODEL), lambda i, *_: (i, 0)),
        scratch_shapes=[pltpu.VMEM((2, TOP_K * tm, PACK_ROWS, LANES), U32),
                        pltpu.VMEM((tm, TOP_K * tm), BF16),
                        pltpu.SemaphoreType.DMA((2,))],
    )
    return pl.pallas_call(
        functools.partial(_combine_body, tm=tm),
        grid_spec=grid_spec,
        out_shape=jax.ShapeDtypeStruct((n, D_MODEL), F32),
        compiler_params=_params("arbitrary"),
        name="combine",
    )(*tables, loc8, gate8, x2, ys)


def _moe(x2, xn, loc8, gate8, tcnt, tcar, cnt, w1, b1, w2, b2, tm):
    n = x2.shape[0]
    ntile = n // tm
    nblk = -(-(n * TOP_K) // MOE_BLOCK) + N_EXPERTS
    blk8, pstart, zlo, zhi = _stage_offsets(cnt, nblk)
    per_tile = lambda a: a.reshape(N_EXPERTS, ntile, LANES)[:, :, 0].T
    tables = (per_tile(tcnt), per_tile(tcar), pstart[:, 0])
    xs = _stage_dispatch(tables + (zlo[:, 0], zhi[:, 0]), loc8, xn, nblk * MOE_BLOCK, tm)
    ys = _stage_experts(blk8, xs, w1, b1, w2, b2)
    return _stage_combine(tables, loc8, gate8, x2, ys, tm)


def kernel(x, norm1_g, w_in, mlstm_gate_b, mlstm_norm_g, attn_q_norm_g, attn_k_norm_g, w_mlstm_branch,
           w_attn_branch, w_out, norm2_g, w_router, b_router, w1, b1, w2, b2):
    batch, seq, _ = x.shape
    n = batch * seq
    for l in range(norm1_g.shape[0]):
        x2d = x.reshape(n, D_MODEL)
        tm = min(512, seq)
        mq, kT, mv, so, gi, gf, aq, ak, av, sgm, sga = _stage_inproj(
            x2d, norm1_g[l], w_in[l], mlstm_gate_b[l], attn_q_norm_g[l], attn_k_norm_g[l], batch, seq, tm)
        h_m = _stage_mlstm(mq, kT, mv, so, gi, gf, mlstm_norm_g[l], batch, seq, tm)
        attn = [_stage_attn(aq[g], ak[g], av[g], batch, seq, g)
                for g in range(N_GROUPS)]
        x2, xn, loc8, gate8, tcnt, tcar, cnt = _stage_merge(
            h_m, attn, sgm, sga, x2d, w_mlstm_branch[l], w_attn_branch[l], w_out[l], norm2_g[l],
            w_router[l], b_router[l], batch, seq, tm)
        out = _moe(x2, xn, loc8, gate8, tcnt, tcar, cnt, w1[l], b1[l], w2[l], b2[l], tm)
        x = out.reshape(batch, seq, D_MODEL)
    return x
```
